```python
import jax, jax.numpy as jnp
from jax import lax
import numpy as np

D_MODEL = 2048
BATCH = 8
SEQ = 2048
DEPTH = 2

N_MIXERS = 2
N_FOX = (DEPTH + 1) // 2
N_SWA = DEPTH // 2
Q_BLOCK = 128

FOX_HEADS = 16
FOX_HEAD_DIM = D_MODEL // FOX_HEADS
FOX_WIDTH = FOX_HEADS * FOX_HEAD_DIM

SWA_HEAD_DIM = 64
SWA_Q_HEADS = D_MODEL // SWA_HEAD_DIM
SWA_KV_HEADS = SWA_Q_HEADS // 8
SWA_GROUP = SWA_Q_HEADS // SWA_KV_HEADS
SWA_WINDOW = 128
ROPE_THETA = 500000.0
ROPE_DIM = SWA_HEAD_DIM // 4

D_FF = 5632
CONV_WIDTH = 3

DEEPNORM_ALPHA = (2.0 * DEPTH) ** 0.25
DEEPNORM_BETA = (8.0 * DEPTH) ** -0.25
LN_EPS = 1e-5
ADA_SCALE = 0.2
MAX_POS_OFFSET = 4096

kernel_name = "hybrid_fox_swa_sink_convffn_deepnorm_adaln"


def layer_norm(x, g, b):
    xf = x.astype(jnp.float32)
    mu = jnp.mean(xf, axis=-1, keepdims=True)
    var = jnp.mean(jnp.square(xf - mu), axis=-1, keepdims=True)
    y = (xf - mu) * lax.rsqrt(var + LN_EPS)
    return (y * g.astype(jnp.float32) + b.astype(jnp.float32)).astype(x.dtype)


def rope_partial(t, pos):
    inv_freq = ROPE_THETA ** (-jnp.arange(0, ROPE_DIM, 2, dtype=jnp.float32) / ROPE_DIM)
    ang = pos.astype(jnp.float32)[..., None] * inv_freq
    cos = jnp.cos(ang)[:, :, None, :]
    sin = jnp.sin(ang)[:, :, None, :]
    tr = t[..., :ROPE_DIM].astype(jnp.float32)
    t1, t2 = tr[..., :ROPE_DIM // 2], tr[..., ROPE_DIM // 2:]
    rot = jnp.concatenate([t1 * cos - t2 * sin, t2 * cos + t1 * sin], axis=-1)
    return jnp.concatenate([rot.astype(t.dtype), t[..., ROPE_DIM:]], axis=-1)


def fox_attention(h, w_in, b_f, w_o):
    B, S, _ = h.shape
    H, dh = FOX_HEADS, FOX_HEAD_DIM
    proj = h @ w_in
    q = proj[..., :FOX_WIDTH].reshape(B, S, H, dh)
    k = proj[..., FOX_WIDTH:2 * FOX_WIDTH].reshape(B, S, H, dh)
    v = proj[..., 2 * FOX_WIDTH:3 * FOX_WIDTH].reshape(B, S, H, dh)
    f_logit = proj[..., 3 * FOX_WIDTH:] + b_f
    log_f = jax.nn.log_sigmoid(f_logit.astype(jnp.float32))
    cum = jnp.cumsum(log_f, axis=1).transpose(0, 2, 1)
    nb = S // Q_BLOCK
    q_blocks = q.reshape(B, nb, Q_BLOCK, H, dh).transpose(1, 0, 2, 3, 4)
    cq_blocks = cum.reshape(B, H, nb, Q_BLOCK).transpose(2, 0, 1, 3)
    key_pos = jnp.arange(S)
    scale = FOX_HEAD_DIM ** -0.5

    def one_block(args):
        qb, cqb, bi = args
        s = jnp.einsum('bqhd,bkhd->bhqk', qb, k).astype(jnp.float32) * scale
        s = s + cqb[..., None] - cum[:, :, None, :]
        q_pos = bi * Q_BLOCK + jnp.arange(Q_BLOCK)
        causal = key_pos[None, :] <= q_pos[:, None]
        s = jnp.where(causal, s, -jnp.inf)
        p = jax.nn.softmax(s, axis=-1).astype(v.dtype)
        return jnp.einsum('bhqk,bkhd->bqhd', p, v)

    o = lax.map(one_block, (q_blocks, cq_blocks, jnp.arange(nb)))
    o = o.transpose(1, 0, 2, 3, 4).reshape(B, S, FOX_WIDTH)
    return o @ w_o


def swa_attention(h, pos, w_in, sinks, w_o):
    B, S, _ = h.shape
    Hq, Hk, G, dh = SWA_Q_HEADS, SWA_KV_HEADS, SWA_GROUP, SWA_HEAD_DIM
    proj = h @ w_in
    q = proj[..., :Hq * dh].reshape(B, S, Hq, dh)
    k = proj[..., Hq * dh:(Hq + Hk) * dh].reshape(B, S, Hk, dh)
    v = proj[..., (Hq + Hk) * dh:].reshape(B, S, Hk, dh)
    q = rope_partial(q, pos)
    k = rope_partial(k, pos)
    nb = S // Q_BLOCK
    qb = q.reshape(B, nb, Q_BLOCK, Hk, G, dh)

    def band(t):
        tb = t.reshape(B, nb, Q_BLOCK, Hk, dh)
        prev = jnp.concatenate([jnp.zeros_like(tb[:, :1]), tb[:, :-1]], axis=1)
        return jnp.concatenate([prev, tb], axis=2)

    kb, vb = band(k), band(v)
    s = jnp.einsum('bnqhgd,bnkhd->bnhgqk', qb, kb).astype(jnp.float32) * (dh ** -0.5)
    qi = jnp.arange(Q_BLOCK)[:, None]
    kj = jnp.arange(2 * Q_BLOCK)[None, :]
    rel = qi + Q_BLOCK - kj
    key_abs = (jnp.arange(nb) * Q_BLOCK)[:, None] - Q_BLOCK + kj
    mask = (rel >= 0)[None] & (rel < SWA_WINDOW)[None] & (key_abs[:, None, :] >= 0)
    s = jnp.where(mask[None, :, None, None], s, -jnp.inf)
    sink = jnp.broadcast_to(sinks.astype(jnp.float32).reshape(1, 1, Hk, G, 1, 1), s.shape[:-1] + (1,))
    p = jax.nn.softmax(jnp.concatenate([s, sink], axis=-1), axis=-1)[..., :-1].astype(v.dtype)
    o = jnp.einsum('bnhgqk,bnkhd->bnqhgd', p, vb).reshape(B, S, Hq * dh)
    return o @ w_o


def conv_ffn(h, w_up, conv_w, conv_b, w_down):
    S = h.shape[1]
    u = h @ w_up
    up = jnp.pad(u, ((0, 0), (CONV_WIDTH - 1, 0), (0, 0)))
    u = sum(up[:, j:j + S] * conv_w[j] for j in range(CONV_WIDTH)) + conv_b
    g, val = u[..., :D_FF], u[..., D_FF:]
    return (jax.nn.silu(g) * val) @ w_down


def _fwd_setup_inputs(seed: int = 0) -> dict:
    key = jax.random.key(seed)
    ks = jax.random.split(key, 20)
    f32 = jnp.float32
    n = lambda k, shape, s: (jax.random.normal(k, shape, f32) * s)
    D = D_MODEL
    x = n(ks[0], (BATCH, SEQ, D), 1.0)
    c = n(ks[1], (BATCH, D), 1.0)
    offset = jax.random.randint(ks[2], (BATCH, 1), 0, MAX_POS_OFFSET, dtype=jnp.int32)
    positions = (offset + jnp.arange(SEQ, dtype=jnp.int32)[None, :]).astype(jnp.int32)
    fox_w_in = n(ks[3], (N_FOX, D, 3 * FOX_WIDTH + FOX_HEADS), D ** -0.5)
    fox_b_f = n(ks[4], (N_FOX, FOX_HEADS), 0.1)
    fox_w_o = n(ks[5], (N_FOX, FOX_WIDTH, D), FOX_WIDTH ** -0.5 * DEEPNORM_BETA)
    swa_w_in = n(ks[6], (N_SWA, D, (SWA_Q_HEADS + 2 * SWA_KV_HEADS) * SWA_HEAD_DIM), D ** -0.5)
    swa_sinks = n(ks[7], (N_SWA, SWA_Q_HEADS), 0.5)
    swa_w_o = n(ks[8], (N_SWA, SWA_Q_HEADS * SWA_HEAD_DIM, D), (SWA_Q_HEADS * SWA_HEAD_DIM) ** -0.5 * DEEPNORM_BETA)
    ada_w = n(ks[9], (DEPTH, D, 6 * D), ADA_SCALE * D ** -0.5)
    ada_b = n(ks[10], (DEPTH, 6 * D), 0.02)
    ffn_w_up = n(ks[11], (DEPTH, D, 2 * D_FF), D ** -0.5)
    ffn_conv_w = n(ks[12], (DEPTH, CONV_WIDTH, 2 * D_FF), CONV_WIDTH ** -0.5)
    ffn_conv_b = n(ks[13], (DEPTH, 2 * D_FF), 0.02)
    ffn_w_down = n(ks[14], (DEPTH, D_FF, D), D_FF ** -0.5 * DEEPNORM_BETA)
    ln_mix_g = 1.0 + n(ks[15], (DEPTH, D), 0.02)
    ln_mix_b = n(ks[16], (DEPTH, D), 0.02)
    ln_ffn_g = 1.0 + n(ks[17], (DEPTH, D), 0.02)
    ln_ffn_b = n(ks[18], (DEPTH, D), 0.02)
    return {"x": x, "c": c, "positions": positions,
            "fox_w_in": fox_w_in, "fox_b_f": fox_b_f, "fox_w_o": fox_w_o,
            "swa_w_in": swa_w_in, "swa_sinks": swa_sinks, "swa_w_o": swa_w_o,
            "ada_w": ada_w, "ada_b": ada_b,
            "ffn_w_up": ffn_w_up, "ffn_conv_w": ffn_conv_w, "ffn_conv_b": ffn_conv_b, "ffn_w_down": ffn_w_down,
            "ln_mix_g": ln_mix_g, "ln_mix_b": ln_mix_b, "ln_ffn_g": ln_ffn_g, "ln_ffn_b": ln_ffn_b}


def _fwd_reference(x, c, positions, fox_w_in, fox_b_f, fox_w_o, swa_w_in, swa_sinks, swa_w_o,
              ada_w, ada_b, ffn_w_up, ffn_conv_w, ffn_conv_b, ffn_w_down,
              ln_mix_g, ln_mix_b, ln_ffn_g, ln_ffn_b):
    c_act = jax.nn.silu(c)
    for i in range(DEPTH):
        mod = c_act @ ada_w[i] + ada_b[i]
        sh1, sc1, g1, sh2, sc2, g2 = jnp.split(mod[:, None, :], 6, axis=-1)
        h = x * (1.0 + sc1) + sh1
        j = i // N_MIXERS
        if i % N_MIXERS == 0:
            y = fox_attention(h, fox_w_in[j], fox_b_f[j], fox_w_o[j])
        else:
            y = swa_attention(h, positions, swa_w_in[j], swa_sinks[j], swa_w_o[j])
        x = layer_norm(DEEPNORM_ALPHA * x + (1.0 + g1) * y, ln_mix_g[i], ln_mix_b[i])
        h = x * (1.0 + sc2) + sh2
        y = conv_ffn(h, ffn_w_up[i], ffn_conv_w[i], ffn_conv_b[i], ffn_w_down[i])
        x = layer_norm(DEEPNORM_ALPHA * x + (1.0 + g2) * y, ln_ffn_g[i], ln_ffn_b[i])
    return x


import jax as _jax
import jax.numpy as _jnp

TWIN_FORMAT = 'train_step'
FWD_PARAMS = ['x', 'c', 'positions', 'fox_w_in', 'fox_b_f', 'fox_w_o', 'swa_w_in', 'swa_sinks', 'swa_w_o', 'ada_w', 'ada_b', 'ffn_w_up', 'ffn_conv_w', 'ffn_conv_b', 'ffn_w_down', 'ln_mix_g', 'ln_mix_b', 'ln_ffn_g', 'ln_ffn_b']
TWIN_WEIGHTS = ['fox_w_in', 'fox_b_f', 'fox_w_o', 'swa_w_in', 'swa_sinks', 'swa_w_o', 'ada_w', 'ada_b', 'ffn_w_up', 'ffn_conv_w', 'ffn_conv_b', 'ffn_w_down', 'ln_mix_g', 'ln_mix_b', 'ln_ffn_g', 'ln_ffn_b']
TWIN_DIFF_INPUT = 'x'
TWIN_INPUTS = ['x', 'c', 'positions', 'fox_w_in', 'fox_b_f', 'fox_w_o', 'swa_w_in', 'swa_sinks', 'swa_w_o', 'ada_w', 'ada_b', 'ffn_w_up', 'ffn_conv_w', 'ffn_conv_b', 'ffn_w_down', 'ln_mix_g', 'ln_mix_b', 'ln_ffn_g', 'ln_ffn_b', 'loss_target', 'm_fox_w_in', 'm_fox_b_f', 'm_fox_w_o', 'm_swa_w_in', 'm_swa_sinks', 'm_swa_w_o', 'm_ada_w', 'm_ada_b', 'm_ffn_w_up', 'm_ffn_conv_w', 'm_ffn_conv_b', 'm_ffn_w_down', 'm_ln_mix_g', 'm_ln_mix_b', 'm_ln_ffn_g', 'm_ln_ffn_b', 'v_fox_w_in', 'v_fox_b_f', 'v_fox_w_o', 'v_swa_w_in', 'v_swa_sinks', 'v_swa_w_o', 'v_ada_w', 'v_ada_b', 'v_ffn_w_up', 'v_ffn_conv_w', 'v_ffn_conv_b', 'v_ffn_w_down', 'v_ln_mix_g', 'v_ln_mix_b', 'v_ln_ffn_g', 'v_ln_ffn_b']
TWIN_OUTPUTS = ['loss', 'grad_x', 'grad_fox_w_in', 'grad_fox_b_f', 'grad_fox_w_o', 'grad_swa_w_in', 'grad_swa_sinks', 'grad_swa_w_o', 'grad_ada_w', 'grad_ada_b', 'grad_ffn_w_up', 'grad_ffn_conv_w', 'grad_ffn_conv_b', 'grad_ffn_w_down', 'grad_ln_mix_g', 'grad_ln_mix_b', 'grad_ln_ffn_g', 'grad_ln_ffn_b', 'delta_fox_w_in', 'delta_fox_b_f', 'delta_fox_w_o', 'delta_swa_w_in', 'delta_swa_sinks', 'delta_swa_w_o', 'delta_ada_w', 'delta_ada_b', 'delta_ffn_w_up', 'delta_ffn_conv_w', 'delta_ffn_conv_b', 'delta_ffn_w_down', 'delta_ln_mix_g', 'delta_ln_mix_b', 'delta_ln_ffn_g', 'delta_ln_ffn_b', 'new_m_fox_w_in', 'new_m_fox_b_f', 'new_m_fox_w_o', 'new_m_swa_w_in', 'new_m_swa_sinks', 'new_m_swa_w_o', 'new_m_ada_w', 'new_m_ada_b', 'new_m_ffn_w_up', 'new_m_ffn_conv_w', 'new_m_ffn_conv_b', 'new_m_ffn_w_down', 'new_m_ln_mix_g', 'new_m_ln_mix_b', 'new_m_ln_ffn_g', 'new_m_ln_ffn_b', 'new_v_fox_w_in', 'new_v_fox_b_f', 'new_v_fox_w_o', 'new_v_swa_w_in', 'new_v_swa_sinks', 'new_v_swa_w_o', 'new_v_ada_w', 'new_v_ada_b', 'new_v_ffn_w_up', 'new_v_ffn_conv_w', 'new_v_ffn_conv_b', 'new_v_ffn_w_down', 'new_v_ln_mix_g', 'new_v_ln_mix_b', 'new_v_ln_ffn_g', 'new_v_ln_ffn_b']
TWIN_LEAF_KINDS = {'loss': 'loss', 'grad_x': 'grad_x', 'grad_fox_w_in': 'grad_w', 'grad_fox_b_f': 'grad_w', 'grad_fox_w_o': 'grad_w', 'grad_swa_w_in': 'grad_w', 'grad_swa_sinks': 'grad_w', 'grad_swa_w_o': 'grad_w', 'grad_ada_w': 'grad_w', 'grad_ada_b': 'grad_w', 'grad_ffn_w_up': 'grad_w', 'grad_ffn_conv_w': 'grad_w', 'grad_ffn_conv_b': 'grad_w', 'grad_ffn_w_down': 'grad_w', 'grad_ln_mix_g': 'grad_w', 'grad_ln_mix_b': 'grad_w', 'grad_ln_ffn_g': 'grad_w', 'grad_ln_ffn_b': 'grad_w', 'delta_fox_w_in': 'delta_w', 'delta_fox_b_f': 'delta_w', 'delta_fox_w_o': 'delta_w', 'delta_swa_w_in': 'delta_w', 'delta_swa_sinks': 'delta_w', 'delta_swa_w_o': 'delta_w', 'delta_ada_w': 'delta_w', 'delta_ada_b': 'delta_w', 'delta_ffn_w_up': 'delta_w', 'delta_ffn_conv_w': 'delta_w', 'delta_ffn_conv_b': 'delta_w', 'delta_ffn_w_down': 'delta_w', 'delta_ln_mix_g': 'delta_w', 'delta_ln_mix_b': 'delta_w', 'delta_ln_ffn_g': 'delta_w', 'delta_ln_ffn_b': 'delta_w', 'new_m_fox_w_in': 'new_m', 'new_m_fox_b_f': 'new_m', 'new_m_fox_w_o': 'new_m', 'new_m_swa_w_in': 'new_m', 'new_m_swa_sinks': 'new_m', 'new_m_swa_w_o': 'new_m', 'new_m_ada_w': 'new_m', 'new_m_ada_b': 'new_m', 'new_m_ffn_w_up': 'new_m', 'new_m_ffn_conv_w': 'new_m', 'new_m_ffn_conv_b': 'new_m', 'new_m_ffn_w_down': 'new_m', 'new_m_ln_mix_g': 'new_m', 'new_m_ln_mix_b': 'new_m', 'new_m_ln_ffn_g': 'new_m', 'new_m_ln_ffn_b': 'new_m', 'new_v_fox_w_in': 'new_v', 'new_v_fox_b_f': 'new_v', 'new_v_fox_w_o': 'new_v', 'new_v_swa_w_in': 'new_v', 'new_v_swa_sinks': 'new_v', 'new_v_swa_w_o': 'new_v', 'new_v_ada_w': 'new_v', 'new_v_ada_b': 'new_v', 'new_v_ffn_w_up': 'new_v', 'new_v_ffn_conv_w': 'new_v', 'new_v_ffn_conv_b': 'new_v', 'new_v_ffn_w_down': 'new_v', 'new_v_ln_mix_g': 'new_v', 'new_v_ln_mix_b': 'new_v', 'new_v_ln_ffn_g': 'new_v', 'new_v_ln_ffn_b': 'new_v'}


def _forward(args):
    return _fwd_reference(*[args[k] for k in FWD_PARAMS])


def _output_shape():
    out = _jax.eval_shape(lambda: _forward(_fwd_setup_inputs(0)))
    return out.shape, out.dtype

N_MICROBATCH = 1
ADAM_LR = 0.001
ADAM_B1 = 0.9
ADAM_B2 = 0.999
ADAM_EPS = 1e-08
ADAM_WD = 0.01
ADAM_STEP = 10
PER_EXAMPLE_BATCH_AXIS = {'x': 0, 'c': 0, 'positions': 0, 'loss_target': 0}
SHARED_INPUTS = []
_WEIGHT_DTYPES = {'fox_w_in': _jnp.float32, 'fox_b_f': _jnp.float32, 'fox_w_o': _jnp.float32, 'swa_w_in': _jnp.float32, 'swa_sinks': _jnp.float32, 'swa_w_o': _jnp.float32, 'ada_w': _jnp.float32, 'ada_b': _jnp.float32, 'ffn_w_up': _jnp.float32, 'ffn_conv_w': _jnp.float32, 'ffn_conv_b': _jnp.float32, 'ffn_w_down': _jnp.float32, 'ln_mix_g': _jnp.float32, 'ln_mix_b': _jnp.float32, 'ln_ffn_g': _jnp.float32, 'ln_ffn_b': _jnp.float32}
MOMENT_SCALE = {'fox_w_in': 1.175289e-02, 'fox_b_f': 6.051402e-02, 'fox_w_o': 3.190791e-02, 'swa_w_in': 7.399979e-03, 'swa_sinks': 3.958003e-03, 'swa_w_o': 1.170356e-02, 'ada_w': 1.263815e-02, 'ada_b': 2.642943e-02, 'ffn_w_up': 8.592772e-03, 'ffn_conv_w': 8.691374e-03, 'ffn_conv_b': 9.531665e-03, 'ffn_w_down': 2.807261e-02, 'ln_mix_g': 2.686606e-01, 'ln_mix_b': 1.461099e-01, 'ln_ffn_g': 5.665455e+00, 'ln_ffn_b': 2.954859e-01}


def _to_microbatches(a, axis):
    t = _jnp.moveaxis(a, axis, 0)
    t = t.reshape((N_MICROBATCH, t.shape[0] // N_MICROBATCH) + t.shape[1:])
    return _jnp.moveaxis(t, 1, axis + 1)


def setup_inputs(seed: int = 0) -> dict:
    inp = _fwd_setup_inputs(seed)
    key = _jax.random.fold_in(_jax.random.key(seed), 7919)
    shape, _ = _output_shape()
    out = dict(inp)
    out["loss_target"] = _jax.random.normal(_jax.random.fold_in(key, 0), shape, _jnp.float32)
    for i, name in enumerate(TWIN_WEIGHTS):
        w = inp[name].astype(_jnp.float32)
        if MOMENT_SCALE is None:
            s = _jnp.sqrt(_jnp.mean(_jnp.square(w)) + 1e-30)
        else:
            s = MOMENT_SCALE[name]
        km, kv = _jax.random.split(_jax.random.fold_in(key, i + 1))
        out[name] = w
        out["m_" + name] = s * _jax.random.normal(km, w.shape, _jnp.float32)
        out["v_" + name] = (s * s) * _jax.random.uniform(kv, w.shape, _jnp.float32, 0.5, 1.5)
    if N_MICROBATCH > 1:
        for name, axis in PER_EXAMPLE_BATCH_AXIS.items():
            out[name] = _to_microbatches(out[name], axis)
    return {'x': out['x'], 'c': out['c'], 'positions': out['positions'], 'fox_w_in': out['fox_w_in'], 'fox_b_f': out['fox_b_f'], 'fox_w_o': out['fox_w_o'], 'swa_w_in': out['swa_w_in'], 'swa_sinks': out['swa_sinks'], 'swa_w_o': out['swa_w_o'], 'ada_w': out['ada_w'], 'ada_b': out['ada_b'], 'ffn_w_up': out['ffn_w_up'], 'ffn_conv_w': out['ffn_conv_w'], 'ffn_conv_b': out['ffn_conv_b'], 'ffn_w_down': out['ffn_w_down'], 'ln_mix_g': out['ln_mix_g'], 'ln_mix_b': out['ln_mix_b'], 'ln_ffn_g': out['ln_ffn_g'], 'ln_ffn_b': out['ln_ffn_b'], 'loss_target': out['loss_target'], 'm_fox_w_in': out['m_fox_w_in'], 'm_fox_b_f': out['m_fox_b_f'], 'm_fox_w_o': out['m_fox_w_o'], 'm_swa_w_in': out['m_swa_w_in'], 'm_swa_sinks': out['m_swa_sinks'], 'm_swa_w_o': out['m_swa_w_o'], 'm_ada_w': out['m_ada_w'], 'm_ada_b': out['m_ada_b'], 'm_ffn_w_up': out['m_ffn_w_up'], 'm_ffn_conv_w': out['m_ffn_conv_w'], 'm_ffn_conv_b': out['m_ffn_conv_b'], 'm_ffn_w_down': out['m_ffn_w_down'], 'm_ln_mix_g': out['m_ln_mix_g'], 'm_ln_mix_b': out['m_ln_mix_b'], 'm_ln_ffn_g': out['m_ln_ffn_g'], 'm_ln_ffn_b': out['m_ln_ffn_b'], 'v_fox_w_in': out['v_fox_w_in'], 'v_fox_b_f': out['v_fox_b_f'], 'v_fox_w_o': out['v_fox_w_o'], 'v_swa_w_in': out['v_swa_w_in'], 'v_swa_sinks': out['v_swa_sinks'], 'v_swa_w_o': out['v_swa_w_o'], 'v_ada_w': out['v_ada_w'], 'v_ada_b': out['v_ada_b'], 'v_ffn_w_up': out['v_ffn_w_up'], 'v_ffn_conv_w': out['v_ffn_conv_w'], 'v_ffn_conv_b': out['v_ffn_conv_b'], 'v_ffn_w_down': out['v_ffn_w_down'], 'v_ln_mix_g': out['v_ln_mix_g'], 'v_ln_mix_b': out['v_ln_mix_b'], 'v_ln_ffn_g': out['v_ln_ffn_g'], 'v_ln_ffn_b': out['v_ln_ffn_b']}


def _loss(weights, diff, rest, loss_target):
    with _jax.named_scope("forward"):
        args = {**rest, TWIN_DIFF_INPUT: diff, **{k: w.astype(_WEIGHT_DTYPES[k]) for k, w in weights.items()}}
        y = _forward(args)
    with _jax.named_scope("loss_head"):
        err = _jnp.square(y.astype(_jnp.float32) - loss_target)
        return 0.5 * _jnp.sum(_jnp.mean(err, axis=-1)) if err.ndim else 0.5 * err


def _adamw(w, g, m, v):
    m = ADAM_B1 * m + (1.0 - ADAM_B1) * g
    v = ADAM_B2 * v + (1.0 - ADAM_B2) * _jnp.square(g)
    m_hat = m / (1.0 - ADAM_B1 ** ADAM_STEP)
    v_hat = v / (1.0 - ADAM_B2 ** ADAM_STEP)
    delta = -ADAM_LR * (m_hat / (_jnp.sqrt(v_hat) + ADAM_EPS) + ADAM_WD * w)
    return delta, m, v


def reference(x, c, positions, fox_w_in, fox_b_f, fox_w_o, swa_w_in, swa_sinks, swa_w_o, ada_w, ada_b, ffn_w_up, ffn_conv_w, ffn_conv_b, ffn_w_down, ln_mix_g, ln_mix_b, ln_ffn_g, ln_ffn_b, loss_target, m_fox_w_in, m_fox_b_f, m_fox_w_o, m_swa_w_in, m_swa_sinks, m_swa_w_o, m_ada_w, m_ada_b, m_ffn_w_up, m_ffn_conv_w, m_ffn_conv_b, m_ffn_w_down, m_ln_mix_g, m_ln_mix_b, m_ln_ffn_g, m_ln_ffn_b, v_fox_w_in, v_fox_b_f, v_fox_w_o, v_swa_w_in, v_swa_sinks, v_swa_w_o, v_ada_w, v_ada_b, v_ffn_w_up, v_ffn_conv_w, v_ffn_conv_b, v_ffn_w_down, v_ln_mix_g, v_ln_mix_b, v_ln_ffn_g, v_ln_ffn_b):
    given = dict(x=x, c=c, positions=positions, fox_w_in=fox_w_in, fox_b_f=fox_b_f, fox_w_o=fox_w_o, swa_w_in=swa_w_in, swa_sinks=swa_sinks, swa_w_o=swa_w_o, ada_w=ada_w, ada_b=ada_b, ffn_w_up=ffn_w_up, ffn_conv_w=ffn_conv_w, ffn_conv_b=ffn_conv_b, ffn_w_down=ffn_w_down, ln_mix_g=ln_mix_g, ln_mix_b=ln_mix_b, ln_ffn_g=ln_ffn_g, ln_ffn_b=ln_ffn_b, loss_target=loss_target, m_fox_w_in=m_fox_w_in, m_fox_b_f=m_fox_b_f, m_fox_w_o=m_fox_w_o, m_swa_w_in=m_swa_w_in, m_swa_sinks=m_swa_sinks, m_swa_w_o=m_swa_w_o, m_ada_w=m_ada_w, m_ada_b=m_ada_b, m_ffn_w_up=m_ffn_w_up, m_ffn_conv_w=m_ffn_conv_w, m_ffn_conv_b=m_ffn_conv_b, m_ffn_w_down=m_ffn_w_down, m_ln_mix_g=m_ln_mix_g, m_ln_mix_b=m_ln_mix_b, m_ln_ffn_g=m_ln_ffn_g, m_ln_ffn_b=m_ln_ffn_b, v_fox_w_in=v_fox_w_in, v_fox_b_f=v_fox_b_f, v_fox_w_o=v_fox_w_o, v_swa_w_in=v_swa_w_in, v_swa_sinks=v_swa_sinks, v_swa_w_o=v_swa_w_o, v_ada_w=v_ada_w, v_ada_b=v_ada_b, v_ffn_w_up=v_ffn_w_up, v_ffn_conv_w=v_ffn_conv_w, v_ffn_conv_b=v_ffn_conv_b, v_ffn_w_down=v_ffn_w_down, v_ln_mix_g=v_ln_mix_g, v_ln_mix_b=v_ln_mix_b, v_ln_ffn_g=v_ln_ffn_g, v_ln_ffn_b=v_ln_ffn_b)
    weights = {n: given[n] for n in TWIN_WEIGHTS}
    shared = {n: given[n] for n in SHARED_INPUTS}
    per_example = {n: given[n] for n in ['x', 'c', 'positions']}
    grad_fn = _jax.value_and_grad(_loss, argnums=(0, 1))

    def one_microbatch(ex, loss_target):
        ex = dict(ex)
        diff = ex.pop(TWIN_DIFF_INPUT)
        return grad_fn(weights, diff, {**shared, **ex}, loss_target)

    if N_MICROBATCH == 1:
        loss, (grad_w, grad_x) = one_microbatch(per_example, given["loss_target"])
    else:
        def body(carry, xs):
            loss_sum, grad_sum = carry
            l_k, (gw_k, gx_k) = one_microbatch(xs[0], xs[1])
            with _jax.named_scope("update"):
                return (loss_sum + l_k, _jax.tree.map(_jnp.add, grad_sum, gw_k)), gx_k

        init = (_jnp.zeros((), _jnp.float32), _jax.tree.map(_jnp.zeros_like, weights))
        (loss, grad_w), grad_x = _jax.lax.scan(body, init, (per_example, given["loss_target"]))
    with _jax.named_scope("update"):
        delta_w, new_m, new_v = {}, {}, {}
        for n in TWIN_WEIGHTS:
            delta_w[n], new_m[n], new_v[n] = _adamw(weights[n], grad_w[n], given["m_" + n], given["v_" + n])
    return (loss, grad_x, *[grad_w[n] for n in TWIN_WEIGHTS], *[delta_w[n] for n in TWIN_WEIGHTS],
            *[new_m[n] for n in TWIN_WEIGHTS], *[new_v[n] for n in TWIN_WEIGHTS])
```

```python
import functools

import jax
import jax.numpy as jnp
from jax import lax
from jax.experimental import pallas as pl
from jax.experimental.pallas import tpu as pltpu

F32 = jnp.float32
BF16 = jnp.bfloat16
MESH = pl.DeviceIdType.MESH
N_DEV = 8
AXES = ("x", "y", "c")

DEPTH = 2
ALPHA = (2.0 * DEPTH) ** 0.25
LN_EPS = 1e-5
FOX_HEAD_DIM = 128
SWA_HEAD_DIM = 64
SWA_GROUP = 8
SWA_WINDOW = 128
Q_BLOCK = 128
ROPE_DIM = 16
ROPE_THETA = 500000.0

ADAM_LR = 0.001
ADAM_B1 = 0.9
ADAM_B2 = 0.999
ADAM_EPS = 1e-08
ADAM_WD = 0.01
ADAM_STEP = 10

LANE = 128
MIB = 1024 * 1024


def _tile(n, pref, unit=LANE):
    if n <= pref:
        return n
    t = (pref // unit) * unit
    while t >= unit:
        if n % t == 0:
            return t
        t -= unit
    return n


def _params(sem, vmem_mib=48):
    return pltpu.CompilerParams(dimension_semantics=sem, vmem_limit_bytes=vmem_mib * MIB)


def _sigmoid(x):
    return 1.0 / (1.0 + jnp.exp(-x))


def _mm_nn(a, b, out_dtype, name, tm=512, tn=512, tk=512):
    M, K = a.shape
    N = b.shape[1]
    tm, tn, tk = _tile(M, tm), _tile(N, tn), _tile(K, tk)
    nk = K // tk

    def body(a_ref, b_ref, o_ref, acc_ref):
        k = pl.program_id(2)

        @pl.when(k == 0)
        def _():
            acc_ref[...] = jnp.zeros_like(acc_ref)

        acc_ref[...] += jnp.dot(a_ref[...].astype(BF16), b_ref[...].astype(BF16), preferred_element_type=F32)

        @pl.when(k == nk - 1)
        def _():
            o_ref[...] = acc_ref[...].astype(o_ref.dtype)

    return pl.pallas_call(
        body, name=name, grid=(M // tm, N // tn, nk),
        in_specs=[pl.BlockSpec((tm, tk), lambda i, j, k: (i, k)), pl.BlockSpec((tk, tn), lambda i, j, k: (k, j))],
        out_specs=pl.BlockSpec((tm, tn), lambda i, j, k: (i, j)),
        out_shape=jax.ShapeDtypeStruct((M, N), out_dtype),
        scratch_shapes=[pltpu.VMEM((tm, tn), F32)],
        compiler_params=_params(("parallel", "parallel", "arbitrary")),
    )(a, b)


def _mm_nt(a, b, out_dtype, name, tm=512, tn=512, tk=512):
    P, M, K = a.shape
    N = b.shape[1]
    tm, tn, tk = _tile(M, tm), _tile(N, tn), _tile(K, tk)
    nk = K // tk
    nkk = P * nk

    def body(a_ref, b_ref, o_ref, acc_ref):
        k = pl.program_id(2)

        @pl.when(k == 0)
        def _():
            acc_ref[...] = jnp.zeros_like(acc_ref)

        acc_ref[...] += lax.dot_general(a_ref[...].astype(BF16), b_ref[...].astype(BF16),
                                        (((1,), (1,)), ((), ())), preferred_element_type=F32)

        @pl.when(k == nkk - 1)
        def _():
            o_ref[...] = acc_ref[...].astype(o_ref.dtype)

    return pl.pallas_call(
        body, name=name, grid=(M // tm, N // tn, nkk),
        in_specs=[pl.BlockSpec((None, tm, tk), lambda i, j, k: (k // nk, i, k % nk)),
                  pl.BlockSpec((None, tn, tk), lambda i, j, k: (k // nk, j, k % nk))],
        out_specs=pl.BlockSpec((tm, tn), lambda i, j, k: (i, j)),
        out_shape=jax.ShapeDtypeStruct((M, N), out_dtype),
        scratch_shapes=[pltpu.VMEM((tm, tn), F32)],
        compiler_params=_params(("parallel", "parallel", "arbitrary")),
    )(a, b)


def _mm_tn(a, b, out_dtype, name, tm=512, tn=512, tk=512):
    K, M = a.shape
    P, _, N = b.shape
    tm, tn, tk = _tile(M, tm), _tile(N, tn), _tile(K, tk)
    nk = K // tk

    def body(a_ref, b_ref, o_ref, acc_ref):
        k = pl.program_id(3)

        @pl.when(k == 0)
        def _():
            acc_ref[...] = jnp.zeros_like(acc_ref)

        acc_ref[...] += lax.dot_general(a_ref[...].astype(BF16), b_ref[...].astype(BF16),
                                        (((0,), (0,)), ((), ())), preferred_element_type=F32)

        @pl.when(k == nk - 1)
        def _():
            o_ref[...] = acc_ref[...].astype(o_ref.dtype)

    return pl.pallas_call(
        body, name=name, grid=(P, M // tm, N // tn, nk),
        in_specs=[pl.BlockSpec((tk, tm), lambda p, i, j, k: (k, i)),
                  pl.BlockSpec((None, tk, tn), lambda p, i, j, k: (p, k, j))],
        out_specs=pl.BlockSpec((None, tm, tn), lambda p, i, j, k: (p, i, j)),
        out_shape=jax.ShapeDtypeStruct((P, M, N), out_dtype),
        scratch_shapes=[pltpu.VMEM((tm, tn), F32)],
        compiler_params=_params(("parallel", "parallel", "parallel", "arbitrary")),
    )(a, b)


ROW_TILE = 256


def _row_spec(tm, D):
    return pl.BlockSpec((tm, D), lambda i: (i, 0))


def _vec_spec(D):
    return pl.BlockSpec((1, D), lambda i: (0, 0))


def _modulate(x, sc, sh, name):
    S, D = x.shape
    tm = _tile(S, ROW_TILE, 8)

    def body(x_ref, sc_ref, sh_ref, h_ref):
        h_ref[...] = (x_ref[...] * (1.0 + sc_ref[...]) + sh_ref[...]).astype(BF16)

    return pl.pallas_call(
        body, name=name, grid=(S // tm,),
        in_specs=[_row_spec(tm, D), _vec_spec(D), _vec_spec(D)],
        out_specs=_row_spec(tm, D),
        out_shape=jax.ShapeDtypeStruct((S, D), BF16),
        compiler_params=_params(("parallel",)),
    )(x, sc, sh)


def _layer_norm_rows(z, gamma, beta):
    mu = jnp.mean(z, axis=-1, keepdims=True)
    zc = z - mu
    var = jnp.mean(zc * zc, axis=-1, keepdims=True)
    return zc * lax.rsqrt(var + LN_EPS) * gamma + beta


def _ln_fwd(x, y, gate, gamma, beta, sc_n, sh_n, name):
    S, D = x.shape
    tm = _tile(S, ROW_TILE, 8)

    def body(x_ref, y_ref, gate_ref, g_ref, b_ref, sc_ref, sh_ref, z_ref, xo_ref, hn_ref):
        z = ALPHA * x_ref[...] + (1.0 + gate_ref[...]) * y_ref[...]
        xo = _layer_norm_rows(z, g_ref[...], b_ref[...])
        z_ref[...] = z
        xo_ref[...] = xo
        hn_ref[...] = (xo * (1.0 + sc_ref[...]) + sh_ref[...]).astype(BF16)

    return pl.pallas_call(
        body, name=name, grid=(S // tm,),
        in_specs=[_row_spec(tm, D), _row_spec(tm, D)] + [_vec_spec(D)] * 5,
        out_specs=[_row_spec(tm, D)] * 3,
        out_shape=[jax.ShapeDtypeStruct((S, D), F32), jax.ShapeDtypeStruct((S, D), F32),
                   jax.ShapeDtypeStruct((S, D), BF16)],
        compiler_params=_params(("parallel",)),
    )(x, y, gate, gamma, beta, sc_n, sh_n)


def _ln_fwd_loss(x, y, gate, gamma, beta, target, name):
    S, D = x.shape
    tm = _tile(S, ROW_TILE, 8)

    def body(x_ref, y_ref, gate_ref, g_ref, b_ref, t_ref, z_ref, dout_ref, loss_ref):
        @pl.when(pl.program_id(0) == 0)
        def _():
            loss_ref[...] = jnp.zeros_like(loss_ref)

        z = ALPHA * x_ref[...] + (1.0 + gate_ref[...]) * y_ref[...]
        xo = _layer_norm_rows(z, g_ref[...], b_ref[...])
        err = xo - t_ref[...]
        z_ref[...] = z
        dout_ref[...] = err * (1.0 / D)
        loss_ref[...] += (0.5 / D) * jnp.sum(err * err)

    return pl.pallas_call(
        body, name=name, grid=(S // tm,),
        in_specs=[_row_spec(tm, D), _row_spec(tm, D)] + [_vec_spec(D)] * 3 + [_row_spec(tm, D)],
        out_specs=[_row_spec(tm, D), _row_spec(tm, D), pl.BlockSpec((1, LANE), lambda i: (0, 0))],
        out_shape=[jax.ShapeDtypeStruct((S, D), F32), jax.ShapeDtypeStruct((S, D), F32),
                   jax.ShapeDtypeStruct((1, LANE), F32)],
        compiler_params=_params(("arbitrary",)),
    )(x, y, gate, gamma, beta, target)


def _ln_bwd(dout, z, y, gate, gamma, name):
    S, D = z.shape
    tm = _tile(S, ROW_TILE, 8)

    def body(dout_ref, z_ref, y_ref, gate_ref, g_ref, dz_ref, dy_ref, dg_ref, db_ref, dgate_ref):
        @pl.when(pl.program_id(0) == 0)
        def _():
            dg_ref[...] = jnp.zeros_like(dg_ref)
            db_ref[...] = jnp.zeros_like(db_ref)
            dgate_ref[...] = jnp.zeros_like(dgate_ref)

        z = z_ref[...]
        dout = dout_ref[...]
        mu = jnp.mean(z, axis=-1, keepdims=True)
        zc = z - mu
        var = jnp.mean(zc * zc, axis=-1, keepdims=True)
        rstd = lax.rsqrt(var + LN_EPS)
        xhat = zc * rstd
        dxhat = dout * g_ref[...]
        m1 = jnp.mean(dxhat, axis=-1, keepdims=True)
        m2 = jnp.mean(dxhat * xhat, axis=-1, keepdims=True)
        dz = rstd * (dxhat - m1 - xhat * m2)
        dz_ref[...] = dz
        dy_ref[...] = (dz * (1.0 + gate_ref[...])).astype(BF16)
        dg_ref[...] += jnp.sum(dout * xhat, axis=0, keepdims=True)
        db_ref[...] += jnp.sum(dout, axis=0, keepdims=True)
        dgate_ref[...] += jnp.sum(dz * y_ref[...], axis=0, keepdims=True)

    return pl.pallas_call(
        body, name=name, grid=(S // tm,),
        in_specs=[_row_spec(tm, D)] * 3 + [_vec_spec(D)] * 2,
        out_specs=[_row_spec(tm, D), _row_spec(tm, D)] + [_vec_spec(D)] * 3,
        out_shape=[jax.ShapeDtypeStruct((S, D), F32), jax.ShapeDtypeStruct((S, D), BF16)]
        + [jax.ShapeDtypeStruct((1, D), F32)] * 3,
        compiler_params=_params(("arbitrary",)),
    )(dout, z, y, gate, gamma)


def _mod_bwd(dz, dh, xin, sc, name):
    S, D = dz.shape
    tm = _tile(S, ROW_TILE, 8)

    def body(dz_ref, dh_ref, x_ref, sc_ref, dx_ref, dsc_ref, dsh_ref):
        @pl.when(pl.program_id(0) == 0)
        def _():
            dsc_ref[...] = jnp.zeros_like(dsc_ref)
            dsh_ref[...] = jnp.zeros_like(dsh_ref)

        dh = dh_ref[...]
        dx_ref[...] = ALPHA * dz_ref[...] + dh * (1.0 + sc_ref[...])
        dsc_ref[...] += jnp.sum(dh * x_ref[...], axis=0, keepdims=True)
        dsh_ref[...] += jnp.sum(dh, axis=0, keepdims=True)

    return pl.pallas_call(
        body, name=name, grid=(S // tm,),
        in_specs=[_row_spec(tm, D)] * 3 + [_vec_spec(D)],
        out_specs=[_row_spec(tm, D), _vec_spec(D), _vec_spec(D)],
        out_shape=[jax.ShapeDtypeStruct((S, D), F32)] + [jax.ShapeDtypeStruct((1, D), F32)] * 2,
        compiler_params=_params(("arbitrary",)),
    )(dz, dh, xin, sc)


def _shift_down(u, k, row):
    return jnp.where(row >= k, pltpu.roll(u, k, axis=0), 0.0)


def _shift_up(u, k, row, S):
    return jnp.where(row < S - k, pltpu.roll(u, S - k, axis=0), 0.0)


def _ffn_up(h, w, cw, cb, name):
    S, D = h.shape
    F = w.shape[2]
    tn = _tile(F, 256)

    def body(h_ref, w_ref, cw_ref, cb_ref, u_ref, a_ref):
        hh = h_ref[...]
        row = lax.broadcasted_iota(jnp.int32, (S, tn), 0)
        conv = []
        for p in range(2):
            u = jnp.dot(hh, w_ref[p], preferred_element_type=F32)
            u_ref[p] = u
            cwp = cw_ref[p]
            conv.append(_shift_down(u, 2, row) * cwp[0:1] + _shift_down(u, 1, row) * cwp[1:2]
                        + u * cwp[2:3] + cb_ref[p])
        g, v = conv
        a_ref[...] = (g * _sigmoid(g) * v).astype(BF16)

    return pl.pallas_call(
        body, name=name, grid=(F // tn,),
        in_specs=[pl.BlockSpec((S, D), lambda j: (0, 0)), pl.BlockSpec((2, D, tn), lambda j: (0, 0, j)),
                  pl.BlockSpec((2, 3, tn), lambda j: (0, 0, j)), pl.BlockSpec((2, 1, tn), lambda j: (0, 0, j))],
        out_specs=[pl.BlockSpec((2, S, tn), lambda j: (0, 0, j)), pl.BlockSpec((S, tn), lambda j: (0, j))],
        out_shape=[jax.ShapeDtypeStruct((2, S, F), F32), jax.ShapeDtypeStruct((S, F), BF16)],
        compiler_params=_params(("parallel",), 56),
    )(h, w, cw, cb)


def _ffn_bwd_elem(da, u, cw, cb, name):
    _, S, F = u.shape
    tn = _tile(F, 256)

    def body(da_ref, u_ref, cw_ref, cb_ref, du_ref, dcw_ref, dcb_ref):
        row = lax.broadcasted_iota(jnp.int32, (S, tn), 0)
        da = da_ref[...]
        shifted, conv = [], []
        for p in range(2):
            u = u_ref[p]
            u1, u2 = _shift_down(u, 1, row), _shift_down(u, 2, row)
            cwp = cw_ref[p]
            shifted.append((u2, u1, u))
            conv.append(u2 * cwp[0:1] + u1 * cwp[1:2] + u * cwp[2:3] + cb_ref[p])
        g, v = conv
        sg = _sigmoid(g)
        d_conv = (da * v * (sg * (1.0 + g * (1.0 - sg))), da * (g * sg))
        for p in range(2):
            d = d_conv[p]
            cwp = cw_ref[p]
            dcb_ref[p] = jnp.sum(d, axis=0, keepdims=True)
            for j in range(3):
                dcw_ref[p, j:j + 1, :] = jnp.sum(d * shifted[p][j], axis=0, keepdims=True)
            du = d * cwp[2:3] + _shift_up(d, 1, row, S) * cwp[1:2] + _shift_up(d, 2, row, S) * cwp[0:1]
            du_ref[p] = du.astype(BF16)

    return pl.pallas_call(
        body, name=name, grid=(F // tn,),
        in_specs=[pl.BlockSpec((S, tn), lambda j: (0, j)), pl.BlockSpec((2, S, tn), lambda j: (0, 0, j)),
                  pl.BlockSpec((2, 3, tn), lambda j: (0, 0, j)), pl.BlockSpec((2, 1, tn), lambda j: (0, 0, j))],
        out_specs=[pl.BlockSpec((2, S, tn), lambda j: (0, 0, j)), pl.BlockSpec((2, 3, tn), lambda j: (0, 0, j)),
                   pl.BlockSpec((2, 1, tn), lambda j: (0, 0, j))],
        out_shape=[jax.ShapeDtypeStruct((2, S, F), BF16), jax.ShapeDtypeStruct((2, 3, F), F32),
                   jax.ShapeDtypeStruct((2, 1, F), F32)],
        compiler_params=_params(("parallel",), 56),
    )(da, u, cw, cb)


def _split3(x):
    hi = x.astype(BF16)
    r1 = x - hi.astype(F32)
    mid = r1.astype(BF16)
    lo = (r1 - mid.astype(F32)).astype(BF16)
    return hi, mid, lo


def _tri_matmul(x, upper, S):
    tc = _tile(S, 512)
    parts = _split3(x)
    outs = []
    for b in range(S // tc):
        r = lax.broadcasted_iota(jnp.int32, (S, tc), 0)
        c = lax.broadcasted_iota(jnp.int32, (S, tc), 1) + b * tc
        tri = jnp.where((r <= c) if upper else (r >= c), 1.0, 0.0).astype(BF16)
        acc = jnp.dot(parts[0], tri, preferred_element_type=F32)
        acc += jnp.dot(parts[1], tri, preferred_element_type=F32)
        acc += jnp.dot(parts[2], tri, preferred_element_type=F32)
        outs.append(acc)
    return outs, tc


def _fox_prep(flT, bf, name):
    H, S = flT.shape

    def body(fl_ref, b_ref, cum_ref):
        zz = fl_ref[...] + b_ref[...]
        lf = jnp.minimum(zz, 0.0) - jnp.log(1.0 + jnp.exp(-jnp.abs(zz)))
        outs, tc = _tri_matmul(lf, True, S)
        for b, o in enumerate(outs):
            cum_ref[:, b * tc:(b + 1) * tc] = o

    return pl.pallas_call(
        body, name=name,
        in_specs=[pl.BlockSpec(memory_space=pltpu.VMEM)] * 2,
        out_specs=pl.BlockSpec(memory_space=pltpu.VMEM),
        out_shape=jax.ShapeDtypeStruct((H, S), F32),
        compiler_params=pltpu.CompilerParams(vmem_limit_bytes=48 * MIB),
    )(flT, bf)


def _fox_prep_bwd(dcumT, flT, bf, name):
    H, S = flT.shape

    def body(dc_ref, fl_ref, b_ref, dfl_ref, dbf_ref):
        zz = fl_ref[...] + b_ref[...]
        outs, tc = _tri_matmul(dc_ref[...], False, S)
        total = jnp.zeros((H, 1), F32)
        for b, o in enumerate(outs):
            dfl = o * _sigmoid(-zz[:, b * tc:(b + 1) * tc])
            dfl_ref[:, b * tc:(b + 1) * tc] = dfl
            total += jnp.sum(dfl, axis=1, keepdims=True)
        dbf_ref[...] = total

    return pl.pallas_call(
        body, name=name,
        in_specs=[pl.BlockSpec(memory_space=pltpu.VMEM)] * 3,
        out_specs=[pl.BlockSpec(memory_space=pltpu.VMEM)] * 2,
        out_shape=[jax.ShapeDtypeStruct((H, S), F32), jax.ShapeDtypeStruct((H, 1), F32)],
        compiler_params=pltpu.CompilerParams(vmem_limit_bytes=48 * MIB),
    )(dcumT, flT, bf)


FOX_TQ = 256


def _fox_scores(q_ref, k_ref, cq_ref, ck_ref, i, tq, S, scale):
    q = q_ref[...].astype(BF16)
    k = k_ref[...].astype(BF16)
    s = lax.dot_general(q, k, (((1,), (1,)), ((), ())), preferred_element_type=F32) * scale
    s = s + cq_ref[...] - ck_ref[...]
    qpos = i * tq + lax.broadcasted_iota(jnp.int32, (tq, S), 0)
    kpos = lax.broadcasted_iota(jnp.int32, (tq, S), 1)
    return jnp.where(kpos <= qpos, s, -jnp.inf), q, k


def _fox_attn_fwd(proj, cum_col, cum_row, H, name):
    S = proj.shape[0]
    dh = FOX_HEAD_DIM
    tq = _tile(S, FOX_TQ)
    scale = dh ** -0.5

    def body(q_ref, k_ref, v_ref, cq_ref, ck_ref, o_ref, lse_ref):
        s, _, _ = _fox_scores(q_ref, k_ref, cq_ref, ck_ref, pl.program_id(1), tq, S, scale)
        m = jnp.max(s, axis=-1, keepdims=True)
        p = jnp.exp(s - m)
        l = jnp.sum(p, axis=-1, keepdims=True)
        o = jnp.dot(p.astype(BF16), v_ref[...].astype(BF16), preferred_element_type=F32) / l
        o_ref[...] = o.astype(BF16)
        lse_ref[...] = m + jnp.log(l)

    return pl.pallas_call(
        body, name=name, grid=(H, S // tq),
        in_specs=[pl.BlockSpec((tq, dh), lambda h, i: (i, h)),
                  pl.BlockSpec((S, dh), lambda h, i: (0, H + h)),
                  pl.BlockSpec((S, dh), lambda h, i: (0, 2 * H + h)),
                  pl.BlockSpec((None, tq, 1), lambda h, i: (h, i, 0)),
                  pl.BlockSpec((None, 1, S), lambda h, i: (h, 0, 0))],
        out_specs=[pl.BlockSpec((tq, dh), lambda h, i: (i, h)),
                   pl.BlockSpec((None, tq, 1), lambda h, i: (h, i, 0))],
        out_shape=[jax.ShapeDtypeStruct((S, H * dh), BF16), jax.ShapeDtypeStruct((H, S, 1), F32)],
        compiler_params=_params(("parallel", "parallel")),
    )(proj, proj, proj, cum_col, cum_row)


def _fox_attn_bwd(proj, do, cum_col, cum_row, lse, H, name):
    S = proj.shape[0]
    dh = FOX_HEAD_DIM
    tq = _tile(S, FOX_TQ)
    scale = dh ** -0.5

    def body(q_ref, k_ref, v_ref, do_ref, cq_ref, ck_ref, lse_ref, dq_ref, dk_ref, dv_ref, dck_ref):
        i = pl.program_id(1)

        @pl.when(i == 0)
        def _():
            dk_ref[...] = jnp.zeros_like(dk_ref)
            dv_ref[...] = jnp.zeros_like(dv_ref)
            dck_ref[...] = jnp.zeros_like(dck_ref)

        s, q, k = _fox_scores(q_ref, k_ref, cq_ref, ck_ref, i, tq, S, scale)
        p = jnp.exp(s - lse_ref[...])
        do_b = do_ref[...].astype(BF16)
        dp = lax.dot_general(do_b, v_ref[...].astype(BF16), (((1,), (1,)), ((), ())), preferred_element_type=F32)
        delta = jnp.sum(p * dp, axis=-1, keepdims=True)
        ds = p * (dp - delta)
        ds_b = ds.astype(BF16)
        dq_ref[...] = (jnp.dot(ds_b, k, preferred_element_type=F32) * scale).astype(BF16)
        dk_ref[...] += lax.dot_general(ds_b, q, (((0,), (0,)), ((), ())), preferred_element_type=F32) * scale
        dv_ref[...] += lax.dot_general(p.astype(BF16), do_b, (((0,), (0,)), ((), ())), preferred_element_type=F32)
        dck_ref[...] -= jnp.sum(ds, axis=0, keepdims=True)

    W = H * dh
    return pl.pallas_call(
        body, name=name, grid=(H, S // tq),
        in_specs=[pl.BlockSpec((tq, dh), lambda h, i: (i, h)),
                  pl.BlockSpec((S, dh), lambda h, i: (0, H + h)),
                  pl.BlockSpec((S, dh), lambda h, i: (0, 2 * H + h)),
                  pl.BlockSpec((tq, dh), lambda h, i: (i, h)),
                  pl.BlockSpec((None, tq, 1), lambda h, i: (h, i, 0)),
                  pl.BlockSpec((None, 1, S), lambda h, i: (h, 0, 0)),
                  pl.BlockSpec((None, tq, 1), lambda h, i: (h, i, 0))],
        out_specs=[pl.BlockSpec((tq, dh), lambda h, i: (i, h)),
                   pl.BlockSpec((S, dh), lambda h, i: (0, h)),
                   pl.BlockSpec((S, dh), lambda h, i: (0, h)),
                   pl.BlockSpec((None, 1, S), lambda h, i: (h, 0, 0))],
        out_shape=[jax.ShapeDtypeStruct((S, W), BF16), jax.ShapeDtypeStruct((S, W), F32),
                   jax.ShapeDtypeStruct((S, W), F32), jax.ShapeDtypeStruct((H, 1, S), F32)],
        compiler_params=_params(("parallel", "arbitrary")),
    )(proj, proj, proj, do, cum_col, cum_row, lse)


def _rope(parts, tabs, out_dtype, name):
    S = parts[0][0].shape[0]
    widths = [a.shape[1] for a, _ in parts]
    total = sum(widths)
    tm = _tile(S, ROW_TILE, 8)
    flags = [r for _, r in parts]

    def body(*refs):
        in_refs = refs[:len(parts)]
        cos_ref, sa_ref, sb_ref, o_ref = refs[len(parts):]
        cos, sa, sb = cos_ref[...], sa_ref[...], sb_ref[...]
        off = 0
        for ref, rot, w in zip(in_refs, flags, widths):
            for j in range(w // LANE):
                t = ref[:, j * LANE:(j + 1) * LANE]
                if rot:
                    t = t * cos + pltpu.roll(t, LANE - ROPE_DIM // 2, axis=1) * sa + pltpu.roll(t, ROPE_DIM // 2, axis=1) * sb
                o_ref[:, off + j * LANE:off + (j + 1) * LANE] = t.astype(o_ref.dtype)
            off += w

    return pl.pallas_call(
        body, name=name, grid=(S // tm,),
        in_specs=[pl.BlockSpec((tm, w), lambda i: (i, 0)) for w in widths] + [_row_spec(tm, LANE)] * 3,
        out_specs=_row_spec(tm, total),
        out_shape=jax.ShapeDtypeStruct((S, total), out_dtype),
        compiler_params=_params(("parallel",)),
    )(*[a for a, _ in parts], *tabs)


def _swa_band(ref_p, ref_c, hk):
    dh = SWA_HEAD_DIM
    return jnp.concatenate([ref_p[:, hk * dh:(hk + 1) * dh], ref_c[:, hk * dh:(hk + 1) * dh]], axis=0).astype(BF16)


def _swa_mask(n):
    qi = lax.broadcasted_iota(jnp.int32, (Q_BLOCK, 2 * Q_BLOCK), 0)
    kj = lax.broadcasted_iota(jnp.int32, (Q_BLOCK, 2 * Q_BLOCK), 1)
    rel = qi + Q_BLOCK - kj
    return (rel >= 0) & (rel < SWA_WINDOW) & ((kj >= Q_BLOCK) | (n > 0))


def _swa_attn_fwd(qk, proj, sinks, Hq, name):
    S = qk.shape[0]
    dh, G, QB = SWA_HEAD_DIM, SWA_GROUP, Q_BLOCK
    Hk = Hq // G
    Wq, Wk = Hq * dh, Hk * dh
    nb = S // QB
    scale = dh ** -0.5

    def body(q_ref, kp_ref, kc_ref, vp_ref, vc_ref, sink_ref, o_ref, lse_ref):
        n = pl.program_id(0)
        mask = _swa_mask(n)
        lane = lax.broadcasted_iota(jnp.int32, (QB, LANE), 1)
        lse_tile = jnp.zeros((QB, LANE), F32)
        for hk in range(Hk):
            kb = _swa_band(kp_ref, kc_ref, hk)
            vb = _swa_band(vp_ref, vc_ref, hk)
            outs = []
            for g in range(G):
                h = hk * G + g
                q = q_ref[:, h * dh:(h + 1) * dh].astype(BF16)
                s = lax.dot_general(q, kb, (((1,), (1,)), ((), ())), preferred_element_type=F32) * scale
                s = jnp.where(mask, s, -jnp.inf)
                sk = sink_ref[0:1, h:h + 1]
                m = jnp.maximum(jnp.max(s, axis=-1, keepdims=True), sk)
                p = jnp.exp(s - m)
                l = jnp.sum(p, axis=-1, keepdims=True) + jnp.exp(sk - m)
                outs.append(jnp.dot(p.astype(BF16), vb, preferred_element_type=F32) / l)
                lse_tile = jnp.where(lane == h, m + jnp.log(l), lse_tile)
            for g in range(0, G, 2):
                c0 = (hk * G + g) * dh
                o_ref[:, c0:c0 + 2 * dh] = jnp.concatenate([outs[g], outs[g + 1]], axis=1).astype(BF16)
        lse_ref[...] = lse_tile

    kcol, vcol = Wq // Wk, (Wq + Wk) // Wk
    return pl.pallas_call(
        body, name=name, grid=(nb,),
        in_specs=[pl.BlockSpec((QB, Wq), lambda n: (n, 0)),
                  pl.BlockSpec((QB, Wk), lambda n: (jnp.maximum(n - 1, 0), kcol)),
                  pl.BlockSpec((QB, Wk), lambda n: (n, kcol)),
                  pl.BlockSpec((QB, Wk), lambda n: (jnp.maximum(n - 1, 0), vcol)),
                  pl.BlockSpec((QB, Wk), lambda n: (n, vcol)),
                  pl.BlockSpec((1, LANE), lambda n: (0, 0))],
        out_specs=[pl.BlockSpec((QB, Wq), lambda n: (n, 0)), pl.BlockSpec((QB, LANE), lambda n: (n, 0))],
        out_shape=[jax.ShapeDtypeStruct((S, Wq), BF16), jax.ShapeDtypeStruct((S, LANE), F32)],
        compiler_params=_params(("parallel",)),
    )(qk, qk, qk, proj, proj, sinks)


def _swa_attn_bwd(qk, proj, sinks, do, lse, Hq, name):
    S = qk.shape[0]
    dh, G, QB = SWA_HEAD_DIM, SWA_GROUP, Q_BLOCK
    Hk = Hq // G
    Wq, Wk = Hq * dh, Hk * dh
    nb = S // QB
    scale = dh ** -0.5

    def body(q_ref, kp_ref, kc_ref, vp_ref, vc_ref, sink_ref, do_ref, lse_ref,
             dq_ref, dk_ref, dv_ref, dsink_ref, carry_k, carry_v):
        n = pl.program_id(0)

        @pl.when(n == 0)
        def _():
            dsink_ref[...] = jnp.zeros_like(dsink_ref)

        @pl.when(n < nb)
        def _():
            mask = _swa_mask(n)
            lane = lax.broadcasted_iota(jnp.int32, (1, LANE), 1)
            dsink = jnp.zeros((1, LANE), F32)
            dk_heads, dv_heads = [], []
            for hk in range(Hk):
                kb = _swa_band(kp_ref, kc_ref, hk)
                vb = _swa_band(vp_ref, vc_ref, hk)
                dkb = jnp.zeros((2 * QB, dh), F32)
                dvb = jnp.zeros((2 * QB, dh), F32)
                dqs = []
                for g in range(G):
                    h = hk * G + g
                    q = q_ref[:, h * dh:(h + 1) * dh].astype(BF16)
                    do_h = do_ref[:, h * dh:(h + 1) * dh].astype(BF16)
                    lse_h = lse_ref[:, h:h + 1]
                    s = lax.dot_general(q, kb, (((1,), (1,)), ((), ())), preferred_element_type=F32) * scale
                    s = jnp.where(mask, s, -jnp.inf)
                    p = jnp.exp(s - lse_h)
                    p_sink = jnp.exp(sink_ref[0:1, h:h + 1] - lse_h)
                    dp = lax.dot_general(do_h, vb, (((1,), (1,)), ((), ())), preferred_element_type=F32)
                    delta = jnp.sum(p * dp, axis=-1, keepdims=True)
                    ds_b = (p * (dp - delta)).astype(BF16)
                    dqs.append(jnp.dot(ds_b, kb, preferred_element_type=F32) * scale)
                    dkb += lax.dot_general(ds_b, q, (((0,), (0,)), ((), ())), preferred_element_type=F32) * scale
                    dvb += lax.dot_general(p.astype(BF16), do_h, (((0,), (0,)), ((), ())), preferred_element_type=F32)
                    dsink = jnp.where(lane == h, -jnp.sum(p_sink * delta, axis=0, keepdims=True), dsink)
                for g in range(0, G, 2):
                    c0 = (hk * G + g) * dh
                    dq_ref[:, c0:c0 + 2 * dh] = jnp.concatenate([dqs[g], dqs[g + 1]], axis=1)
                dk_heads.append(dkb)
                dv_heads.append(dvb)
            dsink_ref[...] += dsink
            dk_all = jnp.concatenate(dk_heads, axis=1)
            dv_all = jnp.concatenate(dv_heads, axis=1)

            @pl.when(n > 0)
            def _():
                dk_ref[...] = carry_k[...] + dk_all[:QB]
                dv_ref[...] = carry_v[...] + dv_all[:QB]

            carry_k[...] = dk_all[QB:]
            carry_v[...] = dv_all[QB:]

        @pl.when(n == nb)
        def _():
            dk_ref[...] = carry_k[...]
            dv_ref[...] = carry_v[...]

    kcol, vcol = Wq // Wk, (Wq + Wk) // Wk
    cur = lambda n: jnp.minimum(n, nb - 1)
    prev = lambda n: jnp.maximum(jnp.minimum(n, nb - 1) - 1, 0)
    return pl.pallas_call(
        body, name=name, grid=(nb + 1,),
        in_specs=[pl.BlockSpec((QB, Wq), lambda n: (cur(n), 0)),
                  pl.BlockSpec((QB, Wk), lambda n: (prev(n), kcol)),
                  pl.BlockSpec((QB, Wk), lambda n: (cur(n), kcol)),
                  pl.BlockSpec((QB, Wk), lambda n: (prev(n), vcol)),
                  pl.BlockSpec((QB, Wk), lambda n: (cur(n), vcol)),
                  pl.BlockSpec((1, LANE), lambda n: (0, 0)),
                  pl.BlockSpec((QB, Wq), lambda n: (cur(n), 0)),
                  pl.BlockSpec((QB, LANE), lambda n: (cur(n), 0))],
        out_specs=[pl.BlockSpec((QB, Wq), lambda n: (cur(n), 0)),
                   pl.BlockSpec((QB, Wk), lambda n: (jnp.maximum(n - 1, 0), 0)),
                   pl.BlockSpec((QB, Wk), lambda n: (jnp.maximum(n - 1, 0), 0)),
                   pl.BlockSpec((1, LANE), lambda n: (0, 0))],
        out_shape=[jax.ShapeDtypeStruct((S, Wq), F32), jax.ShapeDtypeStruct((S, Wk), F32),
                   jax.ShapeDtypeStruct((S, Wk), F32), jax.ShapeDtypeStruct((1, LANE), F32)],
        scratch_shapes=[pltpu.VMEM((QB, Wk), F32), pltpu.VMEM((QB, Wk), F32)],
        compiler_params=_params(("arbitrary",)),
    )(qk, qk, qk, proj, proj, sinks, do, lse)


ADA_ROWS = 16


def _ada_mod(c_pad, w, b, name):
    L, D, N = w.shape
    tn = _tile(N, 512)

    def body(c_ref, w_ref, b_ref, o_ref):
        c = c_ref[...]
        c = c * _sigmoid(c)
        ch = c.astype(BF16)
        cl = (c - ch.astype(F32)).astype(BF16)
        ww = w_ref[...]
        wh = ww.astype(BF16)
        wl = (ww - wh.astype(F32)).astype(BF16)
        acc = jnp.dot(ch, wh, preferred_element_type=F32)
        acc += jnp.dot(ch, wl, preferred_element_type=F32)
        acc += jnp.dot(cl, wh, preferred_element_type=F32)
        o_ref[...] = acc + b_ref[...]

    return pl.pallas_call(
        body, name=name, grid=(L, N // tn),
        in_specs=[pl.BlockSpec((ADA_ROWS, D), lambda l, j: (0, 0)),
                  pl.BlockSpec((None, D, tn), lambda l, j: (l, 0, j)),
                  pl.BlockSpec((None, 1, tn), lambda l, j: (l, 0, j))],
        out_specs=pl.BlockSpec((None, ADA_ROWS, tn), lambda l, j: (l, 0, j)),
        out_shape=jax.ShapeDtypeStruct((L, ADA_ROWS, N), F32),
        compiler_params=_params(("parallel", "parallel")),
    )(c_pad, w, b)


def _ada_bwd(cT, dm, name):
    D = cT.shape[0]
    L, B, N = dm.shape
    tm = _tile(D, 256)

    def body(c_ref, dm_ref, o_ref):
        c = c_ref[...]
        c = c * _sigmoid(c)
        dmv = dm_ref[...]
        acc = c[:, 0:1] * dmv[0:1, :]
        for b in range(1, B):
            acc += c[:, b:b + 1] * dmv[b:b + 1, :]
        o_ref[...] = acc

    return pl.pallas_call(
        body, name=name, grid=(L, D // tm),
        in_specs=[pl.BlockSpec((tm, B), lambda l, i: (i, 0)), pl.BlockSpec((None, B, N), lambda l, i: (l, 0, 0))],
        out_specs=pl.BlockSpec((None, tm, N), lambda l, i: (l, i, 0)),
        out_shape=jax.ShapeDtypeStruct((L, D, N), F32),
        compiler_params=_params(("parallel", "parallel")),
    )(cT, dm)


def _adamw_math(w, g, m, v):
    m = ADAM_B1 * m + (1.0 - ADAM_B1) * g
    v = ADAM_B2 * v + (1.0 - ADAM_B2) * (g * g)
    m_hat = m / (1.0 - ADAM_B1 ** ADAM_STEP)
    v_hat = v / (1.0 - ADAM_B2 ** ADAM_STEP)
    delta = -ADAM_LR * (m_hat / (jnp.sqrt(v_hat) + ADAM_EPS) + ADAM_WD * w)
    return delta, m, v


def _adam_rows(R, C):
    lanes = -(-C // LANE) * LANE
    pref = max(8, (6144 // lanes) // 8 * 8 * 8)
    return _tile(R, pref, 8)


def _adam_sum(recv, w, m, v, name):
    P, L, R, C = recv.shape
    tr = _adam_rows(R, C)

    def body(r_ref, w_ref, m_ref, v_ref, g_ref, d_ref, mo_ref, vo_ref):
        g = r_ref[0].astype(F32)
        for p in range(1, P):
            g = g + r_ref[p].astype(F32)
        delta, mn, vn = _adamw_math(w_ref[...], g, m_ref[...], v_ref[...])
        g_ref[...] = g
        d_ref[...] = delta
        mo_ref[...] = mn
        vo_ref[...] = vn

    spec = pl.BlockSpec((None, tr, C), lambda l, i: (l, i, 0))
    return pl.pallas_call(
        body, name=name, grid=(L, R // tr),
        in_specs=[pl.BlockSpec((P, None, tr, C), lambda l, i: (0, l, i, 0)), spec, spec, spec],
        out_specs=[spec] * 4,
        out_shape=[jax.ShapeDtypeStruct((L, R, C), F32)] * 4,
        compiler_params=_params(("parallel", "parallel")),
    )(recv, w, m, v)


def _adam(g, w, m, v, name):
    L, R, C = w.shape
    tr = _adam_rows(R, C)

    def body(g_ref, w_ref, m_ref, v_ref, d_ref, mo_ref, vo_ref):
        delta, mn, vn = _adamw_math(w_ref[...], g_ref[...], m_ref[...], v_ref[...])
        d_ref[...] = delta
        mo_ref[...] = mn
        vo_ref[...] = vn

    spec = pl.BlockSpec((None, tr, C), lambda l, i: (l, i, 0))
    return pl.pallas_call(
        body, name=name, grid=(L, R // tr),
        in_specs=[spec] * 4, out_specs=[spec] * 3,
        out_shape=[jax.ShapeDtypeStruct((L, R, C), F32)] * 3,
        compiler_params=_params(("parallel", "parallel")),
    )(g, w, m, v)


def _sum_slots(x, name):
    P, R, C = x.shape

    def body(x_ref, o_ref):
        acc = x_ref[0]
        for p in range(1, P):
            acc = acc + x_ref[p]
        o_ref[...] = acc

    return pl.pallas_call(
        body, name=name,
        in_specs=[pl.BlockSpec(memory_space=pltpu.VMEM)], out_specs=pl.BlockSpec(memory_space=pltpu.VMEM),
        out_shape=jax.ShapeDtypeStruct((R, C), F32),
        compiler_params=pltpu.CompilerParams(vmem_limit_bytes=48 * MIB),
    )(x)


def _my_pos():
    return lax.axis_index("x"), lax.axis_index("y"), lax.axis_index("c")


def _all_gather_small(x, name):
    R, C = x.shape

    def body(x_ref, out_ref, send_sems, recv_sems):
        x_, y_, c_ = _my_pos()
        me, sibling = (x_, y_, c_), (x_, y_, 1 - c_)
        chips = [(1 - x_, y_), (x_, 1 - y_), (1 - x_, 1 - y_)]

        def slot(px, py, pc):
            return out_ref.at[4 * px + 2 * py + pc]

        def copy(k, block, to):
            return pltpu.make_async_remote_copy(
                src_ref=slot(*block), dst_ref=slot(*block), send_sem=send_sems.at[k], recv_sem=recv_sems.at[k],
                device_id=to, device_id_type=MESH)

        out_ref[4 * x_ + 2 * y_ + c_] = x_ref[...]
        first = [copy(0, me, sibling)] + [copy(1 + j, me, (*chip, c_)) for j, chip in enumerate(chips)]
        for cp in first:
            cp.start()
        passed = [copy(4 + j, (*chip, c_), sibling) for j, chip in enumerate(chips)]
        for j, chip in enumerate(chips):
            copy(1 + j, (*chip, c_), me).wait_recv()
            passed[j].start()
        copy(0, sibling, me).wait_recv()
        for j, chip in enumerate(chips):
            copy(4 + j, (*chip, 1 - c_), me).wait_recv()
        for cp in first + passed:
            cp.wait_send()

    return pl.pallas_call(
        body, name=name,
        in_specs=[pl.BlockSpec(memory_space=pltpu.VMEM)], out_specs=pl.BlockSpec(memory_space=pltpu.VMEM),
        out_shape=jax.ShapeDtypeStruct((N_DEV, R, C), x.dtype),
        scratch_shapes=[pltpu.SemaphoreType.DMA((7,)), pltpu.SemaphoreType.DMA((7,))],
        compiler_params=pltpu.CompilerParams(vmem_limit_bytes=48 * MIB),
    )(x)


N_BIG = 8


def _all_gather_weights(fox_in, fox_o, swa_in, swa_o, up, down, name):
    L, D, cu = up.shape
    F = 4 * cu
    rd = down.shape[1]
    ro = fox_o.shape[0]

    def body(fi_s, fo_s, si_s, so_s, up_s, dn_s, fi_g, fo_g, si_g, so_g, up_g, dn_g, send_sems, recv_sems, local_sems):
        x_, y_, c_ = _my_pos()
        me, sibling = (x_, y_, c_), (x_, y_, 1 - c_)
        chips = [(1 - x_, y_), (x_, 1 - y_), (1 - x_, 1 - y_)]

        def idx(px, py, pc):
            return 4 * px + 2 * py + pc

        def rows(n, k):
            return pl.ds(pl.multiple_of(k * n, n), n)

        srcs = [fi_s, fo_s, si_s, so_s, up_s.at[0], up_s.at[1], dn_s.at[0], dn_s.at[1]]
        dsts = [
            lambda b: fi_g.at[idx(*b)],
            lambda b: fo_g.at[rows(ro, idx(*b)), :],
            lambda b: si_g.at[idx(*b)],
            lambda b: so_g.at[rows(ro, idx(*b)), :],
            lambda b: up_g.at[0, b[0], :, rows(cu, 2 * b[1] + b[2])],
            lambda b: up_g.at[1, b[0], :, rows(cu, 2 * b[1] + b[2])],
            lambda b: dn_g.at[0, rows(rd, idx(*b)), :],
            lambda b: dn_g.at[1, rows(rd, idx(*b)), :],
        ]

        def copy(e, k, block, to, own=False):
            return pltpu.make_async_remote_copy(
                src_ref=srcs[e] if own else dsts[e](block), dst_ref=dsts[e](block),
                send_sem=send_sems.at[e, k], recv_sem=recv_sems.at[e, k], device_id=to, device_id_type=MESH)

        mine = [pltpu.make_async_copy(srcs[e], dsts[e](me), local_sems.at[e]) for e in range(N_BIG)]
        for cp in mine:
            cp.start()
        first = []
        for e in range(N_BIG):
            first.append(copy(e, 0, me, sibling, own=True))
            first += [copy(e, 1 + j, me, (*chip, c_), own=True) for j, chip in enumerate(chips)]
        for cp in first:
            cp.start()
        passed = []
        for e in range(N_BIG):
            for j, chip in enumerate(chips):
                copy(e, 1 + j, (*chip, c_), me).wait_recv()
                fwd = copy(e, 4 + j, (*chip, c_), sibling)
                fwd.start()
                passed.append(fwd)
        for e in range(N_BIG):
            copy(e, 0, sibling, me).wait_recv()
            for j, chip in enumerate(chips):
                copy(e, 4 + j, (*chip, 1 - c_), me).wait_recv()
        for cp in first + passed:
            cp.wait_send()
        for cp in mine:
            cp.wait()

    any_spec = pl.BlockSpec(memory_space=pl.ANY)
    return pl.pallas_call(
        body, name=name,
        in_specs=[any_spec] * 6, out_specs=[any_spec] * 6,
        out_shape=[jax.ShapeDtypeStruct((N_DEV,) + fox_in.shape, BF16), jax.ShapeDtypeStruct((N_DEV * ro, D), BF16),
                   jax.ShapeDtypeStruct((N_DEV,) + swa_in.shape, BF16), jax.ShapeDtypeStruct((N_DEV * ro, D), BF16),
                   jax.ShapeDtypeStruct((L, 2, D, F), BF16), jax.ShapeDtypeStruct((L, N_DEV * rd, D), BF16)],
        scratch_shapes=[pltpu.SemaphoreType.DMA((N_BIG, 7)), pltpu.SemaphoreType.DMA((N_BIG, 7)),
                        pltpu.SemaphoreType.DMA((N_BIG,))],
    )(fox_in, fox_o, swa_in, swa_o, up, down)


def _reduce_scatter_grads(d_fi, d_fo, d_si, d_so, d_up0, d_up1, d_dn0, d_dn1, name):
    D = d_fo.shape[0]
    ro = D // N_DEV
    F = d_dn0.shape[0]
    rd = F // N_DEV
    cu = F // 4

    def body(fi, fo, si, so, up0, up1, dn0, dn1, r_fi, r_fo, r_si, r_so, r_up, r_dn, send_sems, recv_sems):
        x_, y_, c_ = _my_pos()
        me = 4 * x_ + 2 * y_ + c_

        def src(e, j):
            return [fi.at[j], fo.at[pl.ds(j * ro, ro), :], si.at[j], so.at[pl.ds(j * ro, ro), :],
                    up0.at[j // 4, :, pl.ds((j % 4) * cu, cu)], up1.at[j // 4, :, pl.ds((j % 4) * cu, cu)],
                    dn0.at[pl.ds(j * rd, rd), :], dn1.at[pl.ds(j * rd, rd), :]][e]

        def dst(e, i):
            return [r_fi.at[i, 0], r_fo.at[i, 0], r_si.at[i, 0], r_so.at[i, 0],
                    r_up.at[i, 0], r_up.at[i, 1], r_dn.at[i, 0], r_dn.at[i, 1]][e]

        def remote(e, j):
            return pltpu.make_async_remote_copy(
                src_ref=src(e, j), dst_ref=dst(e, me), send_sem=send_sems.at[e, j], recv_sem=recv_sems.at[e, me],
                device_id=(j // 4, (j // 2) % 2, j % 2), device_id_type=MESH)

        def local(e, j):
            return pltpu.make_async_copy(src(e, j), dst(e, j), recv_sems.at[e, j])

        def arrival(e, i):
            return pltpu.make_async_remote_copy(
                src_ref=src(e, i), dst_ref=dst(e, i), send_sem=send_sems.at[e, i], recv_sem=recv_sems.at[e, i],
                device_id=(i // 4, (i // 2) % 2, i % 2), device_id_type=MESH)

        for e in range(N_BIG):
            for j in range(N_DEV):
                @pl.when(me == j)
                def _():
                    local(e, j).start()

                @pl.when(me != j)
                def _():
                    remote(e, j).start()
        for e in range(N_BIG):
            for i in range(N_DEV):
                @pl.when(me == i)
                def _():
                    local(e, i).wait()

                @pl.when(me != i)
                def _():
                    arrival(e, i).wait_recv()
        for e in range(N_BIG):
            for j in range(N_DEV):
                @pl.when(me != j)
                def _():
                    remote(e, j).wait_send()

    any_spec = pl.BlockSpec(memory_space=pl.ANY)
    return pl.pallas_call(
        body, name=name,
        in_specs=[any_spec] * 8, out_specs=[any_spec] * 6,
        out_shape=[jax.ShapeDtypeStruct((N_DEV, 1) + d_fi.shape[1:], BF16), jax.ShapeDtypeStruct((N_DEV, 1, ro, D), BF16),
                   jax.ShapeDtypeStruct((N_DEV, 1) + d_si.shape[1:], BF16), jax.ShapeDtypeStruct((N_DEV, 1, ro, D), BF16),
                   jax.ShapeDtypeStruct((N_DEV, 2, D, cu), BF16), jax.ShapeDtypeStruct((N_DEV, 2, rd, D), BF16)],
        scratch_shapes=[pltpu.SemaphoreType.DMA((N_BIG, N_DEV)), pltpu.SemaphoreType.DMA((N_BIG, N_DEV))],
    )(d_fi, d_fo, d_si, d_so, d_up0, d_up1, d_dn0, d_dn1)


def _rope_tables(positions, sign):
    half = ROPE_DIM // 2
    inv_freq = ROPE_THETA ** (-jnp.arange(0, ROPE_DIM, 2, dtype=F32) / ROPE_DIM)
    ang = positions.astype(F32)[:, None] * inv_freq
    reps = LANE // half
    cos = jnp.tile(jnp.cos(ang), (1, reps))
    sin = jnp.tile(jnp.sin(ang), (1, reps)) * sign
    d = jnp.arange(LANE) % SWA_HEAD_DIM
    return (jnp.where(d < ROPE_DIM, cos, 1.0), jnp.where(d < half, -sin, 0.0),
            jnp.where((d >= half) & (d < ROPE_DIM), sin, 0.0))


def _pad_cols(a, n):
    return jnp.pad(a, ((0, 0), (0, n - a.shape[1])))


def _local_step(x, target, positions, mods, W, P):
    S, D = x.shape
    Hf = D // FOX_HEAD_DIM
    Hq = D // SWA_HEAD_DIM
    Hk = Hq // SWA_GROUP
    Wk = Hk * SWA_HEAD_DIM
    n_in = 3 * D + Hf
    (sh1a, sc1a, g1a, sh2a, sc2a, g2a), (sh1b, sc1b, g1b, sh2b, sc2b, g2b) = mods
    row = lambda v: v.reshape(1, -1)
    cw = [jnp.transpose(P["conv_w"][l].reshape(3, 2, -1), (1, 0, 2)) for l in range(2)]
    cb = [P["conv_b"][l].reshape(2, 1, -1) for l in range(2)]

    h1a = _modulate(x, sc1a, sh1a, "modulate_in")
    proj_a = _mm_nn(h1a, W["fox_in"], F32, "fox_in_proj", tn=896)
    flT = proj_a[:, 3 * D:n_in].T
    bf_col = P["fox_b_f"].reshape(Hf, 1)
    cumT = _fox_prep(flT, bf_col, "fox_cumsum")
    cum_col, cum_row = cumT.reshape(Hf, S, 1), cumT.reshape(Hf, 1, S)
    o_a, lse_a = _fox_attn_fwd(proj_a, cum_col, cum_row, Hf, "fox_attn_fwd")
    y1a = _mm_nn(o_a, W["fox_o"], F32, "fox_out_proj")
    z1a, x1, h2a = _ln_fwd(x, y1a, g1a, row(P["ln_mix_g"][0]), row(P["ln_mix_b"][0]), sc2a, sh2a, "ln_mix0")
    u_a, a_a = _ffn_up(h2a, W["up"][0], cw[0], cb[0], "ffn_up0")
    y2a = _mm_nn(a_a, W["down"][0], F32, "ffn_down0")
    z2a, x2, h1b = _ln_fwd(x1, y2a, g2a, row(P["ln_ffn_g"][0]), row(P["ln_ffn_b"][0]), sc1b, sh1b, "ln_ffn0")

    proj_b = _mm_nn(h1b, W["swa_in"], F32, "swa_in_proj")
    tabs_f = _rope_tables(positions, 1.0)
    tabs_b = _rope_tables(positions, -1.0)
    qk = _rope([(proj_b[:, :D + Wk], True)], tabs_f, F32, "rope_fwd")
    sinks = _pad_cols(P["swa_sinks"].reshape(1, Hq), LANE)
    o_b, lse_b = _swa_attn_fwd(qk, proj_b, sinks, Hq, "swa_attn_fwd")
    y1b = _mm_nn(o_b, W["swa_o"], F32, "swa_out_proj")
    z1b, x3, h2b = _ln_fwd(x2, y1b, g1b, row(P["ln_mix_g"][1]), row(P["ln_mix_b"][1]), sc2b, sh2b, "ln_mix1")
    u_b, a_b = _ffn_up(h2b, W["up"][1], cw[1], cb[1], "ffn_up1")
    y2b = _mm_nn(a_b, W["down"][1], F32, "ffn_down1")
    z2b, dout, loss_row = _ln_fwd_loss(x3, y2b, g2b, row(P["ln_ffn_g"][1]), row(P["ln_ffn_b"][1]), target, "ln_ffn1_loss")

    def ffn_backward(dy, a, u, h_in, l, tag):
        da = _mm_nt(dy[None], W["down"][l][None], F32, "ffn_da" + tag)
        d_down = _mm_tn(a, dy[None], BF16, "ffn_dwdown" + tag)[0]
        du, dcw, dcb = _ffn_bwd_elem(da, u, cw[l], cb[l], "ffn_bwd_elem" + tag)
        dh = _mm_nt(du, W["up"][l], F32, "ffn_dh" + tag)
        d_up = _mm_tn(h_in, du, BF16, "ffn_dwup" + tag)
        return dh, d_up, d_down, jnp.transpose(dcw, (1, 0, 2)).reshape(3, -1), dcb.reshape(-1)

    dz2b, dy2b, dg_f1, db_f1, dgate2b = _ln_bwd(dout, z2b, y2b, g2b, row(P["ln_ffn_g"][1]), "ln_ffn1_bwd")
    dh2b, d_up1, d_down1, dcw1, dcb1 = ffn_backward(dy2b, a_b, u_b, h2b, 1, "1")
    dx3, dsc2b, dsh2b = _mod_bwd(dz2b, dh2b, x3, sc2b, "mod_ffn1_bwd")

    dz1b, dy1b, dg_m1, db_m1, dgate1b = _ln_bwd(dx3, z1b, y1b, g1b, row(P["ln_mix_g"][1]), "ln_mix1_bwd")
    do_b = _mm_nt(dy1b[None], W["swa_o"][None], F32, "swa_do")
    d_swa_o = _mm_tn(o_b, dy1b[None], BF16, "swa_dwo")[0]
    dq_b, dk_b, dv_b, dsinks = _swa_attn_bwd(qk, proj_b, sinks, do_b, lse_b, Hq, "swa_attn_bwd")
    dproj_b = _rope([(dq_b, True), (dk_b, True), (dv_b, False)], tabs_b, BF16, "rope_bwd")
    dh1b = _mm_nt(dproj_b[None], W["swa_in"][None], F32, "swa_dh")
    d_swa_in = _mm_tn(h1b, dproj_b[None], BF16, "swa_dwin")[0]
    dx2, dsc1b, dsh1b = _mod_bwd(dz1b, dh1b, x2, sc1b, "mod_mix1_bwd")

    dz2a, dy2a, dg_f0, db_f0, dgate2a = _ln_bwd(dx2, z2a, y2a, g2a, row(P["ln_ffn_g"][0]), "ln_ffn0_bwd")
    dh2a, d_up0, d_down0, dcw0, dcb0 = ffn_backward(dy2a, a_a, u_a, h2a, 0, "0")
    dx1, dsc2a, dsh2a = _mod_bwd(dz2a, dh2a, x1, sc2a, "mod_ffn0_bwd")

    dz1a, dy1a, dg_m0, db_m0, dgate1a = _ln_bwd(dx1, z1a, y1a, g1a, row(P["ln_mix_g"][0]), "ln_mix0_bwd")
    do_a = _mm_nt(dy1a[None], W["fox_o"][None], F32, "fox_do")
    d_fox_o = _mm_tn(o_a, dy1a[None], BF16, "fox_dwo")[0]
    dq_a, dk_a, dv_a, dcum_row = _fox_attn_bwd(proj_a, do_a, cum_col, cum_row, lse_a, Hf, "fox_attn_bwd")
    dflT, dbf = _fox_prep_bwd(dcum_row.reshape(Hf, S), flT, bf_col, "fox_cumsum_bwd")
    n_pad = W["fox_in"].shape[1]
    dproj_a = jnp.concatenate([dq_a, dk_a.astype(BF16), dv_a.astype(BF16),
                               _pad_cols(dflT.T, n_pad - 3 * D).astype(BF16)], axis=1)
    dh1a = _mm_nt(dproj_a[None], W["fox_in"][None], F32, "fox_dh", tk=896)
    d_fox_in = _mm_tn(h1a, dproj_a[None], BF16, "fox_dwin", tn=896)[0]
    grad_x, dsc1a, dsh1a = _mod_bwd(dz1a, dh1a, x, sc1a, "mod_mix0_bwd")

    dmod = jnp.stack([jnp.concatenate([dsh1a, dsc1a, dgate1a, dsh2a, dsc2a, dgate2a], axis=1)[0],
                      jnp.concatenate([dsh1b, dsc1b, dgate1b, dsh2b, dsc2b, dgate2b], axis=1)[0]])
    big = dict(fox_in=d_fox_in, fox_o=d_fox_o, swa_in=d_swa_in, swa_o=d_swa_o,
               up0=d_up0, up1=d_up1, down0=d_down0, down1=d_down1)
    small = dict(dmod=dmod, conv_b=jnp.stack([dcb0, dcb1]), conv_w=jnp.stack([dcw0, dcw1]),
                 ln_mix_g=jnp.concatenate([dg_m0, dg_m1]), ln_mix_b=jnp.concatenate([db_m0, db_m1]),
                 ln_ffn_g=jnp.concatenate([dg_f0, dg_f1]), ln_ffn_b=jnp.concatenate([db_f0, db_f1]),
                 fox_b_f=dbf.reshape(-1), swa_sinks=dsinks[0, :Hq])
    return loss_row[0, 0], grad_x, big, small


SMALL_ORDER = ("dmod", "conv_b", "conv_w", "ln_mix_g", "ln_mix_b", "ln_ffn_g", "ln_ffn_b", "fox_b_f", "swa_sinks")


def _pack_rows(arrays):
    chunks, spans, off = [], [], 0
    for a in arrays:
        flat = a.reshape(-1)
        n = -(-flat.shape[0] // LANE) * LANE
        chunks.append(jnp.pad(flat, (0, n - flat.shape[0])))
        spans.append((off, flat.shape[0], a.shape))
        off += n
    total = -(-off // (8 * LANE)) * (8 * LANE)
    chunks.append(jnp.zeros((total - off,), F32))
    return jnp.concatenate(chunks).reshape(-1, LANE), spans


def _unpack_rows(packed, spans):
    flat = packed.reshape(-1)
    return [flat[off:off + n].reshape(shape) for off, n, shape in spans]


def kernel(x, c, positions, fox_w_in, fox_b_f, fox_w_o, swa_w_in, swa_sinks, swa_w_o, ada_w, ada_b, ffn_w_up, ffn_conv_w, ffn_conv_b, ffn_w_down, ln_mix_g, ln_mix_b, ln_ffn_g, ln_ffn_b, loss_target, m_fox_w_in, m_fox_b_f, m_fox_w_o, m_swa_w_in, m_swa_sinks, m_swa_w_o, m_ada_w, m_ada_b, m_ffn_w_up, m_ffn_conv_w, m_ffn_conv_b, m_ffn_w_down, m_ln_mix_g, m_ln_mix_b, m_ln_ffn_g, m_ln_ffn_b, v_fox_w_in, v_fox_b_f, v_fox_w_o, v_swa_w_in, v_swa_sinks, v_swa_w_o, v_ada_w, v_ada_b, v_ffn_w_up, v_ffn_conv_w, v_ffn_conv_b, v_ffn_w_down, v_ln_mix_g, v_ln_mix_b, v_ln_ffn_g, v_ln_ffn_b):
    S, D = x.shape[1], x.shape[2]
    L = ada_w.shape[0]
    me = 4 * lax.axis_index("x") + 2 * lax.axis_index("y") + lax.axis_index("c")
    n_ada = ada_w.shape[2]
    cu = ffn_w_up.shape[2]
    F = 4 * cu
    n_in = fox_w_in.shape[2] * N_DEV
    n_in_pad = -(-n_in // LANE) * LANE

    c_all = _all_gather_small(c.reshape(-1, LANE), "gather_c").reshape(N_DEV, D)
    b_cols = lax.dynamic_slice_in_dim(ada_b, me * n_ada, n_ada, axis=1).reshape(L, 1, n_ada)
    mod_blk = _ada_mod(jnp.pad(c_all, ((0, ADA_ROWS - N_DEV), (0, 0))), ada_w, b_cols, "ada_mod")[:, :N_DEV]
    mod_all = _all_gather_small(mod_blk.reshape(-1, LANE), "gather_mod").reshape(N_DEV, L, N_DEV, n_ada)
    mod_mine = lax.dynamic_index_in_dim(mod_all, me, axis=2, keepdims=False)
    mod_mine = jnp.transpose(mod_mine, (1, 0, 2)).reshape(L, N_DEV * n_ada)
    mods = [[mod_mine[l, k * D:(k + 1) * D].reshape(1, D) for k in range(6)] for l in range(L)]

    g_fi, g_fo, g_si, g_so, g_up, g_dn = _all_gather_weights(
        fox_w_in[0].astype(BF16), fox_w_o[0].astype(BF16), swa_w_in[0].astype(BF16), swa_w_o[0].astype(BF16),
        ffn_w_up.astype(BF16), ffn_w_down.astype(BF16), "gather_weights")
    W = dict(fox_in=_pad_cols(jnp.transpose(g_fi, (1, 0, 2)).reshape(D, n_in), n_in_pad), fox_o=g_fo,
             swa_in=jnp.transpose(g_si, (1, 0, 2)).reshape(D, -1), swa_o=g_so,
             up=[g_up[l] for l in range(L)], down=[g_dn[l] for l in range(L)])

    conv_w_all = None
    P = dict(fox_b_f=fox_b_f[0], swa_sinks=swa_sinks[0], conv_b=ffn_conv_b,
             ln_mix_g=ln_mix_g, ln_mix_b=ln_mix_b, ln_ffn_g=ln_ffn_g, ln_ffn_b=ln_ffn_b)
    cw_rows = _all_gather_small(_pack_rows([ffn_conv_w])[0], "gather_conv_w")
    n_cw = ffn_conv_w.size
    cw_dev = cw_rows.reshape(N_DEV, -1)[:, :n_cw].reshape(N_DEV, L, 3, cu)
    conv_w_all = jnp.transpose(cw_dev, (1, 2, 0, 3)).reshape(L, 3, N_DEV * cu)
    P["conv_w"] = conv_w_all

    loss_local, grad_x, big, small = _local_step(x[0], loss_target[0], positions[0], mods, W, P)
    loss = lax.psum(loss_local, AXES)

    d_fi = jnp.transpose(big["fox_in"][:, :n_in].reshape(D, N_DEV, -1), (1, 0, 2))
    d_si = jnp.transpose(big["swa_in"].reshape(D, N_DEV, -1), (1, 0, 2))
    r_fi, r_fo, r_si, r_so, r_up, r_dn = _reduce_scatter_grads(
        d_fi, big["fox_o"], d_si, big["swa_o"], big["up0"], big["up1"], big["down0"], big["down1"], "scatter_grads")
    out = {}
    out["fox_w_in"] = _adam_sum(r_fi, fox_w_in, m_fox_w_in, v_fox_w_in, "adam_fox_w_in")
    out["fox_w_o"] = _adam_sum(r_fo, fox_w_o, m_fox_w_o, v_fox_w_o, "adam_fox_w_o")
    out["swa_w_in"] = _adam_sum(r_si, swa_w_in, m_swa_w_in, v_swa_w_in, "adam_swa_w_in")
    out["swa_w_o"] = _adam_sum(r_so, swa_w_o, m_swa_w_o, v_swa_w_o, "adam_swa_w_o")
    out["ffn_w_up"] = _adam_sum(r_up, ffn_w_up, m_ffn_w_up, v_ffn_w_up, "adam_ffn_w_up")
    out["ffn_w_down"] = _adam_sum(r_dn, ffn_w_down, m_ffn_w_down, v_ffn_w_down, "adam_ffn_w_down")

    packed, spans = _pack_rows([small[k] for k in SMALL_ORDER])
    gathered = _all_gather_small(packed, "gather_small_grads")
    totals = dict(zip(SMALL_ORDER, _unpack_rows(_sum_slots(gathered, "sum_small_grads"), spans)))
    n_mod = L * 6 * D
    dmod_all = gathered.reshape(N_DEV, -1)[:, :n_mod].reshape(N_DEV, L, 6 * D)
    dmod_cols = jnp.transpose(lax.dynamic_slice_in_dim(dmod_all, me * n_ada, n_ada, axis=2), (1, 0, 2))
    g_ada_w = _ada_bwd(c_all.T, dmod_cols, "ada_w_grad")
    out["ada_w"] = (g_ada_w,) + tuple(_adam(g_ada_w, ada_w, m_ada_w, v_ada_w, "adam_ada_w"))

    g_small = dict(fox_b_f=totals["fox_b_f"].reshape(fox_b_f.shape), swa_sinks=totals["swa_sinks"].reshape(swa_sinks.shape),
                   ada_b=totals["dmod"].reshape(ada_b.shape), ffn_conv_b=totals["conv_b"].reshape(ffn_conv_b.shape),
                   ffn_conv_w=lax.dynamic_slice_in_dim(totals["conv_w"].reshape(L, 3, 2 * F), me * cu, cu, axis=2),
                   ln_mix_g=totals["ln_mix_g"], ln_mix_b=totals["ln_mix_b"],
                   ln_ffn_g=totals["ln_ffn_g"], ln_ffn_b=totals["ln_ffn_b"])
    small_names = ("fox_b_f", "swa_sinks", "ada_b", "ffn_conv_b", "ffn_conv_w", "ln_mix_g", "ln_mix_b", "ln_ffn_g", "ln_ffn_b")
    w_small = dict(fox_b_f=(fox_b_f, m_fox_b_f, v_fox_b_f), swa_sinks=(swa_sinks, m_swa_sinks, v_swa_sinks),
                   ada_b=(ada_b, m_ada_b, v_ada_b), ffn_conv_b=(ffn_conv_b, m_ffn_conv_b, v_ffn_conv_b),
                   ffn_conv_w=(ffn_conv_w, m_ffn_conv_w, v_ffn_conv_w),
                   ln_mix_g=(ln_mix_g, m_ln_mix_g, v_ln_mix_g), ln_mix_b=(ln_mix_b, m_ln_mix_b, v_ln_mix_b),
                   ln_ffn_g=(ln_ffn_g, m_ln_ffn_g, v_ln_ffn_g), ln_ffn_b=(ln_ffn_b, m_ln_ffn_b, v_ln_ffn_b))
    pk_g, sp = _pack_rows([g_small[k] for k in small_names])
    pk_w = _pack_rows([w_small[k][0] for k in small_names])[0]
    pk_m = _pack_rows([w_small[k][1] for k in small_names])[0]
    pk_v = _pack_rows([w_small[k][2] for k in small_names])[0]
    res = _adam(pk_g[None], pk_w[None], pk_m[None], pk_v[None], "adam_small")
    res = [dict(zip(small_names, _unpack_rows(r[0], sp))) for r in res]
    for k in small_names:
        out[k] = (g_small[k], res[0][k], res[1][k], res[2][k])

    order = ("fox_w_in", "fox_b_f", "fox_w_o", "swa_w_in", "swa_sinks", "swa_w_o", "ada_w", "ada_b", "ffn_w_up",
             "ffn_conv_w", "ffn_conv_b", "ffn_w_down", "ln_mix_g", "ln_mix_b", "ln_ffn_g", "ln_ffn_b")
    return (loss, grad_x[None], *[out[k][0] for k in order], *[out[k][1] for k in order],
            *[out[k][2] for k in order], *[out[k][3] for k in order])
```

```python
import functools

import jax
import jax.numpy as jnp
from jax import lax
from jax.experimental import pallas as pl
from jax.experimental.pallas import tpu as pltpu

F32 = jnp.float32
BF16 = jnp.bfloat16
MESH = pl.DeviceIdType.MESH
N_DEV = 8
AXES = ("x", "y", "c")

DEPTH = 2
ALPHA = (2.0 * DEPTH) ** 0.25
LN_EPS = 1e-5
FOX_HEAD_DIM = 128
SWA_HEAD_DIM = 64
SWA_GROUP = 8
SWA_WINDOW = 128
Q_BLOCK = 128
ROPE_DIM = 16
ROPE_THETA = 500000.0

ADAM_LR = 0.001
ADAM_B1 = 0.9
ADAM_B2 = 0.999
ADAM_EPS = 1e-08
ADAM_WD = 0.01
ADAM_STEP = 10

LANE = 128
MIB = 1024 * 1024


def _tile(n, pref, unit=LANE):
    if n <= pref:
        return n
    t = (pref // unit) * unit
    while t >= unit:
        if n % t == 0:
            return t
        t -= unit
    return n


def _params(sem, vmem_mib=48):
    return pltpu.CompilerParams(dimension_semantics=sem, vmem_limit_bytes=vmem_mib * MIB)


def _sigmoid(x):
    return 1.0 / (1.0 + jnp.exp(-x))


def _mm_call(dot, grid_mnk, in_specs, out_spec, out_shape, k_axis, nk, tm, tn, name, operands):
    sem = ("parallel",) * (len(grid_mnk) - 1) + ("arbitrary",)

    if nk == 1:
        def body(a_ref, b_ref, o_ref):
            o_ref[...] = dot(a_ref[...], b_ref[...]).astype(o_ref.dtype)
        scratch = []
    else:
        def body(a_ref, b_ref, o_ref, acc_ref):
            k = pl.program_id(k_axis)

            @pl.when(k == 0)
            def _():
                acc_ref[...] = jnp.zeros_like(acc_ref)

            acc_ref[...] += dot(a_ref[...], b_ref[...])

            @pl.when(k == nk - 1)
            def _():
                o_ref[...] = acc_ref[...].astype(o_ref.dtype)
        scratch = [pltpu.VMEM((tm, tn), F32)]

    return pl.pallas_call(
        body, name=name, grid=grid_mnk, in_specs=in_specs, out_specs=out_spec, out_shape=out_shape,
        scratch_shapes=scratch, compiler_params=_params(sem, 56),
    )(*operands)


def _dot(dims):
    def dot(a, b):
        return lax.dot_general(a.astype(BF16), b.astype(BF16), (dims, ((), ())), preferred_element_type=F32)
    return dot


def _mm_nn(a, b, out_dtype, name, tm=2048, tn=512, tk=2048):
    M, K = a.shape
    N = b.shape[1]
    tm, tn, tk = _tile(M, tm), _tile(N, tn), _tile(K, tk)
    nk = K // tk
    return _mm_call(
        _dot(((1,), (0,))), (M // tm, N // tn, nk),
        [pl.BlockSpec((tm, tk), lambda i, j, k: (i, k)), pl.BlockSpec((tk, tn), lambda i, j, k: (k, j))],
        pl.BlockSpec((tm, tn), lambda i, j, k: (i, j)), jax.ShapeDtypeStruct((M, N), out_dtype),
        2, nk, tm, tn, name, (a, b))


def _mm_nt(a, b, out_dtype, name, tm=2048, tn=512, tk=2048):
    P, M, K = a.shape
    N = b.shape[1]
    tm, tn, tk = _tile(M, tm), _tile(N, tn), _tile(K, tk)
    nk = K // tk
    return _mm_call(
        _dot(((1,), (1,))), (M // tm, N // tn, P * nk),
        [pl.BlockSpec((None, tm, tk), lambda i, j, k: (k // nk, i, k % nk)),
         pl.BlockSpec((None, tn, tk), lambda i, j, k: (k // nk, j, k % nk))],
        pl.BlockSpec((tm, tn), lambda i, j, k: (i, j)), jax.ShapeDtypeStruct((M, N), out_dtype),
        2, P * nk, tm, tn, name, (a, b))


def _mm_tn(a, b, out_dtype, name, tm=2048, tn=512, tk=2048):
    K, M = a.shape
    P, _, N = b.shape
    tm, tn, tk = _tile(M, tm), _tile(N, tn), _tile(K, tk)
    nk = K // tk
    return _mm_call(
        _dot(((0,), (0,))), (P, M // tm, N // tn, nk),
        [pl.BlockSpec((tk, tm), lambda p, i, j, k: (k, i)), pl.BlockSpec((None, tk, tn), lambda p, i, j, k: (p, k, j))],
        pl.BlockSpec((None, tm, tn), lambda p, i, j, k: (p, i, j)), jax.ShapeDtypeStruct((P, M, N), out_dtype),
        3, nk, tm, tn, name, (a, b))


ROW_TILE = 256


def _row_spec(tm, D):
    return pl.BlockSpec((tm, D), lambda i: (i, 0))


def _vec_spec(D):
    return pl.BlockSpec((1, D), lambda i: (0, 0))


def _modulate(x, sc, sh, name):
    S, D = x.shape
    tm = _tile(S, ROW_TILE, 8)

    def body(x_ref, sc_ref, sh_ref, h_ref):
        h_ref[...] = (x_ref[...] * (1.0 + sc_ref[...]) + sh_ref[...]).astype(BF16)

    return pl.pallas_call(
        body, name=name, grid=(S // tm,),
        in_specs=[_row_spec(tm, D), _vec_spec(D), _vec_spec(D)],
        out_specs=_row_spec(tm, D),
        out_shape=jax.ShapeDtypeStruct((S, D), BF16),
        compiler_params=_params(("parallel",)),
    )(x, sc, sh)


def _layer_norm_rows(z, gamma, beta):
    mu = jnp.mean(z, axis=-1, keepdims=True)
    zc = z - mu
    var = jnp.mean(zc * zc, axis=-1, keepdims=True)
    return zc * lax.rsqrt(var + LN_EPS) * gamma + beta


def _ln_fwd(x, y, gate, gamma, beta, sc_n, sh_n, name):
    S, D = x.shape
    tm = _tile(S, ROW_TILE, 8)

    def body(x_ref, y_ref, gate_ref, g_ref, b_ref, sc_ref, sh_ref, z_ref, xo_ref, hn_ref):
        z = ALPHA * x_ref[...] + (1.0 + gate_ref[...]) * y_ref[...]
        xo = _layer_norm_rows(z, g_ref[...], b_ref[...])
        z_ref[...] = z
        xo_ref[...] = xo
        hn_ref[...] = (xo * (1.0 + sc_ref[...]) + sh_ref[...]).astype(BF16)

    return pl.pallas_call(
        body, name=name, grid=(S // tm,),
        in_specs=[_row_spec(tm, D), _row_spec(tm, D)] + [_vec_spec(D)] * 5,
        out_specs=[_row_spec(tm, D)] * 3,
        out_shape=[jax.ShapeDtypeStruct((S, D), F32), jax.ShapeDtypeStruct((S, D), F32),
                   jax.ShapeDtypeStruct((S, D), BF16)],
        compiler_params=_params(("parallel",)),
    )(x, y, gate, gamma, beta, sc_n, sh_n)


def _ln_fwd_loss(x, y, gate, gamma, beta, target, name):
    S, D = x.shape
    tm = _tile(S, ROW_TILE, 8)

    def body(x_ref, y_ref, gate_ref, g_ref, b_ref, t_ref, z_ref, dout_ref, loss_ref):
        @pl.when(pl.program_id(0) == 0)
        def _():
            loss_ref[...] = jnp.zeros_like(loss_ref)

        z = ALPHA * x_ref[...] + (1.0 + gate_ref[...]) * y_ref[...]
        xo = _layer_norm_rows(z, g_ref[...], b_ref[...])
        err = xo - t_ref[...]
        z_ref[...] = z
        dout_ref[...] = err * (1.0 / D)
        loss_ref[...] += (0.5 / D) * jnp.sum(err * err)

    return pl.pallas_call(
        body, name=name, grid=(S // tm,),
        in_specs=[_row_spec(tm, D), _row_spec(tm, D)] + [_vec_spec(D)] * 3 + [_row_spec(tm, D)],
        out_specs=[_row_spec(tm, D), _row_spec(tm, D), pl.BlockSpec((1, LANE), lambda i: (0, 0))],
        out_shape=[jax.ShapeDtypeStruct((S, D), F32), jax.ShapeDtypeStruct((S, D), F32),
                   jax.ShapeDtypeStruct((1, LANE), F32)],
        compiler_params=_params(("arbitrary",)),
    )(x, y, gate, gamma, beta, target)


def _ln_bwd(dout, z, y, gate, gamma, name):
    S, D = z.shape
    tm = _tile(S, ROW_TILE, 8)

    def body(dout_ref, z_ref, y_ref, gate_ref, g_ref, dz_ref, dy_ref, dg_ref, db_ref, dgate_ref):
        @pl.when(pl.program_id(0) == 0)
        def _():
            dg_ref[...] = jnp.zeros_like(dg_ref)
            db_ref[...] = jnp.zeros_like(db_ref)
            dgate_ref[...] = jnp.zeros_like(dgate_ref)

        z = z_ref[...]
        dout = dout_ref[...]
        mu = jnp.mean(z, axis=-1, keepdims=True)
        zc = z - mu
        var = jnp.mean(zc * zc, axis=-1, keepdims=True)
        rstd = lax.rsqrt(var + LN_EPS)
        xhat = zc * rstd
        dxhat = dout * g_ref[...]
        m1 = jnp.mean(dxhat, axis=-1, keepdims=True)
        m2 = jnp.mean(dxhat * xhat, axis=-1, keepdims=True)
        dz = rstd * (dxhat - m1 - xhat * m2)
        dz_ref[...] = dz
        dy_ref[...] = (dz * (1.0 + gate_ref[...])).astype(BF16)
        dg_ref[...] += jnp.sum(dout * xhat, axis=0, keepdims=True)
        db_ref[...] += jnp.sum(dout, axis=0, keepdims=True)
        dgate_ref[...] += jnp.sum(dz * y_ref[...], axis=0, keepdims=True)

    return pl.pallas_call(
        body, name=name, grid=(S // tm,),
        in_specs=[_row_spec(tm, D)] * 3 + [_vec_spec(D)] * 2,
        out_specs=[_row_spec(tm, D), _row_spec(tm, D)] + [_vec_spec(D)] * 3,
        out_shape=[jax.ShapeDtypeStruct((S, D), F32), jax.ShapeDtypeStruct((S, D), BF16)]
        + [jax.ShapeDtypeStruct((1, D), F32)] * 3,
        compiler_params=_params(("arbitrary",)),
    )(dout, z, y, gate, gamma)


def _mod_bwd(dz, dh, xin, sc, name):
    S, D = dz.shape
    tm = _tile(S, ROW_TILE, 8)

    def body(dz_ref, dh_ref, x_ref, sc_ref, dx_ref, dsc_ref, dsh_ref):
        @pl.when(pl.program_id(0) == 0)
        def _():
            dsc_ref[...] = jnp.zeros_like(dsc_ref)
            dsh_ref[...] = jnp.zeros_like(dsh_ref)

        dh = dh_ref[...]
        dx_ref[...] = ALPHA * dz_ref[...] + dh * (1.0 + sc_ref[...])
        dsc_ref[...] += jnp.sum(dh * x_ref[...], axis=0, keepdims=True)
        dsh_ref[...] += jnp.sum(dh, axis=0, keepdims=True)

    return pl.pallas_call(
        body, name=name, grid=(S // tm,),
        in_specs=[_row_spec(tm, D)] * 3 + [_vec_spec(D)],
        out_specs=[_row_spec(tm, D), _vec_spec(D), _vec_spec(D)],
        out_shape=[jax.ShapeDtypeStruct((S, D), F32)] + [jax.ShapeDtypeStruct((1, D), F32)] * 2,
        compiler_params=_params(("arbitrary",)),
    )(dz, dh, xin, sc)


def _shift_down(u, k, row):
    return jnp.where(row >= k, pltpu.roll(u, k, axis=0), 0.0)


def _shift_up(u, k, row, S):
    return jnp.where(row < S - k, pltpu.roll(u, S - k, axis=0), 0.0)


def _ffn_up(h, w, cw, cb, name):
    S, D = h.shape
    F = w.shape[2]
    tn = _tile(F, 256)

    def body(h_ref, w_ref, cw_ref, cb_ref, u_ref, a_ref):
        hh = h_ref[...]
        row = lax.broadcasted_iota(jnp.int32, (S, tn), 0)
        conv = []
        for p in range(2):
            u = jnp.dot(hh, w_ref[p], preferred_element_type=F32)
            u_ref[p] = u
            cwp = cw_ref[p]
            conv.append(_shift_down(u, 2, row) * cwp[0:1] + _shift_down(u, 1, row) * cwp[1:2]
                        + u * cwp[2:3] + cb_ref[p])
        g, v = conv
        a_ref[...] = (g * _sigmoid(g) * v).astype(BF16)

    return pl.pallas_call(
        body, name=name, grid=(F // tn,),
        in_specs=[pl.BlockSpec((S, D), lambda j: (0, 0)), pl.BlockSpec((2, D, tn), lambda j: (0, 0, j)),
                  pl.BlockSpec((2, 3, tn), lambda j: (0, 0, j)), pl.BlockSpec((2, 1, tn), lambda j: (0, 0, j))],
        out_specs=[pl.BlockSpec((2, S, tn), lambda j: (0, 0, j)), pl.BlockSpec((S, tn), lambda j: (0, j))],
        out_shape=[jax.ShapeDtypeStruct((2, S, F), F32), jax.ShapeDtypeStruct((S, F), BF16)],
        compiler_params=_params(("parallel",), 56),
    )(h, w, cw, cb)


def _ffn_bwd_elem(da, u, cw, cb, name):
    _, S, F = u.shape
    tn = _tile(F, 256)

    def body(da_ref, u_ref, cw_ref, cb_ref, du_ref, dcw_ref, dcb_ref):
        row = lax.broadcasted_iota(jnp.int32, (S, tn), 0)
        da = da_ref[...]
        shifted, conv = [], []
        for p in range(2):
            u = u_ref[p]
            u1, u2 = _shift_down(u, 1, row), _shift_down(u, 2, row)
            cwp = cw_ref[p]
            shifted.append((u2, u1, u))
            conv.append(u2 * cwp[0:1] + u1 * cwp[1:2] + u * cwp[2:3] + cb_ref[p])
        g, v = conv
        sg = _sigmoid(g)
        d_conv = (da * v * (sg * (1.0 + g * (1.0 - sg))), da * (g * sg))
        for p in range(2):
            d = d_conv[p]
            cwp = cw_ref[p]
            dcb_ref[p] = jnp.sum(d, axis=0, keepdims=True)
            for j in range(3):
                dcw_ref[p, j:j + 1, :] = jnp.sum(d * shifted[p][j], axis=0, keepdims=True)
            du = d * cwp[2:3] + _shift_up(d, 1, row, S) * cwp[1:2] + _shift_up(d, 2, row, S) * cwp[0:1]
            du_ref[p] = du.astype(BF16)

    return pl.pallas_call(
        body, name=name, grid=(F // tn,),
        in_specs=[pl.BlockSpec((S, tn), lambda j: (0, j)), pl.BlockSpec((2, S, tn), lambda j: (0, 0, j)),
                  pl.BlockSpec((2, 3, tn), lambda j: (0, 0, j)), pl.BlockSpec((2, 1, tn), lambda j: (0, 0, j))],
        out_specs=[pl.BlockSpec((2, S, tn), lambda j: (0, 0, j)), pl.BlockSpec((2, 3, tn), lambda j: (0, 0, j)),
                   pl.BlockSpec((2, 1, tn), lambda j: (0, 0, j))],
        out_shape=[jax.ShapeDtypeStruct((2, S, F), BF16), jax.ShapeDtypeStruct((2, 3, F), F32),
                   jax.ShapeDtypeStruct((2, 1, F), F32)],
        compiler_params=_params(("parallel",), 56),
    )(da, u, cw, cb)


def _split3(x):
    hi = x.astype(BF16)
    r1 = x - hi.astype(F32)
    mid = r1.astype(BF16)
    lo = (r1 - mid.astype(F32)).astype(BF16)
    return hi, mid, lo


def _tri_matmul(x, upper, S):
    tc = _tile(S, 512)
    parts = _split3(x)
    outs = []
    for b in range(S // tc):
        r = lax.broadcasted_iota(jnp.int32, (S, tc), 0)
        c = lax.broadcasted_iota(jnp.int32, (S, tc), 1) + b * tc
        tri = jnp.where((r <= c) if upper else (r >= c), 1.0, 0.0).astype(BF16)
        acc = jnp.dot(parts[0], tri, preferred_element_type=F32)
        acc += jnp.dot(parts[1], tri, preferred_element_type=F32)
        acc += jnp.dot(parts[2], tri, preferred_element_type=F32)
        outs.append(acc)
    return outs, tc


def _fox_prep(flT, bf, name):
    H, S = flT.shape

    def body(fl_ref, b_ref, cum_ref):
        zz = fl_ref[...] + b_ref[...]
        lf = jnp.minimum(zz, 0.0) - jnp.log(1.0 + jnp.exp(-jnp.abs(zz)))
        outs, tc = _tri_matmul(lf, True, S)
        for b, o in enumerate(outs):
            cum_ref[:, b * tc:(b + 1) * tc] = o

    return pl.pallas_call(
        body, name=name,
        in_specs=[pl.BlockSpec(memory_space=pltpu.VMEM)] * 2,
        out_specs=pl.BlockSpec(memory_space=pltpu.VMEM),
        out_shape=jax.ShapeDtypeStruct((H, S), F32),
        compiler_params=pltpu.CompilerParams(vmem_limit_bytes=48 * MIB),
    )(flT, bf)


def _fox_prep_bwd(dcumT, flT, bf, name):
    H, S = flT.shape

    def body(dc_ref, fl_ref, b_ref, dfl_ref, dbf_ref):
        zz = fl_ref[...] + b_ref[...]
        outs, tc = _tri_matmul(dc_ref[...], False, S)
        total = jnp.zeros((H, 1), F32)
        for b, o in enumerate(outs):
            dfl = o * _sigmoid(-zz[:, b * tc:(b + 1) * tc])
            dfl_ref[:, b * tc:(b + 1) * tc] = dfl
            total += jnp.sum(dfl, axis=1, keepdims=True)
        dbf_ref[...] = total

    return pl.pallas_call(
        body, name=name,
        in_specs=[pl.BlockSpec(memory_space=pltpu.VMEM)] * 3,
        out_specs=[pl.BlockSpec(memory_space=pltpu.VMEM)] * 2,
        out_shape=[jax.ShapeDtypeStruct((H, S), F32), jax.ShapeDtypeStruct((H, 1), F32)],
        compiler_params=pltpu.CompilerParams(vmem_limit_bytes=48 * MIB),
    )(dcumT, flT, bf)


FOX_TQ = 256


def _fox_scores(q_ref, k_ref, cq_ref, ck_ref, i, tq, S, scale):
    q = q_ref[...].astype(BF16)
    k = k_ref[...].astype(BF16)
    s = lax.dot_general(q, k, (((1,), (1,)), ((), ())), preferred_element_type=F32) * scale
    s = s + cq_ref[...] - ck_ref[...]
    qpos = i * tq + lax.broadcasted_iota(jnp.int32, (tq, S), 0)
    kpos = lax.broadcasted_iota(jnp.int32, (tq, S), 1)
    return jnp.where(kpos <= qpos, s, -jnp.inf), q, k


def _fox_attn_fwd(proj, cum_col, cum_row, H, name):
    S = proj.shape[0]
    dh = FOX_HEAD_DIM
    tq = _tile(S, FOX_TQ)
    scale = dh ** -0.5

    def body(q_ref, k_ref, v_ref, cq_ref, ck_ref, o_ref, lse_ref):
        s, _, _ = _fox_scores(q_ref, k_ref, cq_ref, ck_ref, pl.program_id(1), tq, S, scale)
        m = jnp.max(s, axis=-1, keepdims=True)
        p = jnp.exp(s - m)
        l = jnp.sum(p, axis=-1, keepdims=True)
        o = jnp.dot(p.astype(BF16), v_ref[...].astype(BF16), preferred_element_type=F32) / l
        o_ref[...] = o.astype(BF16)
        lse_ref[...] = m + jnp.log(l)

    return pl.pallas_call(
        body, name=name, grid=(H, S // tq),
        in_specs=[pl.BlockSpec((tq, dh), lambda h, i: (i, h)),
                  pl.BlockSpec((S, dh), lambda h, i: (0, H + h)),
                  pl.BlockSpec((S, dh), lambda h, i: (0, 2 * H + h)),
                  pl.BlockSpec((None, tq, 1), lambda h, i: (h, i, 0)),
                  pl.BlockSpec((None, 1, S), lambda h, i: (h, 0, 0))],
        out_specs=[pl.BlockSpec((tq, dh), lambda h, i: (i, h)),
                   pl.BlockSpec((None, tq, 1), lambda h, i: (h, i, 0))],
        out_shape=[jax.ShapeDtypeStruct((S, H * dh), BF16), jax.ShapeDtypeStruct((H, S, 1), F32)],
        compiler_params=_params(("parallel", "parallel")),
    )(proj, proj, proj, cum_col, cum_row)


def _fox_attn_bwd(proj, do, cum_col, cum_row, lse, H, name):
    S = proj.shape[0]
    dh = FOX_HEAD_DIM
    tq = _tile(S, FOX_TQ)
    scale = dh ** -0.5

    def body(q_ref, k_ref, v_ref, do_ref, cq_ref, ck_ref, lse_ref, dq_ref, dk_ref, dv_ref, dck_ref):
        i = pl.program_id(1)

        @pl.when(i == 0)
        def _():
            dk_ref[...] = jnp.zeros_like(dk_ref)
            dv_ref[...] = jnp.zeros_like(dv_ref)
            dck_ref[...] = jnp.zeros_like(dck_ref)

        s, q, k = _fox_scores(q_ref, k_ref, cq_ref, ck_ref, i, tq, S, scale)
        p = jnp.exp(s - lse_ref[...])
        do_b = do_ref[...].astype(BF16)
        dp = lax.dot_general(do_b, v_ref[...].astype(BF16), (((1,), (1,)), ((), ())), preferred_element_type=F32)
        delta = jnp.sum(p * dp, axis=-1, keepdims=True)
        ds = p * (dp - delta)
        ds_b = ds.astype(BF16)
        dq_ref[...] = (jnp.dot(ds_b, k, preferred_element_type=F32) * scale).astype(BF16)
        dk_ref[...] += lax.dot_general(ds_b, q, (((0,), (0,)), ((), ())), preferred_element_type=F32) * scale
        dv_ref[...] += lax.dot_general(p.astype(BF16), do_b, (((0,), (0,)), ((), ())), preferred_element_type=F32)
        dck_ref[...] -= jnp.sum(ds, axis=0, keepdims=True)

    W = H * dh
    return pl.pallas_call(
        body, name=name, grid=(H, S // tq),
        in_specs=[pl.BlockSpec((tq, dh), lambda h, i: (i, h)),
                  pl.BlockSpec((S, dh), lambda h, i: (0, H + h)),
                  pl.BlockSpec((S, dh), lambda h, i: (0, 2 * H + h)),
                  pl.BlockSpec((tq, dh), lambda h, i: (i, h)),
                  pl.BlockSpec((None, tq, 1), lambda h, i: (h, i, 0)),
                  pl.BlockSpec((None, 1, S), lambda h, i: (h, 0, 0)),
                  pl.BlockSpec((None, tq, 1), lambda h, i: (h, i, 0))],
        out_specs=[pl.BlockSpec((tq, dh), lambda h, i: (i, h)),
                   pl.BlockSpec((S, dh), lambda h, i: (0, h)),
                   pl.BlockSpec((S, dh), lambda h, i: (0, h)),
                   pl.BlockSpec((None, 1, S), lambda h, i: (h, 0, 0))],
        out_shape=[jax.ShapeDtypeStruct((S, W), BF16), jax.ShapeDtypeStruct((S, W), F32),
                   jax.ShapeDtypeStruct((S, W), F32), jax.ShapeDtypeStruct((H, 1, S), F32)],
        compiler_params=_params(("parallel", "arbitrary")),
    )(proj, proj, proj, do, cum_col, cum_row, lse)


def _rope(parts, tabs, out_dtype, name):
    S = parts[0][0].shape[0]
    widths = [a.shape[1] for a, _ in parts]
    total = sum(widths)
    tm = _tile(S, ROW_TILE, 8)
    flags = [r for _, r in parts]

    def body(*refs):
        in_refs = refs[:len(parts)]
        cos_ref, sa_ref, sb_ref, o_ref = refs[len(parts):]
        cos, sa, sb = cos_ref[...], sa_ref[...], sb_ref[...]
        off = 0
        for ref, rot, w in zip(in_refs, flags, widths):
            for j in range(w // LANE):
                t = ref[:, j * LANE:(j + 1) * LANE]
                if rot:
                    t = t * cos + pltpu.roll(t, LANE - ROPE_DIM // 2, axis=1) * sa + pltpu.roll(t, ROPE_DIM // 2, axis=1) * sb
                o_ref[:, off + j * LANE:off + (j + 1) * LANE] = t.astype(o_ref.dtype)
            off += w

    return pl.pallas_call(
        body, name=name, grid=(S // tm,),
        in_specs=[pl.BlockSpec((tm, w), lambda i: (i, 0)) for w in widths] + [_row_spec(tm, LANE)] * 3,
        out_specs=_row_spec(tm, total),
        out_shape=jax.ShapeDtypeStruct((S, total), out_dtype),
        compiler_params=_params(("parallel",)),
    )(*[a for a, _ in parts], *tabs)


def _swa_band(ref_p, ref_c, hk):
    dh = SWA_HEAD_DIM
    return jnp.concatenate([ref_p[:, hk * dh:(hk + 1) * dh], ref_c[:, hk * dh:(hk + 1) * dh]], axis=0).astype(BF16)


def _swa_mask(n):
    qi = lax.broadcasted_iota(jnp.int32, (Q_BLOCK, 2 * Q_BLOCK), 0)
    kj = lax.broadcasted_iota(jnp.int32, (Q_BLOCK, 2 * Q_BLOCK), 1)
    rel = qi + Q_BLOCK - kj
    return (rel >= 0) & (rel < SWA_WINDOW) & ((kj >= Q_BLOCK) | (n > 0))


def _swa_attn_fwd(qk, proj, sinks, Hq, name):
    S = qk.shape[0]
    dh, G, QB = SWA_HEAD_DIM, SWA_GROUP, Q_BLOCK
    Hk = Hq // G
    Wq, Wk = Hq * dh, Hk * dh
    nb = S // QB
    scale = dh ** -0.5

    def body(q_ref, kp_ref, kc_ref, vp_ref, vc_ref, sink_ref, o_ref, lse_ref):
        n = pl.program_id(0)
        mask = _swa_mask(n)
        lane = lax.broadcasted_iota(jnp.int32, (QB, LANE), 1)
        lse_tile = jnp.zeros((QB, LANE), F32)
        for hk in range(Hk):
            kb = _swa_band(kp_ref, kc_ref, hk)
            vb = _swa_band(vp_ref, vc_ref, hk)
            outs = []
            for g in range(G):
                h = hk * G + g
                q = q_ref[:, h * dh:(h + 1) * dh].astype(BF16)
                s = lax.dot_general(q, kb, (((1,), (1,)), ((), ())), preferred_element_type=F32) * scale
                s = jnp.where(mask, s, -jnp.inf)
                sk = sink_ref[0:1, h:h + 1]
                m = jnp.maximum(jnp.max(s, axis=-1, keepdims=True), sk)
                p = jnp.exp(s - m)
                l = jnp.sum(p, axis=-1, keepdims=True) + jnp.exp(sk - m)
                outs.append(jnp.dot(p.astype(BF16), vb, preferred_element_type=F32) / l)
                lse_tile = jnp.where(lane == h, m + jnp.log(l), lse_tile)
            for g in range(0, G, 2):
                c0 = (hk * G + g) * dh
                o_ref[:, c0:c0 + 2 * dh] = jnp.concatenate([outs[g], outs[g + 1]], axis=1).astype(BF16)
        lse_ref[...] = lse_tile

    kcol, vcol = Wq // Wk, (Wq + Wk) // Wk
    return pl.pallas_call(
        body, name=name, grid=(nb,),
        in_specs=[pl.BlockSpec((QB, Wq), lambda n: (n, 0)),
                  pl.BlockSpec((QB, Wk), lambda n: (jnp.maximum(n - 1, 0), kcol)),
                  pl.BlockSpec((QB, Wk), lambda n: (n, kcol)),
                  pl.BlockSpec((QB, Wk), lambda n: (jnp.maximum(n - 1, 0), vcol)),
                  pl.BlockSpec((QB, Wk), lambda n: (n, vcol)),
                  pl.BlockSpec((1, LANE), lambda n: (0, 0))],
        out_specs=[pl.BlockSpec((QB, Wq), lambda n: (n, 0)), pl.BlockSpec((QB, LANE), lambda n: (n, 0))],
        out_shape=[jax.ShapeDtypeStruct((S, Wq), BF16), jax.ShapeDtypeStruct((S, LANE), F32)],
        compiler_params=_params(("parallel",)),
    )(qk, qk, qk, proj, proj, sinks)


def _swa_attn_bwd(qk, proj, sinks, do, lse, Hq, name):
    S = qk.shape[0]
    dh, G, QB = SWA_HEAD_DIM, SWA_GROUP, Q_BLOCK
    Hk = Hq // G
    Wq, Wk = Hq * dh, Hk * dh
    nb = S // QB
    scale = dh ** -0.5

    def body(q_ref, kp_ref, kc_ref, vp_ref, vc_ref, sink_ref, do_ref, lse_ref,
             dq_ref, dk_ref, dv_ref, dsink_ref, carry_k, carry_v):
        n = pl.program_id(0)

        @pl.when(n == 0)
        def _():
            dsink_ref[...] = jnp.zeros_like(dsink_ref)

        @pl.when(n < nb)
        def _():
            mask = _swa_mask(n)
            lane = lax.broadcasted_iota(jnp.int32, (1, LANE), 1)
            dsink = jnp.zeros((1, LANE), F32)
            dk_heads, dv_heads = [], []
            for hk in range(Hk):
                kb = _swa_band(kp_ref, kc_ref, hk)
                vb = _swa_band(vp_ref, vc_ref, hk)
                dkb = jnp.zeros((2 * QB, dh), F32)
                dvb = jnp.zeros((2 * QB, dh), F32)
                dqs = []
                for g in range(G):
                    h = hk * G + g
                    q = q_ref[:, h * dh:(h + 1) * dh].astype(BF16)
                    do_h = do_ref[:, h * dh:(h + 1) * dh].astype(BF16)
                    lse_h = lse_ref[:, h:h + 1]
                    s = lax.dot_general(q, kb, (((1,), (1,)), ((), ())), preferred_element_type=F32) * scale
                    s = jnp.where(mask, s, -jnp.inf)
                    p = jnp.exp(s - lse_h)
                    p_sink = jnp.exp(sink_ref[0:1, h:h + 1] - lse_h)
                    dp = lax.dot_general(do_h, vb, (((1,), (1,)), ((), ())), preferred_element_type=F32)
                    delta = jnp.sum(p * dp, axis=-1, keepdims=True)
                    ds_b = (p * (dp - delta)).astype(BF16)
                    dqs.append(jnp.dot(ds_b, kb, preferred_element_type=F32) * scale)
                    dkb += lax.dot_general(ds_b, q, (((0,), (0,)), ((), ())), preferred_element_type=F32) * scale
                    dvb += lax.dot_general(p.astype(BF16), do_h, (((0,), (0,)), ((), ())), preferred_element_type=F32)
                    dsink = jnp.where(lane == h, -jnp.sum(p_sink * delta, axis=0, keepdims=True), dsink)
                for g in range(0, G, 2):
                    c0 = (hk * G + g) * dh
                    dq_ref[:, c0:c0 + 2 * dh] = jnp.concatenate([dqs[g], dqs[g + 1]], axis=1)
                dk_heads.append(dkb)
                dv_heads.append(dvb)
            dsink_ref[...] += dsink
            dk_all = jnp.concatenate(dk_heads, axis=1)
            dv_all = jnp.concatenate(dv_heads, axis=1)

            @pl.when(n > 0)
            def _():
                dk_ref[...] = carry_k[...] + dk_all[:QB]
                dv_ref[...] = carry_v[...] + dv_all[:QB]

            carry_k[...] = dk_all[QB:]
            carry_v[...] = dv_all[QB:]

        @pl.when(n == nb)
        def _():
            dk_ref[...] = carry_k[...]
            dv_ref[...] = carry_v[...]

    kcol, vcol = Wq // Wk, (Wq + Wk) // Wk
    cur = lambda n: jnp.minimum(n, nb - 1)
    prev = lambda n: jnp.maximum(jnp.minimum(n, nb - 1) - 1, 0)
    return pl.pallas_call(
        body, name=name, grid=(nb + 1,),
        in_specs=[pl.BlockSpec((QB, Wq), lambda n: (cur(n), 0)),
                  pl.BlockSpec((QB, Wk), lambda n: (prev(n), kcol)),
                  pl.BlockSpec((QB, Wk), lambda n: (cur(n), kcol)),
                  pl.BlockSpec((QB, Wk), lambda n: (prev(n), vcol)),
                  pl.BlockSpec((QB, Wk), lambda n: (cur(n), vcol)),
                  pl.BlockSpec((1, LANE), lambda n: (0, 0)),
                  pl.BlockSpec((QB, Wq), lambda n: (cur(n), 0)),
                  pl.BlockSpec((QB, LANE), lambda n: (cur(n), 0))],
        out_specs=[pl.BlockSpec((QB, Wq), lambda n: (cur(n), 0)),
                   pl.BlockSpec((QB, Wk), lambda n: (jnp.maximum(n - 1, 0), 0)),
                   pl.BlockSpec((QB, Wk), lambda n: (jnp.maximum(n - 1, 0), 0)),
                   pl.BlockSpec((1, LANE), lambda n: (0, 0))],
        out_shape=[jax.ShapeDtypeStruct((S, Wq), F32), jax.ShapeDtypeStruct((S, Wk), F32),
                   jax.ShapeDtypeStruct((S, Wk), F32), jax.ShapeDtypeStruct((1, LANE), F32)],
        scratch_shapes=[pltpu.VMEM((QB, Wk), F32), pltpu.VMEM((QB, Wk), F32)],
        compiler_params=_params(("arbitrary",)),
    )(qk, qk, qk, proj, proj, sinks, do, lse)


ADA_ROWS = 16


def _ada_mod(c_pad, w, b, name):
    L, D, N = w.shape
    tn = _tile(N, 512)

    def body(c_ref, w_ref, b_ref, o_ref):
        c = c_ref[...]
        c = c * _sigmoid(c)
        ch = c.astype(BF16)
        cl = (c - ch.astype(F32)).astype(BF16)
        ww = w_ref[...]
        wh = ww.astype(BF16)
        wl = (ww - wh.astype(F32)).astype(BF16)
        acc = jnp.dot(ch, wh, preferred_element_type=F32)
        acc += jnp.dot(ch, wl, preferred_element_type=F32)
        acc += jnp.dot(cl, wh, preferred_element_type=F32)
        o_ref[...] = acc + b_ref[...]

    return pl.pallas_call(
        body, name=name, grid=(L, N // tn),
        in_specs=[pl.BlockSpec((ADA_ROWS, D), lambda l, j: (0, 0)),
                  pl.BlockSpec((None, D, tn), lambda l, j: (l, 0, j)),
                  pl.BlockSpec((None, 1, tn), lambda l, j: (l, 0, j))],
        out_specs=pl.BlockSpec((None, ADA_ROWS, tn), lambda l, j: (l, 0, j)),
        out_shape=jax.ShapeDtypeStruct((L, ADA_ROWS, N), F32),
        compiler_params=_params(("parallel", "parallel")),
    )(c_pad, w, b)


def _ada_bwd(cT, dm, name):
    D = cT.shape[0]
    L, B, N = dm.shape
    tm = _tile(D, 256)

    def body(c_ref, dm_ref, o_ref):
        c = c_ref[...]
        c = c * _sigmoid(c)
        dmv = dm_ref[...]
        acc = c[:, 0:1] * dmv[0:1, :]
        for b in range(1, B):
            acc += c[:, b:b + 1] * dmv[b:b + 1, :]
        o_ref[...] = acc

    return pl.pallas_call(
        body, name=name, grid=(L, D // tm),
        in_specs=[pl.BlockSpec((tm, B), lambda l, i: (i, 0)), pl.BlockSpec((None, B, N), lambda l, i: (l, 0, 0))],
        out_specs=pl.BlockSpec((None, tm, N), lambda l, i: (l, i, 0)),
        out_shape=jax.ShapeDtypeStruct((L, D, N), F32),
        compiler_params=_params(("parallel", "parallel")),
    )(cT, dm)


def _adamw_math(w, g, m, v):
    m = ADAM_B1 * m + (1.0 - ADAM_B1) * g
    v = ADAM_B2 * v + (1.0 - ADAM_B2) * (g * g)
    m_hat = m / (1.0 - ADAM_B1 ** ADAM_STEP)
    v_hat = v / (1.0 - ADAM_B2 ** ADAM_STEP)
    delta = -ADAM_LR * (m_hat / (jnp.sqrt(v_hat) + ADAM_EPS) + ADAM_WD * w)
    return delta, m, v


def _adam_rows(R, C):
    lanes = -(-C // LANE) * LANE
    return _tile(R, max(8, (262144 // lanes) // 8 * 8), 8)


def _adam_sum(recv, w, m, v, name):
    P, L, R, C = recv.shape
    tr = _adam_rows(R, C)

    def body(r_ref, w_ref, m_ref, v_ref, g_ref, d_ref, mo_ref, vo_ref):
        g = r_ref[0].astype(F32)
        for p in range(1, P):
            g = g + r_ref[p].astype(F32)
        delta, mn, vn = _adamw_math(w_ref[...], g, m_ref[...], v_ref[...])
        g_ref[...] = g
        d_ref[...] = delta
        mo_ref[...] = mn
        vo_ref[...] = vn

    spec = pl.BlockSpec((None, tr, C), lambda l, i: (l, i, 0))
    return pl.pallas_call(
        body, name=name, grid=(L, R // tr),
        in_specs=[pl.BlockSpec((P, None, tr, C), lambda l, i: (0, l, i, 0)), spec, spec, spec],
        out_specs=[spec] * 4,
        out_shape=[jax.ShapeDtypeStruct((L, R, C), F32)] * 4,
        compiler_params=_params(("parallel", "parallel")),
    )(recv, w, m, v)


def _adam(g, w, m, v, name):
    L, R, C = w.shape
    tr = _adam_rows(R, C)

    def body(g_ref, w_ref, m_ref, v_ref, d_ref, mo_ref, vo_ref):
        delta, mn, vn = _adamw_math(w_ref[...], g_ref[...], m_ref[...], v_ref[...])
        d_ref[...] = delta
        mo_ref[...] = mn
        vo_ref[...] = vn

    spec = pl.BlockSpec((None, tr, C), lambda l, i: (l, i, 0))
    return pl.pallas_call(
        body, name=name, grid=(L, R // tr),
        in_specs=[spec] * 4, out_specs=[spec] * 3,
        out_shape=[jax.ShapeDtypeStruct((L, R, C), F32)] * 3,
        compiler_params=_params(("parallel", "parallel")),
    )(g, w, m, v)


def _sum_slots(x, name):
    P, R, C = x.shape

    def body(x_ref, o_ref):
        acc = x_ref[0]
        for p in range(1, P):
            acc = acc + x_ref[p]
        o_ref[...] = acc

    return pl.pallas_call(
        body, name=name,
        in_specs=[pl.BlockSpec(memory_space=pltpu.VMEM)], out_specs=pl.BlockSpec(memory_space=pltpu.VMEM),
        out_shape=jax.ShapeDtypeStruct((R, C), F32),
        compiler_params=pltpu.CompilerParams(vmem_limit_bytes=48 * MIB),
    )(x)


def _my_pos():
    return lax.axis_index("x"), lax.axis_index("y"), lax.axis_index("c")


def _all_gather_small(x, name):
    R, C = x.shape

    def body(x_ref, out_ref, send_sems, recv_sems):
        x_, y_, c_ = _my_pos()
        me, sibling = (x_, y_, c_), (x_, y_, 1 - c_)
        chips = [(1 - x_, y_), (x_, 1 - y_), (1 - x_, 1 - y_)]

        def slot(px, py, pc):
            return out_ref.at[4 * px + 2 * py + pc]

        def copy(k, block, to):
            return pltpu.make_async_remote_copy(
                src_ref=slot(*block), dst_ref=slot(*block), send_sem=send_sems.at[k], recv_sem=recv_sems.at[k],
                device_id=to, device_id_type=MESH)

        out_ref[4 * x_ + 2 * y_ + c_] = x_ref[...]
        first = [copy(0, me, sibling)] + [copy(1 + j, me, (*chip, c_)) for j, chip in enumerate(chips)]
        for cp in first:
            cp.start()
        passed = [copy(4 + j, (*chip, c_), sibling) for j, chip in enumerate(chips)]
        for j, chip in enumerate(chips):
            copy(1 + j, (*chip, c_), me).wait_recv()
            passed[j].start()
        copy(0, sibling, me).wait_recv()
        for j, chip in enumerate(chips):
            copy(4 + j, (*chip, 1 - c_), me).wait_recv()
        for cp in first + passed:
            cp.wait_send()

    return pl.pallas_call(
        body, name=name,
        in_specs=[pl.BlockSpec(memory_space=pltpu.VMEM)], out_specs=pl.BlockSpec(memory_space=pltpu.VMEM),
        out_shape=jax.ShapeDtypeStruct((N_DEV, R, C), x.dtype),
        scratch_shapes=[pltpu.SemaphoreType.DMA((7,)), pltpu.SemaphoreType.DMA((7,))],
        compiler_params=pltpu.CompilerParams(vmem_limit_bytes=48 * MIB),
    )(x)


N_BIG = 8


def _all_gather_weights(fox_in, fox_o, swa_in, swa_o, up, down, name):
    L, D, cu = up.shape
    F = 4 * cu
    rd = down.shape[1]
    ro = fox_o.shape[0]

    def body(fi_s, fo_s, si_s, so_s, up_s, dn_s, fi_g, fo_g, si_g, so_g, up_g, dn_g, send_sems, recv_sems, local_sems):
        x_, y_, c_ = _my_pos()
        me, sibling = (x_, y_, c_), (x_, y_, 1 - c_)
        chips = [(1 - x_, y_), (x_, 1 - y_), (1 - x_, 1 - y_)]

        def idx(px, py, pc):
            return 4 * px + 2 * py + pc

        def rows(n, k):
            return pl.ds(pl.multiple_of(k * n, n), n)

        srcs = [fi_s, fo_s, si_s, so_s, up_s.at[0], up_s.at[1], dn_s.at[0], dn_s.at[1]]
        dsts = [
            lambda b: fi_g.at[idx(*b)],
            lambda b: fo_g.at[rows(ro, idx(*b)), :],
            lambda b: si_g.at[idx(*b)],
            lambda b: so_g.at[rows(ro, idx(*b)), :],
            lambda b: up_g.at[0, b[0], :, rows(cu, 2 * b[1] + b[2])],
            lambda b: up_g.at[1, b[0], :, rows(cu, 2 * b[1] + b[2])],
            lambda b: dn_g.at[0, rows(rd, idx(*b)), :],
            lambda b: dn_g.at[1, rows(rd, idx(*b)), :],
        ]

        def copy(e, k, block, to, own=False):
            return pltpu.make_async_remote_copy(
                src_ref=srcs[e] if own else dsts[e](block), dst_ref=dsts[e](block),
                send_sem=send_sems.at[e, k], recv_sem=recv_sems.at[e, k], device_id=to, device_id_type=MESH)

        mine = [pltpu.make_async_copy(srcs[e], dsts[e](me), local_sems.at[e]) for e in range(N_BIG)]
        for cp in mine:
            cp.start()
        first = []
        for e in range(N_BIG):
            first.append(copy(e, 0, me, sibling, own=True))
            first += [copy(e, 1 + j, me, (*chip, c_), own=True) for j, chip in enumerate(chips)]
        for cp in first:
            cp.start()
        passed = []
        for e in range(N_BIG):
            for j, chip in enumerate(chips):
                copy(e, 1 + j, (*chip, c_), me).wait_recv()
                fwd = copy(e, 4 + j, (*chip, c_), sibling)
                fwd.start()
                passed.append(fwd)
        for e in range(N_BIG):
            copy(e, 0, sibling, me).wait_recv()
            for j, chip in enumerate(chips):
                copy(e, 4 + j, (*chip, 1 - c_), me).wait_recv()
        for cp in first + passed:
            cp.wait_send()
        for cp in mine:
            cp.wait()

    any_spec = pl.BlockSpec(memory_space=pl.ANY)
    return pl.pallas_call(
        body, name=name,
        in_specs=[any_spec] * 6, out_specs=[any_spec] * 6,
        out_shape=[jax.ShapeDtypeStruct((N_DEV,) + fox_in.shape, BF16), jax.ShapeDtypeStruct((N_DEV * ro, D), BF16),
                   jax.ShapeDtypeStruct((N_DEV,) + swa_in.shape, BF16), jax.ShapeDtypeStruct((N_DEV * ro, D), BF16),
                   jax.ShapeDtypeStruct((L, 2, D, F), BF16), jax.ShapeDtypeStruct((L, N_DEV * rd, D), BF16)],
        scratch_shapes=[pltpu.SemaphoreType.DMA((N_BIG, 7)), pltpu.SemaphoreType.DMA((N_BIG, 7)),
                        pltpu.SemaphoreType.DMA((N_BIG,))],
    )(fox_in, fox_o, swa_in, swa_o, up, down)


def _reduce_scatter_grads(d_fi, d_fo, d_si, d_so, d_up0, d_up1, d_dn0, d_dn1, name):
    D = d_fo.shape[0]
    ro = D // N_DEV
    F = d_dn0.shape[0]
    rd = F // N_DEV
    cu = F // 4

    def body(fi, fo, si, so, up0, up1, dn0, dn1, r_fi, r_fo, r_si, r_so, r_up, r_dn, send_sems, recv_sems):
        x_, y_, c_ = _my_pos()
        me = 4 * x_ + 2 * y_ + c_

        def src(e, j):
            return [fi.at[j], fo.at[pl.ds(j * ro, ro), :], si.at[j], so.at[pl.ds(j * ro, ro), :],
                    up0.at[j // 4, :, pl.ds((j % 4) * cu, cu)], up1.at[j // 4, :, pl.ds((j % 4) * cu, cu)],
                    dn0.at[pl.ds(j * rd, rd), :], dn1.at[pl.ds(j * rd, rd), :]][e]

        def dst(e, i):
            return [r_fi.at[i, 0], r_fo.at[i, 0], r_si.at[i, 0], r_so.at[i, 0],
                    r_up.at[i, 0], r_up.at[i, 1], r_dn.at[i, 0], r_dn.at[i, 1]][e]

        def remote(e, j):
            return pltpu.make_async_remote_copy(
                src_ref=src(e, j), dst_ref=dst(e, me), send_sem=send_sems.at[e, j], recv_sem=recv_sems.at[e, me],
                device_id=(j // 4, (j // 2) % 2, j % 2), device_id_type=MESH)

        def local(e, j):
            return pltpu.make_async_copy(src(e, j), dst(e, j), recv_sems.at[e, j])

        def arrival(e, i):
            return pltpu.make_async_remote_copy(
                src_ref=src(e, i), dst_ref=dst(e, i), send_sem=send_sems.at[e, i], recv_sem=recv_sems.at[e, i],
                device_id=(i // 4, (i // 2) % 2, i % 2), device_id_type=MESH)

        for e in range(N_BIG):
            for j in range(N_DEV):
                @pl.when(me == j)
                def _():
                    local(e, j).start()

                @pl.when(me != j)
                def _():
                    remote(e, j).start()
        for e in range(N_BIG):
            for i in range(N_DEV):
                @pl.when(me == i)
                def _():
                    local(e, i).wait()

                @pl.when(me != i)
                def _():
                    arrival(e, i).wait_recv()
        for e in range(N_BIG):
            for j in range(N_DEV):
                @pl.when(me != j)
                def _():
                    remote(e, j).wait_send()

    any_spec = pl.BlockSpec(memory_space=pl.ANY)
    return pl.pallas_call(
        body, name=name,
        in_specs=[any_spec] * 8, out_specs=[any_spec] * 6,
        out_shape=[jax.ShapeDtypeStruct((N_DEV, 1) + d_fi.shape[1:], BF16), jax.ShapeDtypeStruct((N_DEV, 1, ro, D), BF16),
                   jax.ShapeDtypeStruct((N_DEV, 1) + d_si.shape[1:], BF16), jax.ShapeDtypeStruct((N_DEV, 1, ro, D), BF16),
                   jax.ShapeDtypeStruct((N_DEV, 2, D, cu), BF16), jax.ShapeDtypeStruct((N_DEV, 2, rd, D), BF16)],
        scratch_shapes=[pltpu.SemaphoreType.DMA((N_BIG, N_DEV)), pltpu.SemaphoreType.DMA((N_BIG, N_DEV))],
    )(d_fi, d_fo, d_si, d_so, d_up0, d_up1, d_dn0, d_dn1)


def _rope_tables(positions, sign):
    half = ROPE_DIM // 2
    inv_freq = ROPE_THETA ** (-jnp.arange(0, ROPE_DIM, 2, dtype=F32) / ROPE_DIM)
    ang = positions.astype(F32)[:, None] * inv_freq
    reps = LANE // half
    cos = jnp.tile(jnp.cos(ang), (1, reps))
    sin = jnp.tile(jnp.sin(ang), (1, reps)) * sign
    d = jnp.arange(LANE) % SWA_HEAD_DIM
    return (jnp.where(d < ROPE_DIM, cos, 1.0), jnp.where(d < half, -sin, 0.0),
            jnp.where((d >= half) & (d < ROPE_DIM), sin, 0.0))


def _pad_cols(a, n):
    return jnp.pad(a, ((0, 0), (0, n - a.shape[1])))


def _local_step(x, target, positions, mods, W, P):
    S, D = x.shape
    Hf = D // FOX_HEAD_DIM
    Hq = D // SWA_HEAD_DIM
    Hk = Hq // SWA_GROUP
    Wk = Hk * SWA_HEAD_DIM
    n_in = 3 * D + Hf
    (sh1a, sc1a, g1a, sh2a, sc2a, g2a), (sh1b, sc1b, g1b, sh2b, sc2b, g2b) = mods
    row = lambda v: v.reshape(1, -1)
    cw = [jnp.transpose(P["conv_w"][l].reshape(3, 2, -1), (1, 0, 2)) for l in range(2)]
    cb = [P["conv_b"][l].reshape(2, 1, -1) for l in range(2)]

    h1a = _modulate(x, sc1a, sh1a, "modulate_in")
    proj_a = _mm_nn(h1a, W["fox_in"], F32, "fox_in_proj", tn=896)
    flT = proj_a[:, 3 * D:n_in].T
    bf_col = P["fox_b_f"].reshape(Hf, 1)
    cumT = _fox_prep(flT, bf_col, "fox_cumsum")
    cum_col, cum_row = cumT.reshape(Hf, S, 1), cumT.reshape(Hf, 1, S)
    o_a, lse_a = _fox_attn_fwd(proj_a, cum_col, cum_row, Hf, "fox_attn_fwd")
    y1a = _mm_nn(o_a, W["fox_o"], F32, "fox_out_proj")
    z1a, x1, h2a = _ln_fwd(x, y1a, g1a, row(P["ln_mix_g"][0]), row(P["ln_mix_b"][0]), sc2a, sh2a, "ln_mix0")
    u_a, a_a = _ffn_up(h2a, W["up"][0], cw[0], cb[0], "ffn_up0")
    y2a = _mm_nn(a_a, W["down"][0], F32, "ffn_down0", tk=1408)
    z2a, x2, h1b = _ln_fwd(x1, y2a, g2a, row(P["ln_ffn_g"][0]), row(P["ln_ffn_b"][0]), sc1b, sh1b, "ln_ffn0")

    proj_b = _mm_nn(h1b, W["swa_in"], F32, "swa_in_proj")
    tabs_f = _rope_tables(positions, 1.0)
    tabs_b = _rope_tables(positions, -1.0)
    qk = _rope([(proj_b[:, :D + Wk], True)], tabs_f, F32, "rope_fwd")
    sinks = _pad_cols(P["swa_sinks"].reshape(1, Hq), LANE)
    o_b, lse_b = _swa_attn_fwd(qk, proj_b, sinks, Hq, "swa_attn_fwd")
    y1b = _mm_nn(o_b, W["swa_o"], F32, "swa_out_proj")
    z1b, x3, h2b = _ln_fwd(x2, y1b, g1b, row(P["ln_mix_g"][1]), row(P["ln_mix_b"][1]), sc2b, sh2b, "ln_mix1")
    u_b, a_b = _ffn_up(h2b, W["up"][1], cw[1], cb[1], "ffn_up1")
    y2b = _mm_nn(a_b, W["down"][1], F32, "ffn_down1", tk=1408)
    z2b, dout, loss_row = _ln_fwd_loss(x3, y2b, g2b, row(P["ln_ffn_g"][1]), row(P["ln_ffn_b"][1]), target, "ln_ffn1_loss")

    def ffn_backward(dy, a, u, h_in, l, tag):
        da = _mm_nt(dy[None], W["down"][l][None], F32, "ffn_da" + tag)
        d_down = _mm_tn(a, dy[None], BF16, "ffn_dwdown" + tag, tm=1408, tn=1024)[0]
        du, dcw, dcb = _ffn_bwd_elem(da, u, cw[l], cb[l], "ffn_bwd_elem" + tag)
        dh = _mm_nt(du, W["up"][l], F32, "ffn_dh" + tag, tk=1408)
        d_up = _mm_tn(h_in, du, BF16, "ffn_dwup" + tag)
        return dh, d_up, d_down, jnp.transpose(dcw, (1, 0, 2)).reshape(3, -1), dcb.reshape(-1)

    dz2b, dy2b, dg_f1, db_f1, dgate2b = _ln_bwd(dout, z2b, y2b, g2b, row(P["ln_ffn_g"][1]), "ln_ffn1_bwd")
    dh2b, d_up1, d_down1, dcw1, dcb1 = ffn_backward(dy2b, a_b, u_b, h2b, 1, "1")
    dx3, dsc2b, dsh2b = _mod_bwd(dz2b, dh2b, x3, sc2b, "mod_ffn1_bwd")

    dz1b, dy1b, dg_m1, db_m1, dgate1b = _ln_bwd(dx3, z1b, y1b, g1b, row(P["ln_mix_g"][1]), "ln_mix1_bwd")
    do_b = _mm_nt(dy1b[None], W["swa_o"][None], F32, "swa_do")
    d_swa_o = _mm_tn(o_b, dy1b[None], BF16, "swa_dwo")[0]
    dq_b, dk_b, dv_b, dsinks = _swa_attn_bwd(qk, proj_b, sinks, do_b, lse_b, Hq, "swa_attn_bwd")
    dproj_b = _rope([(dq_b, True), (dk_b, True), (dv_b, False)], tabs_b, BF16, "rope_bwd")
    dh1b = _mm_nt(dproj_b[None], W["swa_in"][None], F32, "swa_dh", tk=1280)
    d_swa_in = _mm_tn(h1b, dproj_b[None], BF16, "swa_dwin")[0]
    dx2, dsc1b, dsh1b = _mod_bwd(dz1b, dh1b, x2, sc1b, "mod_mix1_bwd")

    dz2a, dy2a, dg_f0, db_f0, dgate2a = _ln_bwd(dx2, z2a, y2a, g2a, row(P["ln_ffn_g"][0]), "ln_ffn0_bwd")
    dh2a, d_up0, d_down0, dcw0, dcb0 = ffn_backward(dy2a, a_a, u_a, h2a, 0, "0")
    dx1, dsc2a, dsh2a = _mod_bwd(dz2a, dh2a, x1, sc2a, "mod_ffn0_bwd")

    dz1a, dy1a, dg_m0, db_m0, dgate1a = _ln_bwd(dx1, z1a, y1a, g1a, row(P["ln_mix_g"][0]), "ln_mix0_bwd")
    do_a = _mm_nt(dy1a[None], W["fox_o"][None], F32, "fox_do")
    d_fox_o = _mm_tn(o_a, dy1a[None], BF16, "fox_dwo")[0]
    dq_a, dk_a, dv_a, dcum_row = _fox_attn_bwd(proj_a, do_a, cum_col, cum_row, lse_a, Hf, "fox_attn_bwd")
    dflT, dbf = _fox_prep_bwd(dcum_row.reshape(Hf, S), flT, bf_col, "fox_cumsum_bwd")
    n_pad = W["fox_in"].shape[1]
    dproj_a = jnp.concatenate([dq_a, dk_a.astype(BF16), dv_a.astype(BF16),
                               _pad_cols(dflT.T, n_pad - 3 * D).astype(BF16)], axis=1)
    dh1a = _mm_nt(dproj_a[None], W["fox_in"][None], F32, "fox_dh", tk=896)
    d_fox_in = _mm_tn(h1a, dproj_a[None], BF16, "fox_dwin", tn=896)[0]
    grad_x, dsc1a, dsh1a = _mod_bwd(dz1a, dh1a, x, sc1a, "mod_mix0_bwd")

    dmod = jnp.stack([jnp.concatenate([dsh1a, dsc1a, dgate1a, dsh2a, dsc2a, dgate2a], axis=1)[0],
                      jnp.concatenate([dsh1b, dsc1b, dgate1b, dsh2b, dsc2b, dgate2b], axis=1)[0]])
    big = dict(fox_in=d_fox_in, fox_o=d_fox_o, swa_in=d_swa_in, swa_o=d_swa_o,
               up0=d_up0, up1=d_up1, down0=d_down0, down1=d_down1)
    small = dict(dmod=dmod, conv_b=jnp.stack([dcb0, dcb1]), conv_w=jnp.stack([dcw0, dcw1]),
                 ln_mix_g=jnp.concatenate([dg_m0, dg_m1]), ln_mix_b=jnp.concatenate([db_m0, db_m1]),
                 ln_ffn_g=jnp.concatenate([dg_f0, dg_f1]), ln_ffn_b=jnp.concatenate([db_f0, db_f1]),
                 fox_b_f=dbf.reshape(-1), swa_sinks=dsinks[0, :Hq])
    return loss_row[0, 0], grad_x, big, small


SMALL_ORDER = ("dmod", "conv_b", "conv_w", "ln_mix_g", "ln_mix_b", "ln_ffn_g", "ln_ffn_b", "fox_b_f", "swa_sinks")


def _pack_rows(arrays):
    chunks, spans, off = [], [], 0
    for a in arrays:
        flat = a.reshape(-1)
        n = -(-flat.shape[0] // LANE) * LANE
        chunks.append(jnp.pad(flat, (0, n - flat.shape[0])))
        spans.append((off, flat.shape[0], a.shape))
        off += n
    total = -(-off // (8 * LANE)) * (8 * LANE)
    chunks.append(jnp.zeros((total - off,), F32))
    return jnp.concatenate(chunks).reshape(-1, LANE), spans


def _unpack_rows(packed, spans):
    flat = packed.reshape(-1)
    return [flat[off:off + n].reshape(shape) for off, n, shape in spans]


def kernel(x, c, positions, fox_w_in, fox_b_f, fox_w_o, swa_w_in, swa_sinks, swa_w_o, ada_w, ada_b, ffn_w_up, ffn_conv_w, ffn_conv_b, ffn_w_down, ln_mix_g, ln_mix_b, ln_ffn_g, ln_ffn_b, loss_target, m_fox_w_in, m_fox_b_f, m_fox_w_o, m_swa_w_in, m_swa_sinks, m_swa_w_o, m_ada_w, m_ada_b, m_ffn_w_up, m_ffn_conv_w, m_ffn_conv_b, m_ffn_w_down, m_ln_mix_g, m_ln_mix_b, m_ln_ffn_g, m_ln_ffn_b, v_fox_w_in, v_fox_b_f, v_fox_w_o, v_swa_w_in, v_swa_sinks, v_swa_w_o, v_ada_w, v_ada_b, v_ffn_w_up, v_ffn_conv_w, v_ffn_conv_b, v_ffn_w_down, v_ln_mix_g, v_ln_mix_b, v_ln_ffn_g, v_ln_ffn_b):
    S, D = x.shape[1], x.shape[2]
    L = ada_w.shape[0]
    me = 4 * lax.axis_index("x") + 2 * lax.axis_index("y") + lax.axis_index("c")
    n_ada = ada_w.shape[2]
    cu = ffn_w_up.shape[2]
    F = 4 * cu
    n_in = fox_w_in.shape[2] * N_DEV
    n_in_pad = -(-n_in // LANE) * LANE

    c_all = _all_gather_small(c.reshape(-1, LANE), "gather_c").reshape(N_DEV, D)
    b_cols = lax.dynamic_slice_in_dim(ada_b, me * n_ada, n_ada, axis=1).reshape(L, 1, n_ada)
    mod_blk = _ada_mod(jnp.pad(c_all, ((0, ADA_ROWS - N_DEV), (0, 0))), ada_w, b_cols, "ada_mod")[:, :N_DEV]
    mod_all = _all_gather_small(mod_blk.reshape(-1, LANE), "gather_mod").reshape(N_DEV, L, N_DEV, n_ada)
    mod_mine = lax.dynamic_index_in_dim(mod_all, me, axis=2, keepdims=False)
    mod_mine = jnp.transpose(mod_mine, (1, 0, 2)).reshape(L, N_DEV * n_ada)
    mods = [[mod_mine[l, k * D:(k + 1) * D].reshape(1, D) for k in range(6)] for l in range(L)]

    g_fi, g_fo, g_si, g_so, g_up, g_dn = _all_gather_weights(
        fox_w_in[0].astype(BF16), fox_w_o[0].astype(BF16), swa_w_in[0].astype(BF16), swa_w_o[0].astype(BF16),
        ffn_w_up.astype(BF16), ffn_w_down.astype(BF16), "gather_weights")
    W = dict(fox_in=_pad_cols(jnp.transpose(g_fi, (1, 0, 2)).reshape(D, n_in), n_in_pad), fox_o=g_fo,
             swa_in=jnp.transpose(g_si, (1, 0, 2)).reshape(D, -1), swa_o=g_so,
             up=[g_up[l] for l in range(L)], down=[g_dn[l] for l in range(L)])

    conv_w_all = None
    P = dict(fox_b_f=fox_b_f[0], swa_sinks=swa_sinks[0], conv_b=ffn_conv_b,
             ln_mix_g=ln_mix_g, ln_mix_b=ln_mix_b, ln_ffn_g=ln_ffn_g, ln_ffn_b=ln_ffn_b)
    cw_rows = _all_gather_small(_pack_rows([ffn_conv_w])[0], "gather_conv_w")
    n_cw = ffn_conv_w.size
    cw_dev = cw_rows.reshape(N_DEV, -1)[:, :n_cw].reshape(N_DEV, L, 3, cu)
    conv_w_all = jnp.transpose(cw_dev, (1, 2, 0, 3)).reshape(L, 3, N_DEV * cu)
    P["conv_w"] = conv_w_all

    loss_local, grad_x, big, small = _local_step(x[0], loss_target[0], positions[0], mods, W, P)
    loss = lax.psum(loss_local, AXES)

    d_fi = jnp.transpose(big["fox_in"][:, :n_in].reshape(D, N_DEV, -1), (1, 0, 2))
    d_si = jnp.transpose(big["swa_in"].reshape(D, N_DEV, -1), (1, 0, 2))
    r_fi, r_fo, r_si, r_so, r_up, r_dn = _reduce_scatter_grads(
        d_fi, big["fox_o"], d_si, big["swa_o"], big["up0"], big["up1"], big["down0"], big["down1"], "scatter_grads")
    out = {}
    out["fox_w_in"] = _adam_sum(r_fi, fox_w_in, m_fox_w_in, v_fox_w_in, "adam_fox_w_in")
    out["fox_w_o"] = _adam_sum(r_fo, fox_w_o, m_fox_w_o, v_fox_w_o, "adam_fox_w_o")
    out["swa_w_in"] = _adam_sum(r_si, swa_w_in, m_swa_w_in, v_swa_w_in, "adam_swa_w_in")
    out["swa_w_o"] = _adam_sum(r_so, swa_w_o, m_swa_w_o, v_swa_w_o, "adam_swa_w_o")
    out["ffn_w_up"] = _adam_sum(r_up, ffn_w_up, m_ffn_w_up, v_ffn_w_up, "adam_ffn_w_up")
    out["ffn_w_down"] = _adam_sum(r_dn, ffn_w_down, m_ffn_w_down, v_ffn_w_down, "adam_ffn_w_down")

    packed, spans = _pack_rows([small[k] for k in SMALL_ORDER])
    gathered = _all_gather_small(packed, "gather_small_grads")
    totals = dict(zip(SMALL_ORDER, _unpack_rows(_sum_slots(gathered, "sum_small_grads"), spans)))
    n_mod = L * 6 * D
    dmod_all = gathered.reshape(N_DEV, -1)[:, :n_mod].reshape(N_DEV, L, 6 * D)
    dmod_cols = jnp.transpose(lax.dynamic_slice_in_dim(dmod_all, me * n_ada, n_ada, axis=2), (1, 0, 2))
    g_ada_w = _ada_bwd(c_all.T, dmod_cols, "ada_w_grad")
    out["ada_w"] = (g_ada_w,) + tuple(_adam(g_ada_w, ada_w, m_ada_w, v_ada_w, "adam_ada_w"))

    g_small = dict(fox_b_f=totals["fox_b_f"].reshape(fox_b_f.shape), swa_sinks=totals["swa_sinks"].reshape(swa_sinks.shape),
                   ada_b=totals["dmod"].reshape(ada_b.shape), ffn_conv_b=totals["conv_b"].reshape(ffn_conv_b.shape),
                   ffn_conv_w=lax.dynamic_slice_in_dim(totals["conv_w"].reshape(L, 3, 2 * F), me * cu, cu, axis=2),
                   ln_mix_g=totals["ln_mix_g"], ln_mix_b=totals["ln_mix_b"],
                   ln_ffn_g=totals["ln_ffn_g"], ln_ffn_b=totals["ln_ffn_b"])
    small_names = ("fox_b_f", "swa_sinks", "ada_b", "ffn_conv_b", "ffn_conv_w", "ln_mix_g", "ln_mix_b", "ln_ffn_g", "ln_ffn_b")
    w_small = dict(fox_b_f=(fox_b_f, m_fox_b_f, v_fox_b_f), swa_sinks=(swa_sinks, m_swa_sinks, v_swa_sinks),
                   ada_b=(ada_b, m_ada_b, v_ada_b), ffn_conv_b=(ffn_conv_b, m_ffn_conv_b, v_ffn_conv_b),
                   ffn_conv_w=(ffn_conv_w, m_ffn_conv_w, v_ffn_conv_w),
                   ln_mix_g=(ln_mix_g, m_ln_mix_g, v_ln_mix_g), ln_mix_b=(ln_mix_b, m_ln_mix_b, v_ln_mix_b),
                   ln_ffn_g=(ln_ffn_g, m_ln_ffn_g, v_ln_ffn_g), ln_ffn_b=(ln_ffn_b, m_ln_ffn_b, v_ln_ffn_b))
    pk_g, sp = _pack_rows([g_small[k] for k in small_names])
    pk_w = _pack_rows([w_small[k][0] for k in small_names])[0]
    pk_m = _pack_rows([w_small[k][1] for k in small_names])[0]
    pk_v = _pack_rows([w_small[k][2] for k in small_names])[0]
    res = _adam(pk_g[None], pk_w[None], pk_m[None], pk_v[None], "adam_small")
    res = [dict(zip(small_names, _unpack_rows(r[0], sp))) for r in res]
    for k in small_names:
        out[k] = (g_small[k], res[0][k], res[1][k], res[2][k])

    order = ("fox_w_in", "fox_b_f", "fox_w_o", "swa_w_in", "swa_sinks", "swa_w_o", "ada_w", "ada_b", "ffn_w_up",
             "ffn_conv_w", "ffn_conv_b", "ffn_w_down", "ln_mix_g", "ln_mix_b", "ln_ffn_g", "ln_ffn_b")
    return (loss, grad_x[None], *[out[k][0] for k in order], *[out[k][1] for k in order],
            *[out[k][2] for k in order], *[out[k][3] for k in order])
```

```python
import functools

import jax
import jax.numpy as jnp
from jax import lax
from jax.experimental import pallas as pl
from jax.experimental.pallas import tpu as pltpu

F32 = jnp.float32
BF16 = jnp.bfloat16
MESH = pl.DeviceIdType.MESH
N_DEV = 8
AXES = ("x", "y", "c")

DEPTH = 2
ALPHA = (2.0 * DEPTH) ** 0.25
LN_EPS = 1e-5
FOX_HEAD_DIM = 128
SWA_HEAD_DIM = 64
SWA_GROUP = 8
SWA_WINDOW = 128
Q_BLOCK = 128
ROPE_DIM = 16
ROPE_THETA = 500000.0

ADAM_LR = 0.001
ADAM_B1 = 0.9
ADAM_B2 = 0.999
ADAM_EPS = 1e-08
ADAM_WD = 0.01
ADAM_STEP = 10

LANE = 128
MIB = 1024 * 1024


def _tile(n, pref, unit=LANE):
    if n <= pref:
        return n
    t = (pref // unit) * unit
    while t >= unit:
        if n % t == 0:
            return t
        t -= unit
    return n


def _params(sem, vmem_mib=48):
    return pltpu.CompilerParams(dimension_semantics=sem, vmem_limit_bytes=vmem_mib * MIB)


def _sigmoid(x):
    return 1.0 / (1.0 + jnp.exp(-x))


def _mm_call(dot, grid_mnk, in_specs, out_spec, out_shape, k_axis, nk, tm, tn, name, operands):
    sem = ("parallel",) * (len(grid_mnk) - 1) + ("arbitrary",)

    if nk == 1:
        def body(a_ref, b_ref, o_ref):
            o_ref[...] = dot(a_ref[...], b_ref[...]).astype(o_ref.dtype)
        scratch = []
    else:
        def body(a_ref, b_ref, o_ref, acc_ref):
            k = pl.program_id(k_axis)

            @pl.when(k == 0)
            def _():
                acc_ref[...] = jnp.zeros_like(acc_ref)

            acc_ref[...] += dot(a_ref[...], b_ref[...])

            @pl.when(k == nk - 1)
            def _():
                o_ref[...] = acc_ref[...].astype(o_ref.dtype)
        scratch = [pltpu.VMEM((tm, tn), F32)]

    return pl.pallas_call(
        body, name=name, grid=grid_mnk, in_specs=in_specs, out_specs=out_spec, out_shape=out_shape,
        scratch_shapes=scratch, compiler_params=_params(sem, 56),
    )(*operands)


def _dot(dims):
    def dot(a, b):
        return lax.dot_general(a.astype(BF16), b.astype(BF16), (dims, ((), ())), preferred_element_type=F32)
    return dot


def _mm_nn(a, b, out_dtype, name, tm=2048, tn=512, tk=2048):
    M, K = a.shape
    N = b.shape[1]
    tm, tn, tk = _tile(M, tm), _tile(N, tn), _tile(K, tk)
    nk = K // tk
    return _mm_call(
        _dot(((1,), (0,))), (M // tm, N // tn, nk),
        [pl.BlockSpec((tm, tk), lambda i, j, k: (i, k)), pl.BlockSpec((tk, tn), lambda i, j, k: (k, j))],
        pl.BlockSpec((tm, tn), lambda i, j, k: (i, j)), jax.ShapeDtypeStruct((M, N), out_dtype),
        2, nk, tm, tn, name, (a, b))


def _mm_nt(a, b, out_dtype, name, tm=2048, tn=512, tk=2048):
    P, M, K = a.shape
    N = b.shape[1]
    tm, tn, tk = _tile(M, tm), _tile(N, tn), _tile(K, tk)
    nk = K // tk
    return _mm_call(
        _dot(((1,), (1,))), (M // tm, N // tn, P * nk),
        [pl.BlockSpec((None, tm, tk), lambda i, j, k: (k // nk, i, k % nk)),
         pl.BlockSpec((None, tn, tk), lambda i, j, k: (k // nk, j, k % nk))],
        pl.BlockSpec((tm, tn), lambda i, j, k: (i, j)), jax.ShapeDtypeStruct((M, N), out_dtype),
        2, P * nk, tm, tn, name, (a, b))


def _mm_tn(a, b, out_dtype, name, tm=2048, tn=512, tk=2048):
    K, M = a.shape
    P, _, N = b.shape
    tm, tn, tk = _tile(M, tm), _tile(N, tn), _tile(K, tk)
    nk = K // tk
    return _mm_call(
        _dot(((0,), (0,))), (P, M // tm, N // tn, nk),
        [pl.BlockSpec((tk, tm), lambda p, i, j, k: (k, i)), pl.BlockSpec((None, tk, tn), lambda p, i, j, k: (p, k, j))],
        pl.BlockSpec((None, tm, tn), lambda p, i, j, k: (p, i, j)), jax.ShapeDtypeStruct((P, M, N), out_dtype),
        3, nk, tm, tn, name, (a, b))


ROW_TILE = 256


def _row_spec(tm, D):
    return pl.BlockSpec((tm, D), lambda i: (i, 0))


def _vec_spec(D):
    return pl.BlockSpec((1, D), lambda i: (0, 0))


def _modulate(x, sc, sh, name):
    S, D = x.shape
    tm = _tile(S, ROW_TILE, 8)

    def body(x_ref, sc_ref, sh_ref, h_ref):
        h_ref[...] = (x_ref[...] * (1.0 + sc_ref[...]) + sh_ref[...]).astype(BF16)

    return pl.pallas_call(
        body, name=name, grid=(S // tm,),
        in_specs=[_row_spec(tm, D), _vec_spec(D), _vec_spec(D)],
        out_specs=_row_spec(tm, D),
        out_shape=jax.ShapeDtypeStruct((S, D), BF16),
        compiler_params=_params(("parallel",)),
    )(x, sc, sh)


def _layer_norm_rows(z, gamma, beta):
    mu = jnp.mean(z, axis=-1, keepdims=True)
    zc = z - mu
    var = jnp.mean(zc * zc, axis=-1, keepdims=True)
    return zc * lax.rsqrt(var + LN_EPS) * gamma + beta


def _ln_fwd(x, y, gate, gamma, beta, sc_n, sh_n, name):
    S, D = x.shape
    tm = _tile(S, ROW_TILE, 8)

    def body(x_ref, y_ref, gate_ref, g_ref, b_ref, sc_ref, sh_ref, z_ref, xo_ref, hn_ref):
        z = ALPHA * x_ref[...] + (1.0 + gate_ref[...]) * y_ref[...]
        xo = _layer_norm_rows(z, g_ref[...], b_ref[...])
        z_ref[...] = z
        xo_ref[...] = xo
        hn_ref[...] = (xo * (1.0 + sc_ref[...]) + sh_ref[...]).astype(BF16)

    return pl.pallas_call(
        body, name=name, grid=(S // tm,),
        in_specs=[_row_spec(tm, D), _row_spec(tm, D)] + [_vec_spec(D)] * 5,
        out_specs=[_row_spec(tm, D)] * 3,
        out_shape=[jax.ShapeDtypeStruct((S, D), F32), jax.ShapeDtypeStruct((S, D), F32),
                   jax.ShapeDtypeStruct((S, D), BF16)],
        compiler_params=_params(("parallel",)),
    )(x, y, gate, gamma, beta, sc_n, sh_n)


def _ln_fwd_loss(x, y, gate, gamma, beta, target, name):
    S, D = x.shape
    tm = _tile(S, ROW_TILE, 8)

    def body(x_ref, y_ref, gate_ref, g_ref, b_ref, t_ref, z_ref, dout_ref, loss_ref):
        @pl.when(pl.program_id(0) == 0)
        def _():
            loss_ref[...] = jnp.zeros_like(loss_ref)

        z = ALPHA * x_ref[...] + (1.0 + gate_ref[...]) * y_ref[...]
        xo = _layer_norm_rows(z, g_ref[...], b_ref[...])
        err = xo - t_ref[...]
        z_ref[...] = z
        dout_ref[...] = err * (1.0 / D)
        loss_ref[...] += (0.5 / D) * jnp.sum(err * err)

    return pl.pallas_call(
        body, name=name, grid=(S // tm,),
        in_specs=[_row_spec(tm, D), _row_spec(tm, D)] + [_vec_spec(D)] * 3 + [_row_spec(tm, D)],
        out_specs=[_row_spec(tm, D), _row_spec(tm, D), pl.BlockSpec((1, LANE), lambda i: (0, 0))],
        out_shape=[jax.ShapeDtypeStruct((S, D), F32), jax.ShapeDtypeStruct((S, D), F32),
                   jax.ShapeDtypeStruct((1, LANE), F32)],
        compiler_params=_params(("arbitrary",)),
    )(x, y, gate, gamma, beta, target)


def _ln_bwd(dout, z, y, gate, gamma, name):
    S, D = z.shape
    tm = _tile(S, ROW_TILE, 8)

    def body(dout_ref, z_ref, y_ref, gate_ref, g_ref, dz_ref, dy_ref, dg_ref, db_ref, dgate_ref):
        @pl.when(pl.program_id(0) == 0)
        def _():
            dg_ref[...] = jnp.zeros_like(dg_ref)
            db_ref[...] = jnp.zeros_like(db_ref)
            dgate_ref[...] = jnp.zeros_like(dgate_ref)

        z = z_ref[...]
        dout = dout_ref[...]
        mu = jnp.mean(z, axis=-1, keepdims=True)
        zc = z - mu
        var = jnp.mean(zc * zc, axis=-1, keepdims=True)
        rstd = lax.rsqrt(var + LN_EPS)
        xhat = zc * rstd
        dxhat = dout * g_ref[...]
        m1 = jnp.mean(dxhat, axis=-1, keepdims=True)
        m2 = jnp.mean(dxhat * xhat, axis=-1, keepdims=True)
        dz = rstd * (dxhat - m1 - xhat * m2)
        dz_ref[...] = dz
        dy_ref[...] = (dz * (1.0 + gate_ref[...])).astype(BF16)
        dg_ref[...] += jnp.sum(dout * xhat, axis=0, keepdims=True)
        db_ref[...] += jnp.sum(dout, axis=0, keepdims=True)
        dgate_ref[...] += jnp.sum(dz * y_ref[...], axis=0, keepdims=True)

    return pl.pallas_call(
        body, name=name, grid=(S // tm,),
        in_specs=[_row_spec(tm, D)] * 3 + [_vec_spec(D)] * 2,
        out_specs=[_row_spec(tm, D), _row_spec(tm, D)] + [_vec_spec(D)] * 3,
        out_shape=[jax.ShapeDtypeStruct((S, D), F32), jax.ShapeDtypeStruct((S, D), BF16)]
        + [jax.ShapeDtypeStruct((1, D), F32)] * 3,
        compiler_params=_params(("arbitrary",)),
    )(dout, z, y, gate, gamma)


def _mod_bwd(dz, dh, xin, sc, name):
    S, D = dz.shape
    tm = _tile(S, ROW_TILE, 8)

    def body(dz_ref, dh_ref, x_ref, sc_ref, dx_ref, dsc_ref, dsh_ref):
        @pl.when(pl.program_id(0) == 0)
        def _():
            dsc_ref[...] = jnp.zeros_like(dsc_ref)
            dsh_ref[...] = jnp.zeros_like(dsh_ref)

        dh = dh_ref[...]
        dx_ref[...] = ALPHA * dz_ref[...] + dh * (1.0 + sc_ref[...])
        dsc_ref[...] += jnp.sum(dh * x_ref[...], axis=0, keepdims=True)
        dsh_ref[...] += jnp.sum(dh, axis=0, keepdims=True)

    return pl.pallas_call(
        body, name=name, grid=(S // tm,),
        in_specs=[_row_spec(tm, D)] * 3 + [_vec_spec(D)],
        out_specs=[_row_spec(tm, D), _vec_spec(D), _vec_spec(D)],
        out_shape=[jax.ShapeDtypeStruct((S, D), F32)] + [jax.ShapeDtypeStruct((1, D), F32)] * 2,
        compiler_params=_params(("arbitrary",)),
    )(dz, dh, xin, sc)


def _shift_down(u, k, row):
    return jnp.where(row >= k, pltpu.roll(u, k, axis=0), 0.0)


def _shift_up(u, k, row, S):
    return jnp.where(row < S - k, pltpu.roll(u, S - k, axis=0), 0.0)


def _ffn_up(h, w, cw, cb, name):
    S, D = h.shape
    F = w.shape[2]
    tn = _tile(F, 256)

    def body(h_ref, w_ref, cw_ref, cb_ref, u_ref, a_ref):
        hh = h_ref[...]
        row = lax.broadcasted_iota(jnp.int32, (S, tn), 0)
        conv = []
        for p in range(2):
            u = jnp.dot(hh, w_ref[p], preferred_element_type=F32)
            u_ref[p] = u
            cwp = cw_ref[p]
            conv.append(_shift_down(u, 2, row) * cwp[0:1] + _shift_down(u, 1, row) * cwp[1:2]
                        + u * cwp[2:3] + cb_ref[p])
        g, v = conv
        a_ref[...] = (g * _sigmoid(g) * v).astype(BF16)

    return pl.pallas_call(
        body, name=name, grid=(F // tn,),
        in_specs=[pl.BlockSpec((S, D), lambda j: (0, 0)), pl.BlockSpec((2, D, tn), lambda j: (0, 0, j)),
                  pl.BlockSpec((2, 3, tn), lambda j: (0, 0, j)), pl.BlockSpec((2, 1, tn), lambda j: (0, 0, j))],
        out_specs=[pl.BlockSpec((2, S, tn), lambda j: (0, 0, j)), pl.BlockSpec((S, tn), lambda j: (0, j))],
        out_shape=[jax.ShapeDtypeStruct((2, S, F), F32), jax.ShapeDtypeStruct((S, F), BF16)],
        compiler_params=_params(("parallel",), 56),
    )(h, w, cw, cb)


def _ffn_bwd_elem(da, u, cw, cb, name):
    _, S, F = u.shape
    tn = _tile(F, 256)

    def body(da_ref, u_ref, cw_ref, cb_ref, du_ref, dcw_ref, dcb_ref):
        row = lax.broadcasted_iota(jnp.int32, (S, tn), 0)
        da = da_ref[...]
        shifted, conv = [], []
        for p in range(2):
            u = u_ref[p]
            u1, u2 = _shift_down(u, 1, row), _shift_down(u, 2, row)
            cwp = cw_ref[p]
            shifted.append((u2, u1, u))
            conv.append(u2 * cwp[0:1] + u1 * cwp[1:2] + u * cwp[2:3] + cb_ref[p])
        g, v = conv
        sg = _sigmoid(g)
        d_conv = (da * v * (sg * (1.0 + g * (1.0 - sg))), da * (g * sg))
        for p in range(2):
            d = d_conv[p]
            cwp = cw_ref[p]
            dcb_ref[p] = jnp.sum(d, axis=0, keepdims=True)
            for j in range(3):
                dcw_ref[p, j:j + 1, :] = jnp.sum(d * shifted[p][j], axis=0, keepdims=True)
            du = d * cwp[2:3] + _shift_up(d, 1, row, S) * cwp[1:2] + _shift_up(d, 2, row, S) * cwp[0:1]
            du_ref[p] = du.astype(BF16)

    return pl.pallas_call(
        body, name=name, grid=(F // tn,),
        in_specs=[pl.BlockSpec((S, tn), lambda j: (0, j)), pl.BlockSpec((2, S, tn), lambda j: (0, 0, j)),
                  pl.BlockSpec((2, 3, tn), lambda j: (0, 0, j)), pl.BlockSpec((2, 1, tn), lambda j: (0, 0, j))],
        out_specs=[pl.BlockSpec((2, S, tn), lambda j: (0, 0, j)), pl.BlockSpec((2, 3, tn), lambda j: (0, 0, j)),
                   pl.BlockSpec((2, 1, tn), lambda j: (0, 0, j))],
        out_shape=[jax.ShapeDtypeStruct((2, S, F), BF16), jax.ShapeDtypeStruct((2, 3, F), F32),
                   jax.ShapeDtypeStruct((2, 1, F), F32)],
        compiler_params=_params(("parallel",), 56),
    )(da, u, cw, cb)


def _split3(x):
    hi = x.astype(BF16)
    r1 = x - hi.astype(F32)
    mid = r1.astype(BF16)
    lo = (r1 - mid.astype(F32)).astype(BF16)
    return hi, mid, lo


def _tri_matmul(x, upper, S):
    tc = _tile(S, 512)
    parts = _split3(x)
    outs = []
    for b in range(S // tc):
        r = lax.broadcasted_iota(jnp.int32, (S, tc), 0)
        c = lax.broadcasted_iota(jnp.int32, (S, tc), 1) + b * tc
        tri = jnp.where((r <= c) if upper else (r >= c), 1.0, 0.0).astype(BF16)
        acc = jnp.dot(parts[0], tri, preferred_element_type=F32)
        acc += jnp.dot(parts[1], tri, preferred_element_type=F32)
        acc += jnp.dot(parts[2], tri, preferred_element_type=F32)
        outs.append(acc)
    return outs, tc


def _fox_prep(flT, bf, name):
    H, S = flT.shape

    def body(fl_ref, b_ref, cum_ref):
        zz = fl_ref[...] + b_ref[...]
        lf = jnp.minimum(zz, 0.0) - jnp.log(1.0 + jnp.exp(-jnp.abs(zz)))
        outs, tc = _tri_matmul(lf, True, S)
        for b, o in enumerate(outs):
            cum_ref[:, b * tc:(b + 1) * tc] = o

    return pl.pallas_call(
        body, name=name,
        in_specs=[pl.BlockSpec(memory_space=pltpu.VMEM)] * 2,
        out_specs=pl.BlockSpec(memory_space=pltpu.VMEM),
        out_shape=jax.ShapeDtypeStruct((H, S), F32),
        compiler_params=pltpu.CompilerParams(vmem_limit_bytes=48 * MIB),
    )(flT, bf)


def _fox_prep_bwd(dcumT, flT, bf, name):
    H, S = flT.shape

    def body(dc_ref, fl_ref, b_ref, dfl_ref, dbf_ref):
        zz = fl_ref[...] + b_ref[...]
        outs, tc = _tri_matmul(dc_ref[...], False, S)
        total = jnp.zeros((H, 1), F32)
        for b, o in enumerate(outs):
            dfl = o * _sigmoid(-zz[:, b * tc:(b + 1) * tc])
            dfl_ref[:, b * tc:(b + 1) * tc] = dfl
            total += jnp.sum(dfl, axis=1, keepdims=True)
        dbf_ref[...] = total

    return pl.pallas_call(
        body, name=name,
        in_specs=[pl.BlockSpec(memory_space=pltpu.VMEM)] * 3,
        out_specs=[pl.BlockSpec(memory_space=pltpu.VMEM)] * 2,
        out_shape=[jax.ShapeDtypeStruct((H, S), F32), jax.ShapeDtypeStruct((H, 1), F32)],
        compiler_params=pltpu.CompilerParams(vmem_limit_bytes=48 * MIB),
    )(dcumT, flT, bf)


FOX_TQ = 256


def _fox_scores(q_ref, k_ref, cq_ref, ck_ref, i, tq, S, scale):
    q = q_ref[...].astype(BF16)
    k = k_ref[...].astype(BF16)
    s = lax.dot_general(q, k, (((1,), (1,)), ((), ())), preferred_element_type=F32) * scale
    s = s + cq_ref[...] - ck_ref[...]
    qpos = i * tq + lax.broadcasted_iota(jnp.int32, (tq, S), 0)
    kpos = lax.broadcasted_iota(jnp.int32, (tq, S), 1)
    return jnp.where(kpos <= qpos, s, -jnp.inf), q, k


def _fox_attn_fwd(proj, cum_col, cum_row, H, name):
    S = proj.shape[0]
    dh = FOX_HEAD_DIM
    tq = _tile(S, FOX_TQ)
    scale = dh ** -0.5

    def body(q_ref, k_ref, v_ref, cq_ref, ck_ref, o_ref, lse_ref):
        s, _, _ = _fox_scores(q_ref, k_ref, cq_ref, ck_ref, pl.program_id(1), tq, S, scale)
        m = jnp.max(s, axis=-1, keepdims=True)
        p = jnp.exp(s - m)
        l = jnp.sum(p, axis=-1, keepdims=True)
        o = jnp.dot(p.astype(BF16), v_ref[...].astype(BF16), preferred_element_type=F32) / l
        o_ref[...] = o.astype(BF16)
        lse_ref[...] = m + jnp.log(l)

    return pl.pallas_call(
        body, name=name, grid=(H, S // tq),
        in_specs=[pl.BlockSpec((tq, dh), lambda h, i: (i, h)),
                  pl.BlockSpec((S, dh), lambda h, i: (0, H + h)),
                  pl.BlockSpec((S, dh), lambda h, i: (0, 2 * H + h)),
                  pl.BlockSpec((None, tq, 1), lambda h, i: (h, i, 0)),
                  pl.BlockSpec((None, 1, S), lambda h, i: (h, 0, 0))],
        out_specs=[pl.BlockSpec((tq, dh), lambda h, i: (i, h)),
                   pl.BlockSpec((None, tq, 1), lambda h, i: (h, i, 0))],
        out_shape=[jax.ShapeDtypeStruct((S, H * dh), BF16), jax.ShapeDtypeStruct((H, S, 1), F32)],
        compiler_params=_params(("parallel", "parallel")),
    )(proj, proj, proj, cum_col, cum_row)


def _fox_attn_bwd(proj, do, cum_col, cum_row, lse, H, name):
    S = proj.shape[0]
    dh = FOX_HEAD_DIM
    tq = _tile(S, FOX_TQ)
    scale = dh ** -0.5

    def body(q_ref, k_ref, v_ref, do_ref, cq_ref, ck_ref, lse_ref, dq_ref, dk_ref, dv_ref, dck_ref):
        i = pl.program_id(1)

        @pl.when(i == 0)
        def _():
            dk_ref[...] = jnp.zeros_like(dk_ref)
            dv_ref[...] = jnp.zeros_like(dv_ref)
            dck_ref[...] = jnp.zeros_like(dck_ref)

        s, q, k = _fox_scores(q_ref, k_ref, cq_ref, ck_ref, i, tq, S, scale)
        p = jnp.exp(s - lse_ref[...])
        do_b = do_ref[...].astype(BF16)
        dp = lax.dot_general(do_b, v_ref[...].astype(BF16), (((1,), (1,)), ((), ())), preferred_element_type=F32)
        delta = jnp.sum(p * dp, axis=-1, keepdims=True)
        ds = p * (dp - delta)
        ds_b = ds.astype(BF16)
        dq_ref[...] = (jnp.dot(ds_b, k, preferred_element_type=F32) * scale).astype(BF16)
        dk_ref[...] += lax.dot_general(ds_b, q, (((0,), (0,)), ((), ())), preferred_element_type=F32) * scale
        dv_ref[...] += lax.dot_general(p.astype(BF16), do_b, (((0,), (0,)), ((), ())), preferred_element_type=F32)
        dck_ref[...] -= jnp.sum(ds, axis=0, keepdims=True)

    W = H * dh
    return pl.pallas_call(
        body, name=name, grid=(H, S // tq),
        in_specs=[pl.BlockSpec((tq, dh), lambda h, i: (i, h)),
                  pl.BlockSpec((S, dh), lambda h, i: (0, H + h)),
                  pl.BlockSpec((S, dh), lambda h, i: (0, 2 * H + h)),
                  pl.BlockSpec((tq, dh), lambda h, i: (i, h)),
                  pl.BlockSpec((None, tq, 1), lambda h, i: (h, i, 0)),
                  pl.BlockSpec((None, 1, S), lambda h, i: (h, 0, 0)),
                  pl.BlockSpec((None, tq, 1), lambda h, i: (h, i, 0))],
        out_specs=[pl.BlockSpec((tq, dh), lambda h, i: (i, h)),
                   pl.BlockSpec((S, dh), lambda h, i: (0, h)),
                   pl.BlockSpec((S, dh), lambda h, i: (0, h)),
                   pl.BlockSpec((None, 1, S), lambda h, i: (h, 0, 0))],
        out_shape=[jax.ShapeDtypeStruct((S, W), BF16), jax.ShapeDtypeStruct((S, W), F32),
                   jax.ShapeDtypeStruct((S, W), F32), jax.ShapeDtypeStruct((H, 1, S), F32)],
        compiler_params=_params(("parallel", "arbitrary")),
    )(proj, proj, proj, do, cum_col, cum_row, lse)


def _rope(parts, tabs, out_dtype, name):
    S = parts[0][0].shape[0]
    widths = [a.shape[1] for a, _ in parts]
    total = sum(widths)
    tm = _tile(S, ROW_TILE, 8)
    flags = [r for _, r in parts]

    def body(*refs):
        in_refs = refs[:len(parts)]
        cos_ref, sa_ref, sb_ref, o_ref = refs[len(parts):]
        cos, sa, sb = cos_ref[...], sa_ref[...], sb_ref[...]
        off = 0
        for ref, rot, w in zip(in_refs, flags, widths):
            for j in range(w // LANE):
                t = ref[:, j * LANE:(j + 1) * LANE]
                if rot:
                    t = t * cos + pltpu.roll(t, LANE - ROPE_DIM // 2, axis=1) * sa + pltpu.roll(t, ROPE_DIM // 2, axis=1) * sb
                o_ref[:, off + j * LANE:off + (j + 1) * LANE] = t.astype(o_ref.dtype)
            off += w

    return pl.pallas_call(
        body, name=name, grid=(S // tm,),
        in_specs=[pl.BlockSpec((tm, w), lambda i: (i, 0)) for w in widths] + [_row_spec(tm, LANE)] * 3,
        out_specs=_row_spec(tm, total),
        out_shape=jax.ShapeDtypeStruct((S, total), out_dtype),
        compiler_params=_params(("parallel",)),
    )(*[a for a, _ in parts], *tabs)


def _swa_band(ref_p, ref_c, hk):
    dh = SWA_HEAD_DIM
    return jnp.concatenate([ref_p[:, hk * dh:(hk + 1) * dh], ref_c[:, hk * dh:(hk + 1) * dh]], axis=0).astype(BF16)


def _swa_mask(n):
    qi = lax.broadcasted_iota(jnp.int32, (Q_BLOCK, 2 * Q_BLOCK), 0)
    kj = lax.broadcasted_iota(jnp.int32, (Q_BLOCK, 2 * Q_BLOCK), 1)
    rel = qi + Q_BLOCK - kj
    return (rel >= 0) & (rel < SWA_WINDOW) & ((kj >= Q_BLOCK) | (n > 0))


def _swa_attn_fwd(qk, proj, sinks, Hq, name):
    S = qk.shape[0]
    dh, G, QB = SWA_HEAD_DIM, SWA_GROUP, Q_BLOCK
    Hk = Hq // G
    Wq, Wk = Hq * dh, Hk * dh
    nb = S // QB
    scale = dh ** -0.5

    def body(q_ref, kp_ref, kc_ref, vp_ref, vc_ref, sink_ref, o_ref, lse_ref):
        n = pl.program_id(0)
        mask = _swa_mask(n)
        lane = lax.broadcasted_iota(jnp.int32, (QB, LANE), 1)
        lse_tile = jnp.zeros((QB, LANE), F32)
        for hk in range(Hk):
            kb = _swa_band(kp_ref, kc_ref, hk)
            vb = _swa_band(vp_ref, vc_ref, hk)
            outs = []
            for g in range(G):
                h = hk * G + g
                q = q_ref[:, h * dh:(h + 1) * dh].astype(BF16)
                s = lax.dot_general(q, kb, (((1,), (1,)), ((), ())), preferred_element_type=F32) * scale
                s = jnp.where(mask, s, -jnp.inf)
                sk = sink_ref[0:1, h:h + 1]
                m = jnp.maximum(jnp.max(s, axis=-1, keepdims=True), sk)
                p = jnp.exp(s - m)
                l = jnp.sum(p, axis=-1, keepdims=True) + jnp.exp(sk - m)
                outs.append(jnp.dot(p.astype(BF16), vb, preferred_element_type=F32) / l)
                lse_tile = jnp.where(lane == h, m + jnp.log(l), lse_tile)
            for g in range(0, G, 2):
                c0 = (hk * G + g) * dh
                o_ref[:, c0:c0 + 2 * dh] = jnp.concatenate([outs[g], outs[g + 1]], axis=1).astype(BF16)
        lse_ref[...] = lse_tile

    kcol, vcol = Wq // Wk, (Wq + Wk) // Wk
    return pl.pallas_call(
        body, name=name, grid=(nb,),
        in_specs=[pl.BlockSpec((QB, Wq), lambda n: (n, 0)),
                  pl.BlockSpec((QB, Wk), lambda n: (jnp.maximum(n - 1, 0), kcol)),
                  pl.BlockSpec((QB, Wk), lambda n: (n, kcol)),
                  pl.BlockSpec((QB, Wk), lambda n: (jnp.maximum(n - 1, 0), vcol)),
                  pl.BlockSpec((QB, Wk), lambda n: (n, vcol)),
                  pl.BlockSpec((1, LANE), lambda n: (0, 0))],
        out_specs=[pl.BlockSpec((QB, Wq), lambda n: (n, 0)), pl.BlockSpec((QB, LANE), lambda n: (n, 0))],
        out_shape=[jax.ShapeDtypeStruct((S, Wq), BF16), jax.ShapeDtypeStruct((S, LANE), F32)],
        compiler_params=_params(("parallel",)),
    )(qk, qk, qk, proj, proj, sinks)


def _swa_attn_bwd(qk, proj, sinks, do, lse, Hq, name):
    S = qk.shape[0]
    dh, G, QB = SWA_HEAD_DIM, SWA_GROUP, Q_BLOCK
    Hk = Hq // G
    Wq, Wk = Hq * dh, Hk * dh
    nb = S // QB
    scale = dh ** -0.5

    def body(q_ref, kp_ref, kc_ref, vp_ref, vc_ref, sink_ref, do_ref, lse_ref,
             dq_ref, dk_ref, dv_ref, dsink_ref, carry_k, carry_v):
        n = pl.program_id(0)

        @pl.when(n == 0)
        def _():
            dsink_ref[...] = jnp.zeros_like(dsink_ref)

        @pl.when(n < nb)
        def _():
            mask = _swa_mask(n)
            lane = lax.broadcasted_iota(jnp.int32, (1, LANE), 1)
            dsink = jnp.zeros((1, LANE), F32)
            dk_heads, dv_heads = [], []
            for hk in range(Hk):
                kb = _swa_band(kp_ref, kc_ref, hk)
                vb = _swa_band(vp_ref, vc_ref, hk)
                dkb = jnp.zeros((2 * QB, dh), F32)
                dvb = jnp.zeros((2 * QB, dh), F32)
                dqs = []
                for g in range(G):
                    h = hk * G + g
                    q = q_ref[:, h * dh:(h + 1) * dh].astype(BF16)
                    do_h = do_ref[:, h * dh:(h + 1) * dh].astype(BF16)
                    lse_h = lse_ref[:, h:h + 1]
                    s = lax.dot_general(q, kb, (((1,), (1,)), ((), ())), preferred_element_type=F32) * scale
                    s = jnp.where(mask, s, -jnp.inf)
                    p = jnp.exp(s - lse_h)
                    p_sink = jnp.exp(sink_ref[0:1, h:h + 1] - lse_h)
                    dp = lax.dot_general(do_h, vb, (((1,), (1,)), ((), ())), preferred_element_type=F32)
                    delta = jnp.sum(p * dp, axis=-1, keepdims=True)
                    ds_b = (p * (dp - delta)).astype(BF16)
                    dqs.append(jnp.dot(ds_b, kb, preferred_element_type=F32) * scale)
                    dkb += lax.dot_general(ds_b, q, (((0,), (0,)), ((), ())), preferred_element_type=F32) * scale
                    dvb += lax.dot_general(p.astype(BF16), do_h, (((0,), (0,)), ((), ())), preferred_element_type=F32)
                    dsink = jnp.where(lane == h, -jnp.sum(p_sink * delta, axis=0, keepdims=True), dsink)
                for g in range(0, G, 2):
                    c0 = (hk * G + g) * dh
                    dq_ref[:, c0:c0 + 2 * dh] = jnp.concatenate([dqs[g], dqs[g + 1]], axis=1)
                dk_heads.append(dkb)
                dv_heads.append(dvb)
            dsink_ref[...] += dsink
            dk_all = jnp.concatenate(dk_heads, axis=1)
            dv_all = jnp.concatenate(dv_heads, axis=1)

            @pl.when(n > 0)
            def _():
                dk_ref[...] = carry_k[...] + dk_all[:QB]
                dv_ref[...] = carry_v[...] + dv_all[:QB]

            carry_k[...] = dk_all[QB:]
            carry_v[...] = dv_all[QB:]

        @pl.when(n == nb)
        def _():
            dk_ref[...] = carry_k[...]
            dv_ref[...] = carry_v[...]

    kcol, vcol = Wq // Wk, (Wq + Wk) // Wk
    cur = lambda n: jnp.minimum(n, nb - 1)
    prev = lambda n: jnp.maximum(jnp.minimum(n, nb - 1) - 1, 0)
    return pl.pallas_call(
        body, name=name, grid=(nb + 1,),
        in_specs=[pl.BlockSpec((QB, Wq), lambda n: (cur(n), 0)),
                  pl.BlockSpec((QB, Wk), lambda n: (prev(n), kcol)),
                  pl.BlockSpec((QB, Wk), lambda n: (cur(n), kcol)),
                  pl.BlockSpec((QB, Wk), lambda n: (prev(n), vcol)),
                  pl.BlockSpec((QB, Wk), lambda n: (cur(n), vcol)),
                  pl.BlockSpec((1, LANE), lambda n: (0, 0)),
                  pl.BlockSpec((QB, Wq), lambda n: (cur(n), 0)),
                  pl.BlockSpec((QB, LANE), lambda n: (cur(n), 0))],
        out_specs=[pl.BlockSpec((QB, Wq), lambda n: (cur(n), 0)),
                   pl.BlockSpec((QB, Wk), lambda n: (jnp.maximum(n - 1, 0), 0)),
                   pl.BlockSpec((QB, Wk), lambda n: (jnp.maximum(n - 1, 0), 0)),
                   pl.BlockSpec((1, LANE), lambda n: (0, 0))],
        out_shape=[jax.ShapeDtypeStruct((S, Wq), F32), jax.ShapeDtypeStruct((S, Wk), F32),
                   jax.ShapeDtypeStruct((S, Wk), F32), jax.ShapeDtypeStruct((1, LANE), F32)],
        scratch_shapes=[pltpu.VMEM((QB, Wk), F32), pltpu.VMEM((QB, Wk), F32)],
        compiler_params=_params(("arbitrary",)),
    )(qk, qk, qk, proj, proj, sinks, do, lse)


ADA_ROWS = 16


def _ada_mod(c_pad, w, b, name):
    L, D, N = w.shape
    tn = _tile(N, 512)

    def body(c_ref, w_ref, b_ref, o_ref):
        c = c_ref[...]
        c = c * _sigmoid(c)
        ch = c.astype(BF16)
        cl = (c - ch.astype(F32)).astype(BF16)
        ww = w_ref[...]
        wh = ww.astype(BF16)
        wl = (ww - wh.astype(F32)).astype(BF16)
        acc = jnp.dot(ch, wh, preferred_element_type=F32)
        acc += jnp.dot(ch, wl, preferred_element_type=F32)
        acc += jnp.dot(cl, wh, preferred_element_type=F32)
        o_ref[...] = acc + b_ref[...]

    return pl.pallas_call(
        body, name=name, grid=(L, N // tn),
        in_specs=[pl.BlockSpec((ADA_ROWS, D), lambda l, j: (0, 0)),
                  pl.BlockSpec((None, D, tn), lambda l, j: (l, 0, j)),
                  pl.BlockSpec((None, 1, tn), lambda l, j: (l, 0, j))],
        out_specs=pl.BlockSpec((None, ADA_ROWS, tn), lambda l, j: (l, 0, j)),
        out_shape=jax.ShapeDtypeStruct((L, ADA_ROWS, N), F32),
        compiler_params=_params(("parallel", "parallel")),
    )(c_pad, w, b)


def _ada_bwd(cT, dm, name):
    D = cT.shape[0]
    L, B, N = dm.shape
    tm = _tile(D, 256)

    def body(c_ref, dm_ref, o_ref):
        c = c_ref[...]
        c = c * _sigmoid(c)
        dmv = dm_ref[...]
        acc = c[:, 0:1] * dmv[0:1, :]
        for b in range(1, B):
            acc += c[:, b:b + 1] * dmv[b:b + 1, :]
        o_ref[...] = acc

    return pl.pallas_call(
        body, name=name, grid=(L, D // tm),
        in_specs=[pl.BlockSpec((tm, B), lambda l, i: (i, 0)), pl.BlockSpec((None, B, N), lambda l, i: (l, 0, 0))],
        out_specs=pl.BlockSpec((None, tm, N), lambda l, i: (l, i, 0)),
        out_shape=jax.ShapeDtypeStruct((L, D, N), F32),
        compiler_params=_params(("parallel", "parallel")),
    )(cT, dm)


def _adamw_math(w, g, m, v):
    m = ADAM_B1 * m + (1.0 - ADAM_B1) * g
    v = ADAM_B2 * v + (1.0 - ADAM_B2) * (g * g)
    m_hat = m / (1.0 - ADAM_B1 ** ADAM_STEP)
    v_hat = v / (1.0 - ADAM_B2 ** ADAM_STEP)
    delta = -ADAM_LR * (m_hat / (jnp.sqrt(v_hat) + ADAM_EPS) + ADAM_WD * w)
    return delta, m, v


def _adam_rows(R, C):
    lanes = -(-C // LANE) * LANE
    return _tile(R, max(8, (262144 // lanes) // 8 * 8), 8)


def _adam_sum(recv, w, m, v, layer, filled, name):
    P, R, C = recv.shape
    L = w.shape[0]
    tr = _adam_rows(R, C)
    n_keep = 0 if filled is None else 4

    def body(r_ref, w_ref, m_ref, v_ref, *rest):
        g_ref, d_ref, mo_ref, vo_ref = rest[n_keep:]
        g = r_ref[0].astype(F32)
        for p in range(1, P):
            g = g + r_ref[p].astype(F32)
        delta, mn, vn = _adamw_math(w_ref[...], g, m_ref[...], v_ref[...])
        g_ref[...] = g
        d_ref[...] = delta
        mo_ref[...] = mn
        vo_ref[...] = vn

    spec = pl.BlockSpec((None, tr, C), lambda i: (layer, i, 0))
    return pl.pallas_call(
        body, name=name, grid=(R // tr,),
        in_specs=[pl.BlockSpec((P, tr, C), lambda i: (0, i, 0)), spec, spec, spec]
        + [pl.BlockSpec(memory_space=pl.ANY)] * n_keep,
        out_specs=[spec] * 4,
        out_shape=[jax.ShapeDtypeStruct((L, R, C), F32)] * 4,
        input_output_aliases={4 + k: k for k in range(n_keep)},
        compiler_params=_params(("parallel",)),
    )(recv, w, m, v, *(filled or ()))


def _adam(g, w, m, v, name):
    L, R, C = w.shape
    tr = _adam_rows(R, C)

    def body(g_ref, w_ref, m_ref, v_ref, d_ref, mo_ref, vo_ref):
        delta, mn, vn = _adamw_math(w_ref[...], g_ref[...], m_ref[...], v_ref[...])
        d_ref[...] = delta
        mo_ref[...] = mn
        vo_ref[...] = vn

    spec = pl.BlockSpec((None, tr, C), lambda l, i: (l, i, 0))
    return pl.pallas_call(
        body, name=name, grid=(L, R // tr),
        in_specs=[spec] * 4, out_specs=[spec] * 3,
        out_shape=[jax.ShapeDtypeStruct((L, R, C), F32)] * 3,
        compiler_params=_params(("parallel", "parallel")),
    )(g, w, m, v)


def _sum_slots(x, name):
    P, R, C = x.shape

    def body(x_ref, o_ref):
        acc = x_ref[0]
        for p in range(1, P):
            acc = acc + x_ref[p]
        o_ref[...] = acc

    return pl.pallas_call(
        body, name=name,
        in_specs=[pl.BlockSpec(memory_space=pltpu.VMEM)], out_specs=pl.BlockSpec(memory_space=pltpu.VMEM),
        out_shape=jax.ShapeDtypeStruct((R, C), F32),
        compiler_params=pltpu.CompilerParams(vmem_limit_bytes=48 * MIB),
    )(x)


def _my_pos():
    return lax.axis_index("x"), lax.axis_index("y"), lax.axis_index("c")


def _all_gather_small(x, name):
    R, C = x.shape

    def body(x_ref, out_ref, send_sems, recv_sems):
        x_, y_, c_ = _my_pos()
        me, sibling = (x_, y_, c_), (x_, y_, 1 - c_)
        chips = [(1 - x_, y_), (x_, 1 - y_), (1 - x_, 1 - y_)]

        def slot(px, py, pc):
            return out_ref.at[4 * px + 2 * py + pc]

        def copy(k, block, to):
            return pltpu.make_async_remote_copy(
                src_ref=slot(*block), dst_ref=slot(*block), send_sem=send_sems.at[k], recv_sem=recv_sems.at[k],
                device_id=to, device_id_type=MESH)

        out_ref[4 * x_ + 2 * y_ + c_] = x_ref[...]
        first = [copy(0, me, sibling)] + [copy(1 + j, me, (*chip, c_)) for j, chip in enumerate(chips)]
        for cp in first:
            cp.start()
        passed = [copy(4 + j, (*chip, c_), sibling) for j, chip in enumerate(chips)]
        for j, chip in enumerate(chips):
            copy(1 + j, (*chip, c_), me).wait_recv()
            passed[j].start()
        copy(0, sibling, me).wait_recv()
        for j, chip in enumerate(chips):
            copy(4 + j, (*chip, 1 - c_), me).wait_recv()
        for cp in first + passed:
            cp.wait_send()

    return pl.pallas_call(
        body, name=name,
        in_specs=[pl.BlockSpec(memory_space=pltpu.VMEM)], out_specs=pl.BlockSpec(memory_space=pltpu.VMEM),
        out_shape=jax.ShapeDtypeStruct((N_DEV, R, C), x.dtype),
        scratch_shapes=[pltpu.SemaphoreType.DMA((7,)), pltpu.SemaphoreType.DMA((7,))],
        compiler_params=pltpu.CompilerParams(vmem_limit_bytes=48 * MIB),
    )(x)


N_BIG = 8


def _all_gather_weights(fox_in, fox_o, swa_in, swa_o, up, down, name):
    L, D, cu = up.shape
    F = 4 * cu
    rd = down.shape[1]
    ro = fox_o.shape[0]

    def body(fi_s, fo_s, si_s, so_s, up_s, dn_s, fi_g, fo_g, si_g, so_g, up_g, dn_g, send_sems, recv_sems, local_sems):
        x_, y_, c_ = _my_pos()
        me, sibling = (x_, y_, c_), (x_, y_, 1 - c_)
        chips = [(1 - x_, y_), (x_, 1 - y_), (1 - x_, 1 - y_)]

        def idx(px, py, pc):
            return 4 * px + 2 * py + pc

        def rows(n, k):
            return pl.ds(pl.multiple_of(k * n, n), n)

        srcs = [fi_s, fo_s, si_s, so_s, up_s.at[0], up_s.at[1], dn_s.at[0], dn_s.at[1]]
        dsts = [
            lambda b: fi_g.at[idx(*b)],
            lambda b: fo_g.at[rows(ro, idx(*b)), :],
            lambda b: si_g.at[idx(*b)],
            lambda b: so_g.at[rows(ro, idx(*b)), :],
            lambda b: up_g.at[0, b[0], :, rows(cu, 2 * b[1] + b[2])],
            lambda b: up_g.at[1, b[0], :, rows(cu, 2 * b[1] + b[2])],
            lambda b: dn_g.at[0, rows(rd, idx(*b)), :],
            lambda b: dn_g.at[1, rows(rd, idx(*b)), :],
        ]

        def copy(e, k, block, to, own=False):
            return pltpu.make_async_remote_copy(
                src_ref=srcs[e] if own else dsts[e](block), dst_ref=dsts[e](block),
                send_sem=send_sems.at[e, k], recv_sem=recv_sems.at[e, k], device_id=to, device_id_type=MESH)

        mine = [pltpu.make_async_copy(srcs[e], dsts[e](me), local_sems.at[e]) for e in range(N_BIG)]
        for cp in mine:
            cp.start()
        first = []
        for e in range(N_BIG):
            first.append(copy(e, 0, me, sibling, own=True))
            first += [copy(e, 1 + j, me, (*chip, c_), own=True) for j, chip in enumerate(chips)]
        for cp in first:
            cp.start()
        passed = []
        for e in range(N_BIG):
            for j, chip in enumerate(chips):
                copy(e, 1 + j, (*chip, c_), me).wait_recv()
                fwd = copy(e, 4 + j, (*chip, c_), sibling)
                fwd.start()
                passed.append(fwd)
        for e in range(N_BIG):
            copy(e, 0, sibling, me).wait_recv()
            for j, chip in enumerate(chips):
                copy(e, 4 + j, (*chip, 1 - c_), me).wait_recv()
        for cp in first + passed:
            cp.wait_send()
        for cp in mine:
            cp.wait()

    any_spec = pl.BlockSpec(memory_space=pl.ANY)
    return pl.pallas_call(
        body, name=name,
        in_specs=[any_spec] * 6, out_specs=[any_spec] * 6,
        out_shape=[jax.ShapeDtypeStruct((N_DEV,) + fox_in.shape, BF16), jax.ShapeDtypeStruct((N_DEV * ro, D), BF16),
                   jax.ShapeDtypeStruct((N_DEV,) + swa_in.shape, BF16), jax.ShapeDtypeStruct((N_DEV * ro, D), BF16),
                   jax.ShapeDtypeStruct((L, 2, D, F), BF16), jax.ShapeDtypeStruct((L, N_DEV * rd, D), BF16)],
        scratch_shapes=[pltpu.SemaphoreType.DMA((N_BIG, 7)), pltpu.SemaphoreType.DMA((N_BIG, 7)),
                        pltpu.SemaphoreType.DMA((N_BIG,))],
    )(fox_in, fox_o, swa_in, swa_o, up, down)


HBM_SPEC = pl.BlockSpec(memory_space=pltpu.HBM)
SEM_SPEC = pl.BlockSpec(memory_space=pltpu.SEMAPHORE)
SPLIT_EFFECT = pltpu.SideEffectType.DATAFLOW_SIDE_EFFECTING


def _in_hbm(a):
    return pltpu.with_memory_space_constraint(a, pltpu.HBM)


def _grad_slice(ref, kind, j):
    if kind == "major":
        return ref.at[j]
    if kind == "rows":
        r = ref.shape[0] // N_DEV
        return ref.at[pl.ds(j * r, r), :]
    cu = ref.shape[2] // 4
    return ref.at[j // 4, :, pl.ds((j % 4) * cu, cu)]


def _slice_shape(a, kind):
    if kind == "major":
        return a.shape[1:]
    if kind == "rows":
        return (a.shape[0] // N_DEV, a.shape[1])
    return (a.shape[1], a.shape[2] // 4)


def _scatter_copies(srcs, lands, kinds, send_sems, recv_sems):
    x_, y_, c_ = _my_pos()
    me = 4 * x_ + 2 * y_ + c_
    n = len(srcs)

    def remote(e, j):
        return pltpu.make_async_remote_copy(
            src_ref=_grad_slice(srcs[e], kinds[e], j), dst_ref=lands[e].at[me],
            send_sem=send_sems.at[e * N_DEV + j], recv_sem=recv_sems.at[e * N_DEV + me],
            device_id=(j // 4, (j // 2) % 2, j % 2), device_id_type=MESH)

    def local(e, j):
        return pltpu.make_async_copy(_grad_slice(srcs[e], kinds[e], j), lands[e].at[j], recv_sems.at[e * N_DEV + j])

    def arrival(e, i):
        return pltpu.make_async_remote_copy(
            src_ref=_grad_slice(srcs[e], kinds[e], i), dst_ref=lands[e].at[i],
            send_sem=send_sems.at[e * N_DEV + i], recv_sem=recv_sems.at[e * N_DEV + i],
            device_id=(i // 4, (i // 2) % 2, i % 2), device_id_type=MESH)

    def start():
        for e in range(n):
            for j in range(N_DEV):
                @pl.when(me == j)
                def _():
                    local(e, j).start()

                @pl.when(me != j)
                def _():
                    remote(e, j).start()

    def wait():
        for e in range(n):
            for i in range(N_DEV):
                @pl.when(me == i)
                def _():
                    local(e, i).wait()

                @pl.when(me != i)
                def _():
                    arrival(e, i).wait_recv()
        for e in range(n):
            for j in range(N_DEV):
                @pl.when(me != j)
                def _():
                    remote(e, j).wait_send()

    return start, wait


def _scatter_start(srcs, kinds, name):
    n = len(srcs)
    lands = [lax.empty((N_DEV,) + _slice_shape(a, k), a.dtype) for a, k in zip(srcs, kinds)]

    def body(*refs):
        src_refs, land_refs = refs[:n], refs[n:2 * n]
        send_sems, recv_sems = refs[2 * n], refs[2 * n + 1]
        token = refs[-1]
        start, _ = _scatter_copies(src_refs, land_refs, kinds, send_sems, recv_sems)
        start()
        token[...] = jnp.zeros_like(token)

    out = pl.pallas_call(
        body, name=name,
        out_shape=(pltpu.SemaphoreType.DMA((n * N_DEV,)), pltpu.SemaphoreType.DMA((n * N_DEV,)),
                   *[pltpu.HBM(a.shape, a.dtype) for a in srcs], *[pltpu.HBM(a.shape, a.dtype) for a in lands],
                   jax.ShapeDtypeStruct((8, LANE), F32)),
        in_specs=[HBM_SPEC] * (2 * n),
        out_specs=(SEM_SPEC, SEM_SPEC, *[HBM_SPEC] * (2 * n), pl.BlockSpec(memory_space=pltpu.VMEM)),
        input_output_aliases={i: 2 + i for i in range(2 * n)},
        compiler_params=pltpu.CompilerParams(has_side_effects=SPLIT_EFFECT),
    )(*[_in_hbm(a) for a in srcs], *[_in_hbm(a) for a in lands])
    return out[0], out[1], out[2:2 + n], out[2 + n:2 + 2 * n], out[-1]


def _scatter_wait(send_sems, recv_sems, srcs, lands, kinds, after, name):
    n = len(srcs)

    def body(*refs):
        src_refs, land_refs = refs[:n], refs[n:2 * n]
        _, wait = _scatter_copies(src_refs, land_refs, kinds, refs[2 * n], refs[2 * n + 1])
        wait()

    out = pl.pallas_call(
        body, name=name,
        out_shape=(*[pltpu.HBM(a.shape, a.dtype) for a in srcs], *[pltpu.HBM(a.shape, a.dtype) for a in lands]),
        in_specs=[HBM_SPEC] * (2 * n) + [SEM_SPEC, SEM_SPEC, pl.BlockSpec(memory_space=pl.ANY)],
        out_specs=tuple([HBM_SPEC] * (2 * n)),
        input_output_aliases={i: i for i in range(2 * n)},
        compiler_params=pltpu.CompilerParams(has_side_effects=SPLIT_EFFECT),
    )(*srcs, *lands, send_sems, recv_sems, after)
    return out[n:]


def _tie(x, token):
    return lax.optimization_barrier((x, token))[0]


def _rope_tables(positions, sign):
    half = ROPE_DIM // 2
    inv_freq = ROPE_THETA ** (-jnp.arange(0, ROPE_DIM, 2, dtype=F32) / ROPE_DIM)
    ang = positions.astype(F32)[:, None] * inv_freq
    reps = LANE // half
    cos = jnp.tile(jnp.cos(ang), (1, reps))
    sin = jnp.tile(jnp.sin(ang), (1, reps)) * sign
    d = jnp.arange(LANE) % SWA_HEAD_DIM
    return (jnp.where(d < ROPE_DIM, cos, 1.0), jnp.where(d < half, -sin, 0.0),
            jnp.where((d >= half) & (d < ROPE_DIM), sin, 0.0))


def _pad_cols(a, n):
    return jnp.pad(a, ((0, 0), (0, n - a.shape[1])))


def _local_step(x, target, positions, mods, W, P, on_grads):
    S, D = x.shape
    Hf = D // FOX_HEAD_DIM
    Hq = D // SWA_HEAD_DIM
    Hk = Hq // SWA_GROUP
    Wk = Hk * SWA_HEAD_DIM
    n_in = 3 * D + Hf
    (sh1a, sc1a, g1a, sh2a, sc2a, g2a), (sh1b, sc1b, g1b, sh2b, sc2b, g2b) = mods
    row = lambda v: v.reshape(1, -1)
    cw = [jnp.transpose(P["conv_w"][l].reshape(3, 2, -1), (1, 0, 2)) for l in range(2)]
    cb = [P["conv_b"][l].reshape(2, 1, -1) for l in range(2)]

    h1a = _modulate(x, sc1a, sh1a, "modulate_in")
    proj_a = _mm_nn(h1a, W["fox_in"], F32, "fox_in_proj", tn=896)
    flT = proj_a[:, 3 * D:n_in].T
    bf_col = P["fox_b_f"].reshape(Hf, 1)
    cumT = _fox_prep(flT, bf_col, "fox_cumsum")
    cum_col, cum_row = cumT.reshape(Hf, S, 1), cumT.reshape(Hf, 1, S)
    o_a, lse_a = _fox_attn_fwd(proj_a, cum_col, cum_row, Hf, "fox_attn_fwd")
    y1a = _mm_nn(o_a, W["fox_o"], F32, "fox_out_proj")
    z1a, x1, h2a = _ln_fwd(x, y1a, g1a, row(P["ln_mix_g"][0]), row(P["ln_mix_b"][0]), sc2a, sh2a, "ln_mix0")
    u_a, a_a = _ffn_up(h2a, W["up"][0], cw[0], cb[0], "ffn_up0")
    y2a = _mm_nn(a_a, W["down"][0], F32, "ffn_down0", tk=1408)
    z2a, x2, h1b = _ln_fwd(x1, y2a, g2a, row(P["ln_ffn_g"][0]), row(P["ln_ffn_b"][0]), sc1b, sh1b, "ln_ffn0")

    proj_b = _mm_nn(h1b, W["swa_in"], F32, "swa_in_proj")
    tabs_f = _rope_tables(positions, 1.0)
    tabs_b = _rope_tables(positions, -1.0)
    qk = _rope([(proj_b[:, :D + Wk], True)], tabs_f, F32, "rope_fwd")
    sinks = _pad_cols(P["swa_sinks"].reshape(1, Hq), LANE)
    o_b, lse_b = _swa_attn_fwd(qk, proj_b, sinks, Hq, "swa_attn_fwd")
    y1b = _mm_nn(o_b, W["swa_o"], F32, "swa_out_proj")
    z1b, x3, h2b = _ln_fwd(x2, y1b, g1b, row(P["ln_mix_g"][1]), row(P["ln_mix_b"][1]), sc2b, sh2b, "ln_mix1")
    u_b, a_b = _ffn_up(h2b, W["up"][1], cw[1], cb[1], "ffn_up1")
    y2b = _mm_nn(a_b, W["down"][1], F32, "ffn_down1", tk=1408)
    z2b, dout, loss_row = _ln_fwd_loss(x3, y2b, g2b, row(P["ln_ffn_g"][1]), row(P["ln_ffn_b"][1]), target, "ln_ffn1_loss")

    def ffn_backward(dy, a, u, h_in, l, tag):
        da = _mm_nt(dy[None], W["down"][l][None], F32, "ffn_da" + tag)
        d_down = _mm_tn(a, dy[None], BF16, "ffn_dwdown" + tag, tm=1408, tn=1024)[0]
        du, dcw, dcb = _ffn_bwd_elem(da, u, cw[l], cb[l], "ffn_bwd_elem" + tag)
        dh = _mm_nt(du, W["up"][l], F32, "ffn_dh" + tag, tk=1408)
        d_up = _mm_tn(h_in, du, BF16, "ffn_dwup" + tag)
        return dh, d_up, d_down, jnp.transpose(dcw, (1, 0, 2)).reshape(3, -1), dcb.reshape(-1)

    dz2b, dy2b, dg_f1, db_f1, dgate2b = _ln_bwd(dout, z2b, y2b, g2b, row(P["ln_ffn_g"][1]), "ln_ffn1_bwd")
    dh2b, d_up1, d_down1, dcw1, dcb1 = ffn_backward(dy2b, a_b, u_b, h2b, 1, "1")
    dh2b = _tie(dh2b, on_grads("ffn1", dict(up=d_up1, down=d_down1)))
    dx3, dsc2b, dsh2b = _mod_bwd(dz2b, dh2b, x3, sc2b, "mod_ffn1_bwd")

    dz1b, dy1b, dg_m1, db_m1, dgate1b = _ln_bwd(dx3, z1b, y1b, g1b, row(P["ln_mix_g"][1]), "ln_mix1_bwd")
    do_b = _mm_nt(dy1b[None], W["swa_o"][None], F32, "swa_do")
    d_swa_o = _mm_tn(o_b, dy1b[None], BF16, "swa_dwo")[0]
    dq_b, dk_b, dv_b, dsinks = _swa_attn_bwd(qk, proj_b, sinks, do_b, lse_b, Hq, "swa_attn_bwd")
    dproj_b = _rope([(dq_b, True), (dk_b, True), (dv_b, False)], tabs_b, BF16, "rope_bwd")
    dh1b = _mm_nt(dproj_b[None], W["swa_in"][None], F32, "swa_dh", tk=1280)
    d_swa_in = _mm_tn(h1b, dproj_b[None], BF16, "swa_dwin")[0]
    dh1b = _tie(dh1b, on_grads("swa", dict(w_in=d_swa_in, w_o=d_swa_o)))
    dx2, dsc1b, dsh1b = _mod_bwd(dz1b, dh1b, x2, sc1b, "mod_mix1_bwd")

    dz2a, dy2a, dg_f0, db_f0, dgate2a = _ln_bwd(dx2, z2a, y2a, g2a, row(P["ln_ffn_g"][0]), "ln_ffn0_bwd")
    dh2a, d_up0, d_down0, dcw0, dcb0 = ffn_backward(dy2a, a_a, u_a, h2a, 0, "0")
    dh2a = _tie(dh2a, on_grads("ffn0", dict(up=d_up0, down=d_down0)))
    dx1, dsc2a, dsh2a = _mod_bwd(dz2a, dh2a, x1, sc2a, "mod_ffn0_bwd")

    dz1a, dy1a, dg_m0, db_m0, dgate1a = _ln_bwd(dx1, z1a, y1a, g1a, row(P["ln_mix_g"][0]), "ln_mix0_bwd")
    do_a = _mm_nt(dy1a[None], W["fox_o"][None], F32, "fox_do")
    d_fox_o = _mm_tn(o_a, dy1a[None], BF16, "fox_dwo")[0]
    dq_a, dk_a, dv_a, dcum_row = _fox_attn_bwd(proj_a, do_a, cum_col, cum_row, lse_a, Hf, "fox_attn_bwd")
    dflT, dbf = _fox_prep_bwd(dcum_row.reshape(Hf, S), flT, bf_col, "fox_cumsum_bwd")
    n_pad = W["fox_in"].shape[1]
    dproj_a = jnp.concatenate([dq_a, dk_a.astype(BF16), dv_a.astype(BF16),
                               _pad_cols(dflT.T, n_pad - 3 * D).astype(BF16)], axis=1)
    dh1a = _mm_nt(dproj_a[None], W["fox_in"][None], F32, "fox_dh", tk=896)
    d_fox_in = _mm_tn(h1a, dproj_a[None], BF16, "fox_dwin", tn=896)[0]
    dh1a = _tie(dh1a, on_grads("fox", dict(w_in=d_fox_in, w_o=d_fox_o)))
    grad_x, dsc1a, dsh1a = _mod_bwd(dz1a, dh1a, x, sc1a, "mod_mix0_bwd")

    dmod = jnp.stack([jnp.concatenate([dsh1a, dsc1a, dgate1a, dsh2a, dsc2a, dgate2a], axis=1)[0],
                      jnp.concatenate([dsh1b, dsc1b, dgate1b, dsh2b, dsc2b, dgate2b], axis=1)[0]])
    small = dict(dmod=dmod, conv_b=jnp.stack([dcb0, dcb1]), conv_w=jnp.stack([dcw0, dcw1]),
                 ln_mix_g=jnp.concatenate([dg_m0, dg_m1]), ln_mix_b=jnp.concatenate([db_m0, db_m1]),
                 ln_ffn_g=jnp.concatenate([dg_f0, dg_f1]), ln_ffn_b=jnp.concatenate([db_f0, db_f1]),
                 fox_b_f=dbf.reshape(-1), swa_sinks=dsinks[0, :Hq])
    return loss_row[0, 0], grad_x, small


SMALL_ORDER = ("dmod", "conv_b", "conv_w", "ln_mix_g", "ln_mix_b", "ln_ffn_g", "ln_ffn_b", "fox_b_f", "swa_sinks")


def _pack_rows(arrays):
    chunks, spans, off = [], [], 0
    for a in arrays:
        flat = a.reshape(-1)
        n = -(-flat.shape[0] // LANE) * LANE
        chunks.append(jnp.pad(flat, (0, n - flat.shape[0])))
        spans.append((off, flat.shape[0], a.shape))
        off += n
    total = -(-off // (8 * LANE)) * (8 * LANE)
    chunks.append(jnp.zeros((total - off,), F32))
    return jnp.concatenate(chunks).reshape(-1, LANE), spans


def _unpack_rows(packed, spans):
    flat = packed.reshape(-1)
    return [flat[off:off + n].reshape(shape) for off, n, shape in spans]


def kernel(x, c, positions, fox_w_in, fox_b_f, fox_w_o, swa_w_in, swa_sinks, swa_w_o, ada_w, ada_b, ffn_w_up, ffn_conv_w, ffn_conv_b, ffn_w_down, ln_mix_g, ln_mix_b, ln_ffn_g, ln_ffn_b, loss_target, m_fox_w_in, m_fox_b_f, m_fox_w_o, m_swa_w_in, m_swa_sinks, m_swa_w_o, m_ada_w, m_ada_b, m_ffn_w_up, m_ffn_conv_w, m_ffn_conv_b, m_ffn_w_down, m_ln_mix_g, m_ln_mix_b, m_ln_ffn_g, m_ln_ffn_b, v_fox_w_in, v_fox_b_f, v_fox_w_o, v_swa_w_in, v_swa_sinks, v_swa_w_o, v_ada_w, v_ada_b, v_ffn_w_up, v_ffn_conv_w, v_ffn_conv_b, v_ffn_w_down, v_ln_mix_g, v_ln_mix_b, v_ln_ffn_g, v_ln_ffn_b):
    S, D = x.shape[1], x.shape[2]
    L = ada_w.shape[0]
    me = 4 * lax.axis_index("x") + 2 * lax.axis_index("y") + lax.axis_index("c")
    n_ada = ada_w.shape[2]
    cu = ffn_w_up.shape[2]
    F = 4 * cu
    n_in = fox_w_in.shape[2] * N_DEV
    n_in_pad = -(-n_in // LANE) * LANE

    c_all = _all_gather_small(c.reshape(-1, LANE), "gather_c").reshape(N_DEV, D)
    b_cols = lax.dynamic_slice_in_dim(ada_b, me * n_ada, n_ada, axis=1).reshape(L, 1, n_ada)
    mod_blk = _ada_mod(jnp.pad(c_all, ((0, ADA_ROWS - N_DEV), (0, 0))), ada_w, b_cols, "ada_mod")[:, :N_DEV]
    mod_all = _all_gather_small(mod_blk.reshape(-1, LANE), "gather_mod").reshape(N_DEV, L, N_DEV, n_ada)
    mod_mine = lax.dynamic_index_in_dim(mod_all, me, axis=2, keepdims=False)
    mod_mine = jnp.transpose(mod_mine, (1, 0, 2)).reshape(L, N_DEV * n_ada)
    mods = [[mod_mine[l, k * D:(k + 1) * D].reshape(1, D) for k in range(6)] for l in range(L)]

    g_fi, g_fo, g_si, g_so, g_up, g_dn = _all_gather_weights(
        fox_w_in[0].astype(BF16), fox_w_o[0].astype(BF16), swa_w_in[0].astype(BF16), swa_w_o[0].astype(BF16),
        ffn_w_up.astype(BF16), ffn_w_down.astype(BF16), "gather_weights")
    W = dict(fox_in=_pad_cols(jnp.transpose(g_fi, (1, 0, 2)).reshape(D, n_in), n_in_pad), fox_o=g_fo,
             swa_in=jnp.transpose(g_si, (1, 0, 2)).reshape(D, -1), swa_o=g_so,
             up=[g_up[l] for l in range(L)], down=[g_dn[l] for l in range(L)])

    conv_w_all = None
    P = dict(fox_b_f=fox_b_f[0], swa_sinks=swa_sinks[0], conv_b=ffn_conv_b,
             ln_mix_g=ln_mix_g, ln_mix_b=ln_mix_b, ln_ffn_g=ln_ffn_g, ln_ffn_b=ln_ffn_b)
    cw_rows = _all_gather_small(_pack_rows([ffn_conv_w])[0], "gather_conv_w")
    n_cw = ffn_conv_w.size
    cw_dev = cw_rows.reshape(N_DEV, -1)[:, :n_cw].reshape(N_DEV, L, 3, cu)
    conv_w_all = jnp.transpose(cw_dev, (1, 2, 0, 3)).reshape(L, 3, N_DEV * cu)
    P["conv_w"] = conv_w_all

    out, pending = {}, {}

    def shard_major(g):
        return jnp.transpose(g.reshape(D, N_DEV, -1), (1, 0, 2))

    def start(group, srcs, kinds):
        send, recv, thru, lands, token = _scatter_start(srcs, kinds, "scatter_start_" + group)
        pending[group] = (send, recv, thru, lands, kinds)
        return token

    def finish(group, after):
        send, recv, thru, lands, kinds = pending.pop(group)
        return _scatter_wait(send, recv, thru, lands, kinds, after, "scatter_wait_" + group)

    def adam_ffn(r_up, r_dn, layer):
        tag = str(layer)
        out["ffn_w_up"] = _adam_sum(r_up, ffn_w_up, m_ffn_w_up, v_ffn_w_up, layer, out.get("ffn_w_up"), "adam_ffn_w_up" + tag)
        out["ffn_w_down"] = _adam_sum(r_dn, ffn_w_down, m_ffn_w_down, v_ffn_w_down, layer, out.get("ffn_w_down"),
                                      "adam_ffn_w_down" + tag)

    def on_grads(group, g):
        if group == "ffn1":
            return start("ffn1", [g["up"], g["down"]], ["halves", "rows"])
        if group == "swa":
            return start("swa", [shard_major(g["w_in"]), g["w_o"]], ["major", "rows"])
        if group == "ffn0":
            token = start("ffn0", [g["up"], g["down"]], ["halves", "rows"])
            adam_ffn(*finish("ffn1", token), 1)
            return token
        token = start("fox", [shard_major(g["w_in"][:, :n_in]), g["w_o"]], ["major", "rows"])
        r_si, r_so = finish("swa", token)
        out["swa_w_in"] = _adam_sum(r_si, swa_w_in, m_swa_w_in, v_swa_w_in, 0, None, "adam_swa_w_in")
        out["swa_w_o"] = _adam_sum(r_so, swa_w_o, m_swa_w_o, v_swa_w_o, 0, None, "adam_swa_w_o")
        adam_ffn(*finish("ffn0", token), 0)
        return token

    loss_local, grad_x, small = _local_step(x[0], loss_target[0], positions[0], mods, W, P, on_grads)
    loss = lax.psum(loss_local, AXES)
    r_fi, r_fo = finish("fox", grad_x)
    out["fox_w_in"] = _adam_sum(r_fi, fox_w_in, m_fox_w_in, v_fox_w_in, 0, None, "adam_fox_w_in")
    out["fox_w_o"] = _adam_sum(r_fo, fox_w_o, m_fox_w_o, v_fox_w_o, 0, None, "adam_fox_w_o")

    packed, spans = _pack_rows([small[k] for k in SMALL_ORDER])
    gathered = _all_gather_small(packed, "gather_small_grads")
    totals = dict(zip(SMALL_ORDER, _unpack_rows(_sum_slots(gathered, "sum_small_grads"), spans)))
    n_mod = L * 6 * D
    dmod_all = gathered.reshape(N_DEV, -1)[:, :n_mod].reshape(N_DEV, L, 6 * D)
    dmod_cols = jnp.transpose(lax.dynamic_slice_in_dim(dmod_all, me * n_ada, n_ada, axis=2), (1, 0, 2))
    g_ada_w = _ada_bwd(c_all.T, dmod_cols, "ada_w_grad")
    out["ada_w"] = (g_ada_w,) + tuple(_adam(g_ada_w, ada_w, m_ada_w, v_ada_w, "adam_ada_w"))

    g_small = dict(fox_b_f=totals["fox_b_f"].reshape(fox_b_f.shape), swa_sinks=totals["swa_sinks"].reshape(swa_sinks.shape),
                   ada_b=totals["dmod"].reshape(ada_b.shape), ffn_conv_b=totals["conv_b"].reshape(ffn_conv_b.shape),
                   ffn_conv_w=lax.dynamic_slice_in_dim(totals["conv_w"].reshape(L, 3, 2 * F), me * cu, cu, axis=2),
                   ln_mix_g=totals["ln_mix_g"], ln_mix_b=totals["ln_mix_b"],
                   ln_ffn_g=totals["ln_ffn_g"], ln_ffn_b=totals["ln_ffn_b"])
    small_names = ("fox_b_f", "swa_sinks", "ada_b", "ffn_conv_b", "ffn_conv_w", "ln_mix_g", "ln_mix_b", "ln_ffn_g", "ln_ffn_b")
    w_small = dict(fox_b_f=(fox_b_f, m_fox_b_f, v_fox_b_f), swa_sinks=(swa_sinks, m_swa_sinks, v_swa_sinks),
                   ada_b=(ada_b, m_ada_b, v_ada_b), ffn_conv_b=(ffn_conv_b, m_ffn_conv_b, v_ffn_conv_b),
                   ffn_conv_w=(ffn_conv_w, m_ffn_conv_w, v_ffn_conv_w),
                   ln_mix_g=(ln_mix_g, m_ln_mix_g, v_ln_mix_g), ln_mix_b=(ln_mix_b, m_ln_mix_b, v_ln_mix_b),
                   ln_ffn_g=(ln_ffn_g, m_ln_ffn_g, v_ln_ffn_g), ln_ffn_b=(ln_ffn_b, m_ln_ffn_b, v_ln_ffn_b))
    pk_g, sp = _pack_rows([g_small[k] for k in small_names])
    pk_w = _pack_rows([w_small[k][0] for k in small_names])[0]
    pk_m = _pack_rows([w_small[k][1] for k in small_names])[0]
    pk_v = _pack_rows([w_small[k][2] for k in small_names])[0]
    res = _adam(pk_g[None], pk_w[None], pk_m[None], pk_v[None], "adam_small")
    res = [dict(zip(small_names, _unpack_rows(r[0], sp))) for r in res]
    for k in small_names:
        out[k] = (g_small[k], res[0][k], res[1][k], res[2][k])

    order = ("fox_w_in", "fox_b_f", "fox_w_o", "swa_w_in", "swa_sinks", "swa_w_o", "ada_w", "ada_b", "ffn_w_up",
             "ffn_conv_w", "ffn_conv_b", "ffn_w_down", "ln_mix_g", "ln_mix_b", "ln_ffn_g", "ln_ffn_b")
    return (loss, grad_x[None], *[out[k][0] for k in order], *[out[k][1] for k in order],
            *[out[k][2] for k in order], *[out[k][3] for k in order])
```

```python
import functools

import jax
import jax.numpy as jnp
from jax import lax
from jax.experimental import pallas as pl
from jax.experimental.pallas import tpu as pltpu

F32 = jnp.float32
BF16 = jnp.bfloat16
MESH = pl.DeviceIdType.MESH
N_DEV = 8
AXES = ("x", "y", "c")

DEPTH = 2
ALPHA = (2.0 * DEPTH) ** 0.25
LN_EPS = 1e-5
FOX_HEAD_DIM = 128
SWA_HEAD_DIM = 64
SWA_GROUP = 8
SWA_WINDOW = 128
Q_BLOCK = 128
ROPE_DIM = 16
ROPE_THETA = 500000.0

ADAM_LR = 0.001
ADAM_B1 = 0.9
ADAM_B2 = 0.999
ADAM_EPS = 1e-08
ADAM_WD = 0.01
ADAM_STEP = 10

LANE = 128
MIB = 1024 * 1024


def _tile(n, pref, unit=LANE):
    if n <= pref:
        return n
    t = (pref // unit) * unit
    while t >= unit:
        if n % t == 0:
            return t
        t -= unit
    return n


def _params(sem, vmem_mib=48):
    return pltpu.CompilerParams(dimension_semantics=sem, vmem_limit_bytes=vmem_mib * MIB)


def _sigmoid(x):
    return 1.0 / (1.0 + jnp.exp(-x))


def _mm_call(dot, grid_mnk, in_specs, out_spec, out_shape, k_axis, nk, tm, tn, name, operands):
    sem = ("parallel",) * (len(grid_mnk) - 1) + ("arbitrary",)

    if nk == 1:
        def body(a_ref, b_ref, o_ref):
            o_ref[...] = dot(a_ref[...], b_ref[...]).astype(o_ref.dtype)
        scratch = []
    else:
        def body(a_ref, b_ref, o_ref, acc_ref):
            k = pl.program_id(k_axis)

            @pl.when(k == 0)
            def _():
                acc_ref[...] = jnp.zeros_like(acc_ref)

            acc_ref[...] += dot(a_ref[...], b_ref[...])

            @pl.when(k == nk - 1)
            def _():
                o_ref[...] = acc_ref[...].astype(o_ref.dtype)
        scratch = [pltpu.VMEM((tm, tn), F32)]

    return pl.pallas_call(
        body, name=name, grid=grid_mnk, in_specs=in_specs, out_specs=out_spec, out_shape=out_shape,
        scratch_shapes=scratch, compiler_params=_params(sem, 56),
    )(*operands)


def _dot(dims):
    def dot(a, b):
        return lax.dot_general(a.astype(BF16), b.astype(BF16), (dims, ((), ())), preferred_element_type=F32)
    return dot


def _mm_nn(a, b, out_dtype, name, tm=2048, tn=512, tk=2048):
    M, K = a.shape
    N = b.shape[1]
    tm, tn, tk = _tile(M, tm), _tile(N, tn), _tile(K, tk)
    nk = K // tk
    return _mm_call(
        _dot(((1,), (0,))), (M // tm, N // tn, nk),
        [pl.BlockSpec((tm, tk), lambda i, j, k: (i, k)), pl.BlockSpec((tk, tn), lambda i, j, k: (k, j))],
        pl.BlockSpec((tm, tn), lambda i, j, k: (i, j)), jax.ShapeDtypeStruct((M, N), out_dtype),
        2, nk, tm, tn, name, (a, b))


def _mm_nt(a, b, out_dtype, name, tm=2048, tn=512, tk=2048):
    P, M, K = a.shape
    N = b.shape[1]
    tm, tn, tk = _tile(M, tm), _tile(N, tn), _tile(K, tk)
    nk = K // tk
    return _mm_call(
        _dot(((1,), (1,))), (M // tm, N // tn, P * nk),
        [pl.BlockSpec((None, tm, tk), lambda i, j, k: (k // nk, i, k % nk)),
         pl.BlockSpec((None, tn, tk), lambda i, j, k: (k // nk, j, k % nk))],
        pl.BlockSpec((tm, tn), lambda i, j, k: (i, j)), jax.ShapeDtypeStruct((M, N), out_dtype),
        2, P * nk, tm, tn, name, (a, b))


def _mm_tn(a, b, out_dtype, name, tm=2048, tn=512, tk=2048):
    K, M = a.shape
    P, _, N = b.shape
    tm, tn, tk = _tile(M, tm), _tile(N, tn), _tile(K, tk)
    nk = K // tk
    return _mm_call(
        _dot(((0,), (0,))), (P, M // tm, N // tn, nk),
        [pl.BlockSpec((tk, tm), lambda p, i, j, k: (k, i)), pl.BlockSpec((None, tk, tn), lambda p, i, j, k: (p, k, j))],
        pl.BlockSpec((None, tm, tn), lambda p, i, j, k: (p, i, j)), jax.ShapeDtypeStruct((P, M, N), out_dtype),
        3, nk, tm, tn, name, (a, b))


ROW_TILE = 256


def _row_spec(tm, D):
    return pl.BlockSpec((tm, D), lambda i: (i, 0))


def _vec_spec(D):
    return pl.BlockSpec((1, D), lambda i: (0, 0))


def _modulate(x, sc, sh, name):
    S, D = x.shape
    tm = _tile(S, ROW_TILE, 8)

    def body(x_ref, sc_ref, sh_ref, h_ref):
        h_ref[...] = (x_ref[...] * (1.0 + sc_ref[...]) + sh_ref[...]).astype(BF16)

    return pl.pallas_call(
        body, name=name, grid=(S // tm,),
        in_specs=[_row_spec(tm, D), _vec_spec(D), _vec_spec(D)],
        out_specs=_row_spec(tm, D),
        out_shape=jax.ShapeDtypeStruct((S, D), BF16),
        compiler_params=_params(("parallel",)),
    )(x, sc, sh)


def _layer_norm_rows(z, gamma, beta):
    mu = jnp.mean(z, axis=-1, keepdims=True)
    zc = z - mu
    var = jnp.mean(zc * zc, axis=-1, keepdims=True)
    return zc * lax.rsqrt(var + LN_EPS) * gamma + beta


def _ln_fwd(x, y, gate, gamma, beta, sc_n, sh_n, name):
    S, D = x.shape
    tm = _tile(S, ROW_TILE, 8)

    def body(x_ref, y_ref, gate_ref, g_ref, b_ref, sc_ref, sh_ref, z_ref, xo_ref, hn_ref):
        z = ALPHA * x_ref[...] + (1.0 + gate_ref[...]) * y_ref[...]
        xo = _layer_norm_rows(z, g_ref[...], b_ref[...])
        z_ref[...] = z
        xo_ref[...] = xo
        hn_ref[...] = (xo * (1.0 + sc_ref[...]) + sh_ref[...]).astype(BF16)

    return pl.pallas_call(
        body, name=name, grid=(S // tm,),
        in_specs=[_row_spec(tm, D), _row_spec(tm, D)] + [_vec_spec(D)] * 5,
        out_specs=[_row_spec(tm, D)] * 3,
        out_shape=[jax.ShapeDtypeStruct((S, D), F32), jax.ShapeDtypeStruct((S, D), F32),
                   jax.ShapeDtypeStruct((S, D), BF16)],
        compiler_params=_params(("parallel",)),
    )(x, y, gate, gamma, beta, sc_n, sh_n)


def _ln_fwd_loss(x, y, gate, gamma, beta, target, name):
    S, D = x.shape
    tm = _tile(S, ROW_TILE, 8)

    def body(x_ref, y_ref, gate_ref, g_ref, b_ref, t_ref, z_ref, dout_ref, loss_ref):
        @pl.when(pl.program_id(0) == 0)
        def _():
            loss_ref[...] = jnp.zeros_like(loss_ref)

        z = ALPHA * x_ref[...] + (1.0 + gate_ref[...]) * y_ref[...]
        xo = _layer_norm_rows(z, g_ref[...], b_ref[...])
        err = xo - t_ref[...]
        z_ref[...] = z
        dout_ref[...] = err * (1.0 / D)
        loss_ref[...] += (0.5 / D) * jnp.sum(err * err)

    return pl.pallas_call(
        body, name=name, grid=(S // tm,),
        in_specs=[_row_spec(tm, D), _row_spec(tm, D)] + [_vec_spec(D)] * 3 + [_row_spec(tm, D)],
        out_specs=[_row_spec(tm, D), _row_spec(tm, D), pl.BlockSpec((1, LANE), lambda i: (0, 0))],
        out_shape=[jax.ShapeDtypeStruct((S, D), F32), jax.ShapeDtypeStruct((S, D), F32),
                   jax.ShapeDtypeStruct((1, LANE), F32)],
        compiler_params=_params(("arbitrary",)),
    )(x, y, gate, gamma, beta, target)


def _ln_bwd(dout, z, y, gate, gamma, name):
    S, D = z.shape
    tm = _tile(S, ROW_TILE, 8)

    def body(dout_ref, z_ref, y_ref, gate_ref, g_ref, dz_ref, dy_ref, dg_ref, db_ref, dgate_ref):
        @pl.when(pl.program_id(0) == 0)
        def _():
            dg_ref[...] = jnp.zeros_like(dg_ref)
            db_ref[...] = jnp.zeros_like(db_ref)
            dgate_ref[...] = jnp.zeros_like(dgate_ref)

        z = z_ref[...]
        dout = dout_ref[...]
        mu = jnp.mean(z, axis=-1, keepdims=True)
        zc = z - mu
        var = jnp.mean(zc * zc, axis=-1, keepdims=True)
        rstd = lax.rsqrt(var + LN_EPS)
        xhat = zc * rstd
        dxhat = dout * g_ref[...]
        m1 = jnp.mean(dxhat, axis=-1, keepdims=True)
        m2 = jnp.mean(dxhat * xhat, axis=-1, keepdims=True)
        dz = rstd * (dxhat - m1 - xhat * m2)
        dz_ref[...] = dz
        dy_ref[...] = (dz * (1.0 + gate_ref[...])).astype(BF16)
        dg_ref[...] += jnp.sum(dout * xhat, axis=0, keepdims=True)
        db_ref[...] += jnp.sum(dout, axis=0, keepdims=True)
        dgate_ref[...] += jnp.sum(dz * y_ref[...], axis=0, keepdims=True)

    return pl.pallas_call(
        body, name=name, grid=(S // tm,),
        in_specs=[_row_spec(tm, D)] * 3 + [_vec_spec(D)] * 2,
        out_specs=[_row_spec(tm, D), _row_spec(tm, D)] + [_vec_spec(D)] * 3,
        out_shape=[jax.ShapeDtypeStruct((S, D), F32), jax.ShapeDtypeStruct((S, D), BF16)]
        + [jax.ShapeDtypeStruct((1, D), F32)] * 3,
        compiler_params=_params(("arbitrary",)),
    )(dout, z, y, gate, gamma)


def _mod_bwd(dz, dh, xin, sc, after, name):
    S, D = dz.shape
    tm = _tile(S, ROW_TILE, 8)

    def body(dz_ref, dh_ref, x_ref, sc_ref, after_ref, dx_ref, dsc_ref, dsh_ref):
        @pl.when(pl.program_id(0) == 0)
        def _():
            dsc_ref[...] = jnp.zeros_like(dsc_ref)
            dsh_ref[...] = jnp.zeros_like(dsh_ref)

        dh = dh_ref[...]
        dx_ref[...] = ALPHA * dz_ref[...] + dh * (1.0 + sc_ref[...])
        dsc_ref[...] += jnp.sum(dh * x_ref[...], axis=0, keepdims=True)
        dsh_ref[...] += jnp.sum(dh, axis=0, keepdims=True)

    return pl.pallas_call(
        body, name=name, grid=(S // tm,),
        in_specs=[_row_spec(tm, D)] * 3 + [_vec_spec(D), pl.BlockSpec(memory_space=pl.ANY)],
        out_specs=[_row_spec(tm, D), _vec_spec(D), _vec_spec(D)],
        out_shape=[jax.ShapeDtypeStruct((S, D), F32)] + [jax.ShapeDtypeStruct((1, D), F32)] * 2,
        compiler_params=_params(("arbitrary",)),
    )(dz, dh, xin, sc, after)


def _shift_down(u, k, row):
    return jnp.where(row >= k, pltpu.roll(u, k, axis=0), 0.0)


def _shift_up(u, k, row, S):
    return jnp.where(row < S - k, pltpu.roll(u, S - k, axis=0), 0.0)


def _ffn_up(h, w, cw, cb, name):
    S, D = h.shape
    F = w.shape[2]
    tn = _tile(F, 256)

    def body(h_ref, w_ref, cw_ref, cb_ref, u_ref, a_ref):
        hh = h_ref[...]
        row = lax.broadcasted_iota(jnp.int32, (S, tn), 0)
        conv = []
        for p in range(2):
            u = jnp.dot(hh, w_ref[p], preferred_element_type=F32)
            u_ref[p] = u
            cwp = cw_ref[p]
            conv.append(_shift_down(u, 2, row) * cwp[0:1] + _shift_down(u, 1, row) * cwp[1:2]
                        + u * cwp[2:3] + cb_ref[p])
        g, v = conv
        a_ref[...] = (g * _sigmoid(g) * v).astype(BF16)

    return pl.pallas_call(
        body, name=name, grid=(F // tn,),
        in_specs=[pl.BlockSpec((S, D), lambda j: (0, 0)), pl.BlockSpec((2, D, tn), lambda j: (0, 0, j)),
                  pl.BlockSpec((2, 3, tn), lambda j: (0, 0, j)), pl.BlockSpec((2, 1, tn), lambda j: (0, 0, j))],
        out_specs=[pl.BlockSpec((2, S, tn), lambda j: (0, 0, j)), pl.BlockSpec((S, tn), lambda j: (0, j))],
        out_shape=[jax.ShapeDtypeStruct((2, S, F), F32), jax.ShapeDtypeStruct((S, F), BF16)],
        compiler_params=_params(("parallel",), 56),
    )(h, w, cw, cb)


def _ffn_bwd_elem(da, u, cw, cb, name):
    _, S, F = u.shape
    tn = _tile(F, 256)

    def body(da_ref, u_ref, cw_ref, cb_ref, du_ref, dcw_ref, dcb_ref):
        row = lax.broadcasted_iota(jnp.int32, (S, tn), 0)
        da = da_ref[...]
        shifted, conv = [], []
        for p in range(2):
            u = u_ref[p]
            u1, u2 = _shift_down(u, 1, row), _shift_down(u, 2, row)
            cwp = cw_ref[p]
            shifted.append((u2, u1, u))
            conv.append(u2 * cwp[0:1] + u1 * cwp[1:2] + u * cwp[2:3] + cb_ref[p])
        g, v = conv
        sg = _sigmoid(g)
        d_conv = (da * v * (sg * (1.0 + g * (1.0 - sg))), da * (g * sg))
        for p in range(2):
            d = d_conv[p]
            cwp = cw_ref[p]
            dcb_ref[p] = jnp.sum(d, axis=0, keepdims=True)
            for j in range(3):
                dcw_ref[p, j:j + 1, :] = jnp.sum(d * shifted[p][j], axis=0, keepdims=True)
            du = d * cwp[2:3] + _shift_up(d, 1, row, S) * cwp[1:2] + _shift_up(d, 2, row, S) * cwp[0:1]
            du_ref[p] = du.astype(BF16)

    return pl.pallas_call(
        body, name=name, grid=(F // tn,),
        in_specs=[pl.BlockSpec((S, tn), lambda j: (0, j)), pl.BlockSpec((2, S, tn), lambda j: (0, 0, j)),
                  pl.BlockSpec((2, 3, tn), lambda j: (0, 0, j)), pl.BlockSpec((2, 1, tn), lambda j: (0, 0, j))],
        out_specs=[pl.BlockSpec((2, S, tn), lambda j: (0, 0, j)), pl.BlockSpec((2, 3, tn), lambda j: (0, 0, j)),
                   pl.BlockSpec((2, 1, tn), lambda j: (0, 0, j))],
        out_shape=[jax.ShapeDtypeStruct((2, S, F), BF16), jax.ShapeDtypeStruct((2, 3, F), F32),
                   jax.ShapeDtypeStruct((2, 1, F), F32)],
        compiler_params=_params(("parallel",), 56),
    )(da, u, cw, cb)


def _split3(x):
    hi = x.astype(BF16)
    r1 = x - hi.astype(F32)
    mid = r1.astype(BF16)
    lo = (r1 - mid.astype(F32)).astype(BF16)
    return hi, mid, lo


def _tri_matmul(x, upper, S):
    tc = _tile(S, 512)
    parts = _split3(x)
    outs = []
    for b in range(S // tc):
        r = lax.broadcasted_iota(jnp.int32, (S, tc), 0)
        c = lax.broadcasted_iota(jnp.int32, (S, tc), 1) + b * tc
        tri = jnp.where((r <= c) if upper else (r >= c), 1.0, 0.0).astype(BF16)
        acc = jnp.dot(parts[0], tri, preferred_element_type=F32)
        acc += jnp.dot(parts[1], tri, preferred_element_type=F32)
        acc += jnp.dot(parts[2], tri, preferred_element_type=F32)
        outs.append(acc)
    return outs, tc


def _fox_prep(flT, bf, name):
    H, S = flT.shape

    def body(fl_ref, b_ref, cum_ref):
        zz = fl_ref[...] + b_ref[...]
        lf = jnp.minimum(zz, 0.0) - jnp.log(1.0 + jnp.exp(-jnp.abs(zz)))
        outs, tc = _tri_matmul(lf, True, S)
        for b, o in enumerate(outs):
            cum_ref[:, b * tc:(b + 1) * tc] = o

    return pl.pallas_call(
        body, name=name,
        in_specs=[pl.BlockSpec(memory_space=pltpu.VMEM)] * 2,
        out_specs=pl.BlockSpec(memory_space=pltpu.VMEM),
        out_shape=jax.ShapeDtypeStruct((H, S), F32),
        compiler_params=pltpu.CompilerParams(vmem_limit_bytes=48 * MIB),
    )(flT, bf)


def _fox_prep_bwd(dcumT, flT, bf, name):
    H, S = flT.shape

    def body(dc_ref, fl_ref, b_ref, dfl_ref, dbf_ref):
        zz = fl_ref[...] + b_ref[...]
        outs, tc = _tri_matmul(dc_ref[...], False, S)
        total = jnp.zeros((H, 1), F32)
        for b, o in enumerate(outs):
            dfl = o * _sigmoid(-zz[:, b * tc:(b + 1) * tc])
            dfl_ref[:, b * tc:(b + 1) * tc] = dfl
            total += jnp.sum(dfl, axis=1, keepdims=True)
        dbf_ref[...] = total

    return pl.pallas_call(
        body, name=name,
        in_specs=[pl.BlockSpec(memory_space=pltpu.VMEM)] * 3,
        out_specs=[pl.BlockSpec(memory_space=pltpu.VMEM)] * 2,
        out_shape=[jax.ShapeDtypeStruct((H, S), F32), jax.ShapeDtypeStruct((H, 1), F32)],
        compiler_params=pltpu.CompilerParams(vmem_limit_bytes=48 * MIB),
    )(dcumT, flT, bf)


FOX_TQ = 256


def _fox_scores(q_ref, k_ref, cq_ref, ck_ref, i, tq, S, scale):
    q = q_ref[...].astype(BF16)
    k = k_ref[...].astype(BF16)
    s = lax.dot_general(q, k, (((1,), (1,)), ((), ())), preferred_element_type=F32) * scale
    s = s + cq_ref[...] - ck_ref[...]
    qpos = i * tq + lax.broadcasted_iota(jnp.int32, (tq, S), 0)
    kpos = lax.broadcasted_iota(jnp.int32, (tq, S), 1)
    return jnp.where(kpos <= qpos, s, -jnp.inf), q, k


def _fox_attn_fwd(proj, cum_col, cum_row, H, name):
    S = proj.shape[0]
    dh = FOX_HEAD_DIM
    tq = _tile(S, FOX_TQ)
    scale = dh ** -0.5

    def body(q_ref, k_ref, v_ref, cq_ref, ck_ref, o_ref, lse_ref):
        s, _, _ = _fox_scores(q_ref, k_ref, cq_ref, ck_ref, pl.program_id(1), tq, S, scale)
        m = jnp.max(s, axis=-1, keepdims=True)
        p = jnp.exp(s - m)
        l = jnp.sum(p, axis=-1, keepdims=True)
        o = jnp.dot(p.astype(BF16), v_ref[...].astype(BF16), preferred_element_type=F32) / l
        o_ref[...] = o.astype(BF16)
        lse_ref[...] = m + jnp.log(l)

    return pl.pallas_call(
        body, name=name, grid=(H, S // tq),
        in_specs=[pl.BlockSpec((tq, dh), lambda h, i: (i, h)),
                  pl.BlockSpec((S, dh), lambda h, i: (0, H + h)),
                  pl.BlockSpec((S, dh), lambda h, i: (0, 2 * H + h)),
                  pl.BlockSpec((None, tq, 1), lambda h, i: (h, i, 0)),
                  pl.BlockSpec((None, 1, S), lambda h, i: (h, 0, 0))],
        out_specs=[pl.BlockSpec((tq, dh), lambda h, i: (i, h)),
                   pl.BlockSpec((None, tq, 1), lambda h, i: (h, i, 0))],
        out_shape=[jax.ShapeDtypeStruct((S, H * dh), BF16), jax.ShapeDtypeStruct((H, S, 1), F32)],
        compiler_params=_params(("parallel", "parallel")),
    )(proj, proj, proj, cum_col, cum_row)


def _fox_attn_bwd(proj, do, cum_col, cum_row, lse, H, name):
    S = proj.shape[0]
    dh = FOX_HEAD_DIM
    tq = _tile(S, FOX_TQ)
    scale = dh ** -0.5

    def body(q_ref, k_ref, v_ref, do_ref, cq_ref, ck_ref, lse_ref, dq_ref, dk_ref, dv_ref, dck_ref):
        i = pl.program_id(1)

        @pl.when(i == 0)
        def _():
            dk_ref[...] = jnp.zeros_like(dk_ref)
            dv_ref[...] = jnp.zeros_like(dv_ref)
            dck_ref[...] = jnp.zeros_like(dck_ref)

        s, q, k = _fox_scores(q_ref, k_ref, cq_ref, ck_ref, i, tq, S, scale)
        p = jnp.exp(s - lse_ref[...])
        do_b = do_ref[...].astype(BF16)
        dp = lax.dot_general(do_b, v_ref[...].astype(BF16), (((1,), (1,)), ((), ())), preferred_element_type=F32)
        delta = jnp.sum(p * dp, axis=-1, keepdims=True)
        ds = p * (dp - delta)
        ds_b = ds.astype(BF16)
        dq_ref[...] = (jnp.dot(ds_b, k, preferred_element_type=F32) * scale).astype(BF16)
        dk_ref[...] += lax.dot_general(ds_b, q, (((0,), (0,)), ((), ())), preferred_element_type=F32) * scale
        dv_ref[...] += lax.dot_general(p.astype(BF16), do_b, (((0,), (0,)), ((), ())), preferred_element_type=F32)
        dck_ref[...] -= jnp.sum(ds, axis=0, keepdims=True)

    W = H * dh
    return pl.pallas_call(
        body, name=name, grid=(H, S // tq),
        in_specs=[pl.BlockSpec((tq, dh), lambda h, i: (i, h)),
                  pl.BlockSpec((S, dh), lambda h, i: (0, H + h)),
                  pl.BlockSpec((S, dh), lambda h, i: (0, 2 * H + h)),
                  pl.BlockSpec((tq, dh), lambda h, i: (i, h)),
                  pl.BlockSpec((None, tq, 1), lambda h, i: (h, i, 0)),
                  pl.BlockSpec((None, 1, S), lambda h, i: (h, 0, 0)),
                  pl.BlockSpec((None, tq, 1), lambda h, i: (h, i, 0))],
        out_specs=[pl.BlockSpec((tq, dh), lambda h, i: (i, h)),
                   pl.BlockSpec((S, dh), lambda h, i: (0, h)),
                   pl.BlockSpec((S, dh), lambda h, i: (0, h)),
                   pl.BlockSpec((None, 1, S), lambda h, i: (h, 0, 0))],
        out_shape=[jax.ShapeDtypeStruct((S, W), BF16), jax.ShapeDtypeStruct((S, W), F32),
                   jax.ShapeDtypeStruct((S, W), F32), jax.ShapeDtypeStruct((H, 1, S), F32)],
        compiler_params=_params(("parallel", "arbitrary")),
    )(proj, proj, proj, do, cum_col, cum_row, lse)


def _rope(parts, tabs, out_dtype, name):
    S = parts[0][0].shape[0]
    widths = [a.shape[1] for a, _ in parts]
    total = sum(widths)
    tm = _tile(S, ROW_TILE, 8)
    flags = [r for _, r in parts]

    def body(*refs):
        in_refs = refs[:len(parts)]
        cos_ref, sa_ref, sb_ref, o_ref = refs[len(parts):]
        cos, sa, sb = cos_ref[...], sa_ref[...], sb_ref[...]
        off = 0
        for ref, rot, w in zip(in_refs, flags, widths):
            for j in range(w // LANE):
                t = ref[:, j * LANE:(j + 1) * LANE]
                if rot:
                    t = t * cos + pltpu.roll(t, LANE - ROPE_DIM // 2, axis=1) * sa + pltpu.roll(t, ROPE_DIM // 2, axis=1) * sb
                o_ref[:, off + j * LANE:off + (j + 1) * LANE] = t.astype(o_ref.dtype)
            off += w

    return pl.pallas_call(
        body, name=name, grid=(S // tm,),
        in_specs=[pl.BlockSpec((tm, w), lambda i: (i, 0)) for w in widths] + [_row_spec(tm, LANE)] * 3,
        out_specs=_row_spec(tm, total),
        out_shape=jax.ShapeDtypeStruct((S, total), out_dtype),
        compiler_params=_params(("parallel",)),
    )(*[a for a, _ in parts], *tabs)


def _swa_band(ref_p, ref_c, hk):
    dh = SWA_HEAD_DIM
    return jnp.concatenate([ref_p[:, hk * dh:(hk + 1) * dh], ref_c[:, hk * dh:(hk + 1) * dh]], axis=0).astype(BF16)


def _swa_mask(n):
    qi = lax.broadcasted_iota(jnp.int32, (Q_BLOCK, 2 * Q_BLOCK), 0)
    kj = lax.broadcasted_iota(jnp.int32, (Q_BLOCK, 2 * Q_BLOCK), 1)
    rel = qi + Q_BLOCK - kj
    return (rel >= 0) & (rel < SWA_WINDOW) & ((kj >= Q_BLOCK) | (n > 0))


def _swa_attn_fwd(qk, proj, sinks, Hq, name):
    S = qk.shape[0]
    dh, G, QB = SWA_HEAD_DIM, SWA_GROUP, Q_BLOCK
    Hk = Hq // G
    Wq, Wk = Hq * dh, Hk * dh
    nb = S // QB
    scale = dh ** -0.5

    def body(q_ref, kp_ref, kc_ref, vp_ref, vc_ref, sink_ref, o_ref, lse_ref):
        n = pl.program_id(0)
        mask = _swa_mask(n)
        lane = lax.broadcasted_iota(jnp.int32, (QB, LANE), 1)
        lse_tile = jnp.zeros((QB, LANE), F32)
        for hk in range(Hk):
            kb = _swa_band(kp_ref, kc_ref, hk)
            vb = _swa_band(vp_ref, vc_ref, hk)
            outs = []
            for g in range(G):
                h = hk * G + g
                q = q_ref[:, h * dh:(h + 1) * dh].astype(BF16)
                s = lax.dot_general(q, kb, (((1,), (1,)), ((), ())), preferred_element_type=F32) * scale
                s = jnp.where(mask, s, -jnp.inf)
                sk = sink_ref[0:1, h:h + 1]
                m = jnp.maximum(jnp.max(s, axis=-1, keepdims=True), sk)
                p = jnp.exp(s - m)
                l = jnp.sum(p, axis=-1, keepdims=True) + jnp.exp(sk - m)
                outs.append(jnp.dot(p.astype(BF16), vb, preferred_element_type=F32) / l)
                lse_tile = jnp.where(lane == h, m + jnp.log(l), lse_tile)
            for g in range(0, G, 2):
                c0 = (hk * G + g) * dh
                o_ref[:, c0:c0 + 2 * dh] = jnp.concatenate([outs[g], outs[g + 1]], axis=1).astype(BF16)
        lse_ref[...] = lse_tile

    kcol, vcol = Wq // Wk, (Wq + Wk) // Wk
    return pl.pallas_call(
        body, name=name, grid=(nb,),
        in_specs=[pl.BlockSpec((QB, Wq), lambda n: (n, 0)),
                  pl.BlockSpec((QB, Wk), lambda n: (jnp.maximum(n - 1, 0), kcol)),
                  pl.BlockSpec((QB, Wk), lambda n: (n, kcol)),
                  pl.BlockSpec((QB, Wk), lambda n: (jnp.maximum(n - 1, 0), vcol)),
                  pl.BlockSpec((QB, Wk), lambda n: (n, vcol)),
                  pl.BlockSpec((1, LANE), lambda n: (0, 0))],
        out_specs=[pl.BlockSpec((QB, Wq), lambda n: (n, 0)), pl.BlockSpec((QB, LANE), lambda n: (n, 0))],
        out_shape=[jax.ShapeDtypeStruct((S, Wq), BF16), jax.ShapeDtypeStruct((S, LANE), F32)],
        compiler_params=_params(("parallel",)),
    )(qk, qk, qk, proj, proj, sinks)


def _swa_attn_bwd(qk, proj, sinks, do, lse, Hq, name):
    S = qk.shape[0]
    dh, G, QB = SWA_HEAD_DIM, SWA_GROUP, Q_BLOCK
    Hk = Hq // G
    Wq, Wk = Hq * dh, Hk * dh
    nb = S // QB
    scale = dh ** -0.5

    def body(q_ref, kp_ref, kc_ref, vp_ref, vc_ref, sink_ref, do_ref, lse_ref,
             dq_ref, dk_ref, dv_ref, dsink_ref, carry_k, carry_v):
        n = pl.program_id(0)

        @pl.when(n == 0)
        def _():
            dsink_ref[...] = jnp.zeros_like(dsink_ref)

        @pl.when(n < nb)
        def _():
            mask = _swa_mask(n)
            lane = lax.broadcasted_iota(jnp.int32, (1, LANE), 1)
            dsink = jnp.zeros((1, LANE), F32)
            dk_heads, dv_heads = [], []
            for hk in range(Hk):
                kb = _swa_band(kp_ref, kc_ref, hk)
                vb = _swa_band(vp_ref, vc_ref, hk)
                dkb = jnp.zeros((2 * QB, dh), F32)
                dvb = jnp.zeros((2 * QB, dh), F32)
                dqs = []
                for g in range(G):
                    h = hk * G + g
                    q = q_ref[:, h * dh:(h + 1) * dh].astype(BF16)
                    do_h = do_ref[:, h * dh:(h + 1) * dh].astype(BF16)
                    lse_h = lse_ref[:, h:h + 1]
                    s = lax.dot_general(q, kb, (((1,), (1,)), ((), ())), preferred_element_type=F32) * scale
                    s = jnp.where(mask, s, -jnp.inf)
                    p = jnp.exp(s - lse_h)
                    p_sink = jnp.exp(sink_ref[0:1, h:h + 1] - lse_h)
                    dp = lax.dot_general(do_h, vb, (((1,), (1,)), ((), ())), preferred_element_type=F32)
                    delta = jnp.sum(p * dp, axis=-1, keepdims=True)
                    ds_b = (p * (dp - delta)).astype(BF16)
                    dqs.append(jnp.dot(ds_b, kb, preferred_element_type=F32) * scale)
                    dkb += lax.dot_general(ds_b, q, (((0,), (0,)), ((), ())), preferred_element_type=F32) * scale
                    dvb += lax.dot_general(p.astype(BF16), do_h, (((0,), (0,)), ((), ())), preferred_element_type=F32)
                    dsink = jnp.where(lane == h, -jnp.sum(p_sink * delta, axis=0, keepdims=True), dsink)
                for g in range(0, G, 2):
                    c0 = (hk * G + g) * dh
                    dq_ref[:, c0:c0 + 2 * dh] = jnp.concatenate([dqs[g], dqs[g + 1]], axis=1)
                dk_heads.append(dkb)
                dv_heads.append(dvb)
            dsink_ref[...] += dsink
            dk_all = jnp.concatenate(dk_heads, axis=1)
            dv_all = jnp.concatenate(dv_heads, axis=1)

            @pl.when(n > 0)
            def _():
                dk_ref[...] = carry_k[...] + dk_all[:QB]
                dv_ref[...] = carry_v[...] + dv_all[:QB]

            carry_k[...] = dk_all[QB:]
            carry_v[...] = dv_all[QB:]

        @pl.when(n == nb)
        def _():
            dk_ref[...] = carry_k[...]
            dv_ref[...] = carry_v[...]

    kcol, vcol = Wq // Wk, (Wq + Wk) // Wk
    cur = lambda n: jnp.minimum(n, nb - 1)
    prev = lambda n: jnp.maximum(jnp.minimum(n, nb - 1) - 1, 0)
    return pl.pallas_call(
        body, name=name, grid=(nb + 1,),
        in_specs=[pl.BlockSpec((QB, Wq), lambda n: (cur(n), 0)),
                  pl.BlockSpec((QB, Wk), lambda n: (prev(n), kcol)),
                  pl.BlockSpec((QB, Wk), lambda n: (cur(n), kcol)),
                  pl.BlockSpec((QB, Wk), lambda n: (prev(n), vcol)),
                  pl.BlockSpec((QB, Wk), lambda n: (cur(n), vcol)),
                  pl.BlockSpec((1, LANE), lambda n: (0, 0)),
                  pl.BlockSpec((QB, Wq), lambda n: (cur(n), 0)),
                  pl.BlockSpec((QB, LANE), lambda n: (cur(n), 0))],
        out_specs=[pl.BlockSpec((QB, Wq), lambda n: (cur(n), 0)),
                   pl.BlockSpec((QB, Wk), lambda n: (jnp.maximum(n - 1, 0), 0)),
                   pl.BlockSpec((QB, Wk), lambda n: (jnp.maximum(n - 1, 0), 0)),
                   pl.BlockSpec((1, LANE), lambda n: (0, 0))],
        out_shape=[jax.ShapeDtypeStruct((S, Wq), F32), jax.ShapeDtypeStruct((S, Wk), F32),
                   jax.ShapeDtypeStruct((S, Wk), F32), jax.ShapeDtypeStruct((1, LANE), F32)],
        scratch_shapes=[pltpu.VMEM((QB, Wk), F32), pltpu.VMEM((QB, Wk), F32)],
        compiler_params=_params(("arbitrary",)),
    )(qk, qk, qk, proj, proj, sinks, do, lse)


ADA_ROWS = 16


def _ada_mod(c_pad, w, b, name):
    L, D, N = w.shape
    tn = _tile(N, 512)

    def body(c_ref, w_ref, b_ref, o_ref):
        c = c_ref[...]
        c = c * _sigmoid(c)
        ch = c.astype(BF16)
        cl = (c - ch.astype(F32)).astype(BF16)
        ww = w_ref[...]
        wh = ww.astype(BF16)
        wl = (ww - wh.astype(F32)).astype(BF16)
        acc = jnp.dot(ch, wh, preferred_element_type=F32)
        acc += jnp.dot(ch, wl, preferred_element_type=F32)
        acc += jnp.dot(cl, wh, preferred_element_type=F32)
        o_ref[...] = acc + b_ref[...]

    return pl.pallas_call(
        body, name=name, grid=(L, N // tn),
        in_specs=[pl.BlockSpec((ADA_ROWS, D), lambda l, j: (0, 0)),
                  pl.BlockSpec((None, D, tn), lambda l, j: (l, 0, j)),
                  pl.BlockSpec((None, 1, tn), lambda l, j: (l, 0, j))],
        out_specs=pl.BlockSpec((None, ADA_ROWS, tn), lambda l, j: (l, 0, j)),
        out_shape=jax.ShapeDtypeStruct((L, ADA_ROWS, N), F32),
        compiler_params=_params(("parallel", "parallel")),
    )(c_pad, w, b)


def _ada_bwd(cT, dm, name):
    D = cT.shape[0]
    L, B, N = dm.shape
    tm = _tile(D, 256)

    def body(c_ref, dm_ref, o_ref):
        c = c_ref[...]
        c = c * _sigmoid(c)
        dmv = dm_ref[...]
        acc = c[:, 0:1] * dmv[0:1, :]
        for b in range(1, B):
            acc += c[:, b:b + 1] * dmv[b:b + 1, :]
        o_ref[...] = acc

    return pl.pallas_call(
        body, name=name, grid=(L, D // tm),
        in_specs=[pl.BlockSpec((tm, B), lambda l, i: (i, 0)), pl.BlockSpec((None, B, N), lambda l, i: (l, 0, 0))],
        out_specs=pl.BlockSpec((None, tm, N), lambda l, i: (l, i, 0)),
        out_shape=jax.ShapeDtypeStruct((L, D, N), F32),
        compiler_params=_params(("parallel", "parallel")),
    )(cT, dm)


def _adamw_math(w, g, m, v):
    m = ADAM_B1 * m + (1.0 - ADAM_B1) * g
    v = ADAM_B2 * v + (1.0 - ADAM_B2) * (g * g)
    m_hat = m / (1.0 - ADAM_B1 ** ADAM_STEP)
    v_hat = v / (1.0 - ADAM_B2 ** ADAM_STEP)
    delta = -ADAM_LR * (m_hat / (jnp.sqrt(v_hat) + ADAM_EPS) + ADAM_WD * w)
    return delta, m, v


def _adam_rows(R, C):
    lanes = -(-C // LANE) * LANE
    return _tile(R, max(8, (262144 // lanes) // 8 * 8), 8)


def _adam_sum(recv, w, m, v, layer, filled, name):
    P, R, C = recv.shape
    L = w.shape[0]
    tr = _adam_rows(R, C)
    n_keep = 0 if filled is None else 4

    def body(r_ref, w_ref, m_ref, v_ref, *rest):
        g_ref, d_ref, mo_ref, vo_ref = rest[n_keep:]
        g = r_ref[0].astype(F32)
        for p in range(1, P):
            g = g + r_ref[p].astype(F32)
        delta, mn, vn = _adamw_math(w_ref[...], g, m_ref[...], v_ref[...])
        g_ref[...] = g
        d_ref[...] = delta
        mo_ref[...] = mn
        vo_ref[...] = vn

    spec = pl.BlockSpec((None, tr, C), lambda i: (layer, i, 0))
    return pl.pallas_call(
        body, name=name, grid=(R // tr,),
        in_specs=[pl.BlockSpec((P, tr, C), lambda i: (0, i, 0)), spec, spec, spec]
        + [pl.BlockSpec(memory_space=pl.ANY)] * n_keep,
        out_specs=[spec] * 4,
        out_shape=[jax.ShapeDtypeStruct((L, R, C), F32)] * 4,
        input_output_aliases={4 + k: k for k in range(n_keep)},
        compiler_params=_params(("parallel",)),
    )(recv, w, m, v, *(filled or ()))


def _adam(g, w, m, v, name):
    L, R, C = w.shape
    tr = _adam_rows(R, C)

    def body(g_ref, w_ref, m_ref, v_ref, d_ref, mo_ref, vo_ref):
        delta, mn, vn = _adamw_math(w_ref[...], g_ref[...], m_ref[...], v_ref[...])
        d_ref[...] = delta
        mo_ref[...] = mn
        vo_ref[...] = vn

    spec = pl.BlockSpec((None, tr, C), lambda l, i: (l, i, 0))
    return pl.pallas_call(
        body, name=name, grid=(L, R // tr),
        in_specs=[spec] * 4, out_specs=[spec] * 3,
        out_shape=[jax.ShapeDtypeStruct((L, R, C), F32)] * 3,
        compiler_params=_params(("parallel", "parallel")),
    )(g, w, m, v)


def _sum_slots(x, name):
    P, R, C = x.shape

    def body(x_ref, o_ref):
        acc = x_ref[0]
        for p in range(1, P):
            acc = acc + x_ref[p]
        o_ref[...] = acc

    return pl.pallas_call(
        body, name=name,
        in_specs=[pl.BlockSpec(memory_space=pltpu.VMEM)], out_specs=pl.BlockSpec(memory_space=pltpu.VMEM),
        out_shape=jax.ShapeDtypeStruct((R, C), F32),
        compiler_params=pltpu.CompilerParams(vmem_limit_bytes=48 * MIB),
    )(x)


def _my_pos():
    return lax.axis_index("x"), lax.axis_index("y"), lax.axis_index("c")


def _all_gather_small(x, name):
    R, C = x.shape

    def body(x_ref, out_ref, send_sems, recv_sems):
        x_, y_, c_ = _my_pos()
        me, sibling = (x_, y_, c_), (x_, y_, 1 - c_)
        chips = [(1 - x_, y_), (x_, 1 - y_), (1 - x_, 1 - y_)]

        def slot(px, py, pc):
            return out_ref.at[4 * px + 2 * py + pc]

        def copy(k, block, to):
            return pltpu.make_async_remote_copy(
                src_ref=slot(*block), dst_ref=slot(*block), send_sem=send_sems.at[k], recv_sem=recv_sems.at[k],
                device_id=to, device_id_type=MESH)

        out_ref[4 * x_ + 2 * y_ + c_] = x_ref[...]
        first = [copy(0, me, sibling)] + [copy(1 + j, me, (*chip, c_)) for j, chip in enumerate(chips)]
        for cp in first:
            cp.start()
        passed = [copy(4 + j, (*chip, c_), sibling) for j, chip in enumerate(chips)]
        for j, chip in enumerate(chips):
            copy(1 + j, (*chip, c_), me).wait_recv()
            passed[j].start()
        copy(0, sibling, me).wait_recv()
        for j, chip in enumerate(chips):
            copy(4 + j, (*chip, 1 - c_), me).wait_recv()
        for cp in first + passed:
            cp.wait_send()

    return pl.pallas_call(
        body, name=name,
        in_specs=[pl.BlockSpec(memory_space=pltpu.VMEM)], out_specs=pl.BlockSpec(memory_space=pltpu.VMEM),
        out_shape=jax.ShapeDtypeStruct((N_DEV, R, C), x.dtype),
        scratch_shapes=[pltpu.SemaphoreType.DMA((7,)), pltpu.SemaphoreType.DMA((7,))],
        compiler_params=pltpu.CompilerParams(vmem_limit_bytes=48 * MIB),
    )(x)


HBM_SPEC = pl.BlockSpec(memory_space=pltpu.HBM)
SEM_SPEC = pl.BlockSpec(memory_space=pltpu.SEMAPHORE)
ANY_SPEC = pl.BlockSpec(memory_space=pl.ANY)
SPLIT_EFFECT = pltpu.SideEffectType.DATAFLOW_SIDE_EFFECTING


def _in_hbm(a):
    return pltpu.with_memory_space_constraint(a, pltpu.HBM)


def _gathered_shape(a, kind):
    if kind == "major":
        return (N_DEV,) + a.shape
    if kind == "rows":
        return (N_DEV * a.shape[0], a.shape[1])
    return (2, a.shape[0], 4 * a.shape[1])


def _gather_slot(ref, kind, block, shard_shape):
    px, py, pc = block
    if kind == "major":
        return ref.at[4 * px + 2 * py + pc]
    if kind == "rows":
        r = shard_shape[0]
        return ref.at[pl.ds(pl.multiple_of((4 * px + 2 * py + pc) * r, r), r), :]
    cu = shard_shape[1]
    return ref.at[px, :, pl.ds(pl.multiple_of((2 * py + pc) * cu, cu), cu)]


def _gather_peers():
    x_, y_, c_ = _my_pos()
    return (x_, y_, c_), (x_, y_, 1 - c_), [(1 - x_, y_), (x_, 1 - y_), (1 - x_, 1 - y_)]


def _gather_start(shards, kinds, after, name):
    n = len(shards)
    bufs = [lax.empty(_gathered_shape(a, k), a.dtype) for a, k in zip(shards, kinds)]
    extra = [] if after is None else [after]

    def body(*refs):
        shard_refs, buf_refs = refs[:n], refs[n:2 * n]
        send_sems, recv_sems, local_sems = refs[2 * n + len(extra):2 * n + len(extra) + 3]
        token = refs[-1]
        me, sibling, chips = _gather_peers()
        for e in range(n):
            mine = _gather_slot(buf_refs[e], kinds[e], me, shards[e].shape)
            pltpu.make_async_copy(shard_refs[e], mine, local_sems.at[e]).start()
            for k, to in enumerate([sibling] + [(*chip, me[2]) for chip in chips]):
                pltpu.make_async_remote_copy(
                    src_ref=shard_refs[e], dst_ref=mine, send_sem=send_sems.at[4 * e + k],
                    recv_sem=recv_sems.at[4 * e + k], device_id=to, device_id_type=MESH).start()
        token[...] = jnp.zeros_like(token)

    out = pl.pallas_call(
        body, name=name,
        out_shape=(pltpu.SemaphoreType.DMA((4 * n,)), pltpu.SemaphoreType.DMA((4 * n,)), pltpu.SemaphoreType.DMA((n,)),
                   *[pltpu.HBM(a.shape, a.dtype) for a in shards], *[pltpu.HBM(a.shape, a.dtype) for a in bufs],
                   jax.ShapeDtypeStruct((8, LANE), F32)),
        in_specs=[HBM_SPEC] * (2 * n) + [ANY_SPEC] * len(extra),
        out_specs=(SEM_SPEC, SEM_SPEC, SEM_SPEC, *[HBM_SPEC] * (2 * n), pl.BlockSpec(memory_space=pltpu.VMEM)),
        input_output_aliases={i: 3 + i for i in range(2 * n)},
        compiler_params=pltpu.CompilerParams(has_side_effects=SPLIT_EFFECT),
    )(*[_in_hbm(a) for a in shards], *[_in_hbm(a) for a in bufs], *extra)
    return out[0], out[1], out[2], out[3:3 + n], out[3 + n:3 + 2 * n], out[-1]


def _gather_forward(recv_sems, bufs, kinds, shard_shapes, after, name):
    n = len(bufs)

    def body(*refs):
        buf_refs, recv_in = refs[:n], refs[n]
        fsend, frecv = refs[n + 2], refs[n + 3]
        me, sibling, chips = _gather_peers()
        for e in range(n):
            for j, chip in enumerate(chips):
                slot = _gather_slot(buf_refs[e], kinds[e], (*chip, me[2]), shard_shapes[e])
                pltpu.make_async_remote_copy(
                    src_ref=slot, dst_ref=slot, send_sem=recv_in.at[4 * e + 1 + j], recv_sem=recv_in.at[4 * e + 1 + j],
                    device_id=me, device_id_type=MESH).wait_recv()
                pltpu.make_async_remote_copy(
                    src_ref=slot, dst_ref=slot, send_sem=fsend.at[3 * e + j], recv_sem=frecv.at[3 * e + j],
                    device_id=sibling, device_id_type=MESH).start()

    out = pl.pallas_call(
        body, name=name,
        out_shape=(pltpu.SemaphoreType.DMA((3 * n,)), pltpu.SemaphoreType.DMA((3 * n,)),
                   *[pltpu.HBM(a.shape, a.dtype) for a in bufs]),
        in_specs=[HBM_SPEC] * n + [SEM_SPEC, ANY_SPEC],
        out_specs=(SEM_SPEC, SEM_SPEC, *[HBM_SPEC] * n),
        input_output_aliases={i: 2 + i for i in range(n)},
        compiler_params=pltpu.CompilerParams(has_side_effects=SPLIT_EFFECT),
    )(*bufs, recv_sems, after)
    return out[0], out[1], out[2:]


def _gather_wait(send_sems, recv_sems, local_sems, fsend, frecv, shards, bufs, kinds, after, name):
    n = len(bufs)

    def body(*refs):
        shard_refs, buf_refs = refs[:n], refs[n:2 * n]
        send_in, recv_in, local_in, fsend_in, frecv_in = refs[2 * n:2 * n + 5]
        me, sibling, chips = _gather_peers()

        def arrival(slot, sem):
            return pltpu.make_async_remote_copy(src_ref=slot, dst_ref=slot, send_sem=sem, recv_sem=sem,
                                                device_id=me, device_id_type=MESH)

        for e in range(n):
            shape = shards[e].shape
            mine = _gather_slot(buf_refs[e], kinds[e], me, shape)
            pltpu.make_async_copy(shard_refs[e], mine, local_in.at[e]).wait()
            arrival(_gather_slot(buf_refs[e], kinds[e], sibling, shape), recv_in.at[4 * e]).wait_recv()
            for j, chip in enumerate(chips):
                arrival(_gather_slot(buf_refs[e], kinds[e], (*chip, 1 - me[2]), shape), frecv_in.at[3 * e + j]).wait_recv()
            for k in range(4):
                arrival(mine, send_in.at[4 * e + k]).wait_send()
            for j in range(3):
                arrival(mine, fsend_in.at[3 * e + j]).wait_send()

    out = pl.pallas_call(
        body, name=name,
        out_shape=(*[pltpu.HBM(a.shape, a.dtype) for a in shards], *[pltpu.HBM(a.shape, a.dtype) for a in bufs]),
        in_specs=[HBM_SPEC] * (2 * n) + [SEM_SPEC] * 5 + [ANY_SPEC],
        out_specs=tuple([HBM_SPEC] * (2 * n)),
        input_output_aliases={i: i for i in range(2 * n)},
        compiler_params=pltpu.CompilerParams(has_side_effects=SPLIT_EFFECT),
    )(*shards, *bufs, send_sems, recv_sems, local_sems, fsend, frecv, after)
    return out[n:]


def _grad_slice(ref, kind, j):
    if kind == "major":
        return ref.at[j]
    if kind == "rows":
        r = ref.shape[0] // N_DEV
        return ref.at[pl.ds(j * r, r), :]
    cu = ref.shape[2] // 4
    return ref.at[j // 4, :, pl.ds((j % 4) * cu, cu)]


def _slice_shape(a, kind):
    if kind == "major":
        return a.shape[1:]
    if kind == "rows":
        return (a.shape[0] // N_DEV, a.shape[1])
    return (a.shape[1], a.shape[2] // 4)


def _scatter_copies(srcs, lands, kinds, send_sems, recv_sems):
    x_, y_, c_ = _my_pos()
    me = 4 * x_ + 2 * y_ + c_
    n = len(srcs)

    def remote(e, j):
        return pltpu.make_async_remote_copy(
            src_ref=_grad_slice(srcs[e], kinds[e], j), dst_ref=lands[e].at[me],
            send_sem=send_sems.at[e * N_DEV + j], recv_sem=recv_sems.at[e * N_DEV + me],
            device_id=(j // 4, (j // 2) % 2, j % 2), device_id_type=MESH)

    def local(e, j):
        return pltpu.make_async_copy(_grad_slice(srcs[e], kinds[e], j), lands[e].at[j], recv_sems.at[e * N_DEV + j])

    def arrival(e, i):
        return pltpu.make_async_remote_copy(
            src_ref=_grad_slice(srcs[e], kinds[e], i), dst_ref=lands[e].at[i],
            send_sem=send_sems.at[e * N_DEV + i], recv_sem=recv_sems.at[e * N_DEV + i],
            device_id=(i // 4, (i // 2) % 2, i % 2), device_id_type=MESH)

    def start():
        for e in range(n):
            for j in range(N_DEV):
                @pl.when(me == j)
                def _():
                    local(e, j).start()

                @pl.when(me != j)
                def _():
                    remote(e, j).start()

    def wait():
        for e in range(n):
            for i in range(N_DEV):
                @pl.when(me == i)
                def _():
                    local(e, i).wait()

                @pl.when(me != i)
                def _():
                    arrival(e, i).wait_recv()
        for e in range(n):
            for j in range(N_DEV):
                @pl.when(me != j)
                def _():
                    remote(e, j).wait_send()

    return start, wait


def _scatter_start(srcs, kinds, name):
    n = len(srcs)
    lands = [lax.empty((N_DEV,) + _slice_shape(a, k), a.dtype) for a, k in zip(srcs, kinds)]

    def body(*refs):
        src_refs, land_refs = refs[:n], refs[n:2 * n]
        send_sems, recv_sems = refs[2 * n], refs[2 * n + 1]
        token = refs[-1]
        start, _ = _scatter_copies(src_refs, land_refs, kinds, send_sems, recv_sems)
        start()
        token[...] = jnp.zeros_like(token)

    out = pl.pallas_call(
        body, name=name,
        out_shape=(pltpu.SemaphoreType.DMA((n * N_DEV,)), pltpu.SemaphoreType.DMA((n * N_DEV,)),
                   *[pltpu.HBM(a.shape, a.dtype) for a in srcs], *[pltpu.HBM(a.shape, a.dtype) for a in lands],
                   jax.ShapeDtypeStruct((8, LANE), F32)),
        in_specs=[HBM_SPEC] * (2 * n),
        out_specs=(SEM_SPEC, SEM_SPEC, *[HBM_SPEC] * (2 * n), pl.BlockSpec(memory_space=pltpu.VMEM)),
        input_output_aliases={i: 2 + i for i in range(2 * n)},
        compiler_params=pltpu.CompilerParams(has_side_effects=SPLIT_EFFECT),
    )(*[_in_hbm(a) for a in srcs], *[_in_hbm(a) for a in lands])
    return out[0], out[1], out[2:2 + n], out[2 + n:2 + 2 * n], out[-1]


def _scatter_wait(send_sems, recv_sems, srcs, lands, kinds, after, name):
    n = len(srcs)

    def body(*refs):
        src_refs, land_refs = refs[:n], refs[n:2 * n]
        _, wait = _scatter_copies(src_refs, land_refs, kinds, refs[2 * n], refs[2 * n + 1])
        wait()

    out = pl.pallas_call(
        body, name=name,
        out_shape=(*[pltpu.HBM(a.shape, a.dtype) for a in srcs], *[pltpu.HBM(a.shape, a.dtype) for a in lands]),
        in_specs=[HBM_SPEC] * (2 * n) + [SEM_SPEC, SEM_SPEC, pl.BlockSpec(memory_space=pl.ANY)],
        out_specs=tuple([HBM_SPEC] * (2 * n)),
        input_output_aliases={i: i for i in range(2 * n)},
        compiler_params=pltpu.CompilerParams(has_side_effects=SPLIT_EFFECT),
    )(*srcs, *lands, send_sems, recv_sems, after)
    return out[n:]


def _rope_tables(positions, sign):
    half = ROPE_DIM // 2
    inv_freq = ROPE_THETA ** (-jnp.arange(0, ROPE_DIM, 2, dtype=F32) / ROPE_DIM)
    ang = positions.astype(F32)[:, None] * inv_freq
    reps = LANE // half
    cos = jnp.tile(jnp.cos(ang), (1, reps))
    sin = jnp.tile(jnp.sin(ang), (1, reps)) * sign
    d = jnp.arange(LANE) % SWA_HEAD_DIM
    return (jnp.where(d < ROPE_DIM, cos, 1.0), jnp.where(d < half, -sin, 0.0),
            jnp.where((d >= half) & (d < ROPE_DIM), sin, 0.0))


def _pad_cols(a, n):
    return jnp.pad(a, ((0, 0), (0, n - a.shape[1])))


def _local_step(x, target, positions, mods, fetch, P, on_grads):
    S, D = x.shape
    Hf = D // FOX_HEAD_DIM
    Hq = D // SWA_HEAD_DIM
    Hk = Hq // SWA_GROUP
    Wk = Hk * SWA_HEAD_DIM
    n_in = 3 * D + Hf
    (sh1a, sc1a, g1a, sh2a, sc2a, g2a), (sh1b, sc1b, g1b, sh2b, sc2b, g2b) = mods
    row = lambda v: v.reshape(1, -1)
    cw = [jnp.transpose(P["conv_w"][l].reshape(3, 2, -1), (1, 0, 2)) for l in range(2)]
    cb = [P["conv_b"][l].reshape(2, 1, -1) for l in range(2)]

    W = dict(up=[None, None], down=[None, None])
    h1a = _modulate(x, sc1a, sh1a, "modulate_in")
    W["fox_in"], W["fox_o"] = fetch("fox", "wait", h1a)
    proj_a = _mm_nn(h1a, W["fox_in"], F32, "fox_in_proj", tn=896)
    flT = proj_a[:, 3 * D:n_in].T
    bf_col = P["fox_b_f"].reshape(Hf, 1)
    cumT = _fox_prep(flT, bf_col, "fox_cumsum")
    cum_col, cum_row = cumT.reshape(Hf, S, 1), cumT.reshape(Hf, 1, S)
    o_a, lse_a = _fox_attn_fwd(proj_a, cum_col, cum_row, Hf, "fox_attn_fwd")
    fetch("ffn0", "forward", o_a)
    y1a = _mm_nn(o_a, W["fox_o"], F32, "fox_out_proj")
    z1a, x1, h2a = _ln_fwd(x, y1a, g1a, row(P["ln_mix_g"][0]), row(P["ln_mix_b"][0]), sc2a, sh2a, "ln_mix0")
    W["up"][0], W["down"][0] = fetch("ffn0", "wait", h2a)
    u_a, a_a = _ffn_up(h2a, W["up"][0], cw[0], cb[0], "ffn_up0")
    fetch("swa", "forward", a_a)
    y2a = _mm_nn(a_a, W["down"][0], F32, "ffn_down0", tk=1408)
    z2a, x2, h1b = _ln_fwd(x1, y2a, g2a, row(P["ln_ffn_g"][0]), row(P["ln_ffn_b"][0]), sc1b, sh1b, "ln_ffn0")

    W["swa_in"], W["swa_o"] = fetch("swa", "wait", h1b)
    proj_b = _mm_nn(h1b, W["swa_in"], F32, "swa_in_proj")
    tabs_f = _rope_tables(positions, 1.0)
    tabs_b = _rope_tables(positions, -1.0)
    qk = _rope([(proj_b[:, :D + Wk], True)], tabs_f, F32, "rope_fwd")
    sinks = _pad_cols(P["swa_sinks"].reshape(1, Hq), LANE)
    o_b, lse_b = _swa_attn_fwd(qk, proj_b, sinks, Hq, "swa_attn_fwd")
    fetch("ffn1", "forward", o_b)
    y1b = _mm_nn(o_b, W["swa_o"], F32, "swa_out_proj")
    z1b, x3, h2b = _ln_fwd(x2, y1b, g1b, row(P["ln_mix_g"][1]), row(P["ln_mix_b"][1]), sc2b, sh2b, "ln_mix1")
    W["up"][1], W["down"][1] = fetch("ffn1", "wait", h2b)
    u_b, a_b = _ffn_up(h2b, W["up"][1], cw[1], cb[1], "ffn_up1")
    y2b = _mm_nn(a_b, W["down"][1], F32, "ffn_down1", tk=1408)
    z2b, dout, loss_row = _ln_fwd_loss(x3, y2b, g2b, row(P["ln_ffn_g"][1]), row(P["ln_ffn_b"][1]), target, "ln_ffn1_loss")

    def ffn_backward(dy, a, u, h_in, l, tag):
        da = _mm_nt(dy[None], W["down"][l][None], F32, "ffn_da" + tag)
        d_down = _mm_tn(a, dy[None], BF16, "ffn_dwdown" + tag, tm=1408, tn=1024)[0]
        du, dcw, dcb = _ffn_bwd_elem(da, u, cw[l], cb[l], "ffn_bwd_elem" + tag)
        dh = _mm_nt(du, W["up"][l], F32, "ffn_dh" + tag, tk=1408)
        d_up = _mm_tn(h_in, du, BF16, "ffn_dwup" + tag)
        return dh, d_up, d_down, jnp.transpose(dcw, (1, 0, 2)).reshape(3, -1), dcb.reshape(-1)

    dz2b, dy2b, dg_f1, db_f1, dgate2b = _ln_bwd(dout, z2b, y2b, g2b, row(P["ln_ffn_g"][1]), "ln_ffn1_bwd")
    dh2b, d_up1, d_down1, dcw1, dcb1 = ffn_backward(dy2b, a_b, u_b, h2b, 1, "1")
    token = on_grads("ffn1", dict(up=d_up1, down=d_down1))
    dx3, dsc2b, dsh2b = _mod_bwd(dz2b, dh2b, x3, sc2b, token, "mod_ffn1_bwd")

    dz1b, dy1b, dg_m1, db_m1, dgate1b = _ln_bwd(dx3, z1b, y1b, g1b, row(P["ln_mix_g"][1]), "ln_mix1_bwd")
    do_b = _mm_nt(dy1b[None], W["swa_o"][None], F32, "swa_do")
    d_swa_o = _mm_tn(o_b, dy1b[None], BF16, "swa_dwo")[0]
    dq_b, dk_b, dv_b, dsinks = _swa_attn_bwd(qk, proj_b, sinks, do_b, lse_b, Hq, "swa_attn_bwd")
    dproj_b = _rope([(dq_b, True), (dk_b, True), (dv_b, False)], tabs_b, BF16, "rope_bwd")
    dh1b = _mm_nt(dproj_b[None], W["swa_in"][None], F32, "swa_dh", tk=1280)
    d_swa_in = _mm_tn(h1b, dproj_b[None], BF16, "swa_dwin")[0]
    token = on_grads("swa", dict(w_in=d_swa_in, w_o=d_swa_o))
    dx2, dsc1b, dsh1b = _mod_bwd(dz1b, dh1b, x2, sc1b, token, "mod_mix1_bwd")

    dz2a, dy2a, dg_f0, db_f0, dgate2a = _ln_bwd(dx2, z2a, y2a, g2a, row(P["ln_ffn_g"][0]), "ln_ffn0_bwd")
    dh2a, d_up0, d_down0, dcw0, dcb0 = ffn_backward(dy2a, a_a, u_a, h2a, 0, "0")
    token = on_grads("ffn0", dict(up=d_up0, down=d_down0))
    dx1, dsc2a, dsh2a = _mod_bwd(dz2a, dh2a, x1, sc2a, token, "mod_ffn0_bwd")

    dz1a, dy1a, dg_m0, db_m0, dgate1a = _ln_bwd(dx1, z1a, y1a, g1a, row(P["ln_mix_g"][0]), "ln_mix0_bwd")
    do_a = _mm_nt(dy1a[None], W["fox_o"][None], F32, "fox_do")
    d_fox_o = _mm_tn(o_a, dy1a[None], BF16, "fox_dwo")[0]
    dq_a, dk_a, dv_a, dcum_row = _fox_attn_bwd(proj_a, do_a, cum_col, cum_row, lse_a, Hf, "fox_attn_bwd")
    dflT, dbf = _fox_prep_bwd(dcum_row.reshape(Hf, S), flT, bf_col, "fox_cumsum_bwd")
    n_pad = W["fox_in"].shape[1]
    dproj_a = jnp.concatenate([dq_a, dk_a.astype(BF16), dv_a.astype(BF16),
                               _pad_cols(dflT.T, n_pad - 3 * D).astype(BF16)], axis=1)
    dh1a = _mm_nt(dproj_a[None], W["fox_in"][None], F32, "fox_dh", tk=896)
    d_fox_in = _mm_tn(h1a, dproj_a[None], BF16, "fox_dwin", tn=896)[0]
    token = on_grads("fox", dict(w_in=d_fox_in, w_o=d_fox_o))
    grad_x, dsc1a, dsh1a = _mod_bwd(dz1a, dh1a, x, sc1a, token, "mod_mix0_bwd")

    dmod = jnp.stack([jnp.concatenate([dsh1a, dsc1a, dgate1a, dsh2a, dsc2a, dgate2a], axis=1)[0],
                      jnp.concatenate([dsh1b, dsc1b, dgate1b, dsh2b, dsc2b, dgate2b], axis=1)[0]])
    small = dict(dmod=dmod, conv_b=jnp.stack([dcb0, dcb1]), conv_w=jnp.stack([dcw0, dcw1]),
                 ln_mix_g=jnp.concatenate([dg_m0, dg_m1]), ln_mix_b=jnp.concatenate([db_m0, db_m1]),
                 ln_ffn_g=jnp.concatenate([dg_f0, dg_f1]), ln_ffn_b=jnp.concatenate([db_f0, db_f1]),
                 fox_b_f=dbf.reshape(-1), swa_sinks=dsinks[0, :Hq])
    return loss_row[0, 0], grad_x, small


SMALL_ORDER = ("dmod", "conv_b", "conv_w", "ln_mix_g", "ln_mix_b", "ln_ffn_g", "ln_ffn_b", "fox_b_f", "swa_sinks")


def _pack_rows(arrays):
    chunks, spans, off = [], [], 0
    for a in arrays:
        flat = a.reshape(-1)
        n = -(-flat.shape[0] // LANE) * LANE
        chunks.append(jnp.pad(flat, (0, n - flat.shape[0])))
        spans.append((off, flat.shape[0], a.shape))
        off += n
    total = -(-off // (8 * LANE)) * (8 * LANE)
    chunks.append(jnp.zeros((total - off,), F32))
    return jnp.concatenate(chunks).reshape(-1, LANE), spans


def _unpack_rows(packed, spans):
    flat = packed.reshape(-1)
    return [flat[off:off + n].reshape(shape) for off, n, shape in spans]


def kernel(x, c, positions, fox_w_in, fox_b_f, fox_w_o, swa_w_in, swa_sinks, swa_w_o, ada_w, ada_b, ffn_w_up, ffn_conv_w, ffn_conv_b, ffn_w_down, ln_mix_g, ln_mix_b, ln_ffn_g, ln_ffn_b, loss_target, m_fox_w_in, m_fox_b_f, m_fox_w_o, m_swa_w_in, m_swa_sinks, m_swa_w_o, m_ada_w, m_ada_b, m_ffn_w_up, m_ffn_conv_w, m_ffn_conv_b, m_ffn_w_down, m_ln_mix_g, m_ln_mix_b, m_ln_ffn_g, m_ln_ffn_b, v_fox_w_in, v_fox_b_f, v_fox_w_o, v_swa_w_in, v_swa_sinks, v_swa_w_o, v_ada_w, v_ada_b, v_ffn_w_up, v_ffn_conv_w, v_ffn_conv_b, v_ffn_w_down, v_ln_mix_g, v_ln_mix_b, v_ln_ffn_g, v_ln_ffn_b):
    S, D = x.shape[1], x.shape[2]
    L = ada_w.shape[0]
    me = 4 * lax.axis_index("x") + 2 * lax.axis_index("y") + lax.axis_index("c")
    n_ada = ada_w.shape[2]
    cu = ffn_w_up.shape[2]
    F = 4 * cu
    n_in = fox_w_in.shape[2] * N_DEV
    n_in_pad = -(-n_in // LANE) * LANE

    c_all = _all_gather_small(c.reshape(-1, LANE), "gather_c").reshape(N_DEV, D)
    b_cols = lax.dynamic_slice_in_dim(ada_b, me * n_ada, n_ada, axis=1).reshape(L, 1, n_ada)
    mod_blk = _ada_mod(jnp.pad(c_all, ((0, ADA_ROWS - N_DEV), (0, 0))), ada_w, b_cols, "ada_mod")[:, :N_DEV]
    mod_all = _all_gather_small(mod_blk.reshape(-1, LANE), "gather_mod").reshape(N_DEV, L, N_DEV, n_ada)
    mod_mine = lax.dynamic_index_in_dim(mod_all, me, axis=2, keepdims=False)
    mod_mine = jnp.transpose(mod_mine, (1, 0, 2)).reshape(L, N_DEV * n_ada)
    mods = [[mod_mine[l, k * D:(k + 1) * D].reshape(1, D) for k in range(6)] for l in range(L)]

    gather_groups = dict(
        fox=([fox_w_in[0].astype(BF16), fox_w_o[0].astype(BF16)], ["major", "rows"]),
        ffn0=([ffn_w_up[0].astype(BF16), ffn_w_down[0].astype(BF16)], ["halves", "rows"]),
        swa=([swa_w_in[0].astype(BF16), swa_w_o[0].astype(BF16)], ["major", "rows"]),
        ffn1=([ffn_w_up[1].astype(BF16), ffn_w_down[1].astype(BF16)], ["halves", "rows"]))
    gathers, token = {}, None
    for group, (shards, kinds) in gather_groups.items():
        send, recv, local, thru, bufs, token = _gather_start(shards, kinds, token, "gather_start_" + group)
        gathers[group] = dict(send=send, recv=recv, local=local, shards=thru, bufs=bufs, kinds=kinds,
                              shapes=[a.shape for a in shards])
    all_started = token

    def natural(g, pad_to=None):
        w = jnp.transpose(g, (1, 0, 2)).reshape(D, -1)
        return w if pad_to is None else _pad_cols(w, pad_to)

    def forward_stage(group, after):
        s = gathers[group]
        s["fsend"], s["frecv"], s["bufs"] = _gather_forward(s["recv"], s["bufs"], s["kinds"], s["shapes"], after,
                                                            "gather_forward_" + group)

    def fetch(group, stage, after):
        if stage == "forward":
            return forward_stage(group, after)
        if group == "fox":
            forward_stage(group, all_started)
        s = gathers.pop(group)
        first, second = _gather_wait(s["send"], s["recv"], s["local"], s["fsend"], s["frecv"], s["shards"], s["bufs"],
                                     s["kinds"], after, "gather_wait_" + group)
        if group == "fox":
            return natural(first, n_in_pad), second
        if group == "swa":
            return natural(first), second
        return first, second

    conv_w_all = None
    P = dict(fox_b_f=fox_b_f[0], swa_sinks=swa_sinks[0], conv_b=ffn_conv_b,
             ln_mix_g=ln_mix_g, ln_mix_b=ln_mix_b, ln_ffn_g=ln_ffn_g, ln_ffn_b=ln_ffn_b)
    cw_rows = _all_gather_small(_pack_rows([ffn_conv_w])[0], "gather_conv_w")
    n_cw = ffn_conv_w.size
    cw_dev = cw_rows.reshape(N_DEV, -1)[:, :n_cw].reshape(N_DEV, L, 3, cu)
    conv_w_all = jnp.transpose(cw_dev, (1, 2, 0, 3)).reshape(L, 3, N_DEV * cu)
    P["conv_w"] = conv_w_all

    out, pending = {}, {}

    def shard_major(g):
        return jnp.transpose(g.reshape(D, N_DEV, -1), (1, 0, 2))

    def start(group, srcs, kinds):
        send, recv, thru, lands, token = _scatter_start(srcs, kinds, "scatter_start_" + group)
        pending[group] = (send, recv, thru, lands, kinds)
        return token

    def finish(group, after):
        send, recv, thru, lands, kinds = pending.pop(group)
        return _scatter_wait(send, recv, thru, lands, kinds, after, "scatter_wait_" + group)

    def adam_ffn(r_up, r_dn, layer):
        tag = str(layer)
        out["ffn_w_up"] = _adam_sum(r_up, ffn_w_up, m_ffn_w_up, v_ffn_w_up, layer, out.get("ffn_w_up"), "adam_ffn_w_up" + tag)
        out["ffn_w_down"] = _adam_sum(r_dn, ffn_w_down, m_ffn_w_down, v_ffn_w_down, layer, out.get("ffn_w_down"),
                                      "adam_ffn_w_down" + tag)

    def on_grads(group, g):
        if group == "ffn1":
            return start("ffn1", [g["up"], g["down"]], ["halves", "rows"])
        if group == "swa":
            return start("swa", [shard_major(g["w_in"]), g["w_o"]], ["major", "rows"])
        if group == "ffn0":
            token = start("ffn0", [g["up"], g["down"]], ["halves", "rows"])
            adam_ffn(*finish("ffn1", token), 1)
            return token
        token = start("fox", [shard_major(g["w_in"][:, :n_in]), g["w_o"]], ["major", "rows"])
        r_si, r_so = finish("swa", token)
        out["swa_w_in"] = _adam_sum(r_si, swa_w_in, m_swa_w_in, v_swa_w_in, 0, None, "adam_swa_w_in")
        out["swa_w_o"] = _adam_sum(r_so, swa_w_o, m_swa_w_o, v_swa_w_o, 0, None, "adam_swa_w_o")
        adam_ffn(*finish("ffn0", token), 0)
        return token

    loss_local, grad_x, small = _local_step(x[0], loss_target[0], positions[0], mods, fetch, P, on_grads)
    loss = lax.psum(loss_local, AXES)
    r_fi, r_fo = finish("fox", grad_x)
    out["fox_w_in"] = _adam_sum(r_fi, fox_w_in, m_fox_w_in, v_fox_w_in, 0, None, "adam_fox_w_in")
    out["fox_w_o"] = _adam_sum(r_fo, fox_w_o, m_fox_w_o, v_fox_w_o, 0, None, "adam_fox_w_o")

    packed, spans = _pack_rows([small[k] for k in SMALL_ORDER])
    gathered = _all_gather_small(packed, "gather_small_grads")
    totals = dict(zip(SMALL_ORDER, _unpack_rows(_sum_slots(gathered, "sum_small_grads"), spans)))
    n_mod = L * 6 * D
    dmod_all = gathered.reshape(N_DEV, -1)[:, :n_mod].reshape(N_DEV, L, 6 * D)
    dmod_cols = jnp.transpose(lax.dynamic_slice_in_dim(dmod_all, me * n_ada, n_ada, axis=2), (1, 0, 2))
    g_ada_w = _ada_bwd(c_all.T, dmod_cols, "ada_w_grad")
    out["ada_w"] = (g_ada_w,) + tuple(_adam(g_ada_w, ada_w, m_ada_w, v_ada_w, "adam_ada_w"))

    g_small = dict(fox_b_f=totals["fox_b_f"].reshape(fox_b_f.shape), swa_sinks=totals["swa_sinks"].reshape(swa_sinks.shape),
                   ada_b=totals["dmod"].reshape(ada_b.shape), ffn_conv_b=totals["conv_b"].reshape(ffn_conv_b.shape),
                   ffn_conv_w=lax.dynamic_slice_in_dim(totals["conv_w"].reshape(L, 3, 2 * F), me * cu, cu, axis=2),
                   ln_mix_g=totals["ln_mix_g"], ln_mix_b=totals["ln_mix_b"],
                   ln_ffn_g=totals["ln_ffn_g"], ln_ffn_b=totals["ln_ffn_b"])
    small_names = ("fox_b_f", "swa_sinks", "ada_b", "ffn_conv_b", "ffn_conv_w", "ln_mix_g", "ln_mix_b", "ln_ffn_g", "ln_ffn_b")
    w_small = dict(fox_b_f=(fox_b_f, m_fox_b_f, v_fox_b_f), swa_sinks=(swa_sinks, m_swa_sinks, v_swa_sinks),
                   ada_b=(ada_b, m_ada_b, v_ada_b), ffn_conv_b=(ffn_conv_b, m_ffn_conv_b, v_ffn_conv_b),
                   ffn_conv_w=(ffn_conv_w, m_ffn_conv_w, v_ffn_conv_w),
                   ln_mix_g=(ln_mix_g, m_ln_mix_g, v_ln_mix_g), ln_mix_b=(ln_mix_b, m_ln_mix_b, v_ln_mix_b),
                   ln_ffn_g=(ln_ffn_g, m_ln_ffn_g, v_ln_ffn_g), ln_ffn_b=(ln_ffn_b, m_ln_ffn_b, v_ln_ffn_b))
    pk_g, sp = _pack_rows([g_small[k] for k in small_names])
    pk_w = _pack_rows([w_small[k][0] for k in small_names])[0]
    pk_m = _pack_rows([w_small[k][1] for k in small_names])[0]
    pk_v = _pack_rows([w_small[k][2] for k in small_names])[0]
    res = _adam(pk_g[None], pk_w[None], pk_m[None], pk_v[None], "adam_small")
    res = [dict(zip(small_names, _unpack_rows(r[0], sp))) for r in res]
    for k in small_names:
        out[k] = (g_small[k], res[0][k], res[1][k], res[2][k])

    order = ("fox_w_in", "fox_b_f", "fox_w_o", "swa_w_in", "swa_sinks", "swa_w_o", "ada_w", "ada_b", "ffn_w_up",
             "ffn_conv_w", "ffn_conv_b", "ffn_w_down", "ln_mix_g", "ln_mix_b", "ln_ffn_g", "ln_ffn_b")
    return (loss, grad_x[None], *[out[k][0] for k in order], *[out[k][1] for k in order],
            *[out[k][2] for k in order], *[out[k][3] for k in order])
```

```python
import functools

import jax
import jax.numpy as jnp
from jax import lax
from jax.experimental import pallas as pl
from jax.experimental.pallas import tpu as pltpu

F32 = jnp.float32
BF16 = jnp.bfloat16
MESH = pl.DeviceIdType.MESH
N_DEV = 8
AXES = ("x", "y", "c")

DEPTH = 2
ALPHA = (2.0 * DEPTH) ** 0.25
LN_EPS = 1e-5
FOX_HEAD_DIM = 128
SWA_HEAD_DIM = 64
SWA_GROUP = 8
SWA_WINDOW = 128
Q_BLOCK = 128
ROPE_DIM = 16
ROPE_THETA = 500000.0

ADAM_LR = 0.001
ADAM_B1 = 0.9
ADAM_B2 = 0.999
ADAM_EPS = 1e-08
ADAM_WD = 0.01
ADAM_STEP = 10

LANE = 128
MIB = 1024 * 1024


def _tile(n, pref, unit=LANE):
    if n <= pref:
        return n
    t = (pref // unit) * unit
    while t >= unit:
        if n % t == 0:
            return t
        t -= unit
    return n


def _params(sem, vmem_mib=48):
    return pltpu.CompilerParams(dimension_semantics=sem, vmem_limit_bytes=vmem_mib * MIB)


def _sigmoid(x):
    return 1.0 / (1.0 + jnp.exp(-x))


def _mm_call(dot, grid_mnk, in_specs, out_spec, out_shape, k_axis, nk, tm, tn, name, operands):
    sem = ("parallel",) * (len(grid_mnk) - 1) + ("arbitrary",)

    if nk == 1:
        def body(a_ref, b_ref, o_ref):
            o_ref[...] = dot(a_ref[...], b_ref[...]).astype(o_ref.dtype)
        scratch = []
    else:
        def body(a_ref, b_ref, o_ref, acc_ref):
            k = pl.program_id(k_axis)

            @pl.when(k == 0)
            def _():
                acc_ref[...] = jnp.zeros_like(acc_ref)

            acc_ref[...] += dot(a_ref[...], b_ref[...])

            @pl.when(k == nk - 1)
            def _():
                o_ref[...] = acc_ref[...].astype(o_ref.dtype)
        scratch = [pltpu.VMEM((tm, tn), F32)]

    return pl.pallas_call(
        body, name=name, grid=grid_mnk, in_specs=in_specs, out_specs=out_spec, out_shape=out_shape,
        scratch_shapes=scratch, compiler_params=_params(sem, 56),
    )(*operands)


def _dot(dims):
    def dot(a, b):
        return lax.dot_general(a.astype(BF16), b.astype(BF16), (dims, ((), ())), preferred_element_type=F32)
    return dot


def _mm_nn(a, b, out_dtype, name, tm=2048, tn=512, tk=2048):
    M, K = a.shape
    N = b.shape[1]
    tm, tn, tk = _tile(M, tm), _tile(N, tn), _tile(K, tk)
    nk = K // tk
    return _mm_call(
        _dot(((1,), (0,))), (M // tm, N // tn, nk),
        [pl.BlockSpec((tm, tk), lambda i, j, k: (i, k)), pl.BlockSpec((tk, tn), lambda i, j, k: (k, j))],
        pl.BlockSpec((tm, tn), lambda i, j, k: (i, j)), jax.ShapeDtypeStruct((M, N), out_dtype),
        2, nk, tm, tn, name, (a, b))


def _mm_nt(a, b, out_dtype, name, tm=2048, tn=512, tk=2048):
    P, M, K = a.shape
    N = b.shape[1]
    tm, tn, tk = _tile(M, tm), _tile(N, tn), _tile(K, tk)
    nk = K // tk
    return _mm_call(
        _dot(((1,), (1,))), (M // tm, N // tn, P * nk),
        [pl.BlockSpec((None, tm, tk), lambda i, j, k: (k // nk, i, k % nk)),
         pl.BlockSpec((None, tn, tk), lambda i, j, k: (k // nk, j, k % nk))],
        pl.BlockSpec((tm, tn), lambda i, j, k: (i, j)), jax.ShapeDtypeStruct((M, N), out_dtype),
        2, P * nk, tm, tn, name, (a, b))


def _mm_tn(a, b, out_dtype, name, tm=2048, tn=512, tk=2048):
    K, M = a.shape
    P, _, N = b.shape
    tm, tn, tk = _tile(M, tm), _tile(N, tn), _tile(K, tk)
    nk = K // tk
    return _mm_call(
        _dot(((0,), (0,))), (P, M // tm, N // tn, nk),
        [pl.BlockSpec((tk, tm), lambda p, i, j, k: (k, i)), pl.BlockSpec((None, tk, tn), lambda p, i, j, k: (p, k, j))],
        pl.BlockSpec((None, tm, tn), lambda p, i, j, k: (p, i, j)), jax.ShapeDtypeStruct((P, M, N), out_dtype),
        3, nk, tm, tn, name, (a, b))


ROW_TILE = 256


def _row_spec(tm, D):
    return pl.BlockSpec((tm, D), lambda i: (i, 0))


def _vec_spec(D):
    return pl.BlockSpec((1, D), lambda i: (0, 0))


def _modulate(x, sc, sh, name):
    S, D = x.shape
    tm = _tile(S, ROW_TILE, 8)

    def body(x_ref, sc_ref, sh_ref, h_ref):
        h_ref[...] = (x_ref[...] * (1.0 + sc_ref[...]) + sh_ref[...]).astype(BF16)

    return pl.pallas_call(
        body, name=name, grid=(S // tm,),
        in_specs=[_row_spec(tm, D), _vec_spec(D), _vec_spec(D)],
        out_specs=_row_spec(tm, D),
        out_shape=jax.ShapeDtypeStruct((S, D), BF16),
        compiler_params=_params(("parallel",)),
    )(x, sc, sh)


def _layer_norm_rows(z, gamma, beta):
    mu = jnp.mean(z, axis=-1, keepdims=True)
    zc = z - mu
    var = jnp.mean(zc * zc, axis=-1, keepdims=True)
    return zc * lax.rsqrt(var + LN_EPS) * gamma + beta


def _ln_fwd(x, y, gate, gamma, beta, sc_n, sh_n, name):
    S, D = x.shape
    tm = _tile(S, ROW_TILE, 8)

    def body(x_ref, y_ref, gate_ref, g_ref, b_ref, sc_ref, sh_ref, z_ref, xo_ref, hn_ref):
        z = ALPHA * x_ref[...] + (1.0 + gate_ref[...]) * y_ref[...]
        xo = _layer_norm_rows(z, g_ref[...], b_ref[...])
        z_ref[...] = z
        xo_ref[...] = xo
        hn_ref[...] = (xo * (1.0 + sc_ref[...]) + sh_ref[...]).astype(BF16)

    return pl.pallas_call(
        body, name=name, grid=(S // tm,),
        in_specs=[_row_spec(tm, D), _row_spec(tm, D)] + [_vec_spec(D)] * 5,
        out_specs=[_row_spec(tm, D)] * 3,
        out_shape=[jax.ShapeDtypeStruct((S, D), F32), jax.ShapeDtypeStruct((S, D), F32),
                   jax.ShapeDtypeStruct((S, D), BF16)],
        compiler_params=_params(("parallel",)),
    )(x, y, gate, gamma, beta, sc_n, sh_n)


def _ln_fwd_loss(x, y, gate, gamma, beta, target, name):
    S, D = x.shape
    tm = _tile(S, ROW_TILE, 8)

    def body(x_ref, y_ref, gate_ref, g_ref, b_ref, t_ref, z_ref, dout_ref, loss_ref):
        @pl.when(pl.program_id(0) == 0)
        def _():
            loss_ref[...] = jnp.zeros_like(loss_ref)

        z = ALPHA * x_ref[...] + (1.0 + gate_ref[...]) * y_ref[...]
        xo = _layer_norm_rows(z, g_ref[...], b_ref[...])
        err = xo - t_ref[...]
        z_ref[...] = z
        dout_ref[...] = err * (1.0 / D)
        loss_ref[...] += (0.5 / D) * jnp.sum(err * err)

    return pl.pallas_call(
        body, name=name, grid=(S // tm,),
        in_specs=[_row_spec(tm, D), _row_spec(tm, D)] + [_vec_spec(D)] * 3 + [_row_spec(tm, D)],
        out_specs=[_row_spec(tm, D), _row_spec(tm, D), pl.BlockSpec((1, LANE), lambda i: (0, 0))],
        out_shape=[jax.ShapeDtypeStruct((S, D), F32), jax.ShapeDtypeStruct((S, D), F32),
                   jax.ShapeDtypeStruct((1, LANE), F32)],
        compiler_params=_params(("arbitrary",)),
    )(x, y, gate, gamma, beta, target)


def _ln_bwd(dout, z, y, gate, gamma, name):
    S, D = z.shape
    tm = _tile(S, ROW_TILE, 8)

    def body(dout_ref, z_ref, y_ref, gate_ref, g_ref, dz_ref, dy_ref, dg_ref, db_ref, dgate_ref):
        @pl.when(pl.program_id(0) == 0)
        def _():
            dg_ref[...] = jnp.zeros_like(dg_ref)
            db_ref[...] = jnp.zeros_like(db_ref)
            dgate_ref[...] = jnp.zeros_like(dgate_ref)

        z = z_ref[...]
        dout = dout_ref[...]
        mu = jnp.mean(z, axis=-1, keepdims=True)
        zc = z - mu
        var = jnp.mean(zc * zc, axis=-1, keepdims=True)
        rstd = lax.rsqrt(var + LN_EPS)
        xhat = zc * rstd
        dxhat = dout * g_ref[...]
        m1 = jnp.mean(dxhat, axis=-1, keepdims=True)
        m2 = jnp.mean(dxhat * xhat, axis=-1, keepdims=True)
        dz = rstd * (dxhat - m1 - xhat * m2)
        dz_ref[...] = dz
        dy_ref[...] = (dz * (1.0 + gate_ref[...])).astype(BF16)
        dg_ref[...] += jnp.sum(dout * xhat, axis=0, keepdims=True)
        db_ref[...] += jnp.sum(dout, axis=0, keepdims=True)
        dgate_ref[...] += jnp.sum(dz * y_ref[...], axis=0, keepdims=True)

    return pl.pallas_call(
        body, name=name, grid=(S // tm,),
        in_specs=[_row_spec(tm, D)] * 3 + [_vec_spec(D)] * 2,
        out_specs=[_row_spec(tm, D), _row_spec(tm, D)] + [_vec_spec(D)] * 3,
        out_shape=[jax.ShapeDtypeStruct((S, D), F32), jax.ShapeDtypeStruct((S, D), BF16)]
        + [jax.ShapeDtypeStruct((1, D), F32)] * 3,
        compiler_params=_params(("arbitrary",)),
    )(dout, z, y, gate, gamma)


def _mod_bwd(dz, dh, xin, sc, after, name):
    S, D = dz.shape
    tm = _tile(S, ROW_TILE, 8)

    def body(dz_ref, dh_ref, x_ref, sc_ref, after_ref, dx_ref, dsc_ref, dsh_ref):
        @pl.when(pl.program_id(0) == 0)
        def _():
            dsc_ref[...] = jnp.zeros_like(dsc_ref)
            dsh_ref[...] = jnp.zeros_like(dsh_ref)

        dh = dh_ref[...]
        dx_ref[...] = ALPHA * dz_ref[...] + dh * (1.0 + sc_ref[...])
        dsc_ref[...] += jnp.sum(dh * x_ref[...], axis=0, keepdims=True)
        dsh_ref[...] += jnp.sum(dh, axis=0, keepdims=True)

    return pl.pallas_call(
        body, name=name, grid=(S // tm,),
        in_specs=[_row_spec(tm, D)] * 3 + [_vec_spec(D), pl.BlockSpec(memory_space=pl.ANY)],
        out_specs=[_row_spec(tm, D), _vec_spec(D), _vec_spec(D)],
        out_shape=[jax.ShapeDtypeStruct((S, D), F32)] + [jax.ShapeDtypeStruct((1, D), F32)] * 2,
        compiler_params=_params(("arbitrary",)),
    )(dz, dh, xin, sc, after)


def _shift_down(u, k, row):
    return jnp.where(row >= k, pltpu.roll(u, k, axis=0), 0.0)


def _shift_up(u, k, row, S):
    return jnp.where(row < S - k, pltpu.roll(u, S - k, axis=0), 0.0)


def _ffn_up(h, w, cw, cb, name):
    S, D = h.shape
    F = w.shape[2]
    tn = _tile(F, 256)

    def body(h_ref, w_ref, cw_ref, cb_ref, u_ref, a_ref):
        hh = h_ref[...]
        row = lax.broadcasted_iota(jnp.int32, (S, tn), 0)
        conv = []
        for p in range(2):
            u = jnp.dot(hh, w_ref[p], preferred_element_type=F32)
            u_ref[p] = u
            cwp = cw_ref[p]
            conv.append(_shift_down(u, 2, row) * cwp[0:1] + _shift_down(u, 1, row) * cwp[1:2]
                        + u * cwp[2:3] + cb_ref[p])
        g, v = conv
        a_ref[...] = (g * _sigmoid(g) * v).astype(BF16)

    return pl.pallas_call(
        body, name=name, grid=(F // tn,),
        in_specs=[pl.BlockSpec((S, D), lambda j: (0, 0)), pl.BlockSpec((2, D, tn), lambda j: (0, 0, j)),
                  pl.BlockSpec((2, 3, tn), lambda j: (0, 0, j)), pl.BlockSpec((2, 1, tn), lambda j: (0, 0, j))],
        out_specs=[pl.BlockSpec((2, S, tn), lambda j: (0, 0, j)), pl.BlockSpec((S, tn), lambda j: (0, j))],
        out_shape=[jax.ShapeDtypeStruct((2, S, F), F32), jax.ShapeDtypeStruct((S, F), BF16)],
        compiler_params=_params(("parallel",), 56),
    )(h, w, cw, cb)


def _ffn_bwd_elem(da, u, cw, cb, after, name):
    _, S, F = u.shape
    tn = _tile(F, 256)

    def body(da_ref, u_ref, cw_ref, cb_ref, after_ref, du_ref, dcw_ref, dcb_ref):
        row = lax.broadcasted_iota(jnp.int32, (S, tn), 0)
        da = da_ref[...]
        shifted, conv = [], []
        for p in range(2):
            u = u_ref[p]
            u1, u2 = _shift_down(u, 1, row), _shift_down(u, 2, row)
            cwp = cw_ref[p]
            shifted.append((u2, u1, u))
            conv.append(u2 * cwp[0:1] + u1 * cwp[1:2] + u * cwp[2:3] + cb_ref[p])
        g, v = conv
        sg = _sigmoid(g)
        d_conv = (da * v * (sg * (1.0 + g * (1.0 - sg))), da * (g * sg))
        for p in range(2):
            d = d_conv[p]
            cwp = cw_ref[p]
            dcb_ref[p] = jnp.sum(d, axis=0, keepdims=True)
            for j in range(3):
                dcw_ref[p, j:j + 1, :] = jnp.sum(d * shifted[p][j], axis=0, keepdims=True)
            du = d * cwp[2:3] + _shift_up(d, 1, row, S) * cwp[1:2] + _shift_up(d, 2, row, S) * cwp[0:1]
            du_ref[p] = du.astype(BF16)

    return pl.pallas_call(
        body, name=name, grid=(F // tn,),
        in_specs=[pl.BlockSpec((S, tn), lambda j: (0, j)), pl.BlockSpec((2, S, tn), lambda j: (0, 0, j)),
                  pl.BlockSpec((2, 3, tn), lambda j: (0, 0, j)), pl.BlockSpec((2, 1, tn), lambda j: (0, 0, j)),
                  pl.BlockSpec(memory_space=pl.ANY)],
        out_specs=[pl.BlockSpec((2, S, tn), lambda j: (0, 0, j)), pl.BlockSpec((2, 3, tn), lambda j: (0, 0, j)),
                   pl.BlockSpec((2, 1, tn), lambda j: (0, 0, j))],
        out_shape=[jax.ShapeDtypeStruct((2, S, F), BF16), jax.ShapeDtypeStruct((2, 3, F), F32),
                   jax.ShapeDtypeStruct((2, 1, F), F32)],
        compiler_params=_params(("parallel",), 56),
    )(da, u, cw, cb, after)


def _split3(x):
    hi = x.astype(BF16)
    r1 = x - hi.astype(F32)
    mid = r1.astype(BF16)
    lo = (r1 - mid.astype(F32)).astype(BF16)
    return hi, mid, lo


def _tri_matmul(x, upper, S):
    tc = _tile(S, 512)
    parts = _split3(x)
    outs = []
    for b in range(S // tc):
        r = lax.broadcasted_iota(jnp.int32, (S, tc), 0)
        c = lax.broadcasted_iota(jnp.int32, (S, tc), 1) + b * tc
        tri = jnp.where((r <= c) if upper else (r >= c), 1.0, 0.0).astype(BF16)
        acc = jnp.dot(parts[0], tri, preferred_element_type=F32)
        acc += jnp.dot(parts[1], tri, preferred_element_type=F32)
        acc += jnp.dot(parts[2], tri, preferred_element_type=F32)
        outs.append(acc)
    return outs, tc


def _fox_prep(flT, bf, name):
    H, S = flT.shape

    def body(fl_ref, b_ref, cum_ref):
        zz = fl_ref[...] + b_ref[...]
        lf = jnp.minimum(zz, 0.0) - jnp.log(1.0 + jnp.exp(-jnp.abs(zz)))
        outs, tc = _tri_matmul(lf, True, S)
        for b, o in enumerate(outs):
            cum_ref[:, b * tc:(b + 1) * tc] = o

    return pl.pallas_call(
        body, name=name,
        in_specs=[pl.BlockSpec(memory_space=pltpu.VMEM)] * 2,
        out_specs=pl.BlockSpec(memory_space=pltpu.VMEM),
        out_shape=jax.ShapeDtypeStruct((H, S), F32),
        compiler_params=pltpu.CompilerParams(vmem_limit_bytes=48 * MIB),
    )(flT, bf)


def _fox_prep_bwd(dcumT, flT, bf, name):
    H, S = flT.shape

    def body(dc_ref, fl_ref, b_ref, dfl_ref, dbf_ref):
        zz = fl_ref[...] + b_ref[...]
        outs, tc = _tri_matmul(dc_ref[...], False, S)
        total = jnp.zeros((H, 1), F32)
        for b, o in enumerate(outs):
            dfl = o * _sigmoid(-zz[:, b * tc:(b + 1) * tc])
            dfl_ref[:, b * tc:(b + 1) * tc] = dfl
            total += jnp.sum(dfl, axis=1, keepdims=True)
        dbf_ref[...] = total

    return pl.pallas_call(
        body, name=name,
        in_specs=[pl.BlockSpec(memory_space=pltpu.VMEM)] * 3,
        out_specs=[pl.BlockSpec(memory_space=pltpu.VMEM)] * 2,
        out_shape=[jax.ShapeDtypeStruct((H, S), F32), jax.ShapeDtypeStruct((H, 1), F32)],
        compiler_params=pltpu.CompilerParams(vmem_limit_bytes=48 * MIB),
    )(dcumT, flT, bf)


FOX_TQ = 256


def _fox_scores(q_ref, k_ref, cq_ref, ck_ref, i, tq, S, scale):
    q = q_ref[...].astype(BF16)
    k = k_ref[...].astype(BF16)
    s = lax.dot_general(q, k, (((1,), (1,)), ((), ())), preferred_element_type=F32) * scale
    s = s + cq_ref[...] - ck_ref[...]
    qpos = i * tq + lax.broadcasted_iota(jnp.int32, (tq, S), 0)
    kpos = lax.broadcasted_iota(jnp.int32, (tq, S), 1)
    return jnp.where(kpos <= qpos, s, -jnp.inf), q, k


def _fox_attn_fwd(proj, cum_col, cum_row, H, name):
    S = proj.shape[0]
    dh = FOX_HEAD_DIM
    tq = _tile(S, FOX_TQ)
    scale = dh ** -0.5

    def body(q_ref, k_ref, v_ref, cq_ref, ck_ref, o_ref, lse_ref):
        s, _, _ = _fox_scores(q_ref, k_ref, cq_ref, ck_ref, pl.program_id(1), tq, S, scale)
        m = jnp.max(s, axis=-1, keepdims=True)
        p = jnp.exp(s - m)
        l = jnp.sum(p, axis=-1, keepdims=True)
        o = jnp.dot(p.astype(BF16), v_ref[...].astype(BF16), preferred_element_type=F32) / l
        o_ref[...] = o.astype(BF16)
        lse_ref[...] = m + jnp.log(l)

    return pl.pallas_call(
        body, name=name, grid=(H, S // tq),
        in_specs=[pl.BlockSpec((tq, dh), lambda h, i: (i, h)),
                  pl.BlockSpec((S, dh), lambda h, i: (0, H + h)),
                  pl.BlockSpec((S, dh), lambda h, i: (0, 2 * H + h)),
                  pl.BlockSpec((None, tq, 1), lambda h, i: (h, i, 0)),
                  pl.BlockSpec((None, 1, S), lambda h, i: (h, 0, 0))],
        out_specs=[pl.BlockSpec((tq, dh), lambda h, i: (i, h)),
                   pl.BlockSpec((None, tq, 1), lambda h, i: (h, i, 0))],
        out_shape=[jax.ShapeDtypeStruct((S, H * dh), BF16), jax.ShapeDtypeStruct((H, S, 1), F32)],
        compiler_params=_params(("parallel", "parallel")),
    )(proj, proj, proj, cum_col, cum_row)


def _fox_attn_bwd(proj, do, cum_col, cum_row, lse, H, after, name):
    S = proj.shape[0]
    dh = FOX_HEAD_DIM
    tq = _tile(S, FOX_TQ)
    scale = dh ** -0.5

    def body(q_ref, k_ref, v_ref, do_ref, cq_ref, ck_ref, lse_ref, after_ref, dq_ref, dk_ref, dv_ref, dck_ref):
        i = pl.program_id(1)

        @pl.when(i == 0)
        def _():
            dk_ref[...] = jnp.zeros_like(dk_ref)
            dv_ref[...] = jnp.zeros_like(dv_ref)
            dck_ref[...] = jnp.zeros_like(dck_ref)

        s, q, k = _fox_scores(q_ref, k_ref, cq_ref, ck_ref, i, tq, S, scale)
        p = jnp.exp(s - lse_ref[...])
        do_b = do_ref[...].astype(BF16)
        dp = lax.dot_general(do_b, v_ref[...].astype(BF16), (((1,), (1,)), ((), ())), preferred_element_type=F32)
        delta = jnp.sum(p * dp, axis=-1, keepdims=True)
        ds = p * (dp - delta)
        ds_b = ds.astype(BF16)
        dq_ref[...] = (jnp.dot(ds_b, k, preferred_element_type=F32) * scale).astype(BF16)
        dk_ref[...] += lax.dot_general(ds_b, q, (((0,), (0,)), ((), ())), preferred_element_type=F32) * scale
        dv_ref[...] += lax.dot_general(p.astype(BF16), do_b, (((0,), (0,)), ((), ())), preferred_element_type=F32)
        dck_ref[...] -= jnp.sum(ds, axis=0, keepdims=True)

    W = H * dh
    return pl.pallas_call(
        body, name=name, grid=(H, S // tq),
        in_specs=[pl.BlockSpec((tq, dh), lambda h, i: (i, h)),
                  pl.BlockSpec((S, dh), lambda h, i: (0, H + h)),
                  pl.BlockSpec((S, dh), lambda h, i: (0, 2 * H + h)),
                  pl.BlockSpec((tq, dh), lambda h, i: (i, h)),
                  pl.BlockSpec((None, tq, 1), lambda h, i: (h, i, 0)),
                  pl.BlockSpec((None, 1, S), lambda h, i: (h, 0, 0)),
                  pl.BlockSpec((None, tq, 1), lambda h, i: (h, i, 0)),
                  pl.BlockSpec(memory_space=pl.ANY)],
        out_specs=[pl.BlockSpec((tq, dh), lambda h, i: (i, h)),
                   pl.BlockSpec((S, dh), lambda h, i: (0, h)),
                   pl.BlockSpec((S, dh), lambda h, i: (0, h)),
                   pl.BlockSpec((None, 1, S), lambda h, i: (h, 0, 0))],
        out_shape=[jax.ShapeDtypeStruct((S, W), BF16), jax.ShapeDtypeStruct((S, W), F32),
                   jax.ShapeDtypeStruct((S, W), F32), jax.ShapeDtypeStruct((H, 1, S), F32)],
        compiler_params=_params(("parallel", "arbitrary")),
    )(proj, proj, proj, do, cum_col, cum_row, lse, after)


def _rope(parts, tabs, out_dtype, name):
    S = parts[0][0].shape[0]
    widths = [a.shape[1] for a, _ in parts]
    total = sum(widths)
    tm = _tile(S, ROW_TILE, 8)
    flags = [r for _, r in parts]

    def body(*refs):
        in_refs = refs[:len(parts)]
        cos_ref, sa_ref, sb_ref, o_ref = refs[len(parts):]
        cos, sa, sb = cos_ref[...], sa_ref[...], sb_ref[...]
        off = 0
        for ref, rot, w in zip(in_refs, flags, widths):
            for j in range(w // LANE):
                t = ref[:, j * LANE:(j + 1) * LANE]
                if rot:
                    t = t * cos + pltpu.roll(t, LANE - ROPE_DIM // 2, axis=1) * sa + pltpu.roll(t, ROPE_DIM // 2, axis=1) * sb
                o_ref[:, off + j * LANE:off + (j + 1) * LANE] = t.astype(o_ref.dtype)
            off += w

    return pl.pallas_call(
        body, name=name, grid=(S // tm,),
        in_specs=[pl.BlockSpec((tm, w), lambda i: (i, 0)) for w in widths] + [_row_spec(tm, LANE)] * 3,
        out_specs=_row_spec(tm, total),
        out_shape=jax.ShapeDtypeStruct((S, total), out_dtype),
        compiler_params=_params(("parallel",)),
    )(*[a for a, _ in parts], *tabs)


def _swa_band(ref_p, ref_c, hk):
    dh = SWA_HEAD_DIM
    return jnp.concatenate([ref_p[:, hk * dh:(hk + 1) * dh], ref_c[:, hk * dh:(hk + 1) * dh]], axis=0).astype(BF16)


def _swa_mask(n):
    qi = lax.broadcasted_iota(jnp.int32, (Q_BLOCK, 2 * Q_BLOCK), 0)
    kj = lax.broadcasted_iota(jnp.int32, (Q_BLOCK, 2 * Q_BLOCK), 1)
    rel = qi + Q_BLOCK - kj
    return (rel >= 0) & (rel < SWA_WINDOW) & ((kj >= Q_BLOCK) | (n > 0))


def _swa_attn_fwd(qk, proj, sinks, Hq, name):
    S = qk.shape[0]
    dh, G, QB = SWA_HEAD_DIM, SWA_GROUP, Q_BLOCK
    Hk = Hq // G
    Wq, Wk = Hq * dh, Hk * dh
    nb = S // QB
    scale = dh ** -0.5

    def body(q_ref, kp_ref, kc_ref, vp_ref, vc_ref, sink_ref, o_ref, lse_ref):
        n = pl.program_id(0)
        mask = _swa_mask(n)
        lane = lax.broadcasted_iota(jnp.int32, (QB, LANE), 1)
        lse_tile = jnp.zeros((QB, LANE), F32)
        for hk in range(Hk):
            kb = _swa_band(kp_ref, kc_ref, hk)
            vb = _swa_band(vp_ref, vc_ref, hk)
            outs = []
            for g in range(G):
                h = hk * G + g
                q = q_ref[:, h * dh:(h + 1) * dh].astype(BF16)
                s = lax.dot_general(q, kb, (((1,), (1,)), ((), ())), preferred_element_type=F32) * scale
                s = jnp.where(mask, s, -jnp.inf)
                sk = sink_ref[0:1, h:h + 1]
                m = jnp.maximum(jnp.max(s, axis=-1, keepdims=True), sk)
                p = jnp.exp(s - m)
                l = jnp.sum(p, axis=-1, keepdims=True) + jnp.exp(sk - m)
                outs.append(jnp.dot(p.astype(BF16), vb, preferred_element_type=F32) / l)
                lse_tile = jnp.where(lane == h, m + jnp.log(l), lse_tile)
            for g in range(0, G, 2):
                c0 = (hk * G + g) * dh
                o_ref[:, c0:c0 + 2 * dh] = jnp.concatenate([outs[g], outs[g + 1]], axis=1).astype(BF16)
        lse_ref[...] = lse_tile

    kcol, vcol = Wq // Wk, (Wq + Wk) // Wk
    return pl.pallas_call(
        body, name=name, grid=(nb,),
        in_specs=[pl.BlockSpec((QB, Wq), lambda n: (n, 0)),
                  pl.BlockSpec((QB, Wk), lambda n: (jnp.maximum(n - 1, 0), kcol)),
                  pl.BlockSpec((QB, Wk), lambda n: (n, kcol)),
                  pl.BlockSpec((QB, Wk), lambda n: (jnp.maximum(n - 1, 0), vcol)),
                  pl.BlockSpec((QB, Wk), lambda n: (n, vcol)),
                  pl.BlockSpec((1, LANE), lambda n: (0, 0))],
        out_specs=[pl.BlockSpec((QB, Wq), lambda n: (n, 0)), pl.BlockSpec((QB, LANE), lambda n: (n, 0))],
        out_shape=[jax.ShapeDtypeStruct((S, Wq), BF16), jax.ShapeDtypeStruct((S, LANE), F32)],
        compiler_params=_params(("parallel",)),
    )(qk, qk, qk, proj, proj, sinks)


def _swa_attn_bwd(qk, proj, sinks, do, lse, Hq, after, name):
    S = qk.shape[0]
    dh, G, QB = SWA_HEAD_DIM, SWA_GROUP, Q_BLOCK
    Hk = Hq // G
    Wq, Wk = Hq * dh, Hk * dh
    nb = S // QB
    scale = dh ** -0.5

    def body(q_ref, kp_ref, kc_ref, vp_ref, vc_ref, sink_ref, do_ref, lse_ref, after_ref,
             dq_ref, dk_ref, dv_ref, dsink_ref, carry_k, carry_v):
        n = pl.program_id(0)

        @pl.when(n == 0)
        def _():
            dsink_ref[...] = jnp.zeros_like(dsink_ref)

        @pl.when(n < nb)
        def _():
            mask = _swa_mask(n)
            lane = lax.broadcasted_iota(jnp.int32, (1, LANE), 1)
            dsink = jnp.zeros((1, LANE), F32)
            dk_heads, dv_heads = [], []
            for hk in range(Hk):
                kb = _swa_band(kp_ref, kc_ref, hk)
                vb = _swa_band(vp_ref, vc_ref, hk)
                dkb = jnp.zeros((2 * QB, dh), F32)
                dvb = jnp.zeros((2 * QB, dh), F32)
                dqs = []
                for g in range(G):
                    h = hk * G + g
                    q = q_ref[:, h * dh:(h + 1) * dh].astype(BF16)
                    do_h = do_ref[:, h * dh:(h + 1) * dh].astype(BF16)
                    lse_h = lse_ref[:, h:h + 1]
                    s = lax.dot_general(q, kb, (((1,), (1,)), ((), ())), preferred_element_type=F32) * scale
                    s = jnp.where(mask, s, -jnp.inf)
                    p = jnp.exp(s - lse_h)
                    p_sink = jnp.exp(sink_ref[0:1, h:h + 1] - lse_h)
                    dp = lax.dot_general(do_h, vb, (((1,), (1,)), ((), ())), preferred_element_type=F32)
                    delta = jnp.sum(p * dp, axis=-1, keepdims=True)
                    ds_b = (p * (dp - delta)).astype(BF16)
                    dqs.append(jnp.dot(ds_b, kb, preferred_element_type=F32) * scale)
                    dkb += lax.dot_general(ds_b, q, (((0,), (0,)), ((), ())), preferred_element_type=F32) * scale
                    dvb += lax.dot_general(p.astype(BF16), do_h, (((0,), (0,)), ((), ())), preferred_element_type=F32)
                    dsink = jnp.where(lane == h, -jnp.sum(p_sink * delta, axis=0, keepdims=True), dsink)
                for g in range(0, G, 2):
                    c0 = (hk * G + g) * dh
                    dq_ref[:, c0:c0 + 2 * dh] = jnp.concatenate([dqs[g], dqs[g + 1]], axis=1)
                dk_heads.append(dkb)
                dv_heads.append(dvb)
            dsink_ref[...] += dsink
            dk_all = jnp.concatenate(dk_heads, axis=1)
            dv_all = jnp.concatenate(dv_heads, axis=1)

            @pl.when(n > 0)
            def _():
                dk_ref[...] = carry_k[...] + dk_all[:QB]
                dv_ref[...] = carry_v[...] + dv_all[:QB]

            carry_k[...] = dk_all[QB:]
            carry_v[...] = dv_all[QB:]

        @pl.when(n == nb)
        def _():
            dk_ref[...] = carry_k[...]
            dv_ref[...] = carry_v[...]

    kcol, vcol = Wq // Wk, (Wq + Wk) // Wk
    cur = lambda n: jnp.minimum(n, nb - 1)
    prev = lambda n: jnp.maximum(jnp.minimum(n, nb - 1) - 1, 0)
    return pl.pallas_call(
        body, name=name, grid=(nb + 1,),
        in_specs=[pl.BlockSpec((QB, Wq), lambda n: (cur(n), 0)),
                  pl.BlockSpec((QB, Wk), lambda n: (prev(n), kcol)),
                  pl.BlockSpec((QB, Wk), lambda n: (cur(n), kcol)),
                  pl.BlockSpec((QB, Wk), lambda n: (prev(n), vcol)),
                  pl.BlockSpec((QB, Wk), lambda n: (cur(n), vcol)),
                  pl.BlockSpec((1, LANE), lambda n: (0, 0)),
                  pl.BlockSpec((QB, Wq), lambda n: (cur(n), 0)),
                  pl.BlockSpec((QB, LANE), lambda n: (cur(n), 0)),
                  pl.BlockSpec(memory_space=pl.ANY)],
        out_specs=[pl.BlockSpec((QB, Wq), lambda n: (cur(n), 0)),
                   pl.BlockSpec((QB, Wk), lambda n: (jnp.maximum(n - 1, 0), 0)),
                   pl.BlockSpec((QB, Wk), lambda n: (jnp.maximum(n - 1, 0), 0)),
                   pl.BlockSpec((1, LANE), lambda n: (0, 0))],
        out_shape=[jax.ShapeDtypeStruct((S, Wq), F32), jax.ShapeDtypeStruct((S, Wk), F32),
                   jax.ShapeDtypeStruct((S, Wk), F32), jax.ShapeDtypeStruct((1, LANE), F32)],
        scratch_shapes=[pltpu.VMEM((QB, Wk), F32), pltpu.VMEM((QB, Wk), F32)],
        compiler_params=_params(("arbitrary",)),
    )(qk, qk, qk, proj, proj, sinks, do, lse, after)


ADA_ROWS = 16


def _ada_mod(c_pad, w, b, name):
    L, D, N = w.shape
    tn = _tile(N, 512)

    def body(c_ref, w_ref, b_ref, o_ref):
        c = c_ref[...]
        c = c * _sigmoid(c)
        ch = c.astype(BF16)
        cl = (c - ch.astype(F32)).astype(BF16)
        ww = w_ref[...]
        wh = ww.astype(BF16)
        wl = (ww - wh.astype(F32)).astype(BF16)
        acc = jnp.dot(ch, wh, preferred_element_type=F32)
        acc += jnp.dot(ch, wl, preferred_element_type=F32)
        acc += jnp.dot(cl, wh, preferred_element_type=F32)
        o_ref[...] = acc + b_ref[...]

    return pl.pallas_call(
        body, name=name, grid=(L, N // tn),
        in_specs=[pl.BlockSpec((ADA_ROWS, D), lambda l, j: (0, 0)),
                  pl.BlockSpec((None, D, tn), lambda l, j: (l, 0, j)),
                  pl.BlockSpec((None, 1, tn), lambda l, j: (l, 0, j))],
        out_specs=pl.BlockSpec((None, ADA_ROWS, tn), lambda l, j: (l, 0, j)),
        out_shape=jax.ShapeDtypeStruct((L, ADA_ROWS, N), F32),
        compiler_params=_params(("parallel", "parallel")),
    )(c_pad, w, b)


def _ada_bwd(cT, dm, name):
    D = cT.shape[0]
    L, B, N = dm.shape
    tm = _tile(D, 256)

    def body(c_ref, dm_ref, o_ref):
        c = c_ref[...]
        c = c * _sigmoid(c)
        dmv = dm_ref[...]
        acc = c[:, 0:1] * dmv[0:1, :]
        for b in range(1, B):
            acc += c[:, b:b + 1] * dmv[b:b + 1, :]
        o_ref[...] = acc

    return pl.pallas_call(
        body, name=name, grid=(L, D // tm),
        in_specs=[pl.BlockSpec((tm, B), lambda l, i: (i, 0)), pl.BlockSpec((None, B, N), lambda l, i: (l, 0, 0))],
        out_specs=pl.BlockSpec((None, tm, N), lambda l, i: (l, i, 0)),
        out_shape=jax.ShapeDtypeStruct((L, D, N), F32),
        compiler_params=_params(("parallel", "parallel")),
    )(cT, dm)


def _adamw_math(w, g, m, v):
    m = ADAM_B1 * m + (1.0 - ADAM_B1) * g
    v = ADAM_B2 * v + (1.0 - ADAM_B2) * (g * g)
    m_hat = m / (1.0 - ADAM_B1 ** ADAM_STEP)
    v_hat = v / (1.0 - ADAM_B2 ** ADAM_STEP)
    delta = -ADAM_LR * (m_hat / (jnp.sqrt(v_hat) + ADAM_EPS) + ADAM_WD * w)
    return delta, m, v


def _adam_rows(R, C):
    lanes = -(-C // LANE) * LANE
    return _tile(R, max(8, (262144 // lanes) // 8 * 8), 8)


def _adam_sum(recv, w, m, v, layer, filled, name):
    P, R, C = recv.shape
    L = w.shape[0]
    tr = _adam_rows(R, C)
    n_keep = 0 if filled is None else 4

    def body(r_ref, w_ref, m_ref, v_ref, *rest):
        g_ref, d_ref, mo_ref, vo_ref = rest[n_keep:]
        g = r_ref[0].astype(F32)
        for p in range(1, P):
            g = g + r_ref[p].astype(F32)
        delta, mn, vn = _adamw_math(w_ref[...], g, m_ref[...], v_ref[...])
        g_ref[...] = g
        d_ref[...] = delta
        mo_ref[...] = mn
        vo_ref[...] = vn

    spec = pl.BlockSpec((None, tr, C), lambda i: (layer, i, 0))
    return pl.pallas_call(
        body, name=name, grid=(R // tr,),
        in_specs=[pl.BlockSpec((P, tr, C), lambda i: (0, i, 0)), spec, spec, spec]
        + [pl.BlockSpec(memory_space=pl.ANY)] * n_keep,
        out_specs=[spec] * 4,
        out_shape=[jax.ShapeDtypeStruct((L, R, C), F32)] * 4,
        input_output_aliases={4 + k: k for k in range(n_keep)},
        compiler_params=_params(("parallel",)),
    )(recv, w, m, v, *(filled or ()))


def _adam(g, w, m, v, name):
    L, R, C = w.shape
    tr = _adam_rows(R, C)

    def body(g_ref, w_ref, m_ref, v_ref, d_ref, mo_ref, vo_ref):
        delta, mn, vn = _adamw_math(w_ref[...], g_ref[...], m_ref[...], v_ref[...])
        d_ref[...] = delta
        mo_ref[...] = mn
        vo_ref[...] = vn

    spec = pl.BlockSpec((None, tr, C), lambda l, i: (l, i, 0))
    return pl.pallas_call(
        body, name=name, grid=(L, R // tr),
        in_specs=[spec] * 4, out_specs=[spec] * 3,
        out_shape=[jax.ShapeDtypeStruct((L, R, C), F32)] * 3,
        compiler_params=_params(("parallel", "parallel")),
    )(g, w, m, v)


def _sum_slots(x, name):
    P, R, C = x.shape

    def body(x_ref, o_ref):
        acc = x_ref[0]
        for p in range(1, P):
            acc = acc + x_ref[p]
        o_ref[...] = acc

    return pl.pallas_call(
        body, name=name,
        in_specs=[pl.BlockSpec(memory_space=pltpu.VMEM)], out_specs=pl.BlockSpec(memory_space=pltpu.VMEM),
        out_shape=jax.ShapeDtypeStruct((R, C), F32),
        compiler_params=pltpu.CompilerParams(vmem_limit_bytes=48 * MIB),
    )(x)


def _my_pos():
    return lax.axis_index("x"), lax.axis_index("y"), lax.axis_index("c")


def _all_gather_small(x, name):
    R, C = x.shape

    def body(x_ref, out_ref, send_sems, recv_sems):
        x_, y_, c_ = _my_pos()
        me, sibling = (x_, y_, c_), (x_, y_, 1 - c_)
        chips = [(1 - x_, y_), (x_, 1 - y_), (1 - x_, 1 - y_)]

        def slot(px, py, pc):
            return out_ref.at[4 * px + 2 * py + pc]

        def copy(k, block, to):
            return pltpu.make_async_remote_copy(
                src_ref=slot(*block), dst_ref=slot(*block), send_sem=send_sems.at[k], recv_sem=recv_sems.at[k],
                device_id=to, device_id_type=MESH)

        out_ref[4 * x_ + 2 * y_ + c_] = x_ref[...]
        first = [copy(0, me, sibling)] + [copy(1 + j, me, (*chip, c_)) for j, chip in enumerate(chips)]
        for cp in first:
            cp.start()
        passed = [copy(4 + j, (*chip, c_), sibling) for j, chip in enumerate(chips)]
        for j, chip in enumerate(chips):
            copy(1 + j, (*chip, c_), me).wait_recv()
            passed[j].start()
        copy(0, sibling, me).wait_recv()
        for j, chip in enumerate(chips):
            copy(4 + j, (*chip, 1 - c_), me).wait_recv()
        for cp in first + passed:
            cp.wait_send()

    return pl.pallas_call(
        body, name=name,
        in_specs=[pl.BlockSpec(memory_space=pltpu.VMEM)], out_specs=pl.BlockSpec(memory_space=pltpu.VMEM),
        out_shape=jax.ShapeDtypeStruct((N_DEV, R, C), x.dtype),
        scratch_shapes=[pltpu.SemaphoreType.DMA((7,)), pltpu.SemaphoreType.DMA((7,))],
        compiler_params=pltpu.CompilerParams(vmem_limit_bytes=48 * MIB),
    )(x)


HBM_SPEC = pl.BlockSpec(memory_space=pltpu.HBM)
SEM_SPEC = pl.BlockSpec(memory_space=pltpu.SEMAPHORE)
ANY_SPEC = pl.BlockSpec(memory_space=pl.ANY)
SPLIT_EFFECT = pltpu.SideEffectType.DATAFLOW_SIDE_EFFECTING


def _in_hbm(a):
    return pltpu.with_memory_space_constraint(a, pltpu.HBM)


def _gathered_shape(a, kind):
    if kind == "major":
        return (N_DEV,) + a.shape
    if kind == "rows":
        return (N_DEV * a.shape[0], a.shape[1])
    return (2, a.shape[0], 4 * a.shape[1])


def _gather_slot(ref, kind, block, shard_shape):
    px, py, pc = block
    if kind == "major":
        return ref.at[4 * px + 2 * py + pc]
    if kind == "rows":
        r = shard_shape[0]
        return ref.at[pl.ds(pl.multiple_of((4 * px + 2 * py + pc) * r, r), r), :]
    cu = shard_shape[1]
    return ref.at[px, :, pl.ds(pl.multiple_of((2 * py + pc) * cu, cu), cu)]


def _gather_peers():
    x_, y_, c_ = _my_pos()
    return (x_, y_, c_), (x_, y_, 1 - c_), [(1 - x_, y_), (x_, 1 - y_), (1 - x_, 1 - y_)]


def _gather_start(shards, kinds, after, name):
    n = len(shards)
    bufs = [lax.empty(_gathered_shape(a, k), a.dtype) for a, k in zip(shards, kinds)]
    extra = [] if after is None else [after]

    def body(*refs):
        shard_refs, buf_refs = refs[:n], refs[n:2 * n]
        send_sems, recv_sems, local_sems = refs[2 * n + len(extra):2 * n + len(extra) + 3]
        token = refs[-1]
        me, sibling, chips = _gather_peers()
        for e in range(n):
            mine = _gather_slot(buf_refs[e], kinds[e], me, shards[e].shape)
            pltpu.make_async_copy(shard_refs[e], mine, local_sems.at[e]).start()
            for k, to in enumerate([sibling] + [(*chip, me[2]) for chip in chips]):
                pltpu.make_async_remote_copy(
                    src_ref=shard_refs[e], dst_ref=mine, send_sem=send_sems.at[4 * e + k],
                    recv_sem=recv_sems.at[4 * e + k], device_id=to, device_id_type=MESH).start()
        token[...] = jnp.zeros_like(token)

    out = pl.pallas_call(
        body, name=name,
        out_shape=(pltpu.SemaphoreType.DMA((4 * n,)), pltpu.SemaphoreType.DMA((4 * n,)), pltpu.SemaphoreType.DMA((n,)),
                   *[pltpu.HBM(a.shape, a.dtype) for a in shards], *[pltpu.HBM(a.shape, a.dtype) for a in bufs],
                   jax.ShapeDtypeStruct((8, LANE), F32)),
        in_specs=[HBM_SPEC] * (2 * n) + [ANY_SPEC] * len(extra),
        out_specs=(SEM_SPEC, SEM_SPEC, SEM_SPEC, *[HBM_SPEC] * (2 * n), pl.BlockSpec(memory_space=pltpu.VMEM)),
        input_output_aliases={i: 3 + i for i in range(2 * n)},
        compiler_params=pltpu.CompilerParams(has_side_effects=SPLIT_EFFECT),
    )(*[_in_hbm(a) for a in shards], *[_in_hbm(a) for a in bufs], *extra)
    return out[0], out[1], out[2], out[3:3 + n], out[3 + n:3 + 2 * n], out[-1]


def _gather_forward(recv_sems, bufs, kinds, shard_shapes, after, name):
    n = len(bufs)

    def body(*refs):
        buf_refs, recv_in = refs[:n], refs[n]
        fsend, frecv = refs[n + 2], refs[n + 3]
        token = refs[-1]
        me, sibling, chips = _gather_peers()
        for e in range(n):
            for j, chip in enumerate(chips):
                slot = _gather_slot(buf_refs[e], kinds[e], (*chip, me[2]), shard_shapes[e])
                pltpu.make_async_remote_copy(
                    src_ref=slot, dst_ref=slot, send_sem=recv_in.at[4 * e + 1 + j], recv_sem=recv_in.at[4 * e + 1 + j],
                    device_id=me, device_id_type=MESH).wait_recv()
                pltpu.make_async_remote_copy(
                    src_ref=slot, dst_ref=slot, send_sem=fsend.at[3 * e + j], recv_sem=frecv.at[3 * e + j],
                    device_id=sibling, device_id_type=MESH).start()
        token[...] = jnp.zeros_like(token)

    out = pl.pallas_call(
        body, name=name,
        out_shape=(pltpu.SemaphoreType.DMA((3 * n,)), pltpu.SemaphoreType.DMA((3 * n,)),
                   *[pltpu.HBM(a.shape, a.dtype) for a in bufs], jax.ShapeDtypeStruct((8, LANE), F32)),
        in_specs=[HBM_SPEC] * n + [SEM_SPEC, ANY_SPEC],
        out_specs=(SEM_SPEC, SEM_SPEC, *[HBM_SPEC] * n, pl.BlockSpec(memory_space=pltpu.VMEM)),
        input_output_aliases={i: 2 + i for i in range(n)},
        compiler_params=pltpu.CompilerParams(has_side_effects=SPLIT_EFFECT),
    )(*bufs, recv_sems, after)
    return out[0], out[1], out[2:2 + n], out[-1]


def _gather_wait(send_sems, recv_sems, local_sems, fsend, frecv, shards, bufs, kinds, after, name):
    n = len(bufs)

    def body(*refs):
        shard_refs, buf_refs = refs[:n], refs[n:2 * n]
        send_in, recv_in, local_in, fsend_in, frecv_in = refs[2 * n:2 * n + 5]
        me, sibling, chips = _gather_peers()

        def arrival(slot, sem):
            return pltpu.make_async_remote_copy(src_ref=slot, dst_ref=slot, send_sem=sem, recv_sem=sem,
                                                device_id=me, device_id_type=MESH)

        for e in range(n):
            shape = shards[e].shape
            mine = _gather_slot(buf_refs[e], kinds[e], me, shape)
            pltpu.make_async_copy(shard_refs[e], mine, local_in.at[e]).wait()
            arrival(_gather_slot(buf_refs[e], kinds[e], sibling, shape), recv_in.at[4 * e]).wait_recv()
            for j, chip in enumerate(chips):
                arrival(_gather_slot(buf_refs[e], kinds[e], (*chip, 1 - me[2]), shape), frecv_in.at[3 * e + j]).wait_recv()
            for k in range(4):
                arrival(mine, send_in.at[4 * e + k]).wait_send()
            for j in range(3):
                arrival(mine, fsend_in.at[3 * e + j]).wait_send()

    out = pl.pallas_call(
        body, name=name,
        out_shape=(*[pltpu.HBM(a.shape, a.dtype) for a in shards], *[pltpu.HBM(a.shape, a.dtype) for a in bufs]),
        in_specs=[HBM_SPEC] * (2 * n) + [SEM_SPEC] * 5 + [ANY_SPEC],
        out_specs=tuple([HBM_SPEC] * (2 * n)),
        input_output_aliases={i: i for i in range(2 * n)},
        compiler_params=pltpu.CompilerParams(has_side_effects=SPLIT_EFFECT),
    )(*shards, *bufs, send_sems, recv_sems, local_sems, fsend, frecv, after)
    return out[n:]


def _grad_slice(ref, kind, j):
    if kind == "major":
        return ref.at[j]
    if kind == "rows":
        r = ref.shape[0] // N_DEV
        return ref.at[pl.ds(j * r, r), :]
    cu = ref.shape[2] // 4
    return ref.at[j // 4, :, pl.ds((j % 4) * cu, cu)]


def _slice_shape(a, kind):
    if kind == "major":
        return a.shape[1:]
    if kind == "rows":
        return (a.shape[0] // N_DEV, a.shape[1])
    return (a.shape[1], a.shape[2] // 4)


def _scatter_copies(srcs, lands, kinds, send_sems, recv_sems):
    x_, y_, c_ = _my_pos()
    me = 4 * x_ + 2 * y_ + c_
    n = len(srcs)

    def remote(e, j):
        return pltpu.make_async_remote_copy(
            src_ref=_grad_slice(srcs[e], kinds[e], j), dst_ref=lands[e].at[me],
            send_sem=send_sems.at[e * N_DEV + j], recv_sem=recv_sems.at[e * N_DEV + me],
            device_id=(j // 4, (j // 2) % 2, j % 2), device_id_type=MESH)

    def local(e, j):
        return pltpu.make_async_copy(_grad_slice(srcs[e], kinds[e], j), lands[e].at[j], recv_sems.at[e * N_DEV + j])

    def arrival(e, i):
        return pltpu.make_async_remote_copy(
            src_ref=_grad_slice(srcs[e], kinds[e], i), dst_ref=lands[e].at[i],
            send_sem=send_sems.at[e * N_DEV + i], recv_sem=recv_sems.at[e * N_DEV + i],
            device_id=(i // 4, (i // 2) % 2, i % 2), device_id_type=MESH)

    def start():
        for e in range(n):
            for j in range(N_DEV):
                @pl.when(me == j)
                def _():
                    local(e, j).start()

                @pl.when(me != j)
                def _():
                    remote(e, j).start()

    def wait():
        for e in range(n):
            for i in range(N_DEV):
                @pl.when(me == i)
                def _():
                    local(e, i).wait()

                @pl.when(me != i)
                def _():
                    arrival(e, i).wait_recv()
        for e in range(n):
            for j in range(N_DEV):
                @pl.when(me != j)
                def _():
                    remote(e, j).wait_send()

    return start, wait


def _scatter_start(srcs, kinds, name):
    n = len(srcs)
    lands = [lax.empty((N_DEV,) + _slice_shape(a, k), a.dtype) for a, k in zip(srcs, kinds)]

    def body(*refs):
        src_refs, land_refs = refs[:n], refs[n:2 * n]
        send_sems, recv_sems = refs[2 * n], refs[2 * n + 1]
        token = refs[-1]
        start, _ = _scatter_copies(src_refs, land_refs, kinds, send_sems, recv_sems)
        start()
        token[...] = jnp.zeros_like(token)

    out = pl.pallas_call(
        body, name=name,
        out_shape=(pltpu.SemaphoreType.DMA((n * N_DEV,)), pltpu.SemaphoreType.DMA((n * N_DEV,)),
                   *[pltpu.HBM(a.shape, a.dtype) for a in srcs], *[pltpu.HBM(a.shape, a.dtype) for a in lands],
                   jax.ShapeDtypeStruct((8, LANE), F32)),
        in_specs=[HBM_SPEC] * (2 * n),
        out_specs=(SEM_SPEC, SEM_SPEC, *[HBM_SPEC] * (2 * n), pl.BlockSpec(memory_space=pltpu.VMEM)),
        input_output_aliases={i: 2 + i for i in range(2 * n)},
        compiler_params=pltpu.CompilerParams(has_side_effects=SPLIT_EFFECT),
    )(*[_in_hbm(a) for a in srcs], *[_in_hbm(a) for a in lands])
    return out[0], out[1], out[2:2 + n], out[2 + n:2 + 2 * n], out[-1]


def _scatter_wait(send_sems, recv_sems, srcs, lands, kinds, after, name):
    n = len(srcs)

    def body(*refs):
        src_refs, land_refs = refs[:n], refs[n:2 * n]
        _, wait = _scatter_copies(src_refs, land_refs, kinds, refs[2 * n], refs[2 * n + 1])
        wait()

    out = pl.pallas_call(
        body, name=name,
        out_shape=(*[pltpu.HBM(a.shape, a.dtype) for a in srcs], *[pltpu.HBM(a.shape, a.dtype) for a in lands]),
        in_specs=[HBM_SPEC] * (2 * n) + [SEM_SPEC, SEM_SPEC, pl.BlockSpec(memory_space=pl.ANY)],
        out_specs=tuple([HBM_SPEC] * (2 * n)),
        input_output_aliases={i: i for i in range(2 * n)},
        compiler_params=pltpu.CompilerParams(has_side_effects=SPLIT_EFFECT),
    )(*srcs, *lands, send_sems, recv_sems, after)
    return out[n:]


def _rope_tables(positions, sign):
    half = ROPE_DIM // 2
    inv_freq = ROPE_THETA ** (-jnp.arange(0, ROPE_DIM, 2, dtype=F32) / ROPE_DIM)
    ang = positions.astype(F32)[:, None] * inv_freq
    reps = LANE // half
    cos = jnp.tile(jnp.cos(ang), (1, reps))
    sin = jnp.tile(jnp.sin(ang), (1, reps)) * sign
    d = jnp.arange(LANE) % SWA_HEAD_DIM
    return (jnp.where(d < ROPE_DIM, cos, 1.0), jnp.where(d < half, -sin, 0.0),
            jnp.where((d >= half) & (d < ROPE_DIM), sin, 0.0))


def _pad_cols(a, n):
    return jnp.pad(a, ((0, 0), (0, n - a.shape[1])))


def _local_step(x, target, positions, mods, fetch, P, on_grads):
    S, D = x.shape
    Hf = D // FOX_HEAD_DIM
    Hq = D // SWA_HEAD_DIM
    Hk = Hq // SWA_GROUP
    Wk = Hk * SWA_HEAD_DIM
    n_in = 3 * D + Hf
    (sh1a, sc1a, g1a, sh2a, sc2a, g2a), (sh1b, sc1b, g1b, sh2b, sc2b, g2b) = mods
    row = lambda v: v.reshape(1, -1)
    cw = [jnp.transpose(P["conv_w"][l].reshape(3, 2, -1), (1, 0, 2)) for l in range(2)]
    cb = [P["conv_b"][l].reshape(2, 1, -1) for l in range(2)]

    W = dict(up=[None, None], down=[None, None])
    h1a = _modulate(x, sc1a, sh1a, "modulate_in")
    W["fox_in"], W["fox_o"] = fetch("fox", "wait", h1a)
    proj_a = _mm_nn(h1a, W["fox_in"], F32, "fox_in_proj", tn=896)
    flT = proj_a[:, 3 * D:n_in].T
    bf_col = P["fox_b_f"].reshape(Hf, 1)
    cumT = _fox_prep(flT, bf_col, "fox_cumsum")
    cum_col, cum_row = cumT.reshape(Hf, S, 1), cumT.reshape(Hf, 1, S)
    o_a, lse_a = _fox_attn_fwd(proj_a, cum_col, cum_row, Hf, "fox_attn_fwd")
    fetch("ffn0", "forward", o_a)
    y1a = _mm_nn(o_a, W["fox_o"], F32, "fox_out_proj")
    z1a, x1, h2a = _ln_fwd(x, y1a, g1a, row(P["ln_mix_g"][0]), row(P["ln_mix_b"][0]), sc2a, sh2a, "ln_mix0")
    W["up"][0], W["down"][0] = fetch("ffn0", "wait", h2a)
    u_a, a_a = _ffn_up(h2a, W["up"][0], cw[0], cb[0], "ffn_up0")
    fetch("swa", "forward", a_a)
    y2a = _mm_nn(a_a, W["down"][0], F32, "ffn_down0", tk=1408)
    z2a, x2, h1b = _ln_fwd(x1, y2a, g2a, row(P["ln_ffn_g"][0]), row(P["ln_ffn_b"][0]), sc1b, sh1b, "ln_ffn0")

    W["swa_in"], W["swa_o"] = fetch("swa", "wait", h1b)
    proj_b = _mm_nn(h1b, W["swa_in"], F32, "swa_in_proj")
    tabs_f = _rope_tables(positions, 1.0)
    tabs_b = _rope_tables(positions, -1.0)
    qk = _rope([(proj_b[:, :D + Wk], True)], tabs_f, F32, "rope_fwd")
    sinks = _pad_cols(P["swa_sinks"].reshape(1, Hq), LANE)
    o_b, lse_b = _swa_attn_fwd(qk, proj_b, sinks, Hq, "swa_attn_fwd")
    fetch("ffn1", "forward", o_b)
    y1b = _mm_nn(o_b, W["swa_o"], F32, "swa_out_proj")
    z1b, x3, h2b = _ln_fwd(x2, y1b, g1b, row(P["ln_mix_g"][1]), row(P["ln_mix_b"][1]), sc2b, sh2b, "ln_mix1")
    W["up"][1], W["down"][1] = fetch("ffn1", "wait", h2b)
    u_b, a_b = _ffn_up(h2b, W["up"][1], cw[1], cb[1], "ffn_up1")
    y2b = _mm_nn(a_b, W["down"][1], F32, "ffn_down1", tk=1408)
    z2b, dout, loss_row = _ln_fwd_loss(x3, y2b, g2b, row(P["ln_ffn_g"][1]), row(P["ln_ffn_b"][1]), target, "ln_ffn1_loss")

    def ffn_backward(dy, a, u, h_in, l, tag):
        d_down = _mm_tn(a, dy[None], BF16, "ffn_dwdown" + tag, tm=1408, tn=1024)[0]
        token = on_grads("ffn_w_down" + tag, d_down)
        da = _mm_nt(dy[None], W["down"][l][None], F32, "ffn_da" + tag)
        du, dcw, dcb = _ffn_bwd_elem(da, u, cw[l], cb[l], token, "ffn_bwd_elem" + tag)
        d_up = _mm_tn(h_in, du, BF16, "ffn_dwup" + tag)
        token = on_grads("ffn_w_up" + tag, d_up)
        dh = _mm_nt(du, W["up"][l], F32, "ffn_dh" + tag, tk=1408)
        return dh, token, jnp.transpose(dcw, (1, 0, 2)).reshape(3, -1), dcb.reshape(-1)

    dz2b, dy2b, dg_f1, db_f1, dgate2b = _ln_bwd(dout, z2b, y2b, g2b, row(P["ln_ffn_g"][1]), "ln_ffn1_bwd")
    dh2b, token, dcw1, dcb1 = ffn_backward(dy2b, a_b, u_b, h2b, 1, "1")
    dx3, dsc2b, dsh2b = _mod_bwd(dz2b, dh2b, x3, sc2b, token, "mod_ffn1_bwd")

    dz1b, dy1b, dg_m1, db_m1, dgate1b = _ln_bwd(dx3, z1b, y1b, g1b, row(P["ln_mix_g"][1]), "ln_mix1_bwd")
    token = on_grads("swa_w_o", _mm_tn(o_b, dy1b[None], BF16, "swa_dwo")[0])
    do_b = _mm_nt(dy1b[None], W["swa_o"][None], F32, "swa_do")
    dq_b, dk_b, dv_b, dsinks = _swa_attn_bwd(qk, proj_b, sinks, do_b, lse_b, Hq, token, "swa_attn_bwd")
    dproj_b = _rope([(dq_b, True), (dk_b, True), (dv_b, False)], tabs_b, BF16, "rope_bwd")
    token = on_grads("swa_w_in", _mm_tn(h1b, dproj_b[None], BF16, "swa_dwin")[0])
    dh1b = _mm_nt(dproj_b[None], W["swa_in"][None], F32, "swa_dh", tk=1280)
    dx2, dsc1b, dsh1b = _mod_bwd(dz1b, dh1b, x2, sc1b, token, "mod_mix1_bwd")

    dz2a, dy2a, dg_f0, db_f0, dgate2a = _ln_bwd(dx2, z2a, y2a, g2a, row(P["ln_ffn_g"][0]), "ln_ffn0_bwd")
    dh2a, token, dcw0, dcb0 = ffn_backward(dy2a, a_a, u_a, h2a, 0, "0")
    dx1, dsc2a, dsh2a = _mod_bwd(dz2a, dh2a, x1, sc2a, token, "mod_ffn0_bwd")

    dz1a, dy1a, dg_m0, db_m0, dgate1a = _ln_bwd(dx1, z1a, y1a, g1a, row(P["ln_mix_g"][0]), "ln_mix0_bwd")
    token = on_grads("fox_w_o", _mm_tn(o_a, dy1a[None], BF16, "fox_dwo")[0])
    do_a = _mm_nt(dy1a[None], W["fox_o"][None], F32, "fox_do")
    dq_a, dk_a, dv_a, dcum_row = _fox_attn_bwd(proj_a, do_a, cum_col, cum_row, lse_a, Hf, token, "fox_attn_bwd")
    dflT, dbf = _fox_prep_bwd(dcum_row.reshape(Hf, S), flT, bf_col, "fox_cumsum_bwd")
    n_pad = W["fox_in"].shape[1]
    dproj_a = jnp.concatenate([dq_a, dk_a.astype(BF16), dv_a.astype(BF16),
                               _pad_cols(dflT.T, n_pad - 3 * D).astype(BF16)], axis=1)
    token = on_grads("fox_w_in", _mm_tn(h1a, dproj_a[None], BF16, "fox_dwin", tn=896)[0])
    dh1a = _mm_nt(dproj_a[None], W["fox_in"][None], F32, "fox_dh", tk=896)
    grad_x, dsc1a, dsh1a = _mod_bwd(dz1a, dh1a, x, sc1a, token, "mod_mix0_bwd")

    dmod = jnp.stack([jnp.concatenate([dsh1a, dsc1a, dgate1a, dsh2a, dsc2a, dgate2a], axis=1)[0],
                      jnp.concatenate([dsh1b, dsc1b, dgate1b, dsh2b, dsc2b, dgate2b], axis=1)[0]])
    small = dict(dmod=dmod, conv_b=jnp.stack([dcb0, dcb1]), conv_w=jnp.stack([dcw0, dcw1]),
                 ln_mix_g=jnp.concatenate([dg_m0, dg_m1]), ln_mix_b=jnp.concatenate([db_m0, db_m1]),
                 ln_ffn_g=jnp.concatenate([dg_f0, dg_f1]), ln_ffn_b=jnp.concatenate([db_f0, db_f1]),
                 fox_b_f=dbf.reshape(-1), swa_sinks=dsinks[0, :Hq])
    return loss_row[0, 0], grad_x, small


SMALL_ORDER = ("dmod", "conv_b", "conv_w", "ln_mix_g", "ln_mix_b", "ln_ffn_g", "ln_ffn_b", "fox_b_f", "swa_sinks", "loss")


def _pack_rows(arrays):
    chunks, spans, off = [], [], 0
    for a in arrays:
        flat = a.reshape(-1)
        n = -(-flat.shape[0] // LANE) * LANE
        chunks.append(jnp.pad(flat, (0, n - flat.shape[0])))
        spans.append((off, flat.shape[0], a.shape))
        off += n
    total = -(-off // (8 * LANE)) * (8 * LANE)
    chunks.append(jnp.zeros((total - off,), F32))
    return jnp.concatenate(chunks).reshape(-1, LANE), spans


def _unpack_rows(packed, spans):
    flat = packed.reshape(-1)
    return [flat[off:off + n].reshape(shape) for off, n, shape in spans]


def kernel(x, c, positions, fox_w_in, fox_b_f, fox_w_o, swa_w_in, swa_sinks, swa_w_o, ada_w, ada_b, ffn_w_up, ffn_conv_w, ffn_conv_b, ffn_w_down, ln_mix_g, ln_mix_b, ln_ffn_g, ln_ffn_b, loss_target, m_fox_w_in, m_fox_b_f, m_fox_w_o, m_swa_w_in, m_swa_sinks, m_swa_w_o, m_ada_w, m_ada_b, m_ffn_w_up, m_ffn_conv_w, m_ffn_conv_b, m_ffn_w_down, m_ln_mix_g, m_ln_mix_b, m_ln_ffn_g, m_ln_ffn_b, v_fox_w_in, v_fox_b_f, v_fox_w_o, v_swa_w_in, v_swa_sinks, v_swa_w_o, v_ada_w, v_ada_b, v_ffn_w_up, v_ffn_conv_w, v_ffn_conv_b, v_ffn_w_down, v_ln_mix_g, v_ln_mix_b, v_ln_ffn_g, v_ln_ffn_b):
    S, D = x.shape[1], x.shape[2]
    L = ada_w.shape[0]
    me = 4 * lax.axis_index("x") + 2 * lax.axis_index("y") + lax.axis_index("c")
    n_ada = ada_w.shape[2]
    cu = ffn_w_up.shape[2]
    F = 4 * cu
    n_in = fox_w_in.shape[2] * N_DEV
    n_in_pad = -(-n_in // LANE) * LANE

    c_all = _all_gather_small(c.reshape(-1, LANE), "gather_c").reshape(N_DEV, D)
    b_cols = lax.dynamic_slice_in_dim(ada_b, me * n_ada, n_ada, axis=1).reshape(L, 1, n_ada)
    mod_blk = _ada_mod(jnp.pad(c_all, ((0, ADA_ROWS - N_DEV), (0, 0))), ada_w, b_cols, "ada_mod")[:, :N_DEV]
    mod_all = _all_gather_small(mod_blk.reshape(-1, LANE), "gather_mod").reshape(N_DEV, L, N_DEV, n_ada)
    mod_mine = lax.dynamic_index_in_dim(mod_all, me, axis=2, keepdims=False)
    mod_mine = jnp.transpose(mod_mine, (1, 0, 2)).reshape(L, N_DEV * n_ada)
    mods = [[mod_mine[l, k * D:(k + 1) * D].reshape(1, D) for k in range(6)] for l in range(L)]

    gather_groups = dict(
        fox=([fox_w_in[0].astype(BF16), fox_w_o[0].astype(BF16)], ["major", "rows"]),
        ffn0=([ffn_w_up[0].astype(BF16), ffn_w_down[0].astype(BF16)], ["halves", "rows"]),
        swa=([swa_w_in[0].astype(BF16), swa_w_o[0].astype(BF16)], ["major", "rows"]),
        ffn1=([ffn_w_up[1].astype(BF16), ffn_w_down[1].astype(BF16)], ["halves", "rows"]))
    gather_order = list(gather_groups)
    gathers = {}

    def start_group(group, after):
        shards, kinds = gather_groups[group]
        send, recv, local, thru, bufs, token = _gather_start(shards, kinds, after, "gather_start_" + group)
        gathers[group] = dict(send=send, recv=recv, local=local, shards=thru, bufs=bufs, kinds=kinds,
                              shapes=[a.shape for a in shards], token=token)

    start_group(gather_order[0], None)

    def natural(g, pad_to=None):
        w = jnp.transpose(g, (1, 0, 2)).reshape(D, -1)
        return w if pad_to is None else _pad_cols(w, pad_to)

    def forward_stage(group, after):
        s = gathers[group]
        s["fsend"], s["frecv"], s["bufs"], forwarded = _gather_forward(s["recv"], s["bufs"], s["kinds"], s["shapes"], after,
                                                                       "gather_forward_" + group)
        nxt = gather_order.index(group) + 1
        if nxt < len(gather_order):
            start_group(gather_order[nxt], forwarded)

    def fetch(group, stage, after):
        if stage == "forward":
            return forward_stage(group, after)
        if group == "fox":
            forward_stage(group, gathers[group]["token"])
        s = gathers.pop(group)
        first, second = _gather_wait(s["send"], s["recv"], s["local"], s["fsend"], s["frecv"], s["shards"], s["bufs"],
                                     s["kinds"], after, "gather_wait_" + group)
        if group == "fox":
            return natural(first, n_in_pad), second
        if group == "swa":
            return natural(first), second
        return first, second

    conv_w_all = None
    P = dict(fox_b_f=fox_b_f[0], swa_sinks=swa_sinks[0], conv_b=ffn_conv_b,
             ln_mix_g=ln_mix_g, ln_mix_b=ln_mix_b, ln_ffn_g=ln_ffn_g, ln_ffn_b=ln_ffn_b)
    cw_rows = _all_gather_small(_pack_rows([ffn_conv_w])[0], "gather_conv_w")
    n_cw = ffn_conv_w.size
    cw_dev = cw_rows.reshape(N_DEV, -1)[:, :n_cw].reshape(N_DEV, L, 3, cu)
    conv_w_all = jnp.transpose(cw_dev, (1, 2, 0, 3)).reshape(L, 3, N_DEV * cu)
    P["conv_w"] = conv_w_all

    out, pending = {}, {}

    def shard_major(g):
        return jnp.transpose(g.reshape(D, N_DEV, -1), (1, 0, 2))

    big = dict(
        ffn_w_down1=("ffn_w_down", "rows", 1, (ffn_w_down, m_ffn_w_down, v_ffn_w_down)),
        ffn_w_up1=("ffn_w_up", "halves", 1, (ffn_w_up, m_ffn_w_up, v_ffn_w_up)),
        swa_w_o=("swa_w_o", "rows", 0, (swa_w_o, m_swa_w_o, v_swa_w_o)),
        swa_w_in=("swa_w_in", "major", 0, (swa_w_in, m_swa_w_in, v_swa_w_in)),
        ffn_w_down0=("ffn_w_down", "rows", 0, (ffn_w_down, m_ffn_w_down, v_ffn_w_down)),
        ffn_w_up0=("ffn_w_up", "halves", 0, (ffn_w_up, m_ffn_w_up, v_ffn_w_up)),
        fox_w_o=("fox_w_o", "rows", 0, (fox_w_o, m_fox_w_o, v_fox_w_o)),
        fox_w_in=("fox_w_in", "major", 0, (fox_w_in, m_fox_w_in, v_fox_w_in)))
    finish_at = dict(swa_w_o=["ffn_w_down1"], ffn_w_up0=["ffn_w_up1", "swa_w_o", "swa_w_in"],
                     fox_w_o=["ffn_w_down0"], fox_w_in=["ffn_w_up0"])

    def finish(name, after):
        send, recv, thru, lands = pending.pop(name)
        param, kind, layer, wmv = big[name]
        landed, = _scatter_wait(send, recv, thru, lands, [kind], after, "scatter_wait_" + name)
        out[param] = _adam_sum(landed, *wmv, layer, out.get(param), "adam_" + name)

    def on_grads(name, g):
        kind = big[name][1]
        if name == "fox_w_in":
            g = g[:, :n_in]
        src = shard_major(g) if kind == "major" else g
        send, recv, thru, lands, token = _scatter_start([src], [kind], "scatter_start_" + name)
        pending[name] = (send, recv, thru, lands)
        for done in finish_at.get(name, ()):
            finish(done, token)
        return token

    loss_local, grad_x, small = _local_step(x[0], loss_target[0], positions[0], mods, fetch, P, on_grads)
    small["loss"] = loss_local.reshape(1)
    finish("fox_w_o", grad_x)
    finish("fox_w_in", grad_x)

    packed, spans = _pack_rows([small[k] for k in SMALL_ORDER])
    gathered = _all_gather_small(packed, "gather_small_grads")
    totals = dict(zip(SMALL_ORDER, _unpack_rows(_sum_slots(gathered, "sum_small_grads"), spans)))
    loss = totals["loss"].reshape(())
    n_mod = L * 6 * D
    dmod_all = gathered.reshape(N_DEV, -1)[:, :n_mod].reshape(N_DEV, L, 6 * D)
    dmod_cols = jnp.transpose(lax.dynamic_slice_in_dim(dmod_all, me * n_ada, n_ada, axis=2), (1, 0, 2))
    g_ada_w = _ada_bwd(c_all.T, dmod_cols, "ada_w_grad")
    out["ada_w"] = (g_ada_w,) + tuple(_adam(g_ada_w, ada_w, m_ada_w, v_ada_w, "adam_ada_w"))

    g_small = dict(fox_b_f=totals["fox_b_f"].reshape(fox_b_f.shape), swa_sinks=totals["swa_sinks"].reshape(swa_sinks.shape),
                   ada_b=totals["dmod"].reshape(ada_b.shape), ffn_conv_b=totals["conv_b"].reshape(ffn_conv_b.shape),
                   ffn_conv_w=lax.dynamic_slice_in_dim(totals["conv_w"].reshape(L, 3, 2 * F), me * cu, cu, axis=2),
                   ln_mix_g=totals["ln_mix_g"], ln_mix_b=totals["ln_mix_b"],
                   ln_ffn_g=totals["ln_ffn_g"], ln_ffn_b=totals["ln_ffn_b"])
    small_names = ("fox_b_f", "swa_sinks", "ada_b", "ffn_conv_b", "ffn_conv_w", "ln_mix_g", "ln_mix_b", "ln_ffn_g", "ln_ffn_b")
    w_small = dict(fox_b_f=(fox_b_f, m_fox_b_f, v_fox_b_f), swa_sinks=(swa_sinks, m_swa_sinks, v_swa_sinks),
                   ada_b=(ada_b, m_ada_b, v_ada_b), ffn_conv_b=(ffn_conv_b, m_ffn_conv_b, v_ffn_conv_b),
                   ffn_conv_w=(ffn_conv_w, m_ffn_conv_w, v_ffn_conv_w),
                   ln_mix_g=(ln_mix_g, m_ln_mix_g, v_ln_mix_g), ln_mix_b=(ln_mix_b, m_ln_mix_b, v_ln_mix_b),
                   ln_ffn_g=(ln_ffn_g, m_ln_ffn_g, v_ln_ffn_g), ln_ffn_b=(ln_ffn_b, m_ln_ffn_b, v_ln_ffn_b))
    pk_g, sp = _pack_rows([g_small[k] for k in small_names])
    pk_w = _pack_rows([w_small[k][0] for k in small_names])[0]
    pk_m = _pack_rows([w_small[k][1] for k in small_names])[0]
    pk_v = _pack_rows([w_small[k][2] for k in small_names])[0]
    res = _adam(pk_g[None], pk_w[None], pk_m[None], pk_v[None], "adam_small")
    res = [dict(zip(small_names, _unpack_rows(r[0], sp))) for r in res]
    for k in small_names:
        out[k] = (g_small[k], res[0][k], res[1][k], res[2][k])

    order = ("fox_w_in", "fox_b_f", "fox_w_o", "swa_w_in", "swa_sinks", "swa_w_o", "ada_w", "ada_b", "ffn_w_up",
             "ffn_conv_w", "ffn_conv_b", "ffn_w_down", "ln_mix_g", "ln_mix_b", "ln_ffn_g", "ln_ffn_b")
    return (loss, grad_x[None], *[out[k][0] for k in order], *[out[k][1] for k in order],
            *[out[k][2] for k in order], *[out[k][3] for k in order])
```

```python
import functools

import jax
import jax.numpy as jnp
from jax import lax
from jax.experimental import pallas as pl
from jax.experimental.pallas import tpu as pltpu

F32 = jnp.float32
BF16 = jnp.bfloat16
MESH = pl.DeviceIdType.MESH
N_DEV = 8
AXES = ("x", "y", "c")

DEPTH = 2
ALPHA = (2.0 * DEPTH) ** 0.25
LN_EPS = 1e-5
FOX_HEAD_DIM = 128
SWA_HEAD_DIM = 64
SWA_GROUP = 8
SWA_WINDOW = 128
Q_BLOCK = 128
ROPE_DIM = 16
ROPE_THETA = 500000.0

ADAM_LR = 0.001
ADAM_B1 = 0.9
ADAM_B2 = 0.999
ADAM_EPS = 1e-08
ADAM_WD = 0.01
ADAM_STEP = 10

LANE = 128
MIB = 1024 * 1024


def _tile(n, pref, unit=LANE):
    if n <= pref:
        return n
    t = (pref // unit) * unit
    while t >= unit:
        if n % t == 0:
            return t
        t -= unit
    return n


def _params(sem, vmem_mib=48):
    return pltpu.CompilerParams(dimension_semantics=sem, vmem_limit_bytes=vmem_mib * MIB)


def _sigmoid(x):
    return 1.0 / (1.0 + jnp.exp(-x))


def _mm_call(dot, grid_mnk, in_specs, out_spec, out_shape, k_axis, nk, tm, tn, name, operands):
    sem = ("parallel",) * (len(grid_mnk) - 1) + ("arbitrary",)

    if nk == 1:
        def body(a_ref, b_ref, o_ref):
            o_ref[...] = dot(a_ref[...], b_ref[...]).astype(o_ref.dtype)
        scratch = []
    else:
        def body(a_ref, b_ref, o_ref, acc_ref):
            k = pl.program_id(k_axis)

            @pl.when(k == 0)
            def _():
                acc_ref[...] = jnp.zeros_like(acc_ref)

            acc_ref[...] += dot(a_ref[...], b_ref[...])

            @pl.when(k == nk - 1)
            def _():
                o_ref[...] = acc_ref[...].astype(o_ref.dtype)
        scratch = [pltpu.VMEM((tm, tn), F32)]

    return pl.pallas_call(
        body, name=name, grid=grid_mnk, in_specs=in_specs, out_specs=out_spec, out_shape=out_shape,
        scratch_shapes=scratch, compiler_params=_params(sem, 56),
    )(*operands)


def _dot(dims):
    def dot(a, b):
        return lax.dot_general(a.astype(BF16), b.astype(BF16), (dims, ((), ())), preferred_element_type=F32)
    return dot


def _mm_nn(a, b, out_dtype, name, tm=2048, tn=512, tk=2048):
    M, K = a.shape
    N = b.shape[1]
    tm, tn, tk = _tile(M, tm), _tile(N, tn), _tile(K, tk)
    nk = K // tk
    return _mm_call(
        _dot(((1,), (0,))), (M // tm, N // tn, nk),
        [pl.BlockSpec((tm, tk), lambda i, j, k: (i, k)), pl.BlockSpec((tk, tn), lambda i, j, k: (k, j))],
        pl.BlockSpec((tm, tn), lambda i, j, k: (i, j)), jax.ShapeDtypeStruct((M, N), out_dtype),
        2, nk, tm, tn, name, (a, b))


def _mm_nt(a, b, out_dtype, name, tm=2048, tn=512, tk=2048):
    P, M, K = a.shape
    N = b.shape[1]
    tm, tn, tk = _tile(M, tm), _tile(N, tn), _tile(K, tk)
    nk = K // tk
    return _mm_call(
        _dot(((1,), (1,))), (M // tm, N // tn, P * nk),
        [pl.BlockSpec((None, tm, tk), lambda i, j, k: (k // nk, i, k % nk)),
         pl.BlockSpec((None, tn, tk), lambda i, j, k: (k // nk, j, k % nk))],
        pl.BlockSpec((tm, tn), lambda i, j, k: (i, j)), jax.ShapeDtypeStruct((M, N), out_dtype),
        2, P * nk, tm, tn, name, (a, b))


def _mm_tn(a, b, out_dtype, name, tm=2048, tn=512, tk=2048):
    K, M = a.shape
    P, _, N = b.shape
    tm, tn, tk = _tile(M, tm), _tile(N, tn), _tile(K, tk)
    nk = K // tk
    return _mm_call(
        _dot(((0,), (0,))), (P, M // tm, N // tn, nk),
        [pl.BlockSpec((tk, tm), lambda p, i, j, k: (k, i)), pl.BlockSpec((None, tk, tn), lambda p, i, j, k: (p, k, j))],
        pl.BlockSpec((None, tm, tn), lambda p, i, j, k: (p, i, j)), jax.ShapeDtypeStruct((P, M, N), out_dtype),
        3, nk, tm, tn, name, (a, b))


ROW_TILE = 256


def _row_spec(tm, D):
    return pl.BlockSpec((tm, D), lambda i: (i, 0))


def _vec_spec(D):
    return pl.BlockSpec((1, D), lambda i: (0, 0))


def _modulate(x, sc, sh, name):
    S, D = x.shape
    tm = _tile(S, ROW_TILE, 8)

    def body(x_ref, sc_ref, sh_ref, h_ref):
        h_ref[...] = (x_ref[...] * (1.0 + sc_ref[...]) + sh_ref[...]).astype(BF16)

    return pl.pallas_call(
        body, name=name, grid=(S // tm,),
        in_specs=[_row_spec(tm, D), _vec_spec(D), _vec_spec(D)],
        out_specs=_row_spec(tm, D),
        out_shape=jax.ShapeDtypeStruct((S, D), BF16),
        compiler_params=_params(("parallel",)),
    )(x, sc, sh)


def _layer_norm_rows(z, gamma, beta):
    mu = jnp.mean(z, axis=-1, keepdims=True)
    zc = z - mu
    var = jnp.mean(zc * zc, axis=-1, keepdims=True)
    return zc * lax.rsqrt(var + LN_EPS) * gamma + beta


def _ln_fwd(x, y, gate, gamma, beta, sc_n, sh_n, after, name):
    S, D = x.shape
    tm = _tile(S, ROW_TILE, 8)

    def body(x_ref, y_ref, gate_ref, g_ref, b_ref, sc_ref, sh_ref, after_ref, z_ref, xo_ref, hn_ref):
        z = ALPHA * x_ref[...] + (1.0 + gate_ref[...]) * y_ref[...]
        xo = _layer_norm_rows(z, g_ref[...], b_ref[...])
        z_ref[...] = z
        xo_ref[...] = xo
        hn_ref[...] = (xo * (1.0 + sc_ref[...]) + sh_ref[...]).astype(BF16)

    return pl.pallas_call(
        body, name=name, grid=(S // tm,),
        in_specs=[_row_spec(tm, D), _row_spec(tm, D)] + [_vec_spec(D)] * 5 + [pl.BlockSpec(memory_space=pl.ANY)],
        out_specs=[_row_spec(tm, D)] * 3,
        out_shape=[jax.ShapeDtypeStruct((S, D), F32), jax.ShapeDtypeStruct((S, D), F32),
                   jax.ShapeDtypeStruct((S, D), BF16)],
        compiler_params=_params(("parallel",)),
    )(x, y, gate, gamma, beta, sc_n, sh_n, after)


def _ln_fwd_loss(x, y, gate, gamma, beta, target, name):
    S, D = x.shape
    tm = _tile(S, ROW_TILE, 8)

    def body(x_ref, y_ref, gate_ref, g_ref, b_ref, t_ref, z_ref, dout_ref, loss_ref):
        @pl.when(pl.program_id(0) == 0)
        def _():
            loss_ref[...] = jnp.zeros_like(loss_ref)

        z = ALPHA * x_ref[...] + (1.0 + gate_ref[...]) * y_ref[...]
        xo = _layer_norm_rows(z, g_ref[...], b_ref[...])
        err = xo - t_ref[...]
        z_ref[...] = z
        dout_ref[...] = err * (1.0 / D)
        loss_ref[...] += (0.5 / D) * jnp.sum(err * err)

    return pl.pallas_call(
        body, name=name, grid=(S // tm,),
        in_specs=[_row_spec(tm, D), _row_spec(tm, D)] + [_vec_spec(D)] * 3 + [_row_spec(tm, D)],
        out_specs=[_row_spec(tm, D), _row_spec(tm, D), pl.BlockSpec((1, LANE), lambda i: (0, 0))],
        out_shape=[jax.ShapeDtypeStruct((S, D), F32), jax.ShapeDtypeStruct((S, D), F32),
                   jax.ShapeDtypeStruct((1, LANE), F32)],
        compiler_params=_params(("arbitrary",)),
    )(x, y, gate, gamma, beta, target)


def _ln_bwd(dout, z, y, gate, gamma, name):
    S, D = z.shape
    tm = _tile(S, ROW_TILE, 8)

    def body(dout_ref, z_ref, y_ref, gate_ref, g_ref, dz_ref, dy_ref, dg_ref, db_ref, dgate_ref):
        @pl.when(pl.program_id(0) == 0)
        def _():
            dg_ref[...] = jnp.zeros_like(dg_ref)
            db_ref[...] = jnp.zeros_like(db_ref)
            dgate_ref[...] = jnp.zeros_like(dgate_ref)

        z = z_ref[...]
        dout = dout_ref[...]
        mu = jnp.mean(z, axis=-1, keepdims=True)
        zc = z - mu
        var = jnp.mean(zc * zc, axis=-1, keepdims=True)
        rstd = lax.rsqrt(var + LN_EPS)
        xhat = zc * rstd
        dxhat = dout * g_ref[...]
        m1 = jnp.mean(dxhat, axis=-1, keepdims=True)
        m2 = jnp.mean(dxhat * xhat, axis=-1, keepdims=True)
        dz = rstd * (dxhat - m1 - xhat * m2)
        dz_ref[...] = dz
        dy_ref[...] = (dz * (1.0 + gate_ref[...])).astype(BF16)
        dg_ref[...] += jnp.sum(dout * xhat, axis=0, keepdims=True)
        db_ref[...] += jnp.sum(dout, axis=0, keepdims=True)
        dgate_ref[...] += jnp.sum(dz * y_ref[...], axis=0, keepdims=True)

    return pl.pallas_call(
        body, name=name, grid=(S // tm,),
        in_specs=[_row_spec(tm, D)] * 3 + [_vec_spec(D)] * 2,
        out_specs=[_row_spec(tm, D), _row_spec(tm, D)] + [_vec_spec(D)] * 3,
        out_shape=[jax.ShapeDtypeStruct((S, D), F32), jax.ShapeDtypeStruct((S, D), BF16)]
        + [jax.ShapeDtypeStruct((1, D), F32)] * 3,
        compiler_params=_params(("arbitrary",)),
    )(dout, z, y, gate, gamma)


def _mod_bwd(dz, dh, xin, sc, after, name):
    S, D = dz.shape
    tm = _tile(S, ROW_TILE, 8)

    def body(dz_ref, dh_ref, x_ref, sc_ref, after_ref, dx_ref, dsc_ref, dsh_ref):
        @pl.when(pl.program_id(0) == 0)
        def _():
            dsc_ref[...] = jnp.zeros_like(dsc_ref)
            dsh_ref[...] = jnp.zeros_like(dsh_ref)

        dh = dh_ref[...]
        dx_ref[...] = ALPHA * dz_ref[...] + dh * (1.0 + sc_ref[...])
        dsc_ref[...] += jnp.sum(dh * x_ref[...], axis=0, keepdims=True)
        dsh_ref[...] += jnp.sum(dh, axis=0, keepdims=True)

    return pl.pallas_call(
        body, name=name, grid=(S // tm,),
        in_specs=[_row_spec(tm, D)] * 3 + [_vec_spec(D), pl.BlockSpec(memory_space=pl.ANY)],
        out_specs=[_row_spec(tm, D), _vec_spec(D), _vec_spec(D)],
        out_shape=[jax.ShapeDtypeStruct((S, D), F32)] + [jax.ShapeDtypeStruct((1, D), F32)] * 2,
        compiler_params=_params(("arbitrary",)),
    )(dz, dh, xin, sc, after)


def _shift_down(u, k, row):
    return jnp.where(row >= k, pltpu.roll(u, k, axis=0), 0.0)


def _shift_up(u, k, row, S):
    return jnp.where(row < S - k, pltpu.roll(u, S - k, axis=0), 0.0)


def _ffn_up(h, w, cw, cb, name):
    S, D = h.shape
    F = w.shape[2]
    tn = _tile(F, 256)

    def body(h_ref, w_ref, cw_ref, cb_ref, u_ref, a_ref):
        hh = h_ref[...]
        row = lax.broadcasted_iota(jnp.int32, (S, tn), 0)
        conv = []
        for p in range(2):
            u = jnp.dot(hh, w_ref[p], preferred_element_type=F32)
            u_ref[p] = u
            cwp = cw_ref[p]
            conv.append(_shift_down(u, 2, row) * cwp[0:1] + _shift_down(u, 1, row) * cwp[1:2]
                        + u * cwp[2:3] + cb_ref[p])
        g, v = conv
        a_ref[...] = (g * _sigmoid(g) * v).astype(BF16)

    return pl.pallas_call(
        body, name=name, grid=(F // tn,),
        in_specs=[pl.BlockSpec((S, D), lambda j: (0, 0)), pl.BlockSpec((2, D, tn), lambda j: (0, 0, j)),
                  pl.BlockSpec((2, 3, tn), lambda j: (0, 0, j)), pl.BlockSpec((2, 1, tn), lambda j: (0, 0, j))],
        out_specs=[pl.BlockSpec((2, S, tn), lambda j: (0, 0, j)), pl.BlockSpec((S, tn), lambda j: (0, j))],
        out_shape=[jax.ShapeDtypeStruct((2, S, F), F32), jax.ShapeDtypeStruct((S, F), BF16)],
        compiler_params=_params(("parallel",), 56),
    )(h, w, cw, cb)


def _ffn_bwd_elem(da, u, cw, cb, after, name):
    _, S, F = u.shape
    tn = _tile(F, 256)

    def body(da_ref, u_ref, cw_ref, cb_ref, after_ref, du_ref, dcw_ref, dcb_ref):
        row = lax.broadcasted_iota(jnp.int32, (S, tn), 0)
        da = da_ref[...]
        shifted, conv = [], []
        for p in range(2):
            u = u_ref[p]
            u1, u2 = _shift_down(u, 1, row), _shift_down(u, 2, row)
            cwp = cw_ref[p]
            shifted.append((u2, u1, u))
            conv.append(u2 * cwp[0:1] + u1 * cwp[1:2] + u * cwp[2:3] + cb_ref[p])
        g, v = conv
        sg = _sigmoid(g)
        d_conv = (da * v * (sg * (1.0 + g * (1.0 - sg))), da * (g * sg))
        for p in range(2):
            d = d_conv[p]
            cwp = cw_ref[p]
            dcb_ref[p] = jnp.sum(d, axis=0, keepdims=True)
            for j in range(3):
                dcw_ref[p, j:j + 1, :] = jnp.sum(d * shifted[p][j], axis=0, keepdims=True)
            du = d * cwp[2:3] + _shift_up(d, 1, row, S) * cwp[1:2] + _shift_up(d, 2, row, S) * cwp[0:1]
            du_ref[p] = du.astype(BF16)

    return pl.pallas_call(
        body, name=name, grid=(F // tn,),
        in_specs=[pl.BlockSpec((S, tn), lambda j: (0, j)), pl.BlockSpec((2, S, tn), lambda j: (0, 0, j)),
                  pl.BlockSpec((2, 3, tn), lambda j: (0, 0, j)), pl.BlockSpec((2, 1, tn), lambda j: (0, 0, j)),
                  pl.BlockSpec(memory_space=pl.ANY)],
        out_specs=[pl.BlockSpec((2, S, tn), lambda j: (0, 0, j)), pl.BlockSpec((2, 3, tn), lambda j: (0, 0, j)),
                   pl.BlockSpec((2, 1, tn), lambda j: (0, 0, j))],
        out_shape=[jax.ShapeDtypeStruct((2, S, F), BF16), jax.ShapeDtypeStruct((2, 3, F), F32),
                   jax.ShapeDtypeStruct((2, 1, F), F32)],
        compiler_params=_params(("parallel",), 56),
    )(da, u, cw, cb, after)


def _split3(x):
    hi = x.astype(BF16)
    r1 = x - hi.astype(F32)
    mid = r1.astype(BF16)
    lo = (r1 - mid.astype(F32)).astype(BF16)
    return hi, mid, lo


def _tri_matmul(x, upper, S):
    tc = _tile(S, 512)
    parts = _split3(x)
    outs = []
    for b in range(S // tc):
        r = lax.broadcasted_iota(jnp.int32, (S, tc), 0)
        c = lax.broadcasted_iota(jnp.int32, (S, tc), 1) + b * tc
        tri = jnp.where((r <= c) if upper else (r >= c), 1.0, 0.0).astype(BF16)
        acc = jnp.dot(parts[0], tri, preferred_element_type=F32)
        acc += jnp.dot(parts[1], tri, preferred_element_type=F32)
        acc += jnp.dot(parts[2], tri, preferred_element_type=F32)
        outs.append(acc)
    return outs, tc


def _fox_prep(flT, bf, name):
    H, S = flT.shape

    def body(fl_ref, b_ref, cum_ref):
        zz = fl_ref[...] + b_ref[...]
        lf = jnp.minimum(zz, 0.0) - jnp.log(1.0 + jnp.exp(-jnp.abs(zz)))
        outs, tc = _tri_matmul(lf, True, S)
        for b, o in enumerate(outs):
            cum_ref[:, b * tc:(b + 1) * tc] = o

    return pl.pallas_call(
        body, name=name,
        in_specs=[pl.BlockSpec(memory_space=pltpu.VMEM)] * 2,
        out_specs=pl.BlockSpec(memory_space=pltpu.VMEM),
        out_shape=jax.ShapeDtypeStruct((H, S), F32),
        compiler_params=pltpu.CompilerParams(vmem_limit_bytes=48 * MIB),
    )(flT, bf)


def _fox_prep_bwd(dcumT, flT, bf, name):
    H, S = flT.shape

    def body(dc_ref, fl_ref, b_ref, dfl_ref, dbf_ref):
        zz = fl_ref[...] + b_ref[...]
        outs, tc = _tri_matmul(dc_ref[...], False, S)
        total = jnp.zeros((H, 1), F32)
        for b, o in enumerate(outs):
            dfl = o * _sigmoid(-zz[:, b * tc:(b + 1) * tc])
            dfl_ref[:, b * tc:(b + 1) * tc] = dfl
            total += jnp.sum(dfl, axis=1, keepdims=True)
        dbf_ref[...] = total

    return pl.pallas_call(
        body, name=name,
        in_specs=[pl.BlockSpec(memory_space=pltpu.VMEM)] * 3,
        out_specs=[pl.BlockSpec(memory_space=pltpu.VMEM)] * 2,
        out_shape=[jax.ShapeDtypeStruct((H, S), F32), jax.ShapeDtypeStruct((H, 1), F32)],
        compiler_params=pltpu.CompilerParams(vmem_limit_bytes=48 * MIB),
    )(dcumT, flT, bf)


FOX_TQ = 256


def _fox_scores(q_ref, k_ref, cq_ref, ck_ref, i, tq, S, scale):
    q = q_ref[...].astype(BF16)
    k = k_ref[...].astype(BF16)
    s = lax.dot_general(q, k, (((1,), (1,)), ((), ())), preferred_element_type=F32) * scale
    s = s + cq_ref[...] - ck_ref[...]
    qpos = i * tq + lax.broadcasted_iota(jnp.int32, (tq, S), 0)
    kpos = lax.broadcasted_iota(jnp.int32, (tq, S), 1)
    return jnp.where(kpos <= qpos, s, -jnp.inf), q, k


def _fox_attn_fwd(proj, cum_col, cum_row, H, name):
    S = proj.shape[0]
    dh = FOX_HEAD_DIM
    tq = _tile(S, FOX_TQ)
    scale = dh ** -0.5

    def body(q_ref, k_ref, v_ref, cq_ref, ck_ref, o_ref, lse_ref):
        s, _, _ = _fox_scores(q_ref, k_ref, cq_ref, ck_ref, pl.program_id(1), tq, S, scale)
        m = jnp.max(s, axis=-1, keepdims=True)
        p = jnp.exp(s - m)
        l = jnp.sum(p, axis=-1, keepdims=True)
        o = jnp.dot(p.astype(BF16), v_ref[...].astype(BF16), preferred_element_type=F32) / l
        o_ref[...] = o.astype(BF16)
        lse_ref[...] = m + jnp.log(l)

    return pl.pallas_call(
        body, name=name, grid=(H, S // tq),
        in_specs=[pl.BlockSpec((tq, dh), lambda h, i: (i, h)),
                  pl.BlockSpec((S, dh), lambda h, i: (0, H + h)),
                  pl.BlockSpec((S, dh), lambda h, i: (0, 2 * H + h)),
                  pl.BlockSpec((None, tq, 1), lambda h, i: (h, i, 0)),
                  pl.BlockSpec((None, 1, S), lambda h, i: (h, 0, 0))],
        out_specs=[pl.BlockSpec((tq, dh), lambda h, i: (i, h)),
                   pl.BlockSpec((None, tq, 1), lambda h, i: (h, i, 0))],
        out_shape=[jax.ShapeDtypeStruct((S, H * dh), BF16), jax.ShapeDtypeStruct((H, S, 1), F32)],
        compiler_params=_params(("parallel", "parallel")),
    )(proj, proj, proj, cum_col, cum_row)


def _fox_attn_bwd(proj, do, cum_col, cum_row, lse, H, after, name):
    S = proj.shape[0]
    dh = FOX_HEAD_DIM
    tq = _tile(S, FOX_TQ)
    scale = dh ** -0.5

    def body(q_ref, k_ref, v_ref, do_ref, cq_ref, ck_ref, lse_ref, after_ref, dq_ref, dk_ref, dv_ref, dck_ref):
        i = pl.program_id(1)

        @pl.when(i == 0)
        def _():
            dk_ref[...] = jnp.zeros_like(dk_ref)
            dv_ref[...] = jnp.zeros_like(dv_ref)
            dck_ref[...] = jnp.zeros_like(dck_ref)

        s, q, k = _fox_scores(q_ref, k_ref, cq_ref, ck_ref, i, tq, S, scale)
        p = jnp.exp(s - lse_ref[...])
        do_b = do_ref[...].astype(BF16)
        dp = lax.dot_general(do_b, v_ref[...].astype(BF16), (((1,), (1,)), ((), ())), preferred_element_type=F32)
        delta = jnp.sum(p * dp, axis=-1, keepdims=True)
        ds = p * (dp - delta)
        ds_b = ds.astype(BF16)
        dq_ref[...] = (jnp.dot(ds_b, k, preferred_element_type=F32) * scale).astype(BF16)
        dk_ref[...] += lax.dot_general(ds_b, q, (((0,), (0,)), ((), ())), preferred_element_type=F32) * scale
        dv_ref[...] += lax.dot_general(p.astype(BF16), do_b, (((0,), (0,)), ((), ())), preferred_element_type=F32)
        dck_ref[...] -= jnp.sum(ds, axis=0, keepdims=True)

    W = H * dh
    return pl.pallas_call(
        body, name=name, grid=(H, S // tq),
        in_specs=[pl.BlockSpec((tq, dh), lambda h, i: (i, h)),
                  pl.BlockSpec((S, dh), lambda h, i: (0, H + h)),
                  pl.BlockSpec((S, dh), lambda h, i: (0, 2 * H + h)),
                  pl.BlockSpec((tq, dh), lambda h, i: (i, h)),
                  pl.BlockSpec((None, tq, 1), lambda h, i: (h, i, 0)),
                  pl.BlockSpec((None, 1, S), lambda h, i: (h, 0, 0)),
                  pl.BlockSpec((None, tq, 1), lambda h, i: (h, i, 0)),
                  pl.BlockSpec(memory_space=pl.ANY)],
        out_specs=[pl.BlockSpec((tq, dh), lambda h, i: (i, h)),
                   pl.BlockSpec((S, dh), lambda h, i: (0, h)),
                   pl.BlockSpec((S, dh), lambda h, i: (0, h)),
                   pl.BlockSpec((None, 1, S), lambda h, i: (h, 0, 0))],
        out_shape=[jax.ShapeDtypeStruct((S, W), BF16), jax.ShapeDtypeStruct((S, W), F32),
                   jax.ShapeDtypeStruct((S, W), F32), jax.ShapeDtypeStruct((H, 1, S), F32)],
        compiler_params=_params(("parallel", "arbitrary")),
    )(proj, proj, proj, do, cum_col, cum_row, lse, after)


def _rope(parts, tabs, out_dtype, name):
    S = parts[0][0].shape[0]
    widths = [a.shape[1] for a, _ in parts]
    total = sum(widths)
    tm = _tile(S, ROW_TILE, 8)
    flags = [r for _, r in parts]

    def body(*refs):
        in_refs = refs[:len(parts)]
        cos_ref, sa_ref, sb_ref, o_ref = refs[len(parts):]
        cos, sa, sb = cos_ref[...], sa_ref[...], sb_ref[...]
        off = 0
        for ref, rot, w in zip(in_refs, flags, widths):
            for j in range(w // LANE):
                t = ref[:, j * LANE:(j + 1) * LANE]
                if rot:
                    t = t * cos + pltpu.roll(t, LANE - ROPE_DIM // 2, axis=1) * sa + pltpu.roll(t, ROPE_DIM // 2, axis=1) * sb
                o_ref[:, off + j * LANE:off + (j + 1) * LANE] = t.astype(o_ref.dtype)
            off += w

    return pl.pallas_call(
        body, name=name, grid=(S // tm,),
        in_specs=[pl.BlockSpec((tm, w), lambda i: (i, 0)) for w in widths] + [_row_spec(tm, LANE)] * 3,
        out_specs=_row_spec(tm, total),
        out_shape=jax.ShapeDtypeStruct((S, total), out_dtype),
        compiler_params=_params(("parallel",)),
    )(*[a for a, _ in parts], *tabs)


def _swa_band(ref_p, ref_c, hk):
    dh = SWA_HEAD_DIM
    return jnp.concatenate([ref_p[:, hk * dh:(hk + 1) * dh], ref_c[:, hk * dh:(hk + 1) * dh]], axis=0).astype(BF16)


def _swa_mask(n, G):
    qi = lax.broadcasted_iota(jnp.int32, (G * Q_BLOCK, 2 * Q_BLOCK), 0) % Q_BLOCK
    kj = lax.broadcasted_iota(jnp.int32, (G * Q_BLOCK, 2 * Q_BLOCK), 1)
    rel = qi + Q_BLOCK - kj
    return (rel >= 0) & (rel < SWA_WINDOW) & ((kj >= Q_BLOCK) | (n > 0))


def _swa_stack(ref, hk, G):
    dh = SWA_HEAD_DIM
    return jnp.concatenate([ref[:, (hk * G + g) * dh:(hk * G + g + 1) * dh] for g in range(G)], axis=0)


def _swa_unstack(ref, stacked, hk, G):
    dh, QB = SWA_HEAD_DIM, Q_BLOCK
    for g in range(0, G, 2):
        c0 = (hk * G + g) * dh
        pair = jnp.concatenate([stacked[g * QB:(g + 1) * QB], stacked[(g + 1) * QB:(g + 2) * QB]], axis=1)
        ref[:, c0:c0 + 2 * dh] = pair.astype(ref.dtype)


def _swa_sink_rows(sink_ref, hk, G):
    return jnp.concatenate([jnp.broadcast_to(sink_ref[0:1, hk * G + g:hk * G + g + 1], (Q_BLOCK, 1)) for g in range(G)],
                           axis=0)


def _swa_attn_fwd(qk, proj, sinks, Hq, name):
    S = qk.shape[0]
    dh, G, QB = SWA_HEAD_DIM, SWA_GROUP, Q_BLOCK
    Hk = Hq // G
    Wq, Wk = Hq * dh, Hk * dh
    nb = S // QB
    scale = dh ** -0.5

    def body(q_ref, kp_ref, kc_ref, vp_ref, vc_ref, sink_ref, o_ref, lse_ref):
        n = pl.program_id(0)
        mask = _swa_mask(n, G)
        lane = lax.broadcasted_iota(jnp.int32, (QB, LANE), 1)
        lse_tile = jnp.zeros((QB, LANE), F32)
        for hk in range(Hk):
            kb = _swa_band(kp_ref, kc_ref, hk)
            vb = _swa_band(vp_ref, vc_ref, hk)
            q = _swa_stack(q_ref, hk, G).astype(BF16)
            sk = _swa_sink_rows(sink_ref, hk, G)
            s = lax.dot_general(q, kb, (((1,), (1,)), ((), ())), preferred_element_type=F32) * scale
            s = jnp.where(mask, s, -jnp.inf)
            m = jnp.maximum(jnp.max(s, axis=-1, keepdims=True), sk)
            p = jnp.exp(s - m)
            l = jnp.sum(p, axis=-1, keepdims=True) + jnp.exp(sk - m)
            o = jnp.dot(p.astype(BF16), vb, preferred_element_type=F32) / l
            lse = m + jnp.log(l)
            for g in range(G):
                lse_tile = jnp.where(lane == hk * G + g, lse[g * QB:(g + 1) * QB], lse_tile)
            _swa_unstack(o_ref, o, hk, G)
        lse_ref[...] = lse_tile

    kcol, vcol = Wq // Wk, (Wq + Wk) // Wk
    return pl.pallas_call(
        body, name=name, grid=(nb,),
        in_specs=[pl.BlockSpec((QB, Wq), lambda n: (n, 0)),
                  pl.BlockSpec((QB, Wk), lambda n: (jnp.maximum(n - 1, 0), kcol)),
                  pl.BlockSpec((QB, Wk), lambda n: (n, kcol)),
                  pl.BlockSpec((QB, Wk), lambda n: (jnp.maximum(n - 1, 0), vcol)),
                  pl.BlockSpec((QB, Wk), lambda n: (n, vcol)),
                  pl.BlockSpec((1, LANE), lambda n: (0, 0))],
        out_specs=[pl.BlockSpec((QB, Wq), lambda n: (n, 0)), pl.BlockSpec((QB, LANE), lambda n: (n, 0))],
        out_shape=[jax.ShapeDtypeStruct((S, Wq), BF16), jax.ShapeDtypeStruct((S, LANE), F32)],
        compiler_params=_params(("parallel",)),
    )(qk, qk, qk, proj, proj, sinks)


def _swa_attn_bwd(qk, proj, sinks, do, lse, Hq, after, name):
    S = qk.shape[0]
    dh, G, QB = SWA_HEAD_DIM, SWA_GROUP, Q_BLOCK
    Hk = Hq // G
    Wq, Wk = Hq * dh, Hk * dh
    nb = S // QB
    scale = dh ** -0.5

    def body(q_ref, kp_ref, kc_ref, vp_ref, vc_ref, sink_ref, do_ref, lse_ref, after_ref,
             dq_ref, dk_ref, dv_ref, dsink_ref, carry_k, carry_v):
        n = pl.program_id(0)

        @pl.when(n == 0)
        def _():
            dsink_ref[...] = jnp.zeros_like(dsink_ref)

        @pl.when(n < nb)
        def _():
            mask = _swa_mask(n, G)
            lane = lax.broadcasted_iota(jnp.int32, (1, LANE), 1)
            dsink = jnp.zeros((1, LANE), F32)
            dk_heads, dv_heads = [], []
            for hk in range(Hk):
                kb = _swa_band(kp_ref, kc_ref, hk)
                vb = _swa_band(vp_ref, vc_ref, hk)
                q = _swa_stack(q_ref, hk, G).astype(BF16)
                do_s = _swa_stack(do_ref, hk, G).astype(BF16)
                lse = jnp.concatenate([lse_ref[:, hk * G + g:hk * G + g + 1] for g in range(G)], axis=0)
                s = lax.dot_general(q, kb, (((1,), (1,)), ((), ())), preferred_element_type=F32) * scale
                s = jnp.where(mask, s, -jnp.inf)
                p = jnp.exp(s - lse)
                p_sink = jnp.exp(_swa_sink_rows(sink_ref, hk, G) - lse)
                dp = lax.dot_general(do_s, vb, (((1,), (1,)), ((), ())), preferred_element_type=F32)
                delta = jnp.sum(p * dp, axis=-1, keepdims=True)
                ds_b = (p * (dp - delta)).astype(BF16)
                _swa_unstack(dq_ref, jnp.dot(ds_b, kb, preferred_element_type=F32) * scale, hk, G)
                dk_heads.append(lax.dot_general(ds_b, q, (((0,), (0,)), ((), ())), preferred_element_type=F32) * scale)
                dv_heads.append(lax.dot_general(p.astype(BF16), do_s, (((0,), (0,)), ((), ())), preferred_element_type=F32))
                sink_term = p_sink * delta
                for g in range(G):
                    dsink = jnp.where(lane == hk * G + g,
                                      -jnp.sum(sink_term[g * QB:(g + 1) * QB], axis=0, keepdims=True), dsink)
            dsink_ref[...] += dsink
            dk_all = jnp.concatenate(dk_heads, axis=1)
            dv_all = jnp.concatenate(dv_heads, axis=1)

            @pl.when(n > 0)
            def _():
                dk_ref[...] = carry_k[...] + dk_all[:QB]
                dv_ref[...] = carry_v[...] + dv_all[:QB]

            carry_k[...] = dk_all[QB:]
            carry_v[...] = dv_all[QB:]

        @pl.when(n == nb)
        def _():
            dk_ref[...] = carry_k[...]
            dv_ref[...] = carry_v[...]

    kcol, vcol = Wq // Wk, (Wq + Wk) // Wk
    cur = lambda n: jnp.minimum(n, nb - 1)
    prev = lambda n: jnp.maximum(jnp.minimum(n, nb - 1) - 1, 0)
    return pl.pallas_call(
        body, name=name, grid=(nb + 1,),
        in_specs=[pl.BlockSpec((QB, Wq), lambda n: (cur(n), 0)),
                  pl.BlockSpec((QB, Wk), lambda n: (prev(n), kcol)),
                  pl.BlockSpec((QB, Wk), lambda n: (cur(n), kcol)),
                  pl.BlockSpec((QB, Wk), lambda n: (prev(n), vcol)),
                  pl.BlockSpec((QB, Wk), lambda n: (cur(n), vcol)),
                  pl.BlockSpec((1, LANE), lambda n: (0, 0)),
                  pl.BlockSpec((QB, Wq), lambda n: (cur(n), 0)),
                  pl.BlockSpec((QB, LANE), lambda n: (cur(n), 0)),
                  pl.BlockSpec(memory_space=pl.ANY)],
        out_specs=[pl.BlockSpec((QB, Wq), lambda n: (cur(n), 0)),
                   pl.BlockSpec((QB, Wk), lambda n: (jnp.maximum(n - 1, 0), 0)),
                   pl.BlockSpec((QB, Wk), lambda n: (jnp.maximum(n - 1, 0), 0)),
                   pl.BlockSpec((1, LANE), lambda n: (0, 0))],
        out_shape=[jax.ShapeDtypeStruct((S, Wq), F32), jax.ShapeDtypeStruct((S, Wk), F32),
                   jax.ShapeDtypeStruct((S, Wk), F32), jax.ShapeDtypeStruct((1, LANE), F32)],
        scratch_shapes=[pltpu.VMEM((QB, Wk), F32), pltpu.VMEM((QB, Wk), F32)],
        compiler_params=_params(("arbitrary",)),
    )(qk, qk, qk, proj, proj, sinks, do, lse, after)


ADA_ROWS = 16


def _ada_mod(c_pad, w, b, name):
    L, D, N = w.shape
    tn = _tile(N, 512)

    def body(c_ref, w_ref, b_ref, o_ref):
        c = c_ref[...]
        c = c * _sigmoid(c)
        ch = c.astype(BF16)
        cl = (c - ch.astype(F32)).astype(BF16)
        ww = w_ref[...]
        wh = ww.astype(BF16)
        wl = (ww - wh.astype(F32)).astype(BF16)
        acc = jnp.dot(ch, wh, preferred_element_type=F32)
        acc += jnp.dot(ch, wl, preferred_element_type=F32)
        acc += jnp.dot(cl, wh, preferred_element_type=F32)
        o_ref[...] = acc + b_ref[...]

    return pl.pallas_call(
        body, name=name, grid=(L, N // tn),
        in_specs=[pl.BlockSpec((ADA_ROWS, D), lambda l, j: (0, 0)),
                  pl.BlockSpec((None, D, tn), lambda l, j: (l, 0, j)),
                  pl.BlockSpec((None, 1, tn), lambda l, j: (l, 0, j))],
        out_specs=pl.BlockSpec((None, ADA_ROWS, tn), lambda l, j: (l, 0, j)),
        out_shape=jax.ShapeDtypeStruct((L, ADA_ROWS, N), F32),
        compiler_params=_params(("parallel", "parallel")),
    )(c_pad, w, b)


def _ada_bwd(cT, dm, name):
    D = cT.shape[0]
    L, B, N = dm.shape
    tm = _tile(D, 256)

    def body(c_ref, dm_ref, o_ref):
        c = c_ref[...]
        c = c * _sigmoid(c)
        dmv = dm_ref[...]
        acc = c[:, 0:1] * dmv[0:1, :]
        for b in range(1, B):
            acc += c[:, b:b + 1] * dmv[b:b + 1, :]
        o_ref[...] = acc

    return pl.pallas_call(
        body, name=name, grid=(L, D // tm),
        in_specs=[pl.BlockSpec((tm, B), lambda l, i: (i, 0)), pl.BlockSpec((None, B, N), lambda l, i: (l, 0, 0))],
        out_specs=pl.BlockSpec((None, tm, N), lambda l, i: (l, i, 0)),
        out_shape=jax.ShapeDtypeStruct((L, D, N), F32),
        compiler_params=_params(("parallel", "parallel")),
    )(cT, dm)


def _adamw_math(w, g, m, v):
    m = ADAM_B1 * m + (1.0 - ADAM_B1) * g
    v = ADAM_B2 * v + (1.0 - ADAM_B2) * (g * g)
    m_hat = m / (1.0 - ADAM_B1 ** ADAM_STEP)
    v_hat = v / (1.0 - ADAM_B2 ** ADAM_STEP)
    delta = -ADAM_LR * (m_hat / (jnp.sqrt(v_hat) + ADAM_EPS) + ADAM_WD * w)
    return delta, m, v


def _adam_rows(R, C):
    lanes = -(-C // LANE) * LANE
    return _tile(R, max(8, (262144 // lanes) // 8 * 8), 8)


def _adam_sum(recv, w, m, v, layer, filled, name):
    P, R, C = recv.shape
    L = w.shape[0]
    tr = _adam_rows(R, C)
    n_keep = 0 if filled is None else 4

    def body(r_ref, w_ref, m_ref, v_ref, *rest):
        g_ref, d_ref, mo_ref, vo_ref = rest[n_keep:]
        g = r_ref[0].astype(F32)
        for p in range(1, P):
            g = g + r_ref[p].astype(F32)
        delta, mn, vn = _adamw_math(w_ref[...], g, m_ref[...], v_ref[...])
        g_ref[...] = g
        d_ref[...] = delta
        mo_ref[...] = mn
        vo_ref[...] = vn

    spec = pl.BlockSpec((None, tr, C), lambda i: (layer, i, 0))
    return pl.pallas_call(
        body, name=name, grid=(R // tr,),
        in_specs=[pl.BlockSpec((P, tr, C), lambda i: (0, i, 0)), spec, spec, spec]
        + [pl.BlockSpec(memory_space=pl.ANY)] * n_keep,
        out_specs=[spec] * 4,
        out_shape=[jax.ShapeDtypeStruct((L, R, C), F32)] * 4,
        input_output_aliases={4 + k: k for k in range(n_keep)},
        compiler_params=_params(("parallel",)),
    )(recv, w, m, v, *(filled or ()))


def _adam(g, w, m, v, name):
    L, R, C = w.shape
    tr = _adam_rows(R, C)

    def body(g_ref, w_ref, m_ref, v_ref, d_ref, mo_ref, vo_ref):
        delta, mn, vn = _adamw_math(w_ref[...], g_ref[...], m_ref[...], v_ref[...])
        d_ref[...] = delta
        mo_ref[...] = mn
        vo_ref[...] = vn

    spec = pl.BlockSpec((None, tr, C), lambda l, i: (l, i, 0))
    return pl.pallas_call(
        body, name=name, grid=(L, R // tr),
        in_specs=[spec] * 4, out_specs=[spec] * 3,
        out_shape=[jax.ShapeDtypeStruct((L, R, C), F32)] * 3,
        compiler_params=_params(("parallel", "parallel")),
    )(g, w, m, v)


def _sum_slots(x, name):
    P, R, C = x.shape

    def body(x_ref, o_ref):
        acc = x_ref[0]
        for p in range(1, P):
            acc = acc + x_ref[p]
        o_ref[...] = acc

    return pl.pallas_call(
        body, name=name,
        in_specs=[pl.BlockSpec(memory_space=pltpu.VMEM)], out_specs=pl.BlockSpec(memory_space=pltpu.VMEM),
        out_shape=jax.ShapeDtypeStruct((R, C), F32),
        compiler_params=pltpu.CompilerParams(vmem_limit_bytes=48 * MIB),
    )(x)


def _my_pos():
    return lax.axis_index("x"), lax.axis_index("y"), lax.axis_index("c")


def _all_gather_small(x, name, after=()):
    R, C = x.shape
    n_after = len(after)

    def body(x_ref, *rest):
        out_ref, send_sems, recv_sems = rest[n_after:]
        x_, y_, c_ = _my_pos()
        me, sibling = (x_, y_, c_), (x_, y_, 1 - c_)
        chips = [(1 - x_, y_), (x_, 1 - y_), (1 - x_, 1 - y_)]

        def slot(px, py, pc):
            return out_ref.at[4 * px + 2 * py + pc]

        def copy(k, block, to):
            return pltpu.make_async_remote_copy(
                src_ref=slot(*block), dst_ref=slot(*block), send_sem=send_sems.at[k], recv_sem=recv_sems.at[k],
                device_id=to, device_id_type=MESH)

        out_ref[4 * x_ + 2 * y_ + c_] = x_ref[...]
        first = [copy(0, me, sibling)] + [copy(1 + j, me, (*chip, c_)) for j, chip in enumerate(chips)]
        for cp in first:
            cp.start()
        passed = [copy(4 + j, (*chip, c_), sibling) for j, chip in enumerate(chips)]
        for j, chip in enumerate(chips):
            copy(1 + j, (*chip, c_), me).wait_recv()
            passed[j].start()
        copy(0, sibling, me).wait_recv()
        for j, chip in enumerate(chips):
            copy(4 + j, (*chip, 1 - c_), me).wait_recv()
        for cp in first + passed:
            cp.wait_send()

    return pl.pallas_call(
        body, name=name,
        in_specs=[pl.BlockSpec(memory_space=pltpu.VMEM)] + [pl.BlockSpec(memory_space=pl.ANY)] * n_after,
        out_specs=pl.BlockSpec(memory_space=pltpu.VMEM),
        out_shape=jax.ShapeDtypeStruct((N_DEV, R, C), x.dtype),
        scratch_shapes=[pltpu.SemaphoreType.DMA((7,)), pltpu.SemaphoreType.DMA((7,))],
        compiler_params=pltpu.CompilerParams(vmem_limit_bytes=48 * MIB),
    )(x, *after)


HBM_SPEC = pl.BlockSpec(memory_space=pltpu.HBM)
SEM_SPEC = pl.BlockSpec(memory_space=pltpu.SEMAPHORE)
ANY_SPEC = pl.BlockSpec(memory_space=pl.ANY)
SPLIT_EFFECT = pltpu.SideEffectType.DATAFLOW_SIDE_EFFECTING


def _in_hbm(a):
    return pltpu.with_memory_space_constraint(a, pltpu.HBM)


def _gathered_shape(a, kind):
    if kind == "major":
        return (N_DEV,) + a.shape
    if kind == "rows":
        return (N_DEV * a.shape[0], a.shape[1])
    return (2, a.shape[0], 4 * a.shape[1])


def _gather_slot(ref, kind, block, shard_shape):
    px, py, pc = block
    if kind == "major":
        return ref.at[4 * px + 2 * py + pc]
    if kind == "rows":
        r = shard_shape[0]
        return ref.at[pl.ds(pl.multiple_of((4 * px + 2 * py + pc) * r, r), r), :]
    cu = shard_shape[1]
    return ref.at[px, :, pl.ds(pl.multiple_of((2 * py + pc) * cu, cu), cu)]


def _gather_peers():
    x_, y_, c_ = _my_pos()
    return (x_, y_, c_), (x_, y_, 1 - c_), [(1 - x_, y_), (x_, 1 - y_), (1 - x_, 1 - y_)]


def _gather_start(shards, kinds, after, name):
    n = len(shards)
    bufs = [lax.empty(_gathered_shape(a, k), a.dtype) for a, k in zip(shards, kinds)]
    extra = [] if after is None else [after]

    def body(*refs):
        shard_refs, buf_refs = refs[:n], refs[n:2 * n]
        send_sems, recv_sems, local_sems = refs[2 * n + len(extra):2 * n + len(extra) + 3]
        token = refs[-1]
        me, sibling, chips = _gather_peers()
        for e in range(n):
            mine = _gather_slot(buf_refs[e], kinds[e], me, shards[e].shape)
            pltpu.make_async_copy(shard_refs[e], mine, local_sems.at[e]).start()
            for k, to in enumerate([sibling] + [(*chip, me[2]) for chip in chips]):
                pltpu.make_async_remote_copy(
                    src_ref=shard_refs[e], dst_ref=mine, send_sem=send_sems.at[4 * e + k],
                    recv_sem=recv_sems.at[4 * e + k], device_id=to, device_id_type=MESH).start()
        token[...] = jnp.zeros_like(token)

    out = pl.pallas_call(
        body, name=name,
        out_shape=(pltpu.SemaphoreType.DMA((4 * n,)), pltpu.SemaphoreType.DMA((4 * n,)), pltpu.SemaphoreType.DMA((n,)),
                   *[pltpu.HBM(a.shape, a.dtype) for a in shards], *[pltpu.HBM(a.shape, a.dtype) for a in bufs],
                   jax.ShapeDtypeStruct((8, LANE), F32)),
        in_specs=[HBM_SPEC] * (2 * n) + [ANY_SPEC] * len(extra),
        out_specs=(SEM_SPEC, SEM_SPEC, SEM_SPEC, *[HBM_SPEC] * (2 * n), pl.BlockSpec(memory_space=pltpu.VMEM)),
        input_output_aliases={i: 3 + i for i in range(2 * n)},
        compiler_params=pltpu.CompilerParams(has_side_effects=SPLIT_EFFECT),
    )(*[_in_hbm(a) for a in shards], *[_in_hbm(a) for a in bufs], *extra)
    return out[0], out[1], out[2], out[3:3 + n], out[3 + n:3 + 2 * n], out[-1]


def _gather_forward(recv_sems, bufs, kinds, shard_shapes, after, name):
    n = len(bufs)

    def body(*refs):
        buf_refs, recv_in = refs[:n], refs[n]
        fsend, frecv = refs[n + 2], refs[n + 3]
        token = refs[-1]
        me, sibling, chips = _gather_peers()
        for e in range(n):
            for j, chip in enumerate(chips):
                slot = _gather_slot(buf_refs[e], kinds[e], (*chip, me[2]), shard_shapes[e])
                pltpu.make_async_remote_copy(
                    src_ref=slot, dst_ref=slot, send_sem=recv_in.at[4 * e + 1 + j], recv_sem=recv_in.at[4 * e + 1 + j],
                    device_id=me, device_id_type=MESH).wait_recv()
                pltpu.make_async_remote_copy(
                    src_ref=slot, dst_ref=slot, send_sem=fsend.at[3 * e + j], recv_sem=frecv.at[3 * e + j],
                    device_id=sibling, device_id_type=MESH).start()
        token[...] = jnp.zeros_like(token)

    out = pl.pallas_call(
        body, name=name,
        out_shape=(pltpu.SemaphoreType.DMA((3 * n,)), pltpu.SemaphoreType.DMA((3 * n,)),
                   *[pltpu.HBM(a.shape, a.dtype) for a in bufs], jax.ShapeDtypeStruct((8, LANE), F32)),
        in_specs=[HBM_SPEC] * n + [SEM_SPEC, ANY_SPEC],
        out_specs=(SEM_SPEC, SEM_SPEC, *[HBM_SPEC] * n, pl.BlockSpec(memory_space=pltpu.VMEM)),
        input_output_aliases={i: 2 + i for i in range(n)},
        compiler_params=pltpu.CompilerParams(has_side_effects=SPLIT_EFFECT),
    )(*bufs, recv_sems, after)
    return out[0], out[1], out[2:2 + n], out[-1]


def _gather_wait(send_sems, recv_sems, local_sems, fsend, frecv, shards, bufs, kinds, after, name):
    n = len(bufs)

    def body(*refs):
        shard_refs, buf_refs = refs[:n], refs[n:2 * n]
        send_in, recv_in, local_in, fsend_in, frecv_in = refs[2 * n:2 * n + 5]
        me, sibling, chips = _gather_peers()

        def arrival(slot, sem):
            return pltpu.make_async_remote_copy(src_ref=slot, dst_ref=slot, send_sem=sem, recv_sem=sem,
                                                device_id=me, device_id_type=MESH)

        for e in range(n):
            shape = shards[e].shape
            mine = _gather_slot(buf_refs[e], kinds[e], me, shape)
            pltpu.make_async_copy(shard_refs[e], mine, local_in.at[e]).wait()
            arrival(_gather_slot(buf_refs[e], kinds[e], sibling, shape), recv_in.at[4 * e]).wait_recv()
            for j, chip in enumerate(chips):
                arrival(_gather_slot(buf_refs[e], kinds[e], (*chip, 1 - me[2]), shape), frecv_in.at[3 * e + j]).wait_recv()
            for k in range(4):
                arrival(mine, send_in.at[4 * e + k]).wait_send()
            for j in range(3):
                arrival(mine, fsend_in.at[3 * e + j]).wait_send()

    out = pl.pallas_call(
        body, name=name,
        out_shape=(*[pltpu.HBM(a.shape, a.dtype) for a in shards], *[pltpu.HBM(a.shape, a.dtype) for a in bufs]),
        in_specs=[HBM_SPEC] * (2 * n) + [SEM_SPEC] * 5 + [ANY_SPEC],
        out_specs=tuple([HBM_SPEC] * (2 * n)),
        input_output_aliases={i: i for i in range(2 * n)},
        compiler_params=pltpu.CompilerParams(has_side_effects=SPLIT_EFFECT),
    )(*shards, *bufs, send_sems, recv_sems, local_sems, fsend, frecv, after)
    return out[n:]


def _grad_slice(ref, kind, j):
    if kind == "major":
        return ref.at[j]
    if kind == "rows":
        r = ref.shape[0] // N_DEV
        return ref.at[pl.ds(j * r, r), :]
    cu = ref.shape[2] // 4
    return ref.at[j // 4, :, pl.ds((j % 4) * cu, cu)]


def _slice_shape(a, kind):
    if kind == "major":
        return a.shape[1:]
    if kind == "rows":
        return (a.shape[0] // N_DEV, a.shape[1])
    return (a.shape[1], a.shape[2] // 4)


def _scatter_copies(srcs, lands, kinds, send_sems, recv_sems):
    x_, y_, c_ = _my_pos()
    me = 4 * x_ + 2 * y_ + c_
    n = len(srcs)

    def remote(e, j):
        return pltpu.make_async_remote_copy(
            src_ref=_grad_slice(srcs[e], kinds[e], j), dst_ref=lands[e].at[me],
            send_sem=send_sems.at[e * N_DEV + j], recv_sem=recv_sems.at[e * N_DEV + me],
            device_id=(j // 4, (j // 2) % 2, j % 2), device_id_type=MESH)

    def local(e, j):
        return pltpu.make_async_copy(_grad_slice(srcs[e], kinds[e], j), lands[e].at[j], recv_sems.at[e * N_DEV + j])

    def arrival(e, i):
        return pltpu.make_async_remote_copy(
            src_ref=_grad_slice(srcs[e], kinds[e], i), dst_ref=lands[e].at[i],
            send_sem=send_sems.at[e * N_DEV + i], recv_sem=recv_sems.at[e * N_DEV + i],
            device_id=(i // 4, (i // 2) % 2, i % 2), device_id_type=MESH)

    def start():
        for e in range(n):
            for j in range(N_DEV):
                @pl.when(me == j)
                def _():
                    local(e, j).start()

                @pl.when(me != j)
                def _():
                    remote(e, j).start()

    def wait():
        for e in range(n):
            for i in range(N_DEV):
                @pl.when(me == i)
                def _():
                    local(e, i).wait()

                @pl.when(me != i)
                def _():
                    arrival(e, i).wait_recv()
        for e in range(n):
            for j in range(N_DEV):
                @pl.when(me != j)
                def _():
                    remote(e, j).wait_send()

    return start, wait


def _scatter_start(srcs, kinds, name):
    n = len(srcs)
    lands = [lax.empty((N_DEV,) + _slice_shape(a, k), a.dtype) for a, k in zip(srcs, kinds)]

    def body(*refs):
        src_refs, land_refs = refs[:n], refs[n:2 * n]
        send_sems, recv_sems = refs[2 * n], refs[2 * n + 1]
        token = refs[-1]
        start, _ = _scatter_copies(src_refs, land_refs, kinds, send_sems, recv_sems)
        start()
        token[...] = jnp.zeros_like(token)

    out = pl.pallas_call(
        body, name=name,
        out_shape=(pltpu.SemaphoreType.DMA((n * N_DEV,)), pltpu.SemaphoreType.DMA((n * N_DEV,)),
                   *[pltpu.HBM(a.shape, a.dtype) for a in srcs], *[pltpu.HBM(a.shape, a.dtype) for a in lands],
                   jax.ShapeDtypeStruct((8, LANE), F32)),
        in_specs=[HBM_SPEC] * (2 * n),
        out_specs=(SEM_SPEC, SEM_SPEC, *[HBM_SPEC] * (2 * n), pl.BlockSpec(memory_space=pltpu.VMEM)),
        input_output_aliases={i: 2 + i for i in range(2 * n)},
        compiler_params=pltpu.CompilerParams(has_side_effects=SPLIT_EFFECT),
    )(*[_in_hbm(a) for a in srcs], *[_in_hbm(a) for a in lands])
    return out[0], out[1], out[2:2 + n], out[2 + n:2 + 2 * n], out[-1]


def _scatter_wait(send_sems, recv_sems, srcs, lands, kinds, after, name):
    n = len(srcs)

    def body(*refs):
        src_refs, land_refs = refs[:n], refs[n:2 * n]
        _, wait = _scatter_copies(src_refs, land_refs, kinds, refs[2 * n], refs[2 * n + 1])
        wait()

    out = pl.pallas_call(
        body, name=name,
        out_shape=(*[pltpu.HBM(a.shape, a.dtype) for a in srcs], *[pltpu.HBM(a.shape, a.dtype) for a in lands]),
        in_specs=[HBM_SPEC] * (2 * n) + [SEM_SPEC, SEM_SPEC, pl.BlockSpec(memory_space=pl.ANY)],
        out_specs=tuple([HBM_SPEC] * (2 * n)),
        input_output_aliases={i: i for i in range(2 * n)},
        compiler_params=pltpu.CompilerParams(has_side_effects=SPLIT_EFFECT),
    )(*srcs, *lands, send_sems, recv_sems, after)
    return out[n:]


def _rope_tables(positions, sign):
    half = ROPE_DIM // 2
    inv_freq = ROPE_THETA ** (-jnp.arange(0, ROPE_DIM, 2, dtype=F32) / ROPE_DIM)
    ang = positions.astype(F32)[:, None] * inv_freq
    reps = LANE // half
    cos = jnp.tile(jnp.cos(ang), (1, reps))
    sin = jnp.tile(jnp.sin(ang), (1, reps)) * sign
    d = jnp.arange(LANE) % SWA_HEAD_DIM
    return (jnp.where(d < ROPE_DIM, cos, 1.0), jnp.where(d < half, -sin, 0.0),
            jnp.where((d >= half) & (d < ROPE_DIM), sin, 0.0))


def _pad_cols(a, n):
    return jnp.pad(a, ((0, 0), (0, n - a.shape[1])))


def _local_step(x, target, positions, mods, fetch, P, on_grads):
    S, D = x.shape
    Hf = D // FOX_HEAD_DIM
    Hq = D // SWA_HEAD_DIM
    Hk = Hq // SWA_GROUP
    Wk = Hk * SWA_HEAD_DIM
    n_in = 3 * D + Hf
    (sh1a, sc1a, g1a, sh2a, sc2a, g2a), (sh1b, sc1b, g1b, sh2b, sc2b, g2b) = mods
    row = lambda v: v.reshape(1, -1)
    cw = [jnp.transpose(P["conv_w"][l].reshape(3, 2, -1), (1, 0, 2)) for l in range(2)]
    cb = [P["conv_b"][l].reshape(2, 1, -1) for l in range(2)]

    W = dict(up=[None, None], down=[None, None])
    h1a = _modulate(x, sc1a, sh1a, "modulate_in")
    W["fox_in"], W["fox_o"] = fetch("fox", "wait", h1a)
    proj_a = _mm_nn(h1a, W["fox_in"], F32, "fox_in_proj", tn=896)
    flT = proj_a[:, 3 * D:n_in].T
    bf_col = P["fox_b_f"].reshape(Hf, 1)
    cumT = _fox_prep(flT, bf_col, "fox_cumsum")
    cum_col, cum_row = cumT.reshape(Hf, S, 1), cumT.reshape(Hf, 1, S)
    o_a, lse_a = _fox_attn_fwd(proj_a, cum_col, cum_row, Hf, "fox_attn_fwd")
    token = fetch("ffn0", "forward", o_a)
    y1a = _mm_nn(o_a, W["fox_o"], F32, "fox_out_proj")
    z1a, x1, h2a = _ln_fwd(x, y1a, g1a, row(P["ln_mix_g"][0]), row(P["ln_mix_b"][0]), sc2a, sh2a, token, "ln_mix0")
    W["up"][0], W["down"][0] = fetch("ffn0", "wait", h2a)
    u_a, a_a = _ffn_up(h2a, W["up"][0], cw[0], cb[0], "ffn_up0")
    token = fetch("swa", "forward", a_a)
    y2a = _mm_nn(a_a, W["down"][0], F32, "ffn_down0", tk=1408)
    z2a, x2, h1b = _ln_fwd(x1, y2a, g2a, row(P["ln_ffn_g"][0]), row(P["ln_ffn_b"][0]), sc1b, sh1b, token, "ln_ffn0")

    W["swa_in"], W["swa_o"] = fetch("swa", "wait", h1b)
    proj_b = _mm_nn(h1b, W["swa_in"], F32, "swa_in_proj")
    tabs_f = _rope_tables(positions, 1.0)
    tabs_b = _rope_tables(positions, -1.0)
    qk = _rope([(proj_b[:, :D + Wk], True)], tabs_f, F32, "rope_fwd")
    sinks = _pad_cols(P["swa_sinks"].reshape(1, Hq), LANE)
    o_b, lse_b = _swa_attn_fwd(qk, proj_b, sinks, Hq, "swa_attn_fwd")
    token = fetch("ffn1", "forward", o_b)
    y1b = _mm_nn(o_b, W["swa_o"], F32, "swa_out_proj")
    z1b, x3, h2b = _ln_fwd(x2, y1b, g1b, row(P["ln_mix_g"][1]), row(P["ln_mix_b"][1]), sc2b, sh2b, token, "ln_mix1")
    W["up"][1], W["down"][1] = fetch("ffn1", "wait", h2b)
    u_b, a_b = _ffn_up(h2b, W["up"][1], cw[1], cb[1], "ffn_up1")
    y2b = _mm_nn(a_b, W["down"][1], F32, "ffn_down1", tk=1408)
    z2b, dout, loss_row = _ln_fwd_loss(x3, y2b, g2b, row(P["ln_ffn_g"][1]), row(P["ln_ffn_b"][1]), target, "ln_ffn1_loss")

    def ffn_backward(dy, a, u, h_in, l, tag):
        d_down = _mm_tn(a, dy[None], BF16, "ffn_dwdown" + tag, tm=1408, tn=1024)[0]
        token = on_grads("ffn_w_down" + tag, d_down)
        da = _mm_nt(dy[None], W["down"][l][None], F32, "ffn_da" + tag)
        du, dcw, dcb = _ffn_bwd_elem(da, u, cw[l], cb[l], token, "ffn_bwd_elem" + tag)
        d_up = _mm_tn(h_in, du, BF16, "ffn_dwup" + tag)
        token = on_grads("ffn_w_up" + tag, d_up)
        dh = _mm_nt(du, W["up"][l], F32, "ffn_dh" + tag, tk=1408)
        return dh, token, jnp.transpose(dcw, (1, 0, 2)).reshape(3, -1), dcb.reshape(-1)

    dz2b, dy2b, dg_f1, db_f1, dgate2b = _ln_bwd(dout, z2b, y2b, g2b, row(P["ln_ffn_g"][1]), "ln_ffn1_bwd")
    dh2b, token, dcw1, dcb1 = ffn_backward(dy2b, a_b, u_b, h2b, 1, "1")
    dx3, dsc2b, dsh2b = _mod_bwd(dz2b, dh2b, x3, sc2b, token, "mod_ffn1_bwd")

    dz1b, dy1b, dg_m1, db_m1, dgate1b = _ln_bwd(dx3, z1b, y1b, g1b, row(P["ln_mix_g"][1]), "ln_mix1_bwd")
    token = on_grads("swa_w_o", _mm_tn(o_b, dy1b[None], BF16, "swa_dwo")[0])
    do_b = _mm_nt(dy1b[None], W["swa_o"][None], F32, "swa_do")
    dq_b, dk_b, dv_b, dsinks = _swa_attn_bwd(qk, proj_b, sinks, do_b, lse_b, Hq, token, "swa_attn_bwd")
    dproj_b = _rope([(dq_b, True), (dk_b, True), (dv_b, False)], tabs_b, BF16, "rope_bwd")
    token = on_grads("swa_w_in", _mm_tn(h1b, dproj_b[None], BF16, "swa_dwin")[0])
    dh1b = _mm_nt(dproj_b[None], W["swa_in"][None], F32, "swa_dh", tk=1280)
    dx2, dsc1b, dsh1b = _mod_bwd(dz1b, dh1b, x2, sc1b, token, "mod_mix1_bwd")

    dz2a, dy2a, dg_f0, db_f0, dgate2a = _ln_bwd(dx2, z2a, y2a, g2a, row(P["ln_ffn_g"][0]), "ln_ffn0_bwd")
    dh2a, token, dcw0, dcb0 = ffn_backward(dy2a, a_a, u_a, h2a, 0, "0")
    dx1, dsc2a, dsh2a = _mod_bwd(dz2a, dh2a, x1, sc2a, token, "mod_ffn0_bwd")

    dz1a, dy1a, dg_m0, db_m0, dgate1a = _ln_bwd(dx1, z1a, y1a, g1a, row(P["ln_mix_g"][0]), "ln_mix0_bwd")
    token = on_grads("fox_w_o", _mm_tn(o_a, dy1a[None], BF16, "fox_dwo")[0])
    do_a = _mm_nt(dy1a[None], W["fox_o"][None], F32, "fox_do")
    dq_a, dk_a, dv_a, dcum_row = _fox_attn_bwd(proj_a, do_a, cum_col, cum_row, lse_a, Hf, token, "fox_attn_bwd")
    dflT, dbf = _fox_prep_bwd(dcum_row.reshape(Hf, S), flT, bf_col, "fox_cumsum_bwd")
    n_pad = W["fox_in"].shape[1]
    dproj_a = jnp.concatenate([dq_a, dk_a.astype(BF16), dv_a.astype(BF16),
                               _pad_cols(dflT.T, n_pad - 3 * D).astype(BF16)], axis=1)
    token = on_grads("fox_w_in", _mm_tn(h1a, dproj_a[None], BF16, "fox_dwin", tn=896)[0])
    dh1a = _mm_nt(dproj_a[None], W["fox_in"][None], F32, "fox_dh", tk=896)
    grad_x, dsc1a, dsh1a = _mod_bwd(dz1a, dh1a, x, sc1a, token, "mod_mix0_bwd")

    dmod = jnp.stack([jnp.concatenate([dsh1a, dsc1a, dgate1a, dsh2a, dsc2a, dgate2a], axis=1)[0],
                      jnp.concatenate([dsh1b, dsc1b, dgate1b, dsh2b, dsc2b, dgate2b], axis=1)[0]])
    small = dict(dmod=dmod, conv_b=jnp.stack([dcb0, dcb1]), conv_w=jnp.stack([dcw0, dcw1]),
                 ln_mix_g=jnp.concatenate([dg_m0, dg_m1]), ln_mix_b=jnp.concatenate([db_m0, db_m1]),
                 ln_ffn_g=jnp.concatenate([dg_f0, dg_f1]), ln_ffn_b=jnp.concatenate([db_f0, db_f1]),
                 fox_b_f=dbf.reshape(-1), swa_sinks=dsinks[0, :Hq])
    return loss_row[0, 0], grad_x, small


SMALL_ORDER = ("dmod", "conv_b", "conv_w", "ln_mix_g", "ln_mix_b", "ln_ffn_g", "ln_ffn_b", "fox_b_f", "swa_sinks", "loss")


def _pack_rows(arrays):
    chunks, spans, off = [], [], 0
    for a in arrays:
        flat = a.reshape(-1)
        n = -(-flat.shape[0] // LANE) * LANE
        chunks.append(jnp.pad(flat, (0, n - flat.shape[0])))
        spans.append((off, flat.shape[0], a.shape))
        off += n
    total = -(-off // (8 * LANE)) * (8 * LANE)
    chunks.append(jnp.zeros((total - off,), F32))
    return jnp.concatenate(chunks).reshape(-1, LANE), spans


def _unpack_rows(packed, spans):
    flat = packed.reshape(-1)
    return [flat[off:off + n].reshape(shape) for off, n, shape in spans]


def kernel(x, c, positions, fox_w_in, fox_b_f, fox_w_o, swa_w_in, swa_sinks, swa_w_o, ada_w, ada_b, ffn_w_up, ffn_conv_w, ffn_conv_b, ffn_w_down, ln_mix_g, ln_mix_b, ln_ffn_g, ln_ffn_b, loss_target, m_fox_w_in, m_fox_b_f, m_fox_w_o, m_swa_w_in, m_swa_sinks, m_swa_w_o, m_ada_w, m_ada_b, m_ffn_w_up, m_ffn_conv_w, m_ffn_conv_b, m_ffn_w_down, m_ln_mix_g, m_ln_mix_b, m_ln_ffn_g, m_ln_ffn_b, v_fox_w_in, v_fox_b_f, v_fox_w_o, v_swa_w_in, v_swa_sinks, v_swa_w_o, v_ada_w, v_ada_b, v_ffn_w_up, v_ffn_conv_w, v_ffn_conv_b, v_ffn_w_down, v_ln_mix_g, v_ln_mix_b, v_ln_ffn_g, v_ln_ffn_b):
    S, D = x.shape[1], x.shape[2]
    L = ada_w.shape[0]
    me = 4 * lax.axis_index("x") + 2 * lax.axis_index("y") + lax.axis_index("c")
    n_ada = ada_w.shape[2]
    cu = ffn_w_up.shape[2]
    F = 4 * cu
    n_in = fox_w_in.shape[2] * N_DEV
    n_in_pad = -(-n_in // LANE) * LANE

    c_all = _all_gather_small(c.reshape(-1, LANE), "gather_c").reshape(N_DEV, D)
    b_cols = lax.dynamic_slice_in_dim(ada_b, me * n_ada, n_ada, axis=1).reshape(L, 1, n_ada)
    mod_blk = _ada_mod(jnp.pad(c_all, ((0, ADA_ROWS - N_DEV), (0, 0))), ada_w, b_cols, "ada_mod")[:, :N_DEV]
    mod_all = _all_gather_small(mod_blk.reshape(-1, LANE), "gather_mod").reshape(N_DEV, L, N_DEV, n_ada)
    mod_mine = lax.dynamic_index_in_dim(mod_all, me, axis=2, keepdims=False)
    mod_mine = jnp.transpose(mod_mine, (1, 0, 2)).reshape(L, N_DEV * n_ada)
    mods = [[mod_mine[l, k * D:(k + 1) * D].reshape(1, D) for k in range(6)] for l in range(L)]

    gather_groups = dict(
        fox=([fox_w_in[0].astype(BF16), fox_w_o[0].astype(BF16)], ["major", "rows"]),
        ffn0=([ffn_w_up[0].astype(BF16), ffn_w_down[0].astype(BF16)], ["halves", "rows"]),
        swa=([swa_w_in[0].astype(BF16), swa_w_o[0].astype(BF16)], ["major", "rows"]),
        ffn1=([ffn_w_up[1].astype(BF16), ffn_w_down[1].astype(BF16)], ["halves", "rows"]))
    gather_order = list(gather_groups)
    gathers = {}

    def start_group(group, after):
        shards, kinds = gather_groups[group]
        send, recv, local, thru, bufs, token = _gather_start(shards, kinds, after, "gather_start_" + group)
        gathers[group] = dict(send=send, recv=recv, local=local, shards=thru, bufs=bufs, kinds=kinds,
                              shapes=[a.shape for a in shards], token=token)

    start_group(gather_order[0], None)

    def natural(g, pad_to=None):
        w = jnp.transpose(g, (1, 0, 2)).reshape(D, -1)
        return w if pad_to is None else _pad_cols(w, pad_to)

    def forward_stage(group, after):
        s = gathers[group]
        s["fsend"], s["frecv"], s["bufs"], forwarded = _gather_forward(s["recv"], s["bufs"], s["kinds"], s["shapes"], after,
                                                                       "gather_forward_" + group)
        nxt = gather_order.index(group) + 1
        if nxt == len(gather_order):
            return forwarded
        start_group(gather_order[nxt], forwarded)
        return gathers[gather_order[nxt]]["token"]

    def fetch(group, stage, after):
        if stage == "forward":
            return forward_stage(group, after)
        if group == "fox":
            after = forward_stage(group, gathers[group]["token"])
        s = gathers.pop(group)
        first, second = _gather_wait(s["send"], s["recv"], s["local"], s["fsend"], s["frecv"], s["shards"], s["bufs"],
                                     s["kinds"], after, "gather_wait_" + group)
        if group == "fox":
            return natural(first, n_in_pad), second
        if group == "swa":
            return natural(first), second
        return first, second

    conv_w_all = None
    P = dict(fox_b_f=fox_b_f[0], swa_sinks=swa_sinks[0], conv_b=ffn_conv_b,
             ln_mix_g=ln_mix_g, ln_mix_b=ln_mix_b, ln_ffn_g=ln_ffn_g, ln_ffn_b=ln_ffn_b)
    cw_rows = _all_gather_small(_pack_rows([ffn_conv_w])[0], "gather_conv_w")
    n_cw = ffn_conv_w.size
    cw_dev = cw_rows.reshape(N_DEV, -1)[:, :n_cw].reshape(N_DEV, L, 3, cu)
    conv_w_all = jnp.transpose(cw_dev, (1, 2, 0, 3)).reshape(L, 3, N_DEV * cu)
    P["conv_w"] = conv_w_all

    out, pending = {}, {}

    def shard_major(g):
        return jnp.transpose(g.reshape(D, N_DEV, -1), (1, 0, 2))

    big = dict(
        ffn_w_down1=("ffn_w_down", "rows", 1, (ffn_w_down, m_ffn_w_down, v_ffn_w_down)),
        ffn_w_up1=("ffn_w_up", "halves", 1, (ffn_w_up, m_ffn_w_up, v_ffn_w_up)),
        swa_w_o=("swa_w_o", "rows", 0, (swa_w_o, m_swa_w_o, v_swa_w_o)),
        swa_w_in=("swa_w_in", "major", 0, (swa_w_in, m_swa_w_in, v_swa_w_in)),
        ffn_w_down0=("ffn_w_down", "rows", 0, (ffn_w_down, m_ffn_w_down, v_ffn_w_down)),
        ffn_w_up0=("ffn_w_up", "halves", 0, (ffn_w_up, m_ffn_w_up, v_ffn_w_up)),
        fox_w_o=("fox_w_o", "rows", 0, (fox_w_o, m_fox_w_o, v_fox_w_o)),
        fox_w_in=("fox_w_in", "major", 0, (fox_w_in, m_fox_w_in, v_fox_w_in)))
    finish_at = dict(swa_w_o=["ffn_w_down1"], ffn_w_up0=["ffn_w_up1", "swa_w_o", "swa_w_in"],
                     fox_w_o=["ffn_w_down0"], fox_w_in=["ffn_w_up0"])

    def finish(name, after):
        send, recv, thru, lands = pending.pop(name)
        param, kind, layer, wmv = big[name]
        landed, = _scatter_wait(send, recv, thru, lands, [kind], after, "scatter_wait_" + name)
        out[param] = _adam_sum(landed, *wmv, layer, out.get(param), "adam_" + name)

    def on_grads(name, g):
        kind = big[name][1]
        if name == "fox_w_in":
            g = g[:, :n_in]
        src = shard_major(g) if kind == "major" else g
        send, recv, thru, lands, token = _scatter_start([src], [kind], "scatter_start_" + name)
        pending[name] = (send, recv, thru, lands)
        for done in finish_at.get(name, ()):
            finish(done, token)
        return token

    loss_local, grad_x, small = _local_step(x[0], loss_target[0], positions[0], mods, fetch, P, on_grads)
    small["loss"] = loss_local.reshape(1)

    packed, spans = _pack_rows([small[k] for k in SMALL_ORDER])
    done_first = [out[k][0] for k in ("ffn_w_up", "ffn_w_down", "swa_w_in", "swa_w_o")]
    gathered = _all_gather_small(packed, "gather_small_grads", after=done_first)
    finish("fox_w_o", gathered)
    finish("fox_w_in", gathered)
    totals = dict(zip(SMALL_ORDER, _unpack_rows(_sum_slots(gathered, "sum_small_grads"), spans)))
    loss = totals["loss"].reshape(())
    n_mod = L * 6 * D
    dmod_all = gathered.reshape(N_DEV, -1)[:, :n_mod].reshape(N_DEV, L, 6 * D)
    dmod_cols = jnp.transpose(lax.dynamic_slice_in_dim(dmod_all, me * n_ada, n_ada, axis=2), (1, 0, 2))
    g_ada_w = _ada_bwd(c_all.T, dmod_cols, "ada_w_grad")
    out["ada_w"] = (g_ada_w,) + tuple(_adam(g_ada_w, ada_w, m_ada_w, v_ada_w, "adam_ada_w"))

    g_small = dict(fox_b_f=totals["fox_b_f"].reshape(fox_b_f.shape), swa_sinks=totals["swa_sinks"].reshape(swa_sinks.shape),
                   ada_b=totals["dmod"].reshape(ada_b.shape), ffn_conv_b=totals["conv_b"].reshape(ffn_conv_b.shape),
                   ffn_conv_w=lax.dynamic_slice_in_dim(totals["conv_w"].reshape(L, 3, 2 * F), me * cu, cu, axis=2),
                   ln_mix_g=totals["ln_mix_g"], ln_mix_b=totals["ln_mix_b"],
                   ln_ffn_g=totals["ln_ffn_g"], ln_ffn_b=totals["ln_ffn_b"])
    small_names = ("fox_b_f", "swa_sinks", "ada_b", "ffn_conv_b", "ffn_conv_w", "ln_mix_g", "ln_mix_b", "ln_ffn_g", "ln_ffn_b")
    w_small = dict(fox_b_f=(fox_b_f, m_fox_b_f, v_fox_b_f), swa_sinks=(swa_sinks, m_swa_sinks, v_swa_sinks),
                   ada_b=(ada_b, m_ada_b, v_ada_b), ffn_conv_b=(ffn_conv_b, m_ffn_conv_b, v_ffn_conv_b),
                   ffn_conv_w=(ffn_conv_w, m_ffn_conv_w, v_ffn_conv_w),
                   ln_mix_g=(ln_mix_g, m_ln_mix_g, v_ln_mix_g), ln_mix_b=(ln_mix_b, m_ln_mix_b, v_ln_mix_b),
                   ln_ffn_g=(ln_ffn_g, m_ln_ffn_g, v_ln_ffn_g), ln_ffn_b=(ln_ffn_b, m_ln_ffn_b, v_ln_ffn_b))
    pk_g, sp = _pack_rows([g_small[k] for k in small_names])
    pk_w = _pack_rows([w_small[k][0] for k in small_names])[0]
    pk_m = _pack_rows([w_small[k][1] for k in small_names])[0]
    pk_v = _pack_rows([w_small[k][2] for k in small_names])[0]
    res = _adam(pk_g[None], pk_w[None], pk_m[None], pk_v[None], "adam_small")
    res = [dict(zip(small_names, _unpack_rows(r[0], sp))) for r in res]
    for k in small_names:
        out[k] = (g_small[k], res[0][k], res[1][k], res[2][k])

    order = ("fox_w_in", "fox_b_f", "fox_w_o", "swa_w_in", "swa_sinks", "swa_w_o", "ada_w", "ada_b", "ffn_w_up",
             "ffn_conv_w", "ffn_conv_b", "ffn_w_down", "ln_mix_g", "ln_mix_b", "ln_ffn_g", "ln_ffn_b")
    return (loss, grad_x[None], *[out[k][0] for k in order], *[out[k][1] for k in order],
            *[out[k][2] for k in order], *[out[k][3] for k in order])
```

```python
import functools

import jax
import jax.numpy as jnp
from jax import lax
from jax.experimental import pallas as pl
from jax.experimental.pallas import tpu as pltpu

F32 = jnp.float32
BF16 = jnp.bfloat16
MESH = pl.DeviceIdType.MESH
N_DEV = 8
AXES = ("x", "y", "c")

DEPTH = 2
ALPHA = (2.0 * DEPTH) ** 0.25
LN_EPS = 1e-5
FOX_HEAD_DIM = 128
SWA_HEAD_DIM = 64
SWA_GROUP = 8
SWA_WINDOW = 128
Q_BLOCK = 128
ROPE_DIM = 16
ROPE_THETA = 500000.0

ADAM_LR = 0.001
ADAM_B1 = 0.9
ADAM_B2 = 0.999
ADAM_EPS = 1e-08
ADAM_WD = 0.01
ADAM_STEP = 10

LANE = 128
MIB = 1024 * 1024


def _tile(n, pref, unit=LANE):
    if n <= pref:
        return n
    t = (pref // unit) * unit
    while t >= unit:
        if n % t == 0:
            return t
        t -= unit
    return n


def _params(sem, vmem_mib=48):
    return pltpu.CompilerParams(dimension_semantics=sem, vmem_limit_bytes=vmem_mib * MIB)


def _sigmoid(x):
    return 1.0 / (1.0 + jnp.exp(-x))


def _mm_call(dot, grid_mnk, in_specs, out_spec, out_shape, k_axis, nk, tm, tn, name, operands):
    sem = ("parallel",) * (len(grid_mnk) - 1) + ("arbitrary",)

    if nk == 1:
        def body(a_ref, b_ref, o_ref):
            o_ref[...] = dot(a_ref[...], b_ref[...]).astype(o_ref.dtype)
        scratch = []
    else:
        def body(a_ref, b_ref, o_ref, acc_ref):
            k = pl.program_id(k_axis)

            @pl.when(k == 0)
            def _():
                acc_ref[...] = jnp.zeros_like(acc_ref)

            acc_ref[...] += dot(a_ref[...], b_ref[...])

            @pl.when(k == nk - 1)
            def _():
                o_ref[...] = acc_ref[...].astype(o_ref.dtype)
        scratch = [pltpu.VMEM((tm, tn), F32)]

    return pl.pallas_call(
        body, name=name, grid=grid_mnk, in_specs=in_specs, out_specs=out_spec, out_shape=out_shape,
        scratch_shapes=scratch, compiler_params=_params(sem, 56),
    )(*operands)


def _dot(dims):
    def dot(a, b):
        return lax.dot_general(a.astype(BF16), b.astype(BF16), (dims, ((), ())), preferred_element_type=F32)
    return dot


def _mm_nn(a, b, out_dtype, name, tm=2048, tn=512, tk=2048):
    M, K = a.shape
    N = b.shape[1]
    tm, tn, tk = _tile(M, tm), _tile(N, tn), _tile(K, tk)
    nk = K // tk
    return _mm_call(
        _dot(((1,), (0,))), (M // tm, N // tn, nk),
        [pl.BlockSpec((tm, tk), lambda i, j, k: (i, k)), pl.BlockSpec((tk, tn), lambda i, j, k: (k, j))],
        pl.BlockSpec((tm, tn), lambda i, j, k: (i, j)), jax.ShapeDtypeStruct((M, N), out_dtype),
        2, nk, tm, tn, name, (a, b))


def _mm_nt(a, b, out_dtype, name, tm=2048, tn=512, tk=2048):
    P, M, K = a.shape
    N = b.shape[1]
    tm, tn, tk = _tile(M, tm), _tile(N, tn), _tile(K, tk)
    nk = K // tk
    return _mm_call(
        _dot(((1,), (1,))), (M // tm, N // tn, P * nk),
        [pl.BlockSpec((None, tm, tk), lambda i, j, k: (k // nk, i, k % nk)),
         pl.BlockSpec((None, tn, tk), lambda i, j, k: (k // nk, j, k % nk))],
        pl.BlockSpec((tm, tn), lambda i, j, k: (i, j)), jax.ShapeDtypeStruct((M, N), out_dtype),
        2, P * nk, tm, tn, name, (a, b))


def _mm_tn(a, b, out_dtype, name, tm=2048, tn=512, tk=2048):
    K, M = a.shape
    P, _, N = b.shape
    tm, tn, tk = _tile(M, tm), _tile(N, tn), _tile(K, tk)
    nk = K // tk
    return _mm_call(
        _dot(((0,), (0,))), (P, M // tm, N // tn, nk),
        [pl.BlockSpec((tk, tm), lambda p, i, j, k: (k, i)), pl.BlockSpec((None, tk, tn), lambda p, i, j, k: (p, k, j))],
        pl.BlockSpec((None, tm, tn), lambda p, i, j, k: (p, i, j)), jax.ShapeDtypeStruct((P, M, N), out_dtype),
        3, nk, tm, tn, name, (a, b))


ROW_TILE = 256


def _row_spec(tm, D):
    return pl.BlockSpec((tm, D), lambda i: (i, 0))


def _vec_spec(D):
    return pl.BlockSpec((1, D), lambda i: (0, 0))


def _modulate(x, sc, sh, name):
    S, D = x.shape
    tm = _tile(S, ROW_TILE, 8)

    def body(x_ref, sc_ref, sh_ref, h_ref):
        h_ref[...] = (x_ref[...] * (1.0 + sc_ref[...]) + sh_ref[...]).astype(BF16)

    return pl.pallas_call(
        body, name=name, grid=(S // tm,),
        in_specs=[_row_spec(tm, D), _vec_spec(D), _vec_spec(D)],
        out_specs=_row_spec(tm, D),
        out_shape=jax.ShapeDtypeStruct((S, D), BF16),
        compiler_params=_params(("parallel",)),
    )(x, sc, sh)


def _layer_norm_rows(z, gamma, beta):
    mu = jnp.mean(z, axis=-1, keepdims=True)
    zc = z - mu
    var = jnp.mean(zc * zc, axis=-1, keepdims=True)
    return zc * lax.rsqrt(var + LN_EPS) * gamma + beta


def _ln_fwd(x, y, gate, gamma, beta, sc_n, sh_n, after, name):
    S, D = x.shape
    tm = _tile(S, ROW_TILE, 8)

    def body(x_ref, y_ref, gate_ref, g_ref, b_ref, sc_ref, sh_ref, after_ref, z_ref, xo_ref, hn_ref):
        z = ALPHA * x_ref[...] + (1.0 + gate_ref[...]) * y_ref[...]
        xo = _layer_norm_rows(z, g_ref[...], b_ref[...])
        z_ref[...] = z
        xo_ref[...] = xo
        hn_ref[...] = (xo * (1.0 + sc_ref[...]) + sh_ref[...]).astype(BF16)

    return pl.pallas_call(
        body, name=name, grid=(S // tm,),
        in_specs=[_row_spec(tm, D), _row_spec(tm, D)] + [_vec_spec(D)] * 5 + [pl.BlockSpec(memory_space=pl.ANY)],
        out_specs=[_row_spec(tm, D)] * 3,
        out_shape=[jax.ShapeDtypeStruct((S, D), F32), jax.ShapeDtypeStruct((S, D), F32),
                   jax.ShapeDtypeStruct((S, D), BF16)],
        compiler_params=_params(("parallel",)),
    )(x, y, gate, gamma, beta, sc_n, sh_n, after)


def _ln_fwd_loss(x, y, gate, gamma, beta, target, name):
    S, D = x.shape
    tm = _tile(S, ROW_TILE, 8)

    def body(x_ref, y_ref, gate_ref, g_ref, b_ref, t_ref, z_ref, dout_ref, loss_ref):
        @pl.when(pl.program_id(0) == 0)
        def _():
            loss_ref[...] = jnp.zeros_like(loss_ref)

        z = ALPHA * x_ref[...] + (1.0 + gate_ref[...]) * y_ref[...]
        xo = _layer_norm_rows(z, g_ref[...], b_ref[...])
        err = xo - t_ref[...]
        z_ref[...] = z
        dout_ref[...] = err * (1.0 / D)
        loss_ref[...] += (0.5 / D) * jnp.sum(err * err)

    return pl.pallas_call(
        body, name=name, grid=(S // tm,),
        in_specs=[_row_spec(tm, D), _row_spec(tm, D)] + [_vec_spec(D)] * 3 + [_row_spec(tm, D)],
        out_specs=[_row_spec(tm, D), _row_spec(tm, D), pl.BlockSpec((1, LANE), lambda i: (0, 0))],
        out_shape=[jax.ShapeDtypeStruct((S, D), F32), jax.ShapeDtypeStruct((S, D), F32),
                   jax.ShapeDtypeStruct((1, LANE), F32)],
        compiler_params=_params(("arbitrary",)),
    )(x, y, gate, gamma, beta, target)


def _ln_bwd(dout, z, y, gate, gamma, name):
    S, D = z.shape
    tm = _tile(S, ROW_TILE, 8)

    def body(dout_ref, z_ref, y_ref, gate_ref, g_ref, dz_ref, dy_ref, dg_ref, db_ref, dgate_ref):
        @pl.when(pl.program_id(0) == 0)
        def _():
            dg_ref[...] = jnp.zeros_like(dg_ref)
            db_ref[...] = jnp.zeros_like(db_ref)
            dgate_ref[...] = jnp.zeros_like(dgate_ref)

        z = z_ref[...]
        dout = dout_ref[...]
        mu = jnp.mean(z, axis=-1, keepdims=True)
        zc = z - mu
        var = jnp.mean(zc * zc, axis=-1, keepdims=True)
        rstd = lax.rsqrt(var + LN_EPS)
        xhat = zc * rstd
        dxhat = dout * g_ref[...]
        m1 = jnp.mean(dxhat, axis=-1, keepdims=True)
        m2 = jnp.mean(dxhat * xhat, axis=-1, keepdims=True)
        dz = rstd * (dxhat - m1 - xhat * m2)
        dz_ref[...] = dz
        dy_ref[...] = (dz * (1.0 + gate_ref[...])).astype(BF16)
        dg_ref[...] += jnp.sum(dout * xhat, axis=0, keepdims=True)
        db_ref[...] += jnp.sum(dout, axis=0, keepdims=True)
        dgate_ref[...] += jnp.sum(dz * y_ref[...], axis=0, keepdims=True)

    return pl.pallas_call(
        body, name=name, grid=(S // tm,),
        in_specs=[_row_spec(tm, D)] * 3 + [_vec_spec(D)] * 2,
        out_specs=[_row_spec(tm, D), _row_spec(tm, D)] + [_vec_spec(D)] * 3,
        out_shape=[jax.ShapeDtypeStruct((S, D), F32), jax.ShapeDtypeStruct((S, D), BF16)]
        + [jax.ShapeDtypeStruct((1, D), F32)] * 3,
        compiler_params=_params(("arbitrary",)),
    )(dout, z, y, gate, gamma)


def _mod_bwd(dz, dh, xin, sc, after, name):
    S, D = dz.shape
    tm = _tile(S, ROW_TILE, 8)

    def body(dz_ref, dh_ref, x_ref, sc_ref, after_ref, dx_ref, dsc_ref, dsh_ref):
        @pl.when(pl.program_id(0) == 0)
        def _():
            dsc_ref[...] = jnp.zeros_like(dsc_ref)
            dsh_ref[...] = jnp.zeros_like(dsh_ref)

        dh = dh_ref[...]
        dx_ref[...] = ALPHA * dz_ref[...] + dh * (1.0 + sc_ref[...])
        dsc_ref[...] += jnp.sum(dh * x_ref[...], axis=0, keepdims=True)
        dsh_ref[...] += jnp.sum(dh, axis=0, keepdims=True)

    return pl.pallas_call(
        body, name=name, grid=(S // tm,),
        in_specs=[_row_spec(tm, D)] * 3 + [_vec_spec(D), pl.BlockSpec(memory_space=pl.ANY)],
        out_specs=[_row_spec(tm, D), _vec_spec(D), _vec_spec(D)],
        out_shape=[jax.ShapeDtypeStruct((S, D), F32)] + [jax.ShapeDtypeStruct((1, D), F32)] * 2,
        compiler_params=_params(("arbitrary",)),
    )(dz, dh, xin, sc, after)


def _shift_down(u, k, row):
    return jnp.where(row >= k, pltpu.roll(u, k, axis=0), 0.0)


def _shift_up(u, k, row, S):
    return jnp.where(row < S - k, pltpu.roll(u, S - k, axis=0), 0.0)


def _ffn_up(h, w, cw, cb, name):
    S, D = h.shape
    F = w.shape[2]
    tn = _tile(F, 256)

    def body(h_ref, w_ref, cw_ref, cb_ref, u_ref, a_ref):
        hh = h_ref[...]
        row = lax.broadcasted_iota(jnp.int32, (S, tn), 0)
        conv = []
        for p in range(2):
            u = jnp.dot(hh, w_ref[p], preferred_element_type=F32)
            u_ref[p] = u
            cwp = cw_ref[p]
            conv.append(_shift_down(u, 2, row) * cwp[0:1] + _shift_down(u, 1, row) * cwp[1:2]
                        + u * cwp[2:3] + cb_ref[p])
        g, v = conv
        a_ref[...] = (g * _sigmoid(g) * v).astype(BF16)

    return pl.pallas_call(
        body, name=name, grid=(F // tn,),
        in_specs=[pl.BlockSpec((S, D), lambda j: (0, 0)), pl.BlockSpec((2, D, tn), lambda j: (0, 0, j)),
                  pl.BlockSpec((2, 3, tn), lambda j: (0, 0, j)), pl.BlockSpec((2, 1, tn), lambda j: (0, 0, j))],
        out_specs=[pl.BlockSpec((2, S, tn), lambda j: (0, 0, j)), pl.BlockSpec((S, tn), lambda j: (0, j))],
        out_shape=[jax.ShapeDtypeStruct((2, S, F), F32), jax.ShapeDtypeStruct((S, F), BF16)],
        compiler_params=_params(("parallel",), 56),
    )(h, w, cw, cb)


def _ffn_bwd_elem(da, u, cw, cb, after, name):
    _, S, F = u.shape
    tn = _tile(F, 256)

    def body(da_ref, u_ref, cw_ref, cb_ref, after_ref, du_ref, dcw_ref, dcb_ref):
        row = lax.broadcasted_iota(jnp.int32, (S, tn), 0)
        da = da_ref[...]
        shifted, conv = [], []
        for p in range(2):
            u = u_ref[p]
            u1, u2 = _shift_down(u, 1, row), _shift_down(u, 2, row)
            cwp = cw_ref[p]
            shifted.append((u2, u1, u))
            conv.append(u2 * cwp[0:1] + u1 * cwp[1:2] + u * cwp[2:3] + cb_ref[p])
        g, v = conv
        sg = _sigmoid(g)
        d_conv = (da * v * (sg * (1.0 + g * (1.0 - sg))), da * (g * sg))
        for p in range(2):
            d = d_conv[p]
            cwp = cw_ref[p]
            dcb_ref[p] = jnp.sum(d, axis=0, keepdims=True)
            for j in range(3):
                dcw_ref[p, j:j + 1, :] = jnp.sum(d * shifted[p][j], axis=0, keepdims=True)
            du = d * cwp[2:3] + _shift_up(d, 1, row, S) * cwp[1:2] + _shift_up(d, 2, row, S) * cwp[0:1]
            du_ref[p] = du.astype(BF16)

    return pl.pallas_call(
        body, name=name, grid=(F // tn,),
        in_specs=[pl.BlockSpec((S, tn), lambda j: (0, j)), pl.BlockSpec((2, S, tn), lambda j: (0, 0, j)),
                  pl.BlockSpec((2, 3, tn), lambda j: (0, 0, j)), pl.BlockSpec((2, 1, tn), lambda j: (0, 0, j)),
                  pl.BlockSpec(memory_space=pl.ANY)],
        out_specs=[pl.BlockSpec((2, S, tn), lambda j: (0, 0, j)), pl.BlockSpec((2, 3, tn), lambda j: (0, 0, j)),
                   pl.BlockSpec((2, 1, tn), lambda j: (0, 0, j))],
        out_shape=[jax.ShapeDtypeStruct((2, S, F), BF16), jax.ShapeDtypeStruct((2, 3, F), F32),
                   jax.ShapeDtypeStruct((2, 1, F), F32)],
        compiler_params=_params(("parallel",), 56),
    )(da, u, cw, cb, after)


def _split3(x):
    hi = x.astype(BF16)
    r1 = x - hi.astype(F32)
    mid = r1.astype(BF16)
    lo = (r1 - mid.astype(F32)).astype(BF16)
    return hi, mid, lo


def _tri_matmul(x, upper, S):
    tc = _tile(S, 512)
    parts = _split3(x)
    outs = []
    for b in range(S // tc):
        r = lax.broadcasted_iota(jnp.int32, (S, tc), 0)
        c = lax.broadcasted_iota(jnp.int32, (S, tc), 1) + b * tc
        tri = jnp.where((r <= c) if upper else (r >= c), 1.0, 0.0).astype(BF16)
        acc = jnp.dot(parts[0], tri, preferred_element_type=F32)
        acc += jnp.dot(parts[1], tri, preferred_element_type=F32)
        acc += jnp.dot(parts[2], tri, preferred_element_type=F32)
        outs.append(acc)
    return outs, tc


def _fox_prep(flT, bf, name):
    H, S = flT.shape

    def body(fl_ref, b_ref, cum_ref):
        zz = fl_ref[...] + b_ref[...]
        lf = jnp.minimum(zz, 0.0) - jnp.log(1.0 + jnp.exp(-jnp.abs(zz)))
        outs, tc = _tri_matmul(lf, True, S)
        for b, o in enumerate(outs):
            cum_ref[:, b * tc:(b + 1) * tc] = o

    return pl.pallas_call(
        body, name=name,
        in_specs=[pl.BlockSpec(memory_space=pltpu.VMEM)] * 2,
        out_specs=pl.BlockSpec(memory_space=pltpu.VMEM),
        out_shape=jax.ShapeDtypeStruct((H, S), F32),
        compiler_params=pltpu.CompilerParams(vmem_limit_bytes=48 * MIB),
    )(flT, bf)


def _fox_prep_bwd(dcum_key, dcum_query, flT, bf, name):
    H, S = flT.shape

    def body(dck_ref, dcq_ref, fl_ref, b_ref, dfl_ref, dbf_ref):
        zz = fl_ref[...] + b_ref[...]
        outs, tc = _tri_matmul(dck_ref[...] + dcq_ref[...], False, S)
        total = jnp.zeros((H, 1), F32)
        for b, o in enumerate(outs):
            dfl = o * _sigmoid(-zz[:, b * tc:(b + 1) * tc])
            dfl_ref[:, b * tc:(b + 1) * tc] = dfl
            total += jnp.sum(dfl, axis=1, keepdims=True)
        dbf_ref[...] = total

    return pl.pallas_call(
        body, name=name,
        in_specs=[pl.BlockSpec(memory_space=pltpu.VMEM)] * 4,
        out_specs=[pl.BlockSpec(memory_space=pltpu.VMEM)] * 2,
        out_shape=[jax.ShapeDtypeStruct((H, S), F32), jax.ShapeDtypeStruct((H, 1), F32)],
        compiler_params=pltpu.CompilerParams(vmem_limit_bytes=48 * MIB),
    )(dcum_key, dcum_query, flT, bf)


FOX_TQ = 256
FOX_TC = 512


def _fox_chunk_scores(q, k_ref, cq_ref, ck_ref, i, c, tq, tc, scale):
    k = k_ref[c * tc:(c + 1) * tc, :].astype(BF16)
    s = lax.dot_general(q, k, (((1,), (1,)), ((), ())), preferred_element_type=F32) * scale
    s = s + cq_ref[...] - ck_ref[:, c * tc:(c + 1) * tc]
    qpos = i * tq + lax.broadcasted_iota(jnp.int32, (tq, tc), 0)
    kpos = c * tc + lax.broadcasted_iota(jnp.int32, (tq, tc), 1)
    return jnp.where(kpos <= qpos, s, -jnp.inf)


def _fox_attn_fwd(proj, cum_col, cum_row, H, name):
    S = proj.shape[0]
    dh = FOX_HEAD_DIM
    tq = _tile(S, FOX_TQ)
    scale = dh ** -0.5

    tc = _tile(S, FOX_TC)

    def body(q_ref, k_ref, v_ref, cq_ref, ck_ref, o_ref, lse_ref, m_ref, l_ref, acc_ref):
        i = pl.program_id(1)
        m_ref[...] = jnp.full_like(m_ref, -jnp.inf)
        l_ref[...] = jnp.zeros_like(l_ref)
        acc_ref[...] = jnp.zeros_like(acc_ref)
        q = q_ref[...].astype(BF16)
        for c in range(S // tc):
            @pl.when(c * tc <= i * tq + tq - 1)
            def _():
                s = _fox_chunk_scores(q, k_ref, cq_ref, ck_ref, i, c, tq, tc, scale)
                m_old = m_ref[...]
                m_new = jnp.maximum(m_old, jnp.max(s, axis=-1, keepdims=True))
                alpha = jnp.exp(m_old - m_new)
                p = jnp.exp(s - m_new)
                l_ref[...] = alpha * l_ref[...] + jnp.sum(p, axis=-1, keepdims=True)
                v = v_ref[c * tc:(c + 1) * tc, :].astype(BF16)
                acc_ref[...] = alpha * acc_ref[...] + jnp.dot(p.astype(BF16), v, preferred_element_type=F32)
                m_ref[...] = m_new
        o_ref[...] = (acc_ref[...] / l_ref[...]).astype(BF16)
        lse_ref[...] = m_ref[...] + jnp.log(l_ref[...])

    return pl.pallas_call(
        body, name=name, grid=(H, S // tq),
        in_specs=[pl.BlockSpec((tq, dh), lambda h, i: (i, h)),
                  pl.BlockSpec((S, dh), lambda h, i: (0, H + h)),
                  pl.BlockSpec((S, dh), lambda h, i: (0, 2 * H + h)),
                  pl.BlockSpec((None, tq, 1), lambda h, i: (h, i, 0)),
                  pl.BlockSpec((None, 1, S), lambda h, i: (h, 0, 0))],
        out_specs=[pl.BlockSpec((tq, dh), lambda h, i: (i, h)),
                   pl.BlockSpec((None, tq, 1), lambda h, i: (h, i, 0))],
        out_shape=[jax.ShapeDtypeStruct((S, H * dh), BF16), jax.ShapeDtypeStruct((H, S, 1), F32)],
        scratch_shapes=[pltpu.VMEM((tq, 1), F32), pltpu.VMEM((tq, 1), F32), pltpu.VMEM((tq, dh), F32)],
        compiler_params=_params(("parallel", "parallel")),
    )(proj, proj, proj, cum_col, cum_row)


def _fox_attn_bwd(proj, do, o, cum_col, cum_row, lse, H, after, name):
    S = proj.shape[0]
    dh = FOX_HEAD_DIM
    tq = _tile(S, FOX_TQ)
    tc = _tile(S, FOX_TC)
    scale = dh ** -0.5

    def body(q_ref, k_ref, v_ref, do_ref, o_ref, cq_ref, ck_ref, lse_ref, after_ref,
             dq_ref, dk_ref, dv_ref, dck_ref, dcq_ref, acc_ref):
        i = pl.program_id(1)

        @pl.when(i == 0)
        def _():
            dk_ref[...] = jnp.zeros_like(dk_ref)
            dv_ref[...] = jnp.zeros_like(dv_ref)
            dck_ref[...] = jnp.zeros_like(dck_ref)

        acc_ref[...] = jnp.zeros_like(acc_ref)
        dcq_ref[...] = jnp.zeros_like(dcq_ref)
        q = q_ref[...].astype(BF16)
        do_f = do_ref[...]
        do_b = do_f.astype(BF16)
        delta = jnp.sum(do_f * o_ref[...].astype(F32), axis=-1, keepdims=True)
        lse_q = lse_ref[...]
        for c in range(S // tc):
            @pl.when(c * tc <= i * tq + tq - 1)
            def _():
                rows = slice(c * tc, (c + 1) * tc)
                p = jnp.exp(_fox_chunk_scores(q, k_ref, cq_ref, ck_ref, i, c, tq, tc, scale) - lse_q)
                dp = lax.dot_general(do_b, v_ref[rows, :].astype(BF16), (((1,), (1,)), ((), ())),
                                     preferred_element_type=F32)
                ds = p * (dp - delta)
                ds_b = ds.astype(BF16)
                acc_ref[...] += jnp.dot(ds_b, k_ref[rows, :].astype(BF16), preferred_element_type=F32)
                dk_ref[rows, :] += lax.dot_general(ds_b, q, (((0,), (0,)), ((), ())), preferred_element_type=F32) * scale
                dv_ref[rows, :] += lax.dot_general(p.astype(BF16), do_b, (((0,), (0,)), ((), ())),
                                                   preferred_element_type=F32)
                dck_ref[:, rows] -= jnp.sum(ds, axis=0, keepdims=True)
                dcq_ref[...] += jnp.sum(ds, axis=-1, keepdims=True)
        dq_ref[...] = (acc_ref[...] * scale).astype(BF16)

    W = H * dh
    return pl.pallas_call(
        body, name=name, grid=(H, S // tq),
        in_specs=[pl.BlockSpec((tq, dh), lambda h, i: (i, h)),
                  pl.BlockSpec((S, dh), lambda h, i: (0, H + h)),
                  pl.BlockSpec((S, dh), lambda h, i: (0, 2 * H + h)),
                  pl.BlockSpec((tq, dh), lambda h, i: (i, h)),
                  pl.BlockSpec((tq, dh), lambda h, i: (i, h)),
                  pl.BlockSpec((None, tq, 1), lambda h, i: (h, i, 0)),
                  pl.BlockSpec((None, 1, S), lambda h, i: (h, 0, 0)),
                  pl.BlockSpec((None, tq, 1), lambda h, i: (h, i, 0)),
                  pl.BlockSpec(memory_space=pl.ANY)],
        out_specs=[pl.BlockSpec((tq, dh), lambda h, i: (i, h)),
                   pl.BlockSpec((S, dh), lambda h, i: (0, h)),
                   pl.BlockSpec((S, dh), lambda h, i: (0, h)),
                   pl.BlockSpec((None, 1, S), lambda h, i: (h, 0, 0)),
                   pl.BlockSpec((None, tq, 1), lambda h, i: (h, i, 0))],
        out_shape=[jax.ShapeDtypeStruct((S, W), BF16), jax.ShapeDtypeStruct((S, W), F32),
                   jax.ShapeDtypeStruct((S, W), F32), jax.ShapeDtypeStruct((H, 1, S), F32),
                   jax.ShapeDtypeStruct((H, S, 1), F32)],
        scratch_shapes=[pltpu.VMEM((tq, dh), F32)],
        compiler_params=_params(("parallel", "arbitrary")),
    )(proj, proj, proj, do, o, cum_col, cum_row, lse, after)


def _rope(parts, tabs, out_dtype, name):
    S = parts[0][0].shape[0]
    widths = [a.shape[1] for a, _ in parts]
    total = sum(widths)
    tm = _tile(S, ROW_TILE, 8)
    flags = [r for _, r in parts]

    def body(*refs):
        in_refs = refs[:len(parts)]
        cos_ref, sa_ref, sb_ref, o_ref = refs[len(parts):]
        cos, sa, sb = cos_ref[...], sa_ref[...], sb_ref[...]
        off = 0
        for ref, rot, w in zip(in_refs, flags, widths):
            for j in range(w // LANE):
                t = ref[:, j * LANE:(j + 1) * LANE]
                if rot:
                    t = t * cos + pltpu.roll(t, LANE - ROPE_DIM // 2, axis=1) * sa + pltpu.roll(t, ROPE_DIM // 2, axis=1) * sb
                o_ref[:, off + j * LANE:off + (j + 1) * LANE] = t.astype(o_ref.dtype)
            off += w

    return pl.pallas_call(
        body, name=name, grid=(S // tm,),
        in_specs=[pl.BlockSpec((tm, w), lambda i: (i, 0)) for w in widths] + [_row_spec(tm, LANE)] * 3,
        out_specs=_row_spec(tm, total),
        out_shape=jax.ShapeDtypeStruct((S, total), out_dtype),
        compiler_params=_params(("parallel",)),
    )(*[a for a, _ in parts], *tabs)


def _swa_band(ref_p, ref_c, hk):
    dh = SWA_HEAD_DIM
    return jnp.concatenate([ref_p[:, hk * dh:(hk + 1) * dh], ref_c[:, hk * dh:(hk + 1) * dh]], axis=0).astype(BF16)


def _swa_mask(n, G):
    qi = lax.broadcasted_iota(jnp.int32, (G * Q_BLOCK, 2 * Q_BLOCK), 0) % Q_BLOCK
    kj = lax.broadcasted_iota(jnp.int32, (G * Q_BLOCK, 2 * Q_BLOCK), 1)
    rel = qi + Q_BLOCK - kj
    return (rel >= 0) & (rel < SWA_WINDOW) & ((kj >= Q_BLOCK) | (n > 0))


def _swa_stack(ref, hk, G):
    dh = SWA_HEAD_DIM
    return jnp.concatenate([ref[:, (hk * G + g) * dh:(hk * G + g + 1) * dh] for g in range(G)], axis=0)


def _swa_unstack(ref, stacked, hk, G):
    dh, QB = SWA_HEAD_DIM, Q_BLOCK
    for g in range(0, G, 2):
        c0 = (hk * G + g) * dh
        pair = jnp.concatenate([stacked[g * QB:(g + 1) * QB], stacked[(g + 1) * QB:(g + 2) * QB]], axis=1)
        ref[:, c0:c0 + 2 * dh] = pair.astype(ref.dtype)


def _swa_sink_rows(sink_ref, hk, G):
    return jnp.concatenate([jnp.broadcast_to(sink_ref[0:1, hk * G + g:hk * G + g + 1], (Q_BLOCK, 1)) for g in range(G)],
                           axis=0)


def _swa_attn_fwd(qk, proj, sinks, Hq, name):
    S = qk.shape[0]
    dh, G, QB = SWA_HEAD_DIM, SWA_GROUP, Q_BLOCK
    Hk = Hq // G
    Wq, Wk = Hq * dh, Hk * dh
    nb = S // QB
    scale = dh ** -0.5

    def body(q_ref, kp_ref, kc_ref, vp_ref, vc_ref, sink_ref, o_ref, lse_ref):
        n = pl.program_id(0)
        mask = _swa_mask(n, G)
        lane = lax.broadcasted_iota(jnp.int32, (QB, LANE), 1)
        lse_tile = jnp.zeros((QB, LANE), F32)
        for hk in range(Hk):
            kb = _swa_band(kp_ref, kc_ref, hk)
            vb = _swa_band(vp_ref, vc_ref, hk)
            q = _swa_stack(q_ref, hk, G).astype(BF16)
            sk = _swa_sink_rows(sink_ref, hk, G)
            s = lax.dot_general(q, kb, (((1,), (1,)), ((), ())), preferred_element_type=F32) * scale
            s = jnp.where(mask, s, -jnp.inf)
            m = jnp.maximum(jnp.max(s, axis=-1, keepdims=True), sk)
            p = jnp.exp(s - m)
            l = jnp.sum(p, axis=-1, keepdims=True) + jnp.exp(sk - m)
            o = jnp.dot(p.astype(BF16), vb, preferred_element_type=F32) / l
            lse = m + jnp.log(l)
            for g in range(G):
                lse_tile = jnp.where(lane == hk * G + g, lse[g * QB:(g + 1) * QB], lse_tile)
            _swa_unstack(o_ref, o, hk, G)
        lse_ref[...] = lse_tile

    kcol, vcol = Wq // Wk, (Wq + Wk) // Wk
    return pl.pallas_call(
        body, name=name, grid=(nb,),
        in_specs=[pl.BlockSpec((QB, Wq), lambda n: (n, 0)),
                  pl.BlockSpec((QB, Wk), lambda n: (jnp.maximum(n - 1, 0), kcol)),
                  pl.BlockSpec((QB, Wk), lambda n: (n, kcol)),
                  pl.BlockSpec((QB, Wk), lambda n: (jnp.maximum(n - 1, 0), vcol)),
                  pl.BlockSpec((QB, Wk), lambda n: (n, vcol)),
                  pl.BlockSpec((1, LANE), lambda n: (0, 0))],
        out_specs=[pl.BlockSpec((QB, Wq), lambda n: (n, 0)), pl.BlockSpec((QB, LANE), lambda n: (n, 0))],
        out_shape=[jax.ShapeDtypeStruct((S, Wq), BF16), jax.ShapeDtypeStruct((S, LANE), F32)],
        compiler_params=_params(("parallel",)),
    )(qk, qk, qk, proj, proj, sinks)


def _swa_attn_bwd(qk, proj, sinks, do, lse, Hq, after, name):
    S = qk.shape[0]
    dh, G, QB = SWA_HEAD_DIM, SWA_GROUP, Q_BLOCK
    Hk = Hq // G
    Wq, Wk = Hq * dh, Hk * dh
    nb = S // QB
    scale = dh ** -0.5

    def body(q_ref, kp_ref, kc_ref, vp_ref, vc_ref, sink_ref, do_ref, lse_ref, after_ref,
             dq_ref, dk_ref, dv_ref, dsink_ref, carry_k, carry_v):
        n = pl.program_id(0)

        @pl.when(n == 0)
        def _():
            dsink_ref[...] = jnp.zeros_like(dsink_ref)

        @pl.when(n < nb)
        def _():
            mask = _swa_mask(n, G)
            lane = lax.broadcasted_iota(jnp.int32, (1, LANE), 1)
            dsink = jnp.zeros((1, LANE), F32)
            dk_heads, dv_heads = [], []
            for hk in range(Hk):
                kb = _swa_band(kp_ref, kc_ref, hk)
                vb = _swa_band(vp_ref, vc_ref, hk)
                q = _swa_stack(q_ref, hk, G).astype(BF16)
                do_s = _swa_stack(do_ref, hk, G).astype(BF16)
                lse = jnp.concatenate([lse_ref[:, hk * G + g:hk * G + g + 1] for g in range(G)], axis=0)
                s = lax.dot_general(q, kb, (((1,), (1,)), ((), ())), preferred_element_type=F32) * scale
                s = jnp.where(mask, s, -jnp.inf)
                p = jnp.exp(s - lse)
                p_sink = jnp.exp(_swa_sink_rows(sink_ref, hk, G) - lse)
                dp = lax.dot_general(do_s, vb, (((1,), (1,)), ((), ())), preferred_element_type=F32)
                delta = jnp.sum(p * dp, axis=-1, keepdims=True)
                ds_b = (p * (dp - delta)).astype(BF16)
                _swa_unstack(dq_ref, jnp.dot(ds_b, kb, preferred_element_type=F32) * scale, hk, G)
                dk_heads.append(lax.dot_general(ds_b, q, (((0,), (0,)), ((), ())), preferred_element_type=F32) * scale)
                dv_heads.append(lax.dot_general(p.astype(BF16), do_s, (((0,), (0,)), ((), ())), preferred_element_type=F32))
                sink_term = p_sink * delta
                for g in range(G):
                    dsink = jnp.where(lane == hk * G + g,
                                      -jnp.sum(sink_term[g * QB:(g + 1) * QB], axis=0, keepdims=True), dsink)
            dsink_ref[...] += dsink
            dk_all = jnp.concatenate(dk_heads, axis=1)
            dv_all = jnp.concatenate(dv_heads, axis=1)

            @pl.when(n > 0)
            def _():
                dk_ref[...] = carry_k[...] + dk_all[:QB]
                dv_ref[...] = carry_v[...] + dv_all[:QB]

            carry_k[...] = dk_all[QB:]
            carry_v[...] = dv_all[QB:]

        @pl.when(n == nb)
        def _():
            dk_ref[...] = carry_k[...]
            dv_ref[...] = carry_v[...]

    kcol, vcol = Wq // Wk, (Wq + Wk) // Wk
    cur = lambda n: jnp.minimum(n, nb - 1)
    prev = lambda n: jnp.maximum(jnp.minimum(n, nb - 1) - 1, 0)
    return pl.pallas_call(
        body, name=name, grid=(nb + 1,),
        in_specs=[pl.BlockSpec((QB, Wq), lambda n: (cur(n), 0)),
                  pl.BlockSpec((QB, Wk), lambda n: (prev(n), kcol)),
                  pl.BlockSpec((QB, Wk), lambda n: (cur(n), kcol)),
                  pl.BlockSpec((QB, Wk), lambda n: (prev(n), vcol)),
                  pl.BlockSpec((QB, Wk), lambda n: (cur(n), vcol)),
                  pl.BlockSpec((1, LANE), lambda n: (0, 0)),
                  pl.BlockSpec((QB, Wq), lambda n: (cur(n), 0)),
                  pl.BlockSpec((QB, LANE), lambda n: (cur(n), 0)),
                  pl.BlockSpec(memory_space=pl.ANY)],
        out_specs=[pl.BlockSpec((QB, Wq), lambda n: (cur(n), 0)),
                   pl.BlockSpec((QB, Wk), lambda n: (jnp.maximum(n - 1, 0), 0)),
                   pl.BlockSpec((QB, Wk), lambda n: (jnp.maximum(n - 1, 0), 0)),
                   pl.BlockSpec((1, LANE), lambda n: (0, 0))],
        out_shape=[jax.ShapeDtypeStruct((S, Wq), F32), jax.ShapeDtypeStruct((S, Wk), F32),
                   jax.ShapeDtypeStruct((S, Wk), F32), jax.ShapeDtypeStruct((1, LANE), F32)],
        scratch_shapes=[pltpu.VMEM((QB, Wk), F32), pltpu.VMEM((QB, Wk), F32)],
        compiler_params=_params(("arbitrary",)),
    )(qk, qk, qk, proj, proj, sinks, do, lse, after)


ADA_ROWS = 16


def _ada_mod(c_pad, w, b, name):
    L, D, N = w.shape
    tn = _tile(N, 512)

    def body(c_ref, w_ref, b_ref, o_ref):
        c = c_ref[...]
        c = c * _sigmoid(c)
        ch = c.astype(BF16)
        cl = (c - ch.astype(F32)).astype(BF16)
        ww = w_ref[...]
        wh = ww.astype(BF16)
        wl = (ww - wh.astype(F32)).astype(BF16)
        acc = jnp.dot(ch, wh, preferred_element_type=F32)
        acc += jnp.dot(ch, wl, preferred_element_type=F32)
        acc += jnp.dot(cl, wh, preferred_element_type=F32)
        o_ref[...] = acc + b_ref[...]

    return pl.pallas_call(
        body, name=name, grid=(L, N // tn),
        in_specs=[pl.BlockSpec((ADA_ROWS, D), lambda l, j: (0, 0)),
                  pl.BlockSpec((None, D, tn), lambda l, j: (l, 0, j)),
                  pl.BlockSpec((None, 1, tn), lambda l, j: (l, 0, j))],
        out_specs=pl.BlockSpec((None, ADA_ROWS, tn), lambda l, j: (l, 0, j)),
        out_shape=jax.ShapeDtypeStruct((L, ADA_ROWS, N), F32),
        compiler_params=_params(("parallel", "parallel")),
    )(c_pad, w, b)


def _ada_bwd(cT, dm, name):
    D = cT.shape[0]
    L, B, N = dm.shape
    tm = _tile(D, 256)

    def body(c_ref, dm_ref, o_ref):
        c = c_ref[...]
        c = c * _sigmoid(c)
        dmv = dm_ref[...]
        acc = c[:, 0:1] * dmv[0:1, :]
        for b in range(1, B):
            acc += c[:, b:b + 1] * dmv[b:b + 1, :]
        o_ref[...] = acc

    return pl.pallas_call(
        body, name=name, grid=(L, D // tm),
        in_specs=[pl.BlockSpec((tm, B), lambda l, i: (i, 0)), pl.BlockSpec((None, B, N), lambda l, i: (l, 0, 0))],
        out_specs=pl.BlockSpec((None, tm, N), lambda l, i: (l, i, 0)),
        out_shape=jax.ShapeDtypeStruct((L, D, N), F32),
        compiler_params=_params(("parallel", "parallel")),
    )(cT, dm)


def _adamw_math(w, g, m, v):
    m = ADAM_B1 * m + (1.0 - ADAM_B1) * g
    v = ADAM_B2 * v + (1.0 - ADAM_B2) * (g * g)
    m_hat = m / (1.0 - ADAM_B1 ** ADAM_STEP)
    v_hat = v / (1.0 - ADAM_B2 ** ADAM_STEP)
    delta = -ADAM_LR * (m_hat / (jnp.sqrt(v_hat) + ADAM_EPS) + ADAM_WD * w)
    return delta, m, v


def _adam_rows(R, C):
    lanes = -(-C // LANE) * LANE
    return _tile(R, max(8, (262144 // lanes) // 8 * 8), 8)


def _adam_sum(recv, w, m, v, layer, filled, name):
    P, R, C = recv.shape
    L = w.shape[0]
    tr = _adam_rows(R, C)
    n_keep = 0 if filled is None else 4

    def body(r_ref, w_ref, m_ref, v_ref, *rest):
        g_ref, d_ref, mo_ref, vo_ref = rest[n_keep:]
        g = r_ref[0].astype(F32)
        for p in range(1, P):
            g = g + r_ref[p].astype(F32)
        delta, mn, vn = _adamw_math(w_ref[...], g, m_ref[...], v_ref[...])
        g_ref[...] = g
        d_ref[...] = delta
        mo_ref[...] = mn
        vo_ref[...] = vn

    spec = pl.BlockSpec((None, tr, C), lambda i: (layer, i, 0))
    return pl.pallas_call(
        body, name=name, grid=(R // tr,),
        in_specs=[pl.BlockSpec((P, tr, C), lambda i: (0, i, 0)), spec, spec, spec]
        + [pl.BlockSpec(memory_space=pl.ANY)] * n_keep,
        out_specs=[spec] * 4,
        out_shape=[jax.ShapeDtypeStruct((L, R, C), F32)] * 4,
        input_output_aliases={4 + k: k for k in range(n_keep)},
        compiler_params=_params(("parallel",)),
    )(recv, w, m, v, *(filled or ()))


def _adam(g, w, m, v, name):
    L, R, C = w.shape
    tr = _adam_rows(R, C)

    def body(g_ref, w_ref, m_ref, v_ref, d_ref, mo_ref, vo_ref):
        delta, mn, vn = _adamw_math(w_ref[...], g_ref[...], m_ref[...], v_ref[...])
        d_ref[...] = delta
        mo_ref[...] = mn
        vo_ref[...] = vn

    spec = pl.BlockSpec((None, tr, C), lambda l, i: (l, i, 0))
    return pl.pallas_call(
        body, name=name, grid=(L, R // tr),
        in_specs=[spec] * 4, out_specs=[spec] * 3,
        out_shape=[jax.ShapeDtypeStruct((L, R, C), F32)] * 3,
        compiler_params=_params(("parallel", "parallel")),
    )(g, w, m, v)


def _sum_slots(x, after, name):
    P, R, C = x.shape

    def body(x_ref, after_ref, o_ref):
        acc = x_ref[0]
        for p in range(1, P):
            acc = acc + x_ref[p]
        o_ref[...] = acc

    return pl.pallas_call(
        body, name=name,
        in_specs=[pl.BlockSpec(memory_space=pltpu.VMEM), pl.BlockSpec(memory_space=pl.ANY)],
        out_specs=pl.BlockSpec(memory_space=pltpu.VMEM),
        out_shape=jax.ShapeDtypeStruct((R, C), F32),
        compiler_params=pltpu.CompilerParams(vmem_limit_bytes=48 * MIB),
    )(x, after)


def _my_pos():
    return lax.axis_index("x"), lax.axis_index("y"), lax.axis_index("c")


def _all_gather_small(x, name, after=()):
    R, C = x.shape
    n_after = len(after)

    def body(x_ref, *rest):
        out_ref, send_sems, recv_sems = rest[n_after:]
        x_, y_, c_ = _my_pos()
        me, sibling = (x_, y_, c_), (x_, y_, 1 - c_)
        chips = [(1 - x_, y_), (x_, 1 - y_), (1 - x_, 1 - y_)]

        def slot(px, py, pc):
            return out_ref.at[4 * px + 2 * py + pc]

        def copy(k, block, to):
            return pltpu.make_async_remote_copy(
                src_ref=slot(*block), dst_ref=slot(*block), send_sem=send_sems.at[k], recv_sem=recv_sems.at[k],
                device_id=to, device_id_type=MESH)

        out_ref[4 * x_ + 2 * y_ + c_] = x_ref[...]
        first = [copy(0, me, sibling)] + [copy(1 + j, me, (*chip, c_)) for j, chip in enumerate(chips)]
        for cp in first:
            cp.start()
        passed = [copy(4 + j, (*chip, c_), sibling) for j, chip in enumerate(chips)]
        for j, chip in enumerate(chips):
            copy(1 + j, (*chip, c_), me).wait_recv()
            passed[j].start()
        copy(0, sibling, me).wait_recv()
        for j, chip in enumerate(chips):
            copy(4 + j, (*chip, 1 - c_), me).wait_recv()
        for cp in first + passed:
            cp.wait_send()

    return pl.pallas_call(
        body, name=name,
        in_specs=[pl.BlockSpec(memory_space=pltpu.VMEM)] + [pl.BlockSpec(memory_space=pl.ANY)] * n_after,
        out_specs=pl.BlockSpec(memory_space=pltpu.VMEM),
        out_shape=jax.ShapeDtypeStruct((N_DEV, R, C), x.dtype),
        scratch_shapes=[pltpu.SemaphoreType.DMA((7,)), pltpu.SemaphoreType.DMA((7,))],
        compiler_params=pltpu.CompilerParams(vmem_limit_bytes=48 * MIB),
    )(x, *after)


HBM_SPEC = pl.BlockSpec(memory_space=pltpu.HBM)
SEM_SPEC = pl.BlockSpec(memory_space=pltpu.SEMAPHORE)
ANY_SPEC = pl.BlockSpec(memory_space=pl.ANY)
SPLIT_EFFECT = pltpu.SideEffectType.DATAFLOW_SIDE_EFFECTING


def _in_hbm(a):
    return pltpu.with_memory_space_constraint(a, pltpu.HBM)


def _gathered_shape(a, kind):
    if kind == "major":
        return (N_DEV,) + a.shape
    if kind == "rows":
        return (N_DEV * a.shape[0], a.shape[1])
    return (2, a.shape[0], 4 * a.shape[1])


def _gather_slot(ref, kind, block, shard_shape):
    px, py, pc = block
    if kind == "major":
        return ref.at[4 * px + 2 * py + pc]
    if kind == "rows":
        r = shard_shape[0]
        return ref.at[pl.ds(pl.multiple_of((4 * px + 2 * py + pc) * r, r), r), :]
    cu = shard_shape[1]
    return ref.at[px, :, pl.ds(pl.multiple_of((2 * py + pc) * cu, cu), cu)]


def _gather_peers():
    x_, y_, c_ = _my_pos()
    return (x_, y_, c_), (x_, y_, 1 - c_), [(1 - x_, y_), (x_, 1 - y_), (1 - x_, 1 - y_)]


def _gather_start(shards, kinds, after, name):
    n = len(shards)
    bufs = [lax.empty(_gathered_shape(a, k), a.dtype) for a, k in zip(shards, kinds)]
    extra = [] if after is None else [after]

    def body(*refs):
        shard_refs, buf_refs = refs[:n], refs[n:2 * n]
        send_sems, recv_sems, local_sems = refs[2 * n + len(extra):2 * n + len(extra) + 3]
        token = refs[-1]
        me, sibling, chips = _gather_peers()
        for e in range(n):
            mine = _gather_slot(buf_refs[e], kinds[e], me, shards[e].shape)
            pltpu.make_async_copy(shard_refs[e], mine, local_sems.at[e]).start()
            for k, to in enumerate([sibling] + [(*chip, me[2]) for chip in chips]):
                pltpu.make_async_remote_copy(
                    src_ref=shard_refs[e], dst_ref=mine, send_sem=send_sems.at[4 * e + k],
                    recv_sem=recv_sems.at[4 * e + k], device_id=to, device_id_type=MESH).start()
        token[...] = jnp.zeros_like(token)

    out = pl.pallas_call(
        body, name=name,
        out_shape=(pltpu.SemaphoreType.DMA((4 * n,)), pltpu.SemaphoreType.DMA((4 * n,)), pltpu.SemaphoreType.DMA((n,)),
                   *[pltpu.HBM(a.shape, a.dtype) for a in shards], *[pltpu.HBM(a.shape, a.dtype) for a in bufs],
                   jax.ShapeDtypeStruct((8, LANE), F32)),
        in_specs=[HBM_SPEC] * (2 * n) + [ANY_SPEC] * len(extra),
        out_specs=(SEM_SPEC, SEM_SPEC, SEM_SPEC, *[HBM_SPEC] * (2 * n), pl.BlockSpec(memory_space=pltpu.VMEM)),
        input_output_aliases={i: 3 + i for i in range(2 * n)},
        compiler_params=pltpu.CompilerParams(has_side_effects=SPLIT_EFFECT),
    )(*[_in_hbm(a) for a in shards], *[_in_hbm(a) for a in bufs], *extra)
    return out[0], out[1], out[2], out[3:3 + n], out[3 + n:3 + 2 * n], out[-1]


def _gather_forward(recv_sems, bufs, kinds, shard_shapes, after, name):
    n = len(bufs)

    def body(*refs):
        buf_refs, recv_in = refs[:n], refs[n]
        fsend, frecv = refs[n + 2], refs[n + 3]
        token = refs[-1]
        me, sibling, chips = _gather_peers()
        for e in range(n):
            for j, chip in enumerate(chips):
                slot = _gather_slot(buf_refs[e], kinds[e], (*chip, me[2]), shard_shapes[e])
                pltpu.make_async_remote_copy(
                    src_ref=slot, dst_ref=slot, send_sem=recv_in.at[4 * e + 1 + j], recv_sem=recv_in.at[4 * e + 1 + j],
                    device_id=me, device_id_type=MESH).wait_recv()
                pltpu.make_async_remote_copy(
                    src_ref=slot, dst_ref=slot, send_sem=fsend.at[3 * e + j], recv_sem=frecv.at[3 * e + j],
                    device_id=sibling, device_id_type=MESH).start()
        token[...] = jnp.zeros_like(token)

    out = pl.pallas_call(
        body, name=name,
        out_shape=(pltpu.SemaphoreType.DMA((3 * n,)), pltpu.SemaphoreType.DMA((3 * n,)),
                   *[pltpu.HBM(a.shape, a.dtype) for a in bufs], jax.ShapeDtypeStruct((8, LANE), F32)),
        in_specs=[HBM_SPEC] * n + [SEM_SPEC, ANY_SPEC],
        out_specs=(SEM_SPEC, SEM_SPEC, *[HBM_SPEC] * n, pl.BlockSpec(memory_space=pltpu.VMEM)),
        input_output_aliases={i: 2 + i for i in range(n)},
        compiler_params=pltpu.CompilerParams(has_side_effects=SPLIT_EFFECT),
    )(*bufs, recv_sems, after)
    return out[0], out[1], out[2:2 + n], out[-1]


def _gather_wait(send_sems, recv_sems, local_sems, fsend, frecv, shards, bufs, kinds, after, name):
    n = len(bufs)

    def body(*refs):
        shard_refs, buf_refs = refs[:n], refs[n:2 * n]
        send_in, recv_in, local_in, fsend_in, frecv_in = refs[2 * n:2 * n + 5]
        me, sibling, chips = _gather_peers()

        def arrival(slot, sem):
            return pltpu.make_async_remote_copy(src_ref=slot, dst_ref=slot, send_sem=sem, recv_sem=sem,
                                                device_id=me, device_id_type=MESH)

        for e in range(n):
            shape = shards[e].shape
            mine = _gather_slot(buf_refs[e], kinds[e], me, shape)
            pltpu.make_async_copy(shard_refs[e], mine, local_in.at[e]).wait()
            arrival(_gather_slot(buf_refs[e], kinds[e], sibling, shape), recv_in.at[4 * e]).wait_recv()
            for j, chip in enumerate(chips):
                arrival(_gather_slot(buf_refs[e], kinds[e], (*chip, 1 - me[2]), shape), frecv_in.at[3 * e + j]).wait_recv()
            for k in range(4):
                arrival(mine, send_in.at[4 * e + k]).wait_send()
            for j in range(3):
                arrival(mine, fsend_in.at[3 * e + j]).wait_send()

    out = pl.pallas_call(
        body, name=name,
        out_shape=(*[pltpu.HBM(a.shape, a.dtype) for a in shards], *[pltpu.HBM(a.shape, a.dtype) for a in bufs]),
        in_specs=[HBM_SPEC] * (2 * n) + [SEM_SPEC] * 5 + [ANY_SPEC],
        out_specs=tuple([HBM_SPEC] * (2 * n)),
        input_output_aliases={i: i for i in range(2 * n)},
        compiler_params=pltpu.CompilerParams(has_side_effects=SPLIT_EFFECT),
    )(*shards, *bufs, send_sems, recv_sems, local_sems, fsend, frecv, after)
    return out[n:]


def _grad_slice(ref, kind, j):
    if kind == "major":
        return ref.at[j]
    if kind == "rows":
        r = ref.shape[0] // N_DEV
        return ref.at[pl.ds(j * r, r), :]
    cu = ref.shape[2] // 4
    return ref.at[j // 4, :, pl.ds((j % 4) * cu, cu)]


def _slice_shape(a, kind):
    if kind == "major":
        return a.shape[1:]
    if kind == "rows":
        return (a.shape[0] // N_DEV, a.shape[1])
    return (a.shape[1], a.shape[2] // 4)


def _scatter_copies(srcs, lands, kinds, send_sems, recv_sems):
    x_, y_, c_ = _my_pos()
    me = 4 * x_ + 2 * y_ + c_
    n = len(srcs)

    def remote(e, j):
        return pltpu.make_async_remote_copy(
            src_ref=_grad_slice(srcs[e], kinds[e], j), dst_ref=lands[e].at[me],
            send_sem=send_sems.at[e * N_DEV + j], recv_sem=recv_sems.at[e * N_DEV + me],
            device_id=(j // 4, (j // 2) % 2, j % 2), device_id_type=MESH)

    def local(e, j):
        return pltpu.make_async_copy(_grad_slice(srcs[e], kinds[e], j), lands[e].at[j], recv_sems.at[e * N_DEV + j])

    def arrival(e, i):
        return pltpu.make_async_remote_copy(
            src_ref=_grad_slice(srcs[e], kinds[e], i), dst_ref=lands[e].at[i],
            send_sem=send_sems.at[e * N_DEV + i], recv_sem=recv_sems.at[e * N_DEV + i],
            device_id=(i // 4, (i // 2) % 2, i % 2), device_id_type=MESH)

    def start():
        for e in range(n):
            for j in range(N_DEV):
                @pl.when(me == j)
                def _():
                    local(e, j).start()

                @pl.when(me != j)
                def _():
                    remote(e, j).start()

    def wait():
        for e in range(n):
            for i in range(N_DEV):
                @pl.when(me == i)
                def _():
                    local(e, i).wait()

                @pl.when(me != i)
                def _():
                    arrival(e, i).wait_recv()
        for e in range(n):
            for j in range(N_DEV):
                @pl.when(me != j)
                def _():
                    remote(e, j).wait_send()

    return start, wait


def _scatter_start(srcs, kinds, after, name):
    n = len(srcs)
    lands = [lax.empty((N_DEV,) + _slice_shape(a, k), a.dtype) for a, k in zip(srcs, kinds)]
    extra = [] if after is None else [after]

    def body(*refs):
        src_refs, land_refs = refs[:n], refs[n:2 * n]
        send_sems, recv_sems = refs[2 * n + len(extra)], refs[2 * n + len(extra) + 1]
        token = refs[-1]
        start, _ = _scatter_copies(src_refs, land_refs, kinds, send_sems, recv_sems)
        start()
        token[...] = jnp.zeros_like(token)

    out = pl.pallas_call(
        body, name=name,
        out_shape=(pltpu.SemaphoreType.DMA((n * N_DEV,)), pltpu.SemaphoreType.DMA((n * N_DEV,)),
                   *[pltpu.HBM(a.shape, a.dtype) for a in srcs], *[pltpu.HBM(a.shape, a.dtype) for a in lands],
                   jax.ShapeDtypeStruct((8, LANE), F32)),
        in_specs=[HBM_SPEC] * (2 * n) + [ANY_SPEC] * len(extra),
        out_specs=(SEM_SPEC, SEM_SPEC, *[HBM_SPEC] * (2 * n), pl.BlockSpec(memory_space=pltpu.VMEM)),
        input_output_aliases={i: 2 + i for i in range(2 * n)},
        compiler_params=pltpu.CompilerParams(has_side_effects=SPLIT_EFFECT),
    )(*[_in_hbm(a) for a in srcs], *[_in_hbm(a) for a in lands], *extra)
    return out[0], out[1], out[2:2 + n], out[2 + n:2 + 2 * n], out[-1]


def _scatter_wait(send_sems, recv_sems, srcs, lands, kinds, after, name):
    n = len(srcs)

    def body(*refs):
        src_refs, land_refs = refs[:n], refs[n:2 * n]
        _, wait = _scatter_copies(src_refs, land_refs, kinds, refs[2 * n], refs[2 * n + 1])
        wait()

    out = pl.pallas_call(
        body, name=name,
        out_shape=(*[pltpu.HBM(a.shape, a.dtype) for a in srcs], *[pltpu.HBM(a.shape, a.dtype) for a in lands]),
        in_specs=[HBM_SPEC] * (2 * n) + [SEM_SPEC, SEM_SPEC, pl.BlockSpec(memory_space=pl.ANY)],
        out_specs=tuple([HBM_SPEC] * (2 * n)),
        input_output_aliases={i: i for i in range(2 * n)},
        compiler_params=pltpu.CompilerParams(has_side_effects=SPLIT_EFFECT),
    )(*srcs, *lands, send_sems, recv_sems, after)
    return out[n:]


def _rope_tables(positions, sign):
    half = ROPE_DIM // 2
    inv_freq = ROPE_THETA ** (-jnp.arange(0, ROPE_DIM, 2, dtype=F32) / ROPE_DIM)
    ang = positions.astype(F32)[:, None] * inv_freq
    reps = LANE // half
    cos = jnp.tile(jnp.cos(ang), (1, reps))
    sin = jnp.tile(jnp.sin(ang), (1, reps)) * sign
    d = jnp.arange(LANE) % SWA_HEAD_DIM
    return (jnp.where(d < ROPE_DIM, cos, 1.0), jnp.where(d < half, -sin, 0.0),
            jnp.where((d >= half) & (d < ROPE_DIM), sin, 0.0))


def _pad_cols(a, n):
    return jnp.pad(a, ((0, 0), (0, n - a.shape[1])))


def _local_step(x, target, positions, mods, fetch, P, on_grads, on_small):
    S, D = x.shape
    Hf = D // FOX_HEAD_DIM
    Hq = D // SWA_HEAD_DIM
    Hk = Hq // SWA_GROUP
    Wk = Hk * SWA_HEAD_DIM
    n_in = 3 * D + Hf
    (sh1a, sc1a, g1a, sh2a, sc2a, g2a), (sh1b, sc1b, g1b, sh2b, sc2b, g2b) = mods
    row = lambda v: v.reshape(1, -1)
    cw = [jnp.transpose(P["conv_w"][l].reshape(3, 2, -1), (1, 0, 2)) for l in range(2)]
    cb = [P["conv_b"][l].reshape(2, 1, -1) for l in range(2)]

    W = dict(up=[None, None], down=[None, None])
    h1a = _modulate(x, sc1a, sh1a, "modulate_in")
    W["fox_in"], W["fox_o"] = fetch("fox", "wait", h1a)
    proj_a = _mm_nn(h1a, W["fox_in"], F32, "fox_in_proj", tn=896)
    flT = proj_a[:, 3 * D:n_in].T
    bf_col = P["fox_b_f"].reshape(Hf, 1)
    cumT = _fox_prep(flT, bf_col, "fox_cumsum")
    cum_col, cum_row = cumT.reshape(Hf, S, 1), cumT.reshape(Hf, 1, S)
    o_a, lse_a = _fox_attn_fwd(proj_a, cum_col, cum_row, Hf, "fox_attn_fwd")
    token = fetch("ffn0", "forward", o_a)
    y1a = _mm_nn(o_a, W["fox_o"], F32, "fox_out_proj")
    z1a, x1, h2a = _ln_fwd(x, y1a, g1a, row(P["ln_mix_g"][0]), row(P["ln_mix_b"][0]), sc2a, sh2a, token, "ln_mix0")
    W["up"][0], W["down"][0] = fetch("ffn0", "wait", h2a)
    u_a, a_a = _ffn_up(h2a, W["up"][0], cw[0], cb[0], "ffn_up0")
    token = fetch("swa", "forward", a_a)
    y2a = _mm_nn(a_a, W["down"][0], F32, "ffn_down0", tk=1408)
    z2a, x2, h1b = _ln_fwd(x1, y2a, g2a, row(P["ln_ffn_g"][0]), row(P["ln_ffn_b"][0]), sc1b, sh1b, token, "ln_ffn0")

    W["swa_in"], W["swa_o"] = fetch("swa", "wait", h1b)
    proj_b = _mm_nn(h1b, W["swa_in"], F32, "swa_in_proj")
    tabs_f = _rope_tables(positions, 1.0)
    tabs_b = _rope_tables(positions, -1.0)
    qk = _rope([(proj_b[:, :D + Wk], True)], tabs_f, F32, "rope_fwd")
    sinks = _pad_cols(P["swa_sinks"].reshape(1, Hq), LANE)
    o_b, lse_b = _swa_attn_fwd(qk, proj_b, sinks, Hq, "swa_attn_fwd")
    token = fetch("ffn1", "forward", o_b)
    y1b = _mm_nn(o_b, W["swa_o"], F32, "swa_out_proj")
    z1b, x3, h2b = _ln_fwd(x2, y1b, g1b, row(P["ln_mix_g"][1]), row(P["ln_mix_b"][1]), sc2b, sh2b, token, "ln_mix1")
    W["up"][1], W["down"][1] = fetch("ffn1", "wait", h2b)
    u_b, a_b = _ffn_up(h2b, W["up"][1], cw[1], cb[1], "ffn_up1")
    y2b = _mm_nn(a_b, W["down"][1], F32, "ffn_down1", tk=1408)
    z2b, dout, loss_row = _ln_fwd_loss(x3, y2b, g2b, row(P["ln_ffn_g"][1]), row(P["ln_ffn_b"][1]), target, "ln_ffn1_loss")

    def ffn_backward(dy, a, u, h_in, l, tag):
        d_down = _mm_tn(a, dy[None], BF16, "ffn_dwdown" + tag, tm=1408, tn=1024)[0]
        token = on_grads("ffn_w_down" + tag, d_down)
        da = _mm_nt(dy[None], W["down"][l][None], F32, "ffn_da" + tag)
        du, dcw, dcb = _ffn_bwd_elem(da, u, cw[l], cb[l], token, "ffn_bwd_elem" + tag)
        d_up = _mm_tn(h_in, du, BF16, "ffn_dwup" + tag)
        token = on_grads("ffn_w_up" + tag, d_up)
        dh = _mm_nt(du, W["up"][l], F32, "ffn_dh" + tag, tk=1408)
        return dh, token, jnp.transpose(dcw, (1, 0, 2)).reshape(3, -1), dcb.reshape(-1)

    dz2b, dy2b, dg_f1, db_f1, dgate2b = _ln_bwd(dout, z2b, y2b, g2b, row(P["ln_ffn_g"][1]), "ln_ffn1_bwd")
    dh2b, token, dcw1, dcb1 = ffn_backward(dy2b, a_b, u_b, h2b, 1, "1")
    dx3, dsc2b, dsh2b = _mod_bwd(dz2b, dh2b, x3, sc2b, token, "mod_ffn1_bwd")

    dz1b, dy1b, dg_m1, db_m1, dgate1b = _ln_bwd(dx3, z1b, y1b, g1b, row(P["ln_mix_g"][1]), "ln_mix1_bwd")
    token = on_grads("swa_w_o", _mm_tn(o_b, dy1b[None], BF16, "swa_dwo")[0])
    do_b = _mm_nt(dy1b[None], W["swa_o"][None], F32, "swa_do")
    dq_b, dk_b, dv_b, dsinks = _swa_attn_bwd(qk, proj_b, sinks, do_b, lse_b, Hq, token, "swa_attn_bwd")
    dproj_b = _rope([(dq_b, True), (dk_b, True), (dv_b, False)], tabs_b, BF16, "rope_bwd")
    token = on_grads("swa_w_in", _mm_tn(h1b, dproj_b[None], BF16, "swa_dwin")[0])
    dh1b = _mm_nt(dproj_b[None], W["swa_in"][None], F32, "swa_dh", tk=1280)
    dx2, dsc1b, dsh1b = _mod_bwd(dz1b, dh1b, x2, sc1b, token, "mod_mix1_bwd")

    dz2a, dy2a, dg_f0, db_f0, dgate2a = _ln_bwd(dx2, z2a, y2a, g2a, row(P["ln_ffn_g"][0]), "ln_ffn0_bwd")
    dh2a, token, dcw0, dcb0 = ffn_backward(dy2a, a_a, u_a, h2a, 0, "0")
    dx1, dsc2a, dsh2a = _mod_bwd(dz2a, dh2a, x1, sc2a, token, "mod_ffn0_bwd")

    dz1a, dy1a, dg_m0, db_m0, dgate1a = _ln_bwd(dx1, z1a, y1a, g1a, row(P["ln_mix_g"][0]), "ln_mix0_bwd")
    token = on_grads("fox_w_o", _mm_tn(o_a, dy1a[None], BF16, "fox_dwo")[0])
    do_a = _mm_nt(dy1a[None], W["fox_o"][None], F32, "fox_do")
    dq_a, dk_a, dv_a, dcum_row, dcum_col = _fox_attn_bwd(proj_a, do_a, o_a, cum_col, cum_row, lse_a, Hf, token,
                                                         "fox_attn_bwd")
    dflT, dbf = _fox_prep_bwd(dcum_row.reshape(Hf, S), dcum_col.reshape(Hf, S), flT, bf_col, "fox_cumsum_bwd")
    n_pad = W["fox_in"].shape[1]
    dproj_a = jnp.concatenate([dq_a, dk_a.astype(BF16), dv_a.astype(BF16),
                               _pad_cols(dflT.T, n_pad - 3 * D).astype(BF16)], axis=1)
    dh1a = _mm_nt(dproj_a[None], W["fox_in"][None], F32, "fox_dh", tk=896)
    grad_x, dsc1a, dsh1a = _mod_bwd(dz1a, dh1a, x, sc1a, token, "mod_mix0_bwd")

    dmod = jnp.stack([jnp.concatenate([dsh1a, dsc1a, dgate1a, dsh2a, dsc2a, dgate2a], axis=1)[0],
                      jnp.concatenate([dsh1b, dsc1b, dgate1b, dsh2b, dsc2b, dgate2b], axis=1)[0]])
    small = dict(dmod=dmod, conv_b=jnp.stack([dcb0, dcb1]), conv_w=jnp.stack([dcw0, dcw1]),
                 ln_mix_g=jnp.concatenate([dg_m0, dg_m1]), ln_mix_b=jnp.concatenate([db_m0, db_m1]),
                 ln_ffn_g=jnp.concatenate([dg_f0, dg_f1]), ln_ffn_b=jnp.concatenate([db_f0, db_f1]),
                 fox_b_f=dbf.reshape(-1), swa_sinks=dsinks[0, :Hq], loss=loss_row[0, 0].reshape(1))
    exchanged = on_small(small)
    on_grads("fox_w_in", _mm_tn(h1a, dproj_a[None], BF16, "fox_dwin", tn=896)[0], exchanged)
    return grad_x


SMALL_ORDER = ("dmod", "conv_b", "conv_w", "ln_mix_g", "ln_mix_b", "ln_ffn_g", "ln_ffn_b", "fox_b_f", "swa_sinks", "loss")


def _pack_rows(arrays):
    chunks, spans, off = [], [], 0
    for a in arrays:
        flat = a.reshape(-1)
        n = -(-flat.shape[0] // LANE) * LANE
        chunks.append(jnp.pad(flat, (0, n - flat.shape[0])))
        spans.append((off, flat.shape[0], a.shape))
        off += n
    total = -(-off // (8 * LANE)) * (8 * LANE)
    chunks.append(jnp.zeros((total - off,), F32))
    return jnp.concatenate(chunks).reshape(-1, LANE), spans


def _unpack_rows(packed, spans):
    flat = packed.reshape(-1)
    return [flat[off:off + n].reshape(shape) for off, n, shape in spans]


def kernel(x, c, positions, fox_w_in, fox_b_f, fox_w_o, swa_w_in, swa_sinks, swa_w_o, ada_w, ada_b, ffn_w_up, ffn_conv_w, ffn_conv_b, ffn_w_down, ln_mix_g, ln_mix_b, ln_ffn_g, ln_ffn_b, loss_target, m_fox_w_in, m_fox_b_f, m_fox_w_o, m_swa_w_in, m_swa_sinks, m_swa_w_o, m_ada_w, m_ada_b, m_ffn_w_up, m_ffn_conv_w, m_ffn_conv_b, m_ffn_w_down, m_ln_mix_g, m_ln_mix_b, m_ln_ffn_g, m_ln_ffn_b, v_fox_w_in, v_fox_b_f, v_fox_w_o, v_swa_w_in, v_swa_sinks, v_swa_w_o, v_ada_w, v_ada_b, v_ffn_w_up, v_ffn_conv_w, v_ffn_conv_b, v_ffn_w_down, v_ln_mix_g, v_ln_mix_b, v_ln_ffn_g, v_ln_ffn_b):
    S, D = x.shape[1], x.shape[2]
    L = ada_w.shape[0]
    me = 4 * lax.axis_index("x") + 2 * lax.axis_index("y") + lax.axis_index("c")
    n_ada = ada_w.shape[2]
    cu = ffn_w_up.shape[2]
    F = 4 * cu
    n_in = fox_w_in.shape[2] * N_DEV
    n_in_pad = -(-n_in // LANE) * LANE

    c_all = _all_gather_small(c.reshape(-1, LANE), "gather_c").reshape(N_DEV, D)
    b_cols = lax.dynamic_slice_in_dim(ada_b, me * n_ada, n_ada, axis=1).reshape(L, 1, n_ada)
    mod_blk = _ada_mod(jnp.pad(c_all, ((0, ADA_ROWS - N_DEV), (0, 0))), ada_w, b_cols, "ada_mod")[:, :N_DEV]
    mod_all = _all_gather_small(mod_blk.reshape(-1, LANE), "gather_mod").reshape(N_DEV, L, N_DEV, n_ada)
    mod_mine = lax.dynamic_index_in_dim(mod_all, me, axis=2, keepdims=False)
    mod_mine = jnp.transpose(mod_mine, (1, 0, 2)).reshape(L, N_DEV * n_ada)
    mods = [[mod_mine[l, k * D:(k + 1) * D].reshape(1, D) for k in range(6)] for l in range(L)]

    P = dict(fox_b_f=fox_b_f[0], swa_sinks=swa_sinks[0], conv_b=ffn_conv_b,
             ln_mix_g=ln_mix_g, ln_mix_b=ln_mix_b, ln_ffn_g=ln_ffn_g, ln_ffn_b=ln_ffn_b)
    cw_rows = _all_gather_small(_pack_rows([ffn_conv_w])[0], "gather_conv_w", after=[mod_all])
    n_cw = ffn_conv_w.size
    cw_dev = cw_rows.reshape(N_DEV, -1)[:, :n_cw].reshape(N_DEV, L, 3, cu)
    P["conv_w"] = jnp.transpose(cw_dev, (1, 2, 0, 3)).reshape(L, 3, N_DEV * cu)

    gather_groups = dict(
        fox=([fox_w_in[0].astype(BF16), fox_w_o[0].astype(BF16)], ["major", "rows"]),
        ffn0=([ffn_w_up[0].astype(BF16), ffn_w_down[0].astype(BF16)], ["halves", "rows"]),
        swa=([swa_w_in[0].astype(BF16), swa_w_o[0].astype(BF16)], ["major", "rows"]),
        ffn1=([ffn_w_up[1].astype(BF16), ffn_w_down[1].astype(BF16)], ["halves", "rows"]))
    starts_after = dict(fox=["ffn0", "swa"], ffn0=["ffn1"])
    gathers = {}

    def start_group(group, after):
        shards, kinds = gather_groups[group]
        send, recv, local, thru, bufs, token = _gather_start(shards, kinds, after, "gather_start_" + group)
        gathers[group] = dict(send=send, recv=recv, local=local, shards=thru, bufs=bufs, kinds=kinds,
                              shapes=[a.shape for a in shards], token=token)
        return token

    start_group("fox", cw_rows)

    def natural(g, pad_to=None):
        w = jnp.transpose(g, (1, 0, 2)).reshape(D, -1)
        return w if pad_to is None else _pad_cols(w, pad_to)

    def forward_stage(group, after):
        s = gathers[group]
        s["fsend"], s["frecv"], s["bufs"], token = _gather_forward(s["recv"], s["bufs"], s["kinds"], s["shapes"], after,
                                                                   "gather_forward_" + group)
        for nxt in starts_after.get(group, ()):
            token = start_group(nxt, token)
        return token

    def fetch(group, stage, after):
        if stage == "forward":
            return forward_stage(group, after)
        if group == "fox":
            after = forward_stage(group, gathers[group]["token"])
        s = gathers.pop(group)
        first, second = _gather_wait(s["send"], s["recv"], s["local"], s["fsend"], s["frecv"], s["shards"], s["bufs"],
                                     s["kinds"], after, "gather_wait_" + group)
        if group == "fox":
            return natural(first, n_in_pad), second
        if group == "swa":
            return natural(first), second
        return first, second

    out, pending = {}, {}

    def shard_major(g):
        return jnp.transpose(g.reshape(D, N_DEV, -1), (1, 0, 2))

    big = dict(
        ffn_w_down1=("ffn_w_down", "rows", 1, (ffn_w_down, m_ffn_w_down, v_ffn_w_down)),
        ffn_w_up1=("ffn_w_up", "halves", 1, (ffn_w_up, m_ffn_w_up, v_ffn_w_up)),
        swa_w_o=("swa_w_o", "rows", 0, (swa_w_o, m_swa_w_o, v_swa_w_o)),
        swa_w_in=("swa_w_in", "major", 0, (swa_w_in, m_swa_w_in, v_swa_w_in)),
        ffn_w_down0=("ffn_w_down", "rows", 0, (ffn_w_down, m_ffn_w_down, v_ffn_w_down)),
        ffn_w_up0=("ffn_w_up", "halves", 0, (ffn_w_up, m_ffn_w_up, v_ffn_w_up)),
        fox_w_o=("fox_w_o", "rows", 0, (fox_w_o, m_fox_w_o, v_fox_w_o)),
        fox_w_in=("fox_w_in", "major", 0, (fox_w_in, m_fox_w_in, v_fox_w_in)))
    finish_at = dict(swa_w_o=["ffn_w_down1"], ffn_w_up0=["ffn_w_up1", "swa_w_o", "swa_w_in"], fox_w_o=["ffn_w_down0"])
    tail = {}

    def finish(name, after):
        send, recv, thru, lands = pending.pop(name)
        param, kind, layer, wmv = big[name]
        landed, = _scatter_wait(send, recv, thru, lands, [kind], after, "scatter_wait_" + name)
        out[param] = _adam_sum(landed, *wmv, layer, out.get(param), "adam_" + name)

    def on_grads(name, g, after=None):
        kind = big[name][1]
        if name == "fox_w_in":
            g = g[:, :n_in]
        src = shard_major(g) if kind == "major" else g
        send, recv, thru, lands, token = _scatter_start([src], [kind], after, "scatter_start_" + name)
        pending[name] = (send, recv, thru, lands)
        for done in finish_at.get(name, ()):
            finish(done, token)
        tail["last_start"] = token
        return token

    def on_small(small):
        packed, tail["spans"] = _pack_rows([small[k] for k in SMALL_ORDER])
        finish("ffn_w_up0", packed)
        done_first = [out[k][0] for k in ("ffn_w_up", "ffn_w_down", "swa_w_in", "swa_w_o")]
        tail["gathered"] = _all_gather_small(packed, "gather_small_grads", after=done_first)
        return tail["gathered"]

    grad_x = _local_step(x[0], loss_target[0], positions[0], mods, fetch, P, on_grads, on_small)

    gathered, spans = tail["gathered"], tail["spans"]
    totals = dict(zip(SMALL_ORDER, _unpack_rows(_sum_slots(gathered, tail["last_start"], "sum_small_grads"), spans)))
    loss = totals["loss"].reshape(())
    n_mod = L * 6 * D
    dmod_all = gathered.reshape(N_DEV, -1)[:, :n_mod].reshape(N_DEV, L, 6 * D)
    dmod_cols = jnp.transpose(lax.dynamic_slice_in_dim(dmod_all, me * n_ada, n_ada, axis=2), (1, 0, 2))
    g_ada_w = _ada_bwd(c_all.T, dmod_cols, "ada_w_grad")
    out["ada_w"] = (g_ada_w,) + tuple(_adam(g_ada_w, ada_w, m_ada_w, v_ada_w, "adam_ada_w"))

    g_small = dict(fox_b_f=totals["fox_b_f"].reshape(fox_b_f.shape), swa_sinks=totals["swa_sinks"].reshape(swa_sinks.shape),
                   ada_b=totals["dmod"].reshape(ada_b.shape), ffn_conv_b=totals["conv_b"].reshape(ffn_conv_b.shape),
                   ffn_conv_w=lax.dynamic_slice_in_dim(totals["conv_w"].reshape(L, 3, 2 * F), me * cu, cu, axis=2),
                   ln_mix_g=totals["ln_mix_g"], ln_mix_b=totals["ln_mix_b"],
                   ln_ffn_g=totals["ln_ffn_g"], ln_ffn_b=totals["ln_ffn_b"])
    small_names = ("fox_b_f", "swa_sinks", "ada_b", "ffn_conv_b", "ffn_conv_w", "ln_mix_g", "ln_mix_b", "ln_ffn_g", "ln_ffn_b")
    w_small = dict(fox_b_f=(fox_b_f, m_fox_b_f, v_fox_b_f), swa_sinks=(swa_sinks, m_swa_sinks, v_swa_sinks),
                   ada_b=(ada_b, m_ada_b, v_ada_b), ffn_conv_b=(ffn_conv_b, m_ffn_conv_b, v_ffn_conv_b),
                   ffn_conv_w=(ffn_conv_w, m_ffn_conv_w, v_ffn_conv_w),
                   ln_mix_g=(ln_mix_g, m_ln_mix_g, v_ln_mix_g), ln_mix_b=(ln_mix_b, m_ln_mix_b, v_ln_mix_b),
                   ln_ffn_g=(ln_ffn_g, m_ln_ffn_g, v_ln_ffn_g), ln_ffn_b=(ln_ffn_b, m_ln_ffn_b, v_ln_ffn_b))
    pk_g, sp = _pack_rows([g_small[k] for k in small_names])
    pk_w = _pack_rows([w_small[k][0] for k in small_names])[0]
    pk_m = _pack_rows([w_small[k][1] for k in small_names])[0]
    pk_v = _pack_rows([w_small[k][2] for k in small_names])[0]
    res = _adam(pk_g[None], pk_w[None], pk_m[None], pk_v[None], "adam_small")
    finish("fox_w_o", res[0])
    finish("fox_w_in", out["ada_w"][1])
    res = [dict(zip(small_names, _unpack_rows(r[0], sp))) for r in res]
    for k in small_names:
        out[k] = (g_small[k], res[0][k], res[1][k], res[2][k])

    order = ("fox_w_in", "fox_b_f", "fox_w_o", "swa_w_in", "swa_sinks", "swa_w_o", "ada_w", "ada_b", "ffn_w_up",
             "ffn_conv_w", "ffn_conv_b", "ffn_w_down", "ln_mix_g", "ln_mix_b", "ln_ffn_g", "ln_ffn_b")
    return (loss, grad_x[None], *[out[k][0] for k in order], *[out[k][1] for k in order],
            *[out[k][2] for k in order], *[out[k][3] for k in order])
```

```python
import functools

import jax
import jax.numpy as jnp
from jax import lax
from jax.experimental import pallas as pl
from jax.experimental.pallas import tpu as pltpu

F32 = jnp.float32
BF16 = jnp.bfloat16
MESH = pl.DeviceIdType.MESH
N_DEV = 8
AXES = ("x", "y", "c")

DEPTH = 2
ALPHA = (2.0 * DEPTH) ** 0.25
LN_EPS = 1e-5
FOX_HEAD_DIM = 128
SWA_HEAD_DIM = 64
SWA_GROUP = 8
SWA_WINDOW = 128
Q_BLOCK = 128
ROPE_DIM = 16
ROPE_THETA = 500000.0

ADAM_LR = 0.001
ADAM_B1 = 0.9
ADAM_B2 = 0.999
ADAM_EPS = 1e-08
ADAM_WD = 0.01
ADAM_STEP = 10

LANE = 128
MIB = 1024 * 1024


def _tile(n, pref, unit=LANE):
    if n <= pref:
        return n
    t = (pref // unit) * unit
    while t >= unit:
        if n % t == 0:
            return t
        t -= unit
    return n


def _params(sem, vmem_mib=48):
    return pltpu.CompilerParams(dimension_semantics=sem, vmem_limit_bytes=vmem_mib * MIB)


def _sigmoid(x):
    return 1.0 / (1.0 + jnp.exp(-x))


def _mm_call(dot, grid_mnk, in_specs, out_spec, out_shape, k_axis, nk, tm, tn, name, operands):
    sem = ("parallel",) * (len(grid_mnk) - 1) + ("arbitrary",)

    if nk == 1:
        def body(a_ref, b_ref, o_ref):
            o_ref[...] = dot(a_ref[...], b_ref[...]).astype(o_ref.dtype)
        scratch = []
    else:
        def body(a_ref, b_ref, o_ref, acc_ref):
            k = pl.program_id(k_axis)

            @pl.when(k == 0)
            def _():
                acc_ref[...] = jnp.zeros_like(acc_ref)

            acc_ref[...] += dot(a_ref[...], b_ref[...])

            @pl.when(k == nk - 1)
            def _():
                o_ref[...] = acc_ref[...].astype(o_ref.dtype)
        scratch = [pltpu.VMEM((tm, tn), F32)]

    return pl.pallas_call(
        body, name=name, grid=grid_mnk, in_specs=in_specs, out_specs=out_spec, out_shape=out_shape,
        scratch_shapes=scratch, compiler_params=_params(sem, 56),
    )(*operands)


def _dot(dims):
    def dot(a, b):
        return lax.dot_general(a.astype(BF16), b.astype(BF16), (dims, ((), ())), preferred_element_type=F32)
    return dot


def _mm_nn(a, b, out_dtype, name, tm=2048, tn=512, tk=2048):
    M, K = a.shape
    N = b.shape[1]
    tm, tn, tk = _tile(M, tm), _tile(N, tn), _tile(K, tk)
    nk = K // tk
    return _mm_call(
        _dot(((1,), (0,))), (M // tm, N // tn, nk),
        [pl.BlockSpec((tm, tk), lambda i, j, k: (i, k)), pl.BlockSpec((tk, tn), lambda i, j, k: (k, j))],
        pl.BlockSpec((tm, tn), lambda i, j, k: (i, j)), jax.ShapeDtypeStruct((M, N), out_dtype),
        2, nk, tm, tn, name, (a, b))


def _mm_nt(a, b, out_dtype, name, tm=2048, tn=512, tk=2048):
    P, M, K = a.shape
    N = b.shape[1]
    tm, tn, tk = _tile(M, tm), _tile(N, tn), _tile(K, tk)
    nk = K // tk
    return _mm_call(
        _dot(((1,), (1,))), (M // tm, N // tn, P * nk),
        [pl.BlockSpec((None, tm, tk), lambda i, j, k: (k // nk, i, k % nk)),
         pl.BlockSpec((None, tn, tk), lambda i, j, k: (k // nk, j, k % nk))],
        pl.BlockSpec((tm, tn), lambda i, j, k: (i, j)), jax.ShapeDtypeStruct((M, N), out_dtype),
        2, P * nk, tm, tn, name, (a, b))


def _mm_tn(a, b, out_dtype, name, tm=2048, tn=512, tk=2048):
    K, M = a.shape
    P, _, N = b.shape
    tm, tn, tk = _tile(M, tm), _tile(N, tn), _tile(K, tk)
    nk = K // tk
    return _mm_call(
        _dot(((0,), (0,))), (P, M // tm, N // tn, nk),
        [pl.BlockSpec((tk, tm), lambda p, i, j, k: (k, i)), pl.BlockSpec((None, tk, tn), lambda p, i, j, k: (p, k, j))],
        pl.BlockSpec((None, tm, tn), lambda p, i, j, k: (p, i, j)), jax.ShapeDtypeStruct((P, M, N), out_dtype),
        3, nk, tm, tn, name, (a, b))


ROW_TILE = 256


def _row_spec(tm, D):
    return pl.BlockSpec((tm, D), lambda i: (i, 0))


def _vec_spec(D):
    return pl.BlockSpec((1, D), lambda i: (0, 0))


def _modulate(x, sc, sh, name):
    S, D = x.shape
    tm = _tile(S, ROW_TILE, 8)

    def body(x_ref, sc_ref, sh_ref, h_ref):
        h_ref[...] = (x_ref[...] * (1.0 + sc_ref[...]) + sh_ref[...]).astype(BF16)

    return pl.pallas_call(
        body, name=name, grid=(S // tm,),
        in_specs=[_row_spec(tm, D), _vec_spec(D), _vec_spec(D)],
        out_specs=_row_spec(tm, D),
        out_shape=jax.ShapeDtypeStruct((S, D), BF16),
        compiler_params=_params(("parallel",)),
    )(x, sc, sh)


def _layer_norm_rows(z, gamma, beta):
    mu = jnp.mean(z, axis=-1, keepdims=True)
    zc = z - mu
    var = jnp.mean(zc * zc, axis=-1, keepdims=True)
    return zc * lax.rsqrt(var + LN_EPS) * gamma + beta


def _ln_fwd(x, y, gate, gamma, beta, sc_n, sh_n, after, name):
    S, D = x.shape
    tm = _tile(S, ROW_TILE, 8)

    def body(x_ref, y_ref, gate_ref, g_ref, b_ref, sc_ref, sh_ref, after_ref, z_ref, xo_ref, hn_ref):
        z = ALPHA * x_ref[...] + (1.0 + gate_ref[...]) * y_ref[...]
        xo = _layer_norm_rows(z, g_ref[...], b_ref[...])
        z_ref[...] = z
        xo_ref[...] = xo
        hn_ref[...] = (xo * (1.0 + sc_ref[...]) + sh_ref[...]).astype(BF16)

    return pl.pallas_call(
        body, name=name, grid=(S // tm,),
        in_specs=[_row_spec(tm, D), _row_spec(tm, D)] + [_vec_spec(D)] * 5 + [pl.BlockSpec(memory_space=pl.ANY)],
        out_specs=[_row_spec(tm, D)] * 3,
        out_shape=[jax.ShapeDtypeStruct((S, D), F32), jax.ShapeDtypeStruct((S, D), F32),
                   jax.ShapeDtypeStruct((S, D), BF16)],
        compiler_params=_params(("parallel",)),
    )(x, y, gate, gamma, beta, sc_n, sh_n, after)


def _ln_fwd_loss(x, y, gate, gamma, beta, target, name):
    S, D = x.shape
    tm = _tile(S, ROW_TILE, 8)

    def body(x_ref, y_ref, gate_ref, g_ref, b_ref, t_ref, z_ref, dout_ref, loss_ref):
        @pl.when(pl.program_id(0) == 0)
        def _():
            loss_ref[...] = jnp.zeros_like(loss_ref)

        z = ALPHA * x_ref[...] + (1.0 + gate_ref[...]) * y_ref[...]
        xo = _layer_norm_rows(z, g_ref[...], b_ref[...])
        err = xo - t_ref[...]
        z_ref[...] = z
        dout_ref[...] = err * (1.0 / D)
        loss_ref[...] += (0.5 / D) * jnp.sum(err * err)

    return pl.pallas_call(
        body, name=name, grid=(S // tm,),
        in_specs=[_row_spec(tm, D), _row_spec(tm, D)] + [_vec_spec(D)] * 3 + [_row_spec(tm, D)],
        out_specs=[_row_spec(tm, D), _row_spec(tm, D), pl.BlockSpec((1, LANE), lambda i: (0, 0))],
        out_shape=[jax.ShapeDtypeStruct((S, D), F32), jax.ShapeDtypeStruct((S, D), F32),
                   jax.ShapeDtypeStruct((1, LANE), F32)],
        compiler_params=_params(("arbitrary",)),
    )(x, y, gate, gamma, beta, target)


def _ln_bwd(dout, z, y, gate, gamma, name):
    S, D = z.shape
    tm = _tile(S, ROW_TILE, 8)

    def body(dout_ref, z_ref, y_ref, gate_ref, g_ref, dz_ref, dy_ref, dg_ref, db_ref, dgate_ref):
        @pl.when(pl.program_id(0) == 0)
        def _():
            dg_ref[...] = jnp.zeros_like(dg_ref)
            db_ref[...] = jnp.zeros_like(db_ref)
            dgate_ref[...] = jnp.zeros_like(dgate_ref)

        z = z_ref[...]
        dout = dout_ref[...]
        mu = jnp.mean(z, axis=-1, keepdims=True)
        zc = z - mu
        var = jnp.mean(zc * zc, axis=-1, keepdims=True)
        rstd = lax.rsqrt(var + LN_EPS)
        xhat = zc * rstd
        dxhat = dout * g_ref[...]
        m1 = jnp.mean(dxhat, axis=-1, keepdims=True)
        m2 = jnp.mean(dxhat * xhat, axis=-1, keepdims=True)
        dz = rstd * (dxhat - m1 - xhat * m2)
        dz_ref[...] = dz
        dy_ref[...] = (dz * (1.0 + gate_ref[...])).astype(BF16)
        dg_ref[...] += jnp.sum(dout * xhat, axis=0, keepdims=True)
        db_ref[...] += jnp.sum(dout, axis=0, keepdims=True)
        dgate_ref[...] += jnp.sum(dz * y_ref[...], axis=0, keepdims=True)

    return pl.pallas_call(
        body, name=name, grid=(S // tm,),
        in_specs=[_row_spec(tm, D)] * 3 + [_vec_spec(D)] * 2,
        out_specs=[_row_spec(tm, D), _row_spec(tm, D)] + [_vec_spec(D)] * 3,
        out_shape=[jax.ShapeDtypeStruct((S, D), F32), jax.ShapeDtypeStruct((S, D), BF16)]
        + [jax.ShapeDtypeStruct((1, D), F32)] * 3,
        compiler_params=_params(("arbitrary",)),
    )(dout, z, y, gate, gamma)


def _mod_bwd(dz, dh, xin, sc, after, name):
    S, D = dz.shape
    tm = _tile(S, ROW_TILE, 8)

    def body(dz_ref, dh_ref, x_ref, sc_ref, after_ref, dx_ref, dsc_ref, dsh_ref):
        @pl.when(pl.program_id(0) == 0)
        def _():
            dsc_ref[...] = jnp.zeros_like(dsc_ref)
            dsh_ref[...] = jnp.zeros_like(dsh_ref)

        dh = dh_ref[...]
        dx_ref[...] = ALPHA * dz_ref[...] + dh * (1.0 + sc_ref[...])
        dsc_ref[...] += jnp.sum(dh * x_ref[...], axis=0, keepdims=True)
        dsh_ref[...] += jnp.sum(dh, axis=0, keepdims=True)

    return pl.pallas_call(
        body, name=name, grid=(S // tm,),
        in_specs=[_row_spec(tm, D)] * 3 + [_vec_spec(D), pl.BlockSpec(memory_space=pl.ANY)],
        out_specs=[_row_spec(tm, D), _vec_spec(D), _vec_spec(D)],
        out_shape=[jax.ShapeDtypeStruct((S, D), F32)] + [jax.ShapeDtypeStruct((1, D), F32)] * 2,
        compiler_params=_params(("arbitrary",)),
    )(dz, dh, xin, sc, after)


def _shift_down(u, k, row):
    return jnp.where(row >= k, pltpu.roll(u, k, axis=0), 0.0)


def _shift_up(u, k, row, S):
    return jnp.where(row < S - k, pltpu.roll(u, S - k, axis=0), 0.0)


def _ffn_up(h, w, cw, cb, name):
    S, D = h.shape
    F = w.shape[2]
    tn = _tile(F, 256)

    def body(h_ref, w_ref, cw_ref, cb_ref, u_ref, a_ref):
        hh = h_ref[...]
        row = lax.broadcasted_iota(jnp.int32, (S, tn), 0)
        conv = []
        for p in range(2):
            u = jnp.dot(hh, w_ref[p], preferred_element_type=F32)
            u_ref[p] = u
            cwp = cw_ref[p]
            conv.append(_shift_down(u, 2, row) * cwp[0:1] + _shift_down(u, 1, row) * cwp[1:2]
                        + u * cwp[2:3] + cb_ref[p])
        g, v = conv
        a_ref[...] = (g * _sigmoid(g) * v).astype(BF16)

    return pl.pallas_call(
        body, name=name, grid=(F // tn,),
        in_specs=[pl.BlockSpec((S, D), lambda j: (0, 0)), pl.BlockSpec((2, D, tn), lambda j: (0, 0, j)),
                  pl.BlockSpec((2, 3, tn), lambda j: (0, 0, j)), pl.BlockSpec((2, 1, tn), lambda j: (0, 0, j))],
        out_specs=[pl.BlockSpec((2, S, tn), lambda j: (0, 0, j)), pl.BlockSpec((S, tn), lambda j: (0, j))],
        out_shape=[jax.ShapeDtypeStruct((2, S, F), F32), jax.ShapeDtypeStruct((S, F), BF16)],
        compiler_params=_params(("parallel",), 56),
    )(h, w, cw, cb)


def _ffn_bwd_elem(da, u, cw, cb, after, name):
    _, S, F = u.shape
    tn = _tile(F, 256)

    def body(da_ref, u_ref, cw_ref, cb_ref, after_ref, du_ref, dcw_ref, dcb_ref):
        row = lax.broadcasted_iota(jnp.int32, (S, tn), 0)
        da = da_ref[...]
        shifted, conv = [], []
        for p in range(2):
            u = u_ref[p]
            u1, u2 = _shift_down(u, 1, row), _shift_down(u, 2, row)
            cwp = cw_ref[p]
            shifted.append((u2, u1, u))
            conv.append(u2 * cwp[0:1] + u1 * cwp[1:2] + u * cwp[2:3] + cb_ref[p])
        g, v = conv
        sg = _sigmoid(g)
        d_conv = (da * v * (sg * (1.0 + g * (1.0 - sg))), da * (g * sg))
        for p in range(2):
            d = d_conv[p]
            cwp = cw_ref[p]
            dcb_ref[p] = jnp.sum(d, axis=0, keepdims=True)
            for j in range(3):
                dcw_ref[p, j:j + 1, :] = jnp.sum(d * shifted[p][j], axis=0, keepdims=True)
            du = d * cwp[2:3] + _shift_up(d, 1, row, S) * cwp[1:2] + _shift_up(d, 2, row, S) * cwp[0:1]
            du_ref[p] = du.astype(BF16)

    return pl.pallas_call(
        body, name=name, grid=(F // tn,),
        in_specs=[pl.BlockSpec((S, tn), lambda j: (0, j)), pl.BlockSpec((2, S, tn), lambda j: (0, 0, j)),
                  pl.BlockSpec((2, 3, tn), lambda j: (0, 0, j)), pl.BlockSpec((2, 1, tn), lambda j: (0, 0, j)),
                  pl.BlockSpec(memory_space=pl.ANY)],
        out_specs=[pl.BlockSpec((2, S, tn), lambda j: (0, 0, j)), pl.BlockSpec((2, 3, tn), lambda j: (0, 0, j)),
                   pl.BlockSpec((2, 1, tn), lambda j: (0, 0, j))],
        out_shape=[jax.ShapeDtypeStruct((2, S, F), BF16), jax.ShapeDtypeStruct((2, 3, F), F32),
                   jax.ShapeDtypeStruct((2, 1, F), F32)],
        compiler_params=_params(("parallel",), 56),
    )(da, u, cw, cb, after)


def _split3(x):
    hi = x.astype(BF16)
    r1 = x - hi.astype(F32)
    mid = r1.astype(BF16)
    lo = (r1 - mid.astype(F32)).astype(BF16)
    return hi, mid, lo


def _tri_matmul(x, upper, S):
    tc = _tile(S, 512)
    parts = _split3(x)
    outs = []
    for b in range(S // tc):
        r = lax.broadcasted_iota(jnp.int32, (S, tc), 0)
        c = lax.broadcasted_iota(jnp.int32, (S, tc), 1) + b * tc
        tri = jnp.where((r <= c) if upper else (r >= c), 1.0, 0.0).astype(BF16)
        acc = jnp.dot(parts[0], tri, preferred_element_type=F32)
        acc += jnp.dot(parts[1], tri, preferred_element_type=F32)
        acc += jnp.dot(parts[2], tri, preferred_element_type=F32)
        outs.append(acc)
    return outs, tc


def _fox_prep(flT, bf, name):
    H, S = flT.shape

    def body(fl_ref, b_ref, cum_ref):
        zz = fl_ref[...] + b_ref[...]
        lf = jnp.minimum(zz, 0.0) - jnp.log(1.0 + jnp.exp(-jnp.abs(zz)))
        outs, tc = _tri_matmul(lf, True, S)
        for b, o in enumerate(outs):
            cum_ref[:, b * tc:(b + 1) * tc] = o

    return pl.pallas_call(
        body, name=name,
        in_specs=[pl.BlockSpec(memory_space=pltpu.VMEM)] * 2,
        out_specs=pl.BlockSpec(memory_space=pltpu.VMEM),
        out_shape=jax.ShapeDtypeStruct((H, S), F32),
        compiler_params=pltpu.CompilerParams(vmem_limit_bytes=48 * MIB),
    )(flT, bf)


def _fox_prep_bwd(dcum_key, dcum_query, flT, bf, name):
    H, S = flT.shape

    def body(dck_ref, dcq_ref, fl_ref, b_ref, dfl_ref, dbf_ref):
        zz = fl_ref[...] + b_ref[...]
        outs, tc = _tri_matmul(dck_ref[...] + dcq_ref[...], False, S)
        total = jnp.zeros((H, 1), F32)
        for b, o in enumerate(outs):
            dfl = o * _sigmoid(-zz[:, b * tc:(b + 1) * tc])
            dfl_ref[:, b * tc:(b + 1) * tc] = dfl
            total += jnp.sum(dfl, axis=1, keepdims=True)
        dbf_ref[...] = total

    return pl.pallas_call(
        body, name=name,
        in_specs=[pl.BlockSpec(memory_space=pltpu.VMEM)] * 4,
        out_specs=[pl.BlockSpec(memory_space=pltpu.VMEM)] * 2,
        out_shape=[jax.ShapeDtypeStruct((H, S), F32), jax.ShapeDtypeStruct((H, 1), F32)],
        compiler_params=pltpu.CompilerParams(vmem_limit_bytes=48 * MIB),
    )(dcum_key, dcum_query, flT, bf)


FOX_TQ = 256
FOX_TC = 512


def _fox_scores(q, k_ref, ck_ref, i, lo, n, tq):
    k = k_ref[lo:lo + n, :].astype(BF16)
    s = lax.dot_general(q, k, (((1,), (1,)), ((), ())), preferred_element_type=F32) - ck_ref[:, lo:lo + n]
    qpos = i * tq + lax.broadcasted_iota(jnp.int32, (tq, n), 0)
    kpos = lo + lax.broadcasted_iota(jnp.int32, (tq, n), 1)
    return jnp.where(kpos <= qpos, s, -jnp.inf)


def _fox_attn_fwd(proj, cum_row, H, name):
    S = proj.shape[0]
    dh = FOX_HEAD_DIM
    tq = _tile(S, FOX_TQ)
    scale = dh ** -0.5

    def body(q_ref, k_ref, v_ref, ck_ref, o_ref, lse_ref):
        q = (q_ref[...] * scale).astype(BF16)
        s = _fox_scores(q, k_ref, ck_ref, pl.program_id(1), 0, S, tq)
        m = jnp.max(s, axis=-1, keepdims=True)
        p = jnp.exp(s - m)
        l = jnp.sum(p, axis=-1, keepdims=True)
        o = jnp.dot(p.astype(BF16), v_ref[...].astype(BF16), preferred_element_type=F32) / l
        o_ref[...] = o.astype(BF16)
        lse_ref[...] = m + jnp.log(l)

    return pl.pallas_call(
        body, name=name, grid=(H, S // tq),
        in_specs=[pl.BlockSpec((tq, dh), lambda h, i: (i, h)),
                  pl.BlockSpec((S, dh), lambda h, i: (0, H + h)),
                  pl.BlockSpec((S, dh), lambda h, i: (0, 2 * H + h)),
                  pl.BlockSpec((None, 1, S), lambda h, i: (h, 0, 0))],
        out_specs=[pl.BlockSpec((tq, dh), lambda h, i: (i, h)),
                   pl.BlockSpec((None, tq, 1), lambda h, i: (h, i, 0))],
        out_shape=[jax.ShapeDtypeStruct((S, H * dh), BF16), jax.ShapeDtypeStruct((H, S, 1), F32)],
        compiler_params=_params(("parallel", "parallel")),
    )(proj, proj, proj, cum_row)


def _fox_attn_bwd(proj, do, o, cum_row, lse, H, after, name):
    S = proj.shape[0]
    dh = FOX_HEAD_DIM
    tq = _tile(S, FOX_TQ)
    tc = _tile(S, FOX_TC)
    scale = dh ** -0.5

    def body(q_ref, k_ref, v_ref, do_ref, o_ref, ck_ref, lse_ref, after_ref,
             dq_ref, dk_ref, dv_ref, dck_ref, dcq_ref, acc_ref):
        i = pl.program_id(1)

        @pl.when(i == 0)
        def _():
            dk_ref[...] = jnp.zeros_like(dk_ref)
            dv_ref[...] = jnp.zeros_like(dv_ref)
            dck_ref[...] = jnp.zeros_like(dck_ref)

        acc_ref[...] = jnp.zeros_like(acc_ref)
        dcq_ref[...] = jnp.zeros_like(dcq_ref)
        q = (q_ref[...] * scale).astype(BF16)
        do_f = do_ref[...]
        do_b = do_f.astype(BF16)
        delta = jnp.sum(do_f * o_ref[...].astype(F32), axis=-1, keepdims=True)
        lse_q = lse_ref[...]
        for c in range(S // tc):
            @pl.when(c * tc <= i * tq + tq - 1)
            def _():
                rows = slice(c * tc, (c + 1) * tc)
                p = jnp.exp(_fox_scores(q, k_ref, ck_ref, i, c * tc, tc, tq) - lse_q)
                dp = lax.dot_general(do_b, v_ref[rows, :].astype(BF16), (((1,), (1,)), ((), ())),
                                     preferred_element_type=F32)
                ds = p * (dp - delta)
                ds_b = ds.astype(BF16)
                acc_ref[...] += jnp.dot(ds_b, k_ref[rows, :].astype(BF16), preferred_element_type=F32)
                dk_ref[rows, :] += lax.dot_general(ds_b, q, (((0,), (0,)), ((), ())), preferred_element_type=F32)
                dv_ref[rows, :] += lax.dot_general(p.astype(BF16), do_b, (((0,), (0,)), ((), ())),
                                                   preferred_element_type=F32)
                dck_ref[:, rows] -= jnp.sum(ds, axis=0, keepdims=True)
                dcq_ref[...] += jnp.sum(ds, axis=-1, keepdims=True)
        dq_ref[...] = (acc_ref[...] * scale).astype(BF16)

    W = H * dh
    return pl.pallas_call(
        body, name=name, grid=(H, S // tq),
        in_specs=[pl.BlockSpec((tq, dh), lambda h, i: (i, h)),
                  pl.BlockSpec((S, dh), lambda h, i: (0, H + h)),
                  pl.BlockSpec((S, dh), lambda h, i: (0, 2 * H + h)),
                  pl.BlockSpec((tq, dh), lambda h, i: (i, h)),
                  pl.BlockSpec((tq, dh), lambda h, i: (i, h)),
                  pl.BlockSpec((None, 1, S), lambda h, i: (h, 0, 0)),
                  pl.BlockSpec((None, tq, 1), lambda h, i: (h, i, 0)),
                  pl.BlockSpec(memory_space=pl.ANY)],
        out_specs=[pl.BlockSpec((tq, dh), lambda h, i: (i, h)),
                   pl.BlockSpec((S, dh), lambda h, i: (0, h)),
                   pl.BlockSpec((S, dh), lambda h, i: (0, h)),
                   pl.BlockSpec((None, 1, S), lambda h, i: (h, 0, 0)),
                   pl.BlockSpec((None, tq, 1), lambda h, i: (h, i, 0))],
        out_shape=[jax.ShapeDtypeStruct((S, W), BF16), jax.ShapeDtypeStruct((S, W), F32),
                   jax.ShapeDtypeStruct((S, W), F32), jax.ShapeDtypeStruct((H, 1, S), F32),
                   jax.ShapeDtypeStruct((H, S, 1), F32)],
        scratch_shapes=[pltpu.VMEM((tq, dh), F32)],
        compiler_params=_params(("parallel", "arbitrary")),
    )(proj, proj, proj, do, o, cum_row, lse, after)


def _rope(parts, tabs, out_dtype, name):
    S = parts[0][0].shape[0]
    widths = [a.shape[1] for a, _ in parts]
    total = sum(widths)
    tm = _tile(S, ROW_TILE, 8)
    flags = [r for _, r in parts]

    def body(*refs):
        in_refs = refs[:len(parts)]
        cos_ref, sa_ref, sb_ref, o_ref = refs[len(parts):]
        cos, sa, sb = cos_ref[...], sa_ref[...], sb_ref[...]
        off = 0
        for ref, rot, w in zip(in_refs, flags, widths):
            for j in range(w // LANE):
                t = ref[:, j * LANE:(j + 1) * LANE]
                if rot:
                    t = t * cos + pltpu.roll(t, LANE - ROPE_DIM // 2, axis=1) * sa + pltpu.roll(t, ROPE_DIM // 2, axis=1) * sb
                o_ref[:, off + j * LANE:off + (j + 1) * LANE] = t.astype(o_ref.dtype)
            off += w

    return pl.pallas_call(
        body, name=name, grid=(S // tm,),
        in_specs=[pl.BlockSpec((tm, w), lambda i: (i, 0)) for w in widths] + [_row_spec(tm, LANE)] * 3,
        out_specs=_row_spec(tm, total),
        out_shape=jax.ShapeDtypeStruct((S, total), out_dtype),
        compiler_params=_params(("parallel",)),
    )(*[a for a, _ in parts], *tabs)


def _swa_band(ref_p, ref_c, hk):
    dh = SWA_HEAD_DIM
    return jnp.concatenate([ref_p[:, hk * dh:(hk + 1) * dh], ref_c[:, hk * dh:(hk + 1) * dh]], axis=0).astype(BF16)


def _swa_mask(n, G):
    qi = lax.broadcasted_iota(jnp.int32, (G * Q_BLOCK, 2 * Q_BLOCK), 0) % Q_BLOCK
    kj = lax.broadcasted_iota(jnp.int32, (G * Q_BLOCK, 2 * Q_BLOCK), 1)
    rel = qi + Q_BLOCK - kj
    return (rel >= 0) & (rel < SWA_WINDOW) & ((kj >= Q_BLOCK) | (n > 0))


def _swa_stack(ref, hk, G):
    dh = SWA_HEAD_DIM
    return jnp.concatenate([ref[:, (hk * G + g) * dh:(hk * G + g + 1) * dh] for g in range(G)], axis=0)


def _swa_unstack(ref, stacked, hk, G):
    dh, QB = SWA_HEAD_DIM, Q_BLOCK
    for g in range(0, G, 2):
        c0 = (hk * G + g) * dh
        pair = jnp.concatenate([stacked[g * QB:(g + 1) * QB], stacked[(g + 1) * QB:(g + 2) * QB]], axis=1)
        ref[:, c0:c0 + 2 * dh] = pair.astype(ref.dtype)


def _swa_sink_rows(sink_ref, hk, G):
    return jnp.concatenate([jnp.broadcast_to(sink_ref[0:1, hk * G + g:hk * G + g + 1], (Q_BLOCK, 1)) for g in range(G)],
                           axis=0)


def _swa_attn_fwd(qk, proj, sinks, Hq, name):
    S = qk.shape[0]
    dh, G, QB = SWA_HEAD_DIM, SWA_GROUP, Q_BLOCK
    Hk = Hq // G
    Wq, Wk = Hq * dh, Hk * dh
    nb = S // QB
    scale = dh ** -0.5

    def body(q_ref, kp_ref, kc_ref, vp_ref, vc_ref, sink_ref, o_ref, lse_ref):
        n = pl.program_id(0)
        mask = _swa_mask(n, G)
        lane = lax.broadcasted_iota(jnp.int32, (QB, LANE), 1)
        lse_tile = jnp.zeros((QB, LANE), F32)
        for hk in range(Hk):
            kb = _swa_band(kp_ref, kc_ref, hk)
            vb = _swa_band(vp_ref, vc_ref, hk)
            q = _swa_stack(q_ref, hk, G).astype(BF16)
            sk = _swa_sink_rows(sink_ref, hk, G)
            s = lax.dot_general(q, kb, (((1,), (1,)), ((), ())), preferred_element_type=F32) * scale
            s = jnp.where(mask, s, -jnp.inf)
            m = jnp.maximum(jnp.max(s, axis=-1, keepdims=True), sk)
            p = jnp.exp(s - m)
            l = jnp.sum(p, axis=-1, keepdims=True) + jnp.exp(sk - m)
            o = jnp.dot(p.astype(BF16), vb, preferred_element_type=F32) / l
            lse = m + jnp.log(l)
            for g in range(G):
                lse_tile = jnp.where(lane == hk * G + g, lse[g * QB:(g + 1) * QB], lse_tile)
            _swa_unstack(o_ref, o, hk, G)
        lse_ref[...] = lse_tile

    kcol, vcol = Wq // Wk, (Wq + Wk) // Wk
    return pl.pallas_call(
        body, name=name, grid=(nb,),
        in_specs=[pl.BlockSpec((QB, Wq), lambda n: (n, 0)),
                  pl.BlockSpec((QB, Wk), lambda n: (jnp.maximum(n - 1, 0), kcol)),
                  pl.BlockSpec((QB, Wk), lambda n: (n, kcol)),
                  pl.BlockSpec((QB, Wk), lambda n: (jnp.maximum(n - 1, 0), vcol)),
                  pl.BlockSpec((QB, Wk), lambda n: (n, vcol)),
                  pl.BlockSpec((1, LANE), lambda n: (0, 0))],
        out_specs=[pl.BlockSpec((QB, Wq), lambda n: (n, 0)), pl.BlockSpec((QB, LANE), lambda n: (n, 0))],
        out_shape=[jax.ShapeDtypeStruct((S, Wq), BF16), jax.ShapeDtypeStruct((S, LANE), F32)],
        compiler_params=_params(("parallel",)),
    )(qk, qk, qk, proj, proj, sinks)


def _swa_attn_bwd(qk, proj, sinks, do, lse, Hq, after, name):
    S = qk.shape[0]
    dh, G, QB = SWA_HEAD_DIM, SWA_GROUP, Q_BLOCK
    Hk = Hq // G
    Wq, Wk = Hq * dh, Hk * dh
    nb = S // QB
    scale = dh ** -0.5

    def body(q_ref, kp_ref, kc_ref, vp_ref, vc_ref, sink_ref, do_ref, lse_ref, after_ref,
             dq_ref, dk_ref, dv_ref, dsink_ref, carry_k, carry_v):
        n = pl.program_id(0)

        @pl.when(n == 0)
        def _():
            dsink_ref[...] = jnp.zeros_like(dsink_ref)

        @pl.when(n < nb)
        def _():
            mask = _swa_mask(n, G)
            lane = lax.broadcasted_iota(jnp.int32, (1, LANE), 1)
            dsink = jnp.zeros((1, LANE), F32)
            dk_heads, dv_heads = [], []
            for hk in range(Hk):
                kb = _swa_band(kp_ref, kc_ref, hk)
                vb = _swa_band(vp_ref, vc_ref, hk)
                q = _swa_stack(q_ref, hk, G).astype(BF16)
                do_s = _swa_stack(do_ref, hk, G).astype(BF16)
                lse = jnp.concatenate([lse_ref[:, hk * G + g:hk * G + g + 1] for g in range(G)], axis=0)
                s = lax.dot_general(q, kb, (((1,), (1,)), ((), ())), preferred_element_type=F32) * scale
                s = jnp.where(mask, s, -jnp.inf)
                p = jnp.exp(s - lse)
                p_sink = jnp.exp(_swa_sink_rows(sink_ref, hk, G) - lse)
                dp = lax.dot_general(do_s, vb, (((1,), (1,)), ((), ())), preferred_element_type=F32)
                delta = jnp.sum(p * dp, axis=-1, keepdims=True)
                ds_b = (p * (dp - delta)).astype(BF16)
                _swa_unstack(dq_ref, jnp.dot(ds_b, kb, preferred_element_type=F32) * scale, hk, G)
                dk_heads.append(lax.dot_general(ds_b, q, (((0,), (0,)), ((), ())), preferred_element_type=F32) * scale)
                dv_heads.append(lax.dot_general(p.astype(BF16), do_s, (((0,), (0,)), ((), ())), preferred_element_type=F32))
                sink_term = p_sink * delta
                for g in range(G):
                    dsink = jnp.where(lane == hk * G + g,
                                      -jnp.sum(sink_term[g * QB:(g + 1) * QB], axis=0, keepdims=True), dsink)
            dsink_ref[...] += dsink
            dk_all = jnp.concatenate(dk_heads, axis=1)
            dv_all = jnp.concatenate(dv_heads, axis=1)

            @pl.when(n > 0)
            def _():
                dk_ref[...] = carry_k[...] + dk_all[:QB]
                dv_ref[...] = carry_v[...] + dv_all[:QB]

            carry_k[...] = dk_all[QB:]
            carry_v[...] = dv_all[QB:]

        @pl.when(n == nb)
        def _():
            dk_ref[...] = carry_k[...]
            dv_ref[...] = carry_v[...]

    kcol, vcol = Wq // Wk, (Wq + Wk) // Wk
    cur = lambda n: jnp.minimum(n, nb - 1)
    prev = lambda n: jnp.maximum(jnp.minimum(n, nb - 1) - 1, 0)
    return pl.pallas_call(
        body, name=name, grid=(nb + 1,),
        in_specs=[pl.BlockSpec((QB, Wq), lambda n: (cur(n), 0)),
                  pl.BlockSpec((QB, Wk), lambda n: (prev(n), kcol)),
                  pl.BlockSpec((QB, Wk), lambda n: (cur(n), kcol)),
                  pl.BlockSpec((QB, Wk), lambda n: (prev(n), vcol)),
                  pl.BlockSpec((QB, Wk), lambda n: (cur(n), vcol)),
                  pl.BlockSpec((1, LANE), lambda n: (0, 0)),
                  pl.BlockSpec((QB, Wq), lambda n: (cur(n), 0)),
                  pl.BlockSpec((QB, LANE), lambda n: (cur(n), 0)),
                  pl.BlockSpec(memory_space=pl.ANY)],
        out_specs=[pl.BlockSpec((QB, Wq), lambda n: (cur(n), 0)),
                   pl.BlockSpec((QB, Wk), lambda n: (jnp.maximum(n - 1, 0), 0)),
                   pl.BlockSpec((QB, Wk), lambda n: (jnp.maximum(n - 1, 0), 0)),
                   pl.BlockSpec((1, LANE), lambda n: (0, 0))],
        out_shape=[jax.ShapeDtypeStruct((S, Wq), F32), jax.ShapeDtypeStruct((S, Wk), F32),
                   jax.ShapeDtypeStruct((S, Wk), F32), jax.ShapeDtypeStruct((1, LANE), F32)],
        scratch_shapes=[pltpu.VMEM((QB, Wk), F32), pltpu.VMEM((QB, Wk), F32)],
        compiler_params=_params(("arbitrary",)),
    )(qk, qk, qk, proj, proj, sinks, do, lse, after)


ADA_ROWS = 16


def _ada_mod(c_pad, w, b, name):
    L, D, N = w.shape
    tn = _tile(N, 512)

    def body(c_ref, w_ref, b_ref, o_ref):
        c = c_ref[...]
        c = c * _sigmoid(c)
        ch = c.astype(BF16)
        cl = (c - ch.astype(F32)).astype(BF16)
        ww = w_ref[...]
        wh = ww.astype(BF16)
        wl = (ww - wh.astype(F32)).astype(BF16)
        acc = jnp.dot(ch, wh, preferred_element_type=F32)
        acc += jnp.dot(ch, wl, preferred_element_type=F32)
        acc += jnp.dot(cl, wh, preferred_element_type=F32)
        o_ref[...] = acc + b_ref[...]

    return pl.pallas_call(
        body, name=name, grid=(L, N // tn),
        in_specs=[pl.BlockSpec((ADA_ROWS, D), lambda l, j: (0, 0)),
                  pl.BlockSpec((None, D, tn), lambda l, j: (l, 0, j)),
                  pl.BlockSpec((None, 1, tn), lambda l, j: (l, 0, j))],
        out_specs=pl.BlockSpec((None, ADA_ROWS, tn), lambda l, j: (l, 0, j)),
        out_shape=jax.ShapeDtypeStruct((L, ADA_ROWS, N), F32),
        compiler_params=_params(("parallel", "parallel")),
    )(c_pad, w, b)


def _ada_bwd(cT, dm, name):
    D = cT.shape[0]
    L, B, N = dm.shape
    tm = _tile(D, 256)

    def body(c_ref, dm_ref, o_ref):
        c = c_ref[...]
        c = c * _sigmoid(c)
        dmv = dm_ref[...]
        acc = c[:, 0:1] * dmv[0:1, :]
        for b in range(1, B):
            acc += c[:, b:b + 1] * dmv[b:b + 1, :]
        o_ref[...] = acc

    return pl.pallas_call(
        body, name=name, grid=(L, D // tm),
        in_specs=[pl.BlockSpec((tm, B), lambda l, i: (i, 0)), pl.BlockSpec((None, B, N), lambda l, i: (l, 0, 0))],
        out_specs=pl.BlockSpec((None, tm, N), lambda l, i: (l, i, 0)),
        out_shape=jax.ShapeDtypeStruct((L, D, N), F32),
        compiler_params=_params(("parallel", "parallel")),
    )(cT, dm)


def _adamw_math(w, g, m, v):
    m = ADAM_B1 * m + (1.0 - ADAM_B1) * g
    v = ADAM_B2 * v + (1.0 - ADAM_B2) * (g * g)
    m_hat = m / (1.0 - ADAM_B1 ** ADAM_STEP)
    v_hat = v / (1.0 - ADAM_B2 ** ADAM_STEP)
    delta = -ADAM_LR * (m_hat / (jnp.sqrt(v_hat) + ADAM_EPS) + ADAM_WD * w)
    return delta, m, v


def _adam_rows(R, C):
    lanes = -(-C // LANE) * LANE
    return _tile(R, max(8, (262144 // lanes) // 8 * 8), 8)


def _adam_sum(recv, w, m, v, layer, filled, name, by_cols=False):
    P, R, C = recv.shape
    L = w.shape[0]
    n_keep = 0 if filled is None else 4
    if by_cols:
        tc = _tile(C, 256)
        grid, spec = (C // tc,), pl.BlockSpec((None, R, tc), lambda i: (layer, 0, i))
        recv_spec = pl.BlockSpec((P, R, tc), lambda i: (0, 0, i))
    else:
        tr = _adam_rows(R, C)
        grid, spec = (R // tr,), pl.BlockSpec((None, tr, C), lambda i: (layer, i, 0))
        recv_spec = pl.BlockSpec((P, tr, C), lambda i: (0, i, 0))

    def body(r_ref, w_ref, m_ref, v_ref, *rest):
        g_ref, d_ref, mo_ref, vo_ref = rest[n_keep:]
        g = r_ref[0].astype(F32)
        for p in range(1, P):
            g = g + r_ref[p].astype(F32)
        delta, mn, vn = _adamw_math(w_ref[...], g, m_ref[...], v_ref[...])
        g_ref[...] = g
        d_ref[...] = delta
        mo_ref[...] = mn
        vo_ref[...] = vn

    return pl.pallas_call(
        body, name=name, grid=grid,
        in_specs=[recv_spec, spec, spec, spec] + [pl.BlockSpec(memory_space=pl.ANY)] * n_keep,
        out_specs=[spec] * 4,
        out_shape=[jax.ShapeDtypeStruct((L, R, C), F32)] * 4,
        input_output_aliases={4 + k: k for k in range(n_keep)},
        compiler_params=_params(("parallel",)),
    )(recv, w, m, v, *(filled or ()))


def _adam(g, w, m, v, name):
    L, R, C = w.shape
    tr = _adam_rows(R, C)

    def body(g_ref, w_ref, m_ref, v_ref, d_ref, mo_ref, vo_ref):
        delta, mn, vn = _adamw_math(w_ref[...], g_ref[...], m_ref[...], v_ref[...])
        d_ref[...] = delta
        mo_ref[...] = mn
        vo_ref[...] = vn

    spec = pl.BlockSpec((None, tr, C), lambda l, i: (l, i, 0))
    return pl.pallas_call(
        body, name=name, grid=(L, R // tr),
        in_specs=[spec] * 4, out_specs=[spec] * 3,
        out_shape=[jax.ShapeDtypeStruct((L, R, C), F32)] * 3,
        compiler_params=_params(("parallel", "parallel")),
    )(g, w, m, v)


def _sum_slots(x, after, name):
    P, R, C = x.shape

    def body(x_ref, after_ref, o_ref):
        acc = x_ref[0]
        for p in range(1, P):
            acc = acc + x_ref[p]
        o_ref[...] = acc

    return pl.pallas_call(
        body, name=name,
        in_specs=[pl.BlockSpec(memory_space=pltpu.VMEM), pl.BlockSpec(memory_space=pl.ANY)],
        out_specs=pl.BlockSpec(memory_space=pltpu.VMEM),
        out_shape=jax.ShapeDtypeStruct((R, C), F32),
        compiler_params=pltpu.CompilerParams(vmem_limit_bytes=48 * MIB),
    )(x, after)


def _my_pos():
    return lax.axis_index("x"), lax.axis_index("y"), lax.axis_index("c")


def _all_gather_small(x, name, after=()):
    R, C = x.shape
    n_after = len(after)

    def body(x_ref, *rest):
        out_ref, send_sems, recv_sems = rest[n_after:]
        x_, y_, c_ = _my_pos()
        me, sibling = (x_, y_, c_), (x_, y_, 1 - c_)
        chips = [(1 - x_, y_), (x_, 1 - y_), (1 - x_, 1 - y_)]

        def slot(px, py, pc):
            return out_ref.at[4 * px + 2 * py + pc]

        def copy(k, block, to):
            return pltpu.make_async_remote_copy(
                src_ref=slot(*block), dst_ref=slot(*block), send_sem=send_sems.at[k], recv_sem=recv_sems.at[k],
                device_id=to, device_id_type=MESH)

        out_ref[4 * x_ + 2 * y_ + c_] = x_ref[...]
        first = [copy(0, me, sibling)] + [copy(1 + j, me, (*chip, c_)) for j, chip in enumerate(chips)]
        for cp in first:
            cp.start()
        passed = [copy(4 + j, (*chip, c_), sibling) for j, chip in enumerate(chips)]
        for j, chip in enumerate(chips):
            copy(1 + j, (*chip, c_), me).wait_recv()
            passed[j].start()
        copy(0, sibling, me).wait_recv()
        for j, chip in enumerate(chips):
            copy(4 + j, (*chip, 1 - c_), me).wait_recv()
        for cp in first + passed:
            cp.wait_send()

    return pl.pallas_call(
        body, name=name,
        in_specs=[pl.BlockSpec(memory_space=pltpu.VMEM)] + [pl.BlockSpec(memory_space=pl.ANY)] * n_after,
        out_specs=pl.BlockSpec(memory_space=pltpu.VMEM),
        out_shape=jax.ShapeDtypeStruct((N_DEV, R, C), x.dtype),
        scratch_shapes=[pltpu.SemaphoreType.DMA((7,)), pltpu.SemaphoreType.DMA((7,))],
        compiler_params=pltpu.CompilerParams(vmem_limit_bytes=48 * MIB),
    )(x, *after)


HBM_SPEC = pl.BlockSpec(memory_space=pltpu.HBM)
SEM_SPEC = pl.BlockSpec(memory_space=pltpu.SEMAPHORE)
ANY_SPEC = pl.BlockSpec(memory_space=pl.ANY)
SPLIT_EFFECT = pltpu.SideEffectType.DATAFLOW_SIDE_EFFECTING


def _in_hbm(a):
    return pltpu.with_memory_space_constraint(a, pltpu.HBM)


def _gathered_shape(a, kind):
    if kind == "major":
        return (N_DEV,) + a.shape
    if kind == "rows":
        return (N_DEV * a.shape[0], a.shape[1])
    return (2, a.shape[0], 4 * a.shape[1])


def _gather_slot(ref, kind, block, shard_shape):
    px, py, pc = block
    if kind == "major":
        return ref.at[4 * px + 2 * py + pc]
    if kind == "rows":
        r = shard_shape[0]
        return ref.at[pl.ds(pl.multiple_of((4 * px + 2 * py + pc) * r, r), r), :]
    cu = shard_shape[1]
    return ref.at[px, :, pl.ds(pl.multiple_of((2 * py + pc) * cu, cu), cu)]


def _gather_peers():
    x_, y_, c_ = _my_pos()
    return (x_, y_, c_), (x_, y_, 1 - c_), [(1 - x_, y_), (x_, 1 - y_), (1 - x_, 1 - y_)]


def _gather_start(shards, kinds, after, name):
    n = len(shards)
    bufs = [lax.empty(_gathered_shape(a, k), a.dtype) for a, k in zip(shards, kinds)]
    extra = [] if after is None else [after]

    def body(*refs):
        shard_refs, buf_refs = refs[:n], refs[n:2 * n]
        send_sems, recv_sems, local_sems = refs[2 * n + len(extra):2 * n + len(extra) + 3]
        token = refs[-1]
        me, sibling, chips = _gather_peers()
        for e in range(n):
            mine = _gather_slot(buf_refs[e], kinds[e], me, shards[e].shape)
            pltpu.make_async_copy(shard_refs[e], mine, local_sems.at[e]).start()
            for k, to in enumerate([sibling] + [(*chip, me[2]) for chip in chips]):
                pltpu.make_async_remote_copy(
                    src_ref=shard_refs[e], dst_ref=mine, send_sem=send_sems.at[4 * e + k],
                    recv_sem=recv_sems.at[4 * e + k], device_id=to, device_id_type=MESH).start()
        token[...] = jnp.zeros_like(token)

    out = pl.pallas_call(
        body, name=name,
        out_shape=(pltpu.SemaphoreType.DMA((4 * n,)), pltpu.SemaphoreType.DMA((4 * n,)), pltpu.SemaphoreType.DMA((n,)),
                   *[pltpu.HBM(a.shape, a.dtype) for a in shards], *[pltpu.HBM(a.shape, a.dtype) for a in bufs],
                   jax.ShapeDtypeStruct((8, LANE), F32)),
        in_specs=[HBM_SPEC] * (2 * n) + [ANY_SPEC] * len(extra),
        out_specs=(SEM_SPEC, SEM_SPEC, SEM_SPEC, *[HBM_SPEC] * (2 * n), pl.BlockSpec(memory_space=pltpu.VMEM)),
        input_output_aliases={i: 3 + i for i in range(2 * n)},
        compiler_params=pltpu.CompilerParams(has_side_effects=SPLIT_EFFECT),
    )(*[_in_hbm(a) for a in shards], *[_in_hbm(a) for a in bufs], *extra)
    return out[0], out[1], out[2], out[3:3 + n], out[3 + n:3 + 2 * n], out[-1]


def _gather_forward(recv_sems, bufs, kinds, shard_shapes, after, name):
    n = len(bufs)

    def body(*refs):
        buf_refs, recv_in = refs[:n], refs[n]
        fsend, frecv = refs[n + 2], refs[n + 3]
        token = refs[-1]
        me, sibling, chips = _gather_peers()
        for e in range(n):
            for j, chip in enumerate(chips):
                slot = _gather_slot(buf_refs[e], kinds[e], (*chip, me[2]), shard_shapes[e])
                pltpu.make_async_remote_copy(
                    src_ref=slot, dst_ref=slot, send_sem=recv_in.at[4 * e + 1 + j], recv_sem=recv_in.at[4 * e + 1 + j],
                    device_id=me, device_id_type=MESH).wait_recv()
                pltpu.make_async_remote_copy(
                    src_ref=slot, dst_ref=slot, send_sem=fsend.at[3 * e + j], recv_sem=frecv.at[3 * e + j],
                    device_id=sibling, device_id_type=MESH).start()
        token[...] = jnp.zeros_like(token)

    out = pl.pallas_call(
        body, name=name,
        out_shape=(pltpu.SemaphoreType.DMA((3 * n,)), pltpu.SemaphoreType.DMA((3 * n,)),
                   *[pltpu.HBM(a.shape, a.dtype) for a in bufs], jax.ShapeDtypeStruct((8, LANE), F32)),
        in_specs=[HBM_SPEC] * n + [SEM_SPEC, ANY_SPEC],
        out_specs=(SEM_SPEC, SEM_SPEC, *[HBM_SPEC] * n, pl.BlockSpec(memory_space=pltpu.VMEM)),
        input_output_aliases={i: 2 + i for i in range(n)},
        compiler_params=pltpu.CompilerParams(has_side_effects=SPLIT_EFFECT),
    )(*bufs, recv_sems, after)
    return out[0], out[1], out[2:2 + n], out[-1]


def _gather_wait(send_sems, recv_sems, local_sems, fsend, frecv, shards, bufs, kinds, after, name):
    n = len(bufs)

    def body(*refs):
        shard_refs, buf_refs = refs[:n], refs[n:2 * n]
        send_in, recv_in, local_in, fsend_in, frecv_in = refs[2 * n:2 * n + 5]
        me, sibling, chips = _gather_peers()

        def arrival(slot, sem):
            return pltpu.make_async_remote_copy(src_ref=slot, dst_ref=slot, send_sem=sem, recv_sem=sem,
                                                device_id=me, device_id_type=MESH)

        for e in range(n):
            shape = shards[e].shape
            mine = _gather_slot(buf_refs[e], kinds[e], me, shape)
            pltpu.make_async_copy(shard_refs[e], mine, local_in.at[e]).wait()
            arrival(_gather_slot(buf_refs[e], kinds[e], sibling, shape), recv_in.at[4 * e]).wait_recv()
            for j, chip in enumerate(chips):
                arrival(_gather_slot(buf_refs[e], kinds[e], (*chip, 1 - me[2]), shape), frecv_in.at[3 * e + j]).wait_recv()
            for k in range(4):
                arrival(mine, send_in.at[4 * e + k]).wait_send()
            for j in range(3):
                arrival(mine, fsend_in.at[3 * e + j]).wait_send()

    out = pl.pallas_call(
        body, name=name,
        out_shape=(*[pltpu.HBM(a.shape, a.dtype) for a in shards], *[pltpu.HBM(a.shape, a.dtype) for a in bufs]),
        in_specs=[HBM_SPEC] * (2 * n) + [SEM_SPEC] * 5 + [ANY_SPEC],
        out_specs=tuple([HBM_SPEC] * (2 * n)),
        input_output_aliases={i: i for i in range(2 * n)},
        compiler_params=pltpu.CompilerParams(has_side_effects=SPLIT_EFFECT),
    )(*shards, *bufs, send_sems, recv_sems, local_sems, fsend, frecv, after)
    return out[n:]


def _grad_slice(ref, kind, j):
    if kind == "major":
        return ref.at[j]
    if kind == "rows":
        r = ref.shape[0] // N_DEV
        return ref.at[pl.ds(j * r, r), :]
    cu = ref.shape[2] // 4
    return ref.at[j // 4, :, pl.ds((j % 4) * cu, cu)]


def _slice_shape(a, kind):
    if kind == "major":
        return a.shape[1:]
    if kind == "rows":
        return (a.shape[0] // N_DEV, a.shape[1])
    return (a.shape[1], a.shape[2] // 4)


def _scatter_copies(srcs, lands, kinds, send_sems, recv_sems):
    x_, y_, c_ = _my_pos()
    me = 4 * x_ + 2 * y_ + c_
    n = len(srcs)

    def remote(e, j):
        return pltpu.make_async_remote_copy(
            src_ref=_grad_slice(srcs[e], kinds[e], j), dst_ref=lands[e].at[me],
            send_sem=send_sems.at[e * N_DEV + j], recv_sem=recv_sems.at[e * N_DEV + me],
            device_id=(j // 4, (j // 2) % 2, j % 2), device_id_type=MESH)

    def local(e, j):
        return pltpu.make_async_copy(_grad_slice(srcs[e], kinds[e], j), lands[e].at[j], recv_sems.at[e * N_DEV + j])

    def arrival(e, i):
        return pltpu.make_async_remote_copy(
            src_ref=_grad_slice(srcs[e], kinds[e], i), dst_ref=lands[e].at[i],
            send_sem=send_sems.at[e * N_DEV + i], recv_sem=recv_sems.at[e * N_DEV + i],
            device_id=(i // 4, (i // 2) % 2, i % 2), device_id_type=MESH)

    def start():
        for e in range(n):
            for j in range(N_DEV):
                @pl.when(me == j)
                def _():
                    local(e, j).start()

                @pl.when(me != j)
                def _():
                    remote(e, j).start()

    def wait():
        for e in range(n):
            for i in range(N_DEV):
                @pl.when(me == i)
                def _():
                    local(e, i).wait()

                @pl.when(me != i)
                def _():
                    arrival(e, i).wait_recv()
        for e in range(n):
            for j in range(N_DEV):
                @pl.when(me != j)
                def _():
                    remote(e, j).wait_send()

    return start, wait


def _scatter_start(srcs, kinds, after, name):
    n = len(srcs)
    lands = [lax.empty((N_DEV,) + _slice_shape(a, k), a.dtype) for a, k in zip(srcs, kinds)]
    extra = [] if after is None else [after]

    def body(*refs):
        src_refs, land_refs = refs[:n], refs[n:2 * n]
        send_sems, recv_sems = refs[2 * n + len(extra)], refs[2 * n + len(extra) + 1]
        token = refs[-1]
        start, _ = _scatter_copies(src_refs, land_refs, kinds, send_sems, recv_sems)
        start()
        token[...] = jnp.zeros_like(token)

    out = pl.pallas_call(
        body, name=name,
        out_shape=(pltpu.SemaphoreType.DMA((n * N_DEV,)), pltpu.SemaphoreType.DMA((n * N_DEV,)),
                   *[pltpu.HBM(a.shape, a.dtype) for a in srcs], *[pltpu.HBM(a.shape, a.dtype) for a in lands],
                   jax.ShapeDtypeStruct((8, LANE), F32)),
        in_specs=[HBM_SPEC] * (2 * n) + [ANY_SPEC] * len(extra),
        out_specs=(SEM_SPEC, SEM_SPEC, *[HBM_SPEC] * (2 * n), pl.BlockSpec(memory_space=pltpu.VMEM)),
        input_output_aliases={i: 2 + i for i in range(2 * n)},
        compiler_params=pltpu.CompilerParams(has_side_effects=SPLIT_EFFECT),
    )(*[_in_hbm(a) for a in srcs], *[_in_hbm(a) for a in lands], *extra)
    return out[0], out[1], out[2:2 + n], out[2 + n:2 + 2 * n], out[-1]


def _scatter_wait(send_sems, recv_sems, srcs, lands, kinds, after, name):
    n = len(srcs)

    def body(*refs):
        src_refs, land_refs = refs[:n], refs[n:2 * n]
        _, wait = _scatter_copies(src_refs, land_refs, kinds, refs[2 * n], refs[2 * n + 1])
        wait()

    out = pl.pallas_call(
        body, name=name,
        out_shape=(*[pltpu.HBM(a.shape, a.dtype) for a in srcs], *[pltpu.HBM(a.shape, a.dtype) for a in lands]),
        in_specs=[HBM_SPEC] * (2 * n) + [SEM_SPEC, SEM_SPEC] + [ANY_SPEC] * len(after),
        out_specs=tuple([HBM_SPEC] * (2 * n)),
        input_output_aliases={i: i for i in range(2 * n)},
        compiler_params=pltpu.CompilerParams(has_side_effects=SPLIT_EFFECT),
    )(*srcs, *lands, send_sems, recv_sems, *after)
    return out[n:]


def _rope_tables(positions, sign):
    half = ROPE_DIM // 2
    inv_freq = ROPE_THETA ** (-jnp.arange(0, ROPE_DIM, 2, dtype=F32) / ROPE_DIM)
    ang = positions.astype(F32)[:, None] * inv_freq
    reps = LANE // half
    cos = jnp.tile(jnp.cos(ang), (1, reps))
    sin = jnp.tile(jnp.sin(ang), (1, reps)) * sign
    d = jnp.arange(LANE) % SWA_HEAD_DIM
    return (jnp.where(d < ROPE_DIM, cos, 1.0), jnp.where(d < half, -sin, 0.0),
            jnp.where((d >= half) & (d < ROPE_DIM), sin, 0.0))


def _pad_cols(a, n):
    return jnp.pad(a, ((0, 0), (0, n - a.shape[1])))


def _local_step(x, target, positions, mods, fetch, P, on_grads, on_small):
    S, D = x.shape
    Hf = D // FOX_HEAD_DIM
    Hq = D // SWA_HEAD_DIM
    Hk = Hq // SWA_GROUP
    Wk = Hk * SWA_HEAD_DIM
    n_in = 3 * D + Hf
    (sh1a, sc1a, g1a, sh2a, sc2a, g2a), (sh1b, sc1b, g1b, sh2b, sc2b, g2b) = mods
    row = lambda v: v.reshape(1, -1)
    cw = [jnp.transpose(P["conv_w"][l].reshape(3, 2, -1), (1, 0, 2)) for l in range(2)]
    cb = [P["conv_b"][l].reshape(2, 1, -1) for l in range(2)]

    W = dict(up=[None, None], down=[None, None])
    h1a = _modulate(x, sc1a, sh1a, "modulate_in")
    W["fox_in"], W["fox_o"] = fetch("fox", "wait", h1a)
    proj_a = _mm_nn(h1a, W["fox_in"], F32, "fox_in_proj", tn=896)
    flT = proj_a[:, 3 * D:n_in].T
    bf_col = P["fox_b_f"].reshape(Hf, 1)
    cumT = _fox_prep(flT, bf_col, "fox_cumsum")
    cum_row = cumT.reshape(Hf, 1, S)
    o_a, lse_a = _fox_attn_fwd(proj_a, cum_row, Hf, "fox_attn_fwd")
    token = fetch("ffn0", "forward", o_a)
    y1a = _mm_nn(o_a, W["fox_o"], F32, "fox_out_proj")
    z1a, x1, h2a = _ln_fwd(x, y1a, g1a, row(P["ln_mix_g"][0]), row(P["ln_mix_b"][0]), sc2a, sh2a, token, "ln_mix0")
    W["up"][0], W["down"][0] = fetch("ffn0", "wait", h2a)
    u_a, a_a = _ffn_up(h2a, W["up"][0], cw[0], cb[0], "ffn_up0")
    token = fetch("swa", "forward", a_a)
    y2a = _mm_nn(a_a, W["down"][0], F32, "ffn_down0", tk=1408)
    z2a, x2, h1b = _ln_fwd(x1, y2a, g2a, row(P["ln_ffn_g"][0]), row(P["ln_ffn_b"][0]), sc1b, sh1b, token, "ln_ffn0")

    W["swa_in"], W["swa_o"] = fetch("swa", "wait", h1b)
    proj_b = _mm_nn(h1b, W["swa_in"], F32, "swa_in_proj")
    tabs_f = _rope_tables(positions, 1.0)
    tabs_b = _rope_tables(positions, -1.0)
    qk = _rope([(proj_b[:, :D + Wk], True)], tabs_f, F32, "rope_fwd")
    sinks = _pad_cols(P["swa_sinks"].reshape(1, Hq), LANE)
    o_b, lse_b = _swa_attn_fwd(qk, proj_b, sinks, Hq, "swa_attn_fwd")
    token = fetch("ffn1", "forward", o_b)
    y1b = _mm_nn(o_b, W["swa_o"], F32, "swa_out_proj")
    z1b, x3, h2b = _ln_fwd(x2, y1b, g1b, row(P["ln_mix_g"][1]), row(P["ln_mix_b"][1]), sc2b, sh2b, token, "ln_mix1")
    W["up"][1], W["down"][1] = fetch("ffn1", "wait", h2b)
    u_b, a_b = _ffn_up(h2b, W["up"][1], cw[1], cb[1], "ffn_up1")
    y2b = _mm_nn(a_b, W["down"][1], F32, "ffn_down1", tk=1408)
    z2b, dout, loss_row = _ln_fwd_loss(x3, y2b, g2b, row(P["ln_ffn_g"][1]), row(P["ln_ffn_b"][1]), target, "ln_ffn1_loss")

    def ffn_backward(dy, a, u, h_in, l, tag):
        d_down = _mm_tn(a, dy[None], BF16, "ffn_dwdown" + tag, tm=1408, tn=1024)[0]
        token = on_grads("ffn_w_down" + tag, d_down)
        da = _mm_nt(dy[None], W["down"][l][None], F32, "ffn_da" + tag)
        du, dcw, dcb = _ffn_bwd_elem(da, u, cw[l], cb[l], token, "ffn_bwd_elem" + tag)
        d_up = _mm_tn(h_in, du, BF16, "ffn_dwup" + tag)
        token = on_grads("ffn_w_up" + tag, d_up)
        dh = _mm_nt(du, W["up"][l], F32, "ffn_dh" + tag, tk=1408)
        return dh, token, jnp.transpose(dcw, (1, 0, 2)).reshape(3, -1), dcb.reshape(-1)

    dz2b, dy2b, dg_f1, db_f1, dgate2b = _ln_bwd(dout, z2b, y2b, g2b, row(P["ln_ffn_g"][1]), "ln_ffn1_bwd")
    dh2b, token, dcw1, dcb1 = ffn_backward(dy2b, a_b, u_b, h2b, 1, "1")
    dx3, dsc2b, dsh2b = _mod_bwd(dz2b, dh2b, x3, sc2b, token, "mod_ffn1_bwd")

    dz1b, dy1b, dg_m1, db_m1, dgate1b = _ln_bwd(dx3, z1b, y1b, g1b, row(P["ln_mix_g"][1]), "ln_mix1_bwd")
    token = on_grads("swa_w_o", _mm_tn(o_b, dy1b[None], BF16, "swa_dwo")[0])
    do_b = _mm_nt(dy1b[None], W["swa_o"][None], F32, "swa_do")
    dq_b, dk_b, dv_b, dsinks = _swa_attn_bwd(qk, proj_b, sinks, do_b, lse_b, Hq, token, "swa_attn_bwd")
    dproj_b = _rope([(dq_b, True), (dk_b, True), (dv_b, False)], tabs_b, BF16, "rope_bwd")
    token = on_grads("swa_w_in", _mm_tn(h1b, dproj_b[None], BF16, "swa_dwin")[0])
    dh1b = _mm_nt(dproj_b[None], W["swa_in"][None], F32, "swa_dh", tk=1280)
    dx2, dsc1b, dsh1b = _mod_bwd(dz1b, dh1b, x2, sc1b, token, "mod_mix1_bwd")

    dz2a, dy2a, dg_f0, db_f0, dgate2a = _ln_bwd(dx2, z2a, y2a, g2a, row(P["ln_ffn_g"][0]), "ln_ffn0_bwd")
    dh2a, token, dcw0, dcb0 = ffn_backward(dy2a, a_a, u_a, h2a, 0, "0")
    dx1, dsc2a, dsh2a = _mod_bwd(dz2a, dh2a, x1, sc2a, token, "mod_ffn0_bwd")

    dz1a, dy1a, dg_m0, db_m0, dgate1a = _ln_bwd(dx1, z1a, y1a, g1a, row(P["ln_mix_g"][0]), "ln_mix0_bwd")
    token = on_grads("fox_w_o", _mm_tn(o_a, dy1a[None], BF16, "fox_dwo")[0])
    do_a = _mm_nt(dy1a[None], W["fox_o"][None], F32, "fox_do")
    dq_a, dk_a, dv_a, dcum_row, dcum_col = _fox_attn_bwd(proj_a, do_a, o_a, cum_row, lse_a, Hf, token, "fox_attn_bwd")
    dflT, dbf = _fox_prep_bwd(dcum_row.reshape(Hf, S), dcum_col.reshape(Hf, S), flT, bf_col, "fox_cumsum_bwd")
    n_pad = W["fox_in"].shape[1]
    dproj_a = jnp.concatenate([dq_a, dk_a.astype(BF16), dv_a.astype(BF16),
                               _pad_cols(dflT.T, n_pad - 3 * D).astype(BF16)], axis=1)
    dh1a = _mm_nt(dproj_a[None], W["fox_in"][None], F32, "fox_dh", tk=896)
    grad_x, dsc1a, dsh1a = _mod_bwd(dz1a, dh1a, x, sc1a, token, "mod_mix0_bwd")

    dmod = jnp.stack([jnp.concatenate([dsh1a, dsc1a, dgate1a, dsh2a, dsc2a, dgate2a], axis=1)[0],
                      jnp.concatenate([dsh1b, dsc1b, dgate1b, dsh2b, dsc2b, dgate2b], axis=1)[0]])
    small = dict(dmod=dmod, conv_b=jnp.stack([dcb0, dcb1]), conv_w=jnp.stack([dcw0, dcw1]),
                 ln_mix_g=jnp.concatenate([dg_m0, dg_m1]), ln_mix_b=jnp.concatenate([db_m0, db_m1]),
                 ln_ffn_g=jnp.concatenate([dg_f0, dg_f1]), ln_ffn_b=jnp.concatenate([db_f0, db_f1]),
                 fox_b_f=dbf.reshape(-1), swa_sinks=dsinks[0, :Hq], loss=loss_row[0, 0].reshape(1))
    exchanged = on_small(small)
    on_grads("fox_w_in", _mm_tn(h1a, dproj_a[None], BF16, "fox_dwin", tn=896)[0], exchanged)
    return grad_x


SMALL_ORDER = ("dmod", "conv_b", "conv_w", "ln_mix_g", "ln_mix_b", "ln_ffn_g", "ln_ffn_b", "fox_b_f", "swa_sinks", "loss")


def _pack_rows(arrays):
    chunks, spans, off = [], [], 0
    for a in arrays:
        flat = a.reshape(-1)
        n = -(-flat.shape[0] // LANE) * LANE
        chunks.append(jnp.pad(flat, (0, n - flat.shape[0])))
        spans.append((off, flat.shape[0], a.shape))
        off += n
    total = -(-off // (8 * LANE)) * (8 * LANE)
    chunks.append(jnp.zeros((total - off,), F32))
    return jnp.concatenate(chunks).reshape(-1, LANE), spans


def _unpack_rows(packed, spans):
    flat = packed.reshape(-1)
    return [flat[off:off + n].reshape(shape) for off, n, shape in spans]


def kernel(x, c, positions, fox_w_in, fox_b_f, fox_w_o, swa_w_in, swa_sinks, swa_w_o, ada_w, ada_b, ffn_w_up, ffn_conv_w, ffn_conv_b, ffn_w_down, ln_mix_g, ln_mix_b, ln_ffn_g, ln_ffn_b, loss_target, m_fox_w_in, m_fox_b_f, m_fox_w_o, m_swa_w_in, m_swa_sinks, m_swa_w_o, m_ada_w, m_ada_b, m_ffn_w_up, m_ffn_conv_w, m_ffn_conv_b, m_ffn_w_down, m_ln_mix_g, m_ln_mix_b, m_ln_ffn_g, m_ln_ffn_b, v_fox_w_in, v_fox_b_f, v_fox_w_o, v_swa_w_in, v_swa_sinks, v_swa_w_o, v_ada_w, v_ada_b, v_ffn_w_up, v_ffn_conv_w, v_ffn_conv_b, v_ffn_w_down, v_ln_mix_g, v_ln_mix_b, v_ln_ffn_g, v_ln_ffn_b):
    S, D = x.shape[1], x.shape[2]
    L = ada_w.shape[0]
    me = 4 * lax.axis_index("x") + 2 * lax.axis_index("y") + lax.axis_index("c")
    n_ada = ada_w.shape[2]
    cu = ffn_w_up.shape[2]
    F = 4 * cu
    n_in = fox_w_in.shape[2] * N_DEV
    n_in_pad = -(-n_in // LANE) * LANE

    c_all = _all_gather_small(c.reshape(-1, LANE), "gather_c").reshape(N_DEV, D)
    b_cols = lax.dynamic_slice_in_dim(ada_b, me * n_ada, n_ada, axis=1).reshape(L, 1, n_ada)
    mod_blk = _ada_mod(jnp.pad(c_all, ((0, ADA_ROWS - N_DEV), (0, 0))), ada_w, b_cols, "ada_mod")[:, :N_DEV]
    mod_all = _all_gather_small(mod_blk.reshape(-1, LANE), "gather_mod").reshape(N_DEV, L, N_DEV, n_ada)
    mod_mine = lax.dynamic_index_in_dim(mod_all, me, axis=2, keepdims=False)
    mod_mine = jnp.transpose(mod_mine, (1, 0, 2)).reshape(L, N_DEV * n_ada)
    mods = [[mod_mine[l, k * D:(k + 1) * D].reshape(1, D) for k in range(6)] for l in range(L)]

    P = dict(fox_b_f=fox_b_f[0], swa_sinks=swa_sinks[0], conv_b=ffn_conv_b,
             ln_mix_g=ln_mix_g, ln_mix_b=ln_mix_b, ln_ffn_g=ln_ffn_g, ln_ffn_b=ln_ffn_b)
    cw_rows = _all_gather_small(_pack_rows([ffn_conv_w])[0], "gather_conv_w", after=[mod_all])
    n_cw = ffn_conv_w.size
    cw_dev = cw_rows.reshape(N_DEV, -1)[:, :n_cw].reshape(N_DEV, L, 3, cu)
    P["conv_w"] = jnp.transpose(cw_dev, (1, 2, 0, 3)).reshape(L, 3, N_DEV * cu)

    gather_groups = dict(
        fox=([fox_w_in[0].astype(BF16), fox_w_o[0].astype(BF16)], ["major", "rows"]),
        ffn0=([ffn_w_up[0].astype(BF16), ffn_w_down[0].astype(BF16)], ["halves", "rows"]),
        swa=([swa_w_in[0].astype(BF16), swa_w_o[0].astype(BF16)], ["major", "rows"]),
        ffn1=([ffn_w_up[1].astype(BF16), ffn_w_down[1].astype(BF16)], ["halves", "rows"]))
    starts_after = dict(fox=["ffn0", "swa"], ffn0=["ffn1"])
    gathers = {}

    def start_group(group, after):
        shards, kinds = gather_groups[group]
        send, recv, local, thru, bufs, token = _gather_start(shards, kinds, after, "gather_start_" + group)
        gathers[group] = dict(send=send, recv=recv, local=local, shards=thru, bufs=bufs, kinds=kinds,
                              shapes=[a.shape for a in shards], token=token)
        return token

    start_group("fox", cw_rows)

    def natural(g, pad_to=None):
        w = jnp.transpose(g, (1, 0, 2)).reshape(D, -1)
        return w if pad_to is None else _pad_cols(w, pad_to)

    def forward_stage(group, after):
        s = gathers[group]
        s["fsend"], s["frecv"], s["bufs"], token = _gather_forward(s["recv"], s["bufs"], s["kinds"], s["shapes"], after,
                                                                   "gather_forward_" + group)
        for nxt in starts_after.get(group, ()):
            token = start_group(nxt, token)
        return token

    def fetch(group, stage, after):
        if stage == "forward":
            return forward_stage(group, after)
        if group == "fox":
            after = forward_stage(group, gathers[group]["token"])
        s = gathers.pop(group)
        first, second = _gather_wait(s["send"], s["recv"], s["local"], s["fsend"], s["frecv"], s["shards"], s["bufs"],
                                     s["kinds"], after, "gather_wait_" + group)
        if group == "fox":
            return natural(first, n_in_pad), second
        if group == "swa":
            return natural(first), second
        return first, second

    out, pending = {}, {}

    def columns_major(g):
        return jnp.transpose(g.reshape(D, N_DEV, -1), (1, 2, 0))

    def transposed(a):
        return jnp.transpose(a, (0, 2, 1))

    big = dict(
        ffn_w_down1=("ffn_w_down", "rows", 1, (ffn_w_down, m_ffn_w_down, v_ffn_w_down)),
        ffn_w_up1=("ffn_w_up", "halves", 1, (ffn_w_up, m_ffn_w_up, v_ffn_w_up)),
        swa_w_o=("swa_w_o", "rows", 0, (swa_w_o, m_swa_w_o, v_swa_w_o)),
        swa_w_in=("swa_w_in", "major", 0, tuple(transposed(a) for a in (swa_w_in, m_swa_w_in, v_swa_w_in))),
        ffn_w_down0=("ffn_w_down", "rows", 0, (ffn_w_down, m_ffn_w_down, v_ffn_w_down)),
        ffn_w_up0=("ffn_w_up", "halves", 0, (ffn_w_up, m_ffn_w_up, v_ffn_w_up)),
        fox_w_o=("fox_w_o", "rows", 0, (fox_w_o, m_fox_w_o, v_fox_w_o)),
        fox_w_in=("fox_w_in", "major", 0, tuple(transposed(a) for a in (fox_w_in, m_fox_w_in, v_fox_w_in))))
    finish_at = dict(swa_w_o=["ffn_w_down1"], ffn_w_up0=["ffn_w_up1", "swa_w_o", "swa_w_in"], fox_w_o=["ffn_w_down0"],
                     fox_w_in=["ffn_w_up0"])
    tail = {}

    def finish(name, after):
        send, recv, thru, lands = pending.pop(name)
        param, kind, layer, wmv = big[name]
        landed, = _scatter_wait(send, recv, thru, lands, [kind], after, "scatter_wait_" + name)
        res = _adam_sum(landed, *wmv, layer, out.get(param), "adam_" + name, by_cols=kind == "major")
        out[param] = [transposed(r) for r in res] if kind == "major" else res

    def on_grads(name, g, after=None):
        kind = big[name][1]
        if name == "fox_w_in":
            g = g[:, :n_in]
        src = columns_major(g) if kind == "major" else g
        send, recv, thru, lands, token = _scatter_start([src], [kind], after, "scatter_start_" + name)
        pending[name] = (send, recv, thru, lands)
        for done in finish_at.get(name, ()):
            finish(done, [token])
        tail["last_start"] = token
        return token

    def on_small(small):
        packed, tail["spans"] = _pack_rows([small[k] for k in SMALL_ORDER])
        tail["gathered"] = _all_gather_small(packed, "gather_small_grads")
        return tail["gathered"]

    grad_x = _local_step(x[0], loss_target[0], positions[0], mods, fetch, P, on_grads, on_small)

    gathered, spans = tail["gathered"], tail["spans"]
    totals = dict(zip(SMALL_ORDER, _unpack_rows(_sum_slots(gathered, tail["last_start"], "sum_small_grads"), spans)))
    loss = totals["loss"].reshape(())
    n_mod = L * 6 * D
    dmod_all = gathered.reshape(N_DEV, -1)[:, :n_mod].reshape(N_DEV, L, 6 * D)
    dmod_cols = jnp.transpose(lax.dynamic_slice_in_dim(dmod_all, me * n_ada, n_ada, axis=2), (1, 0, 2))
    g_ada_w = _ada_bwd(c_all.T, dmod_cols, "ada_w_grad")
    out["ada_w"] = (g_ada_w,) + tuple(_adam(g_ada_w, ada_w, m_ada_w, v_ada_w, "adam_ada_w"))

    g_small = dict(fox_b_f=totals["fox_b_f"].reshape(fox_b_f.shape), swa_sinks=totals["swa_sinks"].reshape(swa_sinks.shape),
                   ada_b=totals["dmod"].reshape(ada_b.shape), ffn_conv_b=totals["conv_b"].reshape(ffn_conv_b.shape),
                   ffn_conv_w=lax.dynamic_slice_in_dim(totals["conv_w"].reshape(L, 3, 2 * F), me * cu, cu, axis=2),
                   ln_mix_g=totals["ln_mix_g"], ln_mix_b=totals["ln_mix_b"],
                   ln_ffn_g=totals["ln_ffn_g"], ln_ffn_b=totals["ln_ffn_b"])
    small_names = ("fox_b_f", "swa_sinks", "ada_b", "ffn_conv_b", "ffn_conv_w", "ln_mix_g", "ln_mix_b", "ln_ffn_g", "ln_ffn_b")
    w_small = dict(fox_b_f=(fox_b_f, m_fox_b_f, v_fox_b_f), swa_sinks=(swa_sinks, m_swa_sinks, v_swa_sinks),
                   ada_b=(ada_b, m_ada_b, v_ada_b), ffn_conv_b=(ffn_conv_b, m_ffn_conv_b, v_ffn_conv_b),
                   ffn_conv_w=(ffn_conv_w, m_ffn_conv_w, v_ffn_conv_w),
                   ln_mix_g=(ln_mix_g, m_ln_mix_g, v_ln_mix_g), ln_mix_b=(ln_mix_b, m_ln_mix_b, v_ln_mix_b),
                   ln_ffn_g=(ln_ffn_g, m_ln_ffn_g, v_ln_ffn_g), ln_ffn_b=(ln_ffn_b, m_ln_ffn_b, v_ln_ffn_b))
    pk_g, sp = _pack_rows([g_small[k] for k in small_names])
    pk_w = _pack_rows([w_small[k][0] for k in small_names])[0]
    pk_m = _pack_rows([w_small[k][1] for k in small_names])[0]
    pk_v = _pack_rows([w_small[k][2] for k in small_names])[0]
    res = _adam(pk_g[None], pk_w[None], pk_m[None], pk_v[None], "adam_small")
    settled = [res[0], out["ada_w"][1]] + [out[k][1] for k in ("ffn_w_up", "ffn_w_down", "swa_w_in", "swa_w_o")]
    finish("fox_w_o", settled)
    finish("fox_w_in", settled)
    res = [dict(zip(small_names, _unpack_rows(r[0], sp))) for r in res]
    for k in small_names:
        out[k] = (g_small[k], res[0][k], res[1][k], res[2][k])

    order = ("fox_w_in", "fox_b_f", "fox_w_o", "swa_w_in", "swa_sinks", "swa_w_o", "ada_w", "ada_b", "ffn_w_up",
             "ffn_conv_w", "ffn_conv_b", "ffn_w_down", "ln_mix_g", "ln_mix_b", "ln_ffn_g", "ln_ffn_b")
    return (loss, grad_x[None], *[out[k][0] for k in order], *[out[k][1] for k in order],
            *[out[k][2] for k in order], *[out[k][3] for k in order])
```

```python
import functools

import jax
import jax.numpy as jnp
from jax import lax
from jax.experimental import pallas as pl
from jax.experimental.pallas import tpu as pltpu

F32 = jnp.float32
BF16 = jnp.bfloat16
MESH = pl.DeviceIdType.MESH
N_DEV = 8
AXES = ("x", "y", "c")

DEPTH = 2
ALPHA = (2.0 * DEPTH) ** 0.25
LN_EPS = 1e-5
FOX_HEAD_DIM = 128
SWA_HEAD_DIM = 64
SWA_GROUP = 8
SWA_WINDOW = 128
Q_BLOCK = 128
ROPE_DIM = 16
ROPE_THETA = 500000.0

ADAM_LR = 0.001
ADAM_B1 = 0.9
ADAM_B2 = 0.999
ADAM_EPS = 1e-08
ADAM_WD = 0.01
ADAM_STEP = 10

LANE = 128
MIB = 1024 * 1024


def _tile(n, pref, unit=LANE):
    if n <= pref:
        return n
    t = (pref // unit) * unit
    while t >= unit:
        if n % t == 0:
            return t
        t -= unit
    return n


def _params(sem, vmem_mib=48):
    return pltpu.CompilerParams(dimension_semantics=sem, vmem_limit_bytes=vmem_mib * MIB)


def _sigmoid(x):
    return 1.0 / (1.0 + jnp.exp(-x))


def _mm_call(dot, grid_mnk, in_specs, out_spec, out_shape, k_axis, nk, tm, tn, name, operands):
    sem = ("parallel",) * (len(grid_mnk) - 1) + ("arbitrary",)

    if nk == 1:
        def body(a_ref, b_ref, o_ref):
            o_ref[...] = dot(a_ref[...], b_ref[...]).astype(o_ref.dtype)
        scratch = []
    else:
        def body(a_ref, b_ref, o_ref, acc_ref):
            k = pl.program_id(k_axis)

            @pl.when(k == 0)
            def _():
                acc_ref[...] = jnp.zeros_like(acc_ref)

            acc_ref[...] += dot(a_ref[...], b_ref[...])

            @pl.when(k == nk - 1)
            def _():
                o_ref[...] = acc_ref[...].astype(o_ref.dtype)
        scratch = [pltpu.VMEM((tm, tn), F32)]

    return pl.pallas_call(
        body, name=name, grid=grid_mnk, in_specs=in_specs, out_specs=out_spec, out_shape=out_shape,
        scratch_shapes=scratch, compiler_params=_params(sem, 56),
    )(*operands)


def _dot(dims):
    def dot(a, b):
        return lax.dot_general(a.astype(BF16), b.astype(BF16), (dims, ((), ())), preferred_element_type=F32)
    return dot


def _mm_nn(a, b, out_dtype, name, tm=2048, tn=512, tk=2048):
    M, K = a.shape
    N = b.shape[1]
    tm, tn, tk = _tile(M, tm), _tile(N, tn), _tile(K, tk)
    nk = K // tk
    return _mm_call(
        _dot(((1,), (0,))), (M // tm, N // tn, nk),
        [pl.BlockSpec((tm, tk), lambda i, j, k: (i, k)), pl.BlockSpec((tk, tn), lambda i, j, k: (k, j))],
        pl.BlockSpec((tm, tn), lambda i, j, k: (i, j)), jax.ShapeDtypeStruct((M, N), out_dtype),
        2, nk, tm, tn, name, (a, b))


def _mm_nt(a, b, out_dtype, name, tm=2048, tn=512, tk=2048):
    P, M, K = a.shape
    N = b.shape[1]
    tm, tn, tk = _tile(M, tm), _tile(N, tn), _tile(K, tk)
    nk = K // tk
    return _mm_call(
        _dot(((1,), (1,))), (M // tm, N // tn, P * nk),
        [pl.BlockSpec((None, tm, tk), lambda i, j, k: (k // nk, i, k % nk)),
         pl.BlockSpec((None, tn, tk), lambda i, j, k: (k // nk, j, k % nk))],
        pl.BlockSpec((tm, tn), lambda i, j, k: (i, j)), jax.ShapeDtypeStruct((M, N), out_dtype),
        2, P * nk, tm, tn, name, (a, b))


def _mm_tn(a, b, out_dtype, name, tm=2048, tn=512, tk=2048):
    K, M = a.shape
    P, _, N = b.shape
    tm, tn, tk = _tile(M, tm), _tile(N, tn), _tile(K, tk)
    nk = K // tk
    return _mm_call(
        _dot(((0,), (0,))), (P, M // tm, N // tn, nk),
        [pl.BlockSpec((tk, tm), lambda p, i, j, k: (k, i)), pl.BlockSpec((None, tk, tn), lambda p, i, j, k: (p, k, j))],
        pl.BlockSpec((None, tm, tn), lambda p, i, j, k: (p, i, j)), jax.ShapeDtypeStruct((P, M, N), out_dtype),
        3, nk, tm, tn, name, (a, b))


ROW_TILE = 256


def _row_spec(tm, D):
    return pl.BlockSpec((tm, D), lambda i: (i, 0))


def _vec_spec(D):
    return pl.BlockSpec((1, D), lambda i: (0, 0))


def _modulate(x, sc, sh, name):
    S, D = x.shape
    tm = _tile(S, ROW_TILE, 8)

    def body(x_ref, sc_ref, sh_ref, h_ref):
        h_ref[...] = (x_ref[...] * (1.0 + sc_ref[...]) + sh_ref[...]).astype(BF16)

    return pl.pallas_call(
        body, name=name, grid=(S // tm,),
        in_specs=[_row_spec(tm, D), _vec_spec(D), _vec_spec(D)],
        out_specs=_row_spec(tm, D),
        out_shape=jax.ShapeDtypeStruct((S, D), BF16),
        compiler_params=_params(("parallel",)),
    )(x, sc, sh)


def _layer_norm_rows(z, gamma, beta):
    mu = jnp.mean(z, axis=-1, keepdims=True)
    zc = z - mu
    var = jnp.mean(zc * zc, axis=-1, keepdims=True)
    return zc * lax.rsqrt(var + LN_EPS) * gamma + beta


def _ln_fwd(x, y, gate, gamma, beta, sc_n, sh_n, after, name):
    S, D = x.shape
    tm = _tile(S, ROW_TILE, 8)

    def body(x_ref, y_ref, gate_ref, g_ref, b_ref, sc_ref, sh_ref, after_ref, z_ref, xo_ref, hn_ref):
        z = ALPHA * x_ref[...] + (1.0 + gate_ref[...]) * y_ref[...]
        xo = _layer_norm_rows(z, g_ref[...], b_ref[...])
        z_ref[...] = z
        xo_ref[...] = xo
        hn_ref[...] = (xo * (1.0 + sc_ref[...]) + sh_ref[...]).astype(BF16)

    return pl.pallas_call(
        body, name=name, grid=(S // tm,),
        in_specs=[_row_spec(tm, D), _row_spec(tm, D)] + [_vec_spec(D)] * 5 + [pl.BlockSpec(memory_space=pl.ANY)],
        out_specs=[_row_spec(tm, D)] * 3,
        out_shape=[jax.ShapeDtypeStruct((S, D), F32), jax.ShapeDtypeStruct((S, D), F32),
                   jax.ShapeDtypeStruct((S, D), BF16)],
        compiler_params=_params(("parallel",)),
    )(x, y, gate, gamma, beta, sc_n, sh_n, after)


def _ln_fwd_loss(x, y, gate, gamma, beta, target, name):
    S, D = x.shape
    tm = _tile(S, ROW_TILE, 8)

    def body(x_ref, y_ref, gate_ref, g_ref, b_ref, t_ref, z_ref, dout_ref, loss_ref):
        @pl.when(pl.program_id(0) == 0)
        def _():
            loss_ref[...] = jnp.zeros_like(loss_ref)

        z = ALPHA * x_ref[...] + (1.0 + gate_ref[...]) * y_ref[...]
        xo = _layer_norm_rows(z, g_ref[...], b_ref[...])
        err = xo - t_ref[...]
        z_ref[...] = z
        dout_ref[...] = err * (1.0 / D)
        loss_ref[...] += (0.5 / D) * jnp.sum(err * err)

    return pl.pallas_call(
        body, name=name, grid=(S // tm,),
        in_specs=[_row_spec(tm, D), _row_spec(tm, D)] + [_vec_spec(D)] * 3 + [_row_spec(tm, D)],
        out_specs=[_row_spec(tm, D), _row_spec(tm, D), pl.BlockSpec((1, LANE), lambda i: (0, 0))],
        out_shape=[jax.ShapeDtypeStruct((S, D), F32), jax.ShapeDtypeStruct((S, D), F32),
                   jax.ShapeDtypeStruct((1, LANE), F32)],
        compiler_params=_params(("arbitrary",)),
    )(x, y, gate, gamma, beta, target)


def _ln_bwd(dout, z, y, gate, gamma, name):
    S, D = z.shape
    tm = _tile(S, ROW_TILE, 8)

    def body(dout_ref, z_ref, y_ref, gate_ref, g_ref, dz_ref, dy_ref, dg_ref, db_ref, dgate_ref):
        @pl.when(pl.program_id(0) == 0)
        def _():
            dg_ref[...] = jnp.zeros_like(dg_ref)
            db_ref[...] = jnp.zeros_like(db_ref)
            dgate_ref[...] = jnp.zeros_like(dgate_ref)

        z = z_ref[...]
        dout = dout_ref[...]
        mu = jnp.mean(z, axis=-1, keepdims=True)
        zc = z - mu
        var = jnp.mean(zc * zc, axis=-1, keepdims=True)
        rstd = lax.rsqrt(var + LN_EPS)
        xhat = zc * rstd
        dxhat = dout * g_ref[...]
        m1 = jnp.mean(dxhat, axis=-1, keepdims=True)
        m2 = jnp.mean(dxhat * xhat, axis=-1, keepdims=True)
        dz = rstd * (dxhat - m1 - xhat * m2)
        dz_ref[...] = dz
        dy_ref[...] = (dz * (1.0 + gate_ref[...])).astype(BF16)
        dg_ref[...] += jnp.sum(dout * xhat, axis=0, keepdims=True)
        db_ref[...] += jnp.sum(dout, axis=0, keepdims=True)
        dgate_ref[...] += jnp.sum(dz * y_ref[...], axis=0, keepdims=True)

    return pl.pallas_call(
        body, name=name, grid=(S // tm,),
        in_specs=[_row_spec(tm, D)] * 3 + [_vec_spec(D)] * 2,
        out_specs=[_row_spec(tm, D), _row_spec(tm, D)] + [_vec_spec(D)] * 3,
        out_shape=[jax.ShapeDtypeStruct((S, D), F32), jax.ShapeDtypeStruct((S, D), BF16)]
        + [jax.ShapeDtypeStruct((1, D), F32)] * 3,
        compiler_params=_params(("arbitrary",)),
    )(dout, z, y, gate, gamma)


def _mod_bwd(dz, dh, xin, sc, after, name):
    S, D = dz.shape
    tm = _tile(S, ROW_TILE, 8)

    def body(dz_ref, dh_ref, x_ref, sc_ref, after_ref, dx_ref, dsc_ref, dsh_ref):
        @pl.when(pl.program_id(0) == 0)
        def _():
            dsc_ref[...] = jnp.zeros_like(dsc_ref)
            dsh_ref[...] = jnp.zeros_like(dsh_ref)

        dh = dh_ref[...]
        dx_ref[...] = ALPHA * dz_ref[...] + dh * (1.0 + sc_ref[...])
        dsc_ref[...] += jnp.sum(dh * x_ref[...], axis=0, keepdims=True)
        dsh_ref[...] += jnp.sum(dh, axis=0, keepdims=True)

    return pl.pallas_call(
        body, name=name, grid=(S // tm,),
        in_specs=[_row_spec(tm, D)] * 3 + [_vec_spec(D), pl.BlockSpec(memory_space=pl.ANY)],
        out_specs=[_row_spec(tm, D), _vec_spec(D), _vec_spec(D)],
        out_shape=[jax.ShapeDtypeStruct((S, D), F32)] + [jax.ShapeDtypeStruct((1, D), F32)] * 2,
        compiler_params=_params(("arbitrary",)),
    )(dz, dh, xin, sc, after)


def _shift_down(u, k, row):
    return jnp.where(row >= k, pltpu.roll(u, k, axis=0), 0.0)


def _shift_up(u, k, row, S):
    return jnp.where(row < S - k, pltpu.roll(u, S - k, axis=0), 0.0)


def _ffn_up(h, w, cw, cb, name):
    S, D = h.shape
    F = w.shape[2]
    tn = _tile(F, 256)

    def body(h_ref, w_ref, cw_ref, cb_ref, u_ref, uc_ref, a_ref):
        hh = h_ref[...]
        row = lax.broadcasted_iota(jnp.int32, (S, tn), 0)
        conv = []
        for p in range(2):
            u = jnp.dot(hh, w_ref[p], preferred_element_type=F32)
            u_ref[p] = u.astype(BF16)
            cwp = cw_ref[p]
            uc = _shift_down(u, 2, row) * cwp[0:1] + _shift_down(u, 1, row) * cwp[1:2] + u * cwp[2:3] + cb_ref[p]
            uc_ref[p] = uc.astype(BF16)
            conv.append(uc)
        g, v = conv
        a_ref[...] = (g * _sigmoid(g) * v).astype(BF16)

    half = pl.BlockSpec((2, S, tn), lambda j: (0, 0, j))
    return pl.pallas_call(
        body, name=name, grid=(F // tn,),
        in_specs=[pl.BlockSpec((S, D), lambda j: (0, 0)), pl.BlockSpec((2, D, tn), lambda j: (0, 0, j)),
                  pl.BlockSpec((2, 3, tn), lambda j: (0, 0, j)), pl.BlockSpec((2, 1, tn), lambda j: (0, 0, j))],
        out_specs=[half, half, pl.BlockSpec((S, tn), lambda j: (0, j))],
        out_shape=[jax.ShapeDtypeStruct((2, S, F), BF16), jax.ShapeDtypeStruct((2, S, F), BF16),
                   jax.ShapeDtypeStruct((S, F), BF16)],
        compiler_params=_params(("parallel",), 56),
    )(h, w, cw, cb)


def _ffn_bwd_elem(da, u, uc, cw, after, name):
    _, S, F = u.shape
    tn = _tile(F, 256)

    def body(da_ref, u_ref, uc_ref, cw_ref, after_ref, du_ref, dcw_ref, dcb_ref):
        row = lax.broadcasted_iota(jnp.int32, (S, tn), 0)
        da = da_ref[...]
        g, v = uc_ref[0].astype(F32), uc_ref[1].astype(F32)
        sg = _sigmoid(g)
        d_conv = (da * v * (sg * (1.0 + g * (1.0 - sg))), da * (g * sg))
        for p in range(2):
            d = d_conv[p]
            cwp = cw_ref[p]
            u = u_ref[p].astype(F32)
            d1, d2 = _shift_up(d, 1, row, S), _shift_up(d, 2, row, S)
            dcb_ref[p] = jnp.sum(d, axis=0, keepdims=True)
            dcw_ref[p, 0:1, :] = jnp.sum(d2 * u, axis=0, keepdims=True)
            dcw_ref[p, 1:2, :] = jnp.sum(d1 * u, axis=0, keepdims=True)
            dcw_ref[p, 2:3, :] = jnp.sum(d * u, axis=0, keepdims=True)
            du_ref[p] = (d * cwp[2:3] + d1 * cwp[1:2] + d2 * cwp[0:1]).astype(BF16)

    half = pl.BlockSpec((2, S, tn), lambda j: (0, 0, j))
    return pl.pallas_call(
        body, name=name, grid=(F // tn,),
        in_specs=[pl.BlockSpec((S, tn), lambda j: (0, j)), half, half, pl.BlockSpec((2, 3, tn), lambda j: (0, 0, j)),
                  pl.BlockSpec(memory_space=pl.ANY)],
        out_specs=[half, pl.BlockSpec((2, 3, tn), lambda j: (0, 0, j)), pl.BlockSpec((2, 1, tn), lambda j: (0, 0, j))],
        out_shape=[jax.ShapeDtypeStruct((2, S, F), BF16), jax.ShapeDtypeStruct((2, 3, F), F32),
                   jax.ShapeDtypeStruct((2, 1, F), F32)],
        compiler_params=_params(("parallel",), 56),
    )(da, u, uc, cw, after)


def _split3(x):
    hi = x.astype(BF16)
    r1 = x - hi.astype(F32)
    mid = r1.astype(BF16)
    lo = (r1 - mid.astype(F32)).astype(BF16)
    return hi, mid, lo


def _tri_matmul(x, upper, S):
    tc = _tile(S, 512)
    parts = _split3(x)
    outs = []
    for b in range(S // tc):
        r = lax.broadcasted_iota(jnp.int32, (S, tc), 0)
        c = lax.broadcasted_iota(jnp.int32, (S, tc), 1) + b * tc
        tri = jnp.where((r <= c) if upper else (r >= c), 1.0, 0.0).astype(BF16)
        acc = jnp.dot(parts[0], tri, preferred_element_type=F32)
        acc += jnp.dot(parts[1], tri, preferred_element_type=F32)
        acc += jnp.dot(parts[2], tri, preferred_element_type=F32)
        outs.append(acc)
    return outs, tc


def _fox_prep(flT, bf, name):
    H, S = flT.shape

    def body(fl_ref, b_ref, cum_ref):
        zz = fl_ref[...] + b_ref[...]
        lf = jnp.minimum(zz, 0.0) - jnp.log(1.0 + jnp.exp(-jnp.abs(zz)))
        outs, tc = _tri_matmul(lf, True, S)
        for b, o in enumerate(outs):
            cum_ref[:, b * tc:(b + 1) * tc] = o

    return pl.pallas_call(
        body, name=name,
        in_specs=[pl.BlockSpec(memory_space=pltpu.VMEM)] * 2,
        out_specs=pl.BlockSpec(memory_space=pltpu.VMEM),
        out_shape=jax.ShapeDtypeStruct((H, S), F32),
        compiler_params=pltpu.CompilerParams(vmem_limit_bytes=48 * MIB),
    )(flT, bf)


def _fox_prep_bwd(dcum_key, dcum_query, flT, bf, name):
    H, S = flT.shape

    def body(dck_ref, dcq_ref, fl_ref, b_ref, dfl_ref, dbf_ref):
        zz = fl_ref[...] + b_ref[...]
        outs, tc = _tri_matmul(dck_ref[...] + dcq_ref[...], False, S)
        total = jnp.zeros((H, 1), F32)
        for b, o in enumerate(outs):
            dfl = o * _sigmoid(-zz[:, b * tc:(b + 1) * tc])
            dfl_ref[:, b * tc:(b + 1) * tc] = dfl
            total += jnp.sum(dfl, axis=1, keepdims=True)
        dbf_ref[...] = total

    return pl.pallas_call(
        body, name=name,
        in_specs=[pl.BlockSpec(memory_space=pltpu.VMEM)] * 4,
        out_specs=[pl.BlockSpec(memory_space=pltpu.VMEM)] * 2,
        out_shape=[jax.ShapeDtypeStruct((H, S), F32), jax.ShapeDtypeStruct((H, 1), F32)],
        compiler_params=pltpu.CompilerParams(vmem_limit_bytes=48 * MIB),
    )(dcum_key, dcum_query, flT, bf)


FOX_TQ = 256
FOX_TC = 512


def _fox_scores(q, k_ref, ck_ref, i, lo, n, tq):
    k = k_ref[lo:lo + n, :].astype(BF16)
    s = lax.dot_general(q, k, (((1,), (1,)), ((), ())), preferred_element_type=F32) - ck_ref[:, lo:lo + n]
    qpos = i * tq + lax.broadcasted_iota(jnp.int32, (tq, n), 0)
    kpos = lo + lax.broadcasted_iota(jnp.int32, (tq, n), 1)
    return jnp.where(kpos <= qpos, s, -jnp.inf)


def _fox_attn_fwd(proj, cum_row, H, name):
    S = proj.shape[0]
    dh = FOX_HEAD_DIM
    tq = _tile(S, FOX_TQ)
    scale = dh ** -0.5

    def body(q_ref, k_ref, v_ref, ck_ref, o_ref, lse_ref):
        q = (q_ref[...] * scale).astype(BF16)
        s = _fox_scores(q, k_ref, ck_ref, pl.program_id(1), 0, S, tq)
        m = jnp.max(s, axis=-1, keepdims=True)
        p = jnp.exp(s - m)
        l = jnp.sum(p, axis=-1, keepdims=True)
        o = jnp.dot(p.astype(BF16), v_ref[...].astype(BF16), preferred_element_type=F32) / l
        o_ref[...] = o.astype(BF16)
        lse_ref[...] = m + jnp.log(l)

    return pl.pallas_call(
        body, name=name, grid=(H, S // tq),
        in_specs=[pl.BlockSpec((tq, dh), lambda h, i: (i, h)),
                  pl.BlockSpec((S, dh), lambda h, i: (0, H + h)),
                  pl.BlockSpec((S, dh), lambda h, i: (0, 2 * H + h)),
                  pl.BlockSpec((None, 1, S), lambda h, i: (h, 0, 0))],
        out_specs=[pl.BlockSpec((tq, dh), lambda h, i: (i, h)),
                   pl.BlockSpec((None, tq, 1), lambda h, i: (h, i, 0))],
        out_shape=[jax.ShapeDtypeStruct((S, H * dh), BF16), jax.ShapeDtypeStruct((H, S, 1), F32)],
        compiler_params=_params(("parallel", "parallel")),
    )(proj, proj, proj, cum_row)


def _fox_attn_bwd(proj, do, o, cum_row, lse, H, after, name):
    S = proj.shape[0]
    dh = FOX_HEAD_DIM
    tq = _tile(S, FOX_TQ)
    tc = _tile(S, FOX_TC)
    scale = dh ** -0.5

    def body(q_ref, k_ref, v_ref, do_ref, o_ref, ck_ref, lse_ref, after_ref,
             dq_ref, dk_ref, dv_ref, dck_ref, dcq_ref, acc_ref):
        i = pl.program_id(1)

        @pl.when(i == 0)
        def _():
            dk_ref[...] = jnp.zeros_like(dk_ref)
            dv_ref[...] = jnp.zeros_like(dv_ref)
            dck_ref[...] = jnp.zeros_like(dck_ref)

        acc_ref[...] = jnp.zeros_like(acc_ref)
        dcq_ref[...] = jnp.zeros_like(dcq_ref)
        q = (q_ref[...] * scale).astype(BF16)
        do_f = do_ref[...]
        do_b = do_f.astype(BF16)
        delta = jnp.sum(do_f * o_ref[...].astype(F32), axis=-1, keepdims=True)
        lse_q = lse_ref[...]
        for c in range(S // tc):
            @pl.when(c * tc <= i * tq + tq - 1)
            def _():
                rows = slice(c * tc, (c + 1) * tc)
                p = jnp.exp(_fox_scores(q, k_ref, ck_ref, i, c * tc, tc, tq) - lse_q)
                dp = lax.dot_general(do_b, v_ref[rows, :].astype(BF16), (((1,), (1,)), ((), ())),
                                     preferred_element_type=F32)
                ds = p * (dp - delta)
                ds_b = ds.astype(BF16)
                acc_ref[...] += jnp.dot(ds_b, k_ref[rows, :].astype(BF16), preferred_element_type=F32)
                dk_ref[rows, :] += lax.dot_general(ds_b, q, (((0,), (0,)), ((), ())), preferred_element_type=F32)
                dv_ref[rows, :] += lax.dot_general(p.astype(BF16), do_b, (((0,), (0,)), ((), ())),
                                                   preferred_element_type=F32)
                dck_ref[:, rows] -= jnp.sum(ds, axis=0, keepdims=True)
                dcq_ref[...] += jnp.sum(ds, axis=-1, keepdims=True)
        dq_ref[...] = (acc_ref[...] * scale).astype(BF16)

    W = H * dh
    return pl.pallas_call(
        body, name=name, grid=(H, S // tq),
        in_specs=[pl.BlockSpec((tq, dh), lambda h, i: (i, h)),
                  pl.BlockSpec((S, dh), lambda h, i: (0, H + h)),
                  pl.BlockSpec((S, dh), lambda h, i: (0, 2 * H + h)),
                  pl.BlockSpec((tq, dh), lambda h, i: (i, h)),
                  pl.BlockSpec((tq, dh), lambda h, i: (i, h)),
                  pl.BlockSpec((None, 1, S), lambda h, i: (h, 0, 0)),
                  pl.BlockSpec((None, tq, 1), lambda h, i: (h, i, 0)),
                  pl.BlockSpec(memory_space=pl.ANY)],
        out_specs=[pl.BlockSpec((tq, dh), lambda h, i: (i, h)),
                   pl.BlockSpec((S, dh), lambda h, i: (0, h)),
                   pl.BlockSpec((S, dh), lambda h, i: (0, h)),
                   pl.BlockSpec((None, 1, S), lambda h, i: (h, 0, 0)),
                   pl.BlockSpec((None, tq, 1), lambda h, i: (h, i, 0))],
        out_shape=[jax.ShapeDtypeStruct((S, W), BF16), jax.ShapeDtypeStruct((S, W), F32),
                   jax.ShapeDtypeStruct((S, W), F32), jax.ShapeDtypeStruct((H, 1, S), F32),
                   jax.ShapeDtypeStruct((H, S, 1), F32)],
        scratch_shapes=[pltpu.VMEM((tq, dh), F32)],
        compiler_params=_params(("parallel", "arbitrary")),
    )(proj, proj, proj, do, o, cum_row, lse, after)


def _rope(parts, tabs, out_dtype, name):
    S = parts[0][0].shape[0]
    widths = [a.shape[1] for a, _ in parts]
    total = sum(widths)
    tm = _tile(S, ROW_TILE, 8)
    flags = [r for _, r in parts]

    def body(*refs):
        in_refs = refs[:len(parts)]
        cos_ref, sa_ref, sb_ref, o_ref = refs[len(parts):]
        cos, sa, sb = cos_ref[...], sa_ref[...], sb_ref[...]
        off = 0
        for ref, rot, w in zip(in_refs, flags, widths):
            for j in range(w // LANE):
                t = ref[:, j * LANE:(j + 1) * LANE]
                if rot:
                    t = t * cos + pltpu.roll(t, LANE - ROPE_DIM // 2, axis=1) * sa + pltpu.roll(t, ROPE_DIM // 2, axis=1) * sb
                o_ref[:, off + j * LANE:off + (j + 1) * LANE] = t.astype(o_ref.dtype)
            off += w

    return pl.pallas_call(
        body, name=name, grid=(S // tm,),
        in_specs=[pl.BlockSpec((tm, w), lambda i: (i, 0)) for w in widths] + [_row_spec(tm, LANE)] * 3,
        out_specs=_row_spec(tm, total),
        out_shape=jax.ShapeDtypeStruct((S, total), out_dtype),
        compiler_params=_params(("parallel",)),
    )(*[a for a, _ in parts], *tabs)


def _swa_band(ref_p, ref_c, hk):
    dh = SWA_HEAD_DIM
    return jnp.concatenate([ref_p[:, hk * dh:(hk + 1) * dh], ref_c[:, hk * dh:(hk + 1) * dh]], axis=0).astype(BF16)


def _swa_mask(n, G):
    qi = lax.broadcasted_iota(jnp.int32, (G * Q_BLOCK, 2 * Q_BLOCK), 0) % Q_BLOCK
    kj = lax.broadcasted_iota(jnp.int32, (G * Q_BLOCK, 2 * Q_BLOCK), 1)
    rel = qi + Q_BLOCK - kj
    return (rel >= 0) & (rel < SWA_WINDOW) & ((kj >= Q_BLOCK) | (n > 0))


def _swa_stack(ref, hk, G):
    dh = SWA_HEAD_DIM
    return jnp.concatenate([ref[:, (hk * G + g) * dh:(hk * G + g + 1) * dh] for g in range(G)], axis=0)


def _swa_unstack(ref, stacked, hk, G):
    dh, QB = SWA_HEAD_DIM, Q_BLOCK
    for g in range(0, G, 2):
        c0 = (hk * G + g) * dh
        pair = jnp.concatenate([stacked[g * QB:(g + 1) * QB], stacked[(g + 1) * QB:(g + 2) * QB]], axis=1)
        ref[:, c0:c0 + 2 * dh] = pair.astype(ref.dtype)


def _swa_sink_rows(sink_ref, hk, G):
    return jnp.concatenate([jnp.broadcast_to(sink_ref[0:1, hk * G + g:hk * G + g + 1], (Q_BLOCK, 1)) for g in range(G)],
                           axis=0)


def _swa_attn_fwd(qk, proj, sinks, Hq, name):
    S = qk.shape[0]
    dh, G, QB = SWA_HEAD_DIM, SWA_GROUP, Q_BLOCK
    Hk = Hq // G
    Wq, Wk = Hq * dh, Hk * dh
    nb = S // QB
    scale = dh ** -0.5

    def body(q_ref, kp_ref, kc_ref, vp_ref, vc_ref, sink_ref, o_ref, lse_ref):
        n = pl.program_id(0)
        mask = _swa_mask(n, G)
        lane = lax.broadcasted_iota(jnp.int32, (QB, LANE), 1)
        lse_tile = jnp.zeros((QB, LANE), F32)
        for hk in range(Hk):
            kb = _swa_band(kp_ref, kc_ref, hk)
            vb = _swa_band(vp_ref, vc_ref, hk)
            q = _swa_stack(q_ref, hk, G).astype(BF16)
            sk = _swa_sink_rows(sink_ref, hk, G)
            s = lax.dot_general(q, kb, (((1,), (1,)), ((), ())), preferred_element_type=F32) * scale
            s = jnp.where(mask, s, -jnp.inf)
            m = jnp.maximum(jnp.max(s, axis=-1, keepdims=True), sk)
            p = jnp.exp(s - m)
            l = jnp.sum(p, axis=-1, keepdims=True) + jnp.exp(sk - m)
            o = jnp.dot(p.astype(BF16), vb, preferred_element_type=F32) / l
            lse = m + jnp.log(l)
            for g in range(G):
                lse_tile = jnp.where(lane == hk * G + g, lse[g * QB:(g + 1) * QB], lse_tile)
            _swa_unstack(o_ref, o, hk, G)
        lse_ref[...] = lse_tile

    kcol, vcol = Wq // Wk, (Wq + Wk) // Wk
    return pl.pallas_call(
        body, name=name, grid=(nb,),
        in_specs=[pl.BlockSpec((QB, Wq), lambda n: (n, 0)),
                  pl.BlockSpec((QB, Wk), lambda n: (jnp.maximum(n - 1, 0), kcol)),
                  pl.BlockSpec((QB, Wk), lambda n: (n, kcol)),
                  pl.BlockSpec((QB, Wk), lambda n: (jnp.maximum(n - 1, 0), vcol)),
                  pl.BlockSpec((QB, Wk), lambda n: (n, vcol)),
                  pl.BlockSpec((1, LANE), lambda n: (0, 0))],
        out_specs=[pl.BlockSpec((QB, Wq), lambda n: (n, 0)), pl.BlockSpec((QB, LANE), lambda n: (n, 0))],
        out_shape=[jax.ShapeDtypeStruct((S, Wq), BF16), jax.ShapeDtypeStruct((S, LANE), F32)],
        compiler_params=_params(("parallel",)),
    )(qk, qk, qk, proj, proj, sinks)


def _swa_attn_bwd(qk, proj, sinks, do, lse, Hq, after, name):
    S = qk.shape[0]
    dh, G, QB = SWA_HEAD_DIM, SWA_GROUP, Q_BLOCK
    Hk = Hq // G
    Wq, Wk = Hq * dh, Hk * dh
    nb = S // QB
    scale = dh ** -0.5

    def body(q_ref, kp_ref, kc_ref, vp_ref, vc_ref, sink_ref, do_ref, lse_ref, after_ref,
             dq_ref, dk_ref, dv_ref, dsink_ref, carry_k, carry_v):
        n = pl.program_id(0)

        @pl.when(n == 0)
        def _():
            dsink_ref[...] = jnp.zeros_like(dsink_ref)

        @pl.when(n < nb)
        def _():
            mask = _swa_mask(n, G)
            lane = lax.broadcasted_iota(jnp.int32, (1, LANE), 1)
            dsink = jnp.zeros((1, LANE), F32)
            dk_heads, dv_heads = [], []
            for hk in range(Hk):
                kb = _swa_band(kp_ref, kc_ref, hk)
                vb = _swa_band(vp_ref, vc_ref, hk)
                q = _swa_stack(q_ref, hk, G).astype(BF16)
                do_s = _swa_stack(do_ref, hk, G).astype(BF16)
                lse = jnp.concatenate([lse_ref[:, hk * G + g:hk * G + g + 1] for g in range(G)], axis=0)
                s = lax.dot_general(q, kb, (((1,), (1,)), ((), ())), preferred_element_type=F32) * scale
                s = jnp.where(mask, s, -jnp.inf)
                p = jnp.exp(s - lse)
                p_sink = jnp.exp(_swa_sink_rows(sink_ref, hk, G) - lse)
                dp = lax.dot_general(do_s, vb, (((1,), (1,)), ((), ())), preferred_element_type=F32)
                delta = jnp.sum(p * dp, axis=-1, keepdims=True)
                ds_b = (p * (dp - delta)).astype(BF16)
                _swa_unstack(dq_ref, jnp.dot(ds_b, kb, preferred_element_type=F32) * scale, hk, G)
                dk_heads.append(lax.dot_general(ds_b, q, (((0,), (0,)), ((), ())), preferred_element_type=F32) * scale)
                dv_heads.append(lax.dot_general(p.astype(BF16), do_s, (((0,), (0,)), ((), ())), preferred_element_type=F32))
                sink_term = p_sink * delta
                for g in range(G):
                    dsink = jnp.where(lane == hk * G + g,
                                      -jnp.sum(sink_term[g * QB:(g + 1) * QB], axis=0, keepdims=True), dsink)
            dsink_ref[...] += dsink
            dk_all = jnp.concatenate(dk_heads, axis=1)
            dv_all = jnp.concatenate(dv_heads, axis=1)

            @pl.when(n > 0)
            def _():
                dk_ref[...] = carry_k[...] + dk_all[:QB]
                dv_ref[...] = carry_v[...] + dv_all[:QB]

            carry_k[...] = dk_all[QB:]
            carry_v[...] = dv_all[QB:]

        @pl.when(n == nb)
        def _():
            dk_ref[...] = carry_k[...]
            dv_ref[...] = carry_v[...]

    kcol, vcol = Wq // Wk, (Wq + Wk) // Wk
    cur = lambda n: jnp.minimum(n, nb - 1)
    prev = lambda n: jnp.maximum(jnp.minimum(n, nb - 1) - 1, 0)
    return pl.pallas_call(
        body, name=name, grid=(nb + 1,),
        in_specs=[pl.BlockSpec((QB, Wq), lambda n: (cur(n), 0)),
                  pl.BlockSpec((QB, Wk), lambda n: (prev(n), kcol)),
                  pl.BlockSpec((QB, Wk), lambda n: (cur(n), kcol)),
                  pl.BlockSpec((QB, Wk), lambda n: (prev(n), vcol)),
                  pl.BlockSpec((QB, Wk), lambda n: (cur(n), vcol)),
                  pl.BlockSpec((1, LANE), lambda n: (0, 0)),
                  pl.BlockSpec((QB, Wq), lambda n: (cur(n), 0)),
                  pl.BlockSpec((QB, LANE), lambda n: (cur(n), 0)),
                  pl.BlockSpec(memory_space=pl.ANY)],
        out_specs=[pl.BlockSpec((QB, Wq), lambda n: (cur(n), 0)),
                   pl.BlockSpec((QB, Wk), lambda n: (jnp.maximum(n - 1, 0), 0)),
                   pl.BlockSpec((QB, Wk), lambda n: (jnp.maximum(n - 1, 0), 0)),
                   pl.BlockSpec((1, LANE), lambda n: (0, 0))],
        out_shape=[jax.ShapeDtypeStruct((S, Wq), F32), jax.ShapeDtypeStruct((S, Wk), F32),
                   jax.ShapeDtypeStruct((S, Wk), F32), jax.ShapeDtypeStruct((1, LANE), F32)],
        scratch_shapes=[pltpu.VMEM((QB, Wk), F32), pltpu.VMEM((QB, Wk), F32)],
        compiler_params=_params(("arbitrary",)),
    )(qk, qk, qk, proj, proj, sinks, do, lse, after)


ADA_ROWS = 16


def _ada_mod(c_pad, w, b, name):
    L, D, N = w.shape
    tn = _tile(N, 512)

    def body(c_ref, w_ref, b_ref, o_ref):
        c = c_ref[...]
        c = c * _sigmoid(c)
        ch = c.astype(BF16)
        cl = (c - ch.astype(F32)).astype(BF16)
        ww = w_ref[...]
        wh = ww.astype(BF16)
        wl = (ww - wh.astype(F32)).astype(BF16)
        acc = jnp.dot(ch, wh, preferred_element_type=F32)
        acc += jnp.dot(ch, wl, preferred_element_type=F32)
        acc += jnp.dot(cl, wh, preferred_element_type=F32)
        o_ref[...] = acc + b_ref[...]

    return pl.pallas_call(
        body, name=name, grid=(L, N // tn),
        in_specs=[pl.BlockSpec((ADA_ROWS, D), lambda l, j: (0, 0)),
                  pl.BlockSpec((None, D, tn), lambda l, j: (l, 0, j)),
                  pl.BlockSpec((None, 1, tn), lambda l, j: (l, 0, j))],
        out_specs=pl.BlockSpec((None, ADA_ROWS, tn), lambda l, j: (l, 0, j)),
        out_shape=jax.ShapeDtypeStruct((L, ADA_ROWS, N), F32),
        compiler_params=_params(("parallel", "parallel")),
    )(c_pad, w, b)


def _ada_bwd(cT, dm, name):
    D = cT.shape[0]
    L, B, N = dm.shape
    tm = _tile(D, 256)

    def body(c_ref, dm_ref, o_ref):
        c = c_ref[...]
        c = c * _sigmoid(c)
        dmv = dm_ref[...]
        acc = c[:, 0:1] * dmv[0:1, :]
        for b in range(1, B):
            acc += c[:, b:b + 1] * dmv[b:b + 1, :]
        o_ref[...] = acc

    return pl.pallas_call(
        body, name=name, grid=(L, D // tm),
        in_specs=[pl.BlockSpec((tm, B), lambda l, i: (i, 0)), pl.BlockSpec((None, B, N), lambda l, i: (l, 0, 0))],
        out_specs=pl.BlockSpec((None, tm, N), lambda l, i: (l, i, 0)),
        out_shape=jax.ShapeDtypeStruct((L, D, N), F32),
        compiler_params=_params(("parallel", "parallel")),
    )(cT, dm)


def _adamw_math(w, g, m, v):
    m = ADAM_B1 * m + (1.0 - ADAM_B1) * g
    v = ADAM_B2 * v + (1.0 - ADAM_B2) * (g * g)
    m_hat = m / (1.0 - ADAM_B1 ** ADAM_STEP)
    v_hat = v / (1.0 - ADAM_B2 ** ADAM_STEP)
    delta = -ADAM_LR * (m_hat / (jnp.sqrt(v_hat) + ADAM_EPS) + ADAM_WD * w)
    return delta, m, v


def _adam_rows(R, C):
    lanes = -(-C // LANE) * LANE
    return _tile(R, max(8, (262144 // lanes) // 8 * 8), 8)


def _adam_sum(recv, w, m, v, layer, filled, name, by_cols=False):
    P, R, C = recv.shape
    L = w.shape[0]
    n_keep = 0 if filled is None else 4
    if by_cols:
        tc = _tile(C, 256)
        grid, spec = (C // tc,), pl.BlockSpec((None, R, tc), lambda i: (layer, 0, i))
        recv_spec = pl.BlockSpec((P, R, tc), lambda i: (0, 0, i))
    else:
        tr = _adam_rows(R, C)
        grid, spec = (R // tr,), pl.BlockSpec((None, tr, C), lambda i: (layer, i, 0))
        recv_spec = pl.BlockSpec((P, tr, C), lambda i: (0, i, 0))

    def body(r_ref, w_ref, m_ref, v_ref, *rest):
        g_ref, d_ref, mo_ref, vo_ref = rest[n_keep:]
        g = r_ref[0].astype(F32)
        for p in range(1, P):
            g = g + r_ref[p].astype(F32)
        delta, mn, vn = _adamw_math(w_ref[...], g, m_ref[...], v_ref[...])
        g_ref[...] = g
        d_ref[...] = delta
        mo_ref[...] = mn
        vo_ref[...] = vn

    return pl.pallas_call(
        body, name=name, grid=grid,
        in_specs=[recv_spec, spec, spec, spec] + [pl.BlockSpec(memory_space=pl.ANY)] * n_keep,
        out_specs=[spec] * 4,
        out_shape=[jax.ShapeDtypeStruct((L, R, C), F32)] * 4,
        input_output_aliases={4 + k: k for k in range(n_keep)},
        compiler_params=_params(("parallel",)),
    )(recv, w, m, v, *(filled or ()))


def _adam(g, w, m, v, name):
    L, R, C = w.shape
    tr = _adam_rows(R, C)

    def body(g_ref, w_ref, m_ref, v_ref, d_ref, mo_ref, vo_ref):
        delta, mn, vn = _adamw_math(w_ref[...], g_ref[...], m_ref[...], v_ref[...])
        d_ref[...] = delta
        mo_ref[...] = mn
        vo_ref[...] = vn

    spec = pl.BlockSpec((None, tr, C), lambda l, i: (l, i, 0))
    return pl.pallas_call(
        body, name=name, grid=(L, R // tr),
        in_specs=[spec] * 4, out_specs=[spec] * 3,
        out_shape=[jax.ShapeDtypeStruct((L, R, C), F32)] * 3,
        compiler_params=_params(("parallel", "parallel")),
    )(g, w, m, v)


def _sum_slots(x, after, name):
    P, R, C = x.shape

    def body(x_ref, after_ref, o_ref):
        acc = x_ref[0]
        for p in range(1, P):
            acc = acc + x_ref[p]
        o_ref[...] = acc

    return pl.pallas_call(
        body, name=name,
        in_specs=[pl.BlockSpec(memory_space=pltpu.VMEM), pl.BlockSpec(memory_space=pl.ANY)],
        out_specs=pl.BlockSpec(memory_space=pltpu.VMEM),
        out_shape=jax.ShapeDtypeStruct((R, C), F32),
        compiler_params=pltpu.CompilerParams(vmem_limit_bytes=48 * MIB),
    )(x, after)


def _my_pos():
    return lax.axis_index("x"), lax.axis_index("y"), lax.axis_index("c")


def _all_gather_small(x, name, after=()):
    R, C = x.shape
    n_after = len(after)

    def body(x_ref, *rest):
        out_ref, send_sems, recv_sems = rest[n_after:]
        x_, y_, c_ = _my_pos()
        me, sibling = (x_, y_, c_), (x_, y_, 1 - c_)
        chips = [(1 - x_, y_), (x_, 1 - y_), (1 - x_, 1 - y_)]

        def slot(px, py, pc):
            return out_ref.at[4 * px + 2 * py + pc]

        def copy(k, block, to):
            return pltpu.make_async_remote_copy(
                src_ref=slot(*block), dst_ref=slot(*block), send_sem=send_sems.at[k], recv_sem=recv_sems.at[k],
                device_id=to, device_id_type=MESH)

        out_ref[4 * x_ + 2 * y_ + c_] = x_ref[...]
        first = [copy(0, me, sibling)] + [copy(1 + j, me, (*chip, c_)) for j, chip in enumerate(chips)]
        for cp in first:
            cp.start()
        passed = [copy(4 + j, (*chip, c_), sibling) for j, chip in enumerate(chips)]
        for j, chip in enumerate(chips):
            copy(1 + j, (*chip, c_), me).wait_recv()
            passed[j].start()
        copy(0, sibling, me).wait_recv()
        for j, chip in enumerate(chips):
            copy(4 + j, (*chip, 1 - c_), me).wait_recv()
        for cp in first + passed:
            cp.wait_send()

    return pl.pallas_call(
        body, name=name,
        in_specs=[pl.BlockSpec(memory_space=pltpu.VMEM)] + [pl.BlockSpec(memory_space=pl.ANY)] * n_after,
        out_specs=pl.BlockSpec(memory_space=pltpu.VMEM),
        out_shape=jax.ShapeDtypeStruct((N_DEV, R, C), x.dtype),
        scratch_shapes=[pltpu.SemaphoreType.DMA((7,)), pltpu.SemaphoreType.DMA((7,))],
        compiler_params=pltpu.CompilerParams(vmem_limit_bytes=48 * MIB),
    )(x, *after)


HBM_SPEC = pl.BlockSpec(memory_space=pltpu.HBM)
SEM_SPEC = pl.BlockSpec(memory_space=pltpu.SEMAPHORE)
ANY_SPEC = pl.BlockSpec(memory_space=pl.ANY)
SPLIT_EFFECT = pltpu.SideEffectType.DATAFLOW_SIDE_EFFECTING


def _in_hbm(a):
    return pltpu.with_memory_space_constraint(a, pltpu.HBM)


def _gathered_shape(a, kind):
    if kind == "major":
        return (N_DEV,) + a.shape
    if kind == "rows":
        return (N_DEV * a.shape[0], a.shape[1])
    return (2, a.shape[0], 4 * a.shape[1])


def _gather_slot(ref, kind, block, shard_shape):
    px, py, pc = block
    if kind == "major":
        return ref.at[4 * px + 2 * py + pc]
    if kind == "rows":
        r = shard_shape[0]
        return ref.at[pl.ds(pl.multiple_of((4 * px + 2 * py + pc) * r, r), r), :]
    cu = shard_shape[1]
    return ref.at[px, :, pl.ds(pl.multiple_of((2 * py + pc) * cu, cu), cu)]


def _gather_peers():
    x_, y_, c_ = _my_pos()
    return (x_, y_, c_), (x_, y_, 1 - c_), [(1 - x_, y_), (x_, 1 - y_), (1 - x_, 1 - y_)]


def _gather_start(shards, kinds, after, name):
    n = len(shards)
    bufs = [lax.empty(_gathered_shape(a, k), a.dtype) for a, k in zip(shards, kinds)]
    extra = [] if after is None else [after]

    def body(*refs):
        shard_refs, buf_refs = refs[:n], refs[n:2 * n]
        send_sems, recv_sems, local_sems = refs[2 * n + len(extra):2 * n + len(extra) + 3]
        token = refs[-1]
        me, sibling, chips = _gather_peers()
        for e in range(n):
            mine = _gather_slot(buf_refs[e], kinds[e], me, shards[e].shape)
            pltpu.make_async_copy(shard_refs[e], mine, local_sems.at[e]).start()
            for k, to in enumerate([sibling] + [(*chip, me[2]) for chip in chips]):
                pltpu.make_async_remote_copy(
                    src_ref=shard_refs[e], dst_ref=mine, send_sem=send_sems.at[4 * e + k],
                    recv_sem=recv_sems.at[4 * e + k], device_id=to, device_id_type=MESH).start()
        token[...] = jnp.zeros_like(token)

    out = pl.pallas_call(
        body, name=name,
        out_shape=(pltpu.SemaphoreType.DMA((4 * n,)), pltpu.SemaphoreType.DMA((4 * n,)), pltpu.SemaphoreType.DMA((n,)),
                   *[pltpu.HBM(a.shape, a.dtype) for a in shards], *[pltpu.HBM(a.shape, a.dtype) for a in bufs],
                   jax.ShapeDtypeStruct((8, LANE), F32)),
        in_specs=[HBM_SPEC] * (2 * n) + [ANY_SPEC] * len(extra),
        out_specs=(SEM_SPEC, SEM_SPEC, SEM_SPEC, *[HBM_SPEC] * (2 * n), pl.BlockSpec(memory_space=pltpu.VMEM)),
        input_output_aliases={i: 3 + i for i in range(2 * n)},
        compiler_params=pltpu.CompilerParams(has_side_effects=SPLIT_EFFECT),
    )(*[_in_hbm(a) for a in shards], *[_in_hbm(a) for a in bufs], *extra)
    return out[0], out[1], out[2], out[3:3 + n], out[3 + n:3 + 2 * n], out[-1]


def _gather_forward(recv_sems, bufs, kinds, shard_shapes, after, name):
    n = len(bufs)

    def body(*refs):
        buf_refs, recv_in = refs[:n], refs[n]
        fsend, frecv = refs[n + 2], refs[n + 3]
        token = refs[-1]
        me, sibling, chips = _gather_peers()
        for e in range(n):
            for j, chip in enumerate(chips):
                slot = _gather_slot(buf_refs[e], kinds[e], (*chip, me[2]), shard_shapes[e])
                pltpu.make_async_remote_copy(
                    src_ref=slot, dst_ref=slot, send_sem=recv_in.at[4 * e + 1 + j], recv_sem=recv_in.at[4 * e + 1 + j],
                    device_id=me, device_id_type=MESH).wait_recv()
                pltpu.make_async_remote_copy(
                    src_ref=slot, dst_ref=slot, send_sem=fsend.at[3 * e + j], recv_sem=frecv.at[3 * e + j],
                    device_id=sibling, device_id_type=MESH).start()
        token[...] = jnp.zeros_like(token)

    out = pl.pallas_call(
        body, name=name,
        out_shape=(pltpu.SemaphoreType.DMA((3 * n,)), pltpu.SemaphoreType.DMA((3 * n,)),
                   *[pltpu.HBM(a.shape, a.dtype) for a in bufs], jax.ShapeDtypeStruct((8, LANE), F32)),
        in_specs=[HBM_SPEC] * n + [SEM_SPEC, ANY_SPEC],
        out_specs=(SEM_SPEC, SEM_SPEC, *[HBM_SPEC] * n, pl.BlockSpec(memory_space=pltpu.VMEM)),
        input_output_aliases={i: 2 + i for i in range(n)},
        compiler_params=pltpu.CompilerParams(has_side_effects=SPLIT_EFFECT),
    )(*bufs, recv_sems, after)
    return out[0], out[1], out[2:2 + n], out[-1]


def _gather_wait(send_sems, recv_sems, local_sems, fsend, frecv, shards, bufs, kinds, after, name):
    n = len(bufs)

    def body(*refs):
        shard_refs, buf_refs = refs[:n], refs[n:2 * n]
        send_in, recv_in, local_in, fsend_in, frecv_in = refs[2 * n:2 * n + 5]
        me, sibling, chips = _gather_peers()

        def arrival(slot, sem):
            return pltpu.make_async_remote_copy(src_ref=slot, dst_ref=slot, send_sem=sem, recv_sem=sem,
                                                device_id=me, device_id_type=MESH)

        for e in range(n):
            shape = shards[e].shape
            mine = _gather_slot(buf_refs[e], kinds[e], me, shape)
            pltpu.make_async_copy(shard_refs[e], mine, local_in.at[e]).wait()
            arrival(_gather_slot(buf_refs[e], kinds[e], sibling, shape), recv_in.at[4 * e]).wait_recv()
            for j, chip in enumerate(chips):
                arrival(_gather_slot(buf_refs[e], kinds[e], (*chip, 1 - me[2]), shape), frecv_in.at[3 * e + j]).wait_recv()
            for k in range(4):
                arrival(mine, send_in.at[4 * e + k]).wait_send()
            for j in range(3):
                arrival(mine, fsend_in.at[3 * e + j]).wait_send()

    out = pl.pallas_call(
        body, name=name,
        out_shape=(*[pltpu.HBM(a.shape, a.dtype) for a in shards], *[pltpu.HBM(a.shape, a.dtype) for a in bufs]),
        in_specs=[HBM_SPEC] * (2 * n) + [SEM_SPEC] * 5 + [ANY_SPEC],
        out_specs=tuple([HBM_SPEC] * (2 * n)),
        input_output_aliases={i: i for i in range(2 * n)},
        compiler_params=pltpu.CompilerParams(has_side_effects=SPLIT_EFFECT),
    )(*shards, *bufs, send_sems, recv_sems, local_sems, fsend, frecv, after)
    return out[n:]


def _grad_slice(ref, kind, j):
    if kind == "major":
        return ref.at[j]
    if kind == "rows":
        r = ref.shape[0] // N_DEV
        return ref.at[pl.ds(j * r, r), :]
    cu = ref.shape[2] // 4
    return ref.at[j // 4, :, pl.ds((j % 4) * cu, cu)]


def _slice_shape(a, kind):
    if kind == "major":
        return a.shape[1:]
    if kind == "rows":
        return (a.shape[0] // N_DEV, a.shape[1])
    return (a.shape[1], a.shape[2] // 4)


def _scatter_copies(srcs, lands, kinds, send_sems, recv_sems):
    x_, y_, c_ = _my_pos()
    me = 4 * x_ + 2 * y_ + c_
    n = len(srcs)

    def remote(e, j):
        return pltpu.make_async_remote_copy(
            src_ref=_grad_slice(srcs[e], kinds[e], j), dst_ref=lands[e].at[me],
            send_sem=send_sems.at[e * N_DEV + j], recv_sem=recv_sems.at[e * N_DEV + me],
            device_id=(j // 4, (j // 2) % 2, j % 2), device_id_type=MESH)

    def local(e, j):
        return pltpu.make_async_copy(_grad_slice(srcs[e], kinds[e], j), lands[e].at[j], recv_sems.at[e * N_DEV + j])

    def arrival(e, i):
        return pltpu.make_async_remote_copy(
            src_ref=_grad_slice(srcs[e], kinds[e], i), dst_ref=lands[e].at[i],
            send_sem=send_sems.at[e * N_DEV + i], recv_sem=recv_sems.at[e * N_DEV + i],
            device_id=(i // 4, (i // 2) % 2, i % 2), device_id_type=MESH)

    def start():
        for e in range(n):
            for j in range(N_DEV):
                @pl.when(me == j)
                def _():
                    local(e, j).start()

                @pl.when(me != j)
                def _():
                    remote(e, j).start()

    def wait():
        for e in range(n):
            for i in range(N_DEV):
                @pl.when(me == i)
                def _():
                    local(e, i).wait()

                @pl.when(me != i)
                def _():
                    arrival(e, i).wait_recv()
        for e in range(n):
            for j in range(N_DEV):
                @pl.when(me != j)
                def _():
                    remote(e, j).wait_send()

    return start, wait


def _scatter_start(srcs, kinds, after, name):
    n = len(srcs)
    lands = [lax.empty((N_DEV,) + _slice_shape(a, k), a.dtype) for a, k in zip(srcs, kinds)]
    extra = [] if after is None else [after]

    def body(*refs):
        src_refs, land_refs = refs[:n], refs[n:2 * n]
        send_sems, recv_sems = refs[2 * n + len(extra)], refs[2 * n + len(extra) + 1]
        token = refs[-1]
        start, _ = _scatter_copies(src_refs, land_refs, kinds, send_sems, recv_sems)
        start()
        token[...] = jnp.zeros_like(token)

    out = pl.pallas_call(
        body, name=name,
        out_shape=(pltpu.SemaphoreType.DMA((n * N_DEV,)), pltpu.SemaphoreType.DMA((n * N_DEV,)),
                   *[pltpu.HBM(a.shape, a.dtype) for a in srcs], *[pltpu.HBM(a.shape, a.dtype) for a in lands],
                   jax.ShapeDtypeStruct((8, LANE), F32)),
        in_specs=[HBM_SPEC] * (2 * n) + [ANY_SPEC] * len(extra),
        out_specs=(SEM_SPEC, SEM_SPEC, *[HBM_SPEC] * (2 * n), pl.BlockSpec(memory_space=pltpu.VMEM)),
        input_output_aliases={i: 2 + i for i in range(2 * n)},
        compiler_params=pltpu.CompilerParams(has_side_effects=SPLIT_EFFECT),
    )(*[_in_hbm(a) for a in srcs], *[_in_hbm(a) for a in lands], *extra)
    return out[0], out[1], out[2:2 + n], out[2 + n:2 + 2 * n], out[-1]


def _scatter_wait(send_sems, recv_sems, srcs, lands, kinds, after, name):
    n = len(srcs)

    def body(*refs):
        src_refs, land_refs = refs[:n], refs[n:2 * n]
        _, wait = _scatter_copies(src_refs, land_refs, kinds, refs[2 * n], refs[2 * n + 1])
        wait()

    out = pl.pallas_call(
        body, name=name,
        out_shape=(*[pltpu.HBM(a.shape, a.dtype) for a in srcs], *[pltpu.HBM(a.shape, a.dtype) for a in lands]),
        in_specs=[HBM_SPEC] * (2 * n) + [SEM_SPEC, SEM_SPEC] + [ANY_SPEC] * len(after),
        out_specs=tuple([HBM_SPEC] * (2 * n)),
        input_output_aliases={i: i for i in range(2 * n)},
        compiler_params=pltpu.CompilerParams(has_side_effects=SPLIT_EFFECT),
    )(*srcs, *lands, send_sems, recv_sems, *after)
    return out[n:]


def _rope_tables(positions, sign):
    half = ROPE_DIM // 2
    inv_freq = ROPE_THETA ** (-jnp.arange(0, ROPE_DIM, 2, dtype=F32) / ROPE_DIM)
    ang = positions.astype(F32)[:, None] * inv_freq
    reps = LANE // half
    cos = jnp.tile(jnp.cos(ang), (1, reps))
    sin = jnp.tile(jnp.sin(ang), (1, reps)) * sign
    d = jnp.arange(LANE) % SWA_HEAD_DIM
    return (jnp.where(d < ROPE_DIM, cos, 1.0), jnp.where(d < half, -sin, 0.0),
            jnp.where((d >= half) & (d < ROPE_DIM), sin, 0.0))


def _pad_cols(a, n):
    return jnp.pad(a, ((0, 0), (0, n - a.shape[1])))


def _local_step(x, target, positions, mods, fetch, P, on_grads, on_small):
    S, D = x.shape
    Hf = D // FOX_HEAD_DIM
    Hq = D // SWA_HEAD_DIM
    Hk = Hq // SWA_GROUP
    Wk = Hk * SWA_HEAD_DIM
    n_in = 3 * D + Hf
    (sh1a, sc1a, g1a, sh2a, sc2a, g2a), (sh1b, sc1b, g1b, sh2b, sc2b, g2b) = mods
    row = lambda v: v.reshape(1, -1)
    cw = [jnp.transpose(P["conv_w"][l].reshape(3, 2, -1), (1, 0, 2)) for l in range(2)]
    cb = [P["conv_b"][l].reshape(2, 1, -1) for l in range(2)]

    W = dict(up=[None, None], down=[None, None])
    h1a = _modulate(x, sc1a, sh1a, "modulate_in")
    W["fox_in"], W["fox_o"] = fetch("fox", "wait", h1a)
    proj_a = _mm_nn(h1a, W["fox_in"], F32, "fox_in_proj", tn=896)
    flT = proj_a[:, 3 * D:n_in].T
    bf_col = P["fox_b_f"].reshape(Hf, 1)
    cumT = _fox_prep(flT, bf_col, "fox_cumsum")
    cum_row = cumT.reshape(Hf, 1, S)
    o_a, lse_a = _fox_attn_fwd(proj_a, cum_row, Hf, "fox_attn_fwd")
    token = fetch("ffn0", "forward", o_a)
    y1a = _mm_nn(o_a, W["fox_o"], F32, "fox_out_proj")
    z1a, x1, h2a = _ln_fwd(x, y1a, g1a, row(P["ln_mix_g"][0]), row(P["ln_mix_b"][0]), sc2a, sh2a, token, "ln_mix0")
    W["up"][0], W["down"][0] = fetch("ffn0", "wait", h2a)
    u_a, uc_a, a_a = _ffn_up(h2a, W["up"][0], cw[0], cb[0], "ffn_up0")
    token = fetch("swa", "forward", a_a)
    y2a = _mm_nn(a_a, W["down"][0], F32, "ffn_down0", tk=1408)
    z2a, x2, h1b = _ln_fwd(x1, y2a, g2a, row(P["ln_ffn_g"][0]), row(P["ln_ffn_b"][0]), sc1b, sh1b, token, "ln_ffn0")

    W["swa_in"], W["swa_o"] = fetch("swa", "wait", h1b)
    proj_b = _mm_nn(h1b, W["swa_in"], F32, "swa_in_proj")
    tabs_f = _rope_tables(positions, 1.0)
    tabs_b = _rope_tables(positions, -1.0)
    qk = _rope([(proj_b[:, :D + Wk], True)], tabs_f, F32, "rope_fwd")
    sinks = _pad_cols(P["swa_sinks"].reshape(1, Hq), LANE)
    o_b, lse_b = _swa_attn_fwd(qk, proj_b, sinks, Hq, "swa_attn_fwd")
    token = fetch("ffn1", "forward", o_b)
    y1b = _mm_nn(o_b, W["swa_o"], F32, "swa_out_proj")
    z1b, x3, h2b = _ln_fwd(x2, y1b, g1b, row(P["ln_mix_g"][1]), row(P["ln_mix_b"][1]), sc2b, sh2b, token, "ln_mix1")
    W["up"][1], W["down"][1] = fetch("ffn1", "wait", h2b)
    u_b, uc_b, a_b = _ffn_up(h2b, W["up"][1], cw[1], cb[1], "ffn_up1")
    y2b = _mm_nn(a_b, W["down"][1], F32, "ffn_down1", tk=1408)
    z2b, dout, loss_row = _ln_fwd_loss(x3, y2b, g2b, row(P["ln_ffn_g"][1]), row(P["ln_ffn_b"][1]), target, "ln_ffn1_loss")

    def ffn_backward(dy, a, u, uc, h_in, l, tag):
        d_down = _mm_tn(a, dy[None], BF16, "ffn_dwdown" + tag, tm=1408, tn=1024)[0]
        token = on_grads("ffn_w_down" + tag, d_down)
        da = _mm_nt(dy[None], W["down"][l][None], F32, "ffn_da" + tag)
        du, dcw, dcb = _ffn_bwd_elem(da, u, uc, cw[l], token, "ffn_bwd_elem" + tag)
        d_up = _mm_tn(h_in, du, BF16, "ffn_dwup" + tag)
        token = on_grads("ffn_w_up" + tag, d_up)
        dh = _mm_nt(du, W["up"][l], F32, "ffn_dh" + tag, tn=1024, tk=1408)
        return dh, token, jnp.transpose(dcw, (1, 0, 2)).reshape(3, -1), dcb.reshape(-1)

    dz2b, dy2b, dg_f1, db_f1, dgate2b = _ln_bwd(dout, z2b, y2b, g2b, row(P["ln_ffn_g"][1]), "ln_ffn1_bwd")
    dh2b, token, dcw1, dcb1 = ffn_backward(dy2b, a_b, u_b, uc_b, h2b, 1, "1")
    dx3, dsc2b, dsh2b = _mod_bwd(dz2b, dh2b, x3, sc2b, token, "mod_ffn1_bwd")

    dz1b, dy1b, dg_m1, db_m1, dgate1b = _ln_bwd(dx3, z1b, y1b, g1b, row(P["ln_mix_g"][1]), "ln_mix1_bwd")
    token = on_grads("swa_w_o", _mm_tn(o_b, dy1b[None], BF16, "swa_dwo")[0])
    do_b = _mm_nt(dy1b[None], W["swa_o"][None], F32, "swa_do")
    dq_b, dk_b, dv_b, dsinks = _swa_attn_bwd(qk, proj_b, sinks, do_b, lse_b, Hq, token, "swa_attn_bwd")
    dproj_b = _rope([(dq_b, True), (dk_b, True), (dv_b, False)], tabs_b, BF16, "rope_bwd")
    token = on_grads("swa_w_in", _mm_tn(h1b, dproj_b[None], BF16, "swa_dwin")[0])
    dh1b = _mm_nt(dproj_b[None], W["swa_in"][None], F32, "swa_dh", tk=1280)
    dx2, dsc1b, dsh1b = _mod_bwd(dz1b, dh1b, x2, sc1b, token, "mod_mix1_bwd")

    dz2a, dy2a, dg_f0, db_f0, dgate2a = _ln_bwd(dx2, z2a, y2a, g2a, row(P["ln_ffn_g"][0]), "ln_ffn0_bwd")
    dh2a, token, dcw0, dcb0 = ffn_backward(dy2a, a_a, u_a, uc_a, h2a, 0, "0")
    dx1, dsc2a, dsh2a = _mod_bwd(dz2a, dh2a, x1, sc2a, token, "mod_ffn0_bwd")

    dz1a, dy1a, dg_m0, db_m0, dgate1a = _ln_bwd(dx1, z1a, y1a, g1a, row(P["ln_mix_g"][0]), "ln_mix0_bwd")
    token = on_grads("fox_w_o", _mm_tn(o_a, dy1a[None], BF16, "fox_dwo")[0])
    do_a = _mm_nt(dy1a[None], W["fox_o"][None], F32, "fox_do")
    dq_a, dk_a, dv_a, dcum_row, dcum_col = _fox_attn_bwd(proj_a, do_a, o_a, cum_row, lse_a, Hf, token, "fox_attn_bwd")
    dflT, dbf = _fox_prep_bwd(dcum_row.reshape(Hf, S), dcum_col.reshape(Hf, S), flT, bf_col, "fox_cumsum_bwd")
    n_pad = W["fox_in"].shape[1]
    dproj_a = jnp.concatenate([dq_a, dk_a.astype(BF16), dv_a.astype(BF16),
                               _pad_cols(dflT.T, n_pad - 3 * D).astype(BF16)], axis=1)
    dh1a = _mm_nt(dproj_a[None], W["fox_in"][None], F32, "fox_dh", tk=896)
    grad_x, dsc1a, dsh1a = _mod_bwd(dz1a, dh1a, x, sc1a, token, "mod_mix0_bwd")

    dmod = jnp.stack([jnp.concatenate([dsh1a, dsc1a, dgate1a, dsh2a, dsc2a, dgate2a], axis=1)[0],
                      jnp.concatenate([dsh1b, dsc1b, dgate1b, dsh2b, dsc2b, dgate2b], axis=1)[0]])
    small = dict(dmod=dmod, conv_b=jnp.stack([dcb0, dcb1]), conv_w=jnp.stack([dcw0, dcw1]),
                 ln_mix_g=jnp.concatenate([dg_m0, dg_m1]), ln_mix_b=jnp.concatenate([db_m0, db_m1]),
                 ln_ffn_g=jnp.concatenate([dg_f0, dg_f1]), ln_ffn_b=jnp.concatenate([db_f0, db_f1]),
                 fox_b_f=dbf.reshape(-1), swa_sinks=dsinks[0, :Hq], loss=loss_row[0, 0].reshape(1))
    exchanged = on_small(small)
    on_grads("fox_w_in", _mm_tn(h1a, dproj_a[None], BF16, "fox_dwin", tn=896)[0], exchanged)
    return grad_x


SMALL_ORDER = ("dmod", "conv_b", "conv_w", "ln_mix_g", "ln_mix_b", "ln_ffn_g", "ln_ffn_b", "fox_b_f", "swa_sinks", "loss")


def _pack_rows(arrays):
    chunks, spans, off = [], [], 0
    for a in arrays:
        flat = a.reshape(-1)
        n = -(-flat.shape[0] // LANE) * LANE
        chunks.append(jnp.pad(flat, (0, n - flat.shape[0])))
        spans.append((off, flat.shape[0], a.shape))
        off += n
    total = -(-off // (8 * LANE)) * (8 * LANE)
    chunks.append(jnp.zeros((total - off,), F32))
    return jnp.concatenate(chunks).reshape(-1, LANE), spans


def _unpack_rows(packed, spans):
    flat = packed.reshape(-1)
    return [flat[off:off + n].reshape(shape) for off, n, shape in spans]


def kernel(x, c, positions, fox_w_in, fox_b_f, fox_w_o, swa_w_in, swa_sinks, swa_w_o, ada_w, ada_b, ffn_w_up, ffn_conv_w, ffn_conv_b, ffn_w_down, ln_mix_g, ln_mix_b, ln_ffn_g, ln_ffn_b, loss_target, m_fox_w_in, m_fox_b_f, m_fox_w_o, m_swa_w_in, m_swa_sinks, m_swa_w_o, m_ada_w, m_ada_b, m_ffn_w_up, m_ffn_conv_w, m_ffn_conv_b, m_ffn_w_down, m_ln_mix_g, m_ln_mix_b, m_ln_ffn_g, m_ln_ffn_b, v_fox_w_in, v_fox_b_f, v_fox_w_o, v_swa_w_in, v_swa_sinks, v_swa_w_o, v_ada_w, v_ada_b, v_ffn_w_up, v_ffn_conv_w, v_ffn_conv_b, v_ffn_w_down, v_ln_mix_g, v_ln_mix_b, v_ln_ffn_g, v_ln_ffn_b):
    S, D = x.shape[1], x.shape[2]
    L = ada_w.shape[0]
    me = 4 * lax.axis_index("x") + 2 * lax.axis_index("y") + lax.axis_index("c")
    n_ada = ada_w.shape[2]
    cu = ffn_w_up.shape[2]
    F = 4 * cu
    n_in = fox_w_in.shape[2] * N_DEV
    n_in_pad = -(-n_in // LANE) * LANE

    gather_groups = dict(
        fox=([fox_w_in[0].astype(BF16), fox_w_o[0].astype(BF16)], ["major", "rows"]),
        ffn0=([ffn_w_up[0].astype(BF16), ffn_w_down[0].astype(BF16)], ["halves", "rows"]),
        swa=([swa_w_in[0].astype(BF16), swa_w_o[0].astype(BF16)], ["major", "rows"]),
        ffn1=([ffn_w_up[1].astype(BF16), ffn_w_down[1].astype(BF16)], ["halves", "rows"]))
    starts_after = dict(fox=["ffn0", "swa"], ffn0=["ffn1"])
    gathers = {}

    def start_group(group, after):
        shards, kinds = gather_groups[group]
        send, recv, local, thru, bufs, token = _gather_start(shards, kinds, after, "gather_start_" + group)
        gathers[group] = dict(send=send, recv=recv, local=local, shards=thru, bufs=bufs, kinds=kinds,
                              shapes=[a.shape for a in shards], token=token)
        return token

    fox_started = start_group("fox", None)

    c_all = _all_gather_small(c.reshape(-1, LANE), "gather_c", after=[fox_started]).reshape(N_DEV, D)
    b_cols = lax.dynamic_slice_in_dim(ada_b, me * n_ada, n_ada, axis=1).reshape(L, 1, n_ada)
    mod_blk = _ada_mod(jnp.pad(c_all, ((0, ADA_ROWS - N_DEV), (0, 0))), ada_w, b_cols, "ada_mod")[:, :N_DEV]
    mod_all = _all_gather_small(mod_blk.reshape(-1, LANE), "gather_mod").reshape(N_DEV, L, N_DEV, n_ada)
    mod_mine = lax.dynamic_index_in_dim(mod_all, me, axis=2, keepdims=False)
    mod_mine = jnp.transpose(mod_mine, (1, 0, 2)).reshape(L, N_DEV * n_ada)
    mods = [[mod_mine[l, k * D:(k + 1) * D].reshape(1, D) for k in range(6)] for l in range(L)]

    P = dict(fox_b_f=fox_b_f[0], swa_sinks=swa_sinks[0], conv_b=ffn_conv_b,
             ln_mix_g=ln_mix_g, ln_mix_b=ln_mix_b, ln_ffn_g=ln_ffn_g, ln_ffn_b=ln_ffn_b)
    cw_rows = _all_gather_small(_pack_rows([ffn_conv_w])[0], "gather_conv_w", after=[mod_all])
    n_cw = ffn_conv_w.size
    cw_dev = cw_rows.reshape(N_DEV, -1)[:, :n_cw].reshape(N_DEV, L, 3, cu)
    P["conv_w"] = jnp.transpose(cw_dev, (1, 2, 0, 3)).reshape(L, 3, N_DEV * cu)

    def natural(g, pad_to=None):
        w = jnp.transpose(g, (1, 0, 2)).reshape(D, -1)
        return w if pad_to is None else _pad_cols(w, pad_to)

    def forward_stage(group, after):
        s = gathers[group]
        s["fsend"], s["frecv"], s["bufs"], token = _gather_forward(s["recv"], s["bufs"], s["kinds"], s["shapes"], after,
                                                                   "gather_forward_" + group)
        for nxt in starts_after.get(group, ()):
            token = start_group(nxt, token)
        return token

    def fetch(group, stage, after):
        if stage == "forward":
            return forward_stage(group, after)
        if group == "fox":
            after = forward_stage(group, cw_rows)
        s = gathers.pop(group)
        first, second = _gather_wait(s["send"], s["recv"], s["local"], s["fsend"], s["frecv"], s["shards"], s["bufs"],
                                     s["kinds"], after, "gather_wait_" + group)
        if group == "fox":
            return natural(first, n_in_pad), second
        if group == "swa":
            return natural(first), second
        return first, second

    out, pending = {}, {}

    def columns_major(g):
        return jnp.transpose(g.reshape(D, N_DEV, -1), (1, 2, 0))

    def transposed(a):
        return jnp.transpose(a, (0, 2, 1))

    big = dict(
        ffn_w_down1=("ffn_w_down", "rows", 1, (ffn_w_down, m_ffn_w_down, v_ffn_w_down)),
        ffn_w_up1=("ffn_w_up", "halves", 1, (ffn_w_up, m_ffn_w_up, v_ffn_w_up)),
        swa_w_o=("swa_w_o", "rows", 0, (swa_w_o, m_swa_w_o, v_swa_w_o)),
        swa_w_in=("swa_w_in", "major", 0, tuple(transposed(a) for a in (swa_w_in, m_swa_w_in, v_swa_w_in))),
        ffn_w_down0=("ffn_w_down", "rows", 0, (ffn_w_down, m_ffn_w_down, v_ffn_w_down)),
        ffn_w_up0=("ffn_w_up", "halves", 0, (ffn_w_up, m_ffn_w_up, v_ffn_w_up)),
        fox_w_o=("fox_w_o", "rows", 0, (fox_w_o, m_fox_w_o, v_fox_w_o)),
        fox_w_in=("fox_w_in", "major", 0, tuple(transposed(a) for a in (fox_w_in, m_fox_w_in, v_fox_w_in))))
    finish_at = dict(swa_w_o=["ffn_w_down1"], ffn_w_up0=["ffn_w_up1", "swa_w_o", "swa_w_in"], fox_w_o=["ffn_w_down0"],
                     fox_w_in=["ffn_w_up0"])
    tail = {}

    def finish(name, after):
        send, recv, thru, lands = pending.pop(name)
        param, kind, layer, wmv = big[name]
        landed, = _scatter_wait(send, recv, thru, lands, [kind], after, "scatter_wait_" + name)
        res = _adam_sum(landed, *wmv, layer, out.get(param), "adam_" + name, by_cols=kind == "major")
        out[param] = [transposed(r) for r in res] if kind == "major" else res

    def on_grads(name, g, after=None):
        kind = big[name][1]
        if name == "fox_w_in":
            g = g[:, :n_in]
        src = columns_major(g) if kind == "major" else g
        send, recv, thru, lands, token = _scatter_start([src], [kind], after, "scatter_start_" + name)
        pending[name] = (send, recv, thru, lands)
        for done in finish_at.get(name, ()):
            finish(done, [token])
        tail["last_start"] = token
        return token

    def on_small(small):
        packed, tail["spans"] = _pack_rows([small[k] for k in SMALL_ORDER])
        tail["gathered"] = _all_gather_small(packed, "gather_small_grads")
        return tail["gathered"]

    grad_x = _local_step(x[0], loss_target[0], positions[0], mods, fetch, P, on_grads, on_small)

    gathered, spans = tail["gathered"], tail["spans"]
    totals = dict(zip(SMALL_ORDER, _unpack_rows(_sum_slots(gathered, tail["last_start"], "sum_small_grads"), spans)))
    loss = totals["loss"].reshape(())
    n_mod = L * 6 * D
    dmod_all = gathered.reshape(N_DEV, -1)[:, :n_mod].reshape(N_DEV, L, 6 * D)
    dmod_cols = jnp.transpose(lax.dynamic_slice_in_dim(dmod_all, me * n_ada, n_ada, axis=2), (1, 0, 2))
    g_ada_w = _ada_bwd(c_all.T, dmod_cols, "ada_w_grad")
    out["ada_w"] = (g_ada_w,) + tuple(_adam(g_ada_w, ada_w, m_ada_w, v_ada_w, "adam_ada_w"))

    g_small = dict(fox_b_f=totals["fox_b_f"].reshape(fox_b_f.shape), swa_sinks=totals["swa_sinks"].reshape(swa_sinks.shape),
                   ada_b=totals["dmod"].reshape(ada_b.shape), ffn_conv_b=totals["conv_b"].reshape(ffn_conv_b.shape),
                   ffn_conv_w=lax.dynamic_slice_in_dim(totals["conv_w"].reshape(L, 3, 2 * F), me * cu, cu, axis=2),
                   ln_mix_g=totals["ln_mix_g"], ln_mix_b=totals["ln_mix_b"],
                   ln_ffn_g=totals["ln_ffn_g"], ln_ffn_b=totals["ln_ffn_b"])
    small_names = ("fox_b_f", "swa_sinks", "ada_b", "ffn_conv_b", "ffn_conv_w", "ln_mix_g", "ln_mix_b", "ln_ffn_g", "ln_ffn_b")
    w_small = dict(fox_b_f=(fox_b_f, m_fox_b_f, v_fox_b_f), swa_sinks=(swa_sinks, m_swa_sinks, v_swa_sinks),
                   ada_b=(ada_b, m_ada_b, v_ada_b), ffn_conv_b=(ffn_conv_b, m_ffn_conv_b, v_ffn_conv_b),
                   ffn_conv_w=(ffn_conv_w, m_ffn_conv_w, v_ffn_conv_w),
                   ln_mix_g=(ln_mix_g, m_ln_mix_g, v_ln_mix_g), ln_mix_b=(ln_mix_b, m_ln_mix_b, v_ln_mix_b),
                   ln_ffn_g=(ln_ffn_g, m_ln_ffn_g, v_ln_ffn_g), ln_ffn_b=(ln_ffn_b, m_ln_ffn_b, v_ln_ffn_b))
    pk_g, sp = _pack_rows([g_small[k] for k in small_names])
    pk_w = _pack_rows([w_small[k][0] for k in small_names])[0]
    pk_m = _pack_rows([w_small[k][1] for k in small_names])[0]
    pk_v = _pack_rows([w_small[k][2] for k in small_names])[0]
    res = _adam(pk_g[None], pk_w[None], pk_m[None], pk_v[None], "adam_small")
    settled = [res[0], out["ada_w"][1]] + [out[k][1] for k in ("ffn_w_up", "ffn_w_down", "swa_w_in", "swa_w_o")]
    finish("fox_w_o", settled)
    finish("fox_w_in", settled)
    res = [dict(zip(small_names, _unpack_rows(r[0], sp))) for r in res]
    for k in small_names:
        out[k] = (g_small[k], res[0][k], res[1][k], res[2][k])

    order = ("fox_w_in", "fox_b_f", "fox_w_o", "swa_w_in", "swa_sinks", "swa_w_o", "ada_w", "ada_b", "ffn_w_up",
             "ffn_conv_w", "ffn_conv_b", "ffn_w_down", "ln_mix_g", "ln_mix_b", "ln_ffn_g", "ln_ffn_b")
    return (loss, grad_x[None], *[out[k][0] for k in order], *[out[k][1] for k in order],
            *[out[k][2] for k in order], *[out[k][3] for k in order])
```

```python
import functools

import jax
import jax.numpy as jnp
from jax import lax
from jax.experimental import pallas as pl
from jax.experimental.pallas import tpu as pltpu

F32 = jnp.float32
BF16 = jnp.bfloat16
MESH = pl.DeviceIdType.MESH
N_DEV = 8
AXES = ("x", "y", "c")

DEPTH = 2
ALPHA = (2.0 * DEPTH) ** 0.25
LN_EPS = 1e-5
FOX_HEAD_DIM = 128
SWA_HEAD_DIM = 64
SWA_GROUP = 8
SWA_WINDOW = 128
Q_BLOCK = 128
ROPE_DIM = 16
ROPE_THETA = 500000.0

ADAM_LR = 0.001
ADAM_B1 = 0.9
ADAM_B2 = 0.999
ADAM_EPS = 1e-08
ADAM_WD = 0.01
ADAM_STEP = 10

LANE = 128
MIB = 1024 * 1024


def _tile(n, pref, unit=LANE):
    if n <= pref:
        return n
    t = (pref // unit) * unit
    while t >= unit:
        if n % t == 0:
            return t
        t -= unit
    return n


def _params(sem, vmem_mib=48):
    return pltpu.CompilerParams(dimension_semantics=sem, vmem_limit_bytes=vmem_mib * MIB)


def _sigmoid(x):
    return 1.0 / (1.0 + jnp.exp(-x))


def _mm_call(dot, grid_mnk, in_specs, out_spec, out_shape, k_axis, nk, tm, tn, name, operands):
    sem = ("parallel",) * (len(grid_mnk) - 1) + ("arbitrary",)

    if nk == 1:
        def body(a_ref, b_ref, o_ref):
            o_ref[...] = dot(a_ref[...], b_ref[...]).astype(o_ref.dtype)
        scratch = []
    else:
        def body(a_ref, b_ref, o_ref, acc_ref):
            k = pl.program_id(k_axis)

            @pl.when(k == 0)
            def _():
                acc_ref[...] = jnp.zeros_like(acc_ref)

            acc_ref[...] += dot(a_ref[...], b_ref[...])

            @pl.when(k == nk - 1)
            def _():
                o_ref[...] = acc_ref[...].astype(o_ref.dtype)
        scratch = [pltpu.VMEM((tm, tn), F32)]

    return pl.pallas_call(
        body, name=name, grid=grid_mnk, in_specs=in_specs, out_specs=out_spec, out_shape=out_shape,
        scratch_shapes=scratch, compiler_params=_params(sem, 56),
    )(*operands)


def _dot(dims):
    def dot(a, b):
        return lax.dot_general(a.astype(BF16), b.astype(BF16), (dims, ((), ())), preferred_element_type=F32)
    return dot


def _mm_nn(a, b, out_dtype, name, tm=2048, tn=512, tk=2048):
    M, K = a.shape
    N = b.shape[1]
    tm, tn, tk = _tile(M, tm), _tile(N, tn), _tile(K, tk)
    nk = K // tk
    return _mm_call(
        _dot(((1,), (0,))), (M // tm, N // tn, nk),
        [pl.BlockSpec((tm, tk), lambda i, j, k: (i, k)), pl.BlockSpec((tk, tn), lambda i, j, k: (k, j))],
        pl.BlockSpec((tm, tn), lambda i, j, k: (i, j)), jax.ShapeDtypeStruct((M, N), out_dtype),
        2, nk, tm, tn, name, (a, b))


def _mm_nt(a, b, out_dtype, name, tm=2048, tn=512, tk=2048):
    P, M, K = a.shape
    N = b.shape[1]
    tm, tn, tk = _tile(M, tm), _tile(N, tn), _tile(K, tk)
    nk = K // tk
    return _mm_call(
        _dot(((1,), (1,))), (M // tm, N // tn, P * nk),
        [pl.BlockSpec((None, tm, tk), lambda i, j, k: (k // nk, i, k % nk)),
         pl.BlockSpec((None, tn, tk), lambda i, j, k: (k // nk, j, k % nk))],
        pl.BlockSpec((tm, tn), lambda i, j, k: (i, j)), jax.ShapeDtypeStruct((M, N), out_dtype),
        2, P * nk, tm, tn, name, (a, b))


def _mm_tn(a, b, out_dtype, name, tm=2048, tn=512, tk=2048):
    K, M = a.shape
    P, _, N = b.shape
    tm, tn, tk = _tile(M, tm), _tile(N, tn), _tile(K, tk)
    nk = K // tk
    return _mm_call(
        _dot(((0,), (0,))), (P, M // tm, N // tn, nk),
        [pl.BlockSpec((tk, tm), lambda p, i, j, k: (k, i)), pl.BlockSpec((None, tk, tn), lambda p, i, j, k: (p, k, j))],
        pl.BlockSpec((None, tm, tn), lambda p, i, j, k: (p, i, j)), jax.ShapeDtypeStruct((P, M, N), out_dtype),
        3, nk, tm, tn, name, (a, b))


ROW_TILE = 256


def _row_spec(tm, D):
    return pl.BlockSpec((tm, D), lambda i: (i, 0))


def _vec_spec(D):
    return pl.BlockSpec((1, D), lambda i: (0, 0))


def _modulate(x, sc, sh, name):
    S, D = x.shape
    tm = _tile(S, ROW_TILE, 8)

    def body(x_ref, sc_ref, sh_ref, h_ref):
        h_ref[...] = (x_ref[...] * (1.0 + sc_ref[...]) + sh_ref[...]).astype(BF16)

    return pl.pallas_call(
        body, name=name, grid=(S // tm,),
        in_specs=[_row_spec(tm, D), _vec_spec(D), _vec_spec(D)],
        out_specs=_row_spec(tm, D),
        out_shape=jax.ShapeDtypeStruct((S, D), BF16),
        compiler_params=_params(("parallel",)),
    )(x, sc, sh)


def _layer_norm_rows(z, gamma, beta):
    mu = jnp.mean(z, axis=-1, keepdims=True)
    zc = z - mu
    var = jnp.mean(zc * zc, axis=-1, keepdims=True)
    return zc * lax.rsqrt(var + LN_EPS) * gamma + beta


def _ln_fwd(x, y, gate, gamma, beta, sc_n, sh_n, after, name):
    S, D = x.shape
    tm = _tile(S, ROW_TILE, 8)

    def body(x_ref, y_ref, gate_ref, g_ref, b_ref, sc_ref, sh_ref, after_ref, z_ref, xo_ref, hn_ref):
        z = ALPHA * x_ref[...] + (1.0 + gate_ref[...]) * y_ref[...]
        xo = _layer_norm_rows(z, g_ref[...], b_ref[...])
        z_ref[...] = z
        xo_ref[...] = xo
        hn_ref[...] = (xo * (1.0 + sc_ref[...]) + sh_ref[...]).astype(BF16)

    return pl.pallas_call(
        body, name=name, grid=(S // tm,),
        in_specs=[_row_spec(tm, D), _row_spec(tm, D)] + [_vec_spec(D)] * 5 + [pl.BlockSpec(memory_space=pl.ANY)],
        out_specs=[_row_spec(tm, D)] * 3,
        out_shape=[jax.ShapeDtypeStruct((S, D), F32), jax.ShapeDtypeStruct((S, D), F32),
                   jax.ShapeDtypeStruct((S, D), BF16)],
        compiler_params=_params(("parallel",)),
    )(x, y, gate, gamma, beta, sc_n, sh_n, after)


def _ln_fwd_loss(x, y, gate, gamma, beta, target, name):
    S, D = x.shape
    tm = _tile(S, ROW_TILE, 8)

    def body(x_ref, y_ref, gate_ref, g_ref, b_ref, t_ref, z_ref, dout_ref, loss_ref):
        @pl.when(pl.program_id(0) == 0)
        def _():
            loss_ref[...] = jnp.zeros_like(loss_ref)

        z = ALPHA * x_ref[...] + (1.0 + gate_ref[...]) * y_ref[...]
        xo = _layer_norm_rows(z, g_ref[...], b_ref[...])
        err = xo - t_ref[...]
        z_ref[...] = z
        dout_ref[...] = err * (1.0 / D)
        loss_ref[...] += (0.5 / D) * jnp.sum(err * err)

    return pl.pallas_call(
        body, name=name, grid=(S // tm,),
        in_specs=[_row_spec(tm, D), _row_spec(tm, D)] + [_vec_spec(D)] * 3 + [_row_spec(tm, D)],
        out_specs=[_row_spec(tm, D), _row_spec(tm, D), pl.BlockSpec((1, LANE), lambda i: (0, 0))],
        out_shape=[jax.ShapeDtypeStruct((S, D), F32), jax.ShapeDtypeStruct((S, D), F32),
                   jax.ShapeDtypeStruct((1, LANE), F32)],
        compiler_params=_params(("arbitrary",)),
    )(x, y, gate, gamma, beta, target)


def _ln_bwd(dout, z, y, gate, gamma, name):
    S, D = z.shape
    tm = _tile(S, ROW_TILE, 8)

    def body(dout_ref, z_ref, y_ref, gate_ref, g_ref, dz_ref, dy_ref, dg_ref, db_ref, dgate_ref):
        @pl.when(pl.program_id(0) == 0)
        def _():
            dg_ref[...] = jnp.zeros_like(dg_ref)
            db_ref[...] = jnp.zeros_like(db_ref)
            dgate_ref[...] = jnp.zeros_like(dgate_ref)

        z = z_ref[...]
        dout = dout_ref[...]
        mu = jnp.mean(z, axis=-1, keepdims=True)
        zc = z - mu
        var = jnp.mean(zc * zc, axis=-1, keepdims=True)
        rstd = lax.rsqrt(var + LN_EPS)
        xhat = zc * rstd
        dxhat = dout * g_ref[...]
        m1 = jnp.mean(dxhat, axis=-1, keepdims=True)
        m2 = jnp.mean(dxhat * xhat, axis=-1, keepdims=True)
        dz = rstd * (dxhat - m1 - xhat * m2)
        dz_ref[...] = dz
        dy_ref[...] = (dz * (1.0 + gate_ref[...])).astype(BF16)
        dg_ref[...] += jnp.sum(dout * xhat, axis=0, keepdims=True)
        db_ref[...] += jnp.sum(dout, axis=0, keepdims=True)
        dgate_ref[...] += jnp.sum(dz * y_ref[...], axis=0, keepdims=True)

    return pl.pallas_call(
        body, name=name, grid=(S // tm,),
        in_specs=[_row_spec(tm, D)] * 3 + [_vec_spec(D)] * 2,
        out_specs=[_row_spec(tm, D), _row_spec(tm, D)] + [_vec_spec(D)] * 3,
        out_shape=[jax.ShapeDtypeStruct((S, D), F32), jax.ShapeDtypeStruct((S, D), BF16)]
        + [jax.ShapeDtypeStruct((1, D), F32)] * 3,
        compiler_params=_params(("arbitrary",)),
    )(dout, z, y, gate, gamma)


def _mod_bwd(dz, dh, xin, sc, after, name):
    S, D = dz.shape
    tm = _tile(S, ROW_TILE, 8)

    def body(dz_ref, dh_ref, x_ref, sc_ref, after_ref, dx_ref, dsc_ref, dsh_ref):
        @pl.when(pl.program_id(0) == 0)
        def _():
            dsc_ref[...] = jnp.zeros_like(dsc_ref)
            dsh_ref[...] = jnp.zeros_like(dsh_ref)

        dh = dh_ref[...]
        dx_ref[...] = ALPHA * dz_ref[...] + dh * (1.0 + sc_ref[...])
        dsc_ref[...] += jnp.sum(dh * x_ref[...], axis=0, keepdims=True)
        dsh_ref[...] += jnp.sum(dh, axis=0, keepdims=True)

    return pl.pallas_call(
        body, name=name, grid=(S // tm,),
        in_specs=[_row_spec(tm, D)] * 3 + [_vec_spec(D), pl.BlockSpec(memory_space=pl.ANY)],
        out_specs=[_row_spec(tm, D), _vec_spec(D), _vec_spec(D)],
        out_shape=[jax.ShapeDtypeStruct((S, D), F32)] + [jax.ShapeDtypeStruct((1, D), F32)] * 2,
        compiler_params=_params(("arbitrary",)),
    )(dz, dh, xin, sc, after)


def _shift_down(u, k, row):
    return jnp.where(row >= k, pltpu.roll(u, k, axis=0), 0.0)


def _shift_up(u, k, row, S):
    return jnp.where(row < S - k, pltpu.roll(u, S - k, axis=0), 0.0)


def _ffn_up(h, w, cw, cb, name):
    S, D = h.shape
    F = w.shape[2]
    tn = _tile(F, 256)

    def body(h_ref, w_ref, cw_ref, cb_ref, u_ref, uc_ref, a_ref):
        hh = h_ref[...]
        row = lax.broadcasted_iota(jnp.int32, (S, tn), 0)
        conv = []
        for p in range(2):
            u = jnp.dot(hh, w_ref[p], preferred_element_type=F32)
            u_ref[p] = u.astype(BF16)
            cwp = cw_ref[p]
            uc = _shift_down(u, 2, row) * cwp[0:1] + _shift_down(u, 1, row) * cwp[1:2] + u * cwp[2:3] + cb_ref[p]
            uc_ref[p] = uc.astype(BF16)
            conv.append(uc)
        g, v = conv
        a_ref[...] = (g * _sigmoid(g) * v).astype(BF16)

    half = pl.BlockSpec((2, S, tn), lambda j: (0, 0, j))
    return pl.pallas_call(
        body, name=name, grid=(F // tn,),
        in_specs=[pl.BlockSpec((S, D), lambda j: (0, 0)), pl.BlockSpec((2, D, tn), lambda j: (0, 0, j)),
                  pl.BlockSpec((2, 3, tn), lambda j: (0, 0, j)), pl.BlockSpec((2, 1, tn), lambda j: (0, 0, j))],
        out_specs=[half, half, pl.BlockSpec((S, tn), lambda j: (0, j))],
        out_shape=[jax.ShapeDtypeStruct((2, S, F), BF16), jax.ShapeDtypeStruct((2, S, F), BF16),
                   jax.ShapeDtypeStruct((S, F), BF16)],
        compiler_params=_params(("parallel",), 56),
    )(h, w, cw, cb)


def _ffn_bwd_elem(da, u, uc, cw, after, name):
    _, S, F = u.shape
    tn = _tile(F, 256)

    def body(da_ref, u_ref, uc_ref, cw_ref, after_ref, du_ref, dcw_ref, dcb_ref):
        row = lax.broadcasted_iota(jnp.int32, (S, tn), 0)
        da = da_ref[...]
        g, v = uc_ref[0].astype(F32), uc_ref[1].astype(F32)
        sg = _sigmoid(g)
        d_conv = (da * v * (sg * (1.0 + g * (1.0 - sg))), da * (g * sg))
        for p in range(2):
            d = d_conv[p]
            cwp = cw_ref[p]
            u = u_ref[p].astype(F32)
            d1, d2 = _shift_up(d, 1, row, S), _shift_up(d, 2, row, S)
            dcb_ref[p] = jnp.sum(d, axis=0, keepdims=True)
            dcw_ref[p, 0:1, :] = jnp.sum(d2 * u, axis=0, keepdims=True)
            dcw_ref[p, 1:2, :] = jnp.sum(d1 * u, axis=0, keepdims=True)
            dcw_ref[p, 2:3, :] = jnp.sum(d * u, axis=0, keepdims=True)
            du_ref[p] = (d * cwp[2:3] + d1 * cwp[1:2] + d2 * cwp[0:1]).astype(BF16)

    half = pl.BlockSpec((2, S, tn), lambda j: (0, 0, j))
    return pl.pallas_call(
        body, name=name, grid=(F // tn,),
        in_specs=[pl.BlockSpec((S, tn), lambda j: (0, j)), half, half, pl.BlockSpec((2, 3, tn), lambda j: (0, 0, j)),
                  pl.BlockSpec(memory_space=pl.ANY)],
        out_specs=[half, pl.BlockSpec((2, 3, tn), lambda j: (0, 0, j)), pl.BlockSpec((2, 1, tn), lambda j: (0, 0, j))],
        out_shape=[jax.ShapeDtypeStruct((2, S, F), BF16), jax.ShapeDtypeStruct((2, 3, F), F32),
                   jax.ShapeDtypeStruct((2, 1, F), F32)],
        compiler_params=_params(("parallel",), 56),
    )(da, u, uc, cw, after)


def _split3(x):
    hi = x.astype(BF16)
    r1 = x - hi.astype(F32)
    mid = r1.astype(BF16)
    lo = (r1 - mid.astype(F32)).astype(BF16)
    return hi, mid, lo


def _tri_matmul(x, upper, S):
    tc = _tile(S, 512)
    parts = _split3(x)
    outs = []
    for b in range(S // tc):
        r = lax.broadcasted_iota(jnp.int32, (S, tc), 0)
        c = lax.broadcasted_iota(jnp.int32, (S, tc), 1) + b * tc
        tri = jnp.where((r <= c) if upper else (r >= c), 1.0, 0.0).astype(BF16)
        acc = jnp.dot(parts[0], tri, preferred_element_type=F32)
        acc += jnp.dot(parts[1], tri, preferred_element_type=F32)
        acc += jnp.dot(parts[2], tri, preferred_element_type=F32)
        outs.append(acc)
    return outs, tc


def _fox_prep(flT, bf, name):
    H, S = flT.shape

    def body(fl_ref, b_ref, cum_ref):
        zz = fl_ref[...] + b_ref[...]
        lf = jnp.minimum(zz, 0.0) - jnp.log(1.0 + jnp.exp(-jnp.abs(zz)))
        outs, tc = _tri_matmul(lf, True, S)
        for b, o in enumerate(outs):
            cum_ref[:, b * tc:(b + 1) * tc] = o

    return pl.pallas_call(
        body, name=name,
        in_specs=[pl.BlockSpec(memory_space=pltpu.VMEM)] * 2,
        out_specs=pl.BlockSpec(memory_space=pltpu.VMEM),
        out_shape=jax.ShapeDtypeStruct((H, S), F32),
        compiler_params=pltpu.CompilerParams(vmem_limit_bytes=48 * MIB),
    )(flT, bf)


def _fox_prep_bwd(dcum_key, dcum_query, flT, bf, name):
    H, S = flT.shape

    def body(dck_ref, dcq_ref, fl_ref, b_ref, dfl_ref, dbf_ref):
        zz = fl_ref[...] + b_ref[...]
        outs, tc = _tri_matmul(dck_ref[...] + dcq_ref[...], False, S)
        total = jnp.zeros((H, 1), F32)
        for b, o in enumerate(outs):
            dfl = o * _sigmoid(-zz[:, b * tc:(b + 1) * tc])
            dfl_ref[:, b * tc:(b + 1) * tc] = dfl
            total += jnp.sum(dfl, axis=1, keepdims=True)
        dbf_ref[...] = total

    return pl.pallas_call(
        body, name=name,
        in_specs=[pl.BlockSpec(memory_space=pltpu.VMEM)] * 4,
        out_specs=[pl.BlockSpec(memory_space=pltpu.VMEM)] * 2,
        out_shape=[jax.ShapeDtypeStruct((H, S), F32), jax.ShapeDtypeStruct((H, 1), F32)],
        compiler_params=pltpu.CompilerParams(vmem_limit_bytes=48 * MIB),
    )(dcum_key, dcum_query, flT, bf)


FOX_TQ = 256
FOX_TC = 512


def _fox_scores(q, k_ref, ck_ref, i, lo, n, tq):
    k = k_ref[lo:lo + n, :].astype(BF16)
    s = lax.dot_general(q, k, (((1,), (1,)), ((), ())), preferred_element_type=F32) - ck_ref[:, lo:lo + n]
    qpos = i * tq + lax.broadcasted_iota(jnp.int32, (tq, n), 0)
    kpos = lo + lax.broadcasted_iota(jnp.int32, (tq, n), 1)
    return jnp.where(kpos <= qpos, s, -jnp.inf)


FOX_FWD_SPLITS = 4


def _fox_attn_fwd_part(proj, cum_row, H, q_lo, q_hi, filled, name):
    S = proj.shape[0]
    dh = FOX_HEAD_DIM
    tq = _tile(q_hi - q_lo, FOX_TQ)
    first = q_lo // tq
    scale = dh ** -0.5
    n_keep = 0 if filled is None else 2

    def body(q_ref, k_ref, v_ref, ck_ref, *rest):
        o_ref, lse_ref = rest[n_keep:]
        q = (q_ref[...] * scale).astype(BF16)
        s = _fox_scores(q, k_ref, ck_ref, first + pl.program_id(1), 0, q_hi, tq)
        m = jnp.max(s, axis=-1, keepdims=True)
        p = jnp.exp(s - m)
        l = jnp.sum(p, axis=-1, keepdims=True)
        o = jnp.dot(p.astype(BF16), v_ref[...].astype(BF16), preferred_element_type=F32) / l
        o_ref[...] = o.astype(BF16)
        lse_ref[...] = m + jnp.log(l)

    return pl.pallas_call(
        body, name=name, grid=(H, (q_hi - q_lo) // tq),
        in_specs=[pl.BlockSpec((tq, dh), lambda h, i: (first + i, h)),
                  pl.BlockSpec((q_hi, dh), lambda h, i: (0, H + h)),
                  pl.BlockSpec((q_hi, dh), lambda h, i: (0, 2 * H + h)),
                  pl.BlockSpec((None, 1, q_hi), lambda h, i: (h, 0, 0))] + [pl.BlockSpec(memory_space=pl.ANY)] * n_keep,
        out_specs=[pl.BlockSpec((tq, dh), lambda h, i: (first + i, h)),
                   pl.BlockSpec((None, tq, 1), lambda h, i: (h, first + i, 0))],
        out_shape=[jax.ShapeDtypeStruct((S, H * dh), BF16), jax.ShapeDtypeStruct((H, S, 1), F32)],
        input_output_aliases={4 + k: k for k in range(n_keep)},
        compiler_params=_params(("parallel", "parallel")),
    )(proj, proj, proj, cum_row, *(filled or ()))


def _fox_attn_fwd(proj, cum_row, H, name):
    S = proj.shape[0]
    n = FOX_FWD_SPLITS if S % (FOX_FWD_SPLITS * FOX_TQ) == 0 else 1
    out = None
    for part in range(n):
        out = _fox_attn_fwd_part(proj, cum_row, H, part * S // n, (part + 1) * S // n, out, name + str(part))
    return out


def _fox_attn_bwd(proj, do, o, cum_row, lse, H, after, name):
    S = proj.shape[0]
    dh = FOX_HEAD_DIM
    tq = _tile(S, FOX_TQ)
    tc = _tile(S, FOX_TC)
    scale = dh ** -0.5

    def body(q_ref, k_ref, v_ref, do_ref, o_ref, ck_ref, lse_ref, after_ref,
             dq_ref, dk_ref, dv_ref, dck_ref, dcq_ref, acc_ref):
        i = pl.program_id(1)

        @pl.when(i == 0)
        def _():
            dk_ref[...] = jnp.zeros_like(dk_ref)
            dv_ref[...] = jnp.zeros_like(dv_ref)
            dck_ref[...] = jnp.zeros_like(dck_ref)

        acc_ref[...] = jnp.zeros_like(acc_ref)
        dcq_ref[...] = jnp.zeros_like(dcq_ref)
        q = (q_ref[...] * scale).astype(BF16)
        do_f = do_ref[...]
        do_b = do_f.astype(BF16)
        delta = jnp.sum(do_f * o_ref[...].astype(F32), axis=-1, keepdims=True)
        lse_q = lse_ref[...]
        for c in range(S // tc):
            @pl.when(c * tc <= i * tq + tq - 1)
            def _():
                rows = slice(c * tc, (c + 1) * tc)
                p = jnp.exp(_fox_scores(q, k_ref, ck_ref, i, c * tc, tc, tq) - lse_q)
                dp = lax.dot_general(do_b, v_ref[rows, :].astype(BF16), (((1,), (1,)), ((), ())),
                                     preferred_element_type=F32)
                ds = p * (dp - delta)
                ds_b = ds.astype(BF16)
                acc_ref[...] += jnp.dot(ds_b, k_ref[rows, :].astype(BF16), preferred_element_type=F32)
                dk_ref[rows, :] += lax.dot_general(ds_b, q, (((0,), (0,)), ((), ())), preferred_element_type=F32)
                dv_ref[rows, :] += lax.dot_general(p.astype(BF16), do_b, (((0,), (0,)), ((), ())),
                                                   preferred_element_type=F32)
                dck_ref[:, rows] -= jnp.sum(ds, axis=0, keepdims=True)
                dcq_ref[...] += jnp.sum(ds, axis=-1, keepdims=True)
        dq_ref[...] = (acc_ref[...] * scale).astype(BF16)

    W = H * dh
    return pl.pallas_call(
        body, name=name, grid=(H, S // tq),
        in_specs=[pl.BlockSpec((tq, dh), lambda h, i: (i, h)),
                  pl.BlockSpec((S, dh), lambda h, i: (0, H + h)),
                  pl.BlockSpec((S, dh), lambda h, i: (0, 2 * H + h)),
                  pl.BlockSpec((tq, dh), lambda h, i: (i, h)),
                  pl.BlockSpec((tq, dh), lambda h, i: (i, h)),
                  pl.BlockSpec((None, 1, S), lambda h, i: (h, 0, 0)),
                  pl.BlockSpec((None, tq, 1), lambda h, i: (h, i, 0)),
                  pl.BlockSpec(memory_space=pl.ANY)],
        out_specs=[pl.BlockSpec((tq, dh), lambda h, i: (i, h)),
                   pl.BlockSpec((S, dh), lambda h, i: (0, h)),
                   pl.BlockSpec((S, dh), lambda h, i: (0, h)),
                   pl.BlockSpec((None, 1, S), lambda h, i: (h, 0, 0)),
                   pl.BlockSpec((None, tq, 1), lambda h, i: (h, i, 0))],
        out_shape=[jax.ShapeDtypeStruct((S, W), BF16), jax.ShapeDtypeStruct((S, W), F32),
                   jax.ShapeDtypeStruct((S, W), F32), jax.ShapeDtypeStruct((H, 1, S), F32),
                   jax.ShapeDtypeStruct((H, S, 1), F32)],
        scratch_shapes=[pltpu.VMEM((tq, dh), F32)],
        compiler_params=_params(("parallel", "arbitrary")),
    )(proj, proj, proj, do, o, cum_row, lse, after)


def _rope(parts, tabs, out_dtype, name):
    S = parts[0][0].shape[0]
    widths = [w for _, _, w in parts]
    total = sum(widths)
    tm = _tile(S, ROW_TILE, 8)
    flags = [r for _, r, _ in parts]

    def body(*refs):
        in_refs = refs[:len(parts)]
        cos_ref, sa_ref, sb_ref, o_ref = refs[len(parts):]
        cos, sa, sb = cos_ref[...], sa_ref[...], sb_ref[...]
        off = 0
        for ref, rot, w in zip(in_refs, flags, widths):
            for j in range(w // LANE):
                t = ref[:, j * LANE:(j + 1) * LANE]
                if rot:
                    t = t * cos + pltpu.roll(t, LANE - ROPE_DIM // 2, axis=1) * sa + pltpu.roll(t, ROPE_DIM // 2, axis=1) * sb
                o_ref[:, off + j * LANE:off + (j + 1) * LANE] = t.astype(o_ref.dtype)
            off += w

    return pl.pallas_call(
        body, name=name, grid=(S // tm,),
        in_specs=[pl.BlockSpec((tm, w), lambda i: (i, 0)) for w in widths] + [_row_spec(tm, LANE)] * 3,
        out_specs=_row_spec(tm, total),
        out_shape=jax.ShapeDtypeStruct((S, total), out_dtype),
        compiler_params=_params(("parallel",)),
    )(*[a for a, _, _ in parts], *tabs)


def _swa_band(ref_p, ref_c, hk):
    dh = SWA_HEAD_DIM
    return jnp.concatenate([ref_p[:, hk * dh:(hk + 1) * dh], ref_c[:, hk * dh:(hk + 1) * dh]], axis=0).astype(BF16)


def _swa_mask(n, G):
    qi = lax.broadcasted_iota(jnp.int32, (G * Q_BLOCK, 2 * Q_BLOCK), 0) % Q_BLOCK
    kj = lax.broadcasted_iota(jnp.int32, (G * Q_BLOCK, 2 * Q_BLOCK), 1)
    rel = qi + Q_BLOCK - kj
    return (rel >= 0) & (rel < SWA_WINDOW) & ((kj >= Q_BLOCK) | (n > 0))


def _swa_stack(ref, hk, G):
    dh = SWA_HEAD_DIM
    return jnp.concatenate([ref[:, (hk * G + g) * dh:(hk * G + g + 1) * dh] for g in range(G)], axis=0)


def _swa_unstack(ref, stacked, hk, G):
    dh, QB = SWA_HEAD_DIM, Q_BLOCK
    for g in range(0, G, 2):
        c0 = (hk * G + g) * dh
        pair = jnp.concatenate([stacked[g * QB:(g + 1) * QB], stacked[(g + 1) * QB:(g + 2) * QB]], axis=1)
        ref[:, c0:c0 + 2 * dh] = pair.astype(ref.dtype)


def _swa_sink_rows(sink_ref, hk, G):
    return jnp.concatenate([jnp.broadcast_to(sink_ref[0:1, hk * G + g:hk * G + g + 1], (Q_BLOCK, 1)) for g in range(G)],
                           axis=0)


def _swa_attn_fwd(qk, proj, sinks, Hq, name):
    S = qk.shape[0]
    dh, G, QB = SWA_HEAD_DIM, SWA_GROUP, Q_BLOCK
    Hk = Hq // G
    Wq, Wk = Hq * dh, Hk * dh
    nb = S // QB
    scale = dh ** -0.5

    def body(q_ref, kp_ref, kc_ref, vp_ref, vc_ref, sink_ref, o_ref, lse_ref):
        n = pl.program_id(0)
        mask = _swa_mask(n, G)
        lane = lax.broadcasted_iota(jnp.int32, (QB, LANE), 1)
        lse_tile = jnp.zeros((QB, LANE), F32)
        for hk in range(Hk):
            kb = _swa_band(kp_ref, kc_ref, hk)
            vb = _swa_band(vp_ref, vc_ref, hk)
            q = _swa_stack(q_ref, hk, G).astype(BF16)
            sk = _swa_sink_rows(sink_ref, hk, G)
            s = lax.dot_general(q, kb, (((1,), (1,)), ((), ())), preferred_element_type=F32) * scale
            s = jnp.where(mask, s, -jnp.inf)
            m = jnp.maximum(jnp.max(s, axis=-1, keepdims=True), sk)
            p = jnp.exp(s - m)
            l = jnp.sum(p, axis=-1, keepdims=True) + jnp.exp(sk - m)
            o = jnp.dot(p.astype(BF16), vb, preferred_element_type=F32) / l
            lse = m + jnp.log(l)
            for g in range(G):
                lse_tile = jnp.where(lane == hk * G + g, lse[g * QB:(g + 1) * QB], lse_tile)
            _swa_unstack(o_ref, o, hk, G)
        lse_ref[...] = lse_tile

    kcol, vcol = Wq // Wk, (Wq + Wk) // Wk
    return pl.pallas_call(
        body, name=name, grid=(nb,),
        in_specs=[pl.BlockSpec((QB, Wq), lambda n: (n, 0)),
                  pl.BlockSpec((QB, Wk), lambda n: (jnp.maximum(n - 1, 0), kcol)),
                  pl.BlockSpec((QB, Wk), lambda n: (n, kcol)),
                  pl.BlockSpec((QB, Wk), lambda n: (jnp.maximum(n - 1, 0), vcol)),
                  pl.BlockSpec((QB, Wk), lambda n: (n, vcol)),
                  pl.BlockSpec((1, LANE), lambda n: (0, 0))],
        out_specs=[pl.BlockSpec((QB, Wq), lambda n: (n, 0)), pl.BlockSpec((QB, LANE), lambda n: (n, 0))],
        out_shape=[jax.ShapeDtypeStruct((S, Wq), BF16), jax.ShapeDtypeStruct((S, LANE), F32)],
        compiler_params=_params(("parallel",)),
    )(qk, qk, qk, proj, proj, sinks)


def _swa_attn_bwd(qk, proj, sinks, do, lse, Hq, after, name):
    S = qk.shape[0]
    dh, G, QB = SWA_HEAD_DIM, SWA_GROUP, Q_BLOCK
    Hk = Hq // G
    Wq, Wk = Hq * dh, Hk * dh
    nb = S // QB
    scale = dh ** -0.5

    def body(q_ref, kp_ref, kc_ref, vp_ref, vc_ref, sink_ref, do_ref, lse_ref, after_ref,
             dq_ref, dk_ref, dv_ref, dsink_ref, carry_k, carry_v):
        n = pl.program_id(0)

        @pl.when(n == 0)
        def _():
            dsink_ref[...] = jnp.zeros_like(dsink_ref)

        @pl.when(n < nb)
        def _():
            mask = _swa_mask(n, G)
            lane = lax.broadcasted_iota(jnp.int32, (1, LANE), 1)
            dsink = jnp.zeros((1, LANE), F32)
            dk_heads, dv_heads = [], []
            for hk in range(Hk):
                kb = _swa_band(kp_ref, kc_ref, hk)
                vb = _swa_band(vp_ref, vc_ref, hk)
                q = _swa_stack(q_ref, hk, G).astype(BF16)
                do_s = _swa_stack(do_ref, hk, G).astype(BF16)
                lse = jnp.concatenate([lse_ref[:, hk * G + g:hk * G + g + 1] for g in range(G)], axis=0)
                s = lax.dot_general(q, kb, (((1,), (1,)), ((), ())), preferred_element_type=F32) * scale
                s = jnp.where(mask, s, -jnp.inf)
                p = jnp.exp(s - lse)
                p_sink = jnp.exp(_swa_sink_rows(sink_ref, hk, G) - lse)
                dp = lax.dot_general(do_s, vb, (((1,), (1,)), ((), ())), preferred_element_type=F32)
                delta = jnp.sum(p * dp, axis=-1, keepdims=True)
                ds_b = (p * (dp - delta)).astype(BF16)
                _swa_unstack(dq_ref, jnp.dot(ds_b, kb, preferred_element_type=F32) * scale, hk, G)
                dk_heads.append(lax.dot_general(ds_b, q, (((0,), (0,)), ((), ())), preferred_element_type=F32) * scale)
                dv_heads.append(lax.dot_general(p.astype(BF16), do_s, (((0,), (0,)), ((), ())), preferred_element_type=F32))
                sink_term = p_sink * delta
                for g in range(G):
                    dsink = jnp.where(lane == hk * G + g,
                                      -jnp.sum(sink_term[g * QB:(g + 1) * QB], axis=0, keepdims=True), dsink)
            dsink_ref[...] += dsink
            dk_all = jnp.concatenate(dk_heads, axis=1)
            dv_all = jnp.concatenate(dv_heads, axis=1)

            @pl.when(n > 0)
            def _():
                dk_ref[...] = carry_k[...] + dk_all[:QB]
                dv_ref[...] = carry_v[...] + dv_all[:QB]

            carry_k[...] = dk_all[QB:]
            carry_v[...] = dv_all[QB:]

        @pl.when(n == nb)
        def _():
            dk_ref[...] = carry_k[...]
            dv_ref[...] = carry_v[...]

    kcol, vcol = Wq // Wk, (Wq + Wk) // Wk
    cur = lambda n: jnp.minimum(n, nb - 1)
    prev = lambda n: jnp.maximum(jnp.minimum(n, nb - 1) - 1, 0)
    return pl.pallas_call(
        body, name=name, grid=(nb + 1,),
        in_specs=[pl.BlockSpec((QB, Wq), lambda n: (cur(n), 0)),
                  pl.BlockSpec((QB, Wk), lambda n: (prev(n), kcol)),
                  pl.BlockSpec((QB, Wk), lambda n: (cur(n), kcol)),
                  pl.BlockSpec((QB, Wk), lambda n: (prev(n), vcol)),
                  pl.BlockSpec((QB, Wk), lambda n: (cur(n), vcol)),
                  pl.BlockSpec((1, LANE), lambda n: (0, 0)),
                  pl.BlockSpec((QB, Wq), lambda n: (cur(n), 0)),
                  pl.BlockSpec((QB, LANE), lambda n: (cur(n), 0)),
                  pl.BlockSpec(memory_space=pl.ANY)],
        out_specs=[pl.BlockSpec((QB, Wq), lambda n: (cur(n), 0)),
                   pl.BlockSpec((QB, Wk), lambda n: (jnp.maximum(n - 1, 0), 0)),
                   pl.BlockSpec((QB, Wk), lambda n: (jnp.maximum(n - 1, 0), 0)),
                   pl.BlockSpec((1, LANE), lambda n: (0, 0))],
        out_shape=[jax.ShapeDtypeStruct((S, Wq), F32), jax.ShapeDtypeStruct((S, Wk), F32),
                   jax.ShapeDtypeStruct((S, Wk), F32), jax.ShapeDtypeStruct((1, LANE), F32)],
        scratch_shapes=[pltpu.VMEM((QB, Wk), F32), pltpu.VMEM((QB, Wk), F32)],
        compiler_params=_params(("arbitrary",)),
    )(qk, qk, qk, proj, proj, sinks, do, lse, after)


ADA_ROWS = 16


def _ada_mod(c_pad, w, b, name):
    L, D, N = w.shape
    tn = _tile(N, 512)

    def body(c_ref, w_ref, b_ref, o_ref):
        c = c_ref[...]
        c = c * _sigmoid(c)
        ch = c.astype(BF16)
        cl = (c - ch.astype(F32)).astype(BF16)
        ww = w_ref[...]
        wh = ww.astype(BF16)
        wl = (ww - wh.astype(F32)).astype(BF16)
        acc = jnp.dot(ch, wh, preferred_element_type=F32)
        acc += jnp.dot(ch, wl, preferred_element_type=F32)
        acc += jnp.dot(cl, wh, preferred_element_type=F32)
        o_ref[...] = acc + b_ref[...]

    return pl.pallas_call(
        body, name=name, grid=(L, N // tn),
        in_specs=[pl.BlockSpec((ADA_ROWS, D), lambda l, j: (0, 0)),
                  pl.BlockSpec((None, D, tn), lambda l, j: (l, 0, j)),
                  pl.BlockSpec((None, 1, tn), lambda l, j: (l, 0, j))],
        out_specs=pl.BlockSpec((None, ADA_ROWS, tn), lambda l, j: (l, 0, j)),
        out_shape=jax.ShapeDtypeStruct((L, ADA_ROWS, N), F32),
        compiler_params=_params(("parallel", "parallel")),
    )(c_pad, w, b)


def _ada_bwd(cT, dm, name):
    D = cT.shape[0]
    L, B, N = dm.shape
    tm = _tile(D, 256)

    def body(c_ref, dm_ref, o_ref):
        c = c_ref[...]
        c = c * _sigmoid(c)
        dmv = dm_ref[...]
        acc = c[:, 0:1] * dmv[0:1, :]
        for b in range(1, B):
            acc += c[:, b:b + 1] * dmv[b:b + 1, :]
        o_ref[...] = acc

    return pl.pallas_call(
        body, name=name, grid=(L, D // tm),
        in_specs=[pl.BlockSpec((tm, B), lambda l, i: (i, 0)), pl.BlockSpec((None, B, N), lambda l, i: (l, 0, 0))],
        out_specs=pl.BlockSpec((None, tm, N), lambda l, i: (l, i, 0)),
        out_shape=jax.ShapeDtypeStruct((L, D, N), F32),
        compiler_params=_params(("parallel", "parallel")),
    )(cT, dm)


def _adamw_math(w, g, m, v):
    m = ADAM_B1 * m + (1.0 - ADAM_B1) * g
    v = ADAM_B2 * v + (1.0 - ADAM_B2) * (g * g)
    m_hat = m / (1.0 - ADAM_B1 ** ADAM_STEP)
    v_hat = v / (1.0 - ADAM_B2 ** ADAM_STEP)
    delta = -ADAM_LR * (m_hat / (jnp.sqrt(v_hat) + ADAM_EPS) + ADAM_WD * w)
    return delta, m, v


def _adam_rows(R, C):
    lanes = -(-C // LANE) * LANE
    return _tile(R, max(8, (262144 // lanes) // 8 * 8), 8)


def _adam_sum(recv, w, m, v, layer, filled, name, by_cols=False):
    P, R, C = recv.shape
    L = w.shape[0]
    n_keep = 0 if filled is None else 4
    if by_cols:
        tc = _tile(C, 256)
        grid, spec = (C // tc,), pl.BlockSpec((None, R, tc), lambda i: (layer, 0, i))
        recv_spec = pl.BlockSpec((P, R, tc), lambda i: (0, 0, i))
    else:
        tr = _adam_rows(R, C)
        grid, spec = (R // tr,), pl.BlockSpec((None, tr, C), lambda i: (layer, i, 0))
        recv_spec = pl.BlockSpec((P, tr, C), lambda i: (0, i, 0))

    def body(r_ref, w_ref, m_ref, v_ref, *rest):
        g_ref, d_ref, mo_ref, vo_ref = rest[n_keep:]
        g = r_ref[0].astype(F32)
        for p in range(1, P):
            g = g + r_ref[p].astype(F32)
        delta, mn, vn = _adamw_math(w_ref[...], g, m_ref[...], v_ref[...])
        g_ref[...] = g
        d_ref[...] = delta
        mo_ref[...] = mn
        vo_ref[...] = vn

    return pl.pallas_call(
        body, name=name, grid=grid,
        in_specs=[recv_spec, spec, spec, spec] + [pl.BlockSpec(memory_space=pl.ANY)] * n_keep,
        out_specs=[spec] * 4,
        out_shape=[jax.ShapeDtypeStruct((L, R, C), F32)] * 4,
        input_output_aliases={4 + k: k for k in range(n_keep)},
        compiler_params=_params(("parallel",)),
    )(recv, w, m, v, *(filled or ()))


def _adam(g, w, m, v, name):
    L, R, C = w.shape
    tr = _adam_rows(R, C)

    def body(g_ref, w_ref, m_ref, v_ref, d_ref, mo_ref, vo_ref):
        delta, mn, vn = _adamw_math(w_ref[...], g_ref[...], m_ref[...], v_ref[...])
        d_ref[...] = delta
        mo_ref[...] = mn
        vo_ref[...] = vn

    spec = pl.BlockSpec((None, tr, C), lambda l, i: (l, i, 0))
    return pl.pallas_call(
        body, name=name, grid=(L, R // tr),
        in_specs=[spec] * 4, out_specs=[spec] * 3,
        out_shape=[jax.ShapeDtypeStruct((L, R, C), F32)] * 3,
        compiler_params=_params(("parallel", "parallel")),
    )(g, w, m, v)


def _sum_slots(x, after, name):
    P, R, C = x.shape

    def body(x_ref, after_ref, o_ref):
        acc = x_ref[0]
        for p in range(1, P):
            acc = acc + x_ref[p]
        o_ref[...] = acc

    return pl.pallas_call(
        body, name=name,
        in_specs=[pl.BlockSpec(memory_space=pltpu.VMEM), pl.BlockSpec(memory_space=pl.ANY)],
        out_specs=pl.BlockSpec(memory_space=pltpu.VMEM),
        out_shape=jax.ShapeDtypeStruct((R, C), F32),
        compiler_params=pltpu.CompilerParams(vmem_limit_bytes=48 * MIB),
    )(x, after)


def _my_pos():
    return lax.axis_index("x"), lax.axis_index("y"), lax.axis_index("c")


def _all_gather_small(x, name, after=()):
    R, C = x.shape
    n_after = len(after)

    def body(x_ref, *rest):
        out_ref, send_sems, recv_sems = rest[n_after:]
        x_, y_, c_ = _my_pos()
        me, sibling = (x_, y_, c_), (x_, y_, 1 - c_)
        chips = [(1 - x_, y_), (x_, 1 - y_), (1 - x_, 1 - y_)]

        def slot(px, py, pc):
            return out_ref.at[4 * px + 2 * py + pc]

        def copy(k, block, to):
            return pltpu.make_async_remote_copy(
                src_ref=slot(*block), dst_ref=slot(*block), send_sem=send_sems.at[k], recv_sem=recv_sems.at[k],
                device_id=to, device_id_type=MESH)

        out_ref[4 * x_ + 2 * y_ + c_] = x_ref[...]
        first = [copy(0, me, sibling)] + [copy(1 + j, me, (*chip, c_)) for j, chip in enumerate(chips)]
        for cp in first:
            cp.start()
        passed = [copy(4 + j, (*chip, c_), sibling) for j, chip in enumerate(chips)]
        for j, chip in enumerate(chips):
            copy(1 + j, (*chip, c_), me).wait_recv()
            passed[j].start()
        copy(0, sibling, me).wait_recv()
        for j, chip in enumerate(chips):
            copy(4 + j, (*chip, 1 - c_), me).wait_recv()
        for cp in first + passed:
            cp.wait_send()

    return pl.pallas_call(
        body, name=name,
        in_specs=[pl.BlockSpec(memory_space=pltpu.VMEM)] + [pl.BlockSpec(memory_space=pl.ANY)] * n_after,
        out_specs=pl.BlockSpec(memory_space=pltpu.VMEM),
        out_shape=jax.ShapeDtypeStruct((N_DEV, R, C), x.dtype),
        scratch_shapes=[pltpu.SemaphoreType.DMA((7,)), pltpu.SemaphoreType.DMA((7,))],
        compiler_params=pltpu.CompilerParams(vmem_limit_bytes=48 * MIB),
    )(x, *after)


HBM_SPEC = pl.BlockSpec(memory_space=pltpu.HBM)
SEM_SPEC = pl.BlockSpec(memory_space=pltpu.SEMAPHORE)
ANY_SPEC = pl.BlockSpec(memory_space=pl.ANY)
SPLIT_EFFECT = pltpu.SideEffectType.DATAFLOW_SIDE_EFFECTING


def _in_hbm(a):
    return pltpu.with_memory_space_constraint(a, pltpu.HBM)


def _gathered_shape(a, kind):
    if kind == "major":
        return (N_DEV,) + a.shape
    if kind == "rows":
        return (N_DEV * a.shape[0], a.shape[1])
    return (2, a.shape[0], 4 * a.shape[1])


def _gather_slot(ref, kind, block, shard_shape):
    px, py, pc = block
    if kind == "major":
        return ref.at[4 * px + 2 * py + pc]
    if kind == "rows":
        r = shard_shape[0]
        return ref.at[pl.ds(pl.multiple_of((4 * px + 2 * py + pc) * r, r), r), :]
    cu = shard_shape[1]
    return ref.at[px, :, pl.ds(pl.multiple_of((2 * py + pc) * cu, cu), cu)]


def _gather_peers():
    x_, y_, c_ = _my_pos()
    return (x_, y_, c_), (x_, y_, 1 - c_), [(1 - x_, y_), (x_, 1 - y_), (1 - x_, 1 - y_)]


def _gather_start(shards, kinds, after, name):
    n = len(shards)
    bufs = [lax.empty(_gathered_shape(a, k), a.dtype) for a, k in zip(shards, kinds)]
    extra = [] if after is None else [after]

    def body(*refs):
        shard_refs, buf_refs = refs[:n], refs[n:2 * n]
        send_sems, recv_sems, local_sems = refs[2 * n + len(extra):2 * n + len(extra) + 3]
        token = refs[-1]
        me, sibling, chips = _gather_peers()
        for e in range(n):
            mine = _gather_slot(buf_refs[e], kinds[e], me, shards[e].shape)
            pltpu.make_async_copy(shard_refs[e], mine, local_sems.at[e]).start()
            for k, to in enumerate([sibling] + [(*chip, me[2]) for chip in chips]):
                pltpu.make_async_remote_copy(
                    src_ref=shard_refs[e], dst_ref=mine, send_sem=send_sems.at[4 * e + k],
                    recv_sem=recv_sems.at[4 * e + k], device_id=to, device_id_type=MESH).start()
        token[...] = jnp.zeros_like(token)

    out = pl.pallas_call(
        body, name=name,
        out_shape=(pltpu.SemaphoreType.DMA((4 * n,)), pltpu.SemaphoreType.DMA((4 * n,)), pltpu.SemaphoreType.DMA((n,)),
                   *[pltpu.HBM(a.shape, a.dtype) for a in shards], *[pltpu.HBM(a.shape, a.dtype) for a in bufs],
                   jax.ShapeDtypeStruct((8, LANE), F32)),
        in_specs=[HBM_SPEC] * (2 * n) + [ANY_SPEC] * len(extra),
        out_specs=(SEM_SPEC, SEM_SPEC, SEM_SPEC, *[HBM_SPEC] * (2 * n), pl.BlockSpec(memory_space=pltpu.VMEM)),
        input_output_aliases={i: 3 + i for i in range(2 * n)},
        compiler_params=pltpu.CompilerParams(has_side_effects=SPLIT_EFFECT),
    )(*[_in_hbm(a) for a in shards], *[_in_hbm(a) for a in bufs], *extra)
    return out[0], out[1], out[2], out[3:3 + n], out[3 + n:3 + 2 * n], out[-1]


def _gather_forward(recv_sems, bufs, kinds, shard_shapes, after, name):
    n = len(bufs)

    def body(*refs):
        buf_refs, recv_in = refs[:n], refs[n]
        fsend, frecv = refs[n + 2], refs[n + 3]
        token = refs[-1]
        me, sibling, chips = _gather_peers()
        for e in range(n):
            for j, chip in enumerate(chips):
                slot = _gather_slot(buf_refs[e], kinds[e], (*chip, me[2]), shard_shapes[e])
                pltpu.make_async_remote_copy(
                    src_ref=slot, dst_ref=slot, send_sem=recv_in.at[4 * e + 1 + j], recv_sem=recv_in.at[4 * e + 1 + j],
                    device_id=me, device_id_type=MESH).wait_recv()
                pltpu.make_async_remote_copy(
                    src_ref=slot, dst_ref=slot, send_sem=fsend.at[3 * e + j], recv_sem=frecv.at[3 * e + j],
                    device_id=sibling, device_id_type=MESH).start()
        token[...] = jnp.zeros_like(token)

    out = pl.pallas_call(
        body, name=name,
        out_shape=(pltpu.SemaphoreType.DMA((3 * n,)), pltpu.SemaphoreType.DMA((3 * n,)),
                   *[pltpu.HBM(a.shape, a.dtype) for a in bufs], jax.ShapeDtypeStruct((8, LANE), F32)),
        in_specs=[HBM_SPEC] * n + [SEM_SPEC, ANY_SPEC],
        out_specs=(SEM_SPEC, SEM_SPEC, *[HBM_SPEC] * n, pl.BlockSpec(memory_space=pltpu.VMEM)),
        input_output_aliases={i: 2 + i for i in range(n)},
        compiler_params=pltpu.CompilerParams(has_side_effects=SPLIT_EFFECT),
    )(*bufs, recv_sems, after)
    return out[0], out[1], out[2:2 + n], out[-1]


def _gather_wait(send_sems, recv_sems, local_sems, fsend, frecv, shards, bufs, kinds, after, name):
    n = len(bufs)

    def body(*refs):
        shard_refs, buf_refs = refs[:n], refs[n:2 * n]
        send_in, recv_in, local_in, fsend_in, frecv_in = refs[2 * n:2 * n + 5]
        me, sibling, chips = _gather_peers()

        def arrival(slot, sem):
            return pltpu.make_async_remote_copy(src_ref=slot, dst_ref=slot, send_sem=sem, recv_sem=sem,
                                                device_id=me, device_id_type=MESH)

        for e in range(n):
            shape = shards[e].shape
            mine = _gather_slot(buf_refs[e], kinds[e], me, shape)
            pltpu.make_async_copy(shard_refs[e], mine, local_in.at[e]).wait()
            arrival(_gather_slot(buf_refs[e], kinds[e], sibling, shape), recv_in.at[4 * e]).wait_recv()
            for j, chip in enumerate(chips):
                arrival(_gather_slot(buf_refs[e], kinds[e], (*chip, 1 - me[2]), shape), frecv_in.at[3 * e + j]).wait_recv()
            for k in range(4):
                arrival(mine, send_in.at[4 * e + k]).wait_send()
            for j in range(3):
                arrival(mine, fsend_in.at[3 * e + j]).wait_send()

    out = pl.pallas_call(
        body, name=name,
        out_shape=(*[pltpu.HBM(a.shape, a.dtype) for a in shards], *[pltpu.HBM(a.shape, a.dtype) for a in bufs]),
        in_specs=[HBM_SPEC] * (2 * n) + [SEM_SPEC] * 5 + [ANY_SPEC],
        out_specs=tuple([HBM_SPEC] * (2 * n)),
        input_output_aliases={i: i for i in range(2 * n)},
        compiler_params=pltpu.CompilerParams(has_side_effects=SPLIT_EFFECT),
    )(*shards, *bufs, send_sems, recv_sems, local_sems, fsend, frecv, after)
    return out[n:]


def _grad_slice(ref, kind, j):
    if kind == "major":
        return ref.at[j]
    if kind == "rows":
        r = ref.shape[0] // N_DEV
        return ref.at[pl.ds(j * r, r), :]
    cu = ref.shape[2] // 4
    return ref.at[j // 4, :, pl.ds((j % 4) * cu, cu)]


def _slice_shape(a, kind):
    if kind == "major":
        return a.shape[1:]
    if kind == "rows":
        return (a.shape[0] // N_DEV, a.shape[1])
    return (a.shape[1], a.shape[2] // 4)


def _scatter_copies(srcs, lands, kinds, send_sems, recv_sems):
    x_, y_, c_ = _my_pos()
    me = 4 * x_ + 2 * y_ + c_
    n = len(srcs)

    def remote(e, j):
        return pltpu.make_async_remote_copy(
            src_ref=_grad_slice(srcs[e], kinds[e], j), dst_ref=lands[e].at[me],
            send_sem=send_sems.at[e * N_DEV + j], recv_sem=recv_sems.at[e * N_DEV + me],
            device_id=(j // 4, (j // 2) % 2, j % 2), device_id_type=MESH)

    def local(e, j):
        return pltpu.make_async_copy(_grad_slice(srcs[e], kinds[e], j), lands[e].at[j], recv_sems.at[e * N_DEV + j])

    def arrival(e, i):
        return pltpu.make_async_remote_copy(
            src_ref=_grad_slice(srcs[e], kinds[e], i), dst_ref=lands[e].at[i],
            send_sem=send_sems.at[e * N_DEV + i], recv_sem=recv_sems.at[e * N_DEV + i],
            device_id=(i // 4, (i // 2) % 2, i % 2), device_id_type=MESH)

    def start():
        for e in range(n):
            for j in range(N_DEV):
                @pl.when(me == j)
                def _():
                    local(e, j).start()

                @pl.when(me != j)
                def _():
                    remote(e, j).start()

    def wait():
        for e in range(n):
            for i in range(N_DEV):
                @pl.when(me == i)
                def _():
                    local(e, i).wait()

                @pl.when(me != i)
                def _():
                    arrival(e, i).wait_recv()
        for e in range(n):
            for j in range(N_DEV):
                @pl.when(me != j)
                def _():
                    remote(e, j).wait_send()

    return start, wait


def _scatter_start(srcs, kinds, after, name):
    n = len(srcs)
    lands = [lax.empty((N_DEV,) + _slice_shape(a, k), a.dtype) for a, k in zip(srcs, kinds)]
    extra = [] if after is None else [after]

    def body(*refs):
        src_refs, land_refs = refs[:n], refs[n:2 * n]
        send_sems, recv_sems = refs[2 * n + len(extra)], refs[2 * n + len(extra) + 1]
        token = refs[-1]
        start, _ = _scatter_copies(src_refs, land_refs, kinds, send_sems, recv_sems)
        start()
        token[...] = jnp.zeros_like(token)

    out = pl.pallas_call(
        body, name=name,
        out_shape=(pltpu.SemaphoreType.DMA((n * N_DEV,)), pltpu.SemaphoreType.DMA((n * N_DEV,)),
                   *[pltpu.HBM(a.shape, a.dtype) for a in srcs], *[pltpu.HBM(a.shape, a.dtype) for a in lands],
                   jax.ShapeDtypeStruct((8, LANE), F32)),
        in_specs=[HBM_SPEC] * (2 * n) + [ANY_SPEC] * len(extra),
        out_specs=(SEM_SPEC, SEM_SPEC, *[HBM_SPEC] * (2 * n), pl.BlockSpec(memory_space=pltpu.VMEM)),
        input_output_aliases={i: 2 + i for i in range(2 * n)},
        compiler_params=pltpu.CompilerParams(has_side_effects=SPLIT_EFFECT),
    )(*[_in_hbm(a) for a in srcs], *[_in_hbm(a) for a in lands], *extra)
    return out[0], out[1], out[2:2 + n], out[2 + n:2 + 2 * n], out[-1]


def _scatter_wait(send_sems, recv_sems, srcs, lands, kinds, after, name):
    n = len(srcs)

    def body(*refs):
        src_refs, land_refs = refs[:n], refs[n:2 * n]
        _, wait = _scatter_copies(src_refs, land_refs, kinds, refs[2 * n], refs[2 * n + 1])
        wait()

    out = pl.pallas_call(
        body, name=name,
        out_shape=(*[pltpu.HBM(a.shape, a.dtype) for a in srcs], *[pltpu.HBM(a.shape, a.dtype) for a in lands]),
        in_specs=[HBM_SPEC] * (2 * n) + [SEM_SPEC, SEM_SPEC] + [ANY_SPEC] * len(after),
        out_specs=tuple([HBM_SPEC] * (2 * n)),
        input_output_aliases={i: i for i in range(2 * n)},
        compiler_params=pltpu.CompilerParams(has_side_effects=SPLIT_EFFECT),
    )(*srcs, *lands, send_sems, recv_sems, *after)
    return out[n:]


def _rope_tables(positions, sign):
    half = ROPE_DIM // 2
    inv_freq = ROPE_THETA ** (-jnp.arange(0, ROPE_DIM, 2, dtype=F32) / ROPE_DIM)
    ang = positions.astype(F32)[:, None] * inv_freq
    reps = LANE // half
    cos = jnp.tile(jnp.cos(ang), (1, reps))
    sin = jnp.tile(jnp.sin(ang), (1, reps)) * sign
    d = jnp.arange(LANE) % SWA_HEAD_DIM
    return (jnp.where(d < ROPE_DIM, cos, 1.0), jnp.where(d < half, -sin, 0.0),
            jnp.where((d >= half) & (d < ROPE_DIM), sin, 0.0))


def _pad_cols(a, n):
    return jnp.pad(a, ((0, 0), (0, n - a.shape[1])))


def _local_step(x, target, positions, mods, fetch, P, on_grads, on_small):
    S, D = x.shape
    Hf = D // FOX_HEAD_DIM
    Hq = D // SWA_HEAD_DIM
    Hk = Hq // SWA_GROUP
    Wk = Hk * SWA_HEAD_DIM
    n_in = 3 * D + Hf
    (sh1a, sc1a, g1a, sh2a, sc2a, g2a), (sh1b, sc1b, g1b, sh2b, sc2b, g2b) = mods
    row = lambda v: v.reshape(1, -1)
    cw = [jnp.transpose(P["conv_w"][l].reshape(3, 2, -1), (1, 0, 2)) for l in range(2)]
    cb = [P["conv_b"][l].reshape(2, 1, -1) for l in range(2)]

    W = dict(up=[None, None], down=[None, None])
    h1a = _modulate(x, sc1a, sh1a, "modulate_in")
    W["fox_in"], W["fox_o"] = fetch("fox", "wait", h1a)
    proj_a = _mm_nn(h1a, W["fox_in"], F32, "fox_in_proj", tn=896)
    flT = proj_a[:, 3 * D:n_in].T
    bf_col = P["fox_b_f"].reshape(Hf, 1)
    cumT = _fox_prep(flT, bf_col, "fox_cumsum")
    cum_row = cumT.reshape(Hf, 1, S)
    o_a, lse_a = _fox_attn_fwd(proj_a, cum_row, Hf, "fox_attn_fwd")
    token = fetch("ffn0", "forward", o_a)
    y1a = _mm_nn(o_a, W["fox_o"], F32, "fox_out_proj")
    z1a, x1, h2a = _ln_fwd(x, y1a, g1a, row(P["ln_mix_g"][0]), row(P["ln_mix_b"][0]), sc2a, sh2a, token, "ln_mix0")
    W["up"][0], W["down"][0] = fetch("ffn0", "wait", h2a)
    u_a, uc_a, a_a = _ffn_up(h2a, W["up"][0], cw[0], cb[0], "ffn_up0")
    token = fetch("swa", "forward", a_a)
    y2a = _mm_nn(a_a, W["down"][0], F32, "ffn_down0", tk=1408)
    z2a, x2, h1b = _ln_fwd(x1, y2a, g2a, row(P["ln_ffn_g"][0]), row(P["ln_ffn_b"][0]), sc1b, sh1b, token, "ln_ffn0")

    W["swa_in"], W["swa_o"] = fetch("swa", "wait", h1b)
    proj_b = _mm_nn(h1b, W["swa_in"], F32, "swa_in_proj")
    tabs_f = _rope_tables(positions, 1.0)
    tabs_b = _rope_tables(positions, -1.0)
    qk = _rope([(proj_b, True, D + Wk)], tabs_f, F32, "rope_fwd")
    sinks = _pad_cols(P["swa_sinks"].reshape(1, Hq), LANE)
    o_b, lse_b = _swa_attn_fwd(qk, proj_b, sinks, Hq, "swa_attn_fwd")
    token = fetch("ffn1", "forward", o_b)
    y1b = _mm_nn(o_b, W["swa_o"], F32, "swa_out_proj")
    z1b, x3, h2b = _ln_fwd(x2, y1b, g1b, row(P["ln_mix_g"][1]), row(P["ln_mix_b"][1]), sc2b, sh2b, token, "ln_mix1")
    W["up"][1], W["down"][1] = fetch("ffn1", "wait", h2b)
    u_b, uc_b, a_b = _ffn_up(h2b, W["up"][1], cw[1], cb[1], "ffn_up1")
    y2b = _mm_nn(a_b, W["down"][1], F32, "ffn_down1", tk=1408)
    z2b, dout, loss_row = _ln_fwd_loss(x3, y2b, g2b, row(P["ln_ffn_g"][1]), row(P["ln_ffn_b"][1]), target, "ln_ffn1_loss")

    def ffn_backward(dy, a, u, uc, h_in, l, tag):
        d_down = _mm_tn(a, dy[None], BF16, "ffn_dwdown" + tag, tm=1408, tn=1024)[0]
        token = on_grads("ffn_w_down" + tag, d_down)
        da = _mm_nt(dy[None], W["down"][l][None], F32, "ffn_da" + tag)
        du, dcw, dcb = _ffn_bwd_elem(da, u, uc, cw[l], token, "ffn_bwd_elem" + tag)
        d_up = _mm_tn(h_in, du, BF16, "ffn_dwup" + tag)
        token = on_grads("ffn_w_up" + tag, d_up)
        dh = _mm_nt(du, W["up"][l], F32, "ffn_dh" + tag, tn=1024, tk=1408)
        return dh, token, jnp.transpose(dcw, (1, 0, 2)).reshape(3, -1), dcb.reshape(-1)

    dz2b, dy2b, dg_f1, db_f1, dgate2b = _ln_bwd(dout, z2b, y2b, g2b, row(P["ln_ffn_g"][1]), "ln_ffn1_bwd")
    dh2b, token, dcw1, dcb1 = ffn_backward(dy2b, a_b, u_b, uc_b, h2b, 1, "1")
    dx3, dsc2b, dsh2b = _mod_bwd(dz2b, dh2b, x3, sc2b, token, "mod_ffn1_bwd")

    dz1b, dy1b, dg_m1, db_m1, dgate1b = _ln_bwd(dx3, z1b, y1b, g1b, row(P["ln_mix_g"][1]), "ln_mix1_bwd")
    token = on_grads("swa_w_o", _mm_tn(o_b, dy1b[None], BF16, "swa_dwo")[0])
    do_b = _mm_nt(dy1b[None], W["swa_o"][None], F32, "swa_do")
    dq_b, dk_b, dv_b, dsinks = _swa_attn_bwd(qk, proj_b, sinks, do_b, lse_b, Hq, token, "swa_attn_bwd")
    dproj_b = _rope([(dq_b, True, D), (dk_b, True, Wk), (dv_b, False, Wk)], tabs_b, BF16, "rope_bwd")
    token = on_grads("swa_w_in", _mm_tn(h1b, dproj_b[None], BF16, "swa_dwin")[0])
    dh1b = _mm_nt(dproj_b[None], W["swa_in"][None], F32, "swa_dh", tk=1280)
    dx2, dsc1b, dsh1b = _mod_bwd(dz1b, dh1b, x2, sc1b, token, "mod_mix1_bwd")

    dz2a, dy2a, dg_f0, db_f0, dgate2a = _ln_bwd(dx2, z2a, y2a, g2a, row(P["ln_ffn_g"][0]), "ln_ffn0_bwd")
    dh2a, token, dcw0, dcb0 = ffn_backward(dy2a, a_a, u_a, uc_a, h2a, 0, "0")
    dx1, dsc2a, dsh2a = _mod_bwd(dz2a, dh2a, x1, sc2a, token, "mod_ffn0_bwd")

    dz1a, dy1a, dg_m0, db_m0, dgate1a = _ln_bwd(dx1, z1a, y1a, g1a, row(P["ln_mix_g"][0]), "ln_mix0_bwd")
    token = on_grads("fox_w_o", _mm_tn(o_a, dy1a[None], BF16, "fox_dwo")[0])
    do_a = _mm_nt(dy1a[None], W["fox_o"][None], F32, "fox_do")
    dq_a, dk_a, dv_a, dcum_row, dcum_col = _fox_attn_bwd(proj_a, do_a, o_a, cum_row, lse_a, Hf, token, "fox_attn_bwd")
    dflT, dbf = _fox_prep_bwd(dcum_row.reshape(Hf, S), dcum_col.reshape(Hf, S), flT, bf_col, "fox_cumsum_bwd")
    n_pad = W["fox_in"].shape[1]
    dproj_a = jnp.concatenate([dq_a, dk_a.astype(BF16), dv_a.astype(BF16),
                               _pad_cols(dflT.T, n_pad - 3 * D).astype(BF16)], axis=1)
    dh1a = _mm_nt(dproj_a[None], W["fox_in"][None], F32, "fox_dh", tk=896)
    grad_x, dsc1a, dsh1a = _mod_bwd(dz1a, dh1a, x, sc1a, token, "mod_mix0_bwd")

    dmod = jnp.stack([jnp.concatenate([dsh1a, dsc1a, dgate1a, dsh2a, dsc2a, dgate2a], axis=1)[0],
                      jnp.concatenate([dsh1b, dsc1b, dgate1b, dsh2b, dsc2b, dgate2b], axis=1)[0]])
    small = dict(dmod=dmod, conv_b=jnp.stack([dcb0, dcb1]), conv_w=jnp.stack([dcw0, dcw1]),
                 ln_mix_g=jnp.concatenate([dg_m0, dg_m1]), ln_mix_b=jnp.concatenate([db_m0, db_m1]),
                 ln_ffn_g=jnp.concatenate([dg_f0, dg_f1]), ln_ffn_b=jnp.concatenate([db_f0, db_f1]),
                 fox_b_f=dbf.reshape(-1), swa_sinks=dsinks[0, :Hq], loss=loss_row[0, 0].reshape(1))
    exchanged = on_small(small)
    on_grads("fox_w_in", _mm_tn(h1a, dproj_a[None], BF16, "fox_dwin", tn=896)[0], exchanged)
    return grad_x


SMALL_ORDER = ("dmod", "conv_b", "conv_w", "ln_mix_g", "ln_mix_b", "ln_ffn_g", "ln_ffn_b", "fox_b_f", "swa_sinks", "loss")


def _pack_rows(arrays):
    chunks, spans, off = [], [], 0
    for a in arrays:
        flat = a.reshape(-1)
        n = -(-flat.shape[0] // LANE) * LANE
        chunks.append(jnp.pad(flat, (0, n - flat.shape[0])))
        spans.append((off, flat.shape[0], a.shape))
        off += n
    total = -(-off // (8 * LANE)) * (8 * LANE)
    chunks.append(jnp.zeros((total - off,), F32))
    return jnp.concatenate(chunks).reshape(-1, LANE), spans


def _unpack_rows(packed, spans):
    flat = packed.reshape(-1)
    return [flat[off:off + n].reshape(shape) for off, n, shape in spans]


def kernel(x, c, positions, fox_w_in, fox_b_f, fox_w_o, swa_w_in, swa_sinks, swa_w_o, ada_w, ada_b, ffn_w_up, ffn_conv_w, ffn_conv_b, ffn_w_down, ln_mix_g, ln_mix_b, ln_ffn_g, ln_ffn_b, loss_target, m_fox_w_in, m_fox_b_f, m_fox_w_o, m_swa_w_in, m_swa_sinks, m_swa_w_o, m_ada_w, m_ada_b, m_ffn_w_up, m_ffn_conv_w, m_ffn_conv_b, m_ffn_w_down, m_ln_mix_g, m_ln_mix_b, m_ln_ffn_g, m_ln_ffn_b, v_fox_w_in, v_fox_b_f, v_fox_w_o, v_swa_w_in, v_swa_sinks, v_swa_w_o, v_ada_w, v_ada_b, v_ffn_w_up, v_ffn_conv_w, v_ffn_conv_b, v_ffn_w_down, v_ln_mix_g, v_ln_mix_b, v_ln_ffn_g, v_ln_ffn_b):
    S, D = x.shape[1], x.shape[2]
    L = ada_w.shape[0]
    me = 4 * lax.axis_index("x") + 2 * lax.axis_index("y") + lax.axis_index("c")
    n_ada = ada_w.shape[2]
    cu = ffn_w_up.shape[2]
    F = 4 * cu
    n_in = fox_w_in.shape[2] * N_DEV
    n_in_pad = -(-n_in // LANE) * LANE

    gather_groups = dict(
        fox=([fox_w_in[0].astype(BF16), fox_w_o[0].astype(BF16)], ["major", "rows"]),
        ffn0=([ffn_w_up[0].astype(BF16), ffn_w_down[0].astype(BF16)], ["halves", "rows"]),
        swa=([swa_w_in[0].astype(BF16), swa_w_o[0].astype(BF16)], ["major", "rows"]),
        ffn1=([ffn_w_up[1].astype(BF16), ffn_w_down[1].astype(BF16)], ["halves", "rows"]))
    starts_after = dict(fox=["ffn0", "swa"], ffn0=["ffn1"])
    gathers = {}

    def start_group(group, after):
        shards, kinds = gather_groups[group]
        send, recv, local, thru, bufs, token = _gather_start(shards, kinds, after, "gather_start_" + group)
        gathers[group] = dict(send=send, recv=recv, local=local, shards=thru, bufs=bufs, kinds=kinds,
                              shapes=[a.shape for a in shards], token=token)
        return token

    fox_started = start_group("fox", None)

    c_all = _all_gather_small(c.reshape(-1, LANE), "gather_c", after=[fox_started]).reshape(N_DEV, D)
    b_cols = lax.dynamic_slice_in_dim(ada_b, me * n_ada, n_ada, axis=1).reshape(L, 1, n_ada)
    mod_blk = _ada_mod(jnp.pad(c_all, ((0, ADA_ROWS - N_DEV), (0, 0))), ada_w, b_cols, "ada_mod")[:, :N_DEV]
    mod_all = _all_gather_small(mod_blk.reshape(-1, LANE), "gather_mod").reshape(N_DEV, L, N_DEV, n_ada)
    mod_mine = lax.dynamic_index_in_dim(mod_all, me, axis=2, keepdims=False)
    mod_mine = jnp.transpose(mod_mine, (1, 0, 2)).reshape(L, N_DEV * n_ada)
    mods = [[mod_mine[l, k * D:(k + 1) * D].reshape(1, D) for k in range(6)] for l in range(L)]

    P = dict(fox_b_f=fox_b_f[0], swa_sinks=swa_sinks[0], conv_b=ffn_conv_b,
             ln_mix_g=ln_mix_g, ln_mix_b=ln_mix_b, ln_ffn_g=ln_ffn_g, ln_ffn_b=ln_ffn_b)
    cw_rows = _all_gather_small(_pack_rows([ffn_conv_w])[0], "gather_conv_w", after=[mod_all])
    n_cw = ffn_conv_w.size
    cw_dev = cw_rows.reshape(N_DEV, -1)[:, :n_cw].reshape(N_DEV, L, 3, cu)
    P["conv_w"] = jnp.transpose(cw_dev, (1, 2, 0, 3)).reshape(L, 3, N_DEV * cu)

    def natural(g, pad_to=None):
        w = jnp.transpose(g, (1, 0, 2)).reshape(D, -1)
        return w if pad_to is None else _pad_cols(w, pad_to)

    def forward_stage(group, after):
        s = gathers[group]
        s["fsend"], s["frecv"], s["bufs"], token = _gather_forward(s["recv"], s["bufs"], s["kinds"], s["shapes"], after,
                                                                   "gather_forward_" + group)
        for nxt in starts_after.get(group, ()):
            token = start_group(nxt, token)
        return token

    def fetch(group, stage, after):
        if stage == "forward":
            return forward_stage(group, after)
        if group == "fox":
            after = forward_stage(group, cw_rows)
        s = gathers.pop(group)
        first, second = _gather_wait(s["send"], s["recv"], s["local"], s["fsend"], s["frecv"], s["shards"], s["bufs"],
                                     s["kinds"], after, "gather_wait_" + group)
        if group == "fox":
            return natural(first, n_in_pad), second
        if group == "swa":
            return natural(first), second
        return first, second

    out, pending = {}, {}

    def columns_major(g):
        return jnp.transpose(g.reshape(D, N_DEV, -1), (1, 2, 0))

    def transposed(a):
        return jnp.transpose(a, (0, 2, 1))

    big = dict(
        ffn_w_down1=("ffn_w_down", "rows", 1, (ffn_w_down, m_ffn_w_down, v_ffn_w_down)),
        ffn_w_up1=("ffn_w_up", "halves", 1, (ffn_w_up, m_ffn_w_up, v_ffn_w_up)),
        swa_w_o=("swa_w_o", "rows", 0, (swa_w_o, m_swa_w_o, v_swa_w_o)),
        swa_w_in=("swa_w_in", "major", 0, tuple(transposed(a) for a in (swa_w_in, m_swa_w_in, v_swa_w_in))),
        ffn_w_down0=("ffn_w_down", "rows", 0, (ffn_w_down, m_ffn_w_down, v_ffn_w_down)),
        ffn_w_up0=("ffn_w_up", "halves", 0, (ffn_w_up, m_ffn_w_up, v_ffn_w_up)),
        fox_w_o=("fox_w_o", "rows", 0, (fox_w_o, m_fox_w_o, v_fox_w_o)),
        fox_w_in=("fox_w_in", "major", 0, (fox_w_in, m_fox_w_in, v_fox_w_in)))
    finish_at = dict(swa_w_o=["ffn_w_down1"], ffn_w_up0=["ffn_w_up1", "swa_w_o", "swa_w_in"], fox_w_o=["ffn_w_down0"],
                     fox_w_in=["ffn_w_up0"])
    tail = {}

    def finish(name, after):
        send, recv, thru, lands = pending.pop(name)
        param, kind, layer, wmv = big[name]
        if name == "fox_w_in":
            one = tail["last_start"][0, 0] + 1.0
            wmv = tuple(transposed(a * one) for a in wmv)
        landed, = _scatter_wait(send, recv, thru, lands, [kind], after, "scatter_wait_" + name)
        res = _adam_sum(landed, *wmv, layer, out.get(param), "adam_" + name, by_cols=kind == "major")
        out[param] = [transposed(r) for r in res] if kind == "major" else res

    def on_grads(name, g, after=None):
        kind = big[name][1]
        if name == "fox_w_in":
            g = g[:, :n_in]
        src = columns_major(g) if kind == "major" else g
        send, recv, thru, lands, token = _scatter_start([src], [kind], after, "scatter_start_" + name)
        pending[name] = (send, recv, thru, lands)
        for done in finish_at.get(name, ()):
            finish(done, [token])
        tail["last_start"] = token
        return token

    def on_small(small):
        packed, tail["spans"] = _pack_rows([small[k] for k in SMALL_ORDER])
        tail["gathered"] = _all_gather_small(packed, "gather_small_grads")
        return tail["gathered"]

    grad_x = _local_step(x[0], loss_target[0], positions[0], mods, fetch, P, on_grads, on_small)

    gathered, spans = tail["gathered"], tail["spans"]
    totals = dict(zip(SMALL_ORDER, _unpack_rows(_sum_slots(gathered, tail["last_start"], "sum_small_grads"), spans)))
    loss = totals["loss"].reshape(())
    n_mod = L * 6 * D
    dmod_all = gathered.reshape(N_DEV, -1)[:, :n_mod].reshape(N_DEV, L, 6 * D)
    dmod_cols = jnp.transpose(lax.dynamic_slice_in_dim(dmod_all, me * n_ada, n_ada, axis=2), (1, 0, 2))
    g_ada_w = _ada_bwd(c_all.T, dmod_cols, "ada_w_grad")
    out["ada_w"] = (g_ada_w,) + tuple(_adam(g_ada_w, ada_w, m_ada_w, v_ada_w, "adam_ada_w"))

    g_small = dict(fox_b_f=totals["fox_b_f"].reshape(fox_b_f.shape), swa_sinks=totals["swa_sinks"].reshape(swa_sinks.shape),
                   ada_b=totals["dmod"].reshape(ada_b.shape), ffn_conv_b=totals["conv_b"].reshape(ffn_conv_b.shape),
                   ffn_conv_w=lax.dynamic_slice_in_dim(totals["conv_w"].reshape(L, 3, 2 * F), me * cu, cu, axis=2),
                   ln_mix_g=totals["ln_mix_g"], ln_mix_b=totals["ln_mix_b"],
                   ln_ffn_g=totals["ln_ffn_g"], ln_ffn_b=totals["ln_ffn_b"])
    small_names = ("fox_b_f", "swa_sinks", "ada_b", "ffn_conv_b", "ffn_conv_w", "ln_mix_g", "ln_mix_b", "ln_ffn_g", "ln_ffn_b")
    w_small = dict(fox_b_f=(fox_b_f, m_fox_b_f, v_fox_b_f), swa_sinks=(swa_sinks, m_swa_sinks, v_swa_sinks),
                   ada_b=(ada_b, m_ada_b, v_ada_b), ffn_conv_b=(ffn_conv_b, m_ffn_conv_b, v_ffn_conv_b),
                   ffn_conv_w=(ffn_conv_w, m_ffn_conv_w, v_ffn_conv_w),
                   ln_mix_g=(ln_mix_g, m_ln_mix_g, v_ln_mix_g), ln_mix_b=(ln_mix_b, m_ln_mix_b, v_ln_mix_b),
                   ln_ffn_g=(ln_ffn_g, m_ln_ffn_g, v_ln_ffn_g), ln_ffn_b=(ln_ffn_b, m_ln_ffn_b, v_ln_ffn_b))
    pk_g, sp = _pack_rows([g_small[k] for k in small_names])
    pk_w = _pack_rows([w_small[k][0] for k in small_names])[0]
    pk_m = _pack_rows([w_small[k][1] for k in small_names])[0]
    pk_v = _pack_rows([w_small[k][2] for k in small_names])[0]
    res = _adam(pk_g[None], pk_w[None], pk_m[None], pk_v[None], "adam_small")
    settled = [res[0], out["ada_w"][1]] + [out[k][1] for k in ("ffn_w_up", "ffn_w_down", "swa_w_in", "swa_w_o")]
    finish("fox_w_o", settled)
    finish("fox_w_in", settled)
    res = [dict(zip(small_names, _unpack_rows(r[0], sp))) for r in res]
    for k in small_names:
        out[k] = (g_small[k], res[0][k], res[1][k], res[2][k])

    order = ("fox_w_in", "fox_b_f", "fox_w_o", "swa_w_in", "swa_sinks", "swa_w_o", "ada_w", "ada_b", "ffn_w_up",
             "ffn_conv_w", "ffn_conv_b", "ffn_w_down", "ln_mix_g", "ln_mix_b", "ln_ffn_g", "ln_ffn_b")
    return (loss, grad_x[None], *[out[k][0] for k in order], *[out[k][1] for k in order],
            *[out[k][2] for k in order], *[out[k][3] for k in order])
```

```python
import functools

import jax
import jax.numpy as jnp
from jax import lax
from jax.experimental import pallas as pl
from jax.experimental.pallas import tpu as pltpu

F32 = jnp.float32
BF16 = jnp.bfloat16
MESH = pl.DeviceIdType.MESH
N_DEV = 8
AXES = ("x", "y", "c")

DEPTH = 2
ALPHA = (2.0 * DEPTH) ** 0.25
LN_EPS = 1e-5
FOX_HEAD_DIM = 128
SWA_HEAD_DIM = 64
SWA_GROUP = 8
SWA_WINDOW = 128
Q_BLOCK = 128
ROPE_DIM = 16
ROPE_THETA = 500000.0

ADAM_LR = 0.001
ADAM_B1 = 0.9
ADAM_B2 = 0.999
ADAM_EPS = 1e-08
ADAM_WD = 0.01
ADAM_STEP = 10

LANE = 128
MIB = 1024 * 1024


def _tile(n, pref, unit=LANE):
    if n <= pref:
        return n
    t = (pref // unit) * unit
    while t >= unit:
        if n % t == 0:
            return t
        t -= unit
    return n


def _params(sem, vmem_mib=48):
    return pltpu.CompilerParams(dimension_semantics=sem, vmem_limit_bytes=vmem_mib * MIB)


def _sigmoid(x):
    return 1.0 / (1.0 + jnp.exp(-x))


def _mm_call(dot, grid_mnk, in_specs, out_spec, out_shape, k_axis, nk, tm, tn, name, operands):
    sem = ("parallel",) * (len(grid_mnk) - 1) + ("arbitrary",)

    if nk == 1:
        def body(a_ref, b_ref, o_ref):
            o_ref[...] = dot(a_ref[...], b_ref[...]).astype(o_ref.dtype)
        scratch = []
    else:
        def body(a_ref, b_ref, o_ref, acc_ref):
            k = pl.program_id(k_axis)

            @pl.when(k == 0)
            def _():
                acc_ref[...] = jnp.zeros_like(acc_ref)

            acc_ref[...] += dot(a_ref[...], b_ref[...])

            @pl.when(k == nk - 1)
            def _():
                o_ref[...] = acc_ref[...].astype(o_ref.dtype)
        scratch = [pltpu.VMEM((tm, tn), F32)]

    return pl.pallas_call(
        body, name=name, grid=grid_mnk, in_specs=in_specs, out_specs=out_spec, out_shape=out_shape,
        scratch_shapes=scratch, compiler_params=_params(sem, 56),
    )(*operands)


def _dot(dims):
    def dot(a, b):
        return lax.dot_general(a.astype(BF16), b.astype(BF16), (dims, ((), ())), preferred_element_type=F32)
    return dot


def _mm_nn(a, b, out_dtype, name, tm=2048, tn=512, tk=2048):
    M, K = a.shape
    N = b.shape[1]
    tm, tn, tk = _tile(M, tm), _tile(N, tn), _tile(K, tk)
    nk = K // tk
    return _mm_call(
        _dot(((1,), (0,))), (M // tm, N // tn, nk),
        [pl.BlockSpec((tm, tk), lambda i, j, k: (i, k)), pl.BlockSpec((tk, tn), lambda i, j, k: (k, j))],
        pl.BlockSpec((tm, tn), lambda i, j, k: (i, j)), jax.ShapeDtypeStruct((M, N), out_dtype),
        2, nk, tm, tn, name, (a, b))


def _mm_nt(a, b, out_dtype, name, tm=2048, tn=512, tk=2048):
    P, M, K = a.shape
    N = b.shape[1]
    tm, tn, tk = _tile(M, tm), _tile(N, tn), _tile(K, tk)
    nk = K // tk
    return _mm_call(
        _dot(((1,), (1,))), (M // tm, N // tn, P * nk),
        [pl.BlockSpec((None, tm, tk), lambda i, j, k: (k // nk, i, k % nk)),
         pl.BlockSpec((None, tn, tk), lambda i, j, k: (k // nk, j, k % nk))],
        pl.BlockSpec((tm, tn), lambda i, j, k: (i, j)), jax.ShapeDtypeStruct((M, N), out_dtype),
        2, P * nk, tm, tn, name, (a, b))


def _mm_tn(a, b, out_dtype, name, tm=2048, tn=512, tk=2048):
    K, M = a.shape
    P, _, N = b.shape
    tm, tn, tk = _tile(M, tm), _tile(N, tn), _tile(K, tk)
    nk = K // tk
    return _mm_call(
        _dot(((0,), (0,))), (P, M // tm, N // tn, nk),
        [pl.BlockSpec((tk, tm), lambda p, i, j, k: (k, i)), pl.BlockSpec((None, tk, tn), lambda p, i, j, k: (p, k, j))],
        pl.BlockSpec((None, tm, tn), lambda p, i, j, k: (p, i, j)), jax.ShapeDtypeStruct((P, M, N), out_dtype),
        3, nk, tm, tn, name, (a, b))


ROW_TILE = 256


def _row_spec(tm, D):
    return pl.BlockSpec((tm, D), lambda i: (i, 0))


def _vec_spec(D):
    return pl.BlockSpec((1, D), lambda i: (0, 0))


def _modulate(x, sc, sh, name):
    S, D = x.shape
    tm = _tile(S, ROW_TILE, 8)

    def body(x_ref, sc_ref, sh_ref, h_ref):
        h_ref[...] = (x_ref[...] * (1.0 + sc_ref[...]) + sh_ref[...]).astype(BF16)

    return pl.pallas_call(
        body, name=name, grid=(S // tm,),
        in_specs=[_row_spec(tm, D), _vec_spec(D), _vec_spec(D)],
        out_specs=_row_spec(tm, D),
        out_shape=jax.ShapeDtypeStruct((S, D), BF16),
        compiler_params=_params(("parallel",)),
    )(x, sc, sh)


def _layer_norm_rows(z, gamma, beta):
    mu = jnp.mean(z, axis=-1, keepdims=True)
    zc = z - mu
    var = jnp.mean(zc * zc, axis=-1, keepdims=True)
    return zc * lax.rsqrt(var + LN_EPS) * gamma + beta


def _ln_fwd(x, y, gate, gamma, beta, sc_n, sh_n, after, name):
    S, D = x.shape
    tm = _tile(S, ROW_TILE, 8)

    def body(x_ref, y_ref, gate_ref, g_ref, b_ref, sc_ref, sh_ref, after_ref, z_ref, xo_ref, hn_ref):
        z = ALPHA * x_ref[...] + (1.0 + gate_ref[...]) * y_ref[...]
        xo = _layer_norm_rows(z, g_ref[...], b_ref[...])
        z_ref[...] = z
        xo_ref[...] = xo
        hn_ref[...] = (xo * (1.0 + sc_ref[...]) + sh_ref[...]).astype(BF16)

    return pl.pallas_call(
        body, name=name, grid=(S // tm,),
        in_specs=[_row_spec(tm, D), _row_spec(tm, D)] + [_vec_spec(D)] * 5 + [pl.BlockSpec(memory_space=pl.ANY)],
        out_specs=[_row_spec(tm, D)] * 3,
        out_shape=[jax.ShapeDtypeStruct((S, D), F32), jax.ShapeDtypeStruct((S, D), F32),
                   jax.ShapeDtypeStruct((S, D), BF16)],
        compiler_params=_params(("parallel",)),
    )(x, y, gate, gamma, beta, sc_n, sh_n, after)


def _ln_fwd_loss(x, y, gate, gamma, beta, target, name):
    S, D = x.shape
    tm = _tile(S, ROW_TILE, 8)

    def body(x_ref, y_ref, gate_ref, g_ref, b_ref, t_ref, z_ref, dout_ref, loss_ref):
        @pl.when(pl.program_id(0) == 0)
        def _():
            loss_ref[...] = jnp.zeros_like(loss_ref)

        z = ALPHA * x_ref[...] + (1.0 + gate_ref[...]) * y_ref[...]
        xo = _layer_norm_rows(z, g_ref[...], b_ref[...])
        err = xo - t_ref[...]
        z_ref[...] = z
        dout_ref[...] = err * (1.0 / D)
        loss_ref[...] += (0.5 / D) * jnp.sum(err * err)

    return pl.pallas_call(
        body, name=name, grid=(S // tm,),
        in_specs=[_row_spec(tm, D), _row_spec(tm, D)] + [_vec_spec(D)] * 3 + [_row_spec(tm, D)],
        out_specs=[_row_spec(tm, D), _row_spec(tm, D), pl.BlockSpec((1, LANE), lambda i: (0, 0))],
        out_shape=[jax.ShapeDtypeStruct((S, D), F32), jax.ShapeDtypeStruct((S, D), F32),
                   jax.ShapeDtypeStruct((1, LANE), F32)],
        compiler_params=_params(("arbitrary",)),
    )(x, y, gate, gamma, beta, target)


def _ln_bwd(dout, z, y, gate, gamma, name):
    S, D = z.shape
    tm = _tile(S, ROW_TILE, 8)

    def body(dout_ref, z_ref, y_ref, gate_ref, g_ref, dz_ref, dy_ref, dg_ref, db_ref, dgate_ref):
        @pl.when(pl.program_id(0) == 0)
        def _():
            dg_ref[...] = jnp.zeros_like(dg_ref)
            db_ref[...] = jnp.zeros_like(db_ref)
            dgate_ref[...] = jnp.zeros_like(dgate_ref)

        z = z_ref[...]
        dout = dout_ref[...]
        mu = jnp.mean(z, axis=-1, keepdims=True)
        zc = z - mu
        var = jnp.mean(zc * zc, axis=-1, keepdims=True)
        rstd = lax.rsqrt(var + LN_EPS)
        xhat = zc * rstd
        dxhat = dout * g_ref[...]
        m1 = jnp.mean(dxhat, axis=-1, keepdims=True)
        m2 = jnp.mean(dxhat * xhat, axis=-1, keepdims=True)
        dz = rstd * (dxhat - m1 - xhat * m2)
        dz_ref[...] = dz
        dy_ref[...] = (dz * (1.0 + gate_ref[...])).astype(BF16)
        dg_ref[...] += jnp.sum(dout * xhat, axis=0, keepdims=True)
        db_ref[...] += jnp.sum(dout, axis=0, keepdims=True)
        dgate_ref[...] += jnp.sum(dz * y_ref[...], axis=0, keepdims=True)

    return pl.pallas_call(
        body, name=name, grid=(S // tm,),
        in_specs=[_row_spec(tm, D)] * 3 + [_vec_spec(D)] * 2,
        out_specs=[_row_spec(tm, D), _row_spec(tm, D)] + [_vec_spec(D)] * 3,
        out_shape=[jax.ShapeDtypeStruct((S, D), F32), jax.ShapeDtypeStruct((S, D), BF16)]
        + [jax.ShapeDtypeStruct((1, D), F32)] * 3,
        compiler_params=_params(("arbitrary",)),
    )(dout, z, y, gate, gamma)


def _mod_bwd(dz, dh, xin, sc, after, name):
    S, D = dz.shape
    tm = _tile(S, ROW_TILE, 8)

    def body(dz_ref, dh_ref, x_ref, sc_ref, after_ref, dx_ref, dsc_ref, dsh_ref):
        @pl.when(pl.program_id(0) == 0)
        def _():
            dsc_ref[...] = jnp.zeros_like(dsc_ref)
            dsh_ref[...] = jnp.zeros_like(dsh_ref)

        dh = dh_ref[...]
        dx_ref[...] = ALPHA * dz_ref[...] + dh * (1.0 + sc_ref[...])
        dsc_ref[...] += jnp.sum(dh * x_ref[...], axis=0, keepdims=True)
        dsh_ref[...] += jnp.sum(dh, axis=0, keepdims=True)

    return pl.pallas_call(
        body, name=name, grid=(S // tm,),
        in_specs=[_row_spec(tm, D)] * 3 + [_vec_spec(D), pl.BlockSpec(memory_space=pl.ANY)],
        out_specs=[_row_spec(tm, D), _vec_spec(D), _vec_spec(D)],
        out_shape=[jax.ShapeDtypeStruct((S, D), F32)] + [jax.ShapeDtypeStruct((1, D), F32)] * 2,
        compiler_params=_params(("arbitrary",)),
    )(dz, dh, xin, sc, after)


def _shift_down(u, k, row):
    return jnp.where(row >= k, pltpu.roll(u, k, axis=0), 0.0)


def _shift_up(u, k, row, S):
    return jnp.where(row < S - k, pltpu.roll(u, S - k, axis=0), 0.0)


def _ffn_up(h, w, cw, cb, name):
    S, D = h.shape
    F = w.shape[2]
    tn = _tile(F, 256)

    def body(h_ref, w_ref, cw_ref, cb_ref, u_ref, uc_ref, a_ref):
        hh = h_ref[...]
        row = lax.broadcasted_iota(jnp.int32, (S, tn), 0)
        conv = []
        for p in range(2):
            u = jnp.dot(hh, w_ref[p], preferred_element_type=F32)
            u_ref[p] = u.astype(BF16)
            cwp = cw_ref[p]
            uc = _shift_down(u, 2, row) * cwp[0:1] + _shift_down(u, 1, row) * cwp[1:2] + u * cwp[2:3] + cb_ref[p]
            uc_ref[p] = uc.astype(BF16)
            conv.append(uc)
        g, v = conv
        a_ref[...] = (g * _sigmoid(g) * v).astype(BF16)

    half = pl.BlockSpec((2, S, tn), lambda j: (0, 0, j))
    return pl.pallas_call(
        body, name=name, grid=(F // tn,),
        in_specs=[pl.BlockSpec((S, D), lambda j: (0, 0)), pl.BlockSpec((2, D, tn), lambda j: (0, 0, j)),
                  pl.BlockSpec((2, 3, tn), lambda j: (0, 0, j)), pl.BlockSpec((2, 1, tn), lambda j: (0, 0, j))],
        out_specs=[half, half, pl.BlockSpec((S, tn), lambda j: (0, j))],
        out_shape=[jax.ShapeDtypeStruct((2, S, F), BF16), jax.ShapeDtypeStruct((2, S, F), BF16),
                   jax.ShapeDtypeStruct((S, F), BF16)],
        compiler_params=_params(("parallel",), 56),
    )(h, w, cw, cb)


def _ffn_bwd_elem(da, u, uc, cw, after, name):
    _, S, F = u.shape
    tn = _tile(F, 256)

    def body(da_ref, u_ref, uc_ref, cw_ref, after_ref, du_ref, dcw_ref, dcb_ref):
        row = lax.broadcasted_iota(jnp.int32, (S, tn), 0)
        da = da_ref[...]
        g, v = uc_ref[0].astype(F32), uc_ref[1].astype(F32)
        sg = _sigmoid(g)
        d_conv = (da * v * (sg * (1.0 + g * (1.0 - sg))), da * (g * sg))
        for p in range(2):
            d = d_conv[p]
            cwp = cw_ref[p]
            u = u_ref[p].astype(F32)
            d1, d2 = _shift_up(d, 1, row, S), _shift_up(d, 2, row, S)
            dcb_ref[p] = jnp.sum(d, axis=0, keepdims=True)
            dcw_ref[p, 0:1, :] = jnp.sum(d2 * u, axis=0, keepdims=True)
            dcw_ref[p, 1:2, :] = jnp.sum(d1 * u, axis=0, keepdims=True)
            dcw_ref[p, 2:3, :] = jnp.sum(d * u, axis=0, keepdims=True)
            du_ref[p] = (d * cwp[2:3] + d1 * cwp[1:2] + d2 * cwp[0:1]).astype(BF16)

    half = pl.BlockSpec((2, S, tn), lambda j: (0, 0, j))
    return pl.pallas_call(
        body, name=name, grid=(F // tn,),
        in_specs=[pl.BlockSpec((S, tn), lambda j: (0, j)), half, half, pl.BlockSpec((2, 3, tn), lambda j: (0, 0, j)),
                  pl.BlockSpec(memory_space=pl.ANY)],
        out_specs=[half, pl.BlockSpec((2, 3, tn), lambda j: (0, 0, j)), pl.BlockSpec((2, 1, tn), lambda j: (0, 0, j))],
        out_shape=[jax.ShapeDtypeStruct((2, S, F), BF16), jax.ShapeDtypeStruct((2, 3, F), F32),
                   jax.ShapeDtypeStruct((2, 1, F), F32)],
        compiler_params=_params(("parallel",), 56),
    )(da, u, uc, cw, after)


def _split3(x):
    hi = x.astype(BF16)
    r1 = x - hi.astype(F32)
    mid = r1.astype(BF16)
    lo = (r1 - mid.astype(F32)).astype(BF16)
    return hi, mid, lo


def _tri_matmul(x, upper, S):
    tc = _tile(S, 512)
    parts = _split3(x)
    outs = []
    for b in range(S // tc):
        r = lax.broadcasted_iota(jnp.int32, (S, tc), 0)
        c = lax.broadcasted_iota(jnp.int32, (S, tc), 1) + b * tc
        tri = jnp.where((r <= c) if upper else (r >= c), 1.0, 0.0).astype(BF16)
        acc = jnp.dot(parts[0], tri, preferred_element_type=F32)
        acc += jnp.dot(parts[1], tri, preferred_element_type=F32)
        acc += jnp.dot(parts[2], tri, preferred_element_type=F32)
        outs.append(acc)
    return outs, tc


def _fox_prep(flT, bf, name):
    H, S = flT.shape

    def body(fl_ref, b_ref, cum_ref):
        zz = fl_ref[...] + b_ref[...]
        lf = jnp.minimum(zz, 0.0) - jnp.log(1.0 + jnp.exp(-jnp.abs(zz)))
        outs, tc = _tri_matmul(lf, True, S)
        for b, o in enumerate(outs):
            cum_ref[:, b * tc:(b + 1) * tc] = o

    return pl.pallas_call(
        body, name=name,
        in_specs=[pl.BlockSpec(memory_space=pltpu.VMEM)] * 2,
        out_specs=pl.BlockSpec(memory_space=pltpu.VMEM),
        out_shape=jax.ShapeDtypeStruct((H, S), F32),
        compiler_params=pltpu.CompilerParams(vmem_limit_bytes=48 * MIB),
    )(flT, bf)


def _fox_prep_bwd(dcum_key, dcum_query, flT, bf, name):
    H, S = flT.shape

    def body(dck_ref, dcq_ref, fl_ref, b_ref, dfl_ref, dbf_ref):
        zz = fl_ref[...] + b_ref[...]
        outs, tc = _tri_matmul(dck_ref[...] + dcq_ref[...], False, S)
        total = jnp.zeros((H, 1), F32)
        for b, o in enumerate(outs):
            dfl = o * _sigmoid(-zz[:, b * tc:(b + 1) * tc])
            dfl_ref[:, b * tc:(b + 1) * tc] = dfl
            total += jnp.sum(dfl, axis=1, keepdims=True)
        dbf_ref[...] = total

    return pl.pallas_call(
        body, name=name,
        in_specs=[pl.BlockSpec(memory_space=pltpu.VMEM)] * 4,
        out_specs=[pl.BlockSpec(memory_space=pltpu.VMEM)] * 2,
        out_shape=[jax.ShapeDtypeStruct((H, S), F32), jax.ShapeDtypeStruct((H, 1), F32)],
        compiler_params=pltpu.CompilerParams(vmem_limit_bytes=48 * MIB),
    )(dcum_key, dcum_query, flT, bf)


FOX_TQ = 256
FOX_TC = 512


def _fox_scores(q, k_ref, ck_ref, i, lo, n, tq):
    k = k_ref[lo:lo + n, :].astype(BF16)
    s = lax.dot_general(q, k, (((1,), (1,)), ((), ())), preferred_element_type=F32) - ck_ref[:, lo:lo + n]
    qpos = i * tq + lax.broadcasted_iota(jnp.int32, (tq, n), 0)
    kpos = lo + lax.broadcasted_iota(jnp.int32, (tq, n), 1)
    return jnp.where(kpos <= qpos, s, -jnp.inf)


FOX_FWD_SPLITS = 4


def _fox_attn_fwd_part(proj, cum_row, H, q_lo, q_hi, filled, name):
    S = proj.shape[0]
    dh = FOX_HEAD_DIM
    tq = _tile(q_hi - q_lo, FOX_TQ)
    first = q_lo // tq
    scale = dh ** -0.5
    n_keep = 0 if filled is None else 2

    def body(q_ref, k_ref, v_ref, ck_ref, *rest):
        o_ref, lse_ref = rest[n_keep:]
        q = (q_ref[...] * scale).astype(BF16)
        s = _fox_scores(q, k_ref, ck_ref, first + pl.program_id(1), 0, q_hi, tq)
        m = jnp.max(s, axis=-1, keepdims=True)
        p = jnp.exp(s - m)
        l = jnp.sum(p, axis=-1, keepdims=True)
        o = jnp.dot(p.astype(BF16), v_ref[...].astype(BF16), preferred_element_type=F32) / l
        o_ref[...] = o.astype(BF16)
        lse_ref[...] = m + jnp.log(l)

    return pl.pallas_call(
        body, name=name, grid=(H, (q_hi - q_lo) // tq),
        in_specs=[pl.BlockSpec((tq, dh), lambda h, i: (first + i, h)),
                  pl.BlockSpec((q_hi, dh), lambda h, i: (0, H + h)),
                  pl.BlockSpec((q_hi, dh), lambda h, i: (0, 2 * H + h)),
                  pl.BlockSpec((None, 1, q_hi), lambda h, i: (h, 0, 0))] + [pl.BlockSpec(memory_space=pl.ANY)] * n_keep,
        out_specs=[pl.BlockSpec((tq, dh), lambda h, i: (first + i, h)),
                   pl.BlockSpec((None, tq, 1), lambda h, i: (h, first + i, 0))],
        out_shape=[jax.ShapeDtypeStruct((S, H * dh), BF16), jax.ShapeDtypeStruct((H, S, 1), F32)],
        input_output_aliases={4 + k: k for k in range(n_keep)},
        compiler_params=_params(("parallel", "parallel")),
    )(proj, proj, proj, cum_row, *(filled or ()))


def _fox_attn_fwd(proj, cum_row, H, name):
    S = proj.shape[0]
    n = FOX_FWD_SPLITS if S % (FOX_FWD_SPLITS * FOX_TQ) == 0 else 1
    out = None
    for part in range(n):
        out = _fox_attn_fwd_part(proj, cum_row, H, part * S // n, (part + 1) * S // n, out, name + str(part))
    return out


def _fox_attn_bwd(proj, do, o, cum_row, lse, H, after, name):
    S = proj.shape[0]
    dh = FOX_HEAD_DIM
    tq = _tile(S, FOX_TQ)
    tc = _tile(S, FOX_TC)
    scale = dh ** -0.5

    def body(q_ref, k_ref, v_ref, do_ref, o_ref, ck_ref, lse_ref, after_ref,
             dq_ref, dk_ref, dv_ref, dck_ref, dcq_ref, acc_ref):
        i = pl.program_id(1)

        @pl.when(i == 0)
        def _():
            dk_ref[...] = jnp.zeros_like(dk_ref)
            dv_ref[...] = jnp.zeros_like(dv_ref)
            dck_ref[...] = jnp.zeros_like(dck_ref)

        acc_ref[...] = jnp.zeros_like(acc_ref)
        dcq_ref[...] = jnp.zeros_like(dcq_ref)
        q = (q_ref[...] * scale).astype(BF16)
        do_f = do_ref[...]
        do_b = do_f.astype(BF16)
        delta = jnp.sum(do_f * o_ref[...].astype(F32), axis=-1, keepdims=True)
        lse_q = lse_ref[...]
        for c in range(S // tc):
            @pl.when(c * tc <= i * tq + tq - 1)
            def _():
                rows = slice(c * tc, (c + 1) * tc)
                p = jnp.exp(_fox_scores(q, k_ref, ck_ref, i, c * tc, tc, tq) - lse_q)
                dp = lax.dot_general(do_b, v_ref[rows, :].astype(BF16), (((1,), (1,)), ((), ())),
                                     preferred_element_type=F32)
                ds = p * (dp - delta)
                ds_b = ds.astype(BF16)
                acc_ref[...] += jnp.dot(ds_b, k_ref[rows, :].astype(BF16), preferred_element_type=F32)
                dk_ref[rows, :] += lax.dot_general(ds_b, q, (((0,), (0,)), ((), ())), preferred_element_type=F32)
                dv_ref[rows, :] += lax.dot_general(p.astype(BF16), do_b, (((0,), (0,)), ((), ())),
                                                   preferred_element_type=F32)
                dck_ref[:, rows] -= jnp.sum(ds, axis=0, keepdims=True)
                dcq_ref[...] += jnp.sum(ds, axis=-1, keepdims=True)
        dq_ref[...] = (acc_ref[...] * scale).astype(BF16)

    W = H * dh
    return pl.pallas_call(
        body, name=name, grid=(H, S // tq),
        in_specs=[pl.BlockSpec((tq, dh), lambda h, i: (i, h)),
                  pl.BlockSpec((S, dh), lambda h, i: (0, H + h)),
                  pl.BlockSpec((S, dh), lambda h, i: (0, 2 * H + h)),
                  pl.BlockSpec((tq, dh), lambda h, i: (i, h)),
                  pl.BlockSpec((tq, dh), lambda h, i: (i, h)),
                  pl.BlockSpec((None, 1, S), lambda h, i: (h, 0, 0)),
                  pl.BlockSpec((None, tq, 1), lambda h, i: (h, i, 0)),
                  pl.BlockSpec(memory_space=pl.ANY)],
        out_specs=[pl.BlockSpec((tq, dh), lambda h, i: (i, h)),
                   pl.BlockSpec((S, dh), lambda h, i: (0, h)),
                   pl.BlockSpec((S, dh), lambda h, i: (0, h)),
                   pl.BlockSpec((None, 1, S), lambda h, i: (h, 0, 0)),
                   pl.BlockSpec((None, tq, 1), lambda h, i: (h, i, 0))],
        out_shape=[jax.ShapeDtypeStruct((S, W), BF16), jax.ShapeDtypeStruct((S, W), F32),
                   jax.ShapeDtypeStruct((S, W), F32), jax.ShapeDtypeStruct((H, 1, S), F32),
                   jax.ShapeDtypeStruct((H, S, 1), F32)],
        scratch_shapes=[pltpu.VMEM((tq, dh), F32)],
        compiler_params=_params(("parallel", "arbitrary")),
    )(proj, proj, proj, do, o, cum_row, lse, after)


def _rope(parts, tabs, out_dtype, name):
    S = parts[0][0].shape[0]
    widths = [w for _, _, w in parts]
    total = sum(widths)
    tm = _tile(S, ROW_TILE, 8)
    flags = [r for _, r, _ in parts]

    def body(*refs):
        in_refs = refs[:len(parts)]
        cos_ref, sa_ref, sb_ref, o_ref = refs[len(parts):]
        cos, sa, sb = cos_ref[...], sa_ref[...], sb_ref[...]
        off = 0
        for ref, rot, w in zip(in_refs, flags, widths):
            for j in range(w // LANE):
                t = ref[:, j * LANE:(j + 1) * LANE]
                if rot:
                    t = t * cos + pltpu.roll(t, LANE - ROPE_DIM // 2, axis=1) * sa + pltpu.roll(t, ROPE_DIM // 2, axis=1) * sb
                o_ref[:, off + j * LANE:off + (j + 1) * LANE] = t.astype(o_ref.dtype)
            off += w

    return pl.pallas_call(
        body, name=name, grid=(S // tm,),
        in_specs=[pl.BlockSpec((tm, w), lambda i: (i, 0)) for w in widths] + [_row_spec(tm, LANE)] * 3,
        out_specs=_row_spec(tm, total),
        out_shape=jax.ShapeDtypeStruct((S, total), out_dtype),
        compiler_params=_params(("parallel",)),
    )(*[a for a, _, _ in parts], *tabs)


def _swa_band(ref_p, ref_c, hk):
    dh = SWA_HEAD_DIM
    return jnp.concatenate([ref_p[:, hk * dh:(hk + 1) * dh], ref_c[:, hk * dh:(hk + 1) * dh]], axis=0).astype(BF16)


def _swa_bias(G):
    qi = jnp.arange(G * Q_BLOCK)[:, None] % Q_BLOCK
    kj = jnp.arange(2 * Q_BLOCK)[None, :]
    rel = qi + Q_BLOCK - kj
    window = (rel >= 0) & (rel < SWA_WINDOW)
    both = jnp.stack([window & (kj >= Q_BLOCK), window])
    return jnp.where(both, 0.0, -jnp.inf).astype(F32)


def _swa_stack(ref, hk, G):
    dh = SWA_HEAD_DIM
    return jnp.concatenate([ref[:, (hk * G + g) * dh:(hk * G + g + 1) * dh] for g in range(G)], axis=0)


def _swa_unstack(ref, stacked, hk, G):
    dh, QB = SWA_HEAD_DIM, Q_BLOCK
    for g in range(0, G, 2):
        c0 = (hk * G + g) * dh
        pair = jnp.concatenate([stacked[g * QB:(g + 1) * QB], stacked[(g + 1) * QB:(g + 2) * QB]], axis=1)
        ref[:, c0:c0 + 2 * dh] = pair.astype(ref.dtype)


def _swa_sink_rows(sink_ref, hk, G):
    return jnp.concatenate([jnp.broadcast_to(sink_ref[0:1, hk * G + g:hk * G + g + 1], (Q_BLOCK, 1)) for g in range(G)],
                           axis=0)


def _swa_attn_fwd(qk, proj, sinks, Hq, name):
    S = qk.shape[0]
    dh, G, QB = SWA_HEAD_DIM, SWA_GROUP, Q_BLOCK
    Hk = Hq // G
    Wq, Wk = Hq * dh, Hk * dh
    nb = S // QB
    scale = dh ** -0.5

    def body(q_ref, kp_ref, kc_ref, vp_ref, vc_ref, sink_ref, bias_ref, o_ref, lse_ref):
        bias = bias_ref[...]
        lane = lax.broadcasted_iota(jnp.int32, (QB, LANE), 1)
        lse_tile = jnp.zeros((QB, LANE), F32)
        for hk in range(Hk):
            kb = _swa_band(kp_ref, kc_ref, hk)
            vb = _swa_band(vp_ref, vc_ref, hk)
            q = (_swa_stack(q_ref, hk, G) * scale).astype(BF16)
            sk = _swa_sink_rows(sink_ref, hk, G)
            s = lax.dot_general(q, kb, (((1,), (1,)), ((), ())), preferred_element_type=F32) + bias
            m = jnp.maximum(jnp.max(s, axis=-1, keepdims=True), sk)
            p = jnp.exp(s - m)
            l = jnp.sum(p, axis=-1, keepdims=True) + jnp.exp(sk - m)
            o = jnp.dot(p.astype(BF16), vb, preferred_element_type=F32) / l
            lse = m + jnp.log(l)
            for g in range(G):
                lse_tile = jnp.where(lane == hk * G + g, lse[g * QB:(g + 1) * QB], lse_tile)
            _swa_unstack(o_ref, o, hk, G)
        lse_ref[...] = lse_tile

    kcol, vcol = Wq // Wk, (Wq + Wk) // Wk
    return pl.pallas_call(
        body, name=name, grid=(nb,),
        in_specs=[pl.BlockSpec((QB, Wq), lambda n: (n, 0)),
                  pl.BlockSpec((QB, Wk), lambda n: (jnp.maximum(n - 1, 0), kcol)),
                  pl.BlockSpec((QB, Wk), lambda n: (n, kcol)),
                  pl.BlockSpec((QB, Wk), lambda n: (jnp.maximum(n - 1, 0), vcol)),
                  pl.BlockSpec((QB, Wk), lambda n: (n, vcol)),
                  pl.BlockSpec((1, LANE), lambda n: (0, 0)),
                  pl.BlockSpec((None, G * QB, 2 * QB), lambda n: (jnp.minimum(n, 1), 0, 0))],
        out_specs=[pl.BlockSpec((QB, Wq), lambda n: (n, 0)), pl.BlockSpec((QB, LANE), lambda n: (n, 0))],
        out_shape=[jax.ShapeDtypeStruct((S, Wq), BF16), jax.ShapeDtypeStruct((S, LANE), F32)],
        compiler_params=_params(("parallel",)),
    )(qk, qk, qk, proj, proj, sinks, _swa_bias(G))


def _swa_attn_bwd(qk, proj, sinks, do, lse, Hq, after, name):
    S = qk.shape[0]
    dh, G, QB = SWA_HEAD_DIM, SWA_GROUP, Q_BLOCK
    Hk = Hq // G
    Wq, Wk = Hq * dh, Hk * dh
    nb = S // QB
    scale = dh ** -0.5

    def body(q_ref, kp_ref, kc_ref, vp_ref, vc_ref, sink_ref, do_ref, lse_ref, bias_ref, after_ref,
             dq_ref, dk_ref, dv_ref, dsink_ref, carry_k, carry_v):
        n = pl.program_id(0)

        @pl.when(n == 0)
        def _():
            dsink_ref[...] = jnp.zeros_like(dsink_ref)

        @pl.when(n < nb)
        def _():
            bias = bias_ref[...]
            lane = lax.broadcasted_iota(jnp.int32, (1, LANE), 1)
            dsink = jnp.zeros((1, LANE), F32)
            dk_heads, dv_heads = [], []
            for hk in range(Hk):
                kb = _swa_band(kp_ref, kc_ref, hk)
                vb = _swa_band(vp_ref, vc_ref, hk)
                q = (_swa_stack(q_ref, hk, G) * scale).astype(BF16)
                do_s = _swa_stack(do_ref, hk, G).astype(BF16)
                lse = jnp.concatenate([lse_ref[:, hk * G + g:hk * G + g + 1] for g in range(G)], axis=0)
                s = lax.dot_general(q, kb, (((1,), (1,)), ((), ())), preferred_element_type=F32) + bias
                p = jnp.exp(s - lse)
                p_sink = jnp.exp(_swa_sink_rows(sink_ref, hk, G) - lse)
                dp = lax.dot_general(do_s, vb, (((1,), (1,)), ((), ())), preferred_element_type=F32)
                delta = jnp.sum(p * dp, axis=-1, keepdims=True)
                ds_b = (p * (dp - delta)).astype(BF16)
                _swa_unstack(dq_ref, jnp.dot(ds_b, kb, preferred_element_type=F32) * scale, hk, G)
                dk_heads.append(lax.dot_general(ds_b, q, (((0,), (0,)), ((), ())), preferred_element_type=F32))
                dv_heads.append(lax.dot_general(p.astype(BF16), do_s, (((0,), (0,)), ((), ())), preferred_element_type=F32))
                sink_term = p_sink * delta
                for g in range(G):
                    dsink = jnp.where(lane == hk * G + g,
                                      -jnp.sum(sink_term[g * QB:(g + 1) * QB], axis=0, keepdims=True), dsink)
            dsink_ref[...] += dsink
            dk_all = jnp.concatenate(dk_heads, axis=1)
            dv_all = jnp.concatenate(dv_heads, axis=1)

            @pl.when(n > 0)
            def _():
                dk_ref[...] = carry_k[...] + dk_all[:QB]
                dv_ref[...] = carry_v[...] + dv_all[:QB]

            carry_k[...] = dk_all[QB:]
            carry_v[...] = dv_all[QB:]

        @pl.when(n == nb)
        def _():
            dk_ref[...] = carry_k[...]
            dv_ref[...] = carry_v[...]

    kcol, vcol = Wq // Wk, (Wq + Wk) // Wk
    cur = lambda n: jnp.minimum(n, nb - 1)
    prev = lambda n: jnp.maximum(jnp.minimum(n, nb - 1) - 1, 0)
    return pl.pallas_call(
        body, name=name, grid=(nb + 1,),
        in_specs=[pl.BlockSpec((QB, Wq), lambda n: (cur(n), 0)),
                  pl.BlockSpec((QB, Wk), lambda n: (prev(n), kcol)),
                  pl.BlockSpec((QB, Wk), lambda n: (cur(n), kcol)),
                  pl.BlockSpec((QB, Wk), lambda n: (prev(n), vcol)),
                  pl.BlockSpec((QB, Wk), lambda n: (cur(n), vcol)),
                  pl.BlockSpec((1, LANE), lambda n: (0, 0)),
                  pl.BlockSpec((QB, Wq), lambda n: (cur(n), 0)),
                  pl.BlockSpec((QB, LANE), lambda n: (cur(n), 0)),
                  pl.BlockSpec((None, G * QB, 2 * QB), lambda n: (jnp.minimum(n, 1), 0, 0)),
                  pl.BlockSpec(memory_space=pl.ANY)],
        out_specs=[pl.BlockSpec((QB, Wq), lambda n: (cur(n), 0)),
                   pl.BlockSpec((QB, Wk), lambda n: (jnp.maximum(n - 1, 0), 0)),
                   pl.BlockSpec((QB, Wk), lambda n: (jnp.maximum(n - 1, 0), 0)),
                   pl.BlockSpec((1, LANE), lambda n: (0, 0))],
        out_shape=[jax.ShapeDtypeStruct((S, Wq), F32), jax.ShapeDtypeStruct((S, Wk), F32),
                   jax.ShapeDtypeStruct((S, Wk), F32), jax.ShapeDtypeStruct((1, LANE), F32)],
        scratch_shapes=[pltpu.VMEM((QB, Wk), F32), pltpu.VMEM((QB, Wk), F32)],
        compiler_params=_params(("arbitrary",)),
    )(qk, qk, qk, proj, proj, sinks, do, lse, _swa_bias(G), after)


ADA_ROWS = 16


def _ada_mod(c_pad, w, b, name):
    L, D, N = w.shape
    tn = _tile(N, 512)

    def body(c_ref, w_ref, b_ref, o_ref):
        c = c_ref[...]
        c = c * _sigmoid(c)
        ch = c.astype(BF16)
        cl = (c - ch.astype(F32)).astype(BF16)
        ww = w_ref[...]
        wh = ww.astype(BF16)
        wl = (ww - wh.astype(F32)).astype(BF16)
        acc = jnp.dot(ch, wh, preferred_element_type=F32)
        acc += jnp.dot(ch, wl, preferred_element_type=F32)
        acc += jnp.dot(cl, wh, preferred_element_type=F32)
        o_ref[...] = acc + b_ref[...]

    return pl.pallas_call(
        body, name=name, grid=(L, N // tn),
        in_specs=[pl.BlockSpec((ADA_ROWS, D), lambda l, j: (0, 0)),
                  pl.BlockSpec((None, D, tn), lambda l, j: (l, 0, j)),
                  pl.BlockSpec((None, 1, tn), lambda l, j: (l, 0, j))],
        out_specs=pl.BlockSpec((None, ADA_ROWS, tn), lambda l, j: (l, 0, j)),
        out_shape=jax.ShapeDtypeStruct((L, ADA_ROWS, N), F32),
        compiler_params=_params(("parallel", "parallel")),
    )(c_pad, w, b)


def _ada_bwd(cT, dm, name):
    D = cT.shape[0]
    L, B, N = dm.shape
    tm = _tile(D, 256)

    def body(c_ref, dm_ref, o_ref):
        c = c_ref[...]
        c = c * _sigmoid(c)
        dmv = dm_ref[...]
        acc = c[:, 0:1] * dmv[0:1, :]
        for b in range(1, B):
            acc += c[:, b:b + 1] * dmv[b:b + 1, :]
        o_ref[...] = acc

    return pl.pallas_call(
        body, name=name, grid=(L, D // tm),
        in_specs=[pl.BlockSpec((tm, B), lambda l, i: (i, 0)), pl.BlockSpec((None, B, N), lambda l, i: (l, 0, 0))],
        out_specs=pl.BlockSpec((None, tm, N), lambda l, i: (l, i, 0)),
        out_shape=jax.ShapeDtypeStruct((L, D, N), F32),
        compiler_params=_params(("parallel", "parallel")),
    )(cT, dm)


def _adamw_math(w, g, m, v):
    m = ADAM_B1 * m + (1.0 - ADAM_B1) * g
    v = ADAM_B2 * v + (1.0 - ADAM_B2) * (g * g)
    m_hat = m / (1.0 - ADAM_B1 ** ADAM_STEP)
    v_hat = v / (1.0 - ADAM_B2 ** ADAM_STEP)
    delta = -ADAM_LR * (m_hat / (jnp.sqrt(v_hat) + ADAM_EPS) + ADAM_WD * w)
    return delta, m, v


def _adam_rows(R, C):
    lanes = -(-C // LANE) * LANE
    return _tile(R, max(8, (262144 // lanes) // 8 * 8), 8)


def _adam_sum(recv, w, m, v, layer, filled, name, by_cols=False):
    P, R, C = recv.shape
    L = w.shape[0]
    n_keep = 0 if filled is None else 4
    if by_cols:
        tc = _tile(C, 256)
        grid, spec = (C // tc,), pl.BlockSpec((None, R, tc), lambda i: (layer, 0, i))
        recv_spec = pl.BlockSpec((P, R, tc), lambda i: (0, 0, i))
    else:
        tr = _adam_rows(R, C)
        grid, spec = (R // tr,), pl.BlockSpec((None, tr, C), lambda i: (layer, i, 0))
        recv_spec = pl.BlockSpec((P, tr, C), lambda i: (0, i, 0))

    def body(r_ref, w_ref, m_ref, v_ref, *rest):
        g_ref, d_ref, mo_ref, vo_ref = rest[n_keep:]
        g = r_ref[0].astype(F32)
        for p in range(1, P):
            g = g + r_ref[p].astype(F32)
        delta, mn, vn = _adamw_math(w_ref[...], g, m_ref[...], v_ref[...])
        g_ref[...] = g
        d_ref[...] = delta
        mo_ref[...] = mn
        vo_ref[...] = vn

    return pl.pallas_call(
        body, name=name, grid=grid,
        in_specs=[recv_spec, spec, spec, spec] + [pl.BlockSpec(memory_space=pl.ANY)] * n_keep,
        out_specs=[spec] * 4,
        out_shape=[jax.ShapeDtypeStruct((L, R, C), F32)] * 4,
        input_output_aliases={4 + k: k for k in range(n_keep)},
        compiler_params=_params(("parallel",)),
    )(recv, w, m, v, *(filled or ()))


def _adam(g, w, m, v, name):
    L, R, C = w.shape
    tr = _adam_rows(R, C)

    def body(g_ref, w_ref, m_ref, v_ref, d_ref, mo_ref, vo_ref):
        delta, mn, vn = _adamw_math(w_ref[...], g_ref[...], m_ref[...], v_ref[...])
        d_ref[...] = delta
        mo_ref[...] = mn
        vo_ref[...] = vn

    spec = pl.BlockSpec((None, tr, C), lambda l, i: (l, i, 0))
    return pl.pallas_call(
        body, name=name, grid=(L, R // tr),
        in_specs=[spec] * 4, out_specs=[spec] * 3,
        out_shape=[jax.ShapeDtypeStruct((L, R, C), F32)] * 3,
        compiler_params=_params(("parallel", "parallel")),
    )(g, w, m, v)


def _sum_slots(x, after, name):
    P, R, C = x.shape

    def body(x_ref, after_ref, o_ref):
        acc = x_ref[0]
        for p in range(1, P):
            acc = acc + x_ref[p]
        o_ref[...] = acc

    return pl.pallas_call(
        body, name=name,
        in_specs=[pl.BlockSpec(memory_space=pltpu.VMEM), pl.BlockSpec(memory_space=pl.ANY)],
        out_specs=pl.BlockSpec(memory_space=pltpu.VMEM),
        out_shape=jax.ShapeDtypeStruct((R, C), F32),
        compiler_params=pltpu.CompilerParams(vmem_limit_bytes=48 * MIB),
    )(x, after)


def _my_pos():
    return lax.axis_index("x"), lax.axis_index("y"), lax.axis_index("c")


def _all_gather_small(x, name, after=()):
    R, C = x.shape
    n_after = len(after)

    def body(x_ref, *rest):
        out_ref, send_sems, recv_sems = rest[n_after:]
        x_, y_, c_ = _my_pos()
        me, sibling = (x_, y_, c_), (x_, y_, 1 - c_)
        chips = [(1 - x_, y_), (x_, 1 - y_), (1 - x_, 1 - y_)]

        def slot(px, py, pc):
            return out_ref.at[4 * px + 2 * py + pc]

        def copy(k, block, to):
            return pltpu.make_async_remote_copy(
                src_ref=slot(*block), dst_ref=slot(*block), send_sem=send_sems.at[k], recv_sem=recv_sems.at[k],
                device_id=to, device_id_type=MESH)

        out_ref[4 * x_ + 2 * y_ + c_] = x_ref[...]
        first = [copy(0, me, sibling)] + [copy(1 + j, me, (*chip, c_)) for j, chip in enumerate(chips)]
        for cp in first:
            cp.start()
        passed = [copy(4 + j, (*chip, c_), sibling) for j, chip in enumerate(chips)]
        for j, chip in enumerate(chips):
            copy(1 + j, (*chip, c_), me).wait_recv()
            passed[j].start()
        copy(0, sibling, me).wait_recv()
        for j, chip in enumerate(chips):
            copy(4 + j, (*chip, 1 - c_), me).wait_recv()
        for cp in first + passed:
            cp.wait_send()

    return pl.pallas_call(
        body, name=name,
        in_specs=[pl.BlockSpec(memory_space=pltpu.VMEM)] + [pl.BlockSpec(memory_space=pl.ANY)] * n_after,
        out_specs=pl.BlockSpec(memory_space=pltpu.VMEM),
        out_shape=jax.ShapeDtypeStruct((N_DEV, R, C), x.dtype),
        scratch_shapes=[pltpu.SemaphoreType.DMA((7,)), pltpu.SemaphoreType.DMA((7,))],
        compiler_params=pltpu.CompilerParams(vmem_limit_bytes=48 * MIB),
    )(x, *after)


HBM_SPEC = pl.BlockSpec(memory_space=pltpu.HBM)
SEM_SPEC = pl.BlockSpec(memory_space=pltpu.SEMAPHORE)
ANY_SPEC = pl.BlockSpec(memory_space=pl.ANY)
SPLIT_EFFECT = pltpu.SideEffectType.DATAFLOW_SIDE_EFFECTING


def _in_hbm(a):
    return pltpu.with_memory_space_constraint(a, pltpu.HBM)


def _gathered_shape(a, kind):
    if kind == "major":
        return (N_DEV,) + a.shape
    if kind == "rows":
        return (N_DEV * a.shape[0], a.shape[1])
    return (2, a.shape[0], 4 * a.shape[1])


def _gather_slot(ref, kind, block, shard_shape):
    px, py, pc = block
    if kind == "major":
        return ref.at[4 * px + 2 * py + pc]
    if kind == "rows":
        r = shard_shape[0]
        return ref.at[pl.ds(pl.multiple_of((4 * px + 2 * py + pc) * r, r), r), :]
    cu = shard_shape[1]
    return ref.at[px, :, pl.ds(pl.multiple_of((2 * py + pc) * cu, cu), cu)]


def _gather_peers():
    x_, y_, c_ = _my_pos()
    return (x_, y_, c_), (x_, y_, 1 - c_), [(1 - x_, y_), (x_, 1 - y_), (1 - x_, 1 - y_)]


def _gather_start(shards, kinds, after, name):
    n = len(shards)
    bufs = [lax.empty(_gathered_shape(a, k), a.dtype) for a, k in zip(shards, kinds)]
    extra = [] if after is None else [after]

    def body(*refs):
        shard_refs, buf_refs = refs[:n], refs[n:2 * n]
        send_sems, recv_sems, local_sems = refs[2 * n + len(extra):2 * n + len(extra) + 3]
        token = refs[-1]
        me, sibling, chips = _gather_peers()
        for e in range(n):
            mine = _gather_slot(buf_refs[e], kinds[e], me, shards[e].shape)
            pltpu.make_async_copy(shard_refs[e], mine, local_sems.at[e]).start()
            for k, to in enumerate([sibling] + [(*chip, me[2]) for chip in chips]):
                pltpu.make_async_remote_copy(
                    src_ref=shard_refs[e], dst_ref=mine, send_sem=send_sems.at[4 * e + k],
                    recv_sem=recv_sems.at[4 * e + k], device_id=to, device_id_type=MESH).start()
        token[...] = jnp.zeros_like(token)

    out = pl.pallas_call(
        body, name=name,
        out_shape=(pltpu.SemaphoreType.DMA((4 * n,)), pltpu.SemaphoreType.DMA((4 * n,)), pltpu.SemaphoreType.DMA((n,)),
                   *[pltpu.HBM(a.shape, a.dtype) for a in shards], *[pltpu.HBM(a.shape, a.dtype) for a in bufs],
                   jax.ShapeDtypeStruct((8, LANE), F32)),
        in_specs=[HBM_SPEC] * (2 * n) + [ANY_SPEC] * len(extra),
        out_specs=(SEM_SPEC, SEM_SPEC, SEM_SPEC, *[HBM_SPEC] * (2 * n), pl.BlockSpec(memory_space=pltpu.VMEM)),
        input_output_aliases={i: 3 + i for i in range(2 * n)},
        compiler_params=pltpu.CompilerParams(has_side_effects=SPLIT_EFFECT),
    )(*[_in_hbm(a) for a in shards], *[_in_hbm(a) for a in bufs], *extra)
    return out[0], out[1], out[2], out[3:3 + n], out[3 + n:3 + 2 * n], out[-1]


def _gather_forward(recv_sems, bufs, kinds, shard_shapes, after, name):
    n = len(bufs)

    def body(*refs):
        buf_refs, recv_in = refs[:n], refs[n]
        fsend, frecv = refs[n + 2], refs[n + 3]
        token = refs[-1]
        me, sibling, chips = _gather_peers()
        for e in range(n):
            for j, chip in enumerate(chips):
                slot = _gather_slot(buf_refs[e], kinds[e], (*chip, me[2]), shard_shapes[e])
                pltpu.make_async_remote_copy(
                    src_ref=slot, dst_ref=slot, send_sem=recv_in.at[4 * e + 1 + j], recv_sem=recv_in.at[4 * e + 1 + j],
                    device_id=me, device_id_type=MESH).wait_recv()
                pltpu.make_async_remote_copy(
                    src_ref=slot, dst_ref=slot, send_sem=fsend.at[3 * e + j], recv_sem=frecv.at[3 * e + j],
                    device_id=sibling, device_id_type=MESH).start()
        token[...] = jnp.zeros_like(token)

    out = pl.pallas_call(
        body, name=name,
        out_shape=(pltpu.SemaphoreType.DMA((3 * n,)), pltpu.SemaphoreType.DMA((3 * n,)),
                   *[pltpu.HBM(a.shape, a.dtype) for a in bufs], jax.ShapeDtypeStruct((8, LANE), F32)),
        in_specs=[HBM_SPEC] * n + [SEM_SPEC, ANY_SPEC],
        out_specs=(SEM_SPEC, SEM_SPEC, *[HBM_SPEC] * n, pl.BlockSpec(memory_space=pltpu.VMEM)),
        input_output_aliases={i: 2 + i for i in range(n)},
        compiler_params=pltpu.CompilerParams(has_side_effects=SPLIT_EFFECT),
    )(*bufs, recv_sems, after)
    return out[0], out[1], out[2:2 + n], out[-1]


def _gather_wait(send_sems, recv_sems, local_sems, fsend, frecv, shards, bufs, kinds, after, name):
    n = len(bufs)

    def body(*refs):
        shard_refs, buf_refs = refs[:n], refs[n:2 * n]
        send_in, recv_in, local_in, fsend_in, frecv_in = refs[2 * n:2 * n + 5]
        me, sibling, chips = _gather_peers()

        def arrival(slot, sem):
            return pltpu.make_async_remote_copy(src_ref=slot, dst_ref=slot, send_sem=sem, recv_sem=sem,
                                                device_id=me, device_id_type=MESH)

        for e in range(n):
            shape = shards[e].shape
            mine = _gather_slot(buf_refs[e], kinds[e], me, shape)
            pltpu.make_async_copy(shard_refs[e], mine, local_in.at[e]).wait()
            arrival(_gather_slot(buf_refs[e], kinds[e], sibling, shape), recv_in.at[4 * e]).wait_recv()
            for j, chip in enumerate(chips):
                arrival(_gather_slot(buf_refs[e], kinds[e], (*chip, 1 - me[2]), shape), frecv_in.at[3 * e + j]).wait_recv()
            for k in range(4):
                arrival(mine, send_in.at[4 * e + k]).wait_send()
            for j in range(3):
                arrival(mine, fsend_in.at[3 * e + j]).wait_send()

    out = pl.pallas_call(
        body, name=name,
        out_shape=(*[pltpu.HBM(a.shape, a.dtype) for a in shards], *[pltpu.HBM(a.shape, a.dtype) for a in bufs]),
        in_specs=[HBM_SPEC] * (2 * n) + [SEM_SPEC] * 5 + [ANY_SPEC],
        out_specs=tuple([HBM_SPEC] * (2 * n)),
        input_output_aliases={i: i for i in range(2 * n)},
        compiler_params=pltpu.CompilerParams(has_side_effects=SPLIT_EFFECT),
    )(*shards, *bufs, send_sems, recv_sems, local_sems, fsend, frecv, after)
    return out[n:]


def _grad_slice(ref, kind, j):
    if kind == "major":
        return ref.at[j]
    if kind == "rows":
        r = ref.shape[0] // N_DEV
        return ref.at[pl.ds(j * r, r), :]
    cu = ref.shape[2] // 4
    return ref.at[j // 4, :, pl.ds((j % 4) * cu, cu)]


def _slice_shape(a, kind):
    if kind == "major":
        return a.shape[1:]
    if kind == "rows":
        return (a.shape[0] // N_DEV, a.shape[1])
    return (a.shape[1], a.shape[2] // 4)


def _scatter_copies(srcs, lands, kinds, send_sems, recv_sems):
    x_, y_, c_ = _my_pos()
    me = 4 * x_ + 2 * y_ + c_
    n = len(srcs)

    def remote(e, j):
        return pltpu.make_async_remote_copy(
            src_ref=_grad_slice(srcs[e], kinds[e], j), dst_ref=lands[e].at[me],
            send_sem=send_sems.at[e * N_DEV + j], recv_sem=recv_sems.at[e * N_DEV + me],
            device_id=(j // 4, (j // 2) % 2, j % 2), device_id_type=MESH)

    def local(e, j):
        return pltpu.make_async_copy(_grad_slice(srcs[e], kinds[e], j), lands[e].at[j], recv_sems.at[e * N_DEV + j])

    def arrival(e, i):
        return pltpu.make_async_remote_copy(
            src_ref=_grad_slice(srcs[e], kinds[e], i), dst_ref=lands[e].at[i],
            send_sem=send_sems.at[e * N_DEV + i], recv_sem=recv_sems.at[e * N_DEV + i],
            device_id=(i // 4, (i // 2) % 2, i % 2), device_id_type=MESH)

    def start():
        for e in range(n):
            for j in range(N_DEV):
                @pl.when(me == j)
                def _():
                    local(e, j).start()

                @pl.when(me != j)
                def _():
                    remote(e, j).start()

    def wait():
        for e in range(n):
            for i in range(N_DEV):
                @pl.when(me == i)
                def _():
                    local(e, i).wait()

                @pl.when(me != i)
                def _():
                    arrival(e, i).wait_recv()
        for e in range(n):
            for j in range(N_DEV):
                @pl.when(me != j)
                def _():
                    remote(e, j).wait_send()

    return start, wait


def _scatter_start(srcs, kinds, after, name):
    n = len(srcs)
    lands = [lax.empty((N_DEV,) + _slice_shape(a, k), a.dtype) for a, k in zip(srcs, kinds)]
    extra = [] if after is None else [after]

    def body(*refs):
        src_refs, land_refs = refs[:n], refs[n:2 * n]
        send_sems, recv_sems = refs[2 * n + len(extra)], refs[2 * n + len(extra) + 1]
        token = refs[-1]
        start, _ = _scatter_copies(src_refs, land_refs, kinds, send_sems, recv_sems)
        start()
        token[...] = jnp.zeros_like(token)

    out = pl.pallas_call(
        body, name=name,
        out_shape=(pltpu.SemaphoreType.DMA((n * N_DEV,)), pltpu.SemaphoreType.DMA((n * N_DEV,)),
                   *[pltpu.HBM(a.shape, a.dtype) for a in srcs], *[pltpu.HBM(a.shape, a.dtype) for a in lands],
                   jax.ShapeDtypeStruct((8, LANE), F32)),
        in_specs=[HBM_SPEC] * (2 * n) + [ANY_SPEC] * len(extra),
        out_specs=(SEM_SPEC, SEM_SPEC, *[HBM_SPEC] * (2 * n), pl.BlockSpec(memory_space=pltpu.VMEM)),
        input_output_aliases={i: 2 + i for i in range(2 * n)},
        compiler_params=pltpu.CompilerParams(has_side_effects=SPLIT_EFFECT),
    )(*[_in_hbm(a) for a in srcs], *[_in_hbm(a) for a in lands], *extra)
    return out[0], out[1], out[2:2 + n], out[2 + n:2 + 2 * n], out[-1]


def _scatter_wait(send_sems, recv_sems, srcs, lands, kinds, after, name):
    n = len(srcs)

    def body(*refs):
        src_refs, land_refs = refs[:n], refs[n:2 * n]
        _, wait = _scatter_copies(src_refs, land_refs, kinds, refs[2 * n], refs[2 * n + 1])
        wait()

    out = pl.pallas_call(
        body, name=name,
        out_shape=(*[pltpu.HBM(a.shape, a.dtype) for a in srcs], *[pltpu.HBM(a.shape, a.dtype) for a in lands]),
        in_specs=[HBM_SPEC] * (2 * n) + [SEM_SPEC, SEM_SPEC] + [ANY_SPEC] * len(after),
        out_specs=tuple([HBM_SPEC] * (2 * n)),
        input_output_aliases={i: i for i in range(2 * n)},
        compiler_params=pltpu.CompilerParams(has_side_effects=SPLIT_EFFECT),
    )(*srcs, *lands, send_sems, recv_sems, *after)
    return out[n:]


def _rope_tables(positions, sign):
    half = ROPE_DIM // 2
    inv_freq = ROPE_THETA ** (-jnp.arange(0, ROPE_DIM, 2, dtype=F32) / ROPE_DIM)
    ang = positions.astype(F32)[:, None] * inv_freq
    reps = LANE // half
    cos = jnp.tile(jnp.cos(ang), (1, reps))
    sin = jnp.tile(jnp.sin(ang), (1, reps)) * sign
    d = jnp.arange(LANE) % SWA_HEAD_DIM
    return (jnp.where(d < ROPE_DIM, cos, 1.0), jnp.where(d < half, -sin, 0.0),
            jnp.where((d >= half) & (d < ROPE_DIM), sin, 0.0))


def _pad_cols(a, n):
    return jnp.pad(a, ((0, 0), (0, n - a.shape[1])))


def _local_step(x, target, positions, mods, fetch, P, on_grads, on_small):
    S, D = x.shape
    Hf = D // FOX_HEAD_DIM
    Hq = D // SWA_HEAD_DIM
    Hk = Hq // SWA_GROUP
    Wk = Hk * SWA_HEAD_DIM
    n_in = 3 * D + Hf
    (sh1a, sc1a, g1a, sh2a, sc2a, g2a), (sh1b, sc1b, g1b, sh2b, sc2b, g2b) = mods
    row = lambda v: v.reshape(1, -1)
    cw = [jnp.transpose(P["conv_w"][l].reshape(3, 2, -1), (1, 0, 2)) for l in range(2)]
    cb = [P["conv_b"][l].reshape(2, 1, -1) for l in range(2)]

    W = dict(up=[None, None], down=[None, None])
    h1a = _modulate(x, sc1a, sh1a, "modulate_in")
    W["fox_in"], W["fox_o"] = fetch("fox", "wait", h1a)
    proj_a = _mm_nn(h1a, W["fox_in"], F32, "fox_in_proj", tn=896)
    flT = proj_a[:, 3 * D:n_in].T
    bf_col = P["fox_b_f"].reshape(Hf, 1)
    cumT = _fox_prep(flT, bf_col, "fox_cumsum")
    cum_row = cumT.reshape(Hf, 1, S)
    o_a, lse_a = _fox_attn_fwd(proj_a, cum_row, Hf, "fox_attn_fwd")
    token = fetch("ffn0", "forward", o_a)
    y1a = _mm_nn(o_a, W["fox_o"], F32, "fox_out_proj")
    z1a, x1, h2a = _ln_fwd(x, y1a, g1a, row(P["ln_mix_g"][0]), row(P["ln_mix_b"][0]), sc2a, sh2a, token, "ln_mix0")
    W["up"][0], W["down"][0] = fetch("ffn0", "wait", h2a)
    u_a, uc_a, a_a = _ffn_up(h2a, W["up"][0], cw[0], cb[0], "ffn_up0")
    token = fetch("swa", "forward", a_a)
    y2a = _mm_nn(a_a, W["down"][0], F32, "ffn_down0", tk=1408)
    z2a, x2, h1b = _ln_fwd(x1, y2a, g2a, row(P["ln_ffn_g"][0]), row(P["ln_ffn_b"][0]), sc1b, sh1b, token, "ln_ffn0")

    W["swa_in"], W["swa_o"] = fetch("swa", "wait", h1b)
    proj_b = _mm_nn(h1b, W["swa_in"], F32, "swa_in_proj")
    tabs_f = _rope_tables(positions, 1.0)
    tabs_b = _rope_tables(positions, -1.0)
    qk = _rope([(proj_b, True, D + Wk)], tabs_f, F32, "rope_fwd")
    sinks = _pad_cols(P["swa_sinks"].reshape(1, Hq), LANE)
    o_b, lse_b = _swa_attn_fwd(qk, proj_b, sinks, Hq, "swa_attn_fwd")
    token = fetch("ffn1", "forward", o_b)
    y1b = _mm_nn(o_b, W["swa_o"], F32, "swa_out_proj")
    z1b, x3, h2b = _ln_fwd(x2, y1b, g1b, row(P["ln_mix_g"][1]), row(P["ln_mix_b"][1]), sc2b, sh2b, token, "ln_mix1")
    W["up"][1], W["down"][1] = fetch("ffn1", "wait", h2b)
    u_b, uc_b, a_b = _ffn_up(h2b, W["up"][1], cw[1], cb[1], "ffn_up1")
    y2b = _mm_nn(a_b, W["down"][1], F32, "ffn_down1", tk=1408)
    z2b, dout, loss_row = _ln_fwd_loss(x3, y2b, g2b, row(P["ln_ffn_g"][1]), row(P["ln_ffn_b"][1]), target, "ln_ffn1_loss")

    def ffn_backward(dy, a, u, uc, h_in, l, tag):
        d_down = _mm_tn(a, dy[None], BF16, "ffn_dwdown" + tag, tm=1408, tn=1024)[0]
        token = on_grads("ffn_w_down" + tag, d_down)
        da = _mm_nt(dy[None], W["down"][l][None], F32, "ffn_da" + tag)
        du, dcw, dcb = _ffn_bwd_elem(da, u, uc, cw[l], token, "ffn_bwd_elem" + tag)
        d_up = _mm_tn(h_in, du, BF16, "ffn_dwup" + tag)
        token = on_grads("ffn_w_up" + tag, d_up)
        dh = _mm_nt(du, W["up"][l], F32, "ffn_dh" + tag, tn=1024, tk=1408)
        return dh, token, jnp.transpose(dcw, (1, 0, 2)).reshape(3, -1), dcb.reshape(-1)

    dz2b, dy2b, dg_f1, db_f1, dgate2b = _ln_bwd(dout, z2b, y2b, g2b, row(P["ln_ffn_g"][1]), "ln_ffn1_bwd")
    dh2b, token, dcw1, dcb1 = ffn_backward(dy2b, a_b, u_b, uc_b, h2b, 1, "1")
    dx3, dsc2b, dsh2b = _mod_bwd(dz2b, dh2b, x3, sc2b, token, "mod_ffn1_bwd")

    dz1b, dy1b, dg_m1, db_m1, dgate1b = _ln_bwd(dx3, z1b, y1b, g1b, row(P["ln_mix_g"][1]), "ln_mix1_bwd")
    token = on_grads("swa_w_o", _mm_tn(o_b, dy1b[None], BF16, "swa_dwo")[0])
    do_b = _mm_nt(dy1b[None], W["swa_o"][None], F32, "swa_do")
    dq_b, dk_b, dv_b, dsinks = _swa_attn_bwd(qk, proj_b, sinks, do_b, lse_b, Hq, token, "swa_attn_bwd")
    dproj_b = _rope([(dq_b, True, D), (dk_b, True, Wk), (dv_b, False, Wk)], tabs_b, BF16, "rope_bwd")
    token = on_grads("swa_w_in", _mm_tn(dproj_b, h1b[None], BF16, "swa_dwin", tm=1280, tn=1024)[0])
    dh1b = _mm_nt(dproj_b[None], W["swa_in"][None], F32, "swa_dh", tk=1280)
    dx2, dsc1b, dsh1b = _mod_bwd(dz1b, dh1b, x2, sc1b, token, "mod_mix1_bwd")

    dz2a, dy2a, dg_f0, db_f0, dgate2a = _ln_bwd(dx2, z2a, y2a, g2a, row(P["ln_ffn_g"][0]), "ln_ffn0_bwd")
    dh2a, token, dcw0, dcb0 = ffn_backward(dy2a, a_a, u_a, uc_a, h2a, 0, "0")
    dx1, dsc2a, dsh2a = _mod_bwd(dz2a, dh2a, x1, sc2a, token, "mod_ffn0_bwd")

    dz1a, dy1a, dg_m0, db_m0, dgate1a = _ln_bwd(dx1, z1a, y1a, g1a, row(P["ln_mix_g"][0]), "ln_mix0_bwd")
    token = on_grads("fox_w_o", _mm_tn(o_a, dy1a[None], BF16, "fox_dwo")[0])
    do_a = _mm_nt(dy1a[None], W["fox_o"][None], F32, "fox_do")
    dq_a, dk_a, dv_a, dcum_row, dcum_col = _fox_attn_bwd(proj_a, do_a, o_a, cum_row, lse_a, Hf, token, "fox_attn_bwd")
    dflT, dbf = _fox_prep_bwd(dcum_row.reshape(Hf, S), dcum_col.reshape(Hf, S), flT, bf_col, "fox_cumsum_bwd")
    n_pad = W["fox_in"].shape[1]
    dproj_a = jnp.concatenate([dq_a, dk_a.astype(BF16), dv_a.astype(BF16),
                               _pad_cols(dflT.T, n_pad - 3 * D).astype(BF16)], axis=1)
    dh1a = _mm_nt(dproj_a[None], W["fox_in"][None], F32, "fox_dh", tk=896)
    grad_x, dsc1a, dsh1a = _mod_bwd(dz1a, dh1a, x, sc1a, token, "mod_mix0_bwd")

    dmod = jnp.stack([jnp.concatenate([dsh1a, dsc1a, dgate1a, dsh2a, dsc2a, dgate2a], axis=1)[0],
                      jnp.concatenate([dsh1b, dsc1b, dgate1b, dsh2b, dsc2b, dgate2b], axis=1)[0]])
    small = dict(dmod=dmod, conv_b=jnp.stack([dcb0, dcb1]), conv_w=jnp.stack([dcw0, dcw1]),
                 ln_mix_g=jnp.concatenate([dg_m0, dg_m1]), ln_mix_b=jnp.concatenate([db_m0, db_m1]),
                 ln_ffn_g=jnp.concatenate([dg_f0, dg_f1]), ln_ffn_b=jnp.concatenate([db_f0, db_f1]),
                 fox_b_f=dbf.reshape(-1), swa_sinks=dsinks[0, :Hq], loss=loss_row[0, 0].reshape(1))
    exchanged = on_small(small)
    on_grads("fox_w_in", _mm_tn(dproj_a, h1a[None], BF16, "fox_dwin", tm=896, tn=1024)[0], exchanged)
    return grad_x


SMALL_ORDER = ("dmod", "conv_b", "conv_w", "ln_mix_g", "ln_mix_b", "ln_ffn_g", "ln_ffn_b", "fox_b_f", "swa_sinks", "loss")


def _pack_rows(arrays):
    chunks, spans, off = [], [], 0
    for a in arrays:
        flat = a.reshape(-1)
        n = -(-flat.shape[0] // LANE) * LANE
        chunks.append(jnp.pad(flat, (0, n - flat.shape[0])))
        spans.append((off, flat.shape[0], a.shape))
        off += n
    total = -(-off // (8 * LANE)) * (8 * LANE)
    chunks.append(jnp.zeros((total - off,), F32))
    return jnp.concatenate(chunks).reshape(-1, LANE), spans


def _unpack_rows(packed, spans):
    flat = packed.reshape(-1)
    return [flat[off:off + n].reshape(shape) for off, n, shape in spans]


def kernel(x, c, positions, fox_w_in, fox_b_f, fox_w_o, swa_w_in, swa_sinks, swa_w_o, ada_w, ada_b, ffn_w_up, ffn_conv_w, ffn_conv_b, ffn_w_down, ln_mix_g, ln_mix_b, ln_ffn_g, ln_ffn_b, loss_target, m_fox_w_in, m_fox_b_f, m_fox_w_o, m_swa_w_in, m_swa_sinks, m_swa_w_o, m_ada_w, m_ada_b, m_ffn_w_up, m_ffn_conv_w, m_ffn_conv_b, m_ffn_w_down, m_ln_mix_g, m_ln_mix_b, m_ln_ffn_g, m_ln_ffn_b, v_fox_w_in, v_fox_b_f, v_fox_w_o, v_swa_w_in, v_swa_sinks, v_swa_w_o, v_ada_w, v_ada_b, v_ffn_w_up, v_ffn_conv_w, v_ffn_conv_b, v_ffn_w_down, v_ln_mix_g, v_ln_mix_b, v_ln_ffn_g, v_ln_ffn_b):
    S, D = x.shape[1], x.shape[2]
    L = ada_w.shape[0]
    me = 4 * lax.axis_index("x") + 2 * lax.axis_index("y") + lax.axis_index("c")
    n_ada = ada_w.shape[2]
    cu = ffn_w_up.shape[2]
    F = 4 * cu
    n_in = fox_w_in.shape[2] * N_DEV
    n_in_pad = -(-n_in // LANE) * LANE

    gather_groups = dict(
        fox=([fox_w_in[0].astype(BF16), fox_w_o[0].astype(BF16)], ["major", "rows"]),
        ffn0=([ffn_w_up[0].astype(BF16), ffn_w_down[0].astype(BF16)], ["halves", "rows"]),
        swa=([swa_w_in[0].astype(BF16), swa_w_o[0].astype(BF16)], ["major", "rows"]),
        ffn1=([ffn_w_up[1].astype(BF16), ffn_w_down[1].astype(BF16)], ["halves", "rows"]))
    starts_after = dict(fox=["ffn0", "swa"], ffn0=["ffn1"])
    gathers = {}

    def start_group(group, after):
        shards, kinds = gather_groups[group]
        send, recv, local, thru, bufs, token = _gather_start(shards, kinds, after, "gather_start_" + group)
        gathers[group] = dict(send=send, recv=recv, local=local, shards=thru, bufs=bufs, kinds=kinds,
                              shapes=[a.shape for a in shards], token=token)
        return token

    fox_started = start_group("fox", None)

    c_all = _all_gather_small(c.reshape(-1, LANE), "gather_c", after=[fox_started]).reshape(N_DEV, D)
    b_cols = lax.dynamic_slice_in_dim(ada_b, me * n_ada, n_ada, axis=1).reshape(L, 1, n_ada)
    mod_blk = _ada_mod(jnp.pad(c_all, ((0, ADA_ROWS - N_DEV), (0, 0))), ada_w, b_cols, "ada_mod")[:, :N_DEV]
    mod_all = _all_gather_small(mod_blk.reshape(-1, LANE), "gather_mod").reshape(N_DEV, L, N_DEV, n_ada)
    mod_mine = lax.dynamic_index_in_dim(mod_all, me, axis=2, keepdims=False)
    mod_mine = jnp.transpose(mod_mine, (1, 0, 2)).reshape(L, N_DEV * n_ada)
    mods = [[mod_mine[l, k * D:(k + 1) * D].reshape(1, D) for k in range(6)] for l in range(L)]

    P = dict(fox_b_f=fox_b_f[0], swa_sinks=swa_sinks[0], conv_b=ffn_conv_b,
             ln_mix_g=ln_mix_g, ln_mix_b=ln_mix_b, ln_ffn_g=ln_ffn_g, ln_ffn_b=ln_ffn_b)
    cw_rows = _all_gather_small(_pack_rows([ffn_conv_w])[0], "gather_conv_w", after=[mod_all])
    n_cw = ffn_conv_w.size
    cw_dev = cw_rows.reshape(N_DEV, -1)[:, :n_cw].reshape(N_DEV, L, 3, cu)
    P["conv_w"] = jnp.transpose(cw_dev, (1, 2, 0, 3)).reshape(L, 3, N_DEV * cu)

    def natural(g, pad_to=None):
        slabs = [g[k] for k in range(N_DEV)]
        if pad_to is not None:
            slabs.append(jnp.zeros((D, pad_to - N_DEV * g.shape[2]), g.dtype))
        return jnp.concatenate(slabs, axis=1)

    def forward_stage(group, after):
        s = gathers[group]
        s["fsend"], s["frecv"], s["bufs"], token = _gather_forward(s["recv"], s["bufs"], s["kinds"], s["shapes"], after,
                                                                   "gather_forward_" + group)
        for nxt in starts_after.get(group, ()):
            token = start_group(nxt, token)
        return token

    def fetch(group, stage, after):
        if stage == "forward":
            return forward_stage(group, after)
        if group == "fox":
            after = forward_stage(group, cw_rows)
        s = gathers.pop(group)
        first, second = _gather_wait(s["send"], s["recv"], s["local"], s["fsend"], s["frecv"], s["shards"], s["bufs"],
                                     s["kinds"], after, "gather_wait_" + group)
        if group == "fox":
            return natural(first, n_in_pad), second
        if group == "swa":
            return natural(first), second
        return first, second

    out, pending = {}, {}

    def columns_major(g_t, n):
        return g_t[:n].reshape(N_DEV, n // N_DEV, D)

    def transposed(a):
        return jnp.transpose(a, (0, 2, 1))

    big = dict(
        ffn_w_down1=("ffn_w_down", "rows", 1, (ffn_w_down, m_ffn_w_down, v_ffn_w_down)),
        ffn_w_up1=("ffn_w_up", "halves", 1, (ffn_w_up, m_ffn_w_up, v_ffn_w_up)),
        swa_w_o=("swa_w_o", "rows", 0, (swa_w_o, m_swa_w_o, v_swa_w_o)),
        swa_w_in=("swa_w_in", "major", 0, tuple(transposed(a) for a in (swa_w_in, m_swa_w_in, v_swa_w_in))),
        ffn_w_down0=("ffn_w_down", "rows", 0, (ffn_w_down, m_ffn_w_down, v_ffn_w_down)),
        ffn_w_up0=("ffn_w_up", "halves", 0, (ffn_w_up, m_ffn_w_up, v_ffn_w_up)),
        fox_w_o=("fox_w_o", "rows", 0, (fox_w_o, m_fox_w_o, v_fox_w_o)),
        fox_w_in=("fox_w_in", "major", 0, (fox_w_in, m_fox_w_in, v_fox_w_in)))
    finish_at = dict(swa_w_o=["ffn_w_down1"], ffn_w_up0=["ffn_w_up1", "swa_w_o", "swa_w_in"], fox_w_o=["ffn_w_down0"],
                     fox_w_in=["ffn_w_up0"])
    tail = {}

    def finish(name, after):
        send, recv, thru, lands = pending.pop(name)
        param, kind, layer, wmv = big[name]
        if name == "fox_w_in":
            one = tail["last_start"][0, 0] + 1.0
            wmv = tuple(transposed(a * one) for a in wmv)
        landed, = _scatter_wait(send, recv, thru, lands, [kind], after, "scatter_wait_" + name)
        res = _adam_sum(landed, *wmv, layer, out.get(param), "adam_" + name, by_cols=kind == "major")
        out[param] = [transposed(r) for r in res] if kind == "major" else res

    def on_grads(name, g, after=None):
        kind = big[name][1]
        src = columns_major(g, n_in if name == "fox_w_in" else g.shape[0]) if kind == "major" else g
        send, recv, thru, lands, token = _scatter_start([src], [kind], after, "scatter_start_" + name)
        pending[name] = (send, recv, thru, lands)
        for done in finish_at.get(name, ()):
            finish(done, [token])
        tail["last_start"] = token
        return token

    def on_small(small):
        packed, tail["spans"] = _pack_rows([small[k] for k in SMALL_ORDER])
        tail["gathered"] = _all_gather_small(packed, "gather_small_grads")
        return tail["gathered"]

    grad_x = _local_step(x[0], loss_target[0], positions[0], mods, fetch, P, on_grads, on_small)

    gathered, spans = tail["gathered"], tail["spans"]
    totals = dict(zip(SMALL_ORDER, _unpack_rows(_sum_slots(gathered, tail["last_start"], "sum_small_grads"), spans)))
    loss = totals["loss"].reshape(())
    n_mod = L * 6 * D
    dmod_all = gathered.reshape(N_DEV, -1)[:, :n_mod].reshape(N_DEV, L, 6 * D)
    dmod_cols = jnp.transpose(lax.dynamic_slice_in_dim(dmod_all, me * n_ada, n_ada, axis=2), (1, 0, 2))
    g_ada_w = _ada_bwd(c_all.T, dmod_cols, "ada_w_grad")
    out["ada_w"] = (g_ada_w,) + tuple(_adam(g_ada_w, ada_w, m_ada_w, v_ada_w, "adam_ada_w"))

    g_small = dict(fox_b_f=totals["fox_b_f"].reshape(fox_b_f.shape), swa_sinks=totals["swa_sinks"].reshape(swa_sinks.shape),
                   ada_b=totals["dmod"].reshape(ada_b.shape), ffn_conv_b=totals["conv_b"].reshape(ffn_conv_b.shape),
                   ffn_conv_w=lax.dynamic_slice_in_dim(totals["conv_w"].reshape(L, 3, 2 * F), me * cu, cu, axis=2),
                   ln_mix_g=totals["ln_mix_g"], ln_mix_b=totals["ln_mix_b"],
                   ln_ffn_g=totals["ln_ffn_g"], ln_ffn_b=totals["ln_ffn_b"])
    small_names = ("fox_b_f", "swa_sinks", "ada_b", "ffn_conv_b", "ffn_conv_w", "ln_mix_g", "ln_mix_b", "ln_ffn_g", "ln_ffn_b")
    w_small = dict(fox_b_f=(fox_b_f, m_fox_b_f, v_fox_b_f), swa_sinks=(swa_sinks, m_swa_sinks, v_swa_sinks),
                   ada_b=(ada_b, m_ada_b, v_ada_b), ffn_conv_b=(ffn_conv_b, m_ffn_conv_b, v_ffn_conv_b),
                   ffn_conv_w=(ffn_conv_w, m_ffn_conv_w, v_ffn_conv_w),
                   ln_mix_g=(ln_mix_g, m_ln_mix_g, v_ln_mix_g), ln_mix_b=(ln_mix_b, m_ln_mix_b, v_ln_mix_b),
                   ln_ffn_g=(ln_ffn_g, m_ln_ffn_g, v_ln_ffn_g), ln_ffn_b=(ln_ffn_b, m_ln_ffn_b, v_ln_ffn_b))
    pk_g, sp = _pack_rows([g_small[k] for k in small_names])
    pk_w = _pack_rows([w_small[k][0] for k in small_names])[0]
    pk_m = _pack_rows([w_small[k][1] for k in small_names])[0]
    pk_v = _pack_rows([w_small[k][2] for k in small_names])[0]
    res = _adam(pk_g[None], pk_w[None], pk_m[None], pk_v[None], "adam_small")
    settled = [res[0], out["ada_w"][1]] + [out[k][1] for k in ("ffn_w_up", "ffn_w_down", "swa_w_in", "swa_w_o")]
    finish("fox_w_o", settled)
    finish("fox_w_in", settled)
    res = [dict(zip(small_names, _unpack_rows(r[0], sp))) for r in res]
    for k in small_names:
        out[k] = (g_small[k], res[0][k], res[1][k], res[2][k])

    order = ("fox_w_in", "fox_b_f", "fox_w_o", "swa_w_in", "swa_sinks", "swa_w_o", "ada_w", "ada_b", "ffn_w_up",
             "ffn_conv_w", "ffn_conv_b", "ffn_w_down", "ln_mix_g", "ln_mix_b", "ln_ffn_g", "ln_ffn_b")
    return (loss, grad_x[None], *[out[k][0] for k in order], *[out[k][1] for k in order],
            *[out[k][2] for k in order], *[out[k][3] for k in order])
```

```python
import functools

import jax
import jax.numpy as jnp
from jax import lax
from jax.experimental import pallas as pl
from jax.experimental.pallas import tpu as pltpu

F32 = jnp.float32
BF16 = jnp.bfloat16
MESH = pl.DeviceIdType.MESH
N_DEV = 8
AXES = ("x", "y", "c")

DEPTH = 2
ALPHA = (2.0 * DEPTH) ** 0.25
LN_EPS = 1e-5
FOX_HEAD_DIM = 128
SWA_HEAD_DIM = 64
SWA_GROUP = 8
SWA_WINDOW = 128
Q_BLOCK = 128
ROPE_DIM = 16
ROPE_THETA = 500000.0

ADAM_LR = 0.001
ADAM_B1 = 0.9
ADAM_B2 = 0.999
ADAM_EPS = 1e-08
ADAM_WD = 0.01
ADAM_STEP = 10

LANE = 128
MIB = 1024 * 1024


def _tile(n, pref, unit=LANE):
    if n <= pref:
        return n
    t = (pref // unit) * unit
    while t >= unit:
        if n % t == 0:
            return t
        t -= unit
    return n


def _params(sem, vmem_mib=48):
    return pltpu.CompilerParams(dimension_semantics=sem, vmem_limit_bytes=vmem_mib * MIB)


def _sigmoid(x):
    return 1.0 / (1.0 + jnp.exp(-x))


def _mm_call(dot, grid_mnk, in_specs, out_spec, out_shape, k_axis, nk, tm, tn, name, operands):
    sem = ("parallel",) * (len(grid_mnk) - 1) + ("arbitrary",)

    if nk == 1:
        def body(a_ref, b_ref, o_ref):
            o_ref[...] = dot(a_ref[...], b_ref[...]).astype(o_ref.dtype)
        scratch = []
    else:
        def body(a_ref, b_ref, o_ref, acc_ref):
            k = pl.program_id(k_axis)

            @pl.when(k == 0)
            def _():
                acc_ref[...] = jnp.zeros_like(acc_ref)

            acc_ref[...] += dot(a_ref[...], b_ref[...])

            @pl.when(k == nk - 1)
            def _():
                o_ref[...] = acc_ref[...].astype(o_ref.dtype)
        scratch = [pltpu.VMEM((tm, tn), F32)]

    return pl.pallas_call(
        body, name=name, grid=grid_mnk, in_specs=in_specs, out_specs=out_spec, out_shape=out_shape,
        scratch_shapes=scratch, compiler_params=_params(sem, 56),
    )(*operands)


def _dot(dims):
    def dot(a, b):
        return lax.dot_general(a.astype(BF16), b.astype(BF16), (dims, ((), ())), preferred_element_type=F32)
    return dot


def _mm_nn(a, b, out_dtype, name, tm=2048, tn=512, tk=2048):
    M, K = a.shape
    N = b.shape[1]
    tm, tn, tk = _tile(M, tm), _tile(N, tn), _tile(K, tk)
    nk = K // tk
    return _mm_call(
        _dot(((1,), (0,))), (M // tm, N // tn, nk),
        [pl.BlockSpec((tm, tk), lambda i, j, k: (i, k)), pl.BlockSpec((tk, tn), lambda i, j, k: (k, j))],
        pl.BlockSpec((tm, tn), lambda i, j, k: (i, j)), jax.ShapeDtypeStruct((M, N), out_dtype),
        2, nk, tm, tn, name, (a, b))


def _mm_nt(a, b, out_dtype, name, tm=2048, tn=512, tk=2048):
    P, M, K = a.shape
    N = b.shape[1]
    tm, tn, tk = _tile(M, tm), _tile(N, tn), _tile(K, tk)
    nk = K // tk
    return _mm_call(
        _dot(((1,), (1,))), (M // tm, N // tn, P * nk),
        [pl.BlockSpec((None, tm, tk), lambda i, j, k: (k // nk, i, k % nk)),
         pl.BlockSpec((None, tn, tk), lambda i, j, k: (k // nk, j, k % nk))],
        pl.BlockSpec((tm, tn), lambda i, j, k: (i, j)), jax.ShapeDtypeStruct((M, N), out_dtype),
        2, P * nk, tm, tn, name, (a, b))


def _mm_tn(a, b, out_dtype, name, tm=2048, tn=512, tk=2048):
    K, M = a.shape
    P, _, N = b.shape
    tm, tn, tk = _tile(M, tm), _tile(N, tn), _tile(K, tk)
    nk = K // tk
    return _mm_call(
        _dot(((0,), (0,))), (P, M // tm, N // tn, nk),
        [pl.BlockSpec((tk, tm), lambda p, i, j, k: (k, i)), pl.BlockSpec((None, tk, tn), lambda p, i, j, k: (p, k, j))],
        pl.BlockSpec((None, tm, tn), lambda p, i, j, k: (p, i, j)), jax.ShapeDtypeStruct((P, M, N), out_dtype),
        3, nk, tm, tn, name, (a, b))


ROW_TILE = 256


def _row_spec(tm, D):
    return pl.BlockSpec((tm, D), lambda i: (i, 0))


def _vec_spec(D):
    return pl.BlockSpec((1, D), lambda i: (0, 0))


def _modulate(x, sc, sh, name):
    S, D = x.shape
    tm = _tile(S, ROW_TILE, 8)

    def body(x_ref, sc_ref, sh_ref, h_ref):
        h_ref[...] = (x_ref[...] * (1.0 + sc_ref[...]) + sh_ref[...]).astype(BF16)

    return pl.pallas_call(
        body, name=name, grid=(S // tm,),
        in_specs=[_row_spec(tm, D), _vec_spec(D), _vec_spec(D)],
        out_specs=_row_spec(tm, D),
        out_shape=jax.ShapeDtypeStruct((S, D), BF16),
        compiler_params=_params(("parallel",)),
    )(x, sc, sh)


def _layer_norm_rows(z, gamma, beta):
    mu = jnp.mean(z, axis=-1, keepdims=True)
    zc = z - mu
    var = jnp.mean(zc * zc, axis=-1, keepdims=True)
    return zc * lax.rsqrt(var + LN_EPS) * gamma + beta


def _ln_fwd(x, y, gate, gamma, beta, sc_n, sh_n, after, name):
    S, D = x.shape
    tm = _tile(S, ROW_TILE, 8)

    def body(x_ref, y_ref, gate_ref, g_ref, b_ref, sc_ref, sh_ref, after_ref, z_ref, xo_ref, hn_ref):
        z = ALPHA * x_ref[...] + (1.0 + gate_ref[...]) * y_ref[...]
        xo = _layer_norm_rows(z, g_ref[...], b_ref[...])
        z_ref[...] = z
        xo_ref[...] = xo
        hn_ref[...] = (xo * (1.0 + sc_ref[...]) + sh_ref[...]).astype(BF16)

    return pl.pallas_call(
        body, name=name, grid=(S // tm,),
        in_specs=[_row_spec(tm, D), _row_spec(tm, D)] + [_vec_spec(D)] * 5 + [pl.BlockSpec(memory_space=pl.ANY)],
        out_specs=[_row_spec(tm, D)] * 3,
        out_shape=[jax.ShapeDtypeStruct((S, D), F32), jax.ShapeDtypeStruct((S, D), F32),
                   jax.ShapeDtypeStruct((S, D), BF16)],
        compiler_params=_params(("parallel",)),
    )(x, y, gate, gamma, beta, sc_n, sh_n, after)


def _ln_fwd_loss(x, y, gate, gamma, beta, target, name):
    S, D = x.shape
    tm = _tile(S, ROW_TILE, 8)

    def body(x_ref, y_ref, gate_ref, g_ref, b_ref, t_ref, z_ref, dout_ref, loss_ref):
        @pl.when(pl.program_id(0) == 0)
        def _():
            loss_ref[...] = jnp.zeros_like(loss_ref)

        z = ALPHA * x_ref[...] + (1.0 + gate_ref[...]) * y_ref[...]
        xo = _layer_norm_rows(z, g_ref[...], b_ref[...])
        err = xo - t_ref[...]
        z_ref[...] = z
        dout_ref[...] = err * (1.0 / D)
        loss_ref[...] += (0.5 / D) * jnp.sum(err * err)

    return pl.pallas_call(
        body, name=name, grid=(S // tm,),
        in_specs=[_row_spec(tm, D), _row_spec(tm, D)] + [_vec_spec(D)] * 3 + [_row_spec(tm, D)],
        out_specs=[_row_spec(tm, D), _row_spec(tm, D), pl.BlockSpec((1, LANE), lambda i: (0, 0))],
        out_shape=[jax.ShapeDtypeStruct((S, D), F32), jax.ShapeDtypeStruct((S, D), F32),
                   jax.ShapeDtypeStruct((1, LANE), F32)],
        compiler_params=_params(("arbitrary",)),
    )(x, y, gate, gamma, beta, target)


def _ln_bwd(dout, z, y, gate, gamma, name):
    S, D = z.shape
    tm = _tile(S, ROW_TILE, 8)

    def body(dout_ref, z_ref, y_ref, gate_ref, g_ref, dz_ref, dy_ref, dg_ref, db_ref, dgate_ref):
        @pl.when(pl.program_id(0) == 0)
        def _():
            dg_ref[...] = jnp.zeros_like(dg_ref)
            db_ref[...] = jnp.zeros_like(db_ref)
            dgate_ref[...] = jnp.zeros_like(dgate_ref)

        z = z_ref[...]
        dout = dout_ref[...]
        mu = jnp.mean(z, axis=-1, keepdims=True)
        zc = z - mu
        var = jnp.mean(zc * zc, axis=-1, keepdims=True)
        rstd = lax.rsqrt(var + LN_EPS)
        xhat = zc * rstd
        dxhat = dout * g_ref[...]
        m1 = jnp.mean(dxhat, axis=-1, keepdims=True)
        m2 = jnp.mean(dxhat * xhat, axis=-1, keepdims=True)
        dz = rstd * (dxhat - m1 - xhat * m2)
        dz_ref[...] = dz
        dy_ref[...] = (dz * (1.0 + gate_ref[...])).astype(BF16)
        dg_ref[...] += jnp.sum(dout * xhat, axis=0, keepdims=True)
        db_ref[...] += jnp.sum(dout, axis=0, keepdims=True)
        dgate_ref[...] += jnp.sum(dz * y_ref[...], axis=0, keepdims=True)

    return pl.pallas_call(
        body, name=name, grid=(S // tm,),
        in_specs=[_row_spec(tm, D)] * 3 + [_vec_spec(D)] * 2,
        out_specs=[_row_spec(tm, D), _row_spec(tm, D)] + [_vec_spec(D)] * 3,
        out_shape=[jax.ShapeDtypeStruct((S, D), F32), jax.ShapeDtypeStruct((S, D), BF16)]
        + [jax.ShapeDtypeStruct((1, D), F32)] * 3,
        compiler_params=_params(("arbitrary",)),
    )(dout, z, y, gate, gamma)


def _mod_bwd(dz, dh, xin, sc, after, name):
    S, D = dz.shape
    tm = _tile(S, ROW_TILE, 8)

    def body(dz_ref, dh_ref, x_ref, sc_ref, after_ref, dx_ref, dsc_ref, dsh_ref):
        @pl.when(pl.program_id(0) == 0)
        def _():
            dsc_ref[...] = jnp.zeros_like(dsc_ref)
            dsh_ref[...] = jnp.zeros_like(dsh_ref)

        dh = dh_ref[...]
        dx_ref[...] = ALPHA * dz_ref[...] + dh * (1.0 + sc_ref[...])
        dsc_ref[...] += jnp.sum(dh * x_ref[...], axis=0, keepdims=True)
        dsh_ref[...] += jnp.sum(dh, axis=0, keepdims=True)

    return pl.pallas_call(
        body, name=name, grid=(S // tm,),
        in_specs=[_row_spec(tm, D)] * 3 + [_vec_spec(D), pl.BlockSpec(memory_space=pl.ANY)],
        out_specs=[_row_spec(tm, D), _vec_spec(D), _vec_spec(D)],
        out_shape=[jax.ShapeDtypeStruct((S, D), F32)] + [jax.ShapeDtypeStruct((1, D), F32)] * 2,
        compiler_params=_params(("arbitrary",)),
    )(dz, dh, xin, sc, after)


def _shift_down(u, k, row):
    return jnp.where(row >= k, pltpu.roll(u, k, axis=0), 0.0)


def _shift_up(u, k, row, S):
    return jnp.where(row < S - k, pltpu.roll(u, S - k, axis=0), 0.0)


def _ffn_up(h, w, cw, cb, name):
    S, D = h.shape
    F = w.shape[2]
    tn = _tile(F, 256)

    def body(h_ref, w_ref, cw_ref, cb_ref, u_ref, uc_ref, a_ref):
        hh = h_ref[...]
        row = lax.broadcasted_iota(jnp.int32, (S, tn), 0)
        conv = []
        for p in range(2):
            u = jnp.dot(hh, w_ref[p], preferred_element_type=F32)
            u_ref[p] = u.astype(BF16)
            cwp = cw_ref[p]
            uc = _shift_down(u, 2, row) * cwp[0:1] + _shift_down(u, 1, row) * cwp[1:2] + u * cwp[2:3] + cb_ref[p]
            uc_ref[p] = uc.astype(BF16)
            conv.append(uc)
        g, v = conv
        a_ref[...] = (g * _sigmoid(g) * v).astype(BF16)

    half = pl.BlockSpec((2, S, tn), lambda j: (0, 0, j))
    return pl.pallas_call(
        body, name=name, grid=(F // tn,),
        in_specs=[pl.BlockSpec((S, D), lambda j: (0, 0)), pl.BlockSpec((2, D, tn), lambda j: (0, 0, j)),
                  pl.BlockSpec((2, 3, tn), lambda j: (0, 0, j)), pl.BlockSpec((2, 1, tn), lambda j: (0, 0, j))],
        out_specs=[half, half, pl.BlockSpec((S, tn), lambda j: (0, j))],
        out_shape=[jax.ShapeDtypeStruct((2, S, F), BF16), jax.ShapeDtypeStruct((2, S, F), BF16),
                   jax.ShapeDtypeStruct((S, F), BF16)],
        compiler_params=_params(("parallel",), 56),
    )(h, w, cw, cb)


def _ffn_bwd_elem(da, u, uc, cw, after, name):
    _, S, F = u.shape
    tn = _tile(F, 256)

    def body(da_ref, u_ref, uc_ref, cw_ref, after_ref, du_ref, dcw_ref, dcb_ref):
        row = lax.broadcasted_iota(jnp.int32, (S, tn), 0)
        da = da_ref[...]
        g, v = uc_ref[0].astype(F32), uc_ref[1].astype(F32)
        sg = _sigmoid(g)
        d_conv = (da * v * (sg * (1.0 + g * (1.0 - sg))), da * (g * sg))
        for p in range(2):
            d = d_conv[p]
            cwp = cw_ref[p]
            u = u_ref[p].astype(F32)
            d1, d2 = _shift_up(d, 1, row, S), _shift_up(d, 2, row, S)
            dcb_ref[p] = jnp.sum(d, axis=0, keepdims=True)
            dcw_ref[p, 0:1, :] = jnp.sum(d2 * u, axis=0, keepdims=True)
            dcw_ref[p, 1:2, :] = jnp.sum(d1 * u, axis=0, keepdims=True)
            dcw_ref[p, 2:3, :] = jnp.sum(d * u, axis=0, keepdims=True)
            du_ref[p] = (d * cwp[2:3] + d1 * cwp[1:2] + d2 * cwp[0:1]).astype(BF16)

    half = pl.BlockSpec((2, S, tn), lambda j: (0, 0, j))
    return pl.pallas_call(
        body, name=name, grid=(F // tn,),
        in_specs=[pl.BlockSpec((S, tn), lambda j: (0, j)), half, half, pl.BlockSpec((2, 3, tn), lambda j: (0, 0, j)),
                  pl.BlockSpec(memory_space=pl.ANY)],
        out_specs=[half, pl.BlockSpec((2, 3, tn), lambda j: (0, 0, j)), pl.BlockSpec((2, 1, tn), lambda j: (0, 0, j))],
        out_shape=[jax.ShapeDtypeStruct((2, S, F), BF16), jax.ShapeDtypeStruct((2, 3, F), F32),
                   jax.ShapeDtypeStruct((2, 1, F), F32)],
        compiler_params=_params(("parallel",), 56),
    )(da, u, uc, cw, after)


def _split3(x):
    hi = x.astype(BF16)
    r1 = x - hi.astype(F32)
    mid = r1.astype(BF16)
    lo = (r1 - mid.astype(F32)).astype(BF16)
    return hi, mid, lo


def _tri_matmul(x, upper, S):
    tc = _tile(S, 512)
    parts = _split3(x)
    outs = []
    for b in range(S // tc):
        r = lax.broadcasted_iota(jnp.int32, (S, tc), 0)
        c = lax.broadcasted_iota(jnp.int32, (S, tc), 1) + b * tc
        tri = jnp.where((r <= c) if upper else (r >= c), 1.0, 0.0).astype(BF16)
        acc = jnp.dot(parts[0], tri, preferred_element_type=F32)
        acc += jnp.dot(parts[1], tri, preferred_element_type=F32)
        acc += jnp.dot(parts[2], tri, preferred_element_type=F32)
        outs.append(acc)
    return outs, tc


def _fox_prep(flT, bf, name):
    H, S = flT.shape

    def body(fl_ref, b_ref, cum_ref):
        zz = fl_ref[...] + b_ref[...]
        lf = jnp.minimum(zz, 0.0) - jnp.log(1.0 + jnp.exp(-jnp.abs(zz)))
        outs, tc = _tri_matmul(lf, True, S)
        for b, o in enumerate(outs):
            cum_ref[:, b * tc:(b + 1) * tc] = o

    return pl.pallas_call(
        body, name=name,
        in_specs=[pl.BlockSpec(memory_space=pltpu.VMEM)] * 2,
        out_specs=pl.BlockSpec(memory_space=pltpu.VMEM),
        out_shape=jax.ShapeDtypeStruct((H, S), F32),
        compiler_params=pltpu.CompilerParams(vmem_limit_bytes=48 * MIB),
    )(flT, bf)


def _fox_prep_bwd(dcum_key, dcum_query, flT, bf, name):
    H, S = flT.shape

    def body(dck_ref, dcq_ref, fl_ref, b_ref, dfl_ref, dbf_ref):
        zz = fl_ref[...] + b_ref[...]
        outs, tc = _tri_matmul(dck_ref[...] + dcq_ref[...], False, S)
        total = jnp.zeros((H, 1), F32)
        for b, o in enumerate(outs):
            dfl = o * _sigmoid(-zz[:, b * tc:(b + 1) * tc])
            dfl_ref[:, b * tc:(b + 1) * tc] = dfl
            total += jnp.sum(dfl, axis=1, keepdims=True)
        dbf_ref[...] = total

    return pl.pallas_call(
        body, name=name,
        in_specs=[pl.BlockSpec(memory_space=pltpu.VMEM)] * 4,
        out_specs=[pl.BlockSpec(memory_space=pltpu.VMEM)] * 2,
        out_shape=[jax.ShapeDtypeStruct((H, S), F32), jax.ShapeDtypeStruct((H, 1), F32)],
        compiler_params=pltpu.CompilerParams(vmem_limit_bytes=48 * MIB),
    )(dcum_key, dcum_query, flT, bf)


FOX_TQ = 256
FOX_TC = 512


def _fox_scores(q, k_ref, ck_ref, i, lo, n, tq):
    k = k_ref[lo:lo + n, :].astype(BF16)
    s = lax.dot_general(q, k, (((1,), (1,)), ((), ())), preferred_element_type=F32) - ck_ref[:, lo:lo + n]
    qpos = i * tq + lax.broadcasted_iota(jnp.int32, (tq, n), 0)
    kpos = lo + lax.broadcasted_iota(jnp.int32, (tq, n), 1)
    return jnp.where(kpos <= qpos, s, -jnp.inf)


FOX_FWD_SPLITS = 4


def _fox_attn_fwd_part(proj, cum_row, H, q_lo, q_hi, filled, name):
    S = proj.shape[0]
    dh = FOX_HEAD_DIM
    tq = _tile(q_hi - q_lo, FOX_TQ)
    first = q_lo // tq
    scale = dh ** -0.5
    n_keep = 0 if filled is None else 2

    def body(q_ref, k_ref, v_ref, ck_ref, *rest):
        o_ref, lse_ref = rest[n_keep:]
        q = (q_ref[...] * scale).astype(BF16)
        s = _fox_scores(q, k_ref, ck_ref, first + pl.program_id(1), 0, q_hi, tq)
        m = jnp.max(s, axis=-1, keepdims=True)
        p = jnp.exp(s - m)
        l = jnp.sum(p, axis=-1, keepdims=True)
        o = jnp.dot(p.astype(BF16), v_ref[...].astype(BF16), preferred_element_type=F32) / l
        o_ref[...] = o.astype(BF16)
        lse_ref[...] = m + jnp.log(l)

    return pl.pallas_call(
        body, name=name, grid=(H, (q_hi - q_lo) // tq),
        in_specs=[pl.BlockSpec((tq, dh), lambda h, i: (first + i, h)),
                  pl.BlockSpec((q_hi, dh), lambda h, i: (0, H + h)),
                  pl.BlockSpec((q_hi, dh), lambda h, i: (0, 2 * H + h)),
                  pl.BlockSpec((None, 1, q_hi), lambda h, i: (h, 0, 0))] + [pl.BlockSpec(memory_space=pl.ANY)] * n_keep,
        out_specs=[pl.BlockSpec((tq, dh), lambda h, i: (first + i, h)),
                   pl.BlockSpec((None, tq, 1), lambda h, i: (h, first + i, 0))],
        out_shape=[jax.ShapeDtypeStruct((S, H * dh), BF16), jax.ShapeDtypeStruct((H, S, 1), F32)],
        input_output_aliases={4 + k: k for k in range(n_keep)},
        compiler_params=_params(("parallel", "parallel")),
    )(proj, proj, proj, cum_row, *(filled or ()))


def _fox_attn_fwd(proj, cum_row, H, name):
    S = proj.shape[0]
    n = FOX_FWD_SPLITS if S % (FOX_FWD_SPLITS * FOX_TQ) == 0 else 1
    out = None
    for part in range(n):
        out = _fox_attn_fwd_part(proj, cum_row, H, part * S // n, (part + 1) * S // n, out, name + str(part))
    return out


def _fox_attn_bwd(proj, do, o, cum_row, lse, H, after, name):
    S = proj.shape[0]
    dh = FOX_HEAD_DIM
    tq = _tile(S, FOX_TQ)
    tc = _tile(S, FOX_TC)
    scale = dh ** -0.5

    def body(q_ref, k_ref, v_ref, do_ref, o_ref, ck_ref, lse_ref, after_ref,
             dq_ref, dk_ref, dv_ref, dck_ref, dcq_ref, acc_ref):
        i = pl.program_id(1)

        @pl.when(i == 0)
        def _():
            dk_ref[...] = jnp.zeros_like(dk_ref)
            dv_ref[...] = jnp.zeros_like(dv_ref)
            dck_ref[...] = jnp.zeros_like(dck_ref)

        acc_ref[...] = jnp.zeros_like(acc_ref)
        dcq_ref[...] = jnp.zeros_like(dcq_ref)
        q = (q_ref[...] * scale).astype(BF16)
        do_f = do_ref[...]
        do_b = do_f.astype(BF16)
        delta = jnp.sum(do_f * o_ref[...].astype(F32), axis=-1, keepdims=True)
        lse_q = lse_ref[...]
        for c in range(S // tc):
            @pl.when(c * tc <= i * tq + tq - 1)
            def _():
                rows = slice(c * tc, (c + 1) * tc)
                p = jnp.exp(_fox_scores(q, k_ref, ck_ref, i, c * tc, tc, tq) - lse_q)
                dp = lax.dot_general(do_b, v_ref[rows, :].astype(BF16), (((1,), (1,)), ((), ())),
                                     preferred_element_type=F32)
                ds = p * (dp - delta)
                ds_b = ds.astype(BF16)
                acc_ref[...] += jnp.dot(ds_b, k_ref[rows, :].astype(BF16), preferred_element_type=F32)
                dk_ref[rows, :] += lax.dot_general(ds_b, q, (((0,), (0,)), ((), ())), preferred_element_type=F32)
                dv_ref[rows, :] += lax.dot_general(p.astype(BF16), do_b, (((0,), (0,)), ((), ())),
                                                   preferred_element_type=F32)
                dck_ref[:, rows] -= jnp.sum(ds, axis=0, keepdims=True)
                dcq_ref[...] += jnp.sum(ds, axis=-1, keepdims=True)
        dq_ref[...] = (acc_ref[...] * scale).astype(BF16)

    W = H * dh
    return pl.pallas_call(
        body, name=name, grid=(H, S // tq),
        in_specs=[pl.BlockSpec((tq, dh), lambda h, i: (i, h)),
                  pl.BlockSpec((S, dh), lambda h, i: (0, H + h)),
                  pl.BlockSpec((S, dh), lambda h, i: (0, 2 * H + h)),
                  pl.BlockSpec((tq, dh), lambda h, i: (i, h)),
                  pl.BlockSpec((tq, dh), lambda h, i: (i, h)),
                  pl.BlockSpec((None, 1, S), lambda h, i: (h, 0, 0)),
                  pl.BlockSpec((None, tq, 1), lambda h, i: (h, i, 0)),
                  pl.BlockSpec(memory_space=pl.ANY)],
        out_specs=[pl.BlockSpec((tq, dh), lambda h, i: (i, h)),
                   pl.BlockSpec((S, dh), lambda h, i: (0, h)),
                   pl.BlockSpec((S, dh), lambda h, i: (0, h)),
                   pl.BlockSpec((None, 1, S), lambda h, i: (h, 0, 0)),
                   pl.BlockSpec((None, tq, 1), lambda h, i: (h, i, 0))],
        out_shape=[jax.ShapeDtypeStruct((S, W), BF16), jax.ShapeDtypeStruct((S, W), F32),
                   jax.ShapeDtypeStruct((S, W), F32), jax.ShapeDtypeStruct((H, 1, S), F32),
                   jax.ShapeDtypeStruct((H, S, 1), F32)],
        scratch_shapes=[pltpu.VMEM((tq, dh), F32)],
        compiler_params=_params(("parallel", "arbitrary")),
    )(proj, proj, proj, do, o, cum_row, lse, after)


def _rope(parts, tabs, out_dtype, name):
    S = parts[0][0].shape[0]
    widths = [w for _, _, w in parts]
    total = sum(widths)
    tm = _tile(S, ROW_TILE, 8)
    flags = [r for _, r, _ in parts]

    def body(*refs):
        in_refs = refs[:len(parts)]
        cos_ref, sa_ref, sb_ref, o_ref = refs[len(parts):]
        cos, sa, sb = cos_ref[...], sa_ref[...], sb_ref[...]
        off = 0
        for ref, rot, w in zip(in_refs, flags, widths):
            for j in range(w // LANE):
                t = ref[:, j * LANE:(j + 1) * LANE]
                if rot:
                    t = t * cos + pltpu.roll(t, LANE - ROPE_DIM // 2, axis=1) * sa + pltpu.roll(t, ROPE_DIM // 2, axis=1) * sb
                o_ref[:, off + j * LANE:off + (j + 1) * LANE] = t.astype(o_ref.dtype)
            off += w

    return pl.pallas_call(
        body, name=name, grid=(S // tm,),
        in_specs=[pl.BlockSpec((tm, w), lambda i: (i, 0)) for w in widths] + [_row_spec(tm, LANE)] * 3,
        out_specs=_row_spec(tm, total),
        out_shape=jax.ShapeDtypeStruct((S, total), out_dtype),
        compiler_params=_params(("parallel",)),
    )(*[a for a, _, _ in parts], *tabs)


def _swa_band(ref_p, ref_c, hk):
    dh = SWA_HEAD_DIM
    return jnp.concatenate([ref_p[:, hk * dh:(hk + 1) * dh], ref_c[:, hk * dh:(hk + 1) * dh]], axis=0).astype(BF16)


def _swa_bias(G):
    qi = jnp.arange(G * Q_BLOCK)[:, None] % Q_BLOCK
    kj = jnp.arange(2 * Q_BLOCK)[None, :]
    rel = qi + Q_BLOCK - kj
    window = (rel >= 0) & (rel < SWA_WINDOW)
    both = jnp.stack([window & (kj >= Q_BLOCK), window])
    return jnp.where(both, 0.0, -jnp.inf).astype(F32)


def _swa_stack(ref, hk, G):
    dh = SWA_HEAD_DIM
    return jnp.concatenate([ref[:, (hk * G + g) * dh:(hk * G + g + 1) * dh] for g in range(G)], axis=0)


def _swa_unstack(ref, stacked, hk, G):
    dh, QB = SWA_HEAD_DIM, Q_BLOCK
    for g in range(0, G, 2):
        c0 = (hk * G + g) * dh
        pair = jnp.concatenate([stacked[g * QB:(g + 1) * QB], stacked[(g + 1) * QB:(g + 2) * QB]], axis=1)
        ref[:, c0:c0 + 2 * dh] = pair.astype(ref.dtype)


def _swa_sink_rows(sink_ref, hk, G):
    return jnp.concatenate([jnp.broadcast_to(sink_ref[0:1, hk * G + g:hk * G + g + 1], (Q_BLOCK, 1)) for g in range(G)],
                           axis=0)


def _swa_attn_fwd(qk, proj, sinks, Hq, name):
    S = qk.shape[0]
    dh, G, QB = SWA_HEAD_DIM, SWA_GROUP, Q_BLOCK
    Hk = Hq // G
    Wq, Wk = Hq * dh, Hk * dh
    nb = S // QB
    scale = dh ** -0.5

    def body(q_ref, kp_ref, kc_ref, vp_ref, vc_ref, sink_ref, bias_ref, o_ref, lse_ref):
        bias = bias_ref[...]
        lane = lax.broadcasted_iota(jnp.int32, (QB, LANE), 1)
        lse_tile = jnp.zeros((QB, LANE), F32)
        for hk in range(Hk):
            kb = _swa_band(kp_ref, kc_ref, hk)
            vb = _swa_band(vp_ref, vc_ref, hk)
            q = (_swa_stack(q_ref, hk, G) * scale).astype(BF16)
            sk = _swa_sink_rows(sink_ref, hk, G)
            s = lax.dot_general(q, kb, (((1,), (1,)), ((), ())), preferred_element_type=F32) + bias
            m = jnp.maximum(jnp.max(s, axis=-1, keepdims=True), sk)
            p = jnp.exp(s - m)
            l = jnp.sum(p, axis=-1, keepdims=True) + jnp.exp(sk - m)
            o = jnp.dot(p.astype(BF16), vb, preferred_element_type=F32) / l
            lse = m + jnp.log(l)
            for g in range(G):
                lse_tile = jnp.where(lane == hk * G + g, lse[g * QB:(g + 1) * QB], lse_tile)
            _swa_unstack(o_ref, o, hk, G)
        lse_ref[...] = lse_tile

    kcol, vcol = Wq // Wk, (Wq + Wk) // Wk
    return pl.pallas_call(
        body, name=name, grid=(nb,),
        in_specs=[pl.BlockSpec((QB, Wq), lambda n: (n, 0)),
                  pl.BlockSpec((QB, Wk), lambda n: (jnp.maximum(n - 1, 0), kcol)),
                  pl.BlockSpec((QB, Wk), lambda n: (n, kcol)),
                  pl.BlockSpec((QB, Wk), lambda n: (jnp.maximum(n - 1, 0), vcol)),
                  pl.BlockSpec((QB, Wk), lambda n: (n, vcol)),
                  pl.BlockSpec((1, LANE), lambda n: (0, 0)),
                  pl.BlockSpec((None, G * QB, 2 * QB), lambda n: (jnp.minimum(n, 1), 0, 0))],
        out_specs=[pl.BlockSpec((QB, Wq), lambda n: (n, 0)), pl.BlockSpec((QB, LANE), lambda n: (n, 0))],
        out_shape=[jax.ShapeDtypeStruct((S, Wq), BF16), jax.ShapeDtypeStruct((S, LANE), F32)],
        compiler_params=_params(("parallel",)),
    )(qk, qk, qk, proj, proj, sinks, _swa_bias(G))


def _swa_attn_bwd(qk, proj, sinks, do, lse, Hq, after, name):
    S = qk.shape[0]
    dh, G, QB = SWA_HEAD_DIM, SWA_GROUP, Q_BLOCK
    Hk = Hq // G
    Wq, Wk = Hq * dh, Hk * dh
    nb = S // QB
    scale = dh ** -0.5

    def body(q_ref, kp_ref, kc_ref, vp_ref, vc_ref, sink_ref, do_ref, lse_ref, bias_ref, after_ref,
             dq_ref, dk_ref, dv_ref, dsink_ref, carry_k, carry_v):
        n = pl.program_id(0)

        @pl.when(n == 0)
        def _():
            dsink_ref[...] = jnp.zeros_like(dsink_ref)

        @pl.when(n < nb)
        def _():
            bias = bias_ref[...]
            lane = lax.broadcasted_iota(jnp.int32, (1, LANE), 1)
            dsink = jnp.zeros((1, LANE), F32)
            dk_heads, dv_heads = [], []
            for hk in range(Hk):
                kb = _swa_band(kp_ref, kc_ref, hk)
                vb = _swa_band(vp_ref, vc_ref, hk)
                q = (_swa_stack(q_ref, hk, G) * scale).astype(BF16)
                do_s = _swa_stack(do_ref, hk, G).astype(BF16)
                lse = jnp.concatenate([lse_ref[:, hk * G + g:hk * G + g + 1] for g in range(G)], axis=0)
                s = lax.dot_general(q, kb, (((1,), (1,)), ((), ())), preferred_element_type=F32) + bias
                p = jnp.exp(s - lse)
                p_sink = jnp.exp(_swa_sink_rows(sink_ref, hk, G) - lse)
                dp = lax.dot_general(do_s, vb, (((1,), (1,)), ((), ())), preferred_element_type=F32)
                delta = jnp.sum(p * dp, axis=-1, keepdims=True)
                ds_b = (p * (dp - delta)).astype(BF16)
                _swa_unstack(dq_ref, jnp.dot(ds_b, kb, preferred_element_type=F32) * scale, hk, G)
                dk_heads.append(lax.dot_general(ds_b, q, (((0,), (0,)), ((), ())), preferred_element_type=F32))
                dv_heads.append(lax.dot_general(p.astype(BF16), do_s, (((0,), (0,)), ((), ())), preferred_element_type=F32))
                sink_term = p_sink * delta
                for g in range(G):
                    dsink = jnp.where(lane == hk * G + g,
                                      -jnp.sum(sink_term[g * QB:(g + 1) * QB], axis=0, keepdims=True), dsink)
            dsink_ref[...] += dsink
            dk_all = jnp.concatenate(dk_heads, axis=1)
            dv_all = jnp.concatenate(dv_heads, axis=1)

            @pl.when(n > 0)
            def _():
                dk_ref[...] = carry_k[...] + dk_all[:QB]
                dv_ref[...] = carry_v[...] + dv_all[:QB]

            carry_k[...] = dk_all[QB:]
            carry_v[...] = dv_all[QB:]

        @pl.when(n == nb)
        def _():
            dk_ref[...] = carry_k[...]
            dv_ref[...] = carry_v[...]

    kcol, vcol = Wq // Wk, (Wq + Wk) // Wk
    cur = lambda n: jnp.minimum(n, nb - 1)
    prev = lambda n: jnp.maximum(jnp.minimum(n, nb - 1) - 1, 0)
    return pl.pallas_call(
        body, name=name, grid=(nb + 1,),
        in_specs=[pl.BlockSpec((QB, Wq), lambda n: (cur(n), 0)),
                  pl.BlockSpec((QB, Wk), lambda n: (prev(n), kcol)),
                  pl.BlockSpec((QB, Wk), lambda n: (cur(n), kcol)),
                  pl.BlockSpec((QB, Wk), lambda n: (prev(n), vcol)),
                  pl.BlockSpec((QB, Wk), lambda n: (cur(n), vcol)),
                  pl.BlockSpec((1, LANE), lambda n: (0, 0)),
                  pl.BlockSpec((QB, Wq), lambda n: (cur(n), 0)),
                  pl.BlockSpec((QB, LANE), lambda n: (cur(n), 0)),
                  pl.BlockSpec((None, G * QB, 2 * QB), lambda n: (jnp.minimum(n, 1), 0, 0)),
                  pl.BlockSpec(memory_space=pl.ANY)],
        out_specs=[pl.BlockSpec((QB, Wq), lambda n: (cur(n), 0)),
                   pl.BlockSpec((QB, Wk), lambda n: (jnp.maximum(n - 1, 0), 0)),
                   pl.BlockSpec((QB, Wk), lambda n: (jnp.maximum(n - 1, 0), 0)),
                   pl.BlockSpec((1, LANE), lambda n: (0, 0))],
        out_shape=[jax.ShapeDtypeStruct((S, Wq), F32), jax.ShapeDtypeStruct((S, Wk), F32),
                   jax.ShapeDtypeStruct((S, Wk), F32), jax.ShapeDtypeStruct((1, LANE), F32)],
        scratch_shapes=[pltpu.VMEM((QB, Wk), F32), pltpu.VMEM((QB, Wk), F32)],
        compiler_params=_params(("arbitrary",)),
    )(qk, qk, qk, proj, proj, sinks, do, lse, _swa_bias(G), after)


ADA_ROWS = 16


def _ada_mod(c_pad, w, b, name):
    L, D, N = w.shape
    tn = _tile(N, 512)

    def body(c_ref, w_ref, b_ref, o_ref):
        c = c_ref[...]
        c = c * _sigmoid(c)
        ch = c.astype(BF16)
        cl = (c - ch.astype(F32)).astype(BF16)
        ww = w_ref[...]
        wh = ww.astype(BF16)
        wl = (ww - wh.astype(F32)).astype(BF16)
        acc = jnp.dot(ch, wh, preferred_element_type=F32)
        acc += jnp.dot(ch, wl, preferred_element_type=F32)
        acc += jnp.dot(cl, wh, preferred_element_type=F32)
        o_ref[...] = acc + b_ref[...]

    return pl.pallas_call(
        body, name=name, grid=(L, N // tn),
        in_specs=[pl.BlockSpec((ADA_ROWS, D), lambda l, j: (0, 0)),
                  pl.BlockSpec((None, D, tn), lambda l, j: (l, 0, j)),
                  pl.BlockSpec((None, 1, tn), lambda l, j: (l, 0, j))],
        out_specs=pl.BlockSpec((None, ADA_ROWS, tn), lambda l, j: (l, 0, j)),
        out_shape=jax.ShapeDtypeStruct((L, ADA_ROWS, N), F32),
        compiler_params=_params(("parallel", "parallel")),
    )(c_pad, w, b)


def _ada_bwd(cT, dm, name):
    D = cT.shape[0]
    L, B, N = dm.shape
    tm = _tile(D, 256)

    def body(c_ref, dm_ref, o_ref):
        c = c_ref[...]
        c = c * _sigmoid(c)
        dmv = dm_ref[...]
        acc = c[:, 0:1] * dmv[0:1, :]
        for b in range(1, B):
            acc += c[:, b:b + 1] * dmv[b:b + 1, :]
        o_ref[...] = acc

    return pl.pallas_call(
        body, name=name, grid=(L, D // tm),
        in_specs=[pl.BlockSpec((tm, B), lambda l, i: (i, 0)), pl.BlockSpec((None, B, N), lambda l, i: (l, 0, 0))],
        out_specs=pl.BlockSpec((None, tm, N), lambda l, i: (l, i, 0)),
        out_shape=jax.ShapeDtypeStruct((L, D, N), F32),
        compiler_params=_params(("parallel", "parallel")),
    )(cT, dm)


def _adamw_math(w, g, m, v):
    m = ADAM_B1 * m + (1.0 - ADAM_B1) * g
    v = ADAM_B2 * v + (1.0 - ADAM_B2) * (g * g)
    m_hat = m / (1.0 - ADAM_B1 ** ADAM_STEP)
    v_hat = v / (1.0 - ADAM_B2 ** ADAM_STEP)
    delta = -ADAM_LR * (m_hat / (jnp.sqrt(v_hat) + ADAM_EPS) + ADAM_WD * w)
    return delta, m, v


def _adam_rows(R, C):
    lanes = -(-C // LANE) * LANE
    return _tile(R, max(8, (262144 // lanes) // 8 * 8), 8)


def _adam_sum(recv, w, m, v, layer, filled, name, by_cols=False):
    P, R, C = recv.shape
    L = w.shape[0]
    n_keep = 0 if filled is None else 4
    if by_cols:
        tc = _tile(C, 256)
        grid, spec = (C // tc,), pl.BlockSpec((None, R, tc), lambda i: (layer, 0, i))
        recv_spec = pl.BlockSpec((P, R, tc), lambda i: (0, 0, i))
    else:
        tr = _adam_rows(R, C)
        grid, spec = (R // tr,), pl.BlockSpec((None, tr, C), lambda i: (layer, i, 0))
        recv_spec = pl.BlockSpec((P, tr, C), lambda i: (0, i, 0))

    def body(r_ref, w_ref, m_ref, v_ref, *rest):
        g_ref, d_ref, mo_ref, vo_ref = rest[n_keep:]
        g = r_ref[0].astype(F32)
        for p in range(1, P):
            g = g + r_ref[p].astype(F32)
        delta, mn, vn = _adamw_math(w_ref[...], g, m_ref[...], v_ref[...])
        g_ref[...] = g
        d_ref[...] = delta
        mo_ref[...] = mn
        vo_ref[...] = vn

    return pl.pallas_call(
        body, name=name, grid=grid,
        in_specs=[recv_spec, spec, spec, spec] + [pl.BlockSpec(memory_space=pl.ANY)] * n_keep,
        out_specs=[spec] * 4,
        out_shape=[jax.ShapeDtypeStruct((L, R, C), F32)] * 4,
        input_output_aliases={4 + k: k for k in range(n_keep)},
        compiler_params=_params(("parallel",)),
    )(recv, w, m, v, *(filled or ()))


def _adam(g, w, m, v, name):
    L, R, C = w.shape
    tr = _adam_rows(R, C)

    def body(g_ref, w_ref, m_ref, v_ref, d_ref, mo_ref, vo_ref):
        delta, mn, vn = _adamw_math(w_ref[...], g_ref[...], m_ref[...], v_ref[...])
        d_ref[...] = delta
        mo_ref[...] = mn
        vo_ref[...] = vn

    spec = pl.BlockSpec((None, tr, C), lambda l, i: (l, i, 0))
    return pl.pallas_call(
        body, name=name, grid=(L, R // tr),
        in_specs=[spec] * 4, out_specs=[spec] * 3,
        out_shape=[jax.ShapeDtypeStruct((L, R, C), F32)] * 3,
        compiler_params=_params(("parallel", "parallel")),
    )(g, w, m, v)


def _sum_slots(x, after, name):
    P, R, C = x.shape

    def body(x_ref, after_ref, o_ref):
        acc = x_ref[0]
        for p in range(1, P):
            acc = acc + x_ref[p]
        o_ref[...] = acc

    return pl.pallas_call(
        body, name=name,
        in_specs=[pl.BlockSpec(memory_space=pltpu.VMEM), pl.BlockSpec(memory_space=pl.ANY)],
        out_specs=pl.BlockSpec(memory_space=pltpu.VMEM),
        out_shape=jax.ShapeDtypeStruct((R, C), F32),
        compiler_params=pltpu.CompilerParams(vmem_limit_bytes=48 * MIB),
    )(x, after)


def _my_pos():
    return lax.axis_index("x"), lax.axis_index("y"), lax.axis_index("c")


def _all_gather_small(x, name, after=()):
    R, C = x.shape
    n_after = len(after)

    def body(x_ref, *rest):
        out_ref, send_sems, recv_sems = rest[n_after:]
        x_, y_, c_ = _my_pos()
        me, sibling = (x_, y_, c_), (x_, y_, 1 - c_)
        chips = [(1 - x_, y_), (x_, 1 - y_), (1 - x_, 1 - y_)]

        def slot(px, py, pc):
            return out_ref.at[4 * px + 2 * py + pc]

        def copy(k, block, to):
            return pltpu.make_async_remote_copy(
                src_ref=slot(*block), dst_ref=slot(*block), send_sem=send_sems.at[k], recv_sem=recv_sems.at[k],
                device_id=to, device_id_type=MESH)

        out_ref[4 * x_ + 2 * y_ + c_] = x_ref[...]
        first = [copy(0, me, sibling)] + [copy(1 + j, me, (*chip, c_)) for j, chip in enumerate(chips)]
        for cp in first:
            cp.start()
        passed = [copy(4 + j, (*chip, c_), sibling) for j, chip in enumerate(chips)]
        for j, chip in enumerate(chips):
            copy(1 + j, (*chip, c_), me).wait_recv()
            passed[j].start()
        copy(0, sibling, me).wait_recv()
        for j, chip in enumerate(chips):
            copy(4 + j, (*chip, 1 - c_), me).wait_recv()
        for cp in first + passed:
            cp.wait_send()

    return pl.pallas_call(
        body, name=name,
        in_specs=[pl.BlockSpec(memory_space=pltpu.VMEM)] + [pl.BlockSpec(memory_space=pl.ANY)] * n_after,
        out_specs=pl.BlockSpec(memory_space=pltpu.VMEM),
        out_shape=jax.ShapeDtypeStruct((N_DEV, R, C), x.dtype),
        scratch_shapes=[pltpu.SemaphoreType.DMA((7,)), pltpu.SemaphoreType.DMA((7,))],
        compiler_params=pltpu.CompilerParams(vmem_limit_bytes=48 * MIB),
    )(x, *after)


HBM_SPEC = pl.BlockSpec(memory_space=pltpu.HBM)
SEM_SPEC = pl.BlockSpec(memory_space=pltpu.SEMAPHORE)
ANY_SPEC = pl.BlockSpec(memory_space=pl.ANY)
SPLIT_EFFECT = pltpu.SideEffectType.DATAFLOW_SIDE_EFFECTING


def _in_hbm(a):
    return pltpu.with_memory_space_constraint(a, pltpu.HBM)


def _gathered_shape(a, kind):
    if kind == "major":
        return (N_DEV,) + a.shape
    if kind == "rows":
        return (N_DEV * a.shape[0], a.shape[1])
    return (2, a.shape[0], 4 * a.shape[1])


def _gather_slot(ref, kind, block, shard_shape):
    px, py, pc = block
    if kind == "major":
        return ref.at[4 * px + 2 * py + pc]
    if kind == "rows":
        r = shard_shape[0]
        return ref.at[pl.ds(pl.multiple_of((4 * px + 2 * py + pc) * r, r), r), :]
    cu = shard_shape[1]
    return ref.at[px, :, pl.ds(pl.multiple_of((2 * py + pc) * cu, cu), cu)]


def _gather_peers():
    x_, y_, c_ = _my_pos()
    return (x_, y_, c_), (x_, y_, 1 - c_), [(1 - x_, y_), (x_, 1 - y_), (1 - x_, 1 - y_)]


def _gather_start(shards, kinds, after, name):
    n = len(shards)
    bufs = [lax.empty(_gathered_shape(a, k), a.dtype) for a, k in zip(shards, kinds)]
    extra = [] if after is None else [after]

    def body(*refs):
        shard_refs, buf_refs = refs[:n], refs[n:2 * n]
        send_sems, recv_sems, local_sems = refs[2 * n + len(extra):2 * n + len(extra) + 3]
        token = refs[-1]
        me, sibling, chips = _gather_peers()
        for e in range(n):
            mine = _gather_slot(buf_refs[e], kinds[e], me, shards[e].shape)
            pltpu.make_async_copy(shard_refs[e], mine, local_sems.at[e]).start()
            for k, to in enumerate([sibling] + [(*chip, me[2]) for chip in chips]):
                pltpu.make_async_remote_copy(
                    src_ref=shard_refs[e], dst_ref=mine, send_sem=send_sems.at[4 * e + k],
                    recv_sem=recv_sems.at[4 * e + k], device_id=to, device_id_type=MESH).start()
        token[...] = jnp.zeros_like(token)

    out = pl.pallas_call(
        body, name=name,
        out_shape=(pltpu.SemaphoreType.DMA((4 * n,)), pltpu.SemaphoreType.DMA((4 * n,)), pltpu.SemaphoreType.DMA((n,)),
                   *[pltpu.HBM(a.shape, a.dtype) for a in shards], *[pltpu.HBM(a.shape, a.dtype) for a in bufs],
                   jax.ShapeDtypeStruct((8, LANE), F32)),
        in_specs=[HBM_SPEC] * (2 * n) + [ANY_SPEC] * len(extra),
        out_specs=(SEM_SPEC, SEM_SPEC, SEM_SPEC, *[HBM_SPEC] * (2 * n), pl.BlockSpec(memory_space=pltpu.VMEM)),
        input_output_aliases={i: 3 + i for i in range(2 * n)},
        compiler_params=pltpu.CompilerParams(has_side_effects=SPLIT_EFFECT),
    )(*[_in_hbm(a) for a in shards], *[_in_hbm(a) for a in bufs], *extra)
    return out[0], out[1], out[2], out[3:3 + n], out[3 + n:3 + 2 * n], out[-1]


def _gather_forward(recv_sems, bufs, kinds, shard_shapes, after, name):
    n = len(bufs)

    def body(*refs):
        buf_refs, recv_in = refs[:n], refs[n]
        fsend, frecv = refs[n + 2], refs[n + 3]
        token = refs[-1]
        me, sibling, chips = _gather_peers()
        for e in range(n):
            for j, chip in enumerate(chips):
                slot = _gather_slot(buf_refs[e], kinds[e], (*chip, me[2]), shard_shapes[e])
                pltpu.make_async_remote_copy(
                    src_ref=slot, dst_ref=slot, send_sem=recv_in.at[4 * e + 1 + j], recv_sem=recv_in.at[4 * e + 1 + j],
                    device_id=me, device_id_type=MESH).wait_recv()
                pltpu.make_async_remote_copy(
                    src_ref=slot, dst_ref=slot, send_sem=fsend.at[3 * e + j], recv_sem=frecv.at[3 * e + j],
                    device_id=sibling, device_id_type=MESH).start()
        token[...] = jnp.zeros_like(token)

    out = pl.pallas_call(
        body, name=name,
        out_shape=(pltpu.SemaphoreType.DMA((3 * n,)), pltpu.SemaphoreType.DMA((3 * n,)),
                   *[pltpu.HBM(a.shape, a.dtype) for a in bufs], jax.ShapeDtypeStruct((8, LANE), F32)),
        in_specs=[HBM_SPEC] * n + [SEM_SPEC, ANY_SPEC],
        out_specs=(SEM_SPEC, SEM_SPEC, *[HBM_SPEC] * n, pl.BlockSpec(memory_space=pltpu.VMEM)),
        input_output_aliases={i: 2 + i for i in range(n)},
        compiler_params=pltpu.CompilerParams(has_side_effects=SPLIT_EFFECT),
    )(*bufs, recv_sems, after)
    return out[0], out[1], out[2:2 + n], out[-1]


def _gather_wait(send_sems, recv_sems, local_sems, fsend, frecv, shards, bufs, kinds, after, name):
    n = len(bufs)

    def body(*refs):
        shard_refs, buf_refs = refs[:n], refs[n:2 * n]
        send_in, recv_in, local_in, fsend_in, frecv_in = refs[2 * n:2 * n + 5]
        me, sibling, chips = _gather_peers()

        def arrival(slot, sem):
            return pltpu.make_async_remote_copy(src_ref=slot, dst_ref=slot, send_sem=sem, recv_sem=sem,
                                                device_id=me, device_id_type=MESH)

        for e in range(n):
            shape = shards[e].shape
            mine = _gather_slot(buf_refs[e], kinds[e], me, shape)
            pltpu.make_async_copy(shard_refs[e], mine, local_in.at[e]).wait()
            arrival(_gather_slot(buf_refs[e], kinds[e], sibling, shape), recv_in.at[4 * e]).wait_recv()
            for j, chip in enumerate(chips):
                arrival(_gather_slot(buf_refs[e], kinds[e], (*chip, 1 - me[2]), shape), frecv_in.at[3 * e + j]).wait_recv()
            for k in range(4):
                arrival(mine, send_in.at[4 * e + k]).wait_send()
            for j in range(3):
                arrival(mine, fsend_in.at[3 * e + j]).wait_send()

    out = pl.pallas_call(
        body, name=name,
        out_shape=(*[pltpu.HBM(a.shape, a.dtype) for a in shards], *[pltpu.HBM(a.shape, a.dtype) for a in bufs]),
        in_specs=[HBM_SPEC] * (2 * n) + [SEM_SPEC] * 5 + [ANY_SPEC],
        out_specs=tuple([HBM_SPEC] * (2 * n)),
        input_output_aliases={i: i for i in range(2 * n)},
        compiler_params=pltpu.CompilerParams(has_side_effects=SPLIT_EFFECT),
    )(*shards, *bufs, send_sems, recv_sems, local_sems, fsend, frecv, after)
    return out[n:]


def _grad_slice(ref, kind, j):
    if kind == "whole":
        return ref
    if kind == "major":
        return ref.at[j]
    if kind == "rows":
        r = ref.shape[0] // N_DEV
        return ref.at[pl.ds(j * r, r), :]
    cu = ref.shape[2] // 4
    return ref.at[j // 4, :, pl.ds((j % 4) * cu, cu)]


def _slice_shape(a, kind):
    if kind == "whole":
        return a.shape
    if kind == "major":
        return a.shape[1:]
    if kind == "rows":
        return (a.shape[0] // N_DEV, a.shape[1])
    return (a.shape[1], a.shape[2] // 4)


def _scatter_copies(srcs, lands, kinds, send_sems, recv_sems):
    x_, y_, c_ = _my_pos()
    me = 4 * x_ + 2 * y_ + c_
    n = len(srcs)

    def remote(e, j):
        return pltpu.make_async_remote_copy(
            src_ref=_grad_slice(srcs[e], kinds[e], j), dst_ref=lands[e].at[me],
            send_sem=send_sems.at[e * N_DEV + j], recv_sem=recv_sems.at[e * N_DEV + me],
            device_id=(j // 4, (j // 2) % 2, j % 2), device_id_type=MESH)

    def local(e, j):
        return pltpu.make_async_copy(_grad_slice(srcs[e], kinds[e], j), lands[e].at[j], recv_sems.at[e * N_DEV + j])

    def arrival(e, i):
        return pltpu.make_async_remote_copy(
            src_ref=_grad_slice(srcs[e], kinds[e], i), dst_ref=lands[e].at[i],
            send_sem=send_sems.at[e * N_DEV + i], recv_sem=recv_sems.at[e * N_DEV + i],
            device_id=(i // 4, (i // 2) % 2, i % 2), device_id_type=MESH)

    def start():
        for e in range(n):
            for j in range(N_DEV):
                @pl.when(me == j)
                def _():
                    local(e, j).start()

                @pl.when(me != j)
                def _():
                    remote(e, j).start()

    def wait():
        for e in range(n):
            for i in range(N_DEV):
                @pl.when(me == i)
                def _():
                    local(e, i).wait()

                @pl.when(me != i)
                def _():
                    arrival(e, i).wait_recv()
        for e in range(n):
            for j in range(N_DEV):
                @pl.when(me != j)
                def _():
                    remote(e, j).wait_send()

    return start, wait


def _scatter_start(srcs, kinds, after, name):
    n = len(srcs)
    lands = [lax.empty((N_DEV,) + _slice_shape(a, k), a.dtype) for a, k in zip(srcs, kinds)]
    extra = [] if after is None else [after]

    def body(*refs):
        src_refs, land_refs = refs[:n], refs[n:2 * n]
        send_sems, recv_sems = refs[2 * n + len(extra)], refs[2 * n + len(extra) + 1]
        token = refs[-1]
        start, _ = _scatter_copies(src_refs, land_refs, kinds, send_sems, recv_sems)
        start()
        token[...] = jnp.zeros_like(token)

    out = pl.pallas_call(
        body, name=name,
        out_shape=(pltpu.SemaphoreType.DMA((n * N_DEV,)), pltpu.SemaphoreType.DMA((n * N_DEV,)),
                   *[pltpu.HBM(a.shape, a.dtype) for a in srcs], *[pltpu.HBM(a.shape, a.dtype) for a in lands],
                   jax.ShapeDtypeStruct((8, LANE), F32)),
        in_specs=[HBM_SPEC] * (2 * n) + [ANY_SPEC] * len(extra),
        out_specs=(SEM_SPEC, SEM_SPEC, *[HBM_SPEC] * (2 * n), pl.BlockSpec(memory_space=pltpu.VMEM)),
        input_output_aliases={i: 2 + i for i in range(2 * n)},
        compiler_params=pltpu.CompilerParams(has_side_effects=SPLIT_EFFECT),
    )(*[_in_hbm(a) for a in srcs], *[_in_hbm(a) for a in lands], *extra)
    return out[0], out[1], out[2:2 + n], out[2 + n:2 + 2 * n], out[-1]


def _scatter_wait(send_sems, recv_sems, srcs, lands, kinds, after, name):
    n = len(srcs)

    def body(*refs):
        src_refs, land_refs = refs[:n], refs[n:2 * n]
        _, wait = _scatter_copies(src_refs, land_refs, kinds, refs[2 * n], refs[2 * n + 1])
        wait()

    out = pl.pallas_call(
        body, name=name,
        out_shape=(*[pltpu.HBM(a.shape, a.dtype) for a in srcs], *[pltpu.HBM(a.shape, a.dtype) for a in lands]),
        in_specs=[HBM_SPEC] * (2 * n) + [SEM_SPEC, SEM_SPEC] + [ANY_SPEC] * len(after),
        out_specs=tuple([HBM_SPEC] * (2 * n)),
        input_output_aliases={i: i for i in range(2 * n)},
        compiler_params=pltpu.CompilerParams(has_side_effects=SPLIT_EFFECT),
    )(*srcs, *lands, send_sems, recv_sems, *after)
    return out[n:]


def _rope_tables(positions, sign):
    half = ROPE_DIM // 2
    inv_freq = ROPE_THETA ** (-jnp.arange(0, ROPE_DIM, 2, dtype=F32) / ROPE_DIM)
    ang = positions.astype(F32)[:, None] * inv_freq
    reps = LANE // half
    cos = jnp.tile(jnp.cos(ang), (1, reps))
    sin = jnp.tile(jnp.sin(ang), (1, reps)) * sign
    d = jnp.arange(LANE) % SWA_HEAD_DIM
    return (jnp.where(d < ROPE_DIM, cos, 1.0), jnp.where(d < half, -sin, 0.0),
            jnp.where((d >= half) & (d < ROPE_DIM), sin, 0.0))


def _pad_cols(a, n):
    return jnp.pad(a, ((0, 0), (0, n - a.shape[1])))


def _local_step(x, target, positions, mods, fetch, P, on_grads, on_small):
    S, D = x.shape
    Hf = D // FOX_HEAD_DIM
    Hq = D // SWA_HEAD_DIM
    Hk = Hq // SWA_GROUP
    Wk = Hk * SWA_HEAD_DIM
    n_in = 3 * D + Hf
    (sh1a, sc1a, g1a, sh2a, sc2a, g2a), (sh1b, sc1b, g1b, sh2b, sc2b, g2b) = mods
    row = lambda v: v.reshape(1, -1)
    cw = [jnp.transpose(P["conv_w"][l].reshape(3, 2, -1), (1, 0, 2)) for l in range(2)]
    cb = [P["conv_b"][l].reshape(2, 1, -1) for l in range(2)]

    W = dict(up=[None, None], down=[None, None])
    h1a = _modulate(x, sc1a, sh1a, "modulate_in")
    W["fox_in"], W["fox_o"] = fetch("fox", "wait", h1a)
    proj_a = _mm_nn(h1a, W["fox_in"], F32, "fox_in_proj", tn=896)
    flT = proj_a[:, 3 * D:n_in].T
    bf_col = P["fox_b_f"].reshape(Hf, 1)
    cumT = _fox_prep(flT, bf_col, "fox_cumsum")
    cum_row = cumT.reshape(Hf, 1, S)
    o_a, lse_a = _fox_attn_fwd(proj_a, cum_row, Hf, "fox_attn_fwd")
    token = fetch("ffn0", "forward", o_a)
    y1a = _mm_nn(o_a, W["fox_o"], F32, "fox_out_proj")
    z1a, x1, h2a = _ln_fwd(x, y1a, g1a, row(P["ln_mix_g"][0]), row(P["ln_mix_b"][0]), sc2a, sh2a, token, "ln_mix0")
    W["up"][0], W["down"][0] = fetch("ffn0", "wait", h2a)
    u_a, uc_a, a_a = _ffn_up(h2a, W["up"][0], cw[0], cb[0], "ffn_up0")
    token = fetch("swa", "forward", a_a)
    y2a = _mm_nn(a_a, W["down"][0], F32, "ffn_down0", tk=1408)
    z2a, x2, h1b = _ln_fwd(x1, y2a, g2a, row(P["ln_ffn_g"][0]), row(P["ln_ffn_b"][0]), sc1b, sh1b, token, "ln_ffn0")

    W["swa_in"], W["swa_o"] = fetch("swa", "wait", h1b)
    proj_b = _mm_nn(h1b, W["swa_in"], F32, "swa_in_proj")
    tabs_f = _rope_tables(positions, 1.0)
    tabs_b = _rope_tables(positions, -1.0)
    qk = _rope([(proj_b, True, D + Wk)], tabs_f, F32, "rope_fwd")
    sinks = _pad_cols(P["swa_sinks"].reshape(1, Hq), LANE)
    o_b, lse_b = _swa_attn_fwd(qk, proj_b, sinks, Hq, "swa_attn_fwd")
    token = fetch("ffn1", "forward", o_b)
    y1b = _mm_nn(o_b, W["swa_o"], F32, "swa_out_proj")
    z1b, x3, h2b = _ln_fwd(x2, y1b, g1b, row(P["ln_mix_g"][1]), row(P["ln_mix_b"][1]), sc2b, sh2b, token, "ln_mix1")
    W["up"][1], W["down"][1] = fetch("ffn1", "wait", h2b)
    u_b, uc_b, a_b = _ffn_up(h2b, W["up"][1], cw[1], cb[1], "ffn_up1")
    y2b = _mm_nn(a_b, W["down"][1], F32, "ffn_down1", tk=1408)
    z2b, dout, loss_row = _ln_fwd_loss(x3, y2b, g2b, row(P["ln_ffn_g"][1]), row(P["ln_ffn_b"][1]), target, "ln_ffn1_loss")

    def ffn_backward(dy, a, u, uc, h_in, l, tag):
        d_down = _mm_tn(a, dy[None], BF16, "ffn_dwdown" + tag, tm=1408, tn=1024)[0]
        token = on_grads("ffn_w_down" + tag, d_down)
        da = _mm_nt(dy[None], W["down"][l][None], F32, "ffn_da" + tag)
        du, dcw, dcb = _ffn_bwd_elem(da, u, uc, cw[l], token, "ffn_bwd_elem" + tag)
        d_up = _mm_tn(h_in, du, BF16, "ffn_dwup" + tag)
        token = on_grads("ffn_w_up" + tag, d_up)
        dh = _mm_nt(du, W["up"][l], F32, "ffn_dh" + tag, tn=1024, tk=1408)
        return dh, token, jnp.transpose(dcw, (1, 0, 2)).reshape(3, -1), dcb.reshape(-1)

    dz2b, dy2b, dg_f1, db_f1, dgate2b = _ln_bwd(dout, z2b, y2b, g2b, row(P["ln_ffn_g"][1]), "ln_ffn1_bwd")
    dh2b, token, dcw1, dcb1 = ffn_backward(dy2b, a_b, u_b, uc_b, h2b, 1, "1")
    dx3, dsc2b, dsh2b = _mod_bwd(dz2b, dh2b, x3, sc2b, token, "mod_ffn1_bwd")

    dz1b, dy1b, dg_m1, db_m1, dgate1b = _ln_bwd(dx3, z1b, y1b, g1b, row(P["ln_mix_g"][1]), "ln_mix1_bwd")
    token = on_grads("swa_w_o", _mm_tn(o_b, dy1b[None], BF16, "swa_dwo")[0])
    do_b = _mm_nt(dy1b[None], W["swa_o"][None], F32, "swa_do")
    dq_b, dk_b, dv_b, dsinks = _swa_attn_bwd(qk, proj_b, sinks, do_b, lse_b, Hq, token, "swa_attn_bwd")
    dproj_b = _rope([(dq_b, True, D), (dk_b, True, Wk), (dv_b, False, Wk)], tabs_b, BF16, "rope_bwd")
    token = on_grads("swa_w_in", _mm_tn(dproj_b, h1b[None], BF16, "swa_dwin", tm=1280, tn=1024)[0])
    dh1b = _mm_nt(dproj_b[None], W["swa_in"][None], F32, "swa_dh", tk=1280)
    dx2, dsc1b, dsh1b = _mod_bwd(dz1b, dh1b, x2, sc1b, token, "mod_mix1_bwd")

    dz2a, dy2a, dg_f0, db_f0, dgate2a = _ln_bwd(dx2, z2a, y2a, g2a, row(P["ln_ffn_g"][0]), "ln_ffn0_bwd")
    dh2a, token, dcw0, dcb0 = ffn_backward(dy2a, a_a, u_a, uc_a, h2a, 0, "0")
    dx1, dsc2a, dsh2a = _mod_bwd(dz2a, dh2a, x1, sc2a, token, "mod_ffn0_bwd")

    dz1a, dy1a, dg_m0, db_m0, dgate1a = _ln_bwd(dx1, z1a, y1a, g1a, row(P["ln_mix_g"][0]), "ln_mix0_bwd")
    token = on_grads("fox_w_o", _mm_tn(o_a, dy1a[None], BF16, "fox_dwo")[0])
    do_a = _mm_nt(dy1a[None], W["fox_o"][None], F32, "fox_do")
    dq_a, dk_a, dv_a, dcum_row, dcum_col = _fox_attn_bwd(proj_a, do_a, o_a, cum_row, lse_a, Hf, token, "fox_attn_bwd")
    dflT, dbf = _fox_prep_bwd(dcum_row.reshape(Hf, S), dcum_col.reshape(Hf, S), flT, bf_col, "fox_cumsum_bwd")
    n_pad = W["fox_in"].shape[1]
    dproj_a = jnp.concatenate([dq_a, dk_a.astype(BF16), dv_a.astype(BF16),
                               _pad_cols(dflT.T, n_pad - 3 * D).astype(BF16)], axis=1)
    dh1a = _mm_nt(dproj_a[None], W["fox_in"][None], F32, "fox_dh", tk=896)
    grad_x, dsc1a, dsh1a = _mod_bwd(dz1a, dh1a, x, sc1a, token, "mod_mix0_bwd")

    dmod = jnp.stack([jnp.concatenate([dsh1a, dsc1a, dgate1a, dsh2a, dsc2a, dgate2a], axis=1)[0],
                      jnp.concatenate([dsh1b, dsc1b, dgate1b, dsh2b, dsc2b, dgate2b], axis=1)[0]])
    small = dict(dmod=dmod, conv_b=jnp.stack([dcb0, dcb1]), conv_w=jnp.stack([dcw0, dcw1]),
                 ln_mix_g=jnp.concatenate([dg_m0, dg_m1]), ln_mix_b=jnp.concatenate([db_m0, db_m1]),
                 ln_ffn_g=jnp.concatenate([dg_f0, dg_f1]), ln_ffn_b=jnp.concatenate([db_f0, db_f1]),
                 fox_b_f=dbf.reshape(-1), swa_sinks=dsinks[0, :Hq], loss=loss_row[0, 0].reshape(1))
    exchanged = on_small(small)
    on_grads("fox_w_in", _mm_tn(dproj_a, h1a[None], BF16, "fox_dwin", tm=896, tn=1024)[0], exchanged)
    return grad_x


SMALL_ORDER = ("dmod", "conv_b", "conv_w", "ln_mix_g", "ln_mix_b", "ln_ffn_g", "ln_ffn_b", "fox_b_f", "swa_sinks", "loss")


def _pack_rows(arrays):
    chunks, spans, off = [], [], 0
    for a in arrays:
        flat = a.reshape(-1)
        n = -(-flat.shape[0] // LANE) * LANE
        chunks.append(jnp.pad(flat, (0, n - flat.shape[0])))
        spans.append((off, flat.shape[0], a.shape))
        off += n
    total = -(-off // (8 * LANE)) * (8 * LANE)
    chunks.append(jnp.zeros((total - off,), F32))
    return jnp.concatenate(chunks).reshape(-1, LANE), spans


def _unpack_rows(packed, spans):
    flat = packed.reshape(-1)
    return [flat[off:off + n].reshape(shape) for off, n, shape in spans]


def kernel(x, c, positions, fox_w_in, fox_b_f, fox_w_o, swa_w_in, swa_sinks, swa_w_o, ada_w, ada_b, ffn_w_up, ffn_conv_w, ffn_conv_b, ffn_w_down, ln_mix_g, ln_mix_b, ln_ffn_g, ln_ffn_b, loss_target, m_fox_w_in, m_fox_b_f, m_fox_w_o, m_swa_w_in, m_swa_sinks, m_swa_w_o, m_ada_w, m_ada_b, m_ffn_w_up, m_ffn_conv_w, m_ffn_conv_b, m_ffn_w_down, m_ln_mix_g, m_ln_mix_b, m_ln_ffn_g, m_ln_ffn_b, v_fox_w_in, v_fox_b_f, v_fox_w_o, v_swa_w_in, v_swa_sinks, v_swa_w_o, v_ada_w, v_ada_b, v_ffn_w_up, v_ffn_conv_w, v_ffn_conv_b, v_ffn_w_down, v_ln_mix_g, v_ln_mix_b, v_ln_ffn_g, v_ln_ffn_b):
    S, D = x.shape[1], x.shape[2]
    L = ada_w.shape[0]
    me = 4 * lax.axis_index("x") + 2 * lax.axis_index("y") + lax.axis_index("c")
    n_ada = ada_w.shape[2]
    cu = ffn_w_up.shape[2]
    F = 4 * cu
    n_in = fox_w_in.shape[2] * N_DEV
    n_in_pad = -(-n_in // LANE) * LANE

    gather_groups = dict(
        fox=([fox_w_in[0].astype(BF16), fox_w_o[0].astype(BF16)], ["major", "rows"]),
        ffn0=([ffn_w_up[0].astype(BF16), ffn_w_down[0].astype(BF16)], ["halves", "rows"]),
        swa=([swa_w_in[0].astype(BF16), swa_w_o[0].astype(BF16)], ["major", "rows"]),
        ffn1=([ffn_w_up[1].astype(BF16), ffn_w_down[1].astype(BF16)], ["halves", "rows"]))
    starts_after = dict(fox=["ffn0", "swa"], ffn0=["ffn1"])
    gathers = {}

    def start_group(group, after):
        shards, kinds = gather_groups[group]
        send, recv, local, thru, bufs, token = _gather_start(shards, kinds, after, "gather_start_" + group)
        gathers[group] = dict(send=send, recv=recv, local=local, shards=thru, bufs=bufs, kinds=kinds,
                              shapes=[a.shape for a in shards], token=token)
        return token

    fox_started = start_group("fox", None)

    c_all = _all_gather_small(c.reshape(-1, LANE), "gather_c", after=[fox_started]).reshape(N_DEV, D)
    b_cols = lax.dynamic_slice_in_dim(ada_b, me * n_ada, n_ada, axis=1).reshape(L, 1, n_ada)
    mod_blk = _ada_mod(jnp.pad(c_all, ((0, ADA_ROWS - N_DEV), (0, 0))), ada_w, b_cols, "ada_mod")[:, :N_DEV]
    mod_all = _all_gather_small(mod_blk.reshape(-1, LANE), "gather_mod").reshape(N_DEV, L, N_DEV, n_ada)
    mod_mine = lax.dynamic_index_in_dim(mod_all, me, axis=2, keepdims=False)
    mod_mine = jnp.transpose(mod_mine, (1, 0, 2)).reshape(L, N_DEV * n_ada)
    mods = [[mod_mine[l, k * D:(k + 1) * D].reshape(1, D) for k in range(6)] for l in range(L)]

    P = dict(fox_b_f=fox_b_f[0], swa_sinks=swa_sinks[0], conv_b=ffn_conv_b,
             ln_mix_g=ln_mix_g, ln_mix_b=ln_mix_b, ln_ffn_g=ln_ffn_g, ln_ffn_b=ln_ffn_b)
    cw_rows = _all_gather_small(_pack_rows([ffn_conv_w])[0], "gather_conv_w", after=[mod_all])
    n_cw = ffn_conv_w.size
    cw_dev = cw_rows.reshape(N_DEV, -1)[:, :n_cw].reshape(N_DEV, L, 3, cu)
    P["conv_w"] = jnp.transpose(cw_dev, (1, 2, 0, 3)).reshape(L, 3, N_DEV * cu)

    def natural(g, pad_to=None):
        slabs = [g[k] for k in range(N_DEV)]
        if pad_to is not None:
            slabs.append(jnp.zeros((D, pad_to - N_DEV * g.shape[2]), g.dtype))
        return jnp.concatenate(slabs, axis=1)

    def forward_stage(group, after):
        s = gathers[group]
        s["fsend"], s["frecv"], s["bufs"], token = _gather_forward(s["recv"], s["bufs"], s["kinds"], s["shapes"], after,
                                                                   "gather_forward_" + group)
        for nxt in starts_after.get(group, ()):
            token = start_group(nxt, token)
        return token

    def fetch(group, stage, after):
        if stage == "forward":
            return forward_stage(group, after)
        if group == "fox":
            after = forward_stage(group, cw_rows)
        s = gathers.pop(group)
        first, second = _gather_wait(s["send"], s["recv"], s["local"], s["fsend"], s["frecv"], s["shards"], s["bufs"],
                                     s["kinds"], after, "gather_wait_" + group)
        if group == "fox":
            return natural(first, n_in_pad), second
        if group == "swa":
            return natural(first), second
        return first, second

    out, pending = {}, {}

    def columns_major(g_t, n):
        return g_t[:n].reshape(N_DEV, n // N_DEV, D)

    def transposed(a):
        return jnp.transpose(a, (0, 2, 1))

    big = dict(
        ffn_w_down1=("ffn_w_down", "rows", 1, (ffn_w_down, m_ffn_w_down, v_ffn_w_down)),
        ffn_w_up1=("ffn_w_up", "halves", 1, (ffn_w_up, m_ffn_w_up, v_ffn_w_up)),
        swa_w_o=("swa_w_o", "rows", 0, (swa_w_o, m_swa_w_o, v_swa_w_o)),
        swa_w_in=("swa_w_in", "major", 0, tuple(transposed(a) for a in (swa_w_in, m_swa_w_in, v_swa_w_in))),
        ffn_w_down0=("ffn_w_down", "rows", 0, (ffn_w_down, m_ffn_w_down, v_ffn_w_down)),
        ffn_w_up0=("ffn_w_up", "halves", 0, (ffn_w_up, m_ffn_w_up, v_ffn_w_up)),
        fox_w_o=("fox_w_o", "rows", 0, (fox_w_o, m_fox_w_o, v_fox_w_o)),
        fox_w_in=("fox_w_in", "major", 0, (fox_w_in, m_fox_w_in, v_fox_w_in)))
    finish_at = dict(swa_w_o=["ffn_w_down1"], ffn_w_up0=["ffn_w_up1", "swa_w_o", "swa_w_in"], fox_w_o=["ffn_w_down0"],
                     fox_w_in=["ffn_w_up0"])
    tail = {}

    def finish(name, after):
        send, recv, thru, lands = pending.pop(name)
        param, kind, layer, wmv = big[name]
        if name == "fox_w_in":
            one = tail["last_start"][0, 0] + 1.0
            wmv = tuple(transposed(a * one) for a in wmv)
        landed, = _scatter_wait(send, recv, thru, lands, [kind], after, "scatter_wait_" + name)
        res = _adam_sum(landed, *wmv, layer, out.get(param), "adam_" + name, by_cols=kind == "major")
        out[param] = [transposed(r) for r in res] if kind == "major" else res

    def on_grads(name, g, after=None):
        kind = big[name][1]
        src = columns_major(g, n_in if name == "fox_w_in" else g.shape[0]) if kind == "major" else g
        send, recv, thru, lands, token = _scatter_start([src], [kind], after, "scatter_start_" + name)
        pending[name] = (send, recv, thru, lands)
        for done in finish_at.get(name, ()):
            finish(done, [token])
        tail["last_start"] = token
        return token

    def on_small(small):
        packed, tail["spans"] = _pack_rows([small[k] for k in SMALL_ORDER])
        send, recv, thru, lands, token = _scatter_start([packed], ["whole"], None, "small_grads_start")
        tail["small"] = (send, recv, thru, lands)
        return token

    grad_x = _local_step(x[0], loss_target[0], positions[0], mods, fetch, P, on_grads, on_small)

    spans = tail["spans"]
    done_first = [out[k][1] for k in ("ffn_w_up", "ffn_w_down", "swa_w_in", "swa_w_o")]
    gathered, = _scatter_wait(*tail["small"], ["whole"], done_first, "small_grads_wait")
    totals = dict(zip(SMALL_ORDER, _unpack_rows(_sum_slots(gathered, tail["last_start"], "sum_small_grads"), spans)))
    loss = totals["loss"].reshape(())
    n_mod = L * 6 * D
    dmod_all = gathered.reshape(N_DEV, -1)[:, :n_mod].reshape(N_DEV, L, 6 * D)
    dmod_cols = jnp.transpose(lax.dynamic_slice_in_dim(dmod_all, me * n_ada, n_ada, axis=2), (1, 0, 2))
    g_ada_w = _ada_bwd(c_all.T, dmod_cols, "ada_w_grad")
    out["ada_w"] = (g_ada_w,) + tuple(_adam(g_ada_w, ada_w, m_ada_w, v_ada_w, "adam_ada_w"))

    g_small = dict(fox_b_f=totals["fox_b_f"].reshape(fox_b_f.shape), swa_sinks=totals["swa_sinks"].reshape(swa_sinks.shape),
                   ada_b=totals["dmod"].reshape(ada_b.shape), ffn_conv_b=totals["conv_b"].reshape(ffn_conv_b.shape),
                   ffn_conv_w=lax.dynamic_slice_in_dim(totals["conv_w"].reshape(L, 3, 2 * F), me * cu, cu, axis=2),
                   ln_mix_g=totals["ln_mix_g"], ln_mix_b=totals["ln_mix_b"],
                   ln_ffn_g=totals["ln_ffn_g"], ln_ffn_b=totals["ln_ffn_b"])
    small_names = ("fox_b_f", "swa_sinks", "ada_b", "ffn_conv_b", "ffn_conv_w", "ln_mix_g", "ln_mix_b", "ln_ffn_g", "ln_ffn_b")
    w_small = dict(fox_b_f=(fox_b_f, m_fox_b_f, v_fox_b_f), swa_sinks=(swa_sinks, m_swa_sinks, v_swa_sinks),
                   ada_b=(ada_b, m_ada_b, v_ada_b), ffn_conv_b=(ffn_conv_b, m_ffn_conv_b, v_ffn_conv_b),
                   ffn_conv_w=(ffn_conv_w, m_ffn_conv_w, v_ffn_conv_w),
                   ln_mix_g=(ln_mix_g, m_ln_mix_g, v_ln_mix_g), ln_mix_b=(ln_mix_b, m_ln_mix_b, v_ln_mix_b),
                   ln_ffn_g=(ln_ffn_g, m_ln_ffn_g, v_ln_ffn_g), ln_ffn_b=(ln_ffn_b, m_ln_ffn_b, v_ln_ffn_b))
    pk_g, sp = _pack_rows([g_small[k] for k in small_names])
    pk_w = _pack_rows([w_small[k][0] for k in small_names])[0]
    pk_m = _pack_rows([w_small[k][1] for k in small_names])[0]
    pk_v = _pack_rows([w_small[k][2] for k in small_names])[0]
    res = _adam(pk_g[None], pk_w[None], pk_m[None], pk_v[None], "adam_small")
    settled = [res[0], out["ada_w"][1]] + [out[k][1] for k in ("ffn_w_up", "ffn_w_down", "swa_w_in", "swa_w_o")]
    finish("fox_w_o", settled)
    finish("fox_w_in", settled)
    res = [dict(zip(small_names, _unpack_rows(r[0], sp))) for r in res]
    for k in small_names:
        out[k] = (g_small[k], res[0][k], res[1][k], res[2][k])

    order = ("fox_w_in", "fox_b_f", "fox_w_o", "swa_w_in", "swa_sinks", "swa_w_o", "ada_w", "ada_b", "ffn_w_up",
             "ffn_conv_w", "ffn_conv_b", "ffn_w_down", "ln_mix_g", "ln_mix_b", "ln_ffn_g", "ln_ffn_b")
    return (loss, grad_x[None], *[out[k][0] for k in order], *[out[k][1] for k in order],
            *[out[k][2] for k in order], *[out[k][3] for k in order])
```

```python
import functools

import jax
import jax.numpy as jnp
from jax import lax
from jax.experimental import pallas as pl
from jax.experimental.pallas import tpu as pltpu

F32 = jnp.float32
BF16 = jnp.bfloat16
MESH = pl.DeviceIdType.MESH
N_DEV = 8
AXES = ("x", "y", "c")

DEPTH = 2
ALPHA = (2.0 * DEPTH) ** 0.25
LN_EPS = 1e-5
FOX_HEAD_DIM = 128
SWA_HEAD_DIM = 64
SWA_GROUP = 8
SWA_WINDOW = 128
Q_BLOCK = 128
ROPE_DIM = 16
ROPE_THETA = 500000.0

ADAM_LR = 0.001
ADAM_B1 = 0.9
ADAM_B2 = 0.999
ADAM_EPS = 1e-08
ADAM_WD = 0.01
ADAM_STEP = 10

LANE = 128
MIB = 1024 * 1024


def _tile(n, pref, unit=LANE):
    if n <= pref:
        return n
    t = (pref // unit) * unit
    while t >= unit:
        if n % t == 0:
            return t
        t -= unit
    return n


def _params(sem, vmem_mib=48):
    return pltpu.CompilerParams(dimension_semantics=sem, vmem_limit_bytes=vmem_mib * MIB)


def _sigmoid(x):
    return 1.0 / (1.0 + jnp.exp(-x))


def _mm_call(dot, grid_mnk, in_specs, out_spec, out_shape, k_axis, nk, tm, tn, name, operands):
    sem = ("parallel",) * (len(grid_mnk) - 1) + ("arbitrary",)

    if nk == 1:
        def body(a_ref, b_ref, o_ref):
            o_ref[...] = dot(a_ref[...], b_ref[...]).astype(o_ref.dtype)
        scratch = []
    else:
        def body(a_ref, b_ref, o_ref, acc_ref):
            k = pl.program_id(k_axis)

            @pl.when(k == 0)
            def _():
                acc_ref[...] = jnp.zeros_like(acc_ref)

            acc_ref[...] += dot(a_ref[...], b_ref[...])

            @pl.when(k == nk - 1)
            def _():
                o_ref[...] = acc_ref[...].astype(o_ref.dtype)
        scratch = [pltpu.VMEM((tm, tn), F32)]

    return pl.pallas_call(
        body, name=name, grid=grid_mnk, in_specs=in_specs, out_specs=out_spec, out_shape=out_shape,
        scratch_shapes=scratch, compiler_params=_params(sem, 56),
    )(*operands)


def _dot(dims):
    def dot(a, b):
        return lax.dot_general(a.astype(BF16), b.astype(BF16), (dims, ((), ())), preferred_element_type=F32)
    return dot


def _mm_nn(a, b, out_dtype, name, tm=2048, tn=512, tk=2048):
    M, K = a.shape
    N = b.shape[1]
    tm, tn, tk = _tile(M, tm), _tile(N, tn), _tile(K, tk)
    nk = K // tk
    return _mm_call(
        _dot(((1,), (0,))), (M // tm, N // tn, nk),
        [pl.BlockSpec((tm, tk), lambda i, j, k: (i, k)), pl.BlockSpec((tk, tn), lambda i, j, k: (k, j))],
        pl.BlockSpec((tm, tn), lambda i, j, k: (i, j)), jax.ShapeDtypeStruct((M, N), out_dtype),
        2, nk, tm, tn, name, (a, b))


def _mm_nt(a, b, out_dtype, name, tm=2048, tn=512, tk=2048):
    P, M, K = a.shape
    N = b.shape[1]
    tm, tn, tk = _tile(M, tm), _tile(N, tn), _tile(K, tk)
    nk = K // tk
    return _mm_call(
        _dot(((1,), (1,))), (M // tm, N // tn, P * nk),
        [pl.BlockSpec((None, tm, tk), lambda i, j, k: (k // nk, i, k % nk)),
         pl.BlockSpec((None, tn, tk), lambda i, j, k: (k // nk, j, k % nk))],
        pl.BlockSpec((tm, tn), lambda i, j, k: (i, j)), jax.ShapeDtypeStruct((M, N), out_dtype),
        2, P * nk, tm, tn, name, (a, b))


def _mm_tn(a, b, out_dtype, name, tm=2048, tn=512, tk=2048):
    K, M = a.shape
    P, _, N = b.shape
    tm, tn, tk = _tile(M, tm), _tile(N, tn), _tile(K, tk)
    nk = K // tk
    return _mm_call(
        _dot(((0,), (0,))), (P, M // tm, N // tn, nk),
        [pl.BlockSpec((tk, tm), lambda p, i, j, k: (k, i)), pl.BlockSpec((None, tk, tn), lambda p, i, j, k: (p, k, j))],
        pl.BlockSpec((None, tm, tn), lambda p, i, j, k: (p, i, j)), jax.ShapeDtypeStruct((P, M, N), out_dtype),
        3, nk, tm, tn, name, (a, b))


ROW_TILE = 256


def _row_spec(tm, D):
    return pl.BlockSpec((tm, D), lambda i: (i, 0))


def _vec_spec(D):
    return pl.BlockSpec((1, D), lambda i: (0, 0))


def _modulate(x, sc, sh, name):
    S, D = x.shape
    tm = _tile(S, ROW_TILE, 8)

    def body(x_ref, sc_ref, sh_ref, h_ref):
        h_ref[...] = (x_ref[...] * (1.0 + sc_ref[...]) + sh_ref[...]).astype(BF16)

    return pl.pallas_call(
        body, name=name, grid=(S // tm,),
        in_specs=[_row_spec(tm, D), _vec_spec(D), _vec_spec(D)],
        out_specs=_row_spec(tm, D),
        out_shape=jax.ShapeDtypeStruct((S, D), BF16),
        compiler_params=_params(("parallel",)),
    )(x, sc, sh)


def _layer_norm_rows(z, gamma, beta):
    mu = jnp.mean(z, axis=-1, keepdims=True)
    zc = z - mu
    var = jnp.mean(zc * zc, axis=-1, keepdims=True)
    return zc * lax.rsqrt(var + LN_EPS) * gamma + beta


def _ln_fwd(x, y, gate, gamma, beta, sc_n, sh_n, after, name):
    S, D = x.shape
    tm = _tile(S, ROW_TILE, 8)

    def body(x_ref, y_ref, gate_ref, g_ref, b_ref, sc_ref, sh_ref, after_ref, z_ref, xo_ref, hn_ref):
        z = ALPHA * x_ref[...] + (1.0 + gate_ref[...]) * y_ref[...]
        xo = _layer_norm_rows(z, g_ref[...], b_ref[...])
        z_ref[...] = z
        xo_ref[...] = xo
        hn_ref[...] = (xo * (1.0 + sc_ref[...]) + sh_ref[...]).astype(BF16)

    return pl.pallas_call(
        body, name=name, grid=(S // tm,),
        in_specs=[_row_spec(tm, D), _row_spec(tm, D)] + [_vec_spec(D)] * 5 + [pl.BlockSpec(memory_space=pl.ANY)],
        out_specs=[_row_spec(tm, D)] * 3,
        out_shape=[jax.ShapeDtypeStruct((S, D), F32), jax.ShapeDtypeStruct((S, D), F32),
                   jax.ShapeDtypeStruct((S, D), BF16)],
        compiler_params=_params(("parallel",)),
    )(x, y, gate, gamma, beta, sc_n, sh_n, after)


def _ln_fwd_loss(x, y, gate, gamma, beta, target, name):
    S, D = x.shape
    tm = _tile(S, ROW_TILE, 8)

    def body(x_ref, y_ref, gate_ref, g_ref, b_ref, t_ref, z_ref, dout_ref, loss_ref):
        @pl.when(pl.program_id(0) == 0)
        def _():
            loss_ref[...] = jnp.zeros_like(loss_ref)

        z = ALPHA * x_ref[...] + (1.0 + gate_ref[...]) * y_ref[...]
        xo = _layer_norm_rows(z, g_ref[...], b_ref[...])
        err = xo - t_ref[...]
        z_ref[...] = z
        dout_ref[...] = err * (1.0 / D)
        loss_ref[...] += (0.5 / D) * jnp.sum(err * err)

    return pl.pallas_call(
        body, name=name, grid=(S // tm,),
        in_specs=[_row_spec(tm, D), _row_spec(tm, D)] + [_vec_spec(D)] * 3 + [_row_spec(tm, D)],
        out_specs=[_row_spec(tm, D), _row_spec(tm, D), pl.BlockSpec((1, LANE), lambda i: (0, 0))],
        out_shape=[jax.ShapeDtypeStruct((S, D), F32), jax.ShapeDtypeStruct((S, D), F32),
                   jax.ShapeDtypeStruct((1, LANE), F32)],
        compiler_params=_params(("arbitrary",)),
    )(x, y, gate, gamma, beta, target)


def _ln_bwd(dout, z, y, gate, gamma, name):
    S, D = z.shape
    tm = _tile(S, ROW_TILE, 8)

    def body(dout_ref, z_ref, y_ref, gate_ref, g_ref, dz_ref, dy_ref, dg_ref, db_ref, dgate_ref):
        @pl.when(pl.program_id(0) == 0)
        def _():
            dg_ref[...] = jnp.zeros_like(dg_ref)
            db_ref[...] = jnp.zeros_like(db_ref)
            dgate_ref[...] = jnp.zeros_like(dgate_ref)

        z = z_ref[...]
        dout = dout_ref[...]
        mu = jnp.mean(z, axis=-1, keepdims=True)
        zc = z - mu
        var = jnp.mean(zc * zc, axis=-1, keepdims=True)
        rstd = lax.rsqrt(var + LN_EPS)
        xhat = zc * rstd
        dxhat = dout * g_ref[...]
        m1 = jnp.mean(dxhat, axis=-1, keepdims=True)
        m2 = jnp.mean(dxhat * xhat, axis=-1, keepdims=True)
        dz = rstd * (dxhat - m1 - xhat * m2)
        dz_ref[...] = dz
        dy_ref[...] = (dz * (1.0 + gate_ref[...])).astype(BF16)
        dg_ref[...] += jnp.sum(dout * xhat, axis=0, keepdims=True)
        db_ref[...] += jnp.sum(dout, axis=0, keepdims=True)
        dgate_ref[...] += jnp.sum(dz * y_ref[...], axis=0, keepdims=True)

    return pl.pallas_call(
        body, name=name, grid=(S // tm,),
        in_specs=[_row_spec(tm, D)] * 3 + [_vec_spec(D)] * 2,
        out_specs=[_row_spec(tm, D), _row_spec(tm, D)] + [_vec_spec(D)] * 3,
        out_shape=[jax.ShapeDtypeStruct((S, D), F32), jax.ShapeDtypeStruct((S, D), BF16)]
        + [jax.ShapeDtypeStruct((1, D), F32)] * 3,
        compiler_params=_params(("arbitrary",)),
    )(dout, z, y, gate, gamma)


def _mod_bwd(dz, dh, xin, sc, after, name):
    S, D = dz.shape
    tm = _tile(S, ROW_TILE, 8)

    def body(dz_ref, dh_ref, x_ref, sc_ref, after_ref, dx_ref, dsc_ref, dsh_ref):
        @pl.when(pl.program_id(0) == 0)
        def _():
            dsc_ref[...] = jnp.zeros_like(dsc_ref)
            dsh_ref[...] = jnp.zeros_like(dsh_ref)

        dh = dh_ref[...]
        dx_ref[...] = ALPHA * dz_ref[...] + dh * (1.0 + sc_ref[...])
        dsc_ref[...] += jnp.sum(dh * x_ref[...], axis=0, keepdims=True)
        dsh_ref[...] += jnp.sum(dh, axis=0, keepdims=True)

    return pl.pallas_call(
        body, name=name, grid=(S // tm,),
        in_specs=[_row_spec(tm, D)] * 3 + [_vec_spec(D), pl.BlockSpec(memory_space=pl.ANY)],
        out_specs=[_row_spec(tm, D), _vec_spec(D), _vec_spec(D)],
        out_shape=[jax.ShapeDtypeStruct((S, D), F32)] + [jax.ShapeDtypeStruct((1, D), F32)] * 2,
        compiler_params=_params(("arbitrary",)),
    )(dz, dh, xin, sc, after)


def _shift_down(u, k, row):
    return jnp.where(row >= k, pltpu.roll(u, k, axis=0), 0.0)


def _shift_up(u, k, row, S):
    return jnp.where(row < S - k, pltpu.roll(u, S - k, axis=0), 0.0)


def _ffn_up(h, w, cw, cb, name):
    S, D = h.shape
    F = w.shape[2]
    tn = _tile(F, 256)

    def body(h_ref, w_ref, cw_ref, cb_ref, u_ref, uc_ref, a_ref):
        hh = h_ref[...]
        row = lax.broadcasted_iota(jnp.int32, (S, tn), 0)
        conv = []
        for p in range(2):
            u = jnp.dot(hh, w_ref[p], preferred_element_type=F32)
            u_ref[p] = u.astype(BF16)
            cwp = cw_ref[p]
            uc = _shift_down(u, 2, row) * cwp[0:1] + _shift_down(u, 1, row) * cwp[1:2] + u * cwp[2:3] + cb_ref[p]
            uc_ref[p] = uc.astype(BF16)
            conv.append(uc)
        g, v = conv
        a_ref[...] = (g * _sigmoid(g) * v).astype(BF16)

    half = pl.BlockSpec((2, S, tn), lambda j: (0, 0, j))
    return pl.pallas_call(
        body, name=name, grid=(F // tn,),
        in_specs=[pl.BlockSpec((S, D), lambda j: (0, 0)), pl.BlockSpec((2, D, tn), lambda j: (0, 0, j)),
                  pl.BlockSpec((2, 3, tn), lambda j: (0, 0, j)), pl.BlockSpec((2, 1, tn), lambda j: (0, 0, j))],
        out_specs=[half, half, pl.BlockSpec((S, tn), lambda j: (0, j))],
        out_shape=[jax.ShapeDtypeStruct((2, S, F), BF16), jax.ShapeDtypeStruct((2, S, F), BF16),
                   jax.ShapeDtypeStruct((S, F), BF16)],
        compiler_params=_params(("parallel",), 56),
    )(h, w, cw, cb)


def _ffn_bwd_elem(da, u, uc, cw, after, name):
    _, S, F = u.shape
    tn = _tile(F, 256)

    def body(da_ref, u_ref, uc_ref, cw_ref, after_ref, du_ref, dcw_ref, dcb_ref):
        row = lax.broadcasted_iota(jnp.int32, (S, tn), 0)
        da = da_ref[...]
        g, v = uc_ref[0].astype(F32), uc_ref[1].astype(F32)
        sg = _sigmoid(g)
        d_conv = (da * v * (sg * (1.0 + g * (1.0 - sg))), da * (g * sg))
        for p in range(2):
            d = d_conv[p]
            cwp = cw_ref[p]
            u = u_ref[p].astype(F32)
            d1, d2 = _shift_up(d, 1, row, S), _shift_up(d, 2, row, S)
            dcb_ref[p] = jnp.sum(d, axis=0, keepdims=True)
            dcw_ref[p, 0:1, :] = jnp.sum(d2 * u, axis=0, keepdims=True)
            dcw_ref[p, 1:2, :] = jnp.sum(d1 * u, axis=0, keepdims=True)
            dcw_ref[p, 2:3, :] = jnp.sum(d * u, axis=0, keepdims=True)
            du_ref[p] = (d * cwp[2:3] + d1 * cwp[1:2] + d2 * cwp[0:1]).astype(BF16)

    half = pl.BlockSpec((2, S, tn), lambda j: (0, 0, j))
    return pl.pallas_call(
        body, name=name, grid=(F // tn,),
        in_specs=[pl.BlockSpec((S, tn), lambda j: (0, j)), half, half, pl.BlockSpec((2, 3, tn), lambda j: (0, 0, j)),
                  pl.BlockSpec(memory_space=pl.ANY)],
        out_specs=[half, pl.BlockSpec((2, 3, tn), lambda j: (0, 0, j)), pl.BlockSpec((2, 1, tn), lambda j: (0, 0, j))],
        out_shape=[jax.ShapeDtypeStruct((2, S, F), BF16), jax.ShapeDtypeStruct((2, 3, F), F32),
                   jax.ShapeDtypeStruct((2, 1, F), F32)],
        compiler_params=_params(("parallel",), 56),
    )(da, u, uc, cw, after)


def _split3(x):
    hi = x.astype(BF16)
    r1 = x - hi.astype(F32)
    mid = r1.astype(BF16)
    lo = (r1 - mid.astype(F32)).astype(BF16)
    return hi, mid, lo


def _tri_matmul(x, upper, S):
    tc = _tile(S, 512)
    parts = _split3(x)
    outs = []
    for b in range(S // tc):
        r = lax.broadcasted_iota(jnp.int32, (S, tc), 0)
        c = lax.broadcasted_iota(jnp.int32, (S, tc), 1) + b * tc
        tri = jnp.where((r <= c) if upper else (r >= c), 1.0, 0.0).astype(BF16)
        acc = jnp.dot(parts[0], tri, preferred_element_type=F32)
        acc += jnp.dot(parts[1], tri, preferred_element_type=F32)
        acc += jnp.dot(parts[2], tri, preferred_element_type=F32)
        outs.append(acc)
    return outs, tc


def _fox_prep(flT, bf, name):
    H, S = flT.shape

    def body(fl_ref, b_ref, cum_ref):
        zz = fl_ref[...] + b_ref[...]
        lf = jnp.minimum(zz, 0.0) - jnp.log(1.0 + jnp.exp(-jnp.abs(zz)))
        outs, tc = _tri_matmul(lf, True, S)
        for b, o in enumerate(outs):
            cum_ref[:, b * tc:(b + 1) * tc] = o

    return pl.pallas_call(
        body, name=name,
        in_specs=[pl.BlockSpec(memory_space=pltpu.VMEM)] * 2,
        out_specs=pl.BlockSpec(memory_space=pltpu.VMEM),
        out_shape=jax.ShapeDtypeStruct((H, S), F32),
        compiler_params=pltpu.CompilerParams(vmem_limit_bytes=48 * MIB),
    )(flT, bf)


def _fox_prep_bwd(dcum_key, dcum_query, flT, bf, name):
    H, S = flT.shape

    def body(dck_ref, dcq_ref, fl_ref, b_ref, dfl_ref, dbf_ref):
        zz = fl_ref[...] + b_ref[...]
        outs, tc = _tri_matmul(dck_ref[...] + dcq_ref[...], False, S)
        total = jnp.zeros((H, 1), F32)
        for b, o in enumerate(outs):
            dfl = o * _sigmoid(-zz[:, b * tc:(b + 1) * tc])
            dfl_ref[:, b * tc:(b + 1) * tc] = dfl
            total += jnp.sum(dfl, axis=1, keepdims=True)
        dbf_ref[...] = total

    return pl.pallas_call(
        body, name=name,
        in_specs=[pl.BlockSpec(memory_space=pltpu.VMEM)] * 4,
        out_specs=[pl.BlockSpec(memory_space=pltpu.VMEM)] * 2,
        out_shape=[jax.ShapeDtypeStruct((H, S), F32), jax.ShapeDtypeStruct((H, 1), F32)],
        compiler_params=pltpu.CompilerParams(vmem_limit_bytes=48 * MIB),
    )(dcum_key, dcum_query, flT, bf)


FOX_TQ = 256
FOX_TC = 512


def _fox_scores(q, k_ref, ck_ref, i, lo, n, tq):
    k = k_ref[lo:lo + n, :].astype(BF16)
    s = lax.dot_general(q, k, (((1,), (1,)), ((), ())), preferred_element_type=F32) - ck_ref[:, lo:lo + n]
    qpos = i * tq + lax.broadcasted_iota(jnp.int32, (tq, n), 0)
    kpos = lo + lax.broadcasted_iota(jnp.int32, (tq, n), 1)
    return jnp.where(kpos <= qpos, s, -jnp.inf)


FOX_FWD_SPLITS = 4


def _fox_attn_fwd_part(proj, cum_row, H, q_lo, q_hi, filled, name):
    S = proj.shape[0]
    dh = FOX_HEAD_DIM
    tq = _tile(q_hi - q_lo, FOX_TQ)
    first = q_lo // tq
    scale = dh ** -0.5
    n_keep = 0 if filled is None else 2

    def body(q_ref, k_ref, v_ref, ck_ref, *rest):
        o_ref, lse_ref = rest[n_keep:]
        q = (q_ref[...] * scale).astype(BF16)
        s = _fox_scores(q, k_ref, ck_ref, first + pl.program_id(1), 0, q_hi, tq)
        m = jnp.max(s, axis=-1, keepdims=True)
        p = jnp.exp(s - m)
        l = jnp.sum(p, axis=-1, keepdims=True)
        o = jnp.dot(p.astype(BF16), v_ref[...].astype(BF16), preferred_element_type=F32) / l
        o_ref[...] = o.astype(BF16)
        lse_ref[...] = m + jnp.log(l)

    return pl.pallas_call(
        body, name=name, grid=(H, (q_hi - q_lo) // tq),
        in_specs=[pl.BlockSpec((tq, dh), lambda h, i: (first + i, h)),
                  pl.BlockSpec((q_hi, dh), lambda h, i: (0, H + h)),
                  pl.BlockSpec((q_hi, dh), lambda h, i: (0, 2 * H + h)),
                  pl.BlockSpec((None, 1, q_hi), lambda h, i: (h, 0, 0))] + [pl.BlockSpec(memory_space=pl.ANY)] * n_keep,
        out_specs=[pl.BlockSpec((tq, dh), lambda h, i: (first + i, h)),
                   pl.BlockSpec((None, tq, 1), lambda h, i: (h, first + i, 0))],
        out_shape=[jax.ShapeDtypeStruct((S, H * dh), BF16), jax.ShapeDtypeStruct((H, S, 1), F32)],
        input_output_aliases={4 + k: k for k in range(n_keep)},
        compiler_params=_params(("parallel", "parallel")),
    )(proj, proj, proj, cum_row, *(filled or ()))


def _fox_attn_fwd(proj, cum_row, H, name):
    S = proj.shape[0]
    n = FOX_FWD_SPLITS if S % (FOX_FWD_SPLITS * FOX_TQ) == 0 else 1
    out = None
    for part in range(n):
        out = _fox_attn_fwd_part(proj, cum_row, H, part * S // n, (part + 1) * S // n, out, name + str(part))
    return out


def _fox_attn_bwd(proj, do, o, cum_row, lse, H, after, name):
    S = proj.shape[0]
    dh = FOX_HEAD_DIM
    tq = _tile(S, 2 * FOX_TQ)
    tc = _tile(S, FOX_TC)
    scale = dh ** -0.5

    def body(q_ref, k_ref, v_ref, do_ref, o_ref, ck_ref, lse_ref, after_ref,
             dq_ref, dk_ref, dv_ref, dck_ref, dcq_ref, acc_ref):
        i = pl.program_id(1)

        @pl.when(i == 0)
        def _():
            dk_ref[...] = jnp.zeros_like(dk_ref)
            dv_ref[...] = jnp.zeros_like(dv_ref)
            dck_ref[...] = jnp.zeros_like(dck_ref)

        acc_ref[...] = jnp.zeros_like(acc_ref)
        dcq_ref[...] = jnp.zeros_like(dcq_ref)
        q = (q_ref[...] * scale).astype(BF16)
        do_f = do_ref[...]
        do_b = do_f.astype(BF16)
        delta = jnp.sum(do_f * o_ref[...].astype(F32), axis=-1, keepdims=True)
        lse_q = lse_ref[...]
        for c in range(S // tc):
            @pl.when(c * tc <= i * tq + tq - 1)
            def _():
                rows = slice(c * tc, (c + 1) * tc)
                p = jnp.exp(_fox_scores(q, k_ref, ck_ref, i, c * tc, tc, tq) - lse_q)
                dp = lax.dot_general(do_b, v_ref[rows, :].astype(BF16), (((1,), (1,)), ((), ())),
                                     preferred_element_type=F32)
                ds = p * (dp - delta)
                ds_b = ds.astype(BF16)
                acc_ref[...] += jnp.dot(ds_b, k_ref[rows, :].astype(BF16), preferred_element_type=F32)
                dk_ref[rows, :] += lax.dot_general(ds_b, q, (((0,), (0,)), ((), ())), preferred_element_type=F32)
                dv_ref[rows, :] += lax.dot_general(p.astype(BF16), do_b, (((0,), (0,)), ((), ())),
                                                   preferred_element_type=F32)
                dck_ref[:, rows] -= jnp.sum(ds, axis=0, keepdims=True)
                dcq_ref[...] += jnp.sum(ds, axis=-1, keepdims=True)
        dq_ref[...] = (acc_ref[...] * scale).astype(BF16)

    W = H * dh
    return pl.pallas_call(
        body, name=name, grid=(H, S // tq),
        in_specs=[pl.BlockSpec((tq, dh), lambda h, i: (i, h)),
                  pl.BlockSpec((S, dh), lambda h, i: (0, H + h)),
                  pl.BlockSpec((S, dh), lambda h, i: (0, 2 * H + h)),
                  pl.BlockSpec((tq, dh), lambda h, i: (i, h)),
                  pl.BlockSpec((tq, dh), lambda h, i: (i, h)),
                  pl.BlockSpec((None, 1, S), lambda h, i: (h, 0, 0)),
                  pl.BlockSpec((None, tq, 1), lambda h, i: (h, i, 0)),
                  pl.BlockSpec(memory_space=pl.ANY)],
        out_specs=[pl.BlockSpec((tq, dh), lambda h, i: (i, h)),
                   pl.BlockSpec((S, dh), lambda h, i: (0, h)),
                   pl.BlockSpec((S, dh), lambda h, i: (0, h)),
                   pl.BlockSpec((None, 1, S), lambda h, i: (h, 0, 0)),
                   pl.BlockSpec((None, tq, 1), lambda h, i: (h, i, 0))],
        out_shape=[jax.ShapeDtypeStruct((S, W), BF16), jax.ShapeDtypeStruct((S, W), F32),
                   jax.ShapeDtypeStruct((S, W), F32), jax.ShapeDtypeStruct((H, 1, S), F32),
                   jax.ShapeDtypeStruct((H, S, 1), F32)],
        scratch_shapes=[pltpu.VMEM((tq, dh), F32)],
        compiler_params=_params(("parallel", "arbitrary")),
    )(proj, proj, proj, do, o, cum_row, lse, after)


def _rope(parts, tabs, out_dtype, name):
    S = parts[0][0].shape[0]
    widths = [w for _, _, w in parts]
    total = sum(widths)
    tm = _tile(S, ROW_TILE, 8)
    flags = [r for _, r, _ in parts]

    def body(*refs):
        in_refs = refs[:len(parts)]
        cos_ref, sa_ref, sb_ref, o_ref = refs[len(parts):]
        cos, sa, sb = cos_ref[...], sa_ref[...], sb_ref[...]
        off = 0
        for ref, rot, w in zip(in_refs, flags, widths):
            for j in range(w // LANE):
                t = ref[:, j * LANE:(j + 1) * LANE]
                if rot:
                    t = t * cos + pltpu.roll(t, LANE - ROPE_DIM // 2, axis=1) * sa + pltpu.roll(t, ROPE_DIM // 2, axis=1) * sb
                o_ref[:, off + j * LANE:off + (j + 1) * LANE] = t.astype(o_ref.dtype)
            off += w

    return pl.pallas_call(
        body, name=name, grid=(S // tm,),
        in_specs=[pl.BlockSpec((tm, w), lambda i: (i, 0)) for w in widths] + [_row_spec(tm, LANE)] * 3,
        out_specs=_row_spec(tm, total),
        out_shape=jax.ShapeDtypeStruct((S, total), out_dtype),
        compiler_params=_params(("parallel",)),
    )(*[a for a, _, _ in parts], *tabs)


def _swa_band(ref_p, ref_c, hk):
    dh = SWA_HEAD_DIM
    return jnp.concatenate([ref_p[:, hk * dh:(hk + 1) * dh], ref_c[:, hk * dh:(hk + 1) * dh]], axis=0).astype(BF16)


def _swa_bias(G):
    qi = jnp.arange(G * Q_BLOCK)[:, None] % Q_BLOCK
    kj = jnp.arange(2 * Q_BLOCK)[None, :]
    rel = qi + Q_BLOCK - kj
    window = (rel >= 0) & (rel < SWA_WINDOW)
    both = jnp.stack([window & (kj >= Q_BLOCK), window])
    return jnp.where(both, 0.0, -jnp.inf).astype(F32)


def _swa_stack(ref, hk, G):
    dh = SWA_HEAD_DIM
    return jnp.concatenate([ref[:, (hk * G + g) * dh:(hk * G + g + 1) * dh] for g in range(G)], axis=0)


def _swa_unstack(ref, stacked, hk, G):
    dh, QB = SWA_HEAD_DIM, Q_BLOCK
    for g in range(0, G, 2):
        c0 = (hk * G + g) * dh
        pair = jnp.concatenate([stacked[g * QB:(g + 1) * QB], stacked[(g + 1) * QB:(g + 2) * QB]], axis=1)
        ref[:, c0:c0 + 2 * dh] = pair.astype(ref.dtype)


def _swa_sink_rows(sink_ref, hk, G):
    return jnp.concatenate([jnp.broadcast_to(sink_ref[0:1, hk * G + g:hk * G + g + 1], (Q_BLOCK, 1)) for g in range(G)],
                           axis=0)


def _swa_attn_fwd(qk, proj, sinks, Hq, name):
    S = qk.shape[0]
    dh, G, QB = SWA_HEAD_DIM, SWA_GROUP, Q_BLOCK
    Hk = Hq // G
    Wq, Wk = Hq * dh, Hk * dh
    nb = S // QB
    scale = dh ** -0.5

    def body(q_ref, kp_ref, kc_ref, vp_ref, vc_ref, sink_ref, bias_ref, o_ref, lse_ref):
        bias = bias_ref[...]
        lane = lax.broadcasted_iota(jnp.int32, (QB, LANE), 1)
        lse_tile = jnp.zeros((QB, LANE), F32)
        for hk in range(Hk):
            kb = _swa_band(kp_ref, kc_ref, hk)
            vb = _swa_band(vp_ref, vc_ref, hk)
            q = (_swa_stack(q_ref, hk, G) * scale).astype(BF16)
            sk = _swa_sink_rows(sink_ref, hk, G)
            s = lax.dot_general(q, kb, (((1,), (1,)), ((), ())), preferred_element_type=F32) + bias
            m = jnp.maximum(jnp.max(s, axis=-1, keepdims=True), sk)
            p = jnp.exp(s - m)
            l = jnp.sum(p, axis=-1, keepdims=True) + jnp.exp(sk - m)
            o = jnp.dot(p.astype(BF16), vb, preferred_element_type=F32) / l
            lse = m + jnp.log(l)
            for g in range(G):
                lse_tile = jnp.where(lane == hk * G + g, lse[g * QB:(g + 1) * QB], lse_tile)
            _swa_unstack(o_ref, o, hk, G)
        lse_ref[...] = lse_tile

    kcol, vcol = Wq // Wk, (Wq + Wk) // Wk
    return pl.pallas_call(
        body, name=name, grid=(nb,),
        in_specs=[pl.BlockSpec((QB, Wq), lambda n: (n, 0)),
                  pl.BlockSpec((QB, Wk), lambda n: (jnp.maximum(n - 1, 0), kcol)),
                  pl.BlockSpec((QB, Wk), lambda n: (n, kcol)),
                  pl.BlockSpec((QB, Wk), lambda n: (jnp.maximum(n - 1, 0), vcol)),
                  pl.BlockSpec((QB, Wk), lambda n: (n, vcol)),
                  pl.BlockSpec((1, LANE), lambda n: (0, 0)),
                  pl.BlockSpec((None, G * QB, 2 * QB), lambda n: (jnp.minimum(n, 1), 0, 0))],
        out_specs=[pl.BlockSpec((QB, Wq), lambda n: (n, 0)), pl.BlockSpec((QB, LANE), lambda n: (n, 0))],
        out_shape=[jax.ShapeDtypeStruct((S, Wq), BF16), jax.ShapeDtypeStruct((S, LANE), F32)],
        compiler_params=_params(("parallel",)),
    )(qk, qk, qk, proj, proj, sinks, _swa_bias(G))


def _swa_attn_bwd(qk, proj, sinks, do, lse, Hq, after, name):
    S = qk.shape[0]
    dh, G, QB = SWA_HEAD_DIM, SWA_GROUP, Q_BLOCK
    Hk = Hq // G
    Wq, Wk = Hq * dh, Hk * dh
    nb = S // QB
    scale = dh ** -0.5

    def body(q_ref, kp_ref, kc_ref, vp_ref, vc_ref, sink_ref, do_ref, lse_ref, bias_ref, after_ref,
             dq_ref, dk_ref, dv_ref, dsink_ref, carry_k, carry_v):
        n = pl.program_id(0)

        @pl.when(n == 0)
        def _():
            dsink_ref[...] = jnp.zeros_like(dsink_ref)

        @pl.when(n < nb)
        def _():
            bias = bias_ref[...]
            lane = lax.broadcasted_iota(jnp.int32, (1, LANE), 1)
            dsink = jnp.zeros((1, LANE), F32)
            dk_heads, dv_heads = [], []
            for hk in range(Hk):
                kb = _swa_band(kp_ref, kc_ref, hk)
                vb = _swa_band(vp_ref, vc_ref, hk)
                q = (_swa_stack(q_ref, hk, G) * scale).astype(BF16)
                do_s = _swa_stack(do_ref, hk, G).astype(BF16)
                lse = jnp.concatenate([lse_ref[:, hk * G + g:hk * G + g + 1] for g in range(G)], axis=0)
                s = lax.dot_general(q, kb, (((1,), (1,)), ((), ())), preferred_element_type=F32) + bias
                p = jnp.exp(s - lse)
                p_sink = jnp.exp(_swa_sink_rows(sink_ref, hk, G) - lse)
                dp = lax.dot_general(do_s, vb, (((1,), (1,)), ((), ())), preferred_element_type=F32)
                delta = jnp.sum(p * dp, axis=-1, keepdims=True)
                ds_b = (p * (dp - delta)).astype(BF16)
                _swa_unstack(dq_ref, jnp.dot(ds_b, kb, preferred_element_type=F32) * scale, hk, G)
                dk_heads.append(lax.dot_general(ds_b, q, (((0,), (0,)), ((), ())), preferred_element_type=F32))
                dv_heads.append(lax.dot_general(p.astype(BF16), do_s, (((0,), (0,)), ((), ())), preferred_element_type=F32))
                sink_term = p_sink * delta
                for g in range(G):
                    dsink = jnp.where(lane == hk * G + g,
                                      -jnp.sum(sink_term[g * QB:(g + 1) * QB], axis=0, keepdims=True), dsink)
            dsink_ref[...] += dsink
            dk_all = jnp.concatenate(dk_heads, axis=1)
            dv_all = jnp.concatenate(dv_heads, axis=1)

            @pl.when(n > 0)
            def _():
                dk_ref[...] = carry_k[...] + dk_all[:QB]
                dv_ref[...] = carry_v[...] + dv_all[:QB]

            carry_k[...] = dk_all[QB:]
            carry_v[...] = dv_all[QB:]

        @pl.when(n == nb)
        def _():
            dk_ref[...] = carry_k[...]
            dv_ref[...] = carry_v[...]

    kcol, vcol = Wq // Wk, (Wq + Wk) // Wk
    cur = lambda n: jnp.minimum(n, nb - 1)
    prev = lambda n: jnp.maximum(jnp.minimum(n, nb - 1) - 1, 0)
    return pl.pallas_call(
        body, name=name, grid=(nb + 1,),
        in_specs=[pl.BlockSpec((QB, Wq), lambda n: (cur(n), 0)),
                  pl.BlockSpec((QB, Wk), lambda n: (prev(n), kcol)),
                  pl.BlockSpec((QB, Wk), lambda n: (cur(n), kcol)),
                  pl.BlockSpec((QB, Wk), lambda n: (prev(n), vcol)),
                  pl.BlockSpec((QB, Wk), lambda n: (cur(n), vcol)),
                  pl.BlockSpec((1, LANE), lambda n: (0, 0)),
                  pl.BlockSpec((QB, Wq), lambda n: (cur(n), 0)),
                  pl.BlockSpec((QB, LANE), lambda n: (cur(n), 0)),
                  pl.BlockSpec((None, G * QB, 2 * QB), lambda n: (jnp.minimum(n, 1), 0, 0)),
                  pl.BlockSpec(memory_space=pl.ANY)],
        out_specs=[pl.BlockSpec((QB, Wq), lambda n: (cur(n), 0)),
                   pl.BlockSpec((QB, Wk), lambda n: (jnp.maximum(n - 1, 0), 0)),
                   pl.BlockSpec((QB, Wk), lambda n: (jnp.maximum(n - 1, 0), 0)),
                   pl.BlockSpec((1, LANE), lambda n: (0, 0))],
        out_shape=[jax.ShapeDtypeStruct((S, Wq), F32), jax.ShapeDtypeStruct((S, Wk), F32),
                   jax.ShapeDtypeStruct((S, Wk), F32), jax.ShapeDtypeStruct((1, LANE), F32)],
        scratch_shapes=[pltpu.VMEM((QB, Wk), F32), pltpu.VMEM((QB, Wk), F32)],
        compiler_params=_params(("arbitrary",)),
    )(qk, qk, qk, proj, proj, sinks, do, lse, _swa_bias(G), after)


ADA_ROWS = 16


def _ada_mod(c_pad, w, b, name):
    L, D, N = w.shape
    tn = _tile(N, 512)

    def body(c_ref, w_ref, b_ref, o_ref):
        c = c_ref[...]
        c = c * _sigmoid(c)
        ch = c.astype(BF16)
        cl = (c - ch.astype(F32)).astype(BF16)
        ww = w_ref[...]
        wh = ww.astype(BF16)
        wl = (ww - wh.astype(F32)).astype(BF16)
        acc = jnp.dot(ch, wh, preferred_element_type=F32)
        acc += jnp.dot(ch, wl, preferred_element_type=F32)
        acc += jnp.dot(cl, wh, preferred_element_type=F32)
        o_ref[...] = acc + b_ref[...]

    return pl.pallas_call(
        body, name=name, grid=(L, N // tn),
        in_specs=[pl.BlockSpec((ADA_ROWS, D), lambda l, j: (0, 0)),
                  pl.BlockSpec((None, D, tn), lambda l, j: (l, 0, j)),
                  pl.BlockSpec((None, 1, tn), lambda l, j: (l, 0, j))],
        out_specs=pl.BlockSpec((None, ADA_ROWS, tn), lambda l, j: (l, 0, j)),
        out_shape=jax.ShapeDtypeStruct((L, ADA_ROWS, N), F32),
        compiler_params=_params(("parallel", "parallel")),
    )(c_pad, w, b)


def _ada_bwd_adam(cT, dm, w, m, v, name):
    D = cT.shape[0]
    L, B, N = dm.shape
    tm = _tile(D, 256)

    def body(c_ref, dm_ref, w_ref, m_ref, v_ref, g_ref, d_ref, mo_ref, vo_ref):
        c = c_ref[...]
        c = c * _sigmoid(c)
        dmv = dm_ref[...]
        g = c[:, 0:1] * dmv[0:1, :]
        for b in range(1, B):
            g += c[:, b:b + 1] * dmv[b:b + 1, :]
        delta, mn, vn = _adamw_math(w_ref[...], g, m_ref[...], v_ref[...])
        g_ref[...] = g
        d_ref[...] = delta
        mo_ref[...] = mn
        vo_ref[...] = vn

    spec = pl.BlockSpec((None, tm, N), lambda l, i: (l, i, 0))
    return pl.pallas_call(
        body, name=name, grid=(L, D // tm),
        in_specs=[pl.BlockSpec((tm, B), lambda l, i: (i, 0)), pl.BlockSpec((None, B, N), lambda l, i: (l, 0, 0)),
                  spec, spec, spec],
        out_specs=[spec] * 4,
        out_shape=[jax.ShapeDtypeStruct((L, D, N), F32)] * 4,
        compiler_params=_params(("parallel", "parallel")),
    )(cT, dm, w, m, v)


def _adamw_math(w, g, m, v):
    m = ADAM_B1 * m + (1.0 - ADAM_B1) * g
    v = ADAM_B2 * v + (1.0 - ADAM_B2) * (g * g)
    m_hat = m / (1.0 - ADAM_B1 ** ADAM_STEP)
    v_hat = v / (1.0 - ADAM_B2 ** ADAM_STEP)
    delta = -ADAM_LR * (m_hat / (jnp.sqrt(v_hat) + ADAM_EPS) + ADAM_WD * w)
    return delta, m, v


def _adam_rows(R, C):
    lanes = -(-C // LANE) * LANE
    return _tile(R, max(8, (262144 // lanes) // 8 * 8), 8)


def _adam_sum(recv, w, m, v, layer, filled, name, by_cols=False):
    P, R, C = recv.shape
    L = w.shape[0]
    n_keep = 0 if filled is None else 4
    if by_cols:
        tc = _tile(C, 256)
        grid, spec = (C // tc,), pl.BlockSpec((None, R, tc), lambda i: (layer, 0, i))
        recv_spec = pl.BlockSpec((P, R, tc), lambda i: (0, 0, i))
    else:
        tr = _adam_rows(R, C)
        grid, spec = (R // tr,), pl.BlockSpec((None, tr, C), lambda i: (layer, i, 0))
        recv_spec = pl.BlockSpec((P, tr, C), lambda i: (0, i, 0))

    def body(r_ref, w_ref, m_ref, v_ref, *rest):
        g_ref, d_ref, mo_ref, vo_ref = rest[n_keep:]
        g = r_ref[0].astype(F32)
        for p in range(1, P):
            g = g + r_ref[p].astype(F32)
        delta, mn, vn = _adamw_math(w_ref[...], g, m_ref[...], v_ref[...])
        g_ref[...] = g
        d_ref[...] = delta
        mo_ref[...] = mn
        vo_ref[...] = vn

    return pl.pallas_call(
        body, name=name, grid=grid,
        in_specs=[recv_spec, spec, spec, spec] + [pl.BlockSpec(memory_space=pl.ANY)] * n_keep,
        out_specs=[spec] * 4,
        out_shape=[jax.ShapeDtypeStruct((L, R, C), F32)] * 4,
        input_output_aliases={4 + k: k for k in range(n_keep)},
        compiler_params=_params(("parallel",)),
    )(recv, w, m, v, *(filled or ()))


def _adam(g, w, m, v, name):
    L, R, C = w.shape
    tr = _adam_rows(R, C)

    def body(g_ref, w_ref, m_ref, v_ref, d_ref, mo_ref, vo_ref):
        delta, mn, vn = _adamw_math(w_ref[...], g_ref[...], m_ref[...], v_ref[...])
        d_ref[...] = delta
        mo_ref[...] = mn
        vo_ref[...] = vn

    spec = pl.BlockSpec((None, tr, C), lambda l, i: (l, i, 0))
    return pl.pallas_call(
        body, name=name, grid=(L, R // tr),
        in_specs=[spec] * 4, out_specs=[spec] * 3,
        out_shape=[jax.ShapeDtypeStruct((L, R, C), F32)] * 3,
        compiler_params=_params(("parallel", "parallel")),
    )(g, w, m, v)


def _sum_slots(x, after, name):
    P, R, C = x.shape

    def body(x_ref, after_ref, o_ref):
        acc = x_ref[0]
        for p in range(1, P):
            acc = acc + x_ref[p]
        o_ref[...] = acc

    return pl.pallas_call(
        body, name=name,
        in_specs=[pl.BlockSpec(memory_space=pltpu.VMEM), pl.BlockSpec(memory_space=pl.ANY)],
        out_specs=pl.BlockSpec(memory_space=pltpu.VMEM),
        out_shape=jax.ShapeDtypeStruct((R, C), F32),
        compiler_params=pltpu.CompilerParams(vmem_limit_bytes=48 * MIB),
    )(x, after)


def _my_pos():
    return lax.axis_index("x"), lax.axis_index("y"), lax.axis_index("c")


def _all_gather_small(x, name, after=()):
    R, C = x.shape
    n_after = len(after)

    def body(x_ref, *rest):
        out_ref, send_sems, recv_sems = rest[n_after:]
        x_, y_, c_ = _my_pos()
        me, sibling = (x_, y_, c_), (x_, y_, 1 - c_)
        chips = [(1 - x_, y_), (x_, 1 - y_), (1 - x_, 1 - y_)]

        def slot(px, py, pc):
            return out_ref.at[4 * px + 2 * py + pc]

        def copy(k, block, to):
            return pltpu.make_async_remote_copy(
                src_ref=slot(*block), dst_ref=slot(*block), send_sem=send_sems.at[k], recv_sem=recv_sems.at[k],
                device_id=to, device_id_type=MESH)

        out_ref[4 * x_ + 2 * y_ + c_] = x_ref[...]
        first = [copy(0, me, sibling)] + [copy(1 + j, me, (*chip, c_)) for j, chip in enumerate(chips)]
        for cp in first:
            cp.start()
        passed = [copy(4 + j, (*chip, c_), sibling) for j, chip in enumerate(chips)]
        for j, chip in enumerate(chips):
            copy(1 + j, (*chip, c_), me).wait_recv()
            passed[j].start()
        copy(0, sibling, me).wait_recv()
        for j, chip in enumerate(chips):
            copy(4 + j, (*chip, 1 - c_), me).wait_recv()
        for cp in first + passed:
            cp.wait_send()

    return pl.pallas_call(
        body, name=name,
        in_specs=[pl.BlockSpec(memory_space=pltpu.VMEM)] + [pl.BlockSpec(memory_space=pl.ANY)] * n_after,
        out_specs=pl.BlockSpec(memory_space=pltpu.VMEM),
        out_shape=jax.ShapeDtypeStruct((N_DEV, R, C), x.dtype),
        scratch_shapes=[pltpu.SemaphoreType.DMA((7,)), pltpu.SemaphoreType.DMA((7,))],
        compiler_params=pltpu.CompilerParams(vmem_limit_bytes=48 * MIB),
    )(x, *after)


HBM_SPEC = pl.BlockSpec(memory_space=pltpu.HBM)
SEM_SPEC = pl.BlockSpec(memory_space=pltpu.SEMAPHORE)
ANY_SPEC = pl.BlockSpec(memory_space=pl.ANY)
SPLIT_EFFECT = pltpu.SideEffectType.DATAFLOW_SIDE_EFFECTING


def _in_hbm(a):
    return pltpu.with_memory_space_constraint(a, pltpu.HBM)


def _gathered_shape(a, kind):
    if kind == "major":
        return (N_DEV,) + a.shape
    if kind == "rows":
        return (N_DEV * a.shape[0], a.shape[1])
    return (2, a.shape[0], 4 * a.shape[1])


def _gather_slot(ref, kind, block, shard_shape):
    px, py, pc = block
    if kind == "major":
        return ref.at[4 * px + 2 * py + pc]
    if kind == "rows":
        r = shard_shape[0]
        return ref.at[pl.ds(pl.multiple_of((4 * px + 2 * py + pc) * r, r), r), :]
    cu = shard_shape[1]
    return ref.at[px, :, pl.ds(pl.multiple_of((2 * py + pc) * cu, cu), cu)]


def _gather_peers():
    x_, y_, c_ = _my_pos()
    return (x_, y_, c_), (x_, y_, 1 - c_), [(1 - x_, y_), (x_, 1 - y_), (1 - x_, 1 - y_)]


def _gather_start(shards, kinds, after, name):
    n = len(shards)
    bufs = [lax.empty(_gathered_shape(a, k), a.dtype) for a, k in zip(shards, kinds)]
    extra = [] if after is None else [after]

    def body(*refs):
        shard_refs, buf_refs = refs[:n], refs[n:2 * n]
        send_sems, recv_sems, local_sems = refs[2 * n + len(extra):2 * n + len(extra) + 3]
        token = refs[-1]
        me, sibling, chips = _gather_peers()
        for e in range(n):
            mine = _gather_slot(buf_refs[e], kinds[e], me, shards[e].shape)
            pltpu.make_async_copy(shard_refs[e], mine, local_sems.at[e]).start()
            for k, to in enumerate([sibling] + [(*chip, me[2]) for chip in chips]):
                pltpu.make_async_remote_copy(
                    src_ref=shard_refs[e], dst_ref=mine, send_sem=send_sems.at[4 * e + k],
                    recv_sem=recv_sems.at[4 * e + k], device_id=to, device_id_type=MESH).start()
        token[...] = jnp.zeros_like(token)

    out = pl.pallas_call(
        body, name=name,
        out_shape=(pltpu.SemaphoreType.DMA((4 * n,)), pltpu.SemaphoreType.DMA((4 * n,)), pltpu.SemaphoreType.DMA((n,)),
                   *[pltpu.HBM(a.shape, a.dtype) for a in shards], *[pltpu.HBM(a.shape, a.dtype) for a in bufs],
                   jax.ShapeDtypeStruct((8, LANE), F32)),
        in_specs=[HBM_SPEC] * (2 * n) + [ANY_SPEC] * len(extra),
        out_specs=(SEM_SPEC, SEM_SPEC, SEM_SPEC, *[HBM_SPEC] * (2 * n), pl.BlockSpec(memory_space=pltpu.VMEM)),
        input_output_aliases={i: 3 + i for i in range(2 * n)},
        compiler_params=pltpu.CompilerParams(has_side_effects=SPLIT_EFFECT),
    )(*[_in_hbm(a) for a in shards], *[_in_hbm(a) for a in bufs], *extra)
    return out[0], out[1], out[2], out[3:3 + n], out[3 + n:3 + 2 * n], out[-1]


def _gather_forward(recv_sems, bufs, kinds, shard_shapes, after, name):
    n = len(bufs)

    def body(*refs):
        buf_refs, recv_in = refs[:n], refs[n]
        fsend, frecv = refs[n + 2], refs[n + 3]
        token = refs[-1]
        me, sibling, chips = _gather_peers()
        for e in range(n):
            for j, chip in enumerate(chips):
                slot = _gather_slot(buf_refs[e], kinds[e], (*chip, me[2]), shard_shapes[e])
                pltpu.make_async_remote_copy(
                    src_ref=slot, dst_ref=slot, send_sem=recv_in.at[4 * e + 1 + j], recv_sem=recv_in.at[4 * e + 1 + j],
                    device_id=me, device_id_type=MESH).wait_recv()
                pltpu.make_async_remote_copy(
                    src_ref=slot, dst_ref=slot, send_sem=fsend.at[3 * e + j], recv_sem=frecv.at[3 * e + j],
                    device_id=sibling, device_id_type=MESH).start()
        token[...] = jnp.zeros_like(token)

    out = pl.pallas_call(
        body, name=name,
        out_shape=(pltpu.SemaphoreType.DMA((3 * n,)), pltpu.SemaphoreType.DMA((3 * n,)),
                   *[pltpu.HBM(a.shape, a.dtype) for a in bufs], jax.ShapeDtypeStruct((8, LANE), F32)),
        in_specs=[HBM_SPEC] * n + [SEM_SPEC, ANY_SPEC],
        out_specs=(SEM_SPEC, SEM_SPEC, *[HBM_SPEC] * n, pl.BlockSpec(memory_space=pltpu.VMEM)),
        input_output_aliases={i: 2 + i for i in range(n)},
        compiler_params=pltpu.CompilerParams(has_side_effects=SPLIT_EFFECT),
    )(*bufs, recv_sems, after)
    return out[0], out[1], out[2:2 + n], out[-1]


def _gather_wait(send_sems, recv_sems, local_sems, fsend, frecv, shards, bufs, kinds, after, name):
    n = len(bufs)

    def body(*refs):
        shard_refs, buf_refs = refs[:n], refs[n:2 * n]
        send_in, recv_in, local_in, fsend_in, frecv_in = refs[2 * n:2 * n + 5]
        me, sibling, chips = _gather_peers()

        def arrival(slot, sem):
            return pltpu.make_async_remote_copy(src_ref=slot, dst_ref=slot, send_sem=sem, recv_sem=sem,
                                                device_id=me, device_id_type=MESH)

        for e in range(n):
            shape = shards[e].shape
            mine = _gather_slot(buf_refs[e], kinds[e], me, shape)
            pltpu.make_async_copy(shard_refs[e], mine, local_in.at[e]).wait()
            arrival(_gather_slot(buf_refs[e], kinds[e], sibling, shape), recv_in.at[4 * e]).wait_recv()
            for j, chip in enumerate(chips):
                arrival(_gather_slot(buf_refs[e], kinds[e], (*chip, 1 - me[2]), shape), frecv_in.at[3 * e + j]).wait_recv()
            for k in range(4):
                arrival(mine, send_in.at[4 * e + k]).wait_send()
            for j in range(3):
                arrival(mine, fsend_in.at[3 * e + j]).wait_send()

    out = pl.pallas_call(
        body, name=name,
        out_shape=(*[pltpu.HBM(a.shape, a.dtype) for a in shards], *[pltpu.HBM(a.shape, a.dtype) for a in bufs]),
        in_specs=[HBM_SPEC] * (2 * n) + [SEM_SPEC] * 5 + [ANY_SPEC],
        out_specs=tuple([HBM_SPEC] * (2 * n)),
        input_output_aliases={i: i for i in range(2 * n)},
        compiler_params=pltpu.CompilerParams(has_side_effects=SPLIT_EFFECT),
    )(*shards, *bufs, send_sems, recv_sems, local_sems, fsend, frecv, after)
    return out[n:]


def _grad_slice(ref, kind, j):
    if kind == "whole":
        return ref
    if kind == "major":
        return ref.at[j]
    if kind == "rows":
        r = ref.shape[0] // N_DEV
        return ref.at[pl.ds(j * r, r), :]
    cu = ref.shape[2] // 4
    return ref.at[j // 4, :, pl.ds((j % 4) * cu, cu)]


def _slice_shape(a, kind):
    if kind == "whole":
        return a.shape
    if kind == "major":
        return a.shape[1:]
    if kind == "rows":
        return (a.shape[0] // N_DEV, a.shape[1])
    return (a.shape[1], a.shape[2] // 4)


def _scatter_copies(srcs, lands, kinds, send_sems, recv_sems):
    x_, y_, c_ = _my_pos()
    me = 4 * x_ + 2 * y_ + c_
    n = len(srcs)

    def remote(e, j):
        return pltpu.make_async_remote_copy(
            src_ref=_grad_slice(srcs[e], kinds[e], j), dst_ref=lands[e].at[me],
            send_sem=send_sems.at[e * N_DEV + j], recv_sem=recv_sems.at[e * N_DEV + me],
            device_id=(j // 4, (j // 2) % 2, j % 2), device_id_type=MESH)

    def local(e, j):
        return pltpu.make_async_copy(_grad_slice(srcs[e], kinds[e], j), lands[e].at[j], recv_sems.at[e * N_DEV + j])

    def arrival(e, i):
        return pltpu.make_async_remote_copy(
            src_ref=_grad_slice(srcs[e], kinds[e], i), dst_ref=lands[e].at[i],
            send_sem=send_sems.at[e * N_DEV + i], recv_sem=recv_sems.at[e * N_DEV + i],
            device_id=(i // 4, (i // 2) % 2, i % 2), device_id_type=MESH)

    def start():
        for e in range(n):
            for j in range(N_DEV):
                @pl.when(me == j)
                def _():
                    local(e, j).start()

                @pl.when(me != j)
                def _():
                    remote(e, j).start()

    def wait():
        for e in range(n):
            for i in range(N_DEV):
                @pl.when(me == i)
                def _():
                    local(e, i).wait()

                @pl.when(me != i)
                def _():
                    arrival(e, i).wait_recv()
        for e in range(n):
            for j in range(N_DEV):
                @pl.when(me != j)
                def _():
                    remote(e, j).wait_send()

    return start, wait


def _scatter_start(srcs, kinds, after, name):
    n = len(srcs)
    lands = [lax.empty((N_DEV,) + _slice_shape(a, k), a.dtype) for a, k in zip(srcs, kinds)]
    extra = [] if after is None else [after]

    def body(*refs):
        src_refs, land_refs = refs[:n], refs[n:2 * n]
        send_sems, recv_sems = refs[2 * n + len(extra)], refs[2 * n + len(extra) + 1]
        token = refs[-1]
        start, _ = _scatter_copies(src_refs, land_refs, kinds, send_sems, recv_sems)
        start()
        token[...] = jnp.zeros_like(token)

    out = pl.pallas_call(
        body, name=name,
        out_shape=(pltpu.SemaphoreType.DMA((n * N_DEV,)), pltpu.SemaphoreType.DMA((n * N_DEV,)),
                   *[pltpu.HBM(a.shape, a.dtype) for a in srcs], *[pltpu.HBM(a.shape, a.dtype) for a in lands],
                   jax.ShapeDtypeStruct((8, LANE), F32)),
        in_specs=[HBM_SPEC] * (2 * n) + [ANY_SPEC] * len(extra),
        out_specs=(SEM_SPEC, SEM_SPEC, *[HBM_SPEC] * (2 * n), pl.BlockSpec(memory_space=pltpu.VMEM)),
        input_output_aliases={i: 2 + i for i in range(2 * n)},
        compiler_params=pltpu.CompilerParams(has_side_effects=SPLIT_EFFECT),
    )(*[_in_hbm(a) for a in srcs], *[_in_hbm(a) for a in lands], *extra)
    return out[0], out[1], out[2:2 + n], out[2 + n:2 + 2 * n], out[-1]


def _scatter_wait(send_sems, recv_sems, srcs, lands, kinds, after, name):
    n = len(srcs)

    def body(*refs):
        src_refs, land_refs = refs[:n], refs[n:2 * n]
        _, wait = _scatter_copies(src_refs, land_refs, kinds, refs[2 * n], refs[2 * n + 1])
        wait()

    out = pl.pallas_call(
        body, name=name,
        out_shape=(*[pltpu.HBM(a.shape, a.dtype) for a in srcs], *[pltpu.HBM(a.shape, a.dtype) for a in lands]),
        in_specs=[HBM_SPEC] * (2 * n) + [SEM_SPEC, SEM_SPEC] + [ANY_SPEC] * len(after),
        out_specs=tuple([HBM_SPEC] * (2 * n)),
        input_output_aliases={i: i for i in range(2 * n)},
        compiler_params=pltpu.CompilerParams(has_side_effects=SPLIT_EFFECT),
    )(*srcs, *lands, send_sems, recv_sems, *after)
    return out[n:]


def _rope_tables(positions, sign):
    half = ROPE_DIM // 2
    inv_freq = ROPE_THETA ** (-jnp.arange(0, ROPE_DIM, 2, dtype=F32) / ROPE_DIM)
    ang = positions.astype(F32)[:, None] * inv_freq
    reps = LANE // half
    cos = jnp.tile(jnp.cos(ang), (1, reps))
    sin = jnp.tile(jnp.sin(ang), (1, reps)) * sign
    d = jnp.arange(LANE) % SWA_HEAD_DIM
    return (jnp.where(d < ROPE_DIM, cos, 1.0), jnp.where(d < half, -sin, 0.0),
            jnp.where((d >= half) & (d < ROPE_DIM), sin, 0.0))


def _pad_cols(a, n):
    return jnp.pad(a, ((0, 0), (0, n - a.shape[1])))


def _local_step(x, target, positions, mods, fetch, P, on_grads, on_small):
    S, D = x.shape
    Hf = D // FOX_HEAD_DIM
    Hq = D // SWA_HEAD_DIM
    Hk = Hq // SWA_GROUP
    Wk = Hk * SWA_HEAD_DIM
    n_in = 3 * D + Hf
    (sh1a, sc1a, g1a, sh2a, sc2a, g2a), (sh1b, sc1b, g1b, sh2b, sc2b, g2b) = mods
    row = lambda v: v.reshape(1, -1)
    cw = [jnp.transpose(P["conv_w"][l].reshape(3, 2, -1), (1, 0, 2)) for l in range(2)]
    cb = [P["conv_b"][l].reshape(2, 1, -1) for l in range(2)]

    W = dict(up=[None, None], down=[None, None])
    h1a = _modulate(x, sc1a, sh1a, "modulate_in")
    W["fox_in"], W["fox_o"] = fetch("fox", "wait", h1a)
    proj_a = _mm_nn(h1a, W["fox_in"], F32, "fox_in_proj", tn=896)
    flT = proj_a[:, 3 * D:n_in].T
    bf_col = P["fox_b_f"].reshape(Hf, 1)
    cumT = _fox_prep(flT, bf_col, "fox_cumsum")
    cum_row = cumT.reshape(Hf, 1, S)
    o_a, lse_a = _fox_attn_fwd(proj_a, cum_row, Hf, "fox_attn_fwd")
    token = fetch("ffn0", "forward", o_a)
    y1a = _mm_nn(o_a, W["fox_o"], F32, "fox_out_proj")
    z1a, x1, h2a = _ln_fwd(x, y1a, g1a, row(P["ln_mix_g"][0]), row(P["ln_mix_b"][0]), sc2a, sh2a, token, "ln_mix0")
    W["up"][0], W["down"][0] = fetch("ffn0", "wait", h2a)
    u_a, uc_a, a_a = _ffn_up(h2a, W["up"][0], cw[0], cb[0], "ffn_up0")
    token = fetch("swa", "forward", a_a)
    y2a = _mm_nn(a_a, W["down"][0], F32, "ffn_down0", tk=2816)
    z2a, x2, h1b = _ln_fwd(x1, y2a, g2a, row(P["ln_ffn_g"][0]), row(P["ln_ffn_b"][0]), sc1b, sh1b, token, "ln_ffn0")

    W["swa_in"], W["swa_o"] = fetch("swa", "wait", h1b)
    proj_b = _mm_nn(h1b, W["swa_in"], F32, "swa_in_proj")
    tabs_f = _rope_tables(positions, 1.0)
    tabs_b = _rope_tables(positions, -1.0)
    qk = _rope([(proj_b, True, D + Wk)], tabs_f, F32, "rope_fwd")
    sinks = _pad_cols(P["swa_sinks"].reshape(1, Hq), LANE)
    o_b, lse_b = _swa_attn_fwd(qk, proj_b, sinks, Hq, "swa_attn_fwd")
    token = fetch("ffn1", "forward", o_b)
    y1b = _mm_nn(o_b, W["swa_o"], F32, "swa_out_proj")
    z1b, x3, h2b = _ln_fwd(x2, y1b, g1b, row(P["ln_mix_g"][1]), row(P["ln_mix_b"][1]), sc2b, sh2b, token, "ln_mix1")
    W["up"][1], W["down"][1] = fetch("ffn1", "wait", h2b)
    u_b, uc_b, a_b = _ffn_up(h2b, W["up"][1], cw[1], cb[1], "ffn_up1")
    y2b = _mm_nn(a_b, W["down"][1], F32, "ffn_down1", tk=2816)
    z2b, dout, loss_row = _ln_fwd_loss(x3, y2b, g2b, row(P["ln_ffn_g"][1]), row(P["ln_ffn_b"][1]), target, "ln_ffn1_loss")

    def ffn_backward(dy, a, u, uc, h_in, l, tag):
        d_down = _mm_tn(a, dy[None], BF16, "ffn_dwdown" + tag, tm=1408, tn=1024)[0]
        token = on_grads("ffn_w_down" + tag, d_down)
        da = _mm_nt(dy[None], W["down"][l][None], F32, "ffn_da" + tag)
        du, dcw, dcb = _ffn_bwd_elem(da, u, uc, cw[l], token, "ffn_bwd_elem" + tag)
        d_up = _mm_tn(h_in, du, BF16, "ffn_dwup" + tag)
        token = on_grads("ffn_w_up" + tag, d_up)
        dh = _mm_nt(du, W["up"][l], F32, "ffn_dh" + tag, tn=1024, tk=1408)
        return dh, token, jnp.transpose(dcw, (1, 0, 2)).reshape(3, -1), dcb.reshape(-1)

    dz2b, dy2b, dg_f1, db_f1, dgate2b = _ln_bwd(dout, z2b, y2b, g2b, row(P["ln_ffn_g"][1]), "ln_ffn1_bwd")
    dh2b, token, dcw1, dcb1 = ffn_backward(dy2b, a_b, u_b, uc_b, h2b, 1, "1")
    dx3, dsc2b, dsh2b = _mod_bwd(dz2b, dh2b, x3, sc2b, token, "mod_ffn1_bwd")

    dz1b, dy1b, dg_m1, db_m1, dgate1b = _ln_bwd(dx3, z1b, y1b, g1b, row(P["ln_mix_g"][1]), "ln_mix1_bwd")
    token = on_grads("swa_w_o", _mm_tn(o_b, dy1b[None], BF16, "swa_dwo")[0])
    do_b = _mm_nt(dy1b[None], W["swa_o"][None], F32, "swa_do")
    dq_b, dk_b, dv_b, dsinks = _swa_attn_bwd(qk, proj_b, sinks, do_b, lse_b, Hq, token, "swa_attn_bwd")
    dproj_b = _rope([(dq_b, True, D), (dk_b, True, Wk), (dv_b, False, Wk)], tabs_b, BF16, "rope_bwd")
    token = on_grads("swa_w_in", _mm_tn(dproj_b, h1b[None], BF16, "swa_dwin", tm=1280, tn=1024)[0])
    dh1b = _mm_nt(dproj_b[None], W["swa_in"][None], F32, "swa_dh", tk=1280)
    dx2, dsc1b, dsh1b = _mod_bwd(dz1b, dh1b, x2, sc1b, token, "mod_mix1_bwd")

    dz2a, dy2a, dg_f0, db_f0, dgate2a = _ln_bwd(dx2, z2a, y2a, g2a, row(P["ln_ffn_g"][0]), "ln_ffn0_bwd")
    dh2a, token, dcw0, dcb0 = ffn_backward(dy2a, a_a, u_a, uc_a, h2a, 0, "0")
    dx1, dsc2a, dsh2a = _mod_bwd(dz2a, dh2a, x1, sc2a, token, "mod_ffn0_bwd")

    dz1a, dy1a, dg_m0, db_m0, dgate1a = _ln_bwd(dx1, z1a, y1a, g1a, row(P["ln_mix_g"][0]), "ln_mix0_bwd")
    token = on_grads("fox_w_o", _mm_tn(o_a, dy1a[None], BF16, "fox_dwo")[0])
    do_a = _mm_nt(dy1a[None], W["fox_o"][None], F32, "fox_do")
    dq_a, dk_a, dv_a, dcum_row, dcum_col = _fox_attn_bwd(proj_a, do_a, o_a, cum_row, lse_a, Hf, token, "fox_attn_bwd")
    dflT, dbf = _fox_prep_bwd(dcum_row.reshape(Hf, S), dcum_col.reshape(Hf, S), flT, bf_col, "fox_cumsum_bwd")
    n_pad = W["fox_in"].shape[1]
    dproj_a = jnp.concatenate([dq_a, dk_a.astype(BF16), dv_a.astype(BF16),
                               _pad_cols(dflT.T, n_pad - 3 * D).astype(BF16)], axis=1)
    dh1a = _mm_nt(dproj_a[None], W["fox_in"][None], F32, "fox_dh", tn=1024, tk=896)
    grad_x, dsc1a, dsh1a = _mod_bwd(dz1a, dh1a, x, sc1a, token, "mod_mix0_bwd")

    dmod = jnp.stack([jnp.concatenate([dsh1a, dsc1a, dgate1a, dsh2a, dsc2a, dgate2a], axis=1)[0],
                      jnp.concatenate([dsh1b, dsc1b, dgate1b, dsh2b, dsc2b, dgate2b], axis=1)[0]])
    small = dict(dmod=dmod, conv_b=jnp.stack([dcb0, dcb1]), conv_w=jnp.stack([dcw0, dcw1]),
                 ln_mix_g=jnp.concatenate([dg_m0, dg_m1]), ln_mix_b=jnp.concatenate([db_m0, db_m1]),
                 ln_ffn_g=jnp.concatenate([dg_f0, dg_f1]), ln_ffn_b=jnp.concatenate([db_f0, db_f1]),
                 fox_b_f=dbf.reshape(-1), swa_sinks=dsinks[0, :Hq], loss=loss_row[0, 0].reshape(1))
    exchanged = on_small(small)
    on_grads("fox_w_in", _mm_tn(dproj_a, h1a[None], BF16, "fox_dwin", tm=896, tn=1024)[0], exchanged)
    return grad_x


SMALL_ORDER = ("dmod", "conv_b", "conv_w", "ln_mix_g", "ln_mix_b", "ln_ffn_g", "ln_ffn_b", "fox_b_f", "swa_sinks", "loss")


def _pack_rows(arrays):
    chunks, spans, off = [], [], 0
    for a in arrays:
        flat = a.reshape(-1)
        n = -(-flat.shape[0] // LANE) * LANE
        chunks.append(jnp.pad(flat, (0, n - flat.shape[0])))
        spans.append((off, flat.shape[0], a.shape))
        off += n
    total = -(-off // (8 * LANE)) * (8 * LANE)
    chunks.append(jnp.zeros((total - off,), F32))
    return jnp.concatenate(chunks).reshape(-1, LANE), spans


def _unpack_rows(packed, spans):
    flat = packed.reshape(-1)
    return [flat[off:off + n].reshape(shape) for off, n, shape in spans]


def kernel(x, c, positions, fox_w_in, fox_b_f, fox_w_o, swa_w_in, swa_sinks, swa_w_o, ada_w, ada_b, ffn_w_up, ffn_conv_w, ffn_conv_b, ffn_w_down, ln_mix_g, ln_mix_b, ln_ffn_g, ln_ffn_b, loss_target, m_fox_w_in, m_fox_b_f, m_fox_w_o, m_swa_w_in, m_swa_sinks, m_swa_w_o, m_ada_w, m_ada_b, m_ffn_w_up, m_ffn_conv_w, m_ffn_conv_b, m_ffn_w_down, m_ln_mix_g, m_ln_mix_b, m_ln_ffn_g, m_ln_ffn_b, v_fox_w_in, v_fox_b_f, v_fox_w_o, v_swa_w_in, v_swa_sinks, v_swa_w_o, v_ada_w, v_ada_b, v_ffn_w_up, v_ffn_conv_w, v_ffn_conv_b, v_ffn_w_down, v_ln_mix_g, v_ln_mix_b, v_ln_ffn_g, v_ln_ffn_b):
    S, D = x.shape[1], x.shape[2]
    L = ada_w.shape[0]
    me = 4 * lax.axis_index("x") + 2 * lax.axis_index("y") + lax.axis_index("c")
    n_ada = ada_w.shape[2]
    cu = ffn_w_up.shape[2]
    F = 4 * cu
    n_in = fox_w_in.shape[2] * N_DEV
    n_in_pad = -(-n_in // LANE) * LANE

    gather_groups = dict(
        fox=([fox_w_in[0].astype(BF16), fox_w_o[0].astype(BF16)], ["major", "rows"]),
        ffn0=([ffn_w_up[0].astype(BF16), ffn_w_down[0].astype(BF16)], ["halves", "rows"]),
        swa=([swa_w_in[0].astype(BF16), swa_w_o[0].astype(BF16)], ["major", "rows"]),
        ffn1=([ffn_w_up[1].astype(BF16), ffn_w_down[1].astype(BF16)], ["halves", "rows"]))
    starts_after = dict(fox=["ffn0", "swa"], ffn0=["ffn1"])
    gathers = {}

    def start_group(group, after):
        shards, kinds = gather_groups[group]
        send, recv, local, thru, bufs, token = _gather_start(shards, kinds, after, "gather_start_" + group)
        gathers[group] = dict(send=send, recv=recv, local=local, shards=thru, bufs=bufs, kinds=kinds,
                              shapes=[a.shape for a in shards], token=token)
        return token

    fox_started = start_group("fox", None)

    c_all = _all_gather_small(c.reshape(-1, LANE), "gather_c", after=[fox_started]).reshape(N_DEV, D)
    b_cols = lax.dynamic_slice_in_dim(ada_b, me * n_ada, n_ada, axis=1).reshape(L, 1, n_ada)
    mod_blk = _ada_mod(jnp.pad(c_all, ((0, ADA_ROWS - N_DEV), (0, 0))), ada_w, b_cols, "ada_mod")[:, :N_DEV]
    mod_all = _all_gather_small(mod_blk.reshape(-1, LANE), "gather_mod").reshape(N_DEV, L, N_DEV, n_ada)
    mod_mine = lax.dynamic_index_in_dim(mod_all, me, axis=2, keepdims=False)
    mod_mine = jnp.transpose(mod_mine, (1, 0, 2)).reshape(L, N_DEV * n_ada)
    mods = [[mod_mine[l, k * D:(k + 1) * D].reshape(1, D) for k in range(6)] for l in range(L)]

    P = dict(fox_b_f=fox_b_f[0], swa_sinks=swa_sinks[0], conv_b=ffn_conv_b,
             ln_mix_g=ln_mix_g, ln_mix_b=ln_mix_b, ln_ffn_g=ln_ffn_g, ln_ffn_b=ln_ffn_b)
    cw_rows = _all_gather_small(_pack_rows([ffn_conv_w])[0], "gather_conv_w", after=[mod_all])
    n_cw = ffn_conv_w.size
    cw_dev = cw_rows.reshape(N_DEV, -1)[:, :n_cw].reshape(N_DEV, L, 3, cu)
    P["conv_w"] = jnp.transpose(cw_dev, (1, 2, 0, 3)).reshape(L, 3, N_DEV * cu)

    def natural(g, pad_to=None):
        slabs = [g[k] for k in range(N_DEV)]
        if pad_to is not None:
            slabs.append(jnp.zeros((D, pad_to - N_DEV * g.shape[2]), g.dtype))
        return jnp.concatenate(slabs, axis=1)

    def forward_stage(group, after):
        s = gathers[group]
        s["fsend"], s["frecv"], s["bufs"], token = _gather_forward(s["recv"], s["bufs"], s["kinds"], s["shapes"], after,
                                                                   "gather_forward_" + group)
        for nxt in starts_after.get(group, ()):
            token = start_group(nxt, token)
        return token

    def fetch(group, stage, after):
        if stage == "forward":
            return forward_stage(group, after)
        if group == "fox":
            after = forward_stage(group, cw_rows)
        s = gathers.pop(group)
        first, second = _gather_wait(s["send"], s["recv"], s["local"], s["fsend"], s["frecv"], s["shards"], s["bufs"],
                                     s["kinds"], after, "gather_wait_" + group)
        if group == "fox":
            return natural(first, n_in_pad), second
        if group == "swa":
            return natural(first), second
        return first, second

    out, pending = {}, {}

    def columns_major(g_t, n):
        return g_t[:n].reshape(N_DEV, n // N_DEV, D)

    def transposed(a):
        return jnp.transpose(a, (0, 2, 1))

    big = dict(
        ffn_w_down1=("ffn_w_down", "rows", 1, (ffn_w_down, m_ffn_w_down, v_ffn_w_down)),
        ffn_w_up1=("ffn_w_up", "halves", 1, (ffn_w_up, m_ffn_w_up, v_ffn_w_up)),
        swa_w_o=("swa_w_o", "rows", 0, (swa_w_o, m_swa_w_o, v_swa_w_o)),
        swa_w_in=("swa_w_in", "major", 0, tuple(transposed(a) for a in (swa_w_in, m_swa_w_in, v_swa_w_in))),
        ffn_w_down0=("ffn_w_down", "rows", 0, (ffn_w_down, m_ffn_w_down, v_ffn_w_down)),
        ffn_w_up0=("ffn_w_up", "halves", 0, (ffn_w_up, m_ffn_w_up, v_ffn_w_up)),
        fox_w_o=("fox_w_o", "rows", 0, (fox_w_o, m_fox_w_o, v_fox_w_o)),
        fox_w_in=("fox_w_in", "major", 0, (fox_w_in, m_fox_w_in, v_fox_w_in)))
    finish_at = dict(swa_w_o=["ffn_w_down1"], ffn_w_up0=["ffn_w_up1", "swa_w_o", "swa_w_in"], fox_w_o=["ffn_w_down0"],
                     fox_w_in=["ffn_w_up0"])
    tail = {}

    def finish(name, after):
        send, recv, thru, lands = pending.pop(name)
        param, kind, layer, wmv = big[name]
        if name == "fox_w_in":
            one = tail["last_start"][0, 0] + 1.0
            wmv = tuple(transposed(a * one) for a in wmv)
        landed, = _scatter_wait(send, recv, thru, lands, [kind], after, "scatter_wait_" + name)
        res = _adam_sum(landed, *wmv, layer, out.get(param), "adam_" + name, by_cols=kind == "major")
        out[param] = [transposed(r) for r in res] if kind == "major" else res

    def on_grads(name, g, after=None):
        kind = big[name][1]
        src = columns_major(g, n_in if name == "fox_w_in" else g.shape[0]) if kind == "major" else g
        send, recv, thru, lands, token = _scatter_start([src], [kind], after, "scatter_start_" + name)
        pending[name] = (send, recv, thru, lands)
        for done in finish_at.get(name, ()):
            finish(done, [token])
        tail["last_start"] = token
        return token

    def on_small(small):
        packed, tail["spans"] = _pack_rows([small[k] for k in SMALL_ORDER])
        send, recv, thru, lands, token = _scatter_start([packed], ["whole"], None, "small_grads_start")
        tail["small"] = (send, recv, thru, lands)
        return token

    grad_x = _local_step(x[0], loss_target[0], positions[0], mods, fetch, P, on_grads, on_small)

    spans = tail["spans"]
    done_first = [out[k][1] for k in ("ffn_w_up", "ffn_w_down", "swa_w_in", "swa_w_o")]
    gathered, = _scatter_wait(*tail["small"], ["whole"], done_first, "small_grads_wait")
    totals = dict(zip(SMALL_ORDER, _unpack_rows(_sum_slots(gathered, tail["last_start"], "sum_small_grads"), spans)))
    loss = totals["loss"].reshape(())
    n_mod = L * 6 * D
    dmod_all = gathered.reshape(N_DEV, -1)[:, :n_mod].reshape(N_DEV, L, 6 * D)
    dmod_cols = jnp.transpose(lax.dynamic_slice_in_dim(dmod_all, me * n_ada, n_ada, axis=2), (1, 0, 2))
    out["ada_w"] = _ada_bwd_adam(c_all.T, dmod_cols, ada_w, m_ada_w, v_ada_w, "adam_ada_w")

    g_small = dict(fox_b_f=totals["fox_b_f"].reshape(fox_b_f.shape), swa_sinks=totals["swa_sinks"].reshape(swa_sinks.shape),
                   ada_b=totals["dmod"].reshape(ada_b.shape), ffn_conv_b=totals["conv_b"].reshape(ffn_conv_b.shape),
                   ffn_conv_w=lax.dynamic_slice_in_dim(totals["conv_w"].reshape(L, 3, 2 * F), me * cu, cu, axis=2),
                   ln_mix_g=totals["ln_mix_g"], ln_mix_b=totals["ln_mix_b"],
                   ln_ffn_g=totals["ln_ffn_g"], ln_ffn_b=totals["ln_ffn_b"])
    small_names = ("fox_b_f", "swa_sinks", "ada_b", "ffn_conv_b", "ffn_conv_w", "ln_mix_g", "ln_mix_b", "ln_ffn_g", "ln_ffn_b")
    w_small = dict(fox_b_f=(fox_b_f, m_fox_b_f, v_fox_b_f), swa_sinks=(swa_sinks, m_swa_sinks, v_swa_sinks),
                   ada_b=(ada_b, m_ada_b, v_ada_b), ffn_conv_b=(ffn_conv_b, m_ffn_conv_b, v_ffn_conv_b),
                   ffn_conv_w=(ffn_conv_w, m_ffn_conv_w, v_ffn_conv_w),
                   ln_mix_g=(ln_mix_g, m_ln_mix_g, v_ln_mix_g), ln_mix_b=(ln_mix_b, m_ln_mix_b, v_ln_mix_b),
                   ln_ffn_g=(ln_ffn_g, m_ln_ffn_g, v_ln_ffn_g), ln_ffn_b=(ln_ffn_b, m_ln_ffn_b, v_ln_ffn_b))
    pk_g, sp = _pack_rows([g_small[k] for k in small_names])
    pk_w = _pack_rows([w_small[k][0] for k in small_names])[0]
    pk_m = _pack_rows([w_small[k][1] for k in small_names])[0]
    pk_v = _pack_rows([w_small[k][2] for k in small_names])[0]
    res = _adam(pk_g[None], pk_w[None], pk_m[None], pk_v[None], "adam_small")
    settled = [res[0], out["ada_w"][1]] + [out[k][1] for k in ("ffn_w_up", "ffn_w_down", "swa_w_in", "swa_w_o")]
    finish("fox_w_o", settled)
    finish("fox_w_in", settled)
    res = [dict(zip(small_names, _unpack_rows(r[0], sp))) for r in res]
    for k in small_names:
        out[k] = (g_small[k], res[0][k], res[1][k], res[2][k])

    order = ("fox_w_in", "fox_b_f", "fox_w_o", "swa_w_in", "swa_sinks", "swa_w_o", "ada_w", "ada_b", "ffn_w_up",
             "ffn_conv_w", "ffn_conv_b", "ffn_w_down", "ln_mix_g", "ln_mix_b", "ln_ffn_g", "ln_ffn_b")
    return (loss, grad_x[None], *[out[k][0] for k in order], *[out[k][1] for k in order],
            *[out[k][2] for k in order], *[out[k][3] for k in order])
```

```python
import functools

import jax
import jax.numpy as jnp
from jax import lax
from jax.experimental import pallas as pl
from jax.experimental.pallas import tpu as pltpu

F32 = jnp.float32
BF16 = jnp.bfloat16
MESH = pl.DeviceIdType.MESH
N_DEV = 8
AXES = ("x", "y", "c")

DEPTH = 2
ALPHA = (2.0 * DEPTH) ** 0.25
LN_EPS = 1e-5
FOX_HEAD_DIM = 128
SWA_HEAD_DIM = 64
SWA_GROUP = 8
SWA_WINDOW = 128
Q_BLOCK = 128
ROPE_DIM = 16
ROPE_THETA = 500000.0

ADAM_LR = 0.001
ADAM_B1 = 0.9
ADAM_B2 = 0.999
ADAM_EPS = 1e-08
ADAM_WD = 0.01
ADAM_STEP = 10

LANE = 128
MIB = 1024 * 1024


def _tile(n, pref, unit=LANE):
    if n <= pref:
        return n
    t = (pref // unit) * unit
    while t >= unit:
        if n % t == 0:
            return t
        t -= unit
    return n


def _params(sem, vmem_mib=48):
    return pltpu.CompilerParams(dimension_semantics=sem, vmem_limit_bytes=vmem_mib * MIB)


def _sigmoid(x):
    return 1.0 / (1.0 + jnp.exp(-x))


def _mm_call(dot, grid_mnk, in_specs, out_spec, out_shape, k_axis, nk, tm, tn, name, operands):
    sem = ("parallel",) * (len(grid_mnk) - 1) + ("arbitrary",)

    if nk == 1:
        def body(a_ref, b_ref, o_ref):
            o_ref[...] = dot(a_ref[...], b_ref[...]).astype(o_ref.dtype)
        scratch = []
    else:
        def body(a_ref, b_ref, o_ref, acc_ref):
            k = pl.program_id(k_axis)

            @pl.when(k == 0)
            def _():
                acc_ref[...] = jnp.zeros_like(acc_ref)

            acc_ref[...] += dot(a_ref[...], b_ref[...])

            @pl.when(k == nk - 1)
            def _():
                o_ref[...] = acc_ref[...].astype(o_ref.dtype)
        scratch = [pltpu.VMEM((tm, tn), F32)]

    return pl.pallas_call(
        body, name=name, grid=grid_mnk, in_specs=in_specs, out_specs=out_spec, out_shape=out_shape,
        scratch_shapes=scratch, compiler_params=_params(sem, 56),
    )(*operands)


def _dot(dims):
    def dot(a, b):
        return lax.dot_general(a.astype(BF16), b.astype(BF16), (dims, ((), ())), preferred_element_type=F32)
    return dot


def _mm_nn(a, b, out_dtype, name, tm=2048, tn=512, tk=2048):
    M, K = a.shape
    N = b.shape[1]
    tm, tn, tk = _tile(M, tm), _tile(N, tn), _tile(K, tk)
    nk = K // tk
    return _mm_call(
        _dot(((1,), (0,))), (M // tm, N // tn, nk),
        [pl.BlockSpec((tm, tk), lambda i, j, k: (i, k)), pl.BlockSpec((tk, tn), lambda i, j, k: (k, j))],
        pl.BlockSpec((tm, tn), lambda i, j, k: (i, j)), jax.ShapeDtypeStruct((M, N), out_dtype),
        2, nk, tm, tn, name, (a, b))


def _mm_nt(a, b, out_dtype, name, tm=2048, tn=512, tk=2048):
    P, M, K = a.shape
    N = b.shape[1]
    tm, tn, tk = _tile(M, tm), _tile(N, tn), _tile(K, tk)
    nk = K // tk
    return _mm_call(
        _dot(((1,), (1,))), (M // tm, N // tn, P * nk),
        [pl.BlockSpec((None, tm, tk), lambda i, j, k: (k // nk, i, k % nk)),
         pl.BlockSpec((None, tn, tk), lambda i, j, k: (k // nk, j, k % nk))],
        pl.BlockSpec((tm, tn), lambda i, j, k: (i, j)), jax.ShapeDtypeStruct((M, N), out_dtype),
        2, P * nk, tm, tn, name, (a, b))


def _mm_tn(a, b, out_dtype, name, tm=2048, tn=512, tk=2048):
    K, M = a.shape
    P, _, N = b.shape
    tm, tn, tk = _tile(M, tm), _tile(N, tn), _tile(K, tk)
    nk = K // tk
    return _mm_call(
        _dot(((0,), (0,))), (P, M // tm, N // tn, nk),
        [pl.BlockSpec((tk, tm), lambda p, i, j, k: (k, i)), pl.BlockSpec((None, tk, tn), lambda p, i, j, k: (p, k, j))],
        pl.BlockSpec((None, tm, tn), lambda p, i, j, k: (p, i, j)), jax.ShapeDtypeStruct((P, M, N), out_dtype),
        3, nk, tm, tn, name, (a, b))


ROW_TILE = 256


def _row_spec(tm, D):
    return pl.BlockSpec((tm, D), lambda i: (i, 0))


def _vec_spec(D):
    return pl.BlockSpec((1, D), lambda i: (0, 0))


def _modulate(x, sc, sh, name):
    S, D = x.shape
    tm = _tile(S, ROW_TILE, 8)

    def body(x_ref, sc_ref, sh_ref, h_ref):
        h_ref[...] = (x_ref[...] * (1.0 + sc_ref[...]) + sh_ref[...]).astype(BF16)

    return pl.pallas_call(
        body, name=name, grid=(S // tm,),
        in_specs=[_row_spec(tm, D), _vec_spec(D), _vec_spec(D)],
        out_specs=_row_spec(tm, D),
        out_shape=jax.ShapeDtypeStruct((S, D), BF16),
        compiler_params=_params(("parallel",)),
    )(x, sc, sh)


def _layer_norm_rows(z, gamma, beta):
    mu = jnp.mean(z, axis=-1, keepdims=True)
    zc = z - mu
    var = jnp.mean(zc * zc, axis=-1, keepdims=True)
    return zc * lax.rsqrt(var + LN_EPS) * gamma + beta


def _ln_fwd(x, y, gate, gamma, beta, sc_n, sh_n, after, name):
    S, D = x.shape
    tm = _tile(S, ROW_TILE, 8)

    def body(x_ref, y_ref, gate_ref, g_ref, b_ref, sc_ref, sh_ref, after_ref, z_ref, xo_ref, hn_ref):
        z = ALPHA * x_ref[...] + (1.0 + gate_ref[...]) * y_ref[...]
        xo = _layer_norm_rows(z, g_ref[...], b_ref[...])
        z_ref[...] = z
        xo_ref[...] = xo
        hn_ref[...] = (xo * (1.0 + sc_ref[...]) + sh_ref[...]).astype(BF16)

    return pl.pallas_call(
        body, name=name, grid=(S // tm,),
        in_specs=[_row_spec(tm, D), _row_spec(tm, D)] + [_vec_spec(D)] * 5 + [pl.BlockSpec(memory_space=pl.ANY)],
        out_specs=[_row_spec(tm, D)] * 3,
        out_shape=[jax.ShapeDtypeStruct((S, D), F32), jax.ShapeDtypeStruct((S, D), F32),
                   jax.ShapeDtypeStruct((S, D), BF16)],
        compiler_params=_params(("parallel",)),
    )(x, y, gate, gamma, beta, sc_n, sh_n, after)


def _ln_fwd_loss(x, y, gate, gamma, beta, target, name):
    S, D = x.shape
    tm = _tile(S, ROW_TILE, 8)

    def body(x_ref, y_ref, gate_ref, g_ref, b_ref, t_ref, z_ref, dout_ref, loss_ref):
        @pl.when(pl.program_id(0) == 0)
        def _():
            loss_ref[...] = jnp.zeros_like(loss_ref)

        z = ALPHA * x_ref[...] + (1.0 + gate_ref[...]) * y_ref[...]
        xo = _layer_norm_rows(z, g_ref[...], b_ref[...])
        err = xo - t_ref[...]
        z_ref[...] = z
        dout_ref[...] = err * (1.0 / D)
        loss_ref[...] += (0.5 / D) * jnp.sum(err * err)

    return pl.pallas_call(
        body, name=name, grid=(S // tm,),
        in_specs=[_row_spec(tm, D), _row_spec(tm, D)] + [_vec_spec(D)] * 3 + [_row_spec(tm, D)],
        out_specs=[_row_spec(tm, D), _row_spec(tm, D), pl.BlockSpec((1, LANE), lambda i: (0, 0))],
        out_shape=[jax.ShapeDtypeStruct((S, D), F32), jax.ShapeDtypeStruct((S, D), F32),
                   jax.ShapeDtypeStruct((1, LANE), F32)],
        compiler_params=_params(("arbitrary",)),
    )(x, y, gate, gamma, beta, target)


def _ln_bwd(dout, z, y, gate, gamma, name):
    S, D = z.shape
    tm = _tile(S, ROW_TILE, 8)

    def body(dout_ref, z_ref, y_ref, gate_ref, g_ref, dz_ref, dy_ref, dg_ref, db_ref, dgate_ref):
        @pl.when(pl.program_id(0) == 0)
        def _():
            dg_ref[...] = jnp.zeros_like(dg_ref)
            db_ref[...] = jnp.zeros_like(db_ref)
            dgate_ref[...] = jnp.zeros_like(dgate_ref)

        z = z_ref[...]
        dout = dout_ref[...]
        mu = jnp.mean(z, axis=-1, keepdims=True)
        zc = z - mu
        var = jnp.mean(zc * zc, axis=-1, keepdims=True)
        rstd = lax.rsqrt(var + LN_EPS)
        xhat = zc * rstd
        dxhat = dout * g_ref[...]
        m1 = jnp.mean(dxhat, axis=-1, keepdims=True)
        m2 = jnp.mean(dxhat * xhat, axis=-1, keepdims=True)
        dz = rstd * (dxhat - m1 - xhat * m2)
        dz_ref[...] = dz
        dy_ref[...] = (dz * (1.0 + gate_ref[...])).astype(BF16)
        dg_ref[...] += jnp.sum(dout * xhat, axis=0, keepdims=True)
        db_ref[...] += jnp.sum(dout, axis=0, keepdims=True)
        dgate_ref[...] += jnp.sum(dz * y_ref[...], axis=0, keepdims=True)

    return pl.pallas_call(
        body, name=name, grid=(S // tm,),
        in_specs=[_row_spec(tm, D)] * 3 + [_vec_spec(D)] * 2,
        out_specs=[_row_spec(tm, D), _row_spec(tm, D)] + [_vec_spec(D)] * 3,
        out_shape=[jax.ShapeDtypeStruct((S, D), F32), jax.ShapeDtypeStruct((S, D), BF16)]
        + [jax.ShapeDtypeStruct((1, D), F32)] * 3,
        compiler_params=_params(("arbitrary",)),
    )(dout, z, y, gate, gamma)


def _mod_bwd(dz, dh, xin, sc, after, name):
    S, D = dz.shape
    tm = _tile(S, ROW_TILE, 8)

    def body(dz_ref, dh_ref, x_ref, sc_ref, after_ref, dx_ref, dsc_ref, dsh_ref):
        @pl.when(pl.program_id(0) == 0)
        def _():
            dsc_ref[...] = jnp.zeros_like(dsc_ref)
            dsh_ref[...] = jnp.zeros_like(dsh_ref)

        dh = dh_ref[...]
        dx_ref[...] = ALPHA * dz_ref[...] + dh * (1.0 + sc_ref[...])
        dsc_ref[...] += jnp.sum(dh * x_ref[...], axis=0, keepdims=True)
        dsh_ref[...] += jnp.sum(dh, axis=0, keepdims=True)

    return pl.pallas_call(
        body, name=name, grid=(S // tm,),
        in_specs=[_row_spec(tm, D)] * 3 + [_vec_spec(D), pl.BlockSpec(memory_space=pl.ANY)],
        out_specs=[_row_spec(tm, D), _vec_spec(D), _vec_spec(D)],
        out_shape=[jax.ShapeDtypeStruct((S, D), F32)] + [jax.ShapeDtypeStruct((1, D), F32)] * 2,
        compiler_params=_params(("arbitrary",)),
    )(dz, dh, xin, sc, after)


def _shift_down(u, k, row):
    return jnp.where(row >= k, pltpu.roll(u, k, axis=0), 0.0)


def _shift_up(u, k, row, S):
    return jnp.where(row < S - k, pltpu.roll(u, S - k, axis=0), 0.0)


def _ffn_up(h, w, cw, cb, name):
    S, D = h.shape
    F = w.shape[2]
    tn = _tile(F, 256)

    def body(h_ref, w_ref, cw_ref, cb_ref, u_ref, uc_ref, a_ref):
        hh = h_ref[...]
        row = lax.broadcasted_iota(jnp.int32, (S, tn), 0)
        conv = []
        for p in range(2):
            u = jnp.dot(hh, w_ref[p], preferred_element_type=F32)
            u_ref[p] = u.astype(BF16)
            cwp = cw_ref[p]
            uc = _shift_down(u, 2, row) * cwp[0:1] + _shift_down(u, 1, row) * cwp[1:2] + u * cwp[2:3] + cb_ref[p]
            uc_ref[p] = uc.astype(BF16)
            conv.append(uc)
        g, v = conv
        a_ref[...] = (g * _sigmoid(g) * v).astype(BF16)

    half = pl.BlockSpec((2, S, tn), lambda j: (0, 0, j))
    return pl.pallas_call(
        body, name=name, grid=(F // tn,),
        in_specs=[pl.BlockSpec((S, D), lambda j: (0, 0)), pl.BlockSpec((2, D, tn), lambda j: (0, 0, j)),
                  pl.BlockSpec((2, 3, tn), lambda j: (0, 0, j)), pl.BlockSpec((2, 1, tn), lambda j: (0, 0, j))],
        out_specs=[half, half, pl.BlockSpec((S, tn), lambda j: (0, j))],
        out_shape=[jax.ShapeDtypeStruct((2, S, F), BF16), jax.ShapeDtypeStruct((2, S, F), BF16),
                   jax.ShapeDtypeStruct((S, F), BF16)],
        compiler_params=_params(("parallel",), 56),
    )(h, w, cw, cb)


def _ffn_bwd_elem(da, u, uc, cw, after, name):
    _, S, F = u.shape
    tn = _tile(F, 256)

    def body(da_ref, u_ref, uc_ref, cw_ref, after_ref, du_ref, dcw_ref, dcb_ref):
        row = lax.broadcasted_iota(jnp.int32, (S, tn), 0)
        da = da_ref[...]
        g, v = uc_ref[0].astype(F32), uc_ref[1].astype(F32)
        sg = _sigmoid(g)
        d_conv = (da * v * (sg * (1.0 + g * (1.0 - sg))), da * (g * sg))
        for p in range(2):
            d = d_conv[p]
            cwp = cw_ref[p]
            u = u_ref[p].astype(F32)
            d1, d2 = _shift_up(d, 1, row, S), _shift_up(d, 2, row, S)
            dcb_ref[p] = jnp.sum(d, axis=0, keepdims=True)
            dcw_ref[p, 0:1, :] = jnp.sum(d2 * u, axis=0, keepdims=True)
            dcw_ref[p, 1:2, :] = jnp.sum(d1 * u, axis=0, keepdims=True)
            dcw_ref[p, 2:3, :] = jnp.sum(d * u, axis=0, keepdims=True)
            du_ref[p] = (d * cwp[2:3] + d1 * cwp[1:2] + d2 * cwp[0:1]).astype(BF16)

    half = pl.BlockSpec((2, S, tn), lambda j: (0, 0, j))
    return pl.pallas_call(
        body, name=name, grid=(F // tn,),
        in_specs=[pl.BlockSpec((S, tn), lambda j: (0, j)), half, half, pl.BlockSpec((2, 3, tn), lambda j: (0, 0, j)),
                  pl.BlockSpec(memory_space=pl.ANY)],
        out_specs=[half, pl.BlockSpec((2, 3, tn), lambda j: (0, 0, j)), pl.BlockSpec((2, 1, tn), lambda j: (0, 0, j))],
        out_shape=[jax.ShapeDtypeStruct((2, S, F), BF16), jax.ShapeDtypeStruct((2, 3, F), F32),
                   jax.ShapeDtypeStruct((2, 1, F), F32)],
        compiler_params=_params(("parallel",), 56),
    )(da, u, uc, cw, after)


def _split3(x):
    hi = x.astype(BF16)
    r1 = x - hi.astype(F32)
    mid = r1.astype(BF16)
    lo = (r1 - mid.astype(F32)).astype(BF16)
    return hi, mid, lo


def _tri_matmul(x, upper, S):
    tc = _tile(S, 512)
    parts = _split3(x)
    outs = []
    for b in range(S // tc):
        r = lax.broadcasted_iota(jnp.int32, (S, tc), 0)
        c = lax.broadcasted_iota(jnp.int32, (S, tc), 1) + b * tc
        tri = jnp.where((r <= c) if upper else (r >= c), 1.0, 0.0).astype(BF16)
        acc = jnp.dot(parts[0], tri, preferred_element_type=F32)
        acc += jnp.dot(parts[1], tri, preferred_element_type=F32)
        acc += jnp.dot(parts[2], tri, preferred_element_type=F32)
        outs.append(acc)
    return outs, tc


def _fox_prep(flT, bf, name):
    H, S = flT.shape

    def body(fl_ref, b_ref, cum_ref):
        zz = fl_ref[...] + b_ref[...]
        lf = jnp.minimum(zz, 0.0) - jnp.log(1.0 + jnp.exp(-jnp.abs(zz)))
        outs, tc = _tri_matmul(lf, True, S)
        for b, o in enumerate(outs):
            cum_ref[:, b * tc:(b + 1) * tc] = o

    return pl.pallas_call(
        body, name=name,
        in_specs=[pl.BlockSpec(memory_space=pltpu.VMEM)] * 2,
        out_specs=pl.BlockSpec(memory_space=pltpu.VMEM),
        out_shape=jax.ShapeDtypeStruct((H, S), F32),
        compiler_params=pltpu.CompilerParams(vmem_limit_bytes=48 * MIB),
    )(flT, bf)


def _fox_prep_bwd(dcum_key, dcum_query, flT, bf, name):
    H, S = flT.shape

    def body(dck_ref, dcq_ref, fl_ref, b_ref, dfl_ref, dbf_ref):
        zz = fl_ref[...] + b_ref[...]
        outs, tc = _tri_matmul(dck_ref[...] + dcq_ref[...], False, S)
        total = jnp.zeros((H, 1), F32)
        for b, o in enumerate(outs):
            dfl = o * _sigmoid(-zz[:, b * tc:(b + 1) * tc])
            dfl_ref[:, b * tc:(b + 1) * tc] = dfl
            total += jnp.sum(dfl, axis=1, keepdims=True)
        dbf_ref[...] = total

    return pl.pallas_call(
        body, name=name,
        in_specs=[pl.BlockSpec(memory_space=pltpu.VMEM)] * 4,
        out_specs=[pl.BlockSpec(memory_space=pltpu.VMEM)] * 2,
        out_shape=[jax.ShapeDtypeStruct((H, S), F32), jax.ShapeDtypeStruct((H, 1), F32)],
        compiler_params=pltpu.CompilerParams(vmem_limit_bytes=48 * MIB),
    )(dcum_key, dcum_query, flT, bf)


FOX_TQ = 256
FOX_TC = 512


def _fox_scores(q, k_ref, ck_ref, i, lo, n, tq):
    k = k_ref[lo:lo + n, :].astype(BF16)
    s = lax.dot_general(q, k, (((1,), (1,)), ((), ())), preferred_element_type=F32) - ck_ref[:, lo:lo + n]
    qpos = i * tq + lax.broadcasted_iota(jnp.int32, (tq, n), 0)
    kpos = lo + lax.broadcasted_iota(jnp.int32, (tq, n), 1)
    return jnp.where(kpos <= qpos, s, -jnp.inf)


FOX_FWD_SPLITS = 4


def _fox_attn_fwd_part(proj, cum_row, H, q_lo, q_hi, filled, name):
    S = proj.shape[0]
    dh = FOX_HEAD_DIM
    tq = _tile(q_hi - q_lo, 2 * FOX_TQ)
    first = q_lo // tq
    scale = dh ** -0.5
    n_keep = 0 if filled is None else 2

    def body(q_ref, k_ref, v_ref, ck_ref, *rest):
        o_ref, lse_ref = rest[n_keep:]
        q = (q_ref[...] * scale).astype(BF16)
        s = _fox_scores(q, k_ref, ck_ref, first + pl.program_id(1), 0, q_hi, tq)
        m = jnp.max(s, axis=-1, keepdims=True)
        p = jnp.exp(s - m)
        l = jnp.sum(p, axis=-1, keepdims=True)
        o = jnp.dot(p.astype(BF16), v_ref[...].astype(BF16), preferred_element_type=F32) / l
        o_ref[...] = o.astype(BF16)
        lse_ref[...] = m + jnp.log(l)

    return pl.pallas_call(
        body, name=name, grid=(H, (q_hi - q_lo) // tq),
        in_specs=[pl.BlockSpec((tq, dh), lambda h, i: (first + i, h)),
                  pl.BlockSpec((q_hi, dh), lambda h, i: (0, H + h)),
                  pl.BlockSpec((q_hi, dh), lambda h, i: (0, 2 * H + h)),
                  pl.BlockSpec((None, 1, q_hi), lambda h, i: (h, 0, 0))] + [pl.BlockSpec(memory_space=pl.ANY)] * n_keep,
        out_specs=[pl.BlockSpec((tq, dh), lambda h, i: (first + i, h)),
                   pl.BlockSpec((None, tq, 1), lambda h, i: (h, first + i, 0))],
        out_shape=[jax.ShapeDtypeStruct((S, H * dh), BF16), jax.ShapeDtypeStruct((H, S, 1), F32)],
        input_output_aliases={4 + k: k for k in range(n_keep)},
        compiler_params=_params(("parallel", "parallel")),
    )(proj, proj, proj, cum_row, *(filled or ()))


def _fox_attn_fwd(proj, cum_row, H, name):
    S = proj.shape[0]
    n = FOX_FWD_SPLITS if S % (FOX_FWD_SPLITS * FOX_TQ) == 0 else 1
    out = None
    for part in range(n):
        out = _fox_attn_fwd_part(proj, cum_row, H, part * S // n, (part + 1) * S // n, out, name + str(part))
    return out


def _fox_attn_bwd(proj, do, o, cum_row, lse, H, after, name):
    S = proj.shape[0]
    dh = FOX_HEAD_DIM
    tq = _tile(S, 2 * FOX_TQ)
    tc = _tile(S, FOX_TC)
    scale = dh ** -0.5

    def body(q_ref, k_ref, v_ref, do_ref, o_ref, ck_ref, lse_ref, after_ref,
             dq_ref, dk_ref, dv_ref, dck_ref, dcq_ref, acc_ref):
        i = pl.program_id(1)

        @pl.when(i == 0)
        def _():
            dk_ref[...] = jnp.zeros_like(dk_ref)
            dv_ref[...] = jnp.zeros_like(dv_ref)
            dck_ref[...] = jnp.zeros_like(dck_ref)

        acc_ref[...] = jnp.zeros_like(acc_ref)
        dcq_ref[...] = jnp.zeros_like(dcq_ref)
        q = (q_ref[...] * scale).astype(BF16)
        do_f = do_ref[...]
        do_b = do_f.astype(BF16)
        delta = jnp.sum(do_f * o_ref[...].astype(F32), axis=-1, keepdims=True)
        lse_q = lse_ref[...]
        for c in range(S // tc):
            @pl.when(c * tc <= i * tq + tq - 1)
            def _():
                rows = slice(c * tc, (c + 1) * tc)
                p = jnp.exp(_fox_scores(q, k_ref, ck_ref, i, c * tc, tc, tq) - lse_q)
                dp = lax.dot_general(do_b, v_ref[rows, :].astype(BF16), (((1,), (1,)), ((), ())),
                                     preferred_element_type=F32)
                ds = p * (dp - delta)
                ds_b = ds.astype(BF16)
                acc_ref[...] += jnp.dot(ds_b, k_ref[rows, :].astype(BF16), preferred_element_type=F32)
                dk_ref[rows, :] += lax.dot_general(ds_b, q, (((0,), (0,)), ((), ())), preferred_element_type=F32)
                dv_ref[rows, :] += lax.dot_general(p.astype(BF16), do_b, (((0,), (0,)), ((), ())),
                                                   preferred_element_type=F32)
                dck_ref[:, rows] -= jnp.sum(ds, axis=0, keepdims=True)
                dcq_ref[...] += jnp.sum(ds, axis=-1, keepdims=True)
        dq_ref[...] = (acc_ref[...] * scale).astype(BF16)

    W = H * dh
    return pl.pallas_call(
        body, name=name, grid=(H, S // tq),
        in_specs=[pl.BlockSpec((tq, dh), lambda h, i: (i, h)),
                  pl.BlockSpec((S, dh), lambda h, i: (0, H + h)),
                  pl.BlockSpec((S, dh), lambda h, i: (0, 2 * H + h)),
                  pl.BlockSpec((tq, dh), lambda h, i: (i, h)),
                  pl.BlockSpec((tq, dh), lambda h, i: (i, h)),
                  pl.BlockSpec((None, 1, S), lambda h, i: (h, 0, 0)),
                  pl.BlockSpec((None, tq, 1), lambda h, i: (h, i, 0)),
                  pl.BlockSpec(memory_space=pl.ANY)],
        out_specs=[pl.BlockSpec((tq, dh), lambda h, i: (i, h)),
                   pl.BlockSpec((S, dh), lambda h, i: (0, h)),
                   pl.BlockSpec((S, dh), lambda h, i: (0, h)),
                   pl.BlockSpec((None, 1, S), lambda h, i: (h, 0, 0)),
                   pl.BlockSpec((None, tq, 1), lambda h, i: (h, i, 0))],
        out_shape=[jax.ShapeDtypeStruct((S, W), BF16), jax.ShapeDtypeStruct((S, W), F32),
                   jax.ShapeDtypeStruct((S, W), F32), jax.ShapeDtypeStruct((H, 1, S), F32),
                   jax.ShapeDtypeStruct((H, S, 1), F32)],
        scratch_shapes=[pltpu.VMEM((tq, dh), F32)],
        compiler_params=_params(("parallel", "arbitrary")),
    )(proj, proj, proj, do, o, cum_row, lse, after)


def _rope(parts, tabs, out_dtype, name):
    S = parts[0][0].shape[0]
    widths = [w for _, _, w in parts]
    total = sum(widths)
    tm = _tile(S, ROW_TILE, 8)
    flags = [r for _, r, _ in parts]

    def body(*refs):
        in_refs = refs[:len(parts)]
        cos_ref, sa_ref, sb_ref, o_ref = refs[len(parts):]
        cos, sa, sb = cos_ref[...], sa_ref[...], sb_ref[...]
        off = 0
        for ref, rot, w in zip(in_refs, flags, widths):
            for j in range(w // LANE):
                t = ref[:, j * LANE:(j + 1) * LANE]
                if rot:
                    t = t * cos + pltpu.roll(t, LANE - ROPE_DIM // 2, axis=1) * sa + pltpu.roll(t, ROPE_DIM // 2, axis=1) * sb
                o_ref[:, off + j * LANE:off + (j + 1) * LANE] = t.astype(o_ref.dtype)
            off += w

    return pl.pallas_call(
        body, name=name, grid=(S // tm,),
        in_specs=[pl.BlockSpec((tm, w), lambda i: (i, 0)) for w in widths] + [_row_spec(tm, LANE)] * 3,
        out_specs=_row_spec(tm, total),
        out_shape=jax.ShapeDtypeStruct((S, total), out_dtype),
        compiler_params=_params(("parallel",)),
    )(*[a for a, _, _ in parts], *tabs)


def _swa_band(ref_p, ref_c, hk):
    dh = SWA_HEAD_DIM
    return jnp.concatenate([ref_p[:, hk * dh:(hk + 1) * dh], ref_c[:, hk * dh:(hk + 1) * dh]], axis=0).astype(BF16)


def _swa_bias(G):
    qi = jnp.arange(G * Q_BLOCK)[:, None] % Q_BLOCK
    kj = jnp.arange(2 * Q_BLOCK)[None, :]
    rel = qi + Q_BLOCK - kj
    window = (rel >= 0) & (rel < SWA_WINDOW)
    both = jnp.stack([window & (kj >= Q_BLOCK), window])
    return jnp.where(both, 0.0, -jnp.inf).astype(F32)


def _swa_stack(ref, hk, G):
    dh = SWA_HEAD_DIM
    return jnp.concatenate([ref[:, (hk * G + g) * dh:(hk * G + g + 1) * dh] for g in range(G)], axis=0)


def _swa_unstack(ref, stacked, hk, G):
    dh, QB = SWA_HEAD_DIM, Q_BLOCK
    for g in range(0, G, 2):
        c0 = (hk * G + g) * dh
        pair = jnp.concatenate([stacked[g * QB:(g + 1) * QB], stacked[(g + 1) * QB:(g + 2) * QB]], axis=1)
        ref[:, c0:c0 + 2 * dh] = pair.astype(ref.dtype)


def _swa_sink_rows(sink_ref, hk, G):
    return jnp.concatenate([jnp.broadcast_to(sink_ref[0:1, hk * G + g:hk * G + g + 1], (Q_BLOCK, 1)) for g in range(G)],
                           axis=0)


def _swa_attn_fwd(qk, proj, sinks, Hq, name):
    S = qk.shape[0]
    dh, G, QB = SWA_HEAD_DIM, SWA_GROUP, Q_BLOCK
    Hk = Hq // G
    Wq, Wk = Hq * dh, Hk * dh
    nb = S // QB
    scale = dh ** -0.5

    def body(q_ref, kp_ref, kc_ref, vp_ref, vc_ref, sink_ref, bias_ref, o_ref, lse_ref):
        n = pl.program_id(0)
        lane = lax.broadcasted_iota(jnp.int32, (QB, LANE), 1)
        first, second = pl.ds(0, QB), pl.ds(QB, QB)
        for sub, rows in enumerate((first, second)):
            bias = bias_ref[jnp.minimum(n, 1)] if sub == 0 else bias_ref[1]
            k_band = (kp_ref, kc_ref.at[first]) if sub == 0 else (kc_ref.at[first], kc_ref.at[second])
            v_band = (vp_ref, vc_ref.at[first]) if sub == 0 else (vc_ref.at[first], vc_ref.at[second])
            lse_tile = jnp.zeros((QB, LANE), F32)
            for hk in range(Hk):
                kb = _swa_band(*k_band, hk)
                vb = _swa_band(*v_band, hk)
                q = (_swa_stack(q_ref.at[rows], hk, G) * scale).astype(BF16)
                sk = _swa_sink_rows(sink_ref, hk, G)
                s = lax.dot_general(q, kb, (((1,), (1,)), ((), ())), preferred_element_type=F32) + bias
                m = jnp.maximum(jnp.max(s, axis=-1, keepdims=True), sk)
                p = jnp.exp(s - m)
                l = jnp.sum(p, axis=-1, keepdims=True) + jnp.exp(sk - m)
                o = jnp.dot(p.astype(BF16), vb, preferred_element_type=F32) / l
                lse = m + jnp.log(l)
                for g in range(G):
                    lse_tile = jnp.where(lane == hk * G + g, lse[g * QB:(g + 1) * QB], lse_tile)
                _swa_unstack(o_ref.at[rows], o, hk, G)
            lse_ref[rows, :] = lse_tile

    kcol, vcol = Wq // Wk, (Wq + Wk) // Wk
    before = lambda n: jnp.maximum(2 * n - 1, 0)
    return pl.pallas_call(
        body, name=name, grid=(nb // 2,),
        in_specs=[pl.BlockSpec((2 * QB, Wq), lambda n: (n, 0)),
                  pl.BlockSpec((QB, Wk), lambda n: (before(n), kcol)),
                  pl.BlockSpec((2 * QB, Wk), lambda n: (n, kcol)),
                  pl.BlockSpec((QB, Wk), lambda n: (before(n), vcol)),
                  pl.BlockSpec((2 * QB, Wk), lambda n: (n, vcol)),
                  pl.BlockSpec((1, LANE), lambda n: (0, 0)),
                  pl.BlockSpec((2, G * QB, 2 * QB), lambda n: (0, 0, 0))],
        out_specs=[pl.BlockSpec((2 * QB, Wq), lambda n: (n, 0)), pl.BlockSpec((2 * QB, LANE), lambda n: (n, 0))],
        out_shape=[jax.ShapeDtypeStruct((S, Wq), BF16), jax.ShapeDtypeStruct((S, LANE), F32)],
        compiler_params=_params(("parallel",)),
    )(qk, qk, qk, proj, proj, sinks, _swa_bias(G))


def _swa_attn_bwd(qk, proj, sinks, do, lse, Hq, after, name):
    S = qk.shape[0]
    dh, G, QB = SWA_HEAD_DIM, SWA_GROUP, Q_BLOCK
    Hk = Hq // G
    Wq, Wk = Hq * dh, Hk * dh
    nb = S // QB
    scale = dh ** -0.5

    def body(q_ref, kp_ref, kc_ref, vp_ref, vc_ref, sink_ref, do_ref, lse_ref, bias_ref, after_ref,
             dq_ref, dk_ref, dv_ref, dsink_ref, carry_k, carry_v):
        n = pl.program_id(0)

        @pl.when(n == 0)
        def _():
            dsink_ref[...] = jnp.zeros_like(dsink_ref)

        @pl.when(n < nb)
        def _():
            bias = bias_ref[...]
            lane = lax.broadcasted_iota(jnp.int32, (1, LANE), 1)
            dsink = jnp.zeros((1, LANE), F32)
            dk_heads, dv_heads = [], []
            for hk in range(Hk):
                kb = _swa_band(kp_ref, kc_ref, hk)
                vb = _swa_band(vp_ref, vc_ref, hk)
                q = (_swa_stack(q_ref, hk, G) * scale).astype(BF16)
                do_s = _swa_stack(do_ref, hk, G).astype(BF16)
                lse = jnp.concatenate([lse_ref[:, hk * G + g:hk * G + g + 1] for g in range(G)], axis=0)
                s = lax.dot_general(q, kb, (((1,), (1,)), ((), ())), preferred_element_type=F32) + bias
                p = jnp.exp(s - lse)
                p_sink = jnp.exp(_swa_sink_rows(sink_ref, hk, G) - lse)
                dp = lax.dot_general(do_s, vb, (((1,), (1,)), ((), ())), preferred_element_type=F32)
                delta = jnp.sum(p * dp, axis=-1, keepdims=True)
                ds_b = (p * (dp - delta)).astype(BF16)
                _swa_unstack(dq_ref, jnp.dot(ds_b, kb, preferred_element_type=F32) * scale, hk, G)
                dk_heads.append(lax.dot_general(ds_b, q, (((0,), (0,)), ((), ())), preferred_element_type=F32))
                dv_heads.append(lax.dot_general(p.astype(BF16), do_s, (((0,), (0,)), ((), ())), preferred_element_type=F32))
                sink_term = p_sink * delta
                for g in range(G):
                    dsink = jnp.where(lane == hk * G + g,
                                      -jnp.sum(sink_term[g * QB:(g + 1) * QB], axis=0, keepdims=True), dsink)
            dsink_ref[...] += dsink
            dk_all = jnp.concatenate(dk_heads, axis=1)
            dv_all = jnp.concatenate(dv_heads, axis=1)

            @pl.when(n > 0)
            def _():
                dk_ref[...] = carry_k[...] + dk_all[:QB]
                dv_ref[...] = carry_v[...] + dv_all[:QB]

            carry_k[...] = dk_all[QB:]
            carry_v[...] = dv_all[QB:]

        @pl.when(n == nb)
        def _():
            dk_ref[...] = carry_k[...]
            dv_ref[...] = carry_v[...]

    kcol, vcol = Wq // Wk, (Wq + Wk) // Wk
    cur = lambda n: jnp.minimum(n, nb - 1)
    prev = lambda n: jnp.maximum(jnp.minimum(n, nb - 1) - 1, 0)
    return pl.pallas_call(
        body, name=name, grid=(nb + 1,),
        in_specs=[pl.BlockSpec((QB, Wq), lambda n: (cur(n), 0)),
                  pl.BlockSpec((QB, Wk), lambda n: (prev(n), kcol)),
                  pl.BlockSpec((QB, Wk), lambda n: (cur(n), kcol)),
                  pl.BlockSpec((QB, Wk), lambda n: (prev(n), vcol)),
                  pl.BlockSpec((QB, Wk), lambda n: (cur(n), vcol)),
                  pl.BlockSpec((1, LANE), lambda n: (0, 0)),
                  pl.BlockSpec((QB, Wq), lambda n: (cur(n), 0)),
                  pl.BlockSpec((QB, LANE), lambda n: (cur(n), 0)),
                  pl.BlockSpec((None, G * QB, 2 * QB), lambda n: (jnp.minimum(n, 1), 0, 0)),
                  pl.BlockSpec(memory_space=pl.ANY)],
        out_specs=[pl.BlockSpec((QB, Wq), lambda n: (cur(n), 0)),
                   pl.BlockSpec((QB, Wk), lambda n: (jnp.maximum(n - 1, 0), 0)),
                   pl.BlockSpec((QB, Wk), lambda n: (jnp.maximum(n - 1, 0), 0)),
                   pl.BlockSpec((1, LANE), lambda n: (0, 0))],
        out_shape=[jax.ShapeDtypeStruct((S, Wq), F32), jax.ShapeDtypeStruct((S, Wk), F32),
                   jax.ShapeDtypeStruct((S, Wk), F32), jax.ShapeDtypeStruct((1, LANE), F32)],
        scratch_shapes=[pltpu.VMEM((QB, Wk), F32), pltpu.VMEM((QB, Wk), F32)],
        compiler_params=_params(("arbitrary",)),
    )(qk, qk, qk, proj, proj, sinks, do, lse, _swa_bias(G), after)


ADA_ROWS = 16


def _ada_mod(c_pad, w, b, name):
    L, D, N = w.shape
    tn = _tile(N, 512)

    def body(c_ref, w_ref, b_ref, o_ref):
        c = c_ref[...]
        c = c * _sigmoid(c)
        ch = c.astype(BF16)
        cl = (c - ch.astype(F32)).astype(BF16)
        ww = w_ref[...]
        wh = ww.astype(BF16)
        wl = (ww - wh.astype(F32)).astype(BF16)
        acc = jnp.dot(ch, wh, preferred_element_type=F32)
        acc += jnp.dot(ch, wl, preferred_element_type=F32)
        acc += jnp.dot(cl, wh, preferred_element_type=F32)
        o_ref[...] = acc + b_ref[...]

    return pl.pallas_call(
        body, name=name, grid=(L, N // tn),
        in_specs=[pl.BlockSpec((ADA_ROWS, D), lambda l, j: (0, 0)),
                  pl.BlockSpec((None, D, tn), lambda l, j: (l, 0, j)),
                  pl.BlockSpec((None, 1, tn), lambda l, j: (l, 0, j))],
        out_specs=pl.BlockSpec((None, ADA_ROWS, tn), lambda l, j: (l, 0, j)),
        out_shape=jax.ShapeDtypeStruct((L, ADA_ROWS, N), F32),
        compiler_params=_params(("parallel", "parallel")),
    )(c_pad, w, b)


def _ada_bwd_adam(cT, dm, w, m, v, name):
    D = cT.shape[0]
    L, B, N = dm.shape
    tm = _tile(D, 256)

    def body(c_ref, dm_ref, w_ref, m_ref, v_ref, g_ref, d_ref, mo_ref, vo_ref):
        c = c_ref[...]
        c = c * _sigmoid(c)
        dmv = dm_ref[...]
        g = c[:, 0:1] * dmv[0:1, :]
        for b in range(1, B):
            g += c[:, b:b + 1] * dmv[b:b + 1, :]
        delta, mn, vn = _adamw_math(w_ref[...], g, m_ref[...], v_ref[...])
        g_ref[...] = g
        d_ref[...] = delta
        mo_ref[...] = mn
        vo_ref[...] = vn

    spec = pl.BlockSpec((None, tm, N), lambda l, i: (l, i, 0))
    return pl.pallas_call(
        body, name=name, grid=(L, D // tm),
        in_specs=[pl.BlockSpec((tm, B), lambda l, i: (i, 0)), pl.BlockSpec((None, B, N), lambda l, i: (l, 0, 0)),
                  spec, spec, spec],
        out_specs=[spec] * 4,
        out_shape=[jax.ShapeDtypeStruct((L, D, N), F32)] * 4,
        compiler_params=_params(("parallel", "parallel")),
    )(cT, dm, w, m, v)


def _adamw_math(w, g, m, v):
    m = ADAM_B1 * m + (1.0 - ADAM_B1) * g
    v = ADAM_B2 * v + (1.0 - ADAM_B2) * (g * g)
    m_hat = m / (1.0 - ADAM_B1 ** ADAM_STEP)
    v_hat = v / (1.0 - ADAM_B2 ** ADAM_STEP)
    delta = -ADAM_LR * (m_hat / (jnp.sqrt(v_hat) + ADAM_EPS) + ADAM_WD * w)
    return delta, m, v


def _adam_rows(R, C):
    lanes = -(-C // LANE) * LANE
    return _tile(R, max(8, (262144 // lanes) // 8 * 8), 8)


def _adam_sum(recv, w, m, v, layer, filled, name, by_cols=False):
    P, R, C = recv.shape
    L = w.shape[0]
    n_keep = 0 if filled is None else 4
    if by_cols:
        tc = _tile(C, 256)
        grid, spec = (C // tc,), pl.BlockSpec((None, R, tc), lambda i: (layer, 0, i))
        recv_spec = pl.BlockSpec((P, R, tc), lambda i: (0, 0, i))
    else:
        tr = _adam_rows(R, C)
        grid, spec = (R // tr,), pl.BlockSpec((None, tr, C), lambda i: (layer, i, 0))
        recv_spec = pl.BlockSpec((P, tr, C), lambda i: (0, i, 0))

    def body(r_ref, w_ref, m_ref, v_ref, *rest):
        g_ref, d_ref, mo_ref, vo_ref = rest[n_keep:]
        g = r_ref[0].astype(F32)
        for p in range(1, P):
            g = g + r_ref[p].astype(F32)
        delta, mn, vn = _adamw_math(w_ref[...], g, m_ref[...], v_ref[...])
        g_ref[...] = g
        d_ref[...] = delta
        mo_ref[...] = mn
        vo_ref[...] = vn

    return pl.pallas_call(
        body, name=name, grid=grid,
        in_specs=[recv_spec, spec, spec, spec] + [pl.BlockSpec(memory_space=pl.ANY)] * n_keep,
        out_specs=[spec] * 4,
        out_shape=[jax.ShapeDtypeStruct((L, R, C), F32)] * 4,
        input_output_aliases={4 + k: k for k in range(n_keep)},
        compiler_params=_params(("parallel",)),
    )(recv, w, m, v, *(filled or ()))


def _adam(g, w, m, v, name):
    L, R, C = w.shape
    tr = _adam_rows(R, C)

    def body(g_ref, w_ref, m_ref, v_ref, d_ref, mo_ref, vo_ref):
        delta, mn, vn = _adamw_math(w_ref[...], g_ref[...], m_ref[...], v_ref[...])
        d_ref[...] = delta
        mo_ref[...] = mn
        vo_ref[...] = vn

    spec = pl.BlockSpec((None, tr, C), lambda l, i: (l, i, 0))
    return pl.pallas_call(
        body, name=name, grid=(L, R // tr),
        in_specs=[spec] * 4, out_specs=[spec] * 3,
        out_shape=[jax.ShapeDtypeStruct((L, R, C), F32)] * 3,
        compiler_params=_params(("parallel", "parallel")),
    )(g, w, m, v)


def _sum_slots(x, after, name):
    P, R, C = x.shape

    def body(x_ref, after_ref, o_ref):
        acc = x_ref[0]
        for p in range(1, P):
            acc = acc + x_ref[p]
        o_ref[...] = acc

    return pl.pallas_call(
        body, name=name,
        in_specs=[pl.BlockSpec(memory_space=pltpu.VMEM), pl.BlockSpec(memory_space=pl.ANY)],
        out_specs=pl.BlockSpec(memory_space=pltpu.VMEM),
        out_shape=jax.ShapeDtypeStruct((R, C), F32),
        compiler_params=pltpu.CompilerParams(vmem_limit_bytes=48 * MIB),
    )(x, after)


def _my_pos():
    return lax.axis_index("x"), lax.axis_index("y"), lax.axis_index("c")


def _all_gather_small(x, name, after=()):
    R, C = x.shape
    n_after = len(after)

    def body(x_ref, *rest):
        out_ref, send_sems, recv_sems = rest[n_after:]
        x_, y_, c_ = _my_pos()
        me, sibling = (x_, y_, c_), (x_, y_, 1 - c_)
        chips = [(1 - x_, y_), (x_, 1 - y_), (1 - x_, 1 - y_)]

        def slot(px, py, pc):
            return out_ref.at[4 * px + 2 * py + pc]

        def copy(k, block, to):
            return pltpu.make_async_remote_copy(
                src_ref=slot(*block), dst_ref=slot(*block), send_sem=send_sems.at[k], recv_sem=recv_sems.at[k],
                device_id=to, device_id_type=MESH)

        out_ref[4 * x_ + 2 * y_ + c_] = x_ref[...]
        first = [copy(0, me, sibling)] + [copy(1 + j, me, (*chip, c_)) for j, chip in enumerate(chips)]
        for cp in first:
            cp.start()
        passed = [copy(4 + j, (*chip, c_), sibling) for j, chip in enumerate(chips)]
        for j, chip in enumerate(chips):
            copy(1 + j, (*chip, c_), me).wait_recv()
            passed[j].start()
        copy(0, sibling, me).wait_recv()
        for j, chip in enumerate(chips):
            copy(4 + j, (*chip, 1 - c_), me).wait_recv()
        for cp in first + passed:
            cp.wait_send()

    return pl.pallas_call(
        body, name=name,
        in_specs=[pl.BlockSpec(memory_space=pltpu.VMEM)] + [pl.BlockSpec(memory_space=pl.ANY)] * n_after,
        out_specs=pl.BlockSpec(memory_space=pltpu.VMEM),
        out_shape=jax.ShapeDtypeStruct((N_DEV, R, C), x.dtype),
        scratch_shapes=[pltpu.SemaphoreType.DMA((7,)), pltpu.SemaphoreType.DMA((7,))],
        compiler_params=pltpu.CompilerParams(vmem_limit_bytes=48 * MIB),
    )(x, *after)


HBM_SPEC = pl.BlockSpec(memory_space=pltpu.HBM)
SEM_SPEC = pl.BlockSpec(memory_space=pltpu.SEMAPHORE)
ANY_SPEC = pl.BlockSpec(memory_space=pl.ANY)
SPLIT_EFFECT = pltpu.SideEffectType.DATAFLOW_SIDE_EFFECTING


def _in_hbm(a):
    return pltpu.with_memory_space_constraint(a, pltpu.HBM)


def _gathered_shape(a, kind):
    if kind == "major":
        return (N_DEV,) + a.shape
    if kind == "rows":
        return (N_DEV * a.shape[0], a.shape[1])
    return (2, a.shape[0], 4 * a.shape[1])


def _gather_slot(ref, kind, block, shard_shape):
    px, py, pc = block
    if kind == "major":
        return ref.at[4 * px + 2 * py + pc]
    if kind == "rows":
        r = shard_shape[0]
        return ref.at[pl.ds(pl.multiple_of((4 * px + 2 * py + pc) * r, r), r), :]
    cu = shard_shape[1]
    return ref.at[px, :, pl.ds(pl.multiple_of((2 * py + pc) * cu, cu), cu)]


def _gather_peers():
    x_, y_, c_ = _my_pos()
    return (x_, y_, c_), (x_, y_, 1 - c_), [(1 - x_, y_), (x_, 1 - y_), (1 - x_, 1 - y_)]


def _gather_start(shards, kinds, after, name):
    n = len(shards)
    bufs = [lax.empty(_gathered_shape(a, k), a.dtype) for a, k in zip(shards, kinds)]
    extra = [] if after is None else [after]

    def body(*refs):
        shard_refs, buf_refs = refs[:n], refs[n:2 * n]
        send_sems, recv_sems, local_sems = refs[2 * n + len(extra):2 * n + len(extra) + 3]
        token = refs[-1]
        me, sibling, chips = _gather_peers()
        for e in range(n):
            mine = _gather_slot(buf_refs[e], kinds[e], me, shards[e].shape)
            pltpu.make_async_copy(shard_refs[e], mine, local_sems.at[e]).start()
            for k, to in enumerate([sibling] + [(*chip, me[2]) for chip in chips]):
                pltpu.make_async_remote_copy(
                    src_ref=shard_refs[e], dst_ref=mine, send_sem=send_sems.at[4 * e + k],
                    recv_sem=recv_sems.at[4 * e + k], device_id=to, device_id_type=MESH).start()
        token[...] = jnp.zeros_like(token)

    out = pl.pallas_call(
        body, name=name,
        out_shape=(pltpu.SemaphoreType.DMA((4 * n,)), pltpu.SemaphoreType.DMA((4 * n,)), pltpu.SemaphoreType.DMA((n,)),
                   *[pltpu.HBM(a.shape, a.dtype) for a in shards], *[pltpu.HBM(a.shape, a.dtype) for a in bufs],
                   jax.ShapeDtypeStruct((8, LANE), F32)),
        in_specs=[HBM_SPEC] * (2 * n) + [ANY_SPEC] * len(extra),
        out_specs=(SEM_SPEC, SEM_SPEC, SEM_SPEC, *[HBM_SPEC] * (2 * n), pl.BlockSpec(memory_space=pltpu.VMEM)),
        input_output_aliases={i: 3 + i for i in range(2 * n)},
        compiler_params=pltpu.CompilerParams(has_side_effects=SPLIT_EFFECT),
    )(*[_in_hbm(a) for a in shards], *[_in_hbm(a) for a in bufs], *extra)
    return out[0], out[1], out[2], out[3:3 + n], out[3 + n:3 + 2 * n], out[-1]


def _gather_forward(recv_sems, bufs, kinds, shard_shapes, after, name):
    n = len(bufs)

    def body(*refs):
        buf_refs, recv_in = refs[:n], refs[n]
        fsend, frecv = refs[n + 2], refs[n + 3]
        token = refs[-1]
        me, sibling, chips = _gather_peers()
        for e in range(n):
            for j, chip in enumerate(chips):
                slot = _gather_slot(buf_refs[e], kinds[e], (*chip, me[2]), shard_shapes[e])
                pltpu.make_async_remote_copy(
                    src_ref=slot, dst_ref=slot, send_sem=recv_in.at[4 * e + 1 + j], recv_sem=recv_in.at[4 * e + 1 + j],
                    device_id=me, device_id_type=MESH).wait_recv()
                pltpu.make_async_remote_copy(
                    src_ref=slot, dst_ref=slot, send_sem=fsend.at[3 * e + j], recv_sem=frecv.at[3 * e + j],
                    device_id=sibling, device_id_type=MESH).start()
        token[...] = jnp.zeros_like(token)

    out = pl.pallas_call(
        body, name=name,
        out_shape=(pltpu.SemaphoreType.DMA((3 * n,)), pltpu.SemaphoreType.DMA((3 * n,)),
                   *[pltpu.HBM(a.shape, a.dtype) for a in bufs], jax.ShapeDtypeStruct((8, LANE), F32)),
        in_specs=[HBM_SPEC] * n + [SEM_SPEC, ANY_SPEC],
        out_specs=(SEM_SPEC, SEM_SPEC, *[HBM_SPEC] * n, pl.BlockSpec(memory_space=pltpu.VMEM)),
        input_output_aliases={i: 2 + i for i in range(n)},
        compiler_params=pltpu.CompilerParams(has_side_effects=SPLIT_EFFECT),
    )(*bufs, recv_sems, after)
    return out[0], out[1], out[2:2 + n], out[-1]


def _gather_wait(send_sems, recv_sems, local_sems, fsend, frecv, shards, bufs, kinds, after, name):
    n = len(bufs)

    def body(*refs):
        shard_refs, buf_refs = refs[:n], refs[n:2 * n]
        send_in, recv_in, local_in, fsend_in, frecv_in = refs[2 * n:2 * n + 5]
        me, sibling, chips = _gather_peers()

        def arrival(slot, sem):
            return pltpu.make_async_remote_copy(src_ref=slot, dst_ref=slot, send_sem=sem, recv_sem=sem,
                                                device_id=me, device_id_type=MESH)

        for e in range(n):
            shape = shards[e].shape
            mine = _gather_slot(buf_refs[e], kinds[e], me, shape)
            pltpu.make_async_copy(shard_refs[e], mine, local_in.at[e]).wait()
            arrival(_gather_slot(buf_refs[e], kinds[e], sibling, shape), recv_in.at[4 * e]).wait_recv()
            for j, chip in enumerate(chips):
                arrival(_gather_slot(buf_refs[e], kinds[e], (*chip, 1 - me[2]), shape), frecv_in.at[3 * e + j]).wait_recv()
            for k in range(4):
                arrival(mine, send_in.at[4 * e + k]).wait_send()
            for j in range(3):
                arrival(mine, fsend_in.at[3 * e + j]).wait_send()

    out = pl.pallas_call(
        body, name=name,
        out_shape=(*[pltpu.HBM(a.shape, a.dtype) for a in shards], *[pltpu.HBM(a.shape, a.dtype) for a in bufs]),
        in_specs=[HBM_SPEC] * (2 * n) + [SEM_SPEC] * 5 + [ANY_SPEC],
        out_specs=tuple([HBM_SPEC] * (2 * n)),
        input_output_aliases={i: i for i in range(2 * n)},
        compiler_params=pltpu.CompilerParams(has_side_effects=SPLIT_EFFECT),
    )(*shards, *bufs, send_sems, recv_sems, local_sems, fsend, frecv, after)
    return out[n:]


def _grad_slice(ref, kind, j):
    if kind == "whole":
        return ref
    if kind == "major":
        return ref.at[j]
    if kind == "rows":
        r = ref.shape[0] // N_DEV
        return ref.at[pl.ds(j * r, r), :]
    cu = ref.shape[2] // 4
    return ref.at[j // 4, :, pl.ds((j % 4) * cu, cu)]


def _slice_shape(a, kind):
    if kind == "whole":
        return a.shape
    if kind == "major":
        return a.shape[1:]
    if kind == "rows":
        return (a.shape[0] // N_DEV, a.shape[1])
    return (a.shape[1], a.shape[2] // 4)


def _scatter_copies(srcs, lands, kinds, send_sems, recv_sems):
    x_, y_, c_ = _my_pos()
    me = 4 * x_ + 2 * y_ + c_
    n = len(srcs)

    def remote(e, j):
        return pltpu.make_async_remote_copy(
            src_ref=_grad_slice(srcs[e], kinds[e], j), dst_ref=lands[e].at[me],
            send_sem=send_sems.at[e * N_DEV + j], recv_sem=recv_sems.at[e * N_DEV + me],
            device_id=(j // 4, (j // 2) % 2, j % 2), device_id_type=MESH)

    def local(e, j):
        return pltpu.make_async_copy(_grad_slice(srcs[e], kinds[e], j), lands[e].at[j], recv_sems.at[e * N_DEV + j])

    def arrival(e, i):
        return pltpu.make_async_remote_copy(
            src_ref=_grad_slice(srcs[e], kinds[e], i), dst_ref=lands[e].at[i],
            send_sem=send_sems.at[e * N_DEV + i], recv_sem=recv_sems.at[e * N_DEV + i],
            device_id=(i // 4, (i // 2) % 2, i % 2), device_id_type=MESH)

    def start():
        for e in range(n):
            for j in range(N_DEV):
                @pl.when(me == j)
                def _():
                    local(e, j).start()

                @pl.when(me != j)
                def _():
                    remote(e, j).start()

    def wait():
        for e in range(n):
            for i in range(N_DEV):
                @pl.when(me == i)
                def _():
                    local(e, i).wait()

                @pl.when(me != i)
                def _():
                    arrival(e, i).wait_recv()
        for e in range(n):
            for j in range(N_DEV):
                @pl.when(me != j)
                def _():
                    remote(e, j).wait_send()

    return start, wait


def _scatter_start(srcs, kinds, after, name):
    n = len(srcs)
    lands = [lax.empty((N_DEV,) + _slice_shape(a, k), a.dtype) for a, k in zip(srcs, kinds)]
    extra = [] if after is None else [after]

    def body(*refs):
        src_refs, land_refs = refs[:n], refs[n:2 * n]
        send_sems, recv_sems = refs[2 * n + len(extra)], refs[2 * n + len(extra) + 1]
        token = refs[-1]
        start, _ = _scatter_copies(src_refs, land_refs, kinds, send_sems, recv_sems)
        start()
        token[...] = jnp.zeros_like(token)

    out = pl.pallas_call(
        body, name=name,
        out_shape=(pltpu.SemaphoreType.DMA((n * N_DEV,)), pltpu.SemaphoreType.DMA((n * N_DEV,)),
                   *[pltpu.HBM(a.shape, a.dtype) for a in srcs], *[pltpu.HBM(a.shape, a.dtype) for a in lands],
                   jax.ShapeDtypeStruct((8, LANE), F32)),
        in_specs=[HBM_SPEC] * (2 * n) + [ANY_SPEC] * len(extra),
        out_specs=(SEM_SPEC, SEM_SPEC, *[HBM_SPEC] * (2 * n), pl.BlockSpec(memory_space=pltpu.VMEM)),
        input_output_aliases={i: 2 + i for i in range(2 * n)},
        compiler_params=pltpu.CompilerParams(has_side_effects=SPLIT_EFFECT),
    )(*[_in_hbm(a) for a in srcs], *[_in_hbm(a) for a in lands], *extra)
    return out[0], out[1], out[2:2 + n], out[2 + n:2 + 2 * n], out[-1]


def _scatter_wait(send_sems, recv_sems, srcs, lands, kinds, after, name):
    n = len(srcs)

    def body(*refs):
        src_refs, land_refs = refs[:n], refs[n:2 * n]
        _, wait = _scatter_copies(src_refs, land_refs, kinds, refs[2 * n], refs[2 * n + 1])
        wait()

    out = pl.pallas_call(
        body, name=name,
        out_shape=(*[pltpu.HBM(a.shape, a.dtype) for a in srcs], *[pltpu.HBM(a.shape, a.dtype) for a in lands]),
        in_specs=[HBM_SPEC] * (2 * n) + [SEM_SPEC, SEM_SPEC] + [ANY_SPEC] * len(after),
        out_specs=tuple([HBM_SPEC] * (2 * n)),
        input_output_aliases={i: i for i in range(2 * n)},
        compiler_params=pltpu.CompilerParams(has_side_effects=SPLIT_EFFECT),
    )(*srcs, *lands, send_sems, recv_sems, *after)
    return out[n:]


def _rope_tables(positions, sign):
    half = ROPE_DIM // 2
    inv_freq = ROPE_THETA ** (-jnp.arange(0, ROPE_DIM, 2, dtype=F32) / ROPE_DIM)
    ang = positions.astype(F32)[:, None] * inv_freq
    reps = LANE // half
    cos = jnp.tile(jnp.cos(ang), (1, reps))
    sin = jnp.tile(jnp.sin(ang), (1, reps)) * sign
    d = jnp.arange(LANE) % SWA_HEAD_DIM
    return (jnp.where(d < ROPE_DIM, cos, 1.0), jnp.where(d < half, -sin, 0.0),
            jnp.where((d >= half) & (d < ROPE_DIM), sin, 0.0))


def _pad_cols(a, n):
    return jnp.pad(a, ((0, 0), (0, n - a.shape[1])))


def _local_step(x, target, positions, mods, fetch, P, on_grads, on_small):
    S, D = x.shape
    Hf = D // FOX_HEAD_DIM
    Hq = D // SWA_HEAD_DIM
    Hk = Hq // SWA_GROUP
    Wk = Hk * SWA_HEAD_DIM
    n_in = 3 * D + Hf
    (sh1a, sc1a, g1a, sh2a, sc2a, g2a), (sh1b, sc1b, g1b, sh2b, sc2b, g2b) = mods
    row = lambda v: v.reshape(1, -1)
    cw = [jnp.transpose(P["conv_w"][l].reshape(3, 2, -1), (1, 0, 2)) for l in range(2)]
    cb = [P["conv_b"][l].reshape(2, 1, -1) for l in range(2)]

    W = dict(up=[None, None], down=[None, None])
    h1a = _modulate(x, sc1a, sh1a, "modulate_in")
    W["fox_in"], W["fox_o"] = fetch("fox", "wait", h1a)
    proj_a = _mm_nn(h1a, W["fox_in"], F32, "fox_in_proj", tn=896)
    flT = proj_a[:, 3 * D:n_in].T
    bf_col = P["fox_b_f"].reshape(Hf, 1)
    cumT = _fox_prep(flT, bf_col, "fox_cumsum")
    cum_row = cumT.reshape(Hf, 1, S)
    o_a, lse_a = _fox_attn_fwd(proj_a, cum_row, Hf, "fox_attn_fwd")
    token = fetch("ffn0", "forward", o_a)
    y1a = _mm_nn(o_a, W["fox_o"], F32, "fox_out_proj")
    z1a, x1, h2a = _ln_fwd(x, y1a, g1a, row(P["ln_mix_g"][0]), row(P["ln_mix_b"][0]), sc2a, sh2a, token, "ln_mix0")
    W["up"][0], W["down"][0] = fetch("ffn0", "wait", h2a)
    u_a, uc_a, a_a = _ffn_up(h2a, W["up"][0], cw[0], cb[0], "ffn_up0")
    token = fetch("swa", "forward", a_a)
    y2a = _mm_nn(a_a, W["down"][0], F32, "ffn_down0", tk=2816)
    z2a, x2, h1b = _ln_fwd(x1, y2a, g2a, row(P["ln_ffn_g"][0]), row(P["ln_ffn_b"][0]), sc1b, sh1b, token, "ln_ffn0")

    W["swa_in"], W["swa_o"] = fetch("swa", "wait", h1b)
    proj_b = _mm_nn(h1b, W["swa_in"], F32, "swa_in_proj")
    tabs_f = _rope_tables(positions, 1.0)
    tabs_b = _rope_tables(positions, -1.0)
    qk = _rope([(proj_b, True, D + Wk)], tabs_f, F32, "rope_fwd")
    sinks = _pad_cols(P["swa_sinks"].reshape(1, Hq), LANE)
    o_b, lse_b = _swa_attn_fwd(qk, proj_b, sinks, Hq, "swa_attn_fwd")
    token = fetch("ffn1", "forward", o_b)
    y1b = _mm_nn(o_b, W["swa_o"], F32, "swa_out_proj")
    z1b, x3, h2b = _ln_fwd(x2, y1b, g1b, row(P["ln_mix_g"][1]), row(P["ln_mix_b"][1]), sc2b, sh2b, token, "ln_mix1")
    W["up"][1], W["down"][1] = fetch("ffn1", "wait", h2b)
    u_b, uc_b, a_b = _ffn_up(h2b, W["up"][1], cw[1], cb[1], "ffn_up1")
    y2b = _mm_nn(a_b, W["down"][1], F32, "ffn_down1", tk=2816)
    z2b, dout, loss_row = _ln_fwd_loss(x3, y2b, g2b, row(P["ln_ffn_g"][1]), row(P["ln_ffn_b"][1]), target, "ln_ffn1_loss")

    def ffn_backward(dy, a, u, uc, h_in, l, tag):
        d_down = _mm_tn(a, dy[None], BF16, "ffn_dwdown" + tag, tm=1408, tn=1024)[0]
        token = on_grads("ffn_w_down" + tag, d_down)
        da = _mm_nt(dy[None], W["down"][l][None], F32, "ffn_da" + tag)
        du, dcw, dcb = _ffn_bwd_elem(da, u, uc, cw[l], token, "ffn_bwd_elem" + tag)
        d_up = _mm_tn(h_in, du, BF16, "ffn_dwup" + tag)
        token = on_grads("ffn_w_up" + tag, d_up)
        dh = _mm_nt(du, W["up"][l], F32, "ffn_dh" + tag, tn=1024, tk=1408)
        return dh, token, jnp.transpose(dcw, (1, 0, 2)).reshape(3, -1), dcb.reshape(-1)

    dz2b, dy2b, dg_f1, db_f1, dgate2b = _ln_bwd(dout, z2b, y2b, g2b, row(P["ln_ffn_g"][1]), "ln_ffn1_bwd")
    dh2b, token, dcw1, dcb1 = ffn_backward(dy2b, a_b, u_b, uc_b, h2b, 1, "1")
    dx3, dsc2b, dsh2b = _mod_bwd(dz2b, dh2b, x3, sc2b, token, "mod_ffn1_bwd")

    dz1b, dy1b, dg_m1, db_m1, dgate1b = _ln_bwd(dx3, z1b, y1b, g1b, row(P["ln_mix_g"][1]), "ln_mix1_bwd")
    token = on_grads("swa_w_o", _mm_tn(o_b, dy1b[None], BF16, "swa_dwo")[0])
    do_b = _mm_nt(dy1b[None], W["swa_o"][None], F32, "swa_do")
    dq_b, dk_b, dv_b, dsinks = _swa_attn_bwd(qk, proj_b, sinks, do_b, lse_b, Hq, token, "swa_attn_bwd")
    dproj_b = _rope([(dq_b, True, D), (dk_b, True, Wk), (dv_b, False, Wk)], tabs_b, BF16, "rope_bwd")
    token = on_grads("swa_w_in", _mm_tn(dproj_b, h1b[None], BF16, "swa_dwin", tm=1280, tn=1024)[0])
    dh1b = _mm_nt(dproj_b[None], W["swa_in"][None], F32, "swa_dh", tk=1280)
    dx2, dsc1b, dsh1b = _mod_bwd(dz1b, dh1b, x2, sc1b, token, "mod_mix1_bwd")

    dz2a, dy2a, dg_f0, db_f0, dgate2a = _ln_bwd(dx2, z2a, y2a, g2a, row(P["ln_ffn_g"][0]), "ln_ffn0_bwd")
    dh2a, token, dcw0, dcb0 = ffn_backward(dy2a, a_a, u_a, uc_a, h2a, 0, "0")
    dx1, dsc2a, dsh2a = _mod_bwd(dz2a, dh2a, x1, sc2a, token, "mod_ffn0_bwd")

    dz1a, dy1a, dg_m0, db_m0, dgate1a = _ln_bwd(dx1, z1a, y1a, g1a, row(P["ln_mix_g"][0]), "ln_mix0_bwd")
    token = on_grads("fox_w_o", _mm_tn(o_a, dy1a[None], BF16, "fox_dwo")[0])
    do_a = _mm_nt(dy1a[None], W["fox_o"][None], F32, "fox_do")
    dq_a, dk_a, dv_a, dcum_row, dcum_col = _fox_attn_bwd(proj_a, do_a, o_a, cum_row, lse_a, Hf, token, "fox_attn_bwd")
    dflT, dbf = _fox_prep_bwd(dcum_row.reshape(Hf, S), dcum_col.reshape(Hf, S), flT, bf_col, "fox_cumsum_bwd")
    n_pad = W["fox_in"].shape[1]
    dproj_a = jnp.concatenate([dq_a, dk_a.astype(BF16), dv_a.astype(BF16),
                               _pad_cols(dflT.T, n_pad - 3 * D).astype(BF16)], axis=1)
    dh1a = _mm_nt(dproj_a[None], W["fox_in"][None], F32, "fox_dh", tn=1024, tk=896)
    grad_x, dsc1a, dsh1a = _mod_bwd(dz1a, dh1a, x, sc1a, token, "mod_mix0_bwd")

    dmod = jnp.stack([jnp.concatenate([dsh1a, dsc1a, dgate1a, dsh2a, dsc2a, dgate2a], axis=1)[0],
                      jnp.concatenate([dsh1b, dsc1b, dgate1b, dsh2b, dsc2b, dgate2b], axis=1)[0]])
    small = dict(dmod=dmod, conv_b=jnp.stack([dcb0, dcb1]), conv_w=jnp.stack([dcw0, dcw1]),
                 ln_mix_g=jnp.concatenate([dg_m0, dg_m1]), ln_mix_b=jnp.concatenate([db_m0, db_m1]),
                 ln_ffn_g=jnp.concatenate([dg_f0, dg_f1]), ln_ffn_b=jnp.concatenate([db_f0, db_f1]),
                 fox_b_f=dbf.reshape(-1), swa_sinks=dsinks[0, :Hq], loss=loss_row[0, 0].reshape(1))
    exchanged = on_small(small)
    on_grads("fox_w_in", _mm_tn(dproj_a, h1a[None], BF16, "fox_dwin", tm=896, tn=1024)[0], exchanged)
    return grad_x


SMALL_ORDER = ("dmod", "conv_b", "conv_w", "ln_mix_g", "ln_mix_b", "ln_ffn_g", "ln_ffn_b", "fox_b_f", "swa_sinks", "loss")


def _pack_rows(arrays):
    chunks, spans, off = [], [], 0
    for a in arrays:
        flat = a.reshape(-1)
        n = -(-flat.shape[0] // LANE) * LANE
        chunks.append(jnp.pad(flat, (0, n - flat.shape[0])))
        spans.append((off, flat.shape[0], a.shape))
        off += n
    total = -(-off // (8 * LANE)) * (8 * LANE)
    chunks.append(jnp.zeros((total - off,), F32))
    return jnp.concatenate(chunks).reshape(-1, LANE), spans


def _unpack_rows(packed, spans):
    flat = packed.reshape(-1)
    return [flat[off:off + n].reshape(shape) for off, n, shape in spans]


def kernel(x, c, positions, fox_w_in, fox_b_f, fox_w_o, swa_w_in, swa_sinks, swa_w_o, ada_w, ada_b, ffn_w_up, ffn_conv_w, ffn_conv_b, ffn_w_down, ln_mix_g, ln_mix_b, ln_ffn_g, ln_ffn_b, loss_target, m_fox_w_in, m_fox_b_f, m_fox_w_o, m_swa_w_in, m_swa_sinks, m_swa_w_o, m_ada_w, m_ada_b, m_ffn_w_up, m_ffn_conv_w, m_ffn_conv_b, m_ffn_w_down, m_ln_mix_g, m_ln_mix_b, m_ln_ffn_g, m_ln_ffn_b, v_fox_w_in, v_fox_b_f, v_fox_w_o, v_swa_w_in, v_swa_sinks, v_swa_w_o, v_ada_w, v_ada_b, v_ffn_w_up, v_ffn_conv_w, v_ffn_conv_b, v_ffn_w_down, v_ln_mix_g, v_ln_mix_b, v_ln_ffn_g, v_ln_ffn_b):
    S, D = x.shape[1], x.shape[2]
    L = ada_w.shape[0]
    me = 4 * lax.axis_index("x") + 2 * lax.axis_index("y") + lax.axis_index("c")
    n_ada = ada_w.shape[2]
    cu = ffn_w_up.shape[2]
    F = 4 * cu
    n_in = fox_w_in.shape[2] * N_DEV
    n_in_pad = -(-n_in // LANE) * LANE

    gather_groups = dict(
        fox=([fox_w_in[0].astype(BF16), fox_w_o[0].astype(BF16)], ["major", "rows"]),
        ffn0=([ffn_w_up[0].astype(BF16), ffn_w_down[0].astype(BF16)], ["halves", "rows"]),
        swa=([swa_w_in[0].astype(BF16), swa_w_o[0].astype(BF16)], ["major", "rows"]),
        ffn1=([ffn_w_up[1].astype(BF16), ffn_w_down[1].astype(BF16)], ["halves", "rows"]))
    starts_after = dict(fox=["ffn0", "swa"], ffn0=["ffn1"])
    gathers = {}

    def start_group(group, after):
        shards, kinds = gather_groups[group]
        send, recv, local, thru, bufs, token = _gather_start(shards, kinds, after, "gather_start_" + group)
        gathers[group] = dict(send=send, recv=recv, local=local, shards=thru, bufs=bufs, kinds=kinds,
                              shapes=[a.shape for a in shards], token=token)
        return token

    fox_started = start_group("fox", None)

    c_all = _all_gather_small(c.reshape(-1, LANE), "gather_c", after=[fox_started]).reshape(N_DEV, D)
    b_cols = lax.dynamic_slice_in_dim(ada_b, me * n_ada, n_ada, axis=1).reshape(L, 1, n_ada)
    mod_blk = _ada_mod(jnp.pad(c_all, ((0, ADA_ROWS - N_DEV), (0, 0))), ada_w, b_cols, "ada_mod")[:, :N_DEV]
    mod_all = _all_gather_small(mod_blk.reshape(-1, LANE), "gather_mod").reshape(N_DEV, L, N_DEV, n_ada)
    mod_mine = lax.dynamic_index_in_dim(mod_all, me, axis=2, keepdims=False)
    mod_mine = jnp.transpose(mod_mine, (1, 0, 2)).reshape(L, N_DEV * n_ada)
    mods = [[mod_mine[l, k * D:(k + 1) * D].reshape(1, D) for k in range(6)] for l in range(L)]

    P = dict(fox_b_f=fox_b_f[0], swa_sinks=swa_sinks[0], conv_b=ffn_conv_b,
             ln_mix_g=ln_mix_g, ln_mix_b=ln_mix_b, ln_ffn_g=ln_ffn_g, ln_ffn_b=ln_ffn_b)
    cw_rows = _all_gather_small(_pack_rows([ffn_conv_w])[0], "gather_conv_w", after=[mod_all])
    n_cw = ffn_conv_w.size
    cw_dev = cw_rows.reshape(N_DEV, -1)[:, :n_cw].reshape(N_DEV, L, 3, cu)
    P["conv_w"] = jnp.transpose(cw_dev, (1, 2, 0, 3)).reshape(L, 3, N_DEV * cu)

    def natural(g, pad_to=None):
        slabs = [g[k] for k in range(N_DEV)]
        if pad_to is not None:
            slabs.append(jnp.zeros((D, pad_to - N_DEV * g.shape[2]), g.dtype))
        return jnp.concatenate(slabs, axis=1)

    def forward_stage(group, after):
        s = gathers[group]
        s["fsend"], s["frecv"], s["bufs"], token = _gather_forward(s["recv"], s["bufs"], s["kinds"], s["shapes"], after,
                                                                   "gather_forward_" + group)
        for nxt in starts_after.get(group, ()):
            token = start_group(nxt, token)
        return token

    def fetch(group, stage, after):
        if stage == "forward":
            return forward_stage(group, after)
        if group == "fox":
            after = forward_stage(group, cw_rows)
        s = gathers.pop(group)
        first, second = _gather_wait(s["send"], s["recv"], s["local"], s["fsend"], s["frecv"], s["shards"], s["bufs"],
                                     s["kinds"], after, "gather_wait_" + group)
        if group == "fox":
            return natural(first, n_in_pad), second
        if group == "swa":
            return natural(first), second
        return first, second

    out, pending = {}, {}

    def columns_major(g_t, n):
        return g_t[:n].reshape(N_DEV, n // N_DEV, D)

    def transposed(a):
        return jnp.transpose(a, (0, 2, 1))

    big = dict(
        ffn_w_down1=("ffn_w_down", "rows", 1, (ffn_w_down, m_ffn_w_down, v_ffn_w_down)),
        ffn_w_up1=("ffn_w_up", "halves", 1, (ffn_w_up, m_ffn_w_up, v_ffn_w_up)),
        swa_w_o=("swa_w_o", "rows", 0, (swa_w_o, m_swa_w_o, v_swa_w_o)),
        swa_w_in=("swa_w_in", "major", 0, tuple(transposed(a) for a in (swa_w_in, m_swa_w_in, v_swa_w_in))),
        ffn_w_down0=("ffn_w_down", "rows", 0, (ffn_w_down, m_ffn_w_down, v_ffn_w_down)),
        ffn_w_up0=("ffn_w_up", "halves", 0, (ffn_w_up, m_ffn_w_up, v_ffn_w_up)),
        fox_w_o=("fox_w_o", "rows", 0, (fox_w_o, m_fox_w_o, v_fox_w_o)),
        fox_w_in=("fox_w_in", "major", 0, (fox_w_in, m_fox_w_in, v_fox_w_in)))
    finish_at = dict(swa_w_o=["ffn_w_down1"], ffn_w_up0=["ffn_w_up1", "swa_w_o", "swa_w_in"], fox_w_o=["ffn_w_down0"],
                     fox_w_in=["ffn_w_up0"])
    tail = {}

    def finish(name, after):
        send, recv, thru, lands = pending.pop(name)
        param, kind, layer, wmv = big[name]
        if name == "fox_w_in":
            one = tail["last_start"][0, 0] + 1.0
            wmv = tuple(transposed(a * one) for a in wmv)
        landed, = _scatter_wait(send, recv, thru, lands, [kind], after, "scatter_wait_" + name)
        res = _adam_sum(landed, *wmv, layer, out.get(param), "adam_" + name, by_cols=kind == "major")
        out[param] = [transposed(r) for r in res] if kind == "major" else res

    def on_grads(name, g, after=None):
        kind = big[name][1]
        src = columns_major(g, n_in if name == "fox_w_in" else g.shape[0]) if kind == "major" else g
        send, recv, thru, lands, token = _scatter_start([src], [kind], after, "scatter_start_" + name)
        pending[name] = (send, recv, thru, lands)
        for done in finish_at.get(name, ()):
            finish(done, [token])
        tail["last_start"] = token
        return token

    def on_small(small):
        packed, tail["spans"] = _pack_rows([small[k] for k in SMALL_ORDER])
        send, recv, thru, lands, token = _scatter_start([packed], ["whole"], None, "small_grads_start")
        tail["small"] = (send, recv, thru, lands)
        return token

    grad_x = _local_step(x[0], loss_target[0], positions[0], mods, fetch, P, on_grads, on_small)

    spans = tail["spans"]
    done_first = [out[k][1] for k in ("ffn_w_up", "ffn_w_down", "swa_w_in", "swa_w_o")]
    gathered, = _scatter_wait(*tail["small"], ["whole"], done_first, "small_grads_wait")
    totals = dict(zip(SMALL_ORDER, _unpack_rows(_sum_slots(gathered, tail["last_start"], "sum_small_grads"), spans)))
    loss = totals["loss"].reshape(())
    n_mod = L * 6 * D
    dmod_all = gathered.reshape(N_DEV, -1)[:, :n_mod].reshape(N_DEV, L, 6 * D)
    dmod_cols = jnp.transpose(lax.dynamic_slice_in_dim(dmod_all, me * n_ada, n_ada, axis=2), (1, 0, 2))
    out["ada_w"] = _ada_bwd_adam(c_all.T, dmod_cols, ada_w, m_ada_w, v_ada_w, "adam_ada_w")

    g_small = dict(fox_b_f=totals["fox_b_f"].reshape(fox_b_f.shape), swa_sinks=totals["swa_sinks"].reshape(swa_sinks.shape),
                   ada_b=totals["dmod"].reshape(ada_b.shape), ffn_conv_b=totals["conv_b"].reshape(ffn_conv_b.shape),
                   ffn_conv_w=lax.dynamic_slice_in_dim(totals["conv_w"].reshape(L, 3, 2 * F), me * cu, cu, axis=2),
                   ln_mix_g=totals["ln_mix_g"], ln_mix_b=totals["ln_mix_b"],
                   ln_ffn_g=totals["ln_ffn_g"], ln_ffn_b=totals["ln_ffn_b"])
    small_names = ("fox_b_f", "swa_sinks", "ada_b", "ffn_conv_b", "ffn_conv_w", "ln_mix_g", "ln_mix_b", "ln_ffn_g", "ln_ffn_b")
    w_small = dict(fox_b_f=(fox_b_f, m_fox_b_f, v_fox_b_f), swa_sinks=(swa_sinks, m_swa_sinks, v_swa_sinks),
                   ada_b=(ada_b, m_ada_b, v_ada_b), ffn_conv_b=(ffn_conv_b, m_ffn_conv_b, v_ffn_conv_b),
                   ffn_conv_w=(ffn_conv_w, m_ffn_conv_w, v_ffn_conv_w),
                   ln_mix_g=(ln_mix_g, m_ln_mix_g, v_ln_mix_g), ln_mix_b=(ln_mix_b, m_ln_mix_b, v_ln_mix_b),
                   ln_ffn_g=(ln_ffn_g, m_ln_ffn_g, v_ln_ffn_g), ln_ffn_b=(ln_ffn_b, m_ln_ffn_b, v_ln_ffn_b))
    pk_g, sp = _pack_rows([g_small[k] for k in small_names])
    pk_w = _pack_rows([w_small[k][0] for k in small_names])[0]
    pk_m = _pack_rows([w_small[k][1] for k in small_names])[0]
    pk_v = _pack_rows([w_small[k][2] for k in small_names])[0]
    res = _adam(pk_g[None], pk_w[None], pk_m[None], pk_v[None], "adam_small")
    settled = [res[0], out["ada_w"][1]] + [out[k][1] for k in ("ffn_w_up", "ffn_w_down", "swa_w_in", "swa_w_o")]
    finish("fox_w_o", settled)
    finish("fox_w_in", settled)
    res = [dict(zip(small_names, _unpack_rows(r[0], sp))) for r in res]
    for k in small_names:
        out[k] = (g_small[k], res[0][k], res[1][k], res[2][k])

    order = ("fox_w_in", "fox_b_f", "fox_w_o", "swa_w_in", "swa_sinks", "swa_w_o", "ada_w", "ada_b", "ffn_w_up",
             "ffn_conv_w", "ffn_conv_b", "ffn_w_down", "ln_mix_g", "ln_mix_b", "ln_ffn_g", "ln_ffn_b")
    return (loss, grad_x[None], *[out[k][0] for k in order], *[out[k][1] for k in order],
            *[out[k][2] for k in order], *[out[k][3] for k in order])
```

```python
import functools

import jax
import jax.numpy as jnp
from jax import lax
from jax.experimental import pallas as pl
from jax.experimental.pallas import tpu as pltpu

F32 = jnp.float32
BF16 = jnp.bfloat16
MESH = pl.DeviceIdType.MESH
N_DEV = 8
AXES = ("x", "y", "c")

DEPTH = 2
ALPHA = (2.0 * DEPTH) ** 0.25
LN_EPS = 1e-5
FOX_HEAD_DIM = 128
SWA_HEAD_DIM = 64
SWA_GROUP = 8
SWA_WINDOW = 128
Q_BLOCK = 128
ROPE_DIM = 16
ROPE_THETA = 500000.0

ADAM_LR = 0.001
ADAM_B1 = 0.9
ADAM_B2 = 0.999
ADAM_EPS = 1e-08
ADAM_WD = 0.01
ADAM_STEP = 10

LANE = 128
MIB = 1024 * 1024


def _tile(n, pref, unit=LANE):
    if n <= pref:
        return n
    t = (pref // unit) * unit
    while t >= unit:
        if n % t == 0:
            return t
        t -= unit
    return n


def _params(sem, vmem_mib=48):
    return pltpu.CompilerParams(dimension_semantics=sem, vmem_limit_bytes=vmem_mib * MIB)


def _sigmoid(x):
    return 1.0 / (1.0 + jnp.exp(-x))


def _mm_call(dot, grid_mnk, in_specs, out_spec, out_shape, k_axis, nk, tm, tn, name, operands):
    sem = ("parallel",) * (len(grid_mnk) - 1) + ("arbitrary",)

    if nk == 1:
        def body(a_ref, b_ref, o_ref):
            o_ref[...] = dot(a_ref[...], b_ref[...]).astype(o_ref.dtype)
        scratch = []
    else:
        def body(a_ref, b_ref, o_ref, acc_ref):
            k = pl.program_id(k_axis)

            @pl.when(k == 0)
            def _():
                acc_ref[...] = jnp.zeros_like(acc_ref)

            acc_ref[...] += dot(a_ref[...], b_ref[...])

            @pl.when(k == nk - 1)
            def _():
                o_ref[...] = acc_ref[...].astype(o_ref.dtype)
        scratch = [pltpu.VMEM((tm, tn), F32)]

    return pl.pallas_call(
        body, name=name, grid=grid_mnk, in_specs=in_specs, out_specs=out_spec, out_shape=out_shape,
        scratch_shapes=scratch, compiler_params=_params(sem, 56),
    )(*operands)


def _dot(dims):
    def dot(a, b):
        return lax.dot_general(a.astype(BF16), b.astype(BF16), (dims, ((), ())), preferred_element_type=F32)
    return dot


def _mm_nn(a, b, out_dtype, name, tm=2048, tn=512, tk=2048):
    M, K = a.shape
    N = b.shape[1]
    tm, tn, tk = _tile(M, tm), _tile(N, tn), _tile(K, tk)
    nk = K // tk
    return _mm_call(
        _dot(((1,), (0,))), (M // tm, N // tn, nk),
        [pl.BlockSpec((tm, tk), lambda i, j, k: (i, k)), pl.BlockSpec((tk, tn), lambda i, j, k: (k, j))],
        pl.BlockSpec((tm, tn), lambda i, j, k: (i, j)), jax.ShapeDtypeStruct((M, N), out_dtype),
        2, nk, tm, tn, name, (a, b))


def _mm_nt(a, b, out_dtype, name, tm=2048, tn=512, tk=2048):
    P, M, K = a.shape
    N = b.shape[1]
    tm, tn, tk = _tile(M, tm), _tile(N, tn), _tile(K, tk)
    nk = K // tk
    return _mm_call(
        _dot(((1,), (1,))), (M // tm, N // tn, P * nk),
        [pl.BlockSpec((None, tm, tk), lambda i, j, k: (k // nk, i, k % nk)),
         pl.BlockSpec((None, tn, tk), lambda i, j, k: (k // nk, j, k % nk))],
        pl.BlockSpec((tm, tn), lambda i, j, k: (i, j)), jax.ShapeDtypeStruct((M, N), out_dtype),
        2, P * nk, tm, tn, name, (a, b))


def _mm_tn(a, b, out_dtype, name, tm=2048, tn=512, tk=2048):
    K, M = a.shape
    P, _, N = b.shape
    tm, tn, tk = _tile(M, tm), _tile(N, tn), _tile(K, tk)
    nk = K // tk
    return _mm_call(
        _dot(((0,), (0,))), (P, M // tm, N // tn, nk),
        [pl.BlockSpec((tk, tm), lambda p, i, j, k: (k, i)), pl.BlockSpec((None, tk, tn), lambda p, i, j, k: (p, k, j))],
        pl.BlockSpec((None, tm, tn), lambda p, i, j, k: (p, i, j)), jax.ShapeDtypeStruct((P, M, N), out_dtype),
        3, nk, tm, tn, name, (a, b))


ROW_TILE = 256


def _row_spec(tm, D):
    return pl.BlockSpec((tm, D), lambda i: (i, 0))


def _vec_spec(D):
    return pl.BlockSpec((1, D), lambda i: (0, 0))


def _modulate(x, sc, sh, name):
    S, D = x.shape
    tm = _tile(S, ROW_TILE, 8)

    def body(x_ref, sc_ref, sh_ref, h_ref):
        h_ref[...] = (x_ref[...] * (1.0 + sc_ref[...]) + sh_ref[...]).astype(BF16)

    return pl.pallas_call(
        body, name=name, grid=(S // tm,),
        in_specs=[_row_spec(tm, D), _vec_spec(D), _vec_spec(D)],
        out_specs=_row_spec(tm, D),
        out_shape=jax.ShapeDtypeStruct((S, D), BF16),
        compiler_params=_params(("parallel",)),
    )(x, sc, sh)


def _layer_norm_rows(z, gamma, beta):
    mu = jnp.mean(z, axis=-1, keepdims=True)
    zc = z - mu
    var = jnp.mean(zc * zc, axis=-1, keepdims=True)
    return zc * lax.rsqrt(var + LN_EPS) * gamma + beta


def _ln_fwd(x, y, gate, gamma, beta, sc_n, sh_n, after, name):
    S, D = x.shape
    tm = _tile(S, ROW_TILE, 8)

    def body(x_ref, y_ref, gate_ref, g_ref, b_ref, sc_ref, sh_ref, after_ref, z_ref, xo_ref, hn_ref):
        z = ALPHA * x_ref[...] + (1.0 + gate_ref[...]) * y_ref[...]
        xo = _layer_norm_rows(z, g_ref[...], b_ref[...])
        z_ref[...] = z
        xo_ref[...] = xo
        hn_ref[...] = (xo * (1.0 + sc_ref[...]) + sh_ref[...]).astype(BF16)

    return pl.pallas_call(
        body, name=name, grid=(S // tm,),
        in_specs=[_row_spec(tm, D), _row_spec(tm, D)] + [_vec_spec(D)] * 5 + [pl.BlockSpec(memory_space=pl.ANY)],
        out_specs=[_row_spec(tm, D)] * 3,
        out_shape=[jax.ShapeDtypeStruct((S, D), F32), jax.ShapeDtypeStruct((S, D), F32),
                   jax.ShapeDtypeStruct((S, D), BF16)],
        compiler_params=_params(("parallel",)),
    )(x, y, gate, gamma, beta, sc_n, sh_n, after)


def _ln_fwd_loss(x, y, gate, gamma, beta, target, name):
    S, D = x.shape
    tm = _tile(S, ROW_TILE, 8)

    def body(x_ref, y_ref, gate_ref, g_ref, b_ref, t_ref, z_ref, dout_ref, loss_ref):
        @pl.when(pl.program_id(0) == 0)
        def _():
            loss_ref[...] = jnp.zeros_like(loss_ref)

        z = ALPHA * x_ref[...] + (1.0 + gate_ref[...]) * y_ref[...]
        xo = _layer_norm_rows(z, g_ref[...], b_ref[...])
        err = xo - t_ref[...]
        z_ref[...] = z
        dout_ref[...] = err * (1.0 / D)
        loss_ref[...] += (0.5 / D) * jnp.sum(err * err)

    return pl.pallas_call(
        body, name=name, grid=(S // tm,),
        in_specs=[_row_spec(tm, D), _row_spec(tm, D)] + [_vec_spec(D)] * 3 + [_row_spec(tm, D)],
        out_specs=[_row_spec(tm, D), _row_spec(tm, D), pl.BlockSpec((1, LANE), lambda i: (0, 0))],
        out_shape=[jax.ShapeDtypeStruct((S, D), F32), jax.ShapeDtypeStruct((S, D), F32),
                   jax.ShapeDtypeStruct((1, LANE), F32)],
        compiler_params=_params(("arbitrary",)),
    )(x, y, gate, gamma, beta, target)


def _ln_bwd(dout, z, y, gate, gamma, name):
    S, D = z.shape
    tm = _tile(S, ROW_TILE, 8)

    def body(dout_ref, z_ref, y_ref, gate_ref, g_ref, dz_ref, dy_ref, dg_ref, db_ref, dgate_ref):
        @pl.when(pl.program_id(0) == 0)
        def _():
            dg_ref[...] = jnp.zeros_like(dg_ref)
            db_ref[...] = jnp.zeros_like(db_ref)
            dgate_ref[...] = jnp.zeros_like(dgate_ref)

        z = z_ref[...]
        dout = dout_ref[...]
        mu = jnp.mean(z, axis=-1, keepdims=True)
        zc = z - mu
        var = jnp.mean(zc * zc, axis=-1, keepdims=True)
        rstd = lax.rsqrt(var + LN_EPS)
        xhat = zc * rstd
        dxhat = dout * g_ref[...]
        m1 = jnp.mean(dxhat, axis=-1, keepdims=True)
        m2 = jnp.mean(dxhat * xhat, axis=-1, keepdims=True)
        dz = rstd * (dxhat - m1 - xhat * m2)
        dz_ref[...] = dz
        dy_ref[...] = (dz * (1.0 + gate_ref[...])).astype(BF16)
        dg_ref[...] += jnp.sum(dout * xhat, axis=0, keepdims=True)
        db_ref[...] += jnp.sum(dout, axis=0, keepdims=True)
        dgate_ref[...] += jnp.sum(dz * y_ref[...], axis=0, keepdims=True)

    return pl.pallas_call(
        body, name=name, grid=(S // tm,),
        in_specs=[_row_spec(tm, D)] * 3 + [_vec_spec(D)] * 2,
        out_specs=[_row_spec(tm, D), _row_spec(tm, D)] + [_vec_spec(D)] * 3,
        out_shape=[jax.ShapeDtypeStruct((S, D), F32), jax.ShapeDtypeStruct((S, D), BF16)]
        + [jax.ShapeDtypeStruct((1, D), F32)] * 3,
        compiler_params=_params(("arbitrary",)),
    )(dout, z, y, gate, gamma)


def _mod_bwd(dz, dh, xin, sc, after, name):
    S, D = dz.shape
    tm = _tile(S, ROW_TILE, 8)

    def body(dz_ref, dh_ref, x_ref, sc_ref, after_ref, dx_ref, dsc_ref, dsh_ref):
        @pl.when(pl.program_id(0) == 0)
        def _():
            dsc_ref[...] = jnp.zeros_like(dsc_ref)
            dsh_ref[...] = jnp.zeros_like(dsh_ref)

        dh = dh_ref[...]
        dx_ref[...] = ALPHA * dz_ref[...] + dh * (1.0 + sc_ref[...])
        dsc_ref[...] += jnp.sum(dh * x_ref[...], axis=0, keepdims=True)
        dsh_ref[...] += jnp.sum(dh, axis=0, keepdims=True)

    return pl.pallas_call(
        body, name=name, grid=(S // tm,),
        in_specs=[_row_spec(tm, D)] * 3 + [_vec_spec(D), pl.BlockSpec(memory_space=pl.ANY)],
        out_specs=[_row_spec(tm, D), _vec_spec(D), _vec_spec(D)],
        out_shape=[jax.ShapeDtypeStruct((S, D), F32)] + [jax.ShapeDtypeStruct((1, D), F32)] * 2,
        compiler_params=_params(("arbitrary",)),
    )(dz, dh, xin, sc, after)


def _shift_down(u, k, row):
    return jnp.where(row >= k, pltpu.roll(u, k, axis=0), 0.0)


def _shift_up(u, k, row, S):
    return jnp.where(row < S - k, pltpu.roll(u, S - k, axis=0), 0.0)


def _ffn_up(h, w, cw, cb, name):
    S, D = h.shape
    F = w.shape[2]
    tn = _tile(F, 256)

    piece = S // 2 if S % 16 == 0 else S
    halo = 8

    def body(h_ref, w_ref, cw_ref, cb_ref, u_ref, uc_ref, a_ref):
        tails = [None, None]
        for lo in range(0, S, piece):
            rows = slice(lo, lo + piece)
            conv = []
            for p in range(2):
                u = jnp.dot(h_ref[rows, :], w_ref[p], preferred_element_type=F32)
                u_ref[p, rows, :] = u.astype(BF16)
                above, tails[p] = tails[p], u[piece - halo:]
                if above is not None:
                    u = jnp.concatenate([above, u], axis=0)
                row = lax.broadcasted_iota(jnp.int32, u.shape, 0)
                cwp = cw_ref[p]
                uc = _shift_down(u, 2, row) * cwp[0:1] + _shift_down(u, 1, row) * cwp[1:2] + u * cwp[2:3] + cb_ref[p]
                uc = uc if above is None else uc[halo:]
                uc_ref[p, rows, :] = uc.astype(BF16)
                conv.append(uc)
            g, v = conv
            a_ref[rows, :] = (g * _sigmoid(g) * v).astype(BF16)

    half = pl.BlockSpec((2, S, tn), lambda j: (0, 0, j))
    return pl.pallas_call(
        body, name=name, grid=(F // tn,),
        in_specs=[pl.BlockSpec((S, D), lambda j: (0, 0)), pl.BlockSpec((2, D, tn), lambda j: (0, 0, j)),
                  pl.BlockSpec((2, 3, tn), lambda j: (0, 0, j)), pl.BlockSpec((2, 1, tn), lambda j: (0, 0, j))],
        out_specs=[half, half, pl.BlockSpec((S, tn), lambda j: (0, j))],
        out_shape=[jax.ShapeDtypeStruct((2, S, F), BF16), jax.ShapeDtypeStruct((2, S, F), BF16),
                   jax.ShapeDtypeStruct((S, F), BF16)],
        compiler_params=_params(("parallel",), 56),
    )(h, w, cw, cb)


def _ffn_bwd_elem(da, u, uc, cw, after, name):
    _, S, F = u.shape
    tn = _tile(F, 256)

    def body(da_ref, u_ref, uc_ref, cw_ref, after_ref, du_ref, dcw_ref, dcb_ref):
        row = lax.broadcasted_iota(jnp.int32, (S, tn), 0)
        da = da_ref[...]
        g, v = uc_ref[0].astype(F32), uc_ref[1].astype(F32)
        sg = _sigmoid(g)
        d_conv = (da * v * (sg * (1.0 + g * (1.0 - sg))), da * (g * sg))
        for p in range(2):
            d = d_conv[p]
            cwp = cw_ref[p]
            u = u_ref[p].astype(F32)
            d1, d2 = _shift_up(d, 1, row, S), _shift_up(d, 2, row, S)
            dcb_ref[p] = jnp.sum(d, axis=0, keepdims=True)
            dcw_ref[p, 0:1, :] = jnp.sum(d2 * u, axis=0, keepdims=True)
            dcw_ref[p, 1:2, :] = jnp.sum(d1 * u, axis=0, keepdims=True)
            dcw_ref[p, 2:3, :] = jnp.sum(d * u, axis=0, keepdims=True)
            du_ref[p] = (d * cwp[2:3] + d1 * cwp[1:2] + d2 * cwp[0:1]).astype(BF16)

    half = pl.BlockSpec((2, S, tn), lambda j: (0, 0, j))
    return pl.pallas_call(
        body, name=name, grid=(F // tn,),
        in_specs=[pl.BlockSpec((S, tn), lambda j: (0, j)), half, half, pl.BlockSpec((2, 3, tn), lambda j: (0, 0, j)),
                  pl.BlockSpec(memory_space=pl.ANY)],
        out_specs=[half, pl.BlockSpec((2, 3, tn), lambda j: (0, 0, j)), pl.BlockSpec((2, 1, tn), lambda j: (0, 0, j))],
        out_shape=[jax.ShapeDtypeStruct((2, S, F), BF16), jax.ShapeDtypeStruct((2, 3, F), F32),
                   jax.ShapeDtypeStruct((2, 1, F), F32)],
        compiler_params=_params(("parallel",), 56),
    )(da, u, uc, cw, after)


def _split3(x):
    hi = x.astype(BF16)
    r1 = x - hi.astype(F32)
    mid = r1.astype(BF16)
    lo = (r1 - mid.astype(F32)).astype(BF16)
    return hi, mid, lo


def _tri_matmul(x, upper, S):
    tc = _tile(S, 512)
    parts = _split3(x)
    outs = []
    for b in range(S // tc):
        r = lax.broadcasted_iota(jnp.int32, (S, tc), 0)
        c = lax.broadcasted_iota(jnp.int32, (S, tc), 1) + b * tc
        tri = jnp.where((r <= c) if upper else (r >= c), 1.0, 0.0).astype(BF16)
        acc = jnp.dot(parts[0], tri, preferred_element_type=F32)
        acc += jnp.dot(parts[1], tri, preferred_element_type=F32)
        acc += jnp.dot(parts[2], tri, preferred_element_type=F32)
        outs.append(acc)
    return outs, tc


def _fox_prep(flT, bf, name):
    H, S = flT.shape

    def body(fl_ref, b_ref, cum_ref):
        zz = fl_ref[...] + b_ref[...]
        lf = jnp.minimum(zz, 0.0) - jnp.log(1.0 + jnp.exp(-jnp.abs(zz)))
        outs, tc = _tri_matmul(lf, True, S)
        for b, o in enumerate(outs):
            cum_ref[:, b * tc:(b + 1) * tc] = o

    return pl.pallas_call(
        body, name=name,
        in_specs=[pl.BlockSpec(memory_space=pltpu.VMEM)] * 2,
        out_specs=pl.BlockSpec(memory_space=pltpu.VMEM),
        out_shape=jax.ShapeDtypeStruct((H, S), F32),
        compiler_params=pltpu.CompilerParams(vmem_limit_bytes=48 * MIB),
    )(flT, bf)


def _fox_prep_bwd(dcum_key, dcum_query, flT, bf, name):
    H, S = flT.shape

    def body(dck_ref, dcq_ref, fl_ref, b_ref, dfl_ref, dbf_ref):
        zz = fl_ref[...] + b_ref[...]
        outs, tc = _tri_matmul(dck_ref[...] + dcq_ref[...], False, S)
        total = jnp.zeros((H, 1), F32)
        for b, o in enumerate(outs):
            dfl = o * _sigmoid(-zz[:, b * tc:(b + 1) * tc])
            dfl_ref[:, b * tc:(b + 1) * tc] = dfl
            total += jnp.sum(dfl, axis=1, keepdims=True)
        dbf_ref[...] = total

    return pl.pallas_call(
        body, name=name,
        in_specs=[pl.BlockSpec(memory_space=pltpu.VMEM)] * 4,
        out_specs=[pl.BlockSpec(memory_space=pltpu.VMEM)] * 2,
        out_shape=[jax.ShapeDtypeStruct((H, S), F32), jax.ShapeDtypeStruct((H, 1), F32)],
        compiler_params=pltpu.CompilerParams(vmem_limit_bytes=48 * MIB),
    )(dcum_key, dcum_query, flT, bf)


FOX_TQ = 256
FOX_TC = 512


def _fox_scores(q, k_ref, ck_ref, i, lo, n, tq):
    k = k_ref[lo:lo + n, :].astype(BF16)
    s = lax.dot_general(q, k, (((1,), (1,)), ((), ())), preferred_element_type=F32) - ck_ref[:, lo:lo + n]
    qpos = i * tq + lax.broadcasted_iota(jnp.int32, (tq, n), 0)
    kpos = lo + lax.broadcasted_iota(jnp.int32, (tq, n), 1)
    return jnp.where(kpos <= qpos, s, -jnp.inf)


FOX_FWD_SPLITS = 4


def _fox_attn_fwd_part(proj, cum_row, H, q_lo, q_hi, filled, name):
    S = proj.shape[0]
    dh = FOX_HEAD_DIM
    tq = _tile(q_hi - q_lo, 2 * FOX_TQ)
    first = q_lo // tq
    scale = dh ** -0.5
    n_keep = 0 if filled is None else 2

    def body(q_ref, k_ref, v_ref, ck_ref, *rest):
        o_ref, lse_ref = rest[n_keep:]
        q = (q_ref[...] * scale).astype(BF16)
        s = _fox_scores(q, k_ref, ck_ref, first + pl.program_id(1), 0, q_hi, tq)
        m = jnp.max(s, axis=-1, keepdims=True)
        p = jnp.exp(s - m)
        l = jnp.sum(p, axis=-1, keepdims=True)
        o = jnp.dot(p.astype(BF16), v_ref[...].astype(BF16), preferred_element_type=F32) / l
        o_ref[...] = o.astype(BF16)
        lse_ref[...] = m + jnp.log(l)

    return pl.pallas_call(
        body, name=name, grid=(H, (q_hi - q_lo) // tq),
        in_specs=[pl.BlockSpec((tq, dh), lambda h, i: (first + i, h)),
                  pl.BlockSpec((q_hi, dh), lambda h, i: (0, H + h)),
                  pl.BlockSpec((q_hi, dh), lambda h, i: (0, 2 * H + h)),
                  pl.BlockSpec((None, 1, q_hi), lambda h, i: (h, 0, 0))] + [pl.BlockSpec(memory_space=pl.ANY)] * n_keep,
        out_specs=[pl.BlockSpec((tq, dh), lambda h, i: (first + i, h)),
                   pl.BlockSpec((None, tq, 1), lambda h, i: (h, first + i, 0))],
        out_shape=[jax.ShapeDtypeStruct((S, H * dh), BF16), jax.ShapeDtypeStruct((H, S, 1), F32)],
        input_output_aliases={4 + k: k for k in range(n_keep)},
        compiler_params=_params(("parallel", "parallel")),
    )(proj, proj, proj, cum_row, *(filled or ()))


def _fox_attn_fwd(proj, cum_row, H, name):
    S = proj.shape[0]
    n = FOX_FWD_SPLITS if S % (FOX_FWD_SPLITS * FOX_TQ) == 0 else 1
    out = None
    for part in range(n):
        out = _fox_attn_fwd_part(proj, cum_row, H, part * S // n, (part + 1) * S // n, out, name + str(part))
    return out


def _fox_attn_bwd(proj, do, o, cum_row, lse, H, after, name):
    S = proj.shape[0]
    dh = FOX_HEAD_DIM
    tq = _tile(S, 2 * FOX_TQ)
    tc = _tile(S, FOX_TC)
    scale = dh ** -0.5

    def body(q_ref, k_ref, v_ref, do_ref, o_ref, ck_ref, lse_ref, after_ref,
             dq_ref, dk_ref, dv_ref, dck_ref, dcq_ref, acc_ref):
        i = pl.program_id(1)

        @pl.when(i == 0)
        def _():
            dk_ref[...] = jnp.zeros_like(dk_ref)
            dv_ref[...] = jnp.zeros_like(dv_ref)
            dck_ref[...] = jnp.zeros_like(dck_ref)

        acc_ref[...] = jnp.zeros_like(acc_ref)
        dcq_ref[...] = jnp.zeros_like(dcq_ref)
        q = (q_ref[...] * scale).astype(BF16)
        do_f = do_ref[...]
        do_b = do_f.astype(BF16)
        delta = jnp.sum(do_f * o_ref[...].astype(F32), axis=-1, keepdims=True)
        lse_q = lse_ref[...]
        for c in range(S // tc):
            @pl.when(c * tc <= i * tq + tq - 1)
            def _():
                rows = slice(c * tc, (c + 1) * tc)
                p = jnp.exp(_fox_scores(q, k_ref, ck_ref, i, c * tc, tc, tq) - lse_q)
                dp = lax.dot_general(do_b, v_ref[rows, :].astype(BF16), (((1,), (1,)), ((), ())),
                                     preferred_element_type=F32)
                ds = p * (dp - delta)
                ds_b = ds.astype(BF16)
                acc_ref[...] += jnp.dot(ds_b, k_ref[rows, :].astype(BF16), preferred_element_type=F32)
                dk_ref[rows, :] += lax.dot_general(ds_b, q, (((0,), (0,)), ((), ())), preferred_element_type=F32)
                dv_ref[rows, :] += lax.dot_general(p.astype(BF16), do_b, (((0,), (0,)), ((), ())),
                                                   preferred_element_type=F32)
                dck_ref[:, rows] -= jnp.sum(ds, axis=0, keepdims=True)
                dcq_ref[...] += jnp.sum(ds, axis=-1, keepdims=True)
        dq_ref[...] = (acc_ref[...] * scale).astype(BF16)

    W = H * dh
    return pl.pallas_call(
        body, name=name, grid=(H, S // tq),
        in_specs=[pl.BlockSpec((tq, dh), lambda h, i: (i, h)),
                  pl.BlockSpec((S, dh), lambda h, i: (0, H + h)),
                  pl.BlockSpec((S, dh), lambda h, i: (0, 2 * H + h)),
                  pl.BlockSpec((tq, dh), lambda h, i: (i, h)),
                  pl.BlockSpec((tq, dh), lambda h, i: (i, h)),
                  pl.BlockSpec((None, 1, S), lambda h, i: (h, 0, 0)),
                  pl.BlockSpec((None, tq, 1), lambda h, i: (h, i, 0)),
                  pl.BlockSpec(memory_space=pl.ANY)],
        out_specs=[pl.BlockSpec((tq, dh), lambda h, i: (i, h)),
                   pl.BlockSpec((S, dh), lambda h, i: (0, h)),
                   pl.BlockSpec((S, dh), lambda h, i: (0, h)),
                   pl.BlockSpec((None, 1, S), lambda h, i: (h, 0, 0)),
                   pl.BlockSpec((None, tq, 1), lambda h, i: (h, i, 0))],
        out_shape=[jax.ShapeDtypeStruct((S, W), BF16), jax.ShapeDtypeStruct((S, W), F32),
                   jax.ShapeDtypeStruct((S, W), F32), jax.ShapeDtypeStruct((H, 1, S), F32),
                   jax.ShapeDtypeStruct((H, S, 1), F32)],
        scratch_shapes=[pltpu.VMEM((tq, dh), F32)],
        compiler_params=_params(("parallel", "arbitrary")),
    )(proj, proj, proj, do, o, cum_row, lse, after)


def _rope(parts, tabs, out_dtype, name):
    S = parts[0][0].shape[0]
    widths = [w for _, _, w in parts]
    total = sum(widths)
    tm = _tile(S, ROW_TILE, 8)
    flags = [r for _, r, _ in parts]

    def body(*refs):
        in_refs = refs[:len(parts)]
        cos_ref, sa_ref, sb_ref, o_ref = refs[len(parts):]
        cos, sa, sb = cos_ref[...], sa_ref[...], sb_ref[...]
        off = 0
        for ref, rot, w in zip(in_refs, flags, widths):
            for j in range(w // LANE):
                t = ref[:, j * LANE:(j + 1) * LANE]
                if rot:
                    t = t * cos + pltpu.roll(t, LANE - ROPE_DIM // 2, axis=1) * sa + pltpu.roll(t, ROPE_DIM // 2, axis=1) * sb
                o_ref[:, off + j * LANE:off + (j + 1) * LANE] = t.astype(o_ref.dtype)
            off += w

    return pl.pallas_call(
        body, name=name, grid=(S // tm,),
        in_specs=[pl.BlockSpec((tm, w), lambda i: (i, 0)) for w in widths] + [_row_spec(tm, LANE)] * 3,
        out_specs=_row_spec(tm, total),
        out_shape=jax.ShapeDtypeStruct((S, total), out_dtype),
        compiler_params=_params(("parallel",)),
    )(*[a for a, _, _ in parts], *tabs)


def _swa_band(ref_p, ref_c, hk):
    dh = SWA_HEAD_DIM
    return jnp.concatenate([ref_p[:, hk * dh:(hk + 1) * dh], ref_c[:, hk * dh:(hk + 1) * dh]], axis=0).astype(BF16)


def _swa_bias(G):
    qi = jnp.arange(G * Q_BLOCK)[:, None] % Q_BLOCK
    kj = jnp.arange(2 * Q_BLOCK)[None, :]
    rel = qi + Q_BLOCK - kj
    window = (rel >= 0) & (rel < SWA_WINDOW)
    both = jnp.stack([window & (kj >= Q_BLOCK), window])
    return jnp.where(both, 0.0, -jnp.inf).astype(F32)


def _swa_stack(ref, hk, G):
    dh = SWA_HEAD_DIM
    return jnp.concatenate([ref[:, (hk * G + g) * dh:(hk * G + g + 1) * dh] for g in range(G)], axis=0)


def _swa_unstack(ref, stacked, hk, G):
    dh, QB = SWA_HEAD_DIM, Q_BLOCK
    for g in range(0, G, 2):
        c0 = (hk * G + g) * dh
        pair = jnp.concatenate([stacked[g * QB:(g + 1) * QB], stacked[(g + 1) * QB:(g + 2) * QB]], axis=1)
        ref[:, c0:c0 + 2 * dh] = pair.astype(ref.dtype)


def _swa_sink_rows(sink_ref, hk, G):
    return jnp.concatenate([jnp.broadcast_to(sink_ref[0:1, hk * G + g:hk * G + g + 1], (Q_BLOCK, 1)) for g in range(G)],
                           axis=0)


def _swa_attn_fwd(qk, proj, sinks, Hq, name):
    S = qk.shape[0]
    dh, G, QB = SWA_HEAD_DIM, SWA_GROUP, Q_BLOCK
    Hk = Hq // G
    Wq, Wk = Hq * dh, Hk * dh
    nb = S // QB
    scale = dh ** -0.5

    def body(q_ref, kp_ref, kc_ref, vp_ref, vc_ref, sink_ref, bias_ref, o_ref, lse_ref):
        n = pl.program_id(0)
        lane = lax.broadcasted_iota(jnp.int32, (QB, LANE), 1)
        first, second = pl.ds(0, QB), pl.ds(QB, QB)
        for sub, rows in enumerate((first, second)):
            bias = bias_ref[jnp.minimum(n, 1)] if sub == 0 else bias_ref[1]
            k_band = (kp_ref, kc_ref.at[first]) if sub == 0 else (kc_ref.at[first], kc_ref.at[second])
            v_band = (vp_ref, vc_ref.at[first]) if sub == 0 else (vc_ref.at[first], vc_ref.at[second])
            lse_tile = jnp.zeros((QB, LANE), F32)
            for hk in range(Hk):
                kb = _swa_band(*k_band, hk)
                vb = _swa_band(*v_band, hk)
                q = (_swa_stack(q_ref.at[rows], hk, G) * scale).astype(BF16)
                sk = _swa_sink_rows(sink_ref, hk, G)
                s = lax.dot_general(q, kb, (((1,), (1,)), ((), ())), preferred_element_type=F32) + bias
                m = jnp.maximum(jnp.max(s, axis=-1, keepdims=True), sk)
                p = jnp.exp(s - m)
                l = jnp.sum(p, axis=-1, keepdims=True) + jnp.exp(sk - m)
                o = jnp.dot(p.astype(BF16), vb, preferred_element_type=F32) / l
                lse = m + jnp.log(l)
                for g in range(G):
                    lse_tile = jnp.where(lane == hk * G + g, lse[g * QB:(g + 1) * QB], lse_tile)
                _swa_unstack(o_ref.at[rows], o, hk, G)
            lse_ref[rows, :] = lse_tile

    kcol, vcol = Wq // Wk, (Wq + Wk) // Wk
    before = lambda n: jnp.maximum(2 * n - 1, 0)
    return pl.pallas_call(
        body, name=name, grid=(nb // 2,),
        in_specs=[pl.BlockSpec((2 * QB, Wq), lambda n: (n, 0)),
                  pl.BlockSpec((QB, Wk), lambda n: (before(n), kcol)),
                  pl.BlockSpec((2 * QB, Wk), lambda n: (n, kcol)),
                  pl.BlockSpec((QB, Wk), lambda n: (before(n), vcol)),
                  pl.BlockSpec((2 * QB, Wk), lambda n: (n, vcol)),
                  pl.BlockSpec((1, LANE), lambda n: (0, 0)),
                  pl.BlockSpec((2, G * QB, 2 * QB), lambda n: (0, 0, 0))],
        out_specs=[pl.BlockSpec((2 * QB, Wq), lambda n: (n, 0)), pl.BlockSpec((2 * QB, LANE), lambda n: (n, 0))],
        out_shape=[jax.ShapeDtypeStruct((S, Wq), BF16), jax.ShapeDtypeStruct((S, LANE), F32)],
        compiler_params=_params(("parallel",)),
    )(qk, qk, qk, proj, proj, sinks, _swa_bias(G))


def _swa_attn_bwd(qk, proj, sinks, do, lse, Hq, after, name):
    S = qk.shape[0]
    dh, G, QB = SWA_HEAD_DIM, SWA_GROUP, Q_BLOCK
    Hk = Hq // G
    Wq, Wk = Hq * dh, Hk * dh
    nb = S // QB
    scale = dh ** -0.5

    def body(q_ref, kp_ref, kc_ref, vp_ref, vc_ref, sink_ref, do_ref, lse_ref, bias_ref, after_ref,
             dq_ref, dk_ref, dv_ref, dsink_ref, carry_k, carry_v):
        n = pl.program_id(0)

        @pl.when(n == 0)
        def _():
            dsink_ref[...] = jnp.zeros_like(dsink_ref)

        @pl.when(n < nb)
        def _():
            bias = bias_ref[...]
            lane = lax.broadcasted_iota(jnp.int32, (1, LANE), 1)
            dsink = jnp.zeros((1, LANE), F32)
            dk_heads, dv_heads = [], []
            for hk in range(Hk):
                kb = _swa_band(kp_ref, kc_ref, hk)
                vb = _swa_band(vp_ref, vc_ref, hk)
                q = (_swa_stack(q_ref, hk, G) * scale).astype(BF16)
                do_s = _swa_stack(do_ref, hk, G).astype(BF16)
                lse = jnp.concatenate([lse_ref[:, hk * G + g:hk * G + g + 1] for g in range(G)], axis=0)
                s = lax.dot_general(q, kb, (((1,), (1,)), ((), ())), preferred_element_type=F32) + bias
                p = jnp.exp(s - lse)
                p_sink = jnp.exp(_swa_sink_rows(sink_ref, hk, G) - lse)
                dp = lax.dot_general(do_s, vb, (((1,), (1,)), ((), ())), preferred_element_type=F32)
                delta = jnp.sum(p * dp, axis=-1, keepdims=True)
                ds_b = (p * (dp - delta)).astype(BF16)
                _swa_unstack(dq_ref, jnp.dot(ds_b, kb, preferred_element_type=F32) * scale, hk, G)
                dk_heads.append(lax.dot_general(ds_b, q, (((0,), (0,)), ((), ())), preferred_element_type=F32))
                dv_heads.append(lax.dot_general(p.astype(BF16), do_s, (((0,), (0,)), ((), ())), preferred_element_type=F32))
                sink_term = p_sink * delta
                for g in range(G):
                    dsink = jnp.where(lane == hk * G + g,
                                      -jnp.sum(sink_term[g * QB:(g + 1) * QB], axis=0, keepdims=True), dsink)
            dsink_ref[...] += dsink
            dk_all = jnp.concatenate(dk_heads, axis=1)
            dv_all = jnp.concatenate(dv_heads, axis=1)

            @pl.when(n > 0)
            def _():
                dk_ref[...] = carry_k[...] + dk_all[:QB]
                dv_ref[...] = carry_v[...] + dv_all[:QB]

            carry_k[...] = dk_all[QB:]
            carry_v[...] = dv_all[QB:]

        @pl.when(n == nb)
        def _():
            dk_ref[...] = carry_k[...]
            dv_ref[...] = carry_v[...]

    kcol, vcol = Wq // Wk, (Wq + Wk) // Wk
    cur = lambda n: jnp.minimum(n, nb - 1)
    prev = lambda n: jnp.maximum(jnp.minimum(n, nb - 1) - 1, 0)
    return pl.pallas_call(
        body, name=name, grid=(nb + 1,),
        in_specs=[pl.BlockSpec((QB, Wq), lambda n: (cur(n), 0)),
                  pl.BlockSpec((QB, Wk), lambda n: (prev(n), kcol)),
                  pl.BlockSpec((QB, Wk), lambda n: (cur(n), kcol)),
                  pl.BlockSpec((QB, Wk), lambda n: (prev(n), vcol)),
                  pl.BlockSpec((QB, Wk), lambda n: (cur(n), vcol)),
                  pl.BlockSpec((1, LANE), lambda n: (0, 0)),
                  pl.BlockSpec((QB, Wq), lambda n: (cur(n), 0)),
                  pl.BlockSpec((QB, LANE), lambda n: (cur(n), 0)),
                  pl.BlockSpec((None, G * QB, 2 * QB), lambda n: (jnp.minimum(n, 1), 0, 0)),
                  pl.BlockSpec(memory_space=pl.ANY)],
        out_specs=[pl.BlockSpec((QB, Wq), lambda n: (cur(n), 0)),
                   pl.BlockSpec((QB, Wk), lambda n: (jnp.maximum(n - 1, 0), 0)),
                   pl.BlockSpec((QB, Wk), lambda n: (jnp.maximum(n - 1, 0), 0)),
                   pl.BlockSpec((1, LANE), lambda n: (0, 0))],
        out_shape=[jax.ShapeDtypeStruct((S, Wq), F32), jax.ShapeDtypeStruct((S, Wk), F32),
                   jax.ShapeDtypeStruct((S, Wk), F32), jax.ShapeDtypeStruct((1, LANE), F32)],
        scratch_shapes=[pltpu.VMEM((QB, Wk), F32), pltpu.VMEM((QB, Wk), F32)],
        compiler_params=_params(("arbitrary",)),
    )(qk, qk, qk, proj, proj, sinks, do, lse, _swa_bias(G), after)


ADA_ROWS = 16


def _ada_mod(c_pad, w, b, name):
    L, D, N = w.shape
    tn = _tile(N, 512)

    def body(c_ref, w_ref, b_ref, o_ref):
        c = c_ref[...]
        c = c * _sigmoid(c)
        ch = c.astype(BF16)
        cl = (c - ch.astype(F32)).astype(BF16)
        ww = w_ref[...]
        wh = ww.astype(BF16)
        wl = (ww - wh.astype(F32)).astype(BF16)
        acc = jnp.dot(ch, wh, preferred_element_type=F32)
        acc += jnp.dot(ch, wl, preferred_element_type=F32)
        acc += jnp.dot(cl, wh, preferred_element_type=F32)
        o_ref[...] = acc + b_ref[...]

    return pl.pallas_call(
        body, name=name, grid=(L, N // tn),
        in_specs=[pl.BlockSpec((ADA_ROWS, D), lambda l, j: (0, 0)),
                  pl.BlockSpec((None, D, tn), lambda l, j: (l, 0, j)),
                  pl.BlockSpec((None, 1, tn), lambda l, j: (l, 0, j))],
        out_specs=pl.BlockSpec((None, ADA_ROWS, tn), lambda l, j: (l, 0, j)),
        out_shape=jax.ShapeDtypeStruct((L, ADA_ROWS, N), F32),
        compiler_params=_params(("parallel", "parallel")),
    )(c_pad, w, b)


def _ada_bwd_adam(cT, dm, w, m, v, name):
    D = cT.shape[0]
    L, B, N = dm.shape
    tm = _tile(D, 256)

    def body(c_ref, dm_ref, w_ref, m_ref, v_ref, g_ref, d_ref, mo_ref, vo_ref):
        c = c_ref[...]
        c = c * _sigmoid(c)
        dmv = dm_ref[...]
        g = c[:, 0:1] * dmv[0:1, :]
        for b in range(1, B):
            g += c[:, b:b + 1] * dmv[b:b + 1, :]
        delta, mn, vn = _adamw_math(w_ref[...], g, m_ref[...], v_ref[...])
        g_ref[...] = g
        d_ref[...] = delta
        mo_ref[...] = mn
        vo_ref[...] = vn

    spec = pl.BlockSpec((None, tm, N), lambda l, i: (l, i, 0))
    return pl.pallas_call(
        body, name=name, grid=(L, D // tm),
        in_specs=[pl.BlockSpec((tm, B), lambda l, i: (i, 0)), pl.BlockSpec((None, B, N), lambda l, i: (l, 0, 0)),
                  spec, spec, spec],
        out_specs=[spec] * 4,
        out_shape=[jax.ShapeDtypeStruct((L, D, N), F32)] * 4,
        compiler_params=_params(("parallel", "parallel")),
    )(cT, dm, w, m, v)


def _adamw_math(w, g, m, v):
    m = ADAM_B1 * m + (1.0 - ADAM_B1) * g
    v = ADAM_B2 * v + (1.0 - ADAM_B2) * (g * g)
    m_hat = m / (1.0 - ADAM_B1 ** ADAM_STEP)
    v_hat = v / (1.0 - ADAM_B2 ** ADAM_STEP)
    delta = -ADAM_LR * (m_hat / (jnp.sqrt(v_hat) + ADAM_EPS) + ADAM_WD * w)
    return delta, m, v


def _adam_rows(R, C):
    lanes = -(-C // LANE) * LANE
    return _tile(R, max(8, (262144 // lanes) // 8 * 8), 8)


def _adam_sum(recv, w, m, v, layer, filled, name, by_cols=False):
    P, R, C = recv.shape
    L = w.shape[0]
    n_keep = 0 if filled is None else 4
    if by_cols:
        tc = _tile(C, 256)
        grid, spec = (C // tc,), pl.BlockSpec((None, R, tc), lambda i: (layer, 0, i))
        recv_spec = pl.BlockSpec((P, R, tc), lambda i: (0, 0, i))
    else:
        tr = _adam_rows(R, C)
        grid, spec = (R // tr,), pl.BlockSpec((None, tr, C), lambda i: (layer, i, 0))
        recv_spec = pl.BlockSpec((P, tr, C), lambda i: (0, i, 0))

    def body(r_ref, w_ref, m_ref, v_ref, *rest):
        g_ref, d_ref, mo_ref, vo_ref = rest[n_keep:]
        g = r_ref[0].astype(F32)
        for p in range(1, P):
            g = g + r_ref[p].astype(F32)
        delta, mn, vn = _adamw_math(w_ref[...], g, m_ref[...], v_ref[...])
        g_ref[...] = g
        d_ref[...] = delta
        mo_ref[...] = mn
        vo_ref[...] = vn

    return pl.pallas_call(
        body, name=name, grid=grid,
        in_specs=[recv_spec, spec, spec, spec] + [pl.BlockSpec(memory_space=pl.ANY)] * n_keep,
        out_specs=[spec] * 4,
        out_shape=[jax.ShapeDtypeStruct((L, R, C), F32)] * 4,
        input_output_aliases={4 + k: k for k in range(n_keep)},
        compiler_params=_params(("parallel",)),
    )(recv, w, m, v, *(filled or ()))


def _adam(g, w, m, v, name):
    L, R, C = w.shape
    tr = _adam_rows(R, C)

    def body(g_ref, w_ref, m_ref, v_ref, d_ref, mo_ref, vo_ref):
        delta, mn, vn = _adamw_math(w_ref[...], g_ref[...], m_ref[...], v_ref[...])
        d_ref[...] = delta
        mo_ref[...] = mn
        vo_ref[...] = vn

    spec = pl.BlockSpec((None, tr, C), lambda l, i: (l, i, 0))
    return pl.pallas_call(
        body, name=name, grid=(L, R // tr),
        in_specs=[spec] * 4, out_specs=[spec] * 3,
        out_shape=[jax.ShapeDtypeStruct((L, R, C), F32)] * 3,
        compiler_params=_params(("parallel", "parallel")),
    )(g, w, m, v)


def _sum_slots(x, after, name):
    P, R, C = x.shape

    def body(x_ref, after_ref, o_ref):
        acc = x_ref[0]
        for p in range(1, P):
            acc = acc + x_ref[p]
        o_ref[...] = acc

    return pl.pallas_call(
        body, name=name,
        in_specs=[pl.BlockSpec(memory_space=pltpu.VMEM), pl.BlockSpec(memory_space=pl.ANY)],
        out_specs=pl.BlockSpec(memory_space=pltpu.VMEM),
        out_shape=jax.ShapeDtypeStruct((R, C), F32),
        compiler_params=pltpu.CompilerParams(vmem_limit_bytes=48 * MIB),
    )(x, after)


def _my_pos():
    return lax.axis_index("x"), lax.axis_index("y"), lax.axis_index("c")


def _all_gather_small(x, name, after=()):
    R, C = x.shape
    n_after = len(after)

    def body(x_ref, *rest):
        out_ref, send_sems, recv_sems = rest[n_after:]
        x_, y_, c_ = _my_pos()
        me, sibling = (x_, y_, c_), (x_, y_, 1 - c_)
        chips = [(1 - x_, y_), (x_, 1 - y_), (1 - x_, 1 - y_)]

        def slot(px, py, pc):
            return out_ref.at[4 * px + 2 * py + pc]

        def copy(k, block, to):
            return pltpu.make_async_remote_copy(
                src_ref=slot(*block), dst_ref=slot(*block), send_sem=send_sems.at[k], recv_sem=recv_sems.at[k],
                device_id=to, device_id_type=MESH)

        out_ref[4 * x_ + 2 * y_ + c_] = x_ref[...]
        first = [copy(0, me, sibling)] + [copy(1 + j, me, (*chip, c_)) for j, chip in enumerate(chips)]
        for cp in first:
            cp.start()
        passed = [copy(4 + j, (*chip, c_), sibling) for j, chip in enumerate(chips)]
        for j, chip in enumerate(chips):
            copy(1 + j, (*chip, c_), me).wait_recv()
            passed[j].start()
        copy(0, sibling, me).wait_recv()
        for j, chip in enumerate(chips):
            copy(4 + j, (*chip, 1 - c_), me).wait_recv()
        for cp in first + passed:
            cp.wait_send()

    return pl.pallas_call(
        body, name=name,
        in_specs=[pl.BlockSpec(memory_space=pltpu.VMEM)] + [pl.BlockSpec(memory_space=pl.ANY)] * n_after,
        out_specs=pl.BlockSpec(memory_space=pltpu.VMEM),
        out_shape=jax.ShapeDtypeStruct((N_DEV, R, C), x.dtype),
        scratch_shapes=[pltpu.SemaphoreType.DMA((7,)), pltpu.SemaphoreType.DMA((7,))],
        compiler_params=pltpu.CompilerParams(vmem_limit_bytes=48 * MIB),
    )(x, *after)


HBM_SPEC = pl.BlockSpec(memory_space=pltpu.HBM)
SEM_SPEC = pl.BlockSpec(memory_space=pltpu.SEMAPHORE)
ANY_SPEC = pl.BlockSpec(memory_space=pl.ANY)
SPLIT_EFFECT = pltpu.SideEffectType.DATAFLOW_SIDE_EFFECTING


def _in_hbm(a):
    return pltpu.with_memory_space_constraint(a, pltpu.HBM)


def _gathered_shape(a, kind):
    if kind == "major":
        return (N_DEV,) + a.shape
    if kind == "rows":
        return (N_DEV * a.shape[0], a.shape[1])
    return (2, a.shape[0], 4 * a.shape[1])


def _gather_slot(ref, kind, block, shard_shape):
    px, py, pc = block
    if kind == "major":
        return ref.at[4 * px + 2 * py + pc]
    if kind == "rows":
        r = shard_shape[0]
        return ref.at[pl.ds(pl.multiple_of((4 * px + 2 * py + pc) * r, r), r), :]
    cu = shard_shape[1]
    return ref.at[px, :, pl.ds(pl.multiple_of((2 * py + pc) * cu, cu), cu)]


def _gather_peers():
    x_, y_, c_ = _my_pos()
    return (x_, y_, c_), (x_, y_, 1 - c_), [(1 - x_, y_), (x_, 1 - y_), (1 - x_, 1 - y_)]


def _gather_start(shards, kinds, after, name):
    n = len(shards)
    bufs = [lax.empty(_gathered_shape(a, k), a.dtype) for a, k in zip(shards, kinds)]
    extra = [] if after is None else [after]

    def body(*refs):
        shard_refs, buf_refs = refs[:n], refs[n:2 * n]
        send_sems, recv_sems, local_sems = refs[2 * n + len(extra):2 * n + len(extra) + 3]
        token = refs[-1]
        me, sibling, chips = _gather_peers()
        for e in range(n):
            mine = _gather_slot(buf_refs[e], kinds[e], me, shards[e].shape)
            pltpu.make_async_copy(shard_refs[e], mine, local_sems.at[e]).start()
            for k, to in enumerate([sibling] + [(*chip, me[2]) for chip in chips]):
                pltpu.make_async_remote_copy(
                    src_ref=shard_refs[e], dst_ref=mine, send_sem=send_sems.at[4 * e + k],
                    recv_sem=recv_sems.at[4 * e + k], device_id=to, device_id_type=MESH).start()
        token[...] = jnp.zeros_like(token)

    out = pl.pallas_call(
        body, name=name,
        out_shape=(pltpu.SemaphoreType.DMA((4 * n,)), pltpu.SemaphoreType.DMA((4 * n,)), pltpu.SemaphoreType.DMA((n,)),
                   *[pltpu.HBM(a.shape, a.dtype) for a in shards], *[pltpu.HBM(a.shape, a.dtype) for a in bufs],
                   jax.ShapeDtypeStruct((8, LANE), F32)),
        in_specs=[HBM_SPEC] * (2 * n) + [ANY_SPEC] * len(extra),
        out_specs=(SEM_SPEC, SEM_SPEC, SEM_SPEC, *[HBM_SPEC] * (2 * n), pl.BlockSpec(memory_space=pltpu.VMEM)),
        input_output_aliases={i: 3 + i for i in range(2 * n)},
        compiler_params=pltpu.CompilerParams(has_side_effects=SPLIT_EFFECT),
    )(*[_in_hbm(a) for a in shards], *[_in_hbm(a) for a in bufs], *extra)
    return out[0], out[1], out[2], out[3:3 + n], out[3 + n:3 + 2 * n], out[-1]


def _gather_forward(recv_sems, bufs, kinds, shard_shapes, after, name):
    n = len(bufs)

    def body(*refs):
        buf_refs, recv_in = refs[:n], refs[n]
        fsend, frecv = refs[n + 2], refs[n + 3]
        token = refs[-1]
        me, sibling, chips = _gather_peers()
        for e in range(n):
            for j, chip in enumerate(chips):
                slot = _gather_slot(buf_refs[e], kinds[e], (*chip, me[2]), shard_shapes[e])
                pltpu.make_async_remote_copy(
                    src_ref=slot, dst_ref=slot, send_sem=recv_in.at[4 * e + 1 + j], recv_sem=recv_in.at[4 * e + 1 + j],
                    device_id=me, device_id_type=MESH).wait_recv()
                pltpu.make_async_remote_copy(
                    src_ref=slot, dst_ref=slot, send_sem=fsend.at[3 * e + j], recv_sem=frecv.at[3 * e + j],
                    device_id=sibling, device_id_type=MESH).start()
        token[...] = jnp.zeros_like(token)

    out = pl.pallas_call(
        body, name=name,
        out_shape=(pltpu.SemaphoreType.DMA((3 * n,)), pltpu.SemaphoreType.DMA((3 * n,)),
                   *[pltpu.HBM(a.shape, a.dtype) for a in bufs], jax.ShapeDtypeStruct((8, LANE), F32)),
        in_specs=[HBM_SPEC] * n + [SEM_SPEC, ANY_SPEC],
        out_specs=(SEM_SPEC, SEM_SPEC, *[HBM_SPEC] * n, pl.BlockSpec(memory_space=pltpu.VMEM)),
        input_output_aliases={i: 2 + i for i in range(n)},
        compiler_params=pltpu.CompilerParams(has_side_effects=SPLIT_EFFECT),
    )(*bufs, recv_sems, after)
    return out[0], out[1], out[2:2 + n], out[-1]


def _gather_wait(send_sems, recv_sems, local_sems, fsend, frecv, shards, bufs, kinds, after, name):
    n = len(bufs)

    def body(*refs):
        shard_refs, buf_refs = refs[:n], refs[n:2 * n]
        send_in, recv_in, local_in, fsend_in, frecv_in = refs[2 * n:2 * n + 5]
        me, sibling, chips = _gather_peers()

        def arrival(slot, sem):
            return pltpu.make_async_remote_copy(src_ref=slot, dst_ref=slot, send_sem=sem, recv_sem=sem,
                                                device_id=me, device_id_type=MESH)

        for e in range(n):
            shape = shards[e].shape
            mine = _gather_slot(buf_refs[e], kinds[e], me, shape)
            pltpu.make_async_copy(shard_refs[e], mine, local_in.at[e]).wait()
            arrival(_gather_slot(buf_refs[e], kinds[e], sibling, shape), recv_in.at[4 * e]).wait_recv()
            for j, chip in enumerate(chips):
                arrival(_gather_slot(buf_refs[e], kinds[e], (*chip, 1 - me[2]), shape), frecv_in.at[3 * e + j]).wait_recv()
            for k in range(4):
                arrival(mine, send_in.at[4 * e + k]).wait_send()
            for j in range(3):
                arrival(mine, fsend_in.at[3 * e + j]).wait_send()

    out = pl.pallas_call(
        body, name=name,
        out_shape=(*[pltpu.HBM(a.shape, a.dtype) for a in shards], *[pltpu.HBM(a.shape, a.dtype) for a in bufs]),
        in_specs=[HBM_SPEC] * (2 * n) + [SEM_SPEC] * 5 + [ANY_SPEC],
        out_specs=tuple([HBM_SPEC] * (2 * n)),
        input_output_aliases={i: i for i in range(2 * n)},
        compiler_params=pltpu.CompilerParams(has_side_effects=SPLIT_EFFECT),
    )(*shards, *bufs, send_sems, recv_sems, local_sems, fsend, frecv, after)
    return out[n:]


def _grad_slice(ref, kind, j):
    if kind == "whole":
        return ref
    if kind == "major":
        return ref.at[j]
    if kind == "rows":
        r = ref.shape[0] // N_DEV
        return ref.at[pl.ds(j * r, r), :]
    cu = ref.shape[2] // 4
    return ref.at[j // 4, :, pl.ds((j % 4) * cu, cu)]


def _slice_shape(a, kind):
    if kind == "whole":
        return a.shape
    if kind == "major":
        return a.shape[1:]
    if kind == "rows":
        return (a.shape[0] // N_DEV, a.shape[1])
    return (a.shape[1], a.shape[2] // 4)


def _scatter_copies(srcs, lands, kinds, send_sems, recv_sems):
    x_, y_, c_ = _my_pos()
    me = 4 * x_ + 2 * y_ + c_
    n = len(srcs)

    def remote(e, j):
        return pltpu.make_async_remote_copy(
            src_ref=_grad_slice(srcs[e], kinds[e], j), dst_ref=lands[e].at[me],
            send_sem=send_sems.at[e * N_DEV + j], recv_sem=recv_sems.at[e * N_DEV + me],
            device_id=(j // 4, (j // 2) % 2, j % 2), device_id_type=MESH)

    def local(e, j):
        return pltpu.make_async_copy(_grad_slice(srcs[e], kinds[e], j), lands[e].at[j], recv_sems.at[e * N_DEV + j])

    def arrival(e, i):
        return pltpu.make_async_remote_copy(
            src_ref=_grad_slice(srcs[e], kinds[e], i), dst_ref=lands[e].at[i],
            send_sem=send_sems.at[e * N_DEV + i], recv_sem=recv_sems.at[e * N_DEV + i],
            device_id=(i // 4, (i // 2) % 2, i % 2), device_id_type=MESH)

    def start():
        for e in range(n):
            for j in range(N_DEV):
                @pl.when(me == j)
                def _():
                    local(e, j).start()

                @pl.when(me != j)
                def _():
                    remote(e, j).start()

    def wait():
        for e in range(n):
            for i in range(N_DEV):
                @pl.when(me == i)
                def _():
                    local(e, i).wait()

                @pl.when(me != i)
                def _():
                    arrival(e, i).wait_recv()
        for e in range(n):
            for j in range(N_DEV):
                @pl.when(me != j)
                def _():
                    remote(e, j).wait_send()

    return start, wait


def _scatter_start(srcs, kinds, after, name):
    n = len(srcs)
    lands = [lax.empty((N_DEV,) + _slice_shape(a, k), a.dtype) for a, k in zip(srcs, kinds)]
    extra = [] if after is None else [after]

    def body(*refs):
        src_refs, land_refs = refs[:n], refs[n:2 * n]
        send_sems, recv_sems = refs[2 * n + len(extra)], refs[2 * n + len(extra) + 1]
        token = refs[-1]
        start, _ = _scatter_copies(src_refs, land_refs, kinds, send_sems, recv_sems)
        start()
        token[...] = jnp.zeros_like(token)

    out = pl.pallas_call(
        body, name=name,
        out_shape=(pltpu.SemaphoreType.DMA((n * N_DEV,)), pltpu.SemaphoreType.DMA((n * N_DEV,)),
                   *[pltpu.HBM(a.shape, a.dtype) for a in srcs], *[pltpu.HBM(a.shape, a.dtype) for a in lands],
                   jax.ShapeDtypeStruct((8, LANE), F32)),
        in_specs=[HBM_SPEC] * (2 * n) + [ANY_SPEC] * len(extra),
        out_specs=(SEM_SPEC, SEM_SPEC, *[HBM_SPEC] * (2 * n), pl.BlockSpec(memory_space=pltpu.VMEM)),
        input_output_aliases={i: 2 + i for i in range(2 * n)},
        compiler_params=pltpu.CompilerParams(has_side_effects=SPLIT_EFFECT),
    )(*[_in_hbm(a) for a in srcs], *[_in_hbm(a) for a in lands], *extra)
    return out[0], out[1], out[2:2 + n], out[2 + n:2 + 2 * n], out[-1]


def _scatter_wait(send_sems, recv_sems, srcs, lands, kinds, after, name):
    n = len(srcs)

    def body(*refs):
        src_refs, land_refs = refs[:n], refs[n:2 * n]
        _, wait = _scatter_copies(src_refs, land_refs, kinds, refs[2 * n], refs[2 * n + 1])
        wait()

    out = pl.pallas_call(
        body, name=name,
        out_shape=(*[pltpu.HBM(a.shape, a.dtype) for a in srcs], *[pltpu.HBM(a.shape, a.dtype) for a in lands]),
        in_specs=[HBM_SPEC] * (2 * n) + [SEM_SPEC, SEM_SPEC] + [ANY_SPEC] * len(after),
        out_specs=tuple([HBM_SPEC] * (2 * n)),
        input_output_aliases={i: i for i in range(2 * n)},
        compiler_params=pltpu.CompilerParams(has_side_effects=SPLIT_EFFECT),
    )(*srcs, *lands, send_sems, recv_sems, *after)
    return out[n:]


def _rope_tables(positions, sign):
    half = ROPE_DIM // 2
    inv_freq = ROPE_THETA ** (-jnp.arange(0, ROPE_DIM, 2, dtype=F32) / ROPE_DIM)
    ang = positions.astype(F32)[:, None] * inv_freq
    reps = LANE // half
    cos = jnp.tile(jnp.cos(ang), (1, reps))
    sin = jnp.tile(jnp.sin(ang), (1, reps)) * sign
    d = jnp.arange(LANE) % SWA_HEAD_DIM
    return (jnp.where(d < ROPE_DIM, cos, 1.0), jnp.where(d < half, -sin, 0.0),
            jnp.where((d >= half) & (d < ROPE_DIM), sin, 0.0))


def _pad_cols(a, n):
    return jnp.pad(a, ((0, 0), (0, n - a.shape[1])))


def _local_step(x, target, positions, mods, fetch, P, on_grads, on_small):
    S, D = x.shape
    Hf = D // FOX_HEAD_DIM
    Hq = D // SWA_HEAD_DIM
    Hk = Hq // SWA_GROUP
    Wk = Hk * SWA_HEAD_DIM
    n_in = 3 * D + Hf
    (sh1a, sc1a, g1a, sh2a, sc2a, g2a), (sh1b, sc1b, g1b, sh2b, sc2b, g2b) = mods
    row = lambda v: v.reshape(1, -1)
    cw = [jnp.transpose(P["conv_w"][l].reshape(3, 2, -1), (1, 0, 2)) for l in range(2)]
    cb = [P["conv_b"][l].reshape(2, 1, -1) for l in range(2)]

    W = dict(up=[None, None], down=[None, None])
    h1a = _modulate(x, sc1a, sh1a, "modulate_in")
    W["fox_in"], W["fox_o"] = fetch("fox", "wait", h1a)
    proj_a = _mm_nn(h1a, W["fox_in"], F32, "fox_in_proj", tn=896)
    flT = proj_a[:, 3 * D:n_in].T
    bf_col = P["fox_b_f"].reshape(Hf, 1)
    cumT = _fox_prep(flT, bf_col, "fox_cumsum")
    cum_row = cumT.reshape(Hf, 1, S)
    o_a, lse_a = _fox_attn_fwd(proj_a, cum_row, Hf, "fox_attn_fwd")
    token = fetch("ffn0", "forward", o_a)
    y1a = _mm_nn(o_a, W["fox_o"], F32, "fox_out_proj")
    z1a, x1, h2a = _ln_fwd(x, y1a, g1a, row(P["ln_mix_g"][0]), row(P["ln_mix_b"][0]), sc2a, sh2a, token, "ln_mix0")
    W["up"][0], W["down"][0] = fetch("ffn0", "wait", h2a)
    u_a, uc_a, a_a = _ffn_up(h2a, W["up"][0], cw[0], cb[0], "ffn_up0")
    token = fetch("swa", "forward", a_a)
    y2a = _mm_nn(a_a, W["down"][0], F32, "ffn_down0", tk=2816)
    z2a, x2, h1b = _ln_fwd(x1, y2a, g2a, row(P["ln_ffn_g"][0]), row(P["ln_ffn_b"][0]), sc1b, sh1b, token, "ln_ffn0")

    W["swa_in"], W["swa_o"] = fetch("swa", "wait", h1b)
    proj_b = _mm_nn(h1b, W["swa_in"], F32, "swa_in_proj")
    tabs_f = _rope_tables(positions, 1.0)
    tabs_b = _rope_tables(positions, -1.0)
    qk = _rope([(proj_b, True, D + Wk)], tabs_f, F32, "rope_fwd")
    sinks = _pad_cols(P["swa_sinks"].reshape(1, Hq), LANE)
    o_b, lse_b = _swa_attn_fwd(qk, proj_b, sinks, Hq, "swa_attn_fwd")
    token = fetch("ffn1", "forward", o_b)
    y1b = _mm_nn(o_b, W["swa_o"], F32, "swa_out_proj")
    z1b, x3, h2b = _ln_fwd(x2, y1b, g1b, row(P["ln_mix_g"][1]), row(P["ln_mix_b"][1]), sc2b, sh2b, token, "ln_mix1")
    W["up"][1], W["down"][1] = fetch("ffn1", "wait", h2b)
    u_b, uc_b, a_b = _ffn_up(h2b, W["up"][1], cw[1], cb[1], "ffn_up1")
    y2b = _mm_nn(a_b, W["down"][1], F32, "ffn_down1", tk=2816)
    z2b, dout, loss_row = _ln_fwd_loss(x3, y2b, g2b, row(P["ln_ffn_g"][1]), row(P["ln_ffn_b"][1]), target, "ln_ffn1_loss")

    def ffn_backward(dy, a, u, uc, h_in, l, tag):
        d_down = _mm_tn(a, dy[None], BF16, "ffn_dwdown" + tag, tm=1408, tn=1024)[0]
        token = on_grads("ffn_w_down" + tag, d_down)
        da = _mm_nt(dy[None], W["down"][l][None], F32, "ffn_da" + tag)
        du, dcw, dcb = _ffn_bwd_elem(da, u, uc, cw[l], token, "ffn_bwd_elem" + tag)
        d_up = _mm_tn(h_in, du, BF16, "ffn_dwup" + tag)
        token = on_grads("ffn_w_up" + tag, d_up)
        dh = _mm_nt(du, W["up"][l], F32, "ffn_dh" + tag, tn=1024, tk=1408)
        return dh, token, jnp.transpose(dcw, (1, 0, 2)).reshape(3, -1), dcb.reshape(-1)

    dz2b, dy2b, dg_f1, db_f1, dgate2b = _ln_bwd(dout, z2b, y2b, g2b, row(P["ln_ffn_g"][1]), "ln_ffn1_bwd")
    dh2b, token, dcw1, dcb1 = ffn_backward(dy2b, a_b, u_b, uc_b, h2b, 1, "1")
    dx3, dsc2b, dsh2b = _mod_bwd(dz2b, dh2b, x3, sc2b, token, "mod_ffn1_bwd")

    dz1b, dy1b, dg_m1, db_m1, dgate1b = _ln_bwd(dx3, z1b, y1b, g1b, row(P["ln_mix_g"][1]), "ln_mix1_bwd")
    token = on_grads("swa_w_o", _mm_tn(o_b, dy1b[None], BF16, "swa_dwo")[0])
    do_b = _mm_nt(dy1b[None], W["swa_o"][None], F32, "swa_do")
    dq_b, dk_b, dv_b, dsinks = _swa_attn_bwd(qk, proj_b, sinks, do_b, lse_b, Hq, token, "swa_attn_bwd")
    dproj_b = _rope([(dq_b, True, D), (dk_b, True, Wk), (dv_b, False, Wk)], tabs_b, BF16, "rope_bwd")
    token = on_grads("swa_w_in", _mm_tn(dproj_b, h1b[None], BF16, "swa_dwin", tm=1280, tn=1024)[0])
    dh1b = _mm_nt(dproj_b[None], W["swa_in"][None], F32, "swa_dh", tk=1280)
    dx2, dsc1b, dsh1b = _mod_bwd(dz1b, dh1b, x2, sc1b, token, "mod_mix1_bwd")

    dz2a, dy2a, dg_f0, db_f0, dgate2a = _ln_bwd(dx2, z2a, y2a, g2a, row(P["ln_ffn_g"][0]), "ln_ffn0_bwd")
    dh2a, token, dcw0, dcb0 = ffn_backward(dy2a, a_a, u_a, uc_a, h2a, 0, "0")
    dx1, dsc2a, dsh2a = _mod_bwd(dz2a, dh2a, x1, sc2a, token, "mod_ffn0_bwd")

    dz1a, dy1a, dg_m0, db_m0, dgate1a = _ln_bwd(dx1, z1a, y1a, g1a, row(P["ln_mix_g"][0]), "ln_mix0_bwd")
    token = on_grads("fox_w_o", _mm_tn(o_a, dy1a[None], BF16, "fox_dwo")[0])
    do_a = _mm_nt(dy1a[None], W["fox_o"][None], F32, "fox_do")
    dq_a, dk_a, dv_a, dcum_row, dcum_col = _fox_attn_bwd(proj_a, do_a, o_a, cum_row, lse_a, Hf, token, "fox_attn_bwd")
    dflT, dbf = _fox_prep_bwd(dcum_row.reshape(Hf, S), dcum_col.reshape(Hf, S), flT, bf_col, "fox_cumsum_bwd")
    n_pad = W["fox_in"].shape[1]
    dproj_a = jnp.concatenate([dq_a, dk_a.astype(BF16), dv_a.astype(BF16),
                               _pad_cols(dflT.T, n_pad - 3 * D).astype(BF16)], axis=1)
    dh1a = _mm_nt(dproj_a[None], W["fox_in"][None], F32, "fox_dh", tn=1024, tk=896)
    grad_x, dsc1a, dsh1a = _mod_bwd(dz1a, dh1a, x, sc1a, token, "mod_mix0_bwd")

    dmod = jnp.stack([jnp.concatenate([dsh1a, dsc1a, dgate1a, dsh2a, dsc2a, dgate2a], axis=1)[0],
                      jnp.concatenate([dsh1b, dsc1b, dgate1b, dsh2b, dsc2b, dgate2b], axis=1)[0]])
    small = dict(dmod=dmod, conv_b=jnp.stack([dcb0, dcb1]), conv_w=jnp.stack([dcw0, dcw1]),
                 ln_mix_g=jnp.concatenate([dg_m0, dg_m1]), ln_mix_b=jnp.concatenate([db_m0, db_m1]),
                 ln_ffn_g=jnp.concatenate([dg_f0, dg_f1]), ln_ffn_b=jnp.concatenate([db_f0, db_f1]),
                 fox_b_f=dbf.reshape(-1), swa_sinks=dsinks[0, :Hq], loss=loss_row[0, 0].reshape(1))
    exchanged = on_small(small)
    on_grads("fox_w_in", _mm_tn(dproj_a, h1a[None], BF16, "fox_dwin", tm=896, tn=1024)[0], exchanged)
    return grad_x


SMALL_ORDER = ("dmod", "conv_b", "conv_w", "ln_mix_g", "ln_mix_b", "ln_ffn_g", "ln_ffn_b", "fox_b_f", "swa_sinks", "loss")


def _pack_rows(arrays):
    chunks, spans, off = [], [], 0
    for a in arrays:
        flat = a.reshape(-1)
        n = -(-flat.shape[0] // LANE) * LANE
        chunks.append(jnp.pad(flat, (0, n - flat.shape[0])))
        spans.append((off, flat.shape[0], a.shape))
        off += n
    total = -(-off // (8 * LANE)) * (8 * LANE)
    chunks.append(jnp.zeros((total - off,), F32))
    return jnp.concatenate(chunks).reshape(-1, LANE), spans


def _unpack_rows(packed, spans):
    flat = packed.reshape(-1)
    return [flat[off:off + n].reshape(shape) for off, n, shape in spans]


def kernel(x, c, positions, fox_w_in, fox_b_f, fox_w_o, swa_w_in, swa_sinks, swa_w_o, ada_w, ada_b, ffn_w_up, ffn_conv_w, ffn_conv_b, ffn_w_down, ln_mix_g, ln_mix_b, ln_ffn_g, ln_ffn_b, loss_target, m_fox_w_in, m_fox_b_f, m_fox_w_o, m_swa_w_in, m_swa_sinks, m_swa_w_o, m_ada_w, m_ada_b, m_ffn_w_up, m_ffn_conv_w, m_ffn_conv_b, m_ffn_w_down, m_ln_mix_g, m_ln_mix_b, m_ln_ffn_g, m_ln_ffn_b, v_fox_w_in, v_fox_b_f, v_fox_w_o, v_swa_w_in, v_swa_sinks, v_swa_w_o, v_ada_w, v_ada_b, v_ffn_w_up, v_ffn_conv_w, v_ffn_conv_b, v_ffn_w_down, v_ln_mix_g, v_ln_mix_b, v_ln_ffn_g, v_ln_ffn_b):
    S, D = x.shape[1], x.shape[2]
    L = ada_w.shape[0]
    me = 4 * lax.axis_index("x") + 2 * lax.axis_index("y") + lax.axis_index("c")
    n_ada = ada_w.shape[2]
    cu = ffn_w_up.shape[2]
    F = 4 * cu
    n_in = fox_w_in.shape[2] * N_DEV
    n_in_pad = -(-n_in // LANE) * LANE

    gather_groups = dict(
        fox=([fox_w_in[0].astype(BF16), fox_w_o[0].astype(BF16)], ["major", "rows"]),
        ffn0=([ffn_w_up[0].astype(BF16), ffn_w_down[0].astype(BF16)], ["halves", "rows"]),
        swa=([swa_w_in[0].astype(BF16), swa_w_o[0].astype(BF16)], ["major", "rows"]),
        ffn1=([ffn_w_up[1].astype(BF16), ffn_w_down[1].astype(BF16)], ["halves", "rows"]))
    starts_after = dict(fox=["ffn0", "swa"], ffn0=["ffn1"])
    gathers = {}

    def start_group(group, after):
        shards, kinds = gather_groups[group]
        send, recv, local, thru, bufs, token = _gather_start(shards, kinds, after, "gather_start_" + group)
        gathers[group] = dict(send=send, recv=recv, local=local, shards=thru, bufs=bufs, kinds=kinds,
                              shapes=[a.shape for a in shards], token=token)
        return token

    fox_started = start_group("fox", None)

    c_all = _all_gather_small(c.reshape(-1, LANE), "gather_c", after=[fox_started]).reshape(N_DEV, D)
    b_cols = lax.dynamic_slice_in_dim(ada_b, me * n_ada, n_ada, axis=1).reshape(L, 1, n_ada)
    mod_blk = _ada_mod(jnp.pad(c_all, ((0, ADA_ROWS - N_DEV), (0, 0))), ada_w, b_cols, "ada_mod")[:, :N_DEV]
    mod_all = _all_gather_small(mod_blk.reshape(-1, LANE), "gather_mod").reshape(N_DEV, L, N_DEV, n_ada)
    mod_mine = lax.dynamic_index_in_dim(mod_all, me, axis=2, keepdims=False)
    mod_mine = jnp.transpose(mod_mine, (1, 0, 2)).reshape(L, N_DEV * n_ada)
    mods = [[mod_mine[l, k * D:(k + 1) * D].reshape(1, D) for k in range(6)] for l in range(L)]

    P = dict(fox_b_f=fox_b_f[0], swa_sinks=swa_sinks[0], conv_b=ffn_conv_b,
             ln_mix_g=ln_mix_g, ln_mix_b=ln_mix_b, ln_ffn_g=ln_ffn_g, ln_ffn_b=ln_ffn_b)
    cw_rows = _all_gather_small(_pack_rows([ffn_conv_w])[0], "gather_conv_w", after=[mod_all])
    n_cw = ffn_conv_w.size
    cw_dev = cw_rows.reshape(N_DEV, -1)[:, :n_cw].reshape(N_DEV, L, 3, cu)
    P["conv_w"] = jnp.transpose(cw_dev, (1, 2, 0, 3)).reshape(L, 3, N_DEV * cu)

    def natural(g, pad_to=None):
        slabs = [g[k] for k in range(N_DEV)]
        if pad_to is not None:
            slabs.append(jnp.zeros((D, pad_to - N_DEV * g.shape[2]), g.dtype))
        return jnp.concatenate(slabs, axis=1)

    def forward_stage(group, after):
        s = gathers[group]
        s["fsend"], s["frecv"], s["bufs"], token = _gather_forward(s["recv"], s["bufs"], s["kinds"], s["shapes"], after,
                                                                   "gather_forward_" + group)
        for nxt in starts_after.get(group, ()):
            token = start_group(nxt, token)
        return token

    def fetch(group, stage, after):
        if stage == "forward":
            return forward_stage(group, after)
        if group == "fox":
            after = forward_stage(group, cw_rows)
        s = gathers.pop(group)
        first, second = _gather_wait(s["send"], s["recv"], s["local"], s["fsend"], s["frecv"], s["shards"], s["bufs"],
                                     s["kinds"], after, "gather_wait_" + group)
        if group == "fox":
            return natural(first, n_in_pad), second
        if group == "swa":
            return natural(first), second
        return first, second

    out, pending = {}, {}

    def columns_major(g_t, n):
        return g_t[:n].reshape(N_DEV, n // N_DEV, D)

    def transposed(a):
        return jnp.transpose(a, (0, 2, 1))

    big = dict(
        ffn_w_down1=("ffn_w_down", "rows", 1, (ffn_w_down, m_ffn_w_down, v_ffn_w_down)),
        ffn_w_up1=("ffn_w_up", "halves", 1, (ffn_w_up, m_ffn_w_up, v_ffn_w_up)),
        swa_w_o=("swa_w_o", "rows", 0, (swa_w_o, m_swa_w_o, v_swa_w_o)),
        swa_w_in=("swa_w_in", "major", 0, tuple(transposed(a) for a in (swa_w_in, m_swa_w_in, v_swa_w_in))),
        ffn_w_down0=("ffn_w_down", "rows", 0, (ffn_w_down, m_ffn_w_down, v_ffn_w_down)),
        ffn_w_up0=("ffn_w_up", "halves", 0, (ffn_w_up, m_ffn_w_up, v_ffn_w_up)),
        fox_w_o=("fox_w_o", "rows", 0, (fox_w_o, m_fox_w_o, v_fox_w_o)),
        fox_w_in=("fox_w_in", "major", 0, (fox_w_in, m_fox_w_in, v_fox_w_in)))
    finish_at = dict(swa_w_o=["ffn_w_down1"], ffn_w_up0=["ffn_w_up1", "swa_w_o", "swa_w_in"], fox_w_o=["ffn_w_down0"],
                     fox_w_in=["ffn_w_up0"])
    tail = {}

    def finish(name, after):
        send, recv, thru, lands = pending.pop(name)
        param, kind, layer, wmv = big[name]
        if name == "fox_w_in":
            one = tail["last_start"][0, 0] + 1.0
            wmv = tuple(transposed(a * one) for a in wmv)
        landed, = _scatter_wait(send, recv, thru, lands, [kind], after, "scatter_wait_" + name)
        res = _adam_sum(landed, *wmv, layer, out.get(param), "adam_" + name, by_cols=kind == "major")
        out[param] = [transposed(r) for r in res] if kind == "major" else res

    def on_grads(name, g, after=None):
        kind = big[name][1]
        src = columns_major(g, n_in if name == "fox_w_in" else g.shape[0]) if kind == "major" else g
        send, recv, thru, lands, token = _scatter_start([src], [kind], after, "scatter_start_" + name)
        pending[name] = (send, recv, thru, lands)
        for done in finish_at.get(name, ()):
            finish(done, [token])
        tail["last_start"] = token
        return token

    def on_small(small):
        packed, tail["spans"] = _pack_rows([small[k] for k in SMALL_ORDER])
        send, recv, thru, lands, token = _scatter_start([packed], ["whole"], None, "small_grads_start")
        tail["small"] = (send, recv, thru, lands)
        return token

    grad_x = _local_step(x[0], loss_target[0], positions[0], mods, fetch, P, on_grads, on_small)

    spans = tail["spans"]
    done_first = [out[k][1] for k in ("ffn_w_up", "ffn_w_down", "swa_w_in", "swa_w_o")]
    gathered, = _scatter_wait(*tail["small"], ["whole"], done_first, "small_grads_wait")
    totals = dict(zip(SMALL_ORDER, _unpack_rows(_sum_slots(gathered, tail["last_start"], "sum_small_grads"), spans)))
    loss = totals["loss"].reshape(())
    n_mod = L * 6 * D
    dmod_all = gathered.reshape(N_DEV, -1)[:, :n_mod].reshape(N_DEV, L, 6 * D)
    dmod_cols = jnp.transpose(lax.dynamic_slice_in_dim(dmod_all, me * n_ada, n_ada, axis=2), (1, 0, 2))
    out["ada_w"] = _ada_bwd_adam(c_all.T, dmod_cols, ada_w, m_ada_w, v_ada_w, "adam_ada_w")

    g_small = dict(fox_b_f=totals["fox_b_f"].reshape(fox_b_f.shape), swa_sinks=totals["swa_sinks"].reshape(swa_sinks.shape),
                   ada_b=totals["dmod"].reshape(ada_b.shape), ffn_conv_b=totals["conv_b"].reshape(ffn_conv_b.shape),
                   ffn_conv_w=lax.dynamic_slice_in_dim(totals["conv_w"].reshape(L, 3, 2 * F), me * cu, cu, axis=2),
                   ln_mix_g=totals["ln_mix_g"], ln_mix_b=totals["ln_mix_b"],
                   ln_ffn_g=totals["ln_ffn_g"], ln_ffn_b=totals["ln_ffn_b"])
    small_names = ("fox_b_f", "swa_sinks", "ada_b", "ffn_conv_b", "ffn_conv_w", "ln_mix_g", "ln_mix_b", "ln_ffn_g", "ln_ffn_b")
    w_small = dict(fox_b_f=(fox_b_f, m_fox_b_f, v_fox_b_f), swa_sinks=(swa_sinks, m_swa_sinks, v_swa_sinks),
                   ada_b=(ada_b, m_ada_b, v_ada_b), ffn_conv_b=(ffn_conv_b, m_ffn_conv_b, v_ffn_conv_b),
                   ffn_conv_w=(ffn_conv_w, m_ffn_conv_w, v_ffn_conv_w),
                   ln_mix_g=(ln_mix_g, m_ln_mix_g, v_ln_mix_g), ln_mix_b=(ln_mix_b, m_ln_mix_b, v_ln_mix_b),
                   ln_ffn_g=(ln_ffn_g, m_ln_ffn_g, v_ln_ffn_g), ln_ffn_b=(ln_ffn_b, m_ln_ffn_b, v_ln_ffn_b))
    pk_g, sp = _pack_rows([g_small[k] for k in small_names])
    pk_w = _pack_rows([w_small[k][0] for k in small_names])[0]
    pk_m = _pack_rows([w_small[k][1] for k in small_names])[0]
    pk_v = _pack_rows([w_small[k][2] for k in small_names])[0]
    res = _adam(pk_g[None], pk_w[None], pk_m[None], pk_v[None], "adam_small")
    settled = [res[0], out["ada_w"][1]] + [out[k][1] for k in ("ffn_w_up", "ffn_w_down", "swa_w_in", "swa_w_o")]
    finish("fox_w_o", settled)
    finish("fox_w_in", settled)
    res = [dict(zip(small_names, _unpack_rows(r[0], sp))) for r in res]
    for k in small_names:
        out[k] = (g_small[k], res[0][k], res[1][k], res[2][k])

    order = ("fox_w_in", "fox_b_f", "fox_w_o", "swa_w_in", "swa_sinks", "swa_w_o", "ada_w", "ada_b", "ffn_w_up",
             "ffn_conv_w", "ffn_conv_b", "ffn_w_down", "ln_mix_g", "ln_mix_b", "ln_ffn_g", "ln_ffn_b")
    return (loss, grad_x[None], *[out[k][0] for k in order], *[out[k][1] for k in order],
            *[out[k][2] for k in order], *[out[k][3] for k in order])
```

```python
import functools

import jax
import jax.numpy as jnp
from jax import lax
from jax.experimental import pallas as pl
from jax.experimental.pallas import tpu as pltpu

F32 = jnp.float32
BF16 = jnp.bfloat16
MESH = pl.DeviceIdType.MESH
N_DEV = 8
AXES = ("x", "y", "c")

DEPTH = 2
ALPHA = (2.0 * DEPTH) ** 0.25
LN_EPS = 1e-5
FOX_HEAD_DIM = 128
SWA_HEAD_DIM = 64
SWA_GROUP = 8
SWA_WINDOW = 128
Q_BLOCK = 128
ROPE_DIM = 16
ROPE_THETA = 500000.0

ADAM_LR = 0.001
ADAM_B1 = 0.9
ADAM_B2 = 0.999
ADAM_EPS = 1e-08
ADAM_WD = 0.01
ADAM_STEP = 10

LANE = 128
MIB = 1024 * 1024


def _tile(n, pref, unit=LANE):
    if n <= pref:
        return n
    t = (pref // unit) * unit
    while t >= unit:
        if n % t == 0:
            return t
        t -= unit
    return n


def _params(sem, vmem_mib=48):
    return pltpu.CompilerParams(dimension_semantics=sem, vmem_limit_bytes=vmem_mib * MIB)


def _sigmoid(x):
    return 1.0 / (1.0 + jnp.exp(-x))


def _mm_call(dot, grid_mnk, in_specs, out_spec, out_shape, k_axis, nk, tm, tn, name, operands):
    sem = ("parallel",) * (len(grid_mnk) - 1) + ("arbitrary",)

    if nk == 1:
        def body(a_ref, b_ref, o_ref):
            o_ref[...] = dot(a_ref[...], b_ref[...]).astype(o_ref.dtype)
        scratch = []
    else:
        def body(a_ref, b_ref, o_ref, acc_ref):
            k = pl.program_id(k_axis)

            @pl.when(k == 0)
            def _():
                acc_ref[...] = jnp.zeros_like(acc_ref)

            acc_ref[...] += dot(a_ref[...], b_ref[...])

            @pl.when(k == nk - 1)
            def _():
                o_ref[...] = acc_ref[...].astype(o_ref.dtype)
        scratch = [pltpu.VMEM((tm, tn), F32)]

    return pl.pallas_call(
        body, name=name, grid=grid_mnk, in_specs=in_specs, out_specs=out_spec, out_shape=out_shape,
        scratch_shapes=scratch, compiler_params=_params(sem, 56),
    )(*operands)


def _dot(dims):
    def dot(a, b):
        return lax.dot_general(a.astype(BF16), b.astype(BF16), (dims, ((), ())), preferred_element_type=F32)
    return dot


def _mm_nn(a, b, out_dtype, name, tm=2048, tn=512, tk=2048):
    M, K = a.shape
    N = b.shape[1]
    tm, tn, tk = _tile(M, tm), _tile(N, tn), _tile(K, tk)
    nk = K // tk
    return _mm_call(
        _dot(((1,), (0,))), (M // tm, N // tn, nk),
        [pl.BlockSpec((tm, tk), lambda i, j, k: (i, k)), pl.BlockSpec((tk, tn), lambda i, j, k: (k, j))],
        pl.BlockSpec((tm, tn), lambda i, j, k: (i, j)), jax.ShapeDtypeStruct((M, N), out_dtype),
        2, nk, tm, tn, name, (a, b))


def _mm_nt(a, b, out_dtype, name, tm=2048, tn=512, tk=2048):
    P, M, K = a.shape
    N = b.shape[1]
    tm, tn, tk = _tile(M, tm), _tile(N, tn), _tile(K, tk)
    nk = K // tk
    return _mm_call(
        _dot(((1,), (1,))), (M // tm, N // tn, P * nk),
        [pl.BlockSpec((None, tm, tk), lambda i, j, k: (k // nk, i, k % nk)),
         pl.BlockSpec((None, tn, tk), lambda i, j, k: (k // nk, j, k % nk))],
        pl.BlockSpec((tm, tn), lambda i, j, k: (i, j)), jax.ShapeDtypeStruct((M, N), out_dtype),
        2, P * nk, tm, tn, name, (a, b))


def _mm_tn(a, b, out_dtype, name, tm=2048, tn=512, tk=2048):
    K, M = a.shape
    P, _, N = b.shape
    tm, tn, tk = _tile(M, tm), _tile(N, tn), _tile(K, tk)
    nk = K // tk
    return _mm_call(
        _dot(((0,), (0,))), (P, M // tm, N // tn, nk),
        [pl.BlockSpec((tk, tm), lambda p, i, j, k: (k, i)), pl.BlockSpec((None, tk, tn), lambda p, i, j, k: (p, k, j))],
        pl.BlockSpec((None, tm, tn), lambda p, i, j, k: (p, i, j)), jax.ShapeDtypeStruct((P, M, N), out_dtype),
        3, nk, tm, tn, name, (a, b))


ROW_TILE = 256


def _row_spec(tm, D):
    return pl.BlockSpec((tm, D), lambda i: (i, 0))


def _vec_spec(D):
    return pl.BlockSpec((1, D), lambda i: (0, 0))


def _modulate(x, sc, sh, name):
    S, D = x.shape
    tm = _tile(S, ROW_TILE, 8)

    def body(x_ref, sc_ref, sh_ref, h_ref):
        h_ref[...] = (x_ref[...] * (1.0 + sc_ref[...]) + sh_ref[...]).astype(BF16)

    return pl.pallas_call(
        body, name=name, grid=(S // tm,),
        in_specs=[_row_spec(tm, D), _vec_spec(D), _vec_spec(D)],
        out_specs=_row_spec(tm, D),
        out_shape=jax.ShapeDtypeStruct((S, D), BF16),
        compiler_params=_params(("parallel",)),
    )(x, sc, sh)


def _layer_norm_rows(z, gamma, beta):
    mu = jnp.mean(z, axis=-1, keepdims=True)
    zc = z - mu
    var = jnp.mean(zc * zc, axis=-1, keepdims=True)
    return zc * lax.rsqrt(var + LN_EPS) * gamma + beta


def _ln_fwd(x, y, gate, gamma, beta, sc_n, sh_n, after, name):
    S, D = x.shape
    tm = _tile(S, ROW_TILE, 8)

    def body(x_ref, y_ref, gate_ref, g_ref, b_ref, sc_ref, sh_ref, after_ref, z_ref, xo_ref, hn_ref):
        z = ALPHA * x_ref[...] + (1.0 + gate_ref[...]) * y_ref[...]
        xo = _layer_norm_rows(z, g_ref[...], b_ref[...])
        z_ref[...] = z
        xo_ref[...] = xo
        hn_ref[...] = (xo * (1.0 + sc_ref[...]) + sh_ref[...]).astype(BF16)

    return pl.pallas_call(
        body, name=name, grid=(S // tm,),
        in_specs=[_row_spec(tm, D), _row_spec(tm, D)] + [_vec_spec(D)] * 5 + [pl.BlockSpec(memory_space=pl.ANY)],
        out_specs=[_row_spec(tm, D)] * 3,
        out_shape=[jax.ShapeDtypeStruct((S, D), F32), jax.ShapeDtypeStruct((S, D), F32),
                   jax.ShapeDtypeStruct((S, D), BF16)],
        compiler_params=_params(("parallel",)),
    )(x, y, gate, gamma, beta, sc_n, sh_n, after)


def _ln_fwd_loss(x, y, gate, gamma, beta, target, name):
    S, D = x.shape
    tm = _tile(S, ROW_TILE, 8)

    def body(x_ref, y_ref, gate_ref, g_ref, b_ref, t_ref, z_ref, dout_ref, loss_ref):
        @pl.when(pl.program_id(0) == 0)
        def _():
            loss_ref[...] = jnp.zeros_like(loss_ref)

        z = ALPHA * x_ref[...] + (1.0 + gate_ref[...]) * y_ref[...]
        xo = _layer_norm_rows(z, g_ref[...], b_ref[...])
        err = xo - t_ref[...]
        z_ref[...] = z
        dout_ref[...] = err * (1.0 / D)
        loss_ref[...] += (0.5 / D) * jnp.sum(err * err)

    return pl.pallas_call(
        body, name=name, grid=(S // tm,),
        in_specs=[_row_spec(tm, D), _row_spec(tm, D)] + [_vec_spec(D)] * 3 + [_row_spec(tm, D)],
        out_specs=[_row_spec(tm, D), _row_spec(tm, D), pl.BlockSpec((1, LANE), lambda i: (0, 0))],
        out_shape=[jax.ShapeDtypeStruct((S, D), F32), jax.ShapeDtypeStruct((S, D), F32),
                   jax.ShapeDtypeStruct((1, LANE), F32)],
        compiler_params=_params(("arbitrary",)),
    )(x, y, gate, gamma, beta, target)


def _ln_bwd(dout, z, y, gate, gamma, name):
    S, D = z.shape
    tm = _tile(S, ROW_TILE, 8)

    def body(dout_ref, z_ref, y_ref, gate_ref, g_ref, dz_ref, dy_ref, dg_ref, db_ref, dgate_ref):
        @pl.when(pl.program_id(0) == 0)
        def _():
            dg_ref[...] = jnp.zeros_like(dg_ref)
            db_ref[...] = jnp.zeros_like(db_ref)
            dgate_ref[...] = jnp.zeros_like(dgate_ref)

        z = z_ref[...]
        dout = dout_ref[...]
        mu = jnp.mean(z, axis=-1, keepdims=True)
        zc = z - mu
        var = jnp.mean(zc * zc, axis=-1, keepdims=True)
        rstd = lax.rsqrt(var + LN_EPS)
        xhat = zc * rstd
        dxhat = dout * g_ref[...]
        m1 = jnp.mean(dxhat, axis=-1, keepdims=True)
        m2 = jnp.mean(dxhat * xhat, axis=-1, keepdims=True)
        dz = rstd * (dxhat - m1 - xhat * m2)
        dz_ref[...] = dz
        dy_ref[...] = (dz * (1.0 + gate_ref[...])).astype(BF16)
        dg_ref[...] += jnp.sum(dout * xhat, axis=0, keepdims=True)
        db_ref[...] += jnp.sum(dout, axis=0, keepdims=True)
        dgate_ref[...] += jnp.sum(dz * y_ref[...], axis=0, keepdims=True)

    return pl.pallas_call(
        body, name=name, grid=(S // tm,),
        in_specs=[_row_spec(tm, D)] * 3 + [_vec_spec(D)] * 2,
        out_specs=[_row_spec(tm, D), _row_spec(tm, D)] + [_vec_spec(D)] * 3,
        out_shape=[jax.ShapeDtypeStruct((S, D), F32), jax.ShapeDtypeStruct((S, D), BF16)]
        + [jax.ShapeDtypeStruct((1, D), F32)] * 3,
        compiler_params=_params(("arbitrary",)),
    )(dout, z, y, gate, gamma)


def _mod_bwd(dz, dh, xin, sc, after, name):
    S, D = dz.shape
    tm = _tile(S, ROW_TILE, 8)

    def body(dz_ref, dh_ref, x_ref, sc_ref, after_ref, dx_ref, dsc_ref, dsh_ref):
        @pl.when(pl.program_id(0) == 0)
        def _():
            dsc_ref[...] = jnp.zeros_like(dsc_ref)
            dsh_ref[...] = jnp.zeros_like(dsh_ref)

        dh = dh_ref[...]
        dx_ref[...] = ALPHA * dz_ref[...] + dh * (1.0 + sc_ref[...])
        dsc_ref[...] += jnp.sum(dh * x_ref[...], axis=0, keepdims=True)
        dsh_ref[...] += jnp.sum(dh, axis=0, keepdims=True)

    return pl.pallas_call(
        body, name=name, grid=(S // tm,),
        in_specs=[_row_spec(tm, D)] * 3 + [_vec_spec(D), pl.BlockSpec(memory_space=pl.ANY)],
        out_specs=[_row_spec(tm, D), _vec_spec(D), _vec_spec(D)],
        out_shape=[jax.ShapeDtypeStruct((S, D), F32)] + [jax.ShapeDtypeStruct((1, D), F32)] * 2,
        compiler_params=_params(("arbitrary",)),
    )(dz, dh, xin, sc, after)


def _shift_down(u, k, row):
    return jnp.where(row >= k, pltpu.roll(u, k, axis=0), 0.0)


def _shift_up(u, k, row, S):
    return jnp.where(row < S - k, pltpu.roll(u, S - k, axis=0), 0.0)


def _ffn_up(h, w, cw, cb, name):
    S, D = h.shape
    F = w.shape[2]
    tn = _tile(F, 256)

    piece = S // 4 if S % 32 == 0 else S
    halo = 8

    def body(h_ref, w_ref, cw_ref, cb_ref, u_ref, uc_ref, a_ref):
        tails = [None, None]
        for lo in range(0, S, piece):
            rows = slice(lo, lo + piece)
            conv = []
            for p in range(2):
                u = jnp.dot(h_ref[rows, :], w_ref[p], preferred_element_type=F32)
                u_ref[p, rows, :] = u.astype(BF16)
                above, tails[p] = tails[p], u[piece - halo:]
                if above is not None:
                    u = jnp.concatenate([above, u], axis=0)
                row = lax.broadcasted_iota(jnp.int32, u.shape, 0)
                cwp = cw_ref[p]
                uc = _shift_down(u, 2, row) * cwp[0:1] + _shift_down(u, 1, row) * cwp[1:2] + u * cwp[2:3] + cb_ref[p]
                uc = uc if above is None else uc[halo:]
                uc_ref[p, rows, :] = uc.astype(BF16)
                conv.append(uc)
            g, v = conv
            a_ref[rows, :] = (g * _sigmoid(g) * v).astype(BF16)

    half = pl.BlockSpec((2, S, tn), lambda j: (0, 0, j))
    return pl.pallas_call(
        body, name=name, grid=(F // tn,),
        in_specs=[pl.BlockSpec((S, D), lambda j: (0, 0)), pl.BlockSpec((2, D, tn), lambda j: (0, 0, j)),
                  pl.BlockSpec((2, 3, tn), lambda j: (0, 0, j)), pl.BlockSpec((2, 1, tn), lambda j: (0, 0, j))],
        out_specs=[half, half, pl.BlockSpec((S, tn), lambda j: (0, j))],
        out_shape=[jax.ShapeDtypeStruct((2, S, F), BF16), jax.ShapeDtypeStruct((2, S, F), BF16),
                   jax.ShapeDtypeStruct((S, F), BF16)],
        compiler_params=_params(("parallel",), 56),
    )(h, w, cw, cb)


def _ffn_bwd_elem(da, u, uc, cw, after, name):
    _, S, F = u.shape
    tn = _tile(F, 256)

    def body(da_ref, u_ref, uc_ref, cw_ref, after_ref, du_ref, dcw_ref, dcb_ref):
        row = lax.broadcasted_iota(jnp.int32, (S, tn), 0)
        da = da_ref[...]
        g, v = uc_ref[0].astype(F32), uc_ref[1].astype(F32)
        sg = _sigmoid(g)
        d_conv = (da * v * (sg * (1.0 + g * (1.0 - sg))), da * (g * sg))
        for p in range(2):
            d = d_conv[p]
            cwp = cw_ref[p]
            u = u_ref[p].astype(F32)
            d1, d2 = _shift_up(d, 1, row, S), _shift_up(d, 2, row, S)
            dcb_ref[p] = jnp.sum(d, axis=0, keepdims=True)
            dcw_ref[p, 0:1, :] = jnp.sum(d2 * u, axis=0, keepdims=True)
            dcw_ref[p, 1:2, :] = jnp.sum(d1 * u, axis=0, keepdims=True)
            dcw_ref[p, 2:3, :] = jnp.sum(d * u, axis=0, keepdims=True)
            du_ref[p] = (d * cwp[2:3] + d1 * cwp[1:2] + d2 * cwp[0:1]).astype(BF16)

    half = pl.BlockSpec((2, S, tn), lambda j: (0, 0, j))
    return pl.pallas_call(
        body, name=name, grid=(F // tn,),
        in_specs=[pl.BlockSpec((S, tn), lambda j: (0, j)), half, half, pl.BlockSpec((2, 3, tn), lambda j: (0, 0, j)),
                  pl.BlockSpec(memory_space=pl.ANY)],
        out_specs=[half, pl.BlockSpec((2, 3, tn), lambda j: (0, 0, j)), pl.BlockSpec((2, 1, tn), lambda j: (0, 0, j))],
        out_shape=[jax.ShapeDtypeStruct((2, S, F), BF16), jax.ShapeDtypeStruct((2, 3, F), F32),
                   jax.ShapeDtypeStruct((2, 1, F), F32)],
        compiler_params=_params(("parallel",), 56),
    )(da, u, uc, cw, after)


def _split3(x):
    hi = x.astype(BF16)
    r1 = x - hi.astype(F32)
    mid = r1.astype(BF16)
    lo = (r1 - mid.astype(F32)).astype(BF16)
    return hi, mid, lo


def _tri_matmul(x, upper, S):
    tc = _tile(S, 512)
    parts = _split3(x)
    outs = []
    for b in range(S // tc):
        r = lax.broadcasted_iota(jnp.int32, (S, tc), 0)
        c = lax.broadcasted_iota(jnp.int32, (S, tc), 1) + b * tc
        tri = jnp.where((r <= c) if upper else (r >= c), 1.0, 0.0).astype(BF16)
        acc = jnp.dot(parts[0], tri, preferred_element_type=F32)
        acc += jnp.dot(parts[1], tri, preferred_element_type=F32)
        acc += jnp.dot(parts[2], tri, preferred_element_type=F32)
        outs.append(acc)
    return outs, tc


def _fox_prep(flT, bf, name):
    H, S = flT.shape

    def body(fl_ref, b_ref, cum_ref):
        zz = fl_ref[...] + b_ref[...]
        lf = jnp.minimum(zz, 0.0) - jnp.log(1.0 + jnp.exp(-jnp.abs(zz)))
        outs, tc = _tri_matmul(lf, True, S)
        for b, o in enumerate(outs):
            cum_ref[:, b * tc:(b + 1) * tc] = o

    return pl.pallas_call(
        body, name=name,
        in_specs=[pl.BlockSpec(memory_space=pltpu.VMEM)] * 2,
        out_specs=pl.BlockSpec(memory_space=pltpu.VMEM),
        out_shape=jax.ShapeDtypeStruct((H, S), F32),
        compiler_params=pltpu.CompilerParams(vmem_limit_bytes=48 * MIB),
    )(flT, bf)


def _fox_prep_bwd(dcum_key, dcum_query, flT, bf, name):
    H, S = flT.shape

    def body(dck_ref, dcq_ref, fl_ref, b_ref, dfl_ref, dbf_ref):
        zz = fl_ref[...] + b_ref[...]
        outs, tc = _tri_matmul(dck_ref[...] + dcq_ref[...], False, S)
        total = jnp.zeros((H, 1), F32)
        for b, o in enumerate(outs):
            dfl = o * _sigmoid(-zz[:, b * tc:(b + 1) * tc])
            dfl_ref[:, b * tc:(b + 1) * tc] = dfl
            total += jnp.sum(dfl, axis=1, keepdims=True)
        dbf_ref[...] = total

    return pl.pallas_call(
        body, name=name,
        in_specs=[pl.BlockSpec(memory_space=pltpu.VMEM)] * 4,
        out_specs=[pl.BlockSpec(memory_space=pltpu.VMEM)] * 2,
        out_shape=[jax.ShapeDtypeStruct((H, S), F32), jax.ShapeDtypeStruct((H, 1), F32)],
        compiler_params=pltpu.CompilerParams(vmem_limit_bytes=48 * MIB),
    )(dcum_key, dcum_query, flT, bf)


FOX_TQ = 256
FOX_TC = 512


def _fox_scores(q, k_ref, ck_ref, i, lo, n, tq):
    k = k_ref[lo:lo + n, :].astype(BF16)
    s = lax.dot_general(q, k, (((1,), (1,)), ((), ())), preferred_element_type=F32) - ck_ref[:, lo:lo + n]
    qpos = i * tq + lax.broadcasted_iota(jnp.int32, (tq, n), 0)
    kpos = lo + lax.broadcasted_iota(jnp.int32, (tq, n), 1)
    return jnp.where(kpos <= qpos, s, -jnp.inf)


FOX_FWD_SPLITS = 4


def _fox_attn_fwd_part(proj, cum_row, H, q_lo, q_hi, filled, name):
    S = proj.shape[0]
    dh = FOX_HEAD_DIM
    tq = _tile(q_hi - q_lo, 2 * FOX_TQ)
    first = q_lo // tq
    scale = dh ** -0.5
    n_keep = 0 if filled is None else 2

    def body(q_ref, k_ref, v_ref, ck_ref, *rest):
        o_ref, lse_ref = rest[n_keep:]
        q = (q_ref[...] * scale).astype(BF16)
        s = _fox_scores(q, k_ref, ck_ref, first + pl.program_id(1), 0, q_hi, tq)
        m = jnp.max(s, axis=-1, keepdims=True)
        p = jnp.exp(s - m)
        l = jnp.sum(p, axis=-1, keepdims=True)
        o = jnp.dot(p.astype(BF16), v_ref[...].astype(BF16), preferred_element_type=F32) / l
        o_ref[...] = o.astype(BF16)
        lse_ref[...] = m + jnp.log(l)

    return pl.pallas_call(
        body, name=name, grid=(H, (q_hi - q_lo) // tq),
        in_specs=[pl.BlockSpec((tq, dh), lambda h, i: (first + i, h)),
                  pl.BlockSpec((q_hi, dh), lambda h, i: (0, H + h)),
                  pl.BlockSpec((q_hi, dh), lambda h, i: (0, 2 * H + h)),
                  pl.BlockSpec((None, 1, q_hi), lambda h, i: (h, 0, 0))] + [pl.BlockSpec(memory_space=pl.ANY)] * n_keep,
        out_specs=[pl.BlockSpec((tq, dh), lambda h, i: (first + i, h)),
                   pl.BlockSpec((None, tq, 1), lambda h, i: (h, first + i, 0))],
        out_shape=[jax.ShapeDtypeStruct((S, H * dh), BF16), jax.ShapeDtypeStruct((H, S, 1), F32)],
        input_output_aliases={4 + k: k for k in range(n_keep)},
        compiler_params=_params(("parallel", "parallel")),
    )(proj, proj, proj, cum_row, *(filled or ()))


def _fox_attn_fwd(proj, cum_row, H, name):
    S = proj.shape[0]
    n = FOX_FWD_SPLITS if S % (FOX_FWD_SPLITS * FOX_TQ) == 0 else 1
    out = None
    for part in range(n):
        out = _fox_attn_fwd_part(proj, cum_row, H, part * S // n, (part + 1) * S // n, out, name + str(part))
    return out


def _fox_attn_bwd(proj, do, o, cum_row, lse, H, after, name):
    S = proj.shape[0]
    dh = FOX_HEAD_DIM
    tq = _tile(S, 2 * FOX_TQ)
    tc = _tile(S, FOX_TC)
    scale = dh ** -0.5

    def body(q_ref, k_ref, v_ref, do_ref, o_ref, ck_ref, lse_ref, after_ref,
             dq_ref, dk_ref, dv_ref, dck_ref, dcq_ref, acc_ref):
        i = pl.program_id(1)

        @pl.when(i == 0)
        def _():
            dk_ref[...] = jnp.zeros_like(dk_ref)
            dv_ref[...] = jnp.zeros_like(dv_ref)
            dck_ref[...] = jnp.zeros_like(dck_ref)

        acc_ref[...] = jnp.zeros_like(acc_ref)
        dcq_ref[...] = jnp.zeros_like(dcq_ref)
        q = (q_ref[...] * scale).astype(BF16)
        do_f = do_ref[...]
        do_b = do_f.astype(BF16)
        delta = jnp.sum(do_f * o_ref[...].astype(F32), axis=-1, keepdims=True)
        lse_q = lse_ref[...]
        for c in range(S // tc):
            @pl.when(c * tc <= i * tq + tq - 1)
            def _():
                rows = slice(c * tc, (c + 1) * tc)
                p = jnp.exp(_fox_scores(q, k_ref, ck_ref, i, c * tc, tc, tq) - lse_q)
                dp = lax.dot_general(do_b, v_ref[rows, :].astype(BF16), (((1,), (1,)), ((), ())),
                                     preferred_element_type=F32)
                ds = p * (dp - delta)
                ds_b = ds.astype(BF16)
                acc_ref[...] += jnp.dot(ds_b, k_ref[rows, :].astype(BF16), preferred_element_type=F32)
                dk_ref[rows, :] += lax.dot_general(ds_b, q, (((0,), (0,)), ((), ())), preferred_element_type=F32)
                dv_ref[rows, :] += lax.dot_general(p.astype(BF16), do_b, (((0,), (0,)), ((), ())),
                                                   preferred_element_type=F32)
                dck_ref[:, rows] -= jnp.sum(ds, axis=0, keepdims=True)
                dcq_ref[...] += jnp.sum(ds, axis=-1, keepdims=True)
        dq_ref[...] = (acc_ref[...] * scale).astype(BF16)

    W = H * dh
    return pl.pallas_call(
        body, name=name, grid=(H, S // tq),
        in_specs=[pl.BlockSpec((tq, dh), lambda h, i: (i, h)),
                  pl.BlockSpec((S, dh), lambda h, i: (0, H + h)),
                  pl.BlockSpec((S, dh), lambda h, i: (0, 2 * H + h)),
                  pl.BlockSpec((tq, dh), lambda h, i: (i, h)),
                  pl.BlockSpec((tq, dh), lambda h, i: (i, h)),
                  pl.BlockSpec((None, 1, S), lambda h, i: (h, 0, 0)),
                  pl.BlockSpec((None, tq, 1), lambda h, i: (h, i, 0)),
                  pl.BlockSpec(memory_space=pl.ANY)],
        out_specs=[pl.BlockSpec((tq, dh), lambda h, i: (i, h)),
                   pl.BlockSpec((S, dh), lambda h, i: (0, h)),
                   pl.BlockSpec((S, dh), lambda h, i: (0, h)),
                   pl.BlockSpec((None, 1, S), lambda h, i: (h, 0, 0)),
                   pl.BlockSpec((None, tq, 1), lambda h, i: (h, i, 0))],
        out_shape=[jax.ShapeDtypeStruct((S, W), BF16), jax.ShapeDtypeStruct((S, W), F32),
                   jax.ShapeDtypeStruct((S, W), F32), jax.ShapeDtypeStruct((H, 1, S), F32),
                   jax.ShapeDtypeStruct((H, S, 1), F32)],
        scratch_shapes=[pltpu.VMEM((tq, dh), F32)],
        compiler_params=_params(("parallel", "arbitrary")),
    )(proj, proj, proj, do, o, cum_row, lse, after)


def _rope(parts, tabs, out_dtype, name):
    S = parts[0][0].shape[0]
    widths = [w for _, _, w in parts]
    total = sum(widths)
    tm = _tile(S, ROW_TILE, 8)
    flags = [r for _, r, _ in parts]

    def body(*refs):
        in_refs = refs[:len(parts)]
        cos_ref, sa_ref, sb_ref, o_ref = refs[len(parts):]
        cos, sa, sb = cos_ref[...], sa_ref[...], sb_ref[...]
        off = 0
        for ref, rot, w in zip(in_refs, flags, widths):
            for j in range(w // LANE):
                t = ref[:, j * LANE:(j + 1) * LANE]
                if rot:
                    t = t * cos + pltpu.roll(t, LANE - ROPE_DIM // 2, axis=1) * sa + pltpu.roll(t, ROPE_DIM // 2, axis=1) * sb
                o_ref[:, off + j * LANE:off + (j + 1) * LANE] = t.astype(o_ref.dtype)
            off += w

    return pl.pallas_call(
        body, name=name, grid=(S // tm,),
        in_specs=[pl.BlockSpec((tm, w), lambda i: (i, 0)) for w in widths] + [_row_spec(tm, LANE)] * 3,
        out_specs=_row_spec(tm, total),
        out_shape=jax.ShapeDtypeStruct((S, total), out_dtype),
        compiler_params=_params(("parallel",)),
    )(*[a for a, _, _ in parts], *tabs)


def _swa_band(ref_p, ref_c, hk):
    dh = SWA_HEAD_DIM
    return jnp.concatenate([ref_p[:, hk * dh:(hk + 1) * dh], ref_c[:, hk * dh:(hk + 1) * dh]], axis=0).astype(BF16)


def _swa_bias(G):
    qi = jnp.arange(G * Q_BLOCK)[:, None] % Q_BLOCK
    kj = jnp.arange(2 * Q_BLOCK)[None, :]
    rel = qi + Q_BLOCK - kj
    window = (rel >= 0) & (rel < SWA_WINDOW)
    both = jnp.stack([window & (kj >= Q_BLOCK), window])
    return jnp.where(both, 0.0, -jnp.inf).astype(F32)


def _swa_stack(ref, hk, G):
    dh = SWA_HEAD_DIM
    return jnp.concatenate([ref[:, (hk * G + g) * dh:(hk * G + g + 1) * dh] for g in range(G)], axis=0)


def _swa_unstack(ref, stacked, hk, G):
    dh, QB = SWA_HEAD_DIM, Q_BLOCK
    for g in range(0, G, 2):
        c0 = (hk * G + g) * dh
        pair = jnp.concatenate([stacked[g * QB:(g + 1) * QB], stacked[(g + 1) * QB:(g + 2) * QB]], axis=1)
        ref[:, c0:c0 + 2 * dh] = pair.astype(ref.dtype)


def _swa_sink_rows(sink_ref, hk, G):
    return jnp.concatenate([jnp.broadcast_to(sink_ref[0:1, hk * G + g:hk * G + g + 1], (Q_BLOCK, 1)) for g in range(G)],
                           axis=0)


def _swa_attn_fwd(qk, proj, sinks, Hq, name):
    S = qk.shape[0]
    dh, G, QB = SWA_HEAD_DIM, SWA_GROUP, Q_BLOCK
    Hk = Hq // G
    Wq, Wk = Hq * dh, Hk * dh
    nb = S // QB
    scale = dh ** -0.5

    def body(q_ref, kp_ref, kc_ref, vp_ref, vc_ref, sink_ref, bias_ref, o_ref, lse_ref):
        n = pl.program_id(0)
        lane = lax.broadcasted_iota(jnp.int32, (QB, LANE), 1)
        first, second = pl.ds(0, QB), pl.ds(QB, QB)
        for sub, rows in enumerate((first, second)):
            bias = bias_ref[jnp.minimum(n, 1)] if sub == 0 else bias_ref[1]
            k_band = (kp_ref, kc_ref.at[first]) if sub == 0 else (kc_ref.at[first], kc_ref.at[second])
            v_band = (vp_ref, vc_ref.at[first]) if sub == 0 else (vc_ref.at[first], vc_ref.at[second])
            lse_tile = jnp.zeros((QB, LANE), F32)
            for hk in range(Hk):
                kb = _swa_band(*k_band, hk)
                vb = _swa_band(*v_band, hk)
                q = (_swa_stack(q_ref.at[rows], hk, G) * scale).astype(BF16)
                sk = _swa_sink_rows(sink_ref, hk, G)
                s = lax.dot_general(q, kb, (((1,), (1,)), ((), ())), preferred_element_type=F32) + bias
                m = jnp.maximum(jnp.max(s, axis=-1, keepdims=True), sk)
                p = jnp.exp(s - m)
                l = jnp.sum(p, axis=-1, keepdims=True) + jnp.exp(sk - m)
                o = jnp.dot(p.astype(BF16), vb, preferred_element_type=F32) / l
                lse = m + jnp.log(l)
                for g in range(G):
                    lse_tile = jnp.where(lane == hk * G + g, lse[g * QB:(g + 1) * QB], lse_tile)
                _swa_unstack(o_ref.at[rows], o, hk, G)
            lse_ref[rows, :] = lse_tile

    kcol, vcol = Wq // Wk, (Wq + Wk) // Wk
    before = lambda n: jnp.maximum(2 * n - 1, 0)
    return pl.pallas_call(
        body, name=name, grid=(nb // 2,),
        in_specs=[pl.BlockSpec((2 * QB, Wq), lambda n: (n, 0)),
                  pl.BlockSpec((QB, Wk), lambda n: (before(n), kcol)),
                  pl.BlockSpec((2 * QB, Wk), lambda n: (n, kcol)),
                  pl.BlockSpec((QB, Wk), lambda n: (before(n), vcol)),
                  pl.BlockSpec((2 * QB, Wk), lambda n: (n, vcol)),
                  pl.BlockSpec((1, LANE), lambda n: (0, 0)),
                  pl.BlockSpec((2, G * QB, 2 * QB), lambda n: (0, 0, 0))],
        out_specs=[pl.BlockSpec((2 * QB, Wq), lambda n: (n, 0)), pl.BlockSpec((2 * QB, LANE), lambda n: (n, 0))],
        out_shape=[jax.ShapeDtypeStruct((S, Wq), BF16), jax.ShapeDtypeStruct((S, LANE), F32)],
        compiler_params=_params(("parallel",)),
    )(qk, qk, qk, proj, proj, sinks, _swa_bias(G))


def _swa_attn_bwd(qk, proj, sinks, do, lse, Hq, after, name):
    S = qk.shape[0]
    dh, G, QB = SWA_HEAD_DIM, SWA_GROUP, Q_BLOCK
    Hk = Hq // G
    Wq, Wk = Hq * dh, Hk * dh
    nb = S // QB
    scale = dh ** -0.5

    def body(q_ref, kp_ref, kc_ref, vp_ref, vc_ref, sink_ref, do_ref, lse_ref, bias_ref, after_ref,
             dq_ref, dk_ref, dv_ref, dsink_ref, carry_k, carry_v):
        n = pl.program_id(0)

        @pl.when(n == 0)
        def _():
            dsink_ref[...] = jnp.zeros_like(dsink_ref)

        @pl.when(n < nb)
        def _():
            bias = bias_ref[...]
            lane = lax.broadcasted_iota(jnp.int32, (1, LANE), 1)
            dsink = jnp.zeros((1, LANE), F32)
            dk_heads, dv_heads = [], []
            for hk in range(Hk):
                kb = _swa_band(kp_ref, kc_ref, hk)
                vb = _swa_band(vp_ref, vc_ref, hk)
                q = (_swa_stack(q_ref, hk, G) * scale).astype(BF16)
                do_s = _swa_stack(do_ref, hk, G).astype(BF16)
                lse = jnp.concatenate([lse_ref[:, hk * G + g:hk * G + g + 1] for g in range(G)], axis=0)
                s = lax.dot_general(q, kb, (((1,), (1,)), ((), ())), preferred_element_type=F32) + bias
                p = jnp.exp(s - lse)
                p_sink = jnp.exp(_swa_sink_rows(sink_ref, hk, G) - lse)
                dp = lax.dot_general(do_s, vb, (((1,), (1,)), ((), ())), preferred_element_type=F32)
                delta = jnp.sum(p * dp, axis=-1, keepdims=True)
                ds_b = (p * (dp - delta)).astype(BF16)
                _swa_unstack(dq_ref, jnp.dot(ds_b, kb, preferred_element_type=F32) * scale, hk, G)
                dk_heads.append(lax.dot_general(ds_b, q, (((0,), (0,)), ((), ())), preferred_element_type=F32))
                dv_heads.append(lax.dot_general(p.astype(BF16), do_s, (((0,), (0,)), ((), ())), preferred_element_type=F32))
                sink_term = p_sink * delta
                for g in range(G):
                    dsink = jnp.where(lane == hk * G + g,
                                      -jnp.sum(sink_term[g * QB:(g + 1) * QB], axis=0, keepdims=True), dsink)
            dsink_ref[...] += dsink
            dk_all = jnp.concatenate(dk_heads, axis=1)
            dv_all = jnp.concatenate(dv_heads, axis=1)

            @pl.when(n > 0)
            def _():
                dk_ref[...] = carry_k[...] + dk_all[:QB]
                dv_ref[...] = carry_v[...] + dv_all[:QB]

            carry_k[...] = dk_all[QB:]
            carry_v[...] = dv_all[QB:]

        @pl.when(n == nb)
        def _():
            dk_ref[...] = carry_k[...]
            dv_ref[...] = carry_v[...]

    kcol, vcol = Wq // Wk, (Wq + Wk) // Wk
    cur = lambda n: jnp.minimum(n, nb - 1)
    prev = lambda n: jnp.maximum(jnp.minimum(n, nb - 1) - 1, 0)
    return pl.pallas_call(
        body, name=name, grid=(nb + 1,),
        in_specs=[pl.BlockSpec((QB, Wq), lambda n: (cur(n), 0)),
                  pl.BlockSpec((QB, Wk), lambda n: (prev(n), kcol)),
                  pl.BlockSpec((QB, Wk), lambda n: (cur(n), kcol)),
                  pl.BlockSpec((QB, Wk), lambda n: (prev(n), vcol)),
                  pl.BlockSpec((QB, Wk), lambda n: (cur(n), vcol)),
                  pl.BlockSpec((1, LANE), lambda n: (0, 0)),
                  pl.BlockSpec((QB, Wq), lambda n: (cur(n), 0)),
                  pl.BlockSpec((QB, LANE), lambda n: (cur(n), 0)),
                  pl.BlockSpec((None, G * QB, 2 * QB), lambda n: (jnp.minimum(n, 1), 0, 0)),
                  pl.BlockSpec(memory_space=pl.ANY)],
        out_specs=[pl.BlockSpec((QB, Wq), lambda n: (cur(n), 0)),
                   pl.BlockSpec((QB, Wk), lambda n: (jnp.maximum(n - 1, 0), 0)),
                   pl.BlockSpec((QB, Wk), lambda n: (jnp.maximum(n - 1, 0), 0)),
                   pl.BlockSpec((1, LANE), lambda n: (0, 0))],
        out_shape=[jax.ShapeDtypeStruct((S, Wq), F32), jax.ShapeDtypeStruct((S, Wk), F32),
                   jax.ShapeDtypeStruct((S, Wk), F32), jax.ShapeDtypeStruct((1, LANE), F32)],
        scratch_shapes=[pltpu.VMEM((QB, Wk), F32), pltpu.VMEM((QB, Wk), F32)],
        compiler_params=_params(("arbitrary",)),
    )(qk, qk, qk, proj, proj, sinks, do, lse, _swa_bias(G), after)


ADA_ROWS = 16


def _ada_mod(c_pad, w, b, name):
    L, D, N = w.shape
    tn = _tile(N, 512)

    def body(c_ref, w_ref, b_ref, o_ref):
        c = c_ref[...]
        c = c * _sigmoid(c)
        ch = c.astype(BF16)
        cl = (c - ch.astype(F32)).astype(BF16)
        ww = w_ref[...]
        wh = ww.astype(BF16)
        wl = (ww - wh.astype(F32)).astype(BF16)
        acc = jnp.dot(ch, wh, preferred_element_type=F32)
        acc += jnp.dot(ch, wl, preferred_element_type=F32)
        acc += jnp.dot(cl, wh, preferred_element_type=F32)
        o_ref[...] = acc + b_ref[...]

    return pl.pallas_call(
        body, name=name, grid=(L, N // tn),
        in_specs=[pl.BlockSpec((ADA_ROWS, D), lambda l, j: (0, 0)),
                  pl.BlockSpec((None, D, tn), lambda l, j: (l, 0, j)),
                  pl.BlockSpec((None, 1, tn), lambda l, j: (l, 0, j))],
        out_specs=pl.BlockSpec((None, ADA_ROWS, tn), lambda l, j: (l, 0, j)),
        out_shape=jax.ShapeDtypeStruct((L, ADA_ROWS, N), F32),
        compiler_params=_params(("parallel", "parallel")),
    )(c_pad, w, b)


def _ada_bwd_adam(cT, dm, w, m, v, name):
    D = cT.shape[0]
    L, B, N = dm.shape
    tm = _tile(D, 256)

    def body(c_ref, dm_ref, w_ref, m_ref, v_ref, g_ref, d_ref, mo_ref, vo_ref):
        c = c_ref[...]
        c = c * _sigmoid(c)
        dmv = dm_ref[...]
        g = c[:, 0:1] * dmv[0:1, :]
        for b in range(1, B):
            g += c[:, b:b + 1] * dmv[b:b + 1, :]
        delta, mn, vn = _adamw_math(w_ref[...], g, m_ref[...], v_ref[...])
        g_ref[...] = g
        d_ref[...] = delta
        mo_ref[...] = mn
        vo_ref[...] = vn

    spec = pl.BlockSpec((None, tm, N), lambda l, i: (l, i, 0))
    return pl.pallas_call(
        body, name=name, grid=(L, D // tm),
        in_specs=[pl.BlockSpec((tm, B), lambda l, i: (i, 0)), pl.BlockSpec((None, B, N), lambda l, i: (l, 0, 0)),
                  spec, spec, spec],
        out_specs=[spec] * 4,
        out_shape=[jax.ShapeDtypeStruct((L, D, N), F32)] * 4,
        compiler_params=_params(("parallel", "parallel")),
    )(cT, dm, w, m, v)


def _adamw_math(w, g, m, v):
    m = ADAM_B1 * m + (1.0 - ADAM_B1) * g
    v = ADAM_B2 * v + (1.0 - ADAM_B2) * (g * g)
    m_hat = m / (1.0 - ADAM_B1 ** ADAM_STEP)
    v_hat = v / (1.0 - ADAM_B2 ** ADAM_STEP)
    delta = -ADAM_LR * (m_hat / (jnp.sqrt(v_hat) + ADAM_EPS) + ADAM_WD * w)
    return delta, m, v


def _adam_rows(R, C):
    lanes = -(-C // LANE) * LANE
    return _tile(R, max(8, (262144 // lanes) // 8 * 8), 8)


def _adam_sum(recv, w, m, v, layer, filled, name, by_cols=False):
    P, R, C = recv.shape
    L = w.shape[0]
    n_keep = 0 if filled is None else 4
    if by_cols:
        tc = _tile(C, 256)
        grid, spec = (C // tc,), pl.BlockSpec((None, R, tc), lambda i: (layer, 0, i))
        recv_spec = pl.BlockSpec((P, R, tc), lambda i: (0, 0, i))
    else:
        tr = _adam_rows(R, C)
        grid, spec = (R // tr,), pl.BlockSpec((None, tr, C), lambda i: (layer, i, 0))
        recv_spec = pl.BlockSpec((P, tr, C), lambda i: (0, i, 0))

    def body(r_ref, w_ref, m_ref, v_ref, *rest):
        g_ref, d_ref, mo_ref, vo_ref = rest[n_keep:]
        g = r_ref[0].astype(F32)
        for p in range(1, P):
            g = g + r_ref[p].astype(F32)
        delta, mn, vn = _adamw_math(w_ref[...], g, m_ref[...], v_ref[...])
        g_ref[...] = g
        d_ref[...] = delta
        mo_ref[...] = mn
        vo_ref[...] = vn

    return pl.pallas_call(
        body, name=name, grid=grid,
        in_specs=[recv_spec, spec, spec, spec] + [pl.BlockSpec(memory_space=pl.ANY)] * n_keep,
        out_specs=[spec] * 4,
        out_shape=[jax.ShapeDtypeStruct((L, R, C), F32)] * 4,
        input_output_aliases={4 + k: k for k in range(n_keep)},
        compiler_params=_params(("parallel",)),
    )(recv, w, m, v, *(filled or ()))


def _adam(g, w, m, v, name):
    L, R, C = w.shape
    tr = _adam_rows(R, C)

    def body(g_ref, w_ref, m_ref, v_ref, d_ref, mo_ref, vo_ref):
        delta, mn, vn = _adamw_math(w_ref[...], g_ref[...], m_ref[...], v_ref[...])
        d_ref[...] = delta
        mo_ref[...] = mn
        vo_ref[...] = vn

    spec = pl.BlockSpec((None, tr, C), lambda l, i: (l, i, 0))
    return pl.pallas_call(
        body, name=name, grid=(L, R // tr),
        in_specs=[spec] * 4, out_specs=[spec] * 3,
        out_shape=[jax.ShapeDtypeStruct((L, R, C), F32)] * 3,
        compiler_params=_params(("parallel", "parallel")),
    )(g, w, m, v)


def _sum_slots(x, after, name):
    P, R, C = x.shape

    def body(x_ref, after_ref, o_ref):
        acc = x_ref[0]
        for p in range(1, P):
            acc = acc + x_ref[p]
        o_ref[...] = acc

    return pl.pallas_call(
        body, name=name,
        in_specs=[pl.BlockSpec(memory_space=pltpu.VMEM), pl.BlockSpec(memory_space=pl.ANY)],
        out_specs=pl.BlockSpec(memory_space=pltpu.VMEM),
        out_shape=jax.ShapeDtypeStruct((R, C), F32),
        compiler_params=pltpu.CompilerParams(vmem_limit_bytes=48 * MIB),
    )(x, after)


def _my_pos():
    return lax.axis_index("x"), lax.axis_index("y"), lax.axis_index("c")


def _all_gather_small(x, name, after=()):
    R, C = x.shape
    n_after = len(after)

    def body(x_ref, *rest):
        out_ref, send_sems, recv_sems = rest[n_after:]
        x_, y_, c_ = _my_pos()
        me, sibling = (x_, y_, c_), (x_, y_, 1 - c_)
        chips = [(1 - x_, y_), (x_, 1 - y_), (1 - x_, 1 - y_)]

        def slot(px, py, pc):
            return out_ref.at[4 * px + 2 * py + pc]

        def copy(k, block, to):
            return pltpu.make_async_remote_copy(
                src_ref=slot(*block), dst_ref=slot(*block), send_sem=send_sems.at[k], recv_sem=recv_sems.at[k],
                device_id=to, device_id_type=MESH)

        out_ref[4 * x_ + 2 * y_ + c_] = x_ref[...]
        first = [copy(0, me, sibling)] + [copy(1 + j, me, (*chip, c_)) for j, chip in enumerate(chips)]
        for cp in first:
            cp.start()
        passed = [copy(4 + j, (*chip, c_), sibling) for j, chip in enumerate(chips)]
        for j, chip in enumerate(chips):
            copy(1 + j, (*chip, c_), me).wait_recv()
            passed[j].start()
        copy(0, sibling, me).wait_recv()
        for j, chip in enumerate(chips):
            copy(4 + j, (*chip, 1 - c_), me).wait_recv()
        for cp in first + passed:
            cp.wait_send()

    return pl.pallas_call(
        body, name=name,
        in_specs=[pl.BlockSpec(memory_space=pltpu.VMEM)] + [pl.BlockSpec(memory_space=pl.ANY)] * n_after,
        out_specs=pl.BlockSpec(memory_space=pltpu.VMEM),
        out_shape=jax.ShapeDtypeStruct((N_DEV, R, C), x.dtype),
        scratch_shapes=[pltpu.SemaphoreType.DMA((7,)), pltpu.SemaphoreType.DMA((7,))],
        compiler_params=pltpu.CompilerParams(vmem_limit_bytes=48 * MIB),
    )(x, *after)


HBM_SPEC = pl.BlockSpec(memory_space=pltpu.HBM)
SEM_SPEC = pl.BlockSpec(memory_space=pltpu.SEMAPHORE)
ANY_SPEC = pl.BlockSpec(memory_space=pl.ANY)
SPLIT_EFFECT = pltpu.SideEffectType.DATAFLOW_SIDE_EFFECTING


def _in_hbm(a):
    return pltpu.with_memory_space_constraint(a, pltpu.HBM)


def _gathered_shape(a, kind):
    if kind == "major":
        return (N_DEV,) + a.shape
    if kind == "rows":
        return (N_DEV * a.shape[0], a.shape[1])
    return (2, a.shape[0], 4 * a.shape[1])


def _gather_slot(ref, kind, block, shard_shape):
    px, py, pc = block
    if kind == "major":
        return ref.at[4 * px + 2 * py + pc]
    if kind == "rows":
        r = shard_shape[0]
        return ref.at[pl.ds(pl.multiple_of((4 * px + 2 * py + pc) * r, r), r), :]
    cu = shard_shape[1]
    return ref.at[px, :, pl.ds(pl.multiple_of((2 * py + pc) * cu, cu), cu)]


def _gather_peers():
    x_, y_, c_ = _my_pos()
    return (x_, y_, c_), (x_, y_, 1 - c_), [(1 - x_, y_), (x_, 1 - y_), (1 - x_, 1 - y_)]


def _gather_start(shards, kinds, after, name):
    n = len(shards)
    bufs = [lax.empty(_gathered_shape(a, k), a.dtype) for a, k in zip(shards, kinds)]
    extra = [] if after is None else [after]

    def body(*refs):
        shard_refs, buf_refs = refs[:n], refs[n:2 * n]
        send_sems, recv_sems, local_sems = refs[2 * n + len(extra):2 * n + len(extra) + 3]
        token = refs[-1]
        me, sibling, chips = _gather_peers()
        for e in range(n):
            mine = _gather_slot(buf_refs[e], kinds[e], me, shards[e].shape)
            pltpu.make_async_copy(shard_refs[e], mine, local_sems.at[e]).start()
            for k, to in enumerate([sibling] + [(*chip, me[2]) for chip in chips]):
                pltpu.make_async_remote_copy(
                    src_ref=shard_refs[e], dst_ref=mine, send_sem=send_sems.at[4 * e + k],
                    recv_sem=recv_sems.at[4 * e + k], device_id=to, device_id_type=MESH).start()
        token[...] = jnp.zeros_like(token)

    out = pl.pallas_call(
        body, name=name,
        out_shape=(pltpu.SemaphoreType.DMA((4 * n,)), pltpu.SemaphoreType.DMA((4 * n,)), pltpu.SemaphoreType.DMA((n,)),
                   *[pltpu.HBM(a.shape, a.dtype) for a in shards], *[pltpu.HBM(a.shape, a.dtype) for a in bufs],
                   jax.ShapeDtypeStruct((8, LANE), F32)),
        in_specs=[HBM_SPEC] * (2 * n) + [ANY_SPEC] * len(extra),
        out_specs=(SEM_SPEC, SEM_SPEC, SEM_SPEC, *[HBM_SPEC] * (2 * n), pl.BlockSpec(memory_space=pltpu.VMEM)),
        input_output_aliases={i: 3 + i for i in range(2 * n)},
        compiler_params=pltpu.CompilerParams(has_side_effects=SPLIT_EFFECT),
    )(*[_in_hbm(a) for a in shards], *[_in_hbm(a) for a in bufs], *extra)
    return out[0], out[1], out[2], out[3:3 + n], out[3 + n:3 + 2 * n], out[-1]


def _gather_forward(recv_sems, bufs, kinds, shard_shapes, after, name):
    n = len(bufs)

    def body(*refs):
        buf_refs, recv_in = refs[:n], refs[n]
        fsend, frecv = refs[n + 2], refs[n + 3]
        token = refs[-1]
        me, sibling, chips = _gather_peers()
        for e in range(n):
            for j, chip in enumerate(chips):
                slot = _gather_slot(buf_refs[e], kinds[e], (*chip, me[2]), shard_shapes[e])
                pltpu.make_async_remote_copy(
                    src_ref=slot, dst_ref=slot, send_sem=recv_in.at[4 * e + 1 + j], recv_sem=recv_in.at[4 * e + 1 + j],
                    device_id=me, device_id_type=MESH).wait_recv()
                pltpu.make_async_remote_copy(
                    src_ref=slot, dst_ref=slot, send_sem=fsend.at[3 * e + j], recv_sem=frecv.at[3 * e + j],
                    device_id=sibling, device_id_type=MESH).start()
        token[...] = jnp.zeros_like(token)

    out = pl.pallas_call(
        body, name=name,
        out_shape=(pltpu.SemaphoreType.DMA((3 * n,)), pltpu.SemaphoreType.DMA((3 * n,)),
                   *[pltpu.HBM(a.shape, a.dtype) for a in bufs], jax.ShapeDtypeStruct((8, LANE), F32)),
        in_specs=[HBM_SPEC] * n + [SEM_SPEC, ANY_SPEC],
        out_specs=(SEM_SPEC, SEM_SPEC, *[HBM_SPEC] * n, pl.BlockSpec(memory_space=pltpu.VMEM)),
        input_output_aliases={i: 2 + i for i in range(n)},
        compiler_params=pltpu.CompilerParams(has_side_effects=SPLIT_EFFECT),
    )(*bufs, recv_sems, after)
    return out[0], out[1], out[2:2 + n], out[-1]


def _gather_wait(send_sems, recv_sems, local_sems, fsend, frecv, shards, bufs, kinds, after, name):
    n = len(bufs)

    def body(*refs):
        shard_refs, buf_refs = refs[:n], refs[n:2 * n]
        send_in, recv_in, local_in, fsend_in, frecv_in = refs[2 * n:2 * n + 5]
        me, sibling, chips = _gather_peers()

        def arrival(slot, sem):
            return pltpu.make_async_remote_copy(src_ref=slot, dst_ref=slot, send_sem=sem, recv_sem=sem,
                                                device_id=me, device_id_type=MESH)

        for e in range(n):
            shape = shards[e].shape
            mine = _gather_slot(buf_refs[e], kinds[e], me, shape)
            pltpu.make_async_copy(shard_refs[e], mine, local_in.at[e]).wait()
            arrival(_gather_slot(buf_refs[e], kinds[e], sibling, shape), recv_in.at[4 * e]).wait_recv()
            for j, chip in enumerate(chips):
                arrival(_gather_slot(buf_refs[e], kinds[e], (*chip, 1 - me[2]), shape), frecv_in.at[3 * e + j]).wait_recv()
            for k in range(4):
                arrival(mine, send_in.at[4 * e + k]).wait_send()
            for j in range(3):
                arrival(mine, fsend_in.at[3 * e + j]).wait_send()

    out = pl.pallas_call(
        body, name=name,
        out_shape=(*[pltpu.HBM(a.shape, a.dtype) for a in shards], *[pltpu.HBM(a.shape, a.dtype) for a in bufs]),
        in_specs=[HBM_SPEC] * (2 * n) + [SEM_SPEC] * 5 + [ANY_SPEC],
        out_specs=tuple([HBM_SPEC] * (2 * n)),
        input_output_aliases={i: i for i in range(2 * n)},
        compiler_params=pltpu.CompilerParams(has_side_effects=SPLIT_EFFECT),
    )(*shards, *bufs, send_sems, recv_sems, local_sems, fsend, frecv, after)
    return out[n:]


def _grad_slice(ref, kind, j):
    if kind == "whole":
        return ref
    if kind == "major":
        return ref.at[j]
    if kind == "rows":
        r = ref.shape[0] // N_DEV
        return ref.at[pl.ds(j * r, r), :]
    cu = ref.shape[2] // 4
    return ref.at[j // 4, :, pl.ds((j % 4) * cu, cu)]


def _slice_shape(a, kind):
    if kind == "whole":
        return a.shape
    if kind == "major":
        return a.shape[1:]
    if kind == "rows":
        return (a.shape[0] // N_DEV, a.shape[1])
    return (a.shape[1], a.shape[2] // 4)


def _scatter_copies(srcs, lands, kinds, send_sems, recv_sems):
    x_, y_, c_ = _my_pos()
    me = 4 * x_ + 2 * y_ + c_
    n = len(srcs)

    def remote(e, j):
        return pltpu.make_async_remote_copy(
            src_ref=_grad_slice(srcs[e], kinds[e], j), dst_ref=lands[e].at[me],
            send_sem=send_sems.at[e * N_DEV + j], recv_sem=recv_sems.at[e * N_DEV + me],
            device_id=(j // 4, (j // 2) % 2, j % 2), device_id_type=MESH)

    def local(e, j):
        return pltpu.make_async_copy(_grad_slice(srcs[e], kinds[e], j), lands[e].at[j], recv_sems.at[e * N_DEV + j])

    def arrival(e, i):
        return pltpu.make_async_remote_copy(
            src_ref=_grad_slice(srcs[e], kinds[e], i), dst_ref=lands[e].at[i],
            send_sem=send_sems.at[e * N_DEV + i], recv_sem=recv_sems.at[e * N_DEV + i],
            device_id=(i // 4, (i // 2) % 2, i % 2), device_id_type=MESH)

    def start():
        for e in range(n):
            for j in range(N_DEV):
                @pl.when(me == j)
                def _():
                    local(e, j).start()

                @pl.when(me != j)
                def _():
                    remote(e, j).start()

    def wait():
        for e in range(n):
            for i in range(N_DEV):
                @pl.when(me == i)
                def _():
                    local(e, i).wait()

                @pl.when(me != i)
                def _():
                    arrival(e, i).wait_recv()
        for e in range(n):
            for j in range(N_DEV):
                @pl.when(me != j)
                def _():
                    remote(e, j).wait_send()

    return start, wait


def _scatter_start(srcs, kinds, after, name):
    n = len(srcs)
    lands = [lax.empty((N_DEV,) + _slice_shape(a, k), a.dtype) for a, k in zip(srcs, kinds)]
    extra = [] if after is None else [after]

    def body(*refs):
        src_refs, land_refs = refs[:n], refs[n:2 * n]
        send_sems, recv_sems = refs[2 * n + len(extra)], refs[2 * n + len(extra) + 1]
        token = refs[-1]
        start, _ = _scatter_copies(src_refs, land_refs, kinds, send_sems, recv_sems)
        start()
        token[...] = jnp.zeros_like(token)

    out = pl.pallas_call(
        body, name=name,
        out_shape=(pltpu.SemaphoreType.DMA((n * N_DEV,)), pltpu.SemaphoreType.DMA((n * N_DEV,)),
                   *[pltpu.HBM(a.shape, a.dtype) for a in srcs], *[pltpu.HBM(a.shape, a.dtype) for a in lands],
                   jax.ShapeDtypeStruct((8, LANE), F32)),
        in_specs=[HBM_SPEC] * (2 * n) + [ANY_SPEC] * len(extra),
        out_specs=(SEM_SPEC, SEM_SPEC, *[HBM_SPEC] * (2 * n), pl.BlockSpec(memory_space=pltpu.VMEM)),
        input_output_aliases={i: 2 + i for i in range(2 * n)},
        compiler_params=pltpu.CompilerParams(has_side_effects=SPLIT_EFFECT),
    )(*[_in_hbm(a) for a in srcs], *[_in_hbm(a) for a in lands], *extra)
    return out[0], out[1], out[2:2 + n], out[2 + n:2 + 2 * n], out[-1]


def _scatter_wait(send_sems, recv_sems, srcs, lands, kinds, after, name):
    n = len(srcs)

    def body(*refs):
        src_refs, land_refs = refs[:n], refs[n:2 * n]
        _, wait = _scatter_copies(src_refs, land_refs, kinds, refs[2 * n], refs[2 * n + 1])
        wait()

    out = pl.pallas_call(
        body, name=name,
        out_shape=(*[pltpu.HBM(a.shape, a.dtype) for a in srcs], *[pltpu.HBM(a.shape, a.dtype) for a in lands]),
        in_specs=[HBM_SPEC] * (2 * n) + [SEM_SPEC, SEM_SPEC] + [ANY_SPEC] * len(after),
        out_specs=tuple([HBM_SPEC] * (2 * n)),
        input_output_aliases={i: i for i in range(2 * n)},
        compiler_params=pltpu.CompilerParams(has_side_effects=SPLIT_EFFECT),
    )(*srcs, *lands, send_sems, recv_sems, *after)
    return out[n:]


def _rope_tables(positions, sign):
    half = ROPE_DIM // 2
    inv_freq = ROPE_THETA ** (-jnp.arange(0, ROPE_DIM, 2, dtype=F32) / ROPE_DIM)
    ang = positions.astype(F32)[:, None] * inv_freq
    reps = LANE // half
    cos = jnp.tile(jnp.cos(ang), (1, reps))
    sin = jnp.tile(jnp.sin(ang), (1, reps)) * sign
    d = jnp.arange(LANE) % SWA_HEAD_DIM
    return (jnp.where(d < ROPE_DIM, cos, 1.0), jnp.where(d < half, -sin, 0.0),
            jnp.where((d >= half) & (d < ROPE_DIM), sin, 0.0))


def _pad_cols(a, n):
    return jnp.pad(a, ((0, 0), (0, n - a.shape[1])))


def _local_step(x, target, positions, mods, fetch, P, on_grads, on_small):
    S, D = x.shape
    Hf = D // FOX_HEAD_DIM
    Hq = D // SWA_HEAD_DIM
    Hk = Hq // SWA_GROUP
    Wk = Hk * SWA_HEAD_DIM
    n_in = 3 * D + Hf
    (sh1a, sc1a, g1a, sh2a, sc2a, g2a), (sh1b, sc1b, g1b, sh2b, sc2b, g2b) = mods
    row = lambda v: v.reshape(1, -1)
    cw = [jnp.transpose(P["conv_w"][l].reshape(3, 2, -1), (1, 0, 2)) for l in range(2)]
    cb = [P["conv_b"][l].reshape(2, 1, -1) for l in range(2)]

    W = dict(up=[None, None], down=[None, None])
    h1a = _modulate(x, sc1a, sh1a, "modulate_in")
    W["fox_in"], W["fox_o"] = fetch("fox", "wait", h1a)
    proj_a = _mm_nn(h1a, W["fox_in"], F32, "fox_in_proj", tn=896)
    flT = proj_a[:, 3 * D:n_in].T
    bf_col = P["fox_b_f"].reshape(Hf, 1)
    cumT = _fox_prep(flT, bf_col, "fox_cumsum")
    cum_row = cumT.reshape(Hf, 1, S)
    o_a, lse_a = _fox_attn_fwd(proj_a, cum_row, Hf, "fox_attn_fwd")
    token = fetch("ffn0", "forward", o_a)
    y1a = _mm_nn(o_a, W["fox_o"], F32, "fox_out_proj")
    z1a, x1, h2a = _ln_fwd(x, y1a, g1a, row(P["ln_mix_g"][0]), row(P["ln_mix_b"][0]), sc2a, sh2a, token, "ln_mix0")
    W["up"][0], W["down"][0] = fetch("ffn0", "wait", h2a)
    u_a, uc_a, a_a = _ffn_up(h2a, W["up"][0], cw[0], cb[0], "ffn_up0")
    token = fetch("swa", "forward", a_a)
    y2a = _mm_nn(a_a, W["down"][0], F32, "ffn_down0", tk=2816)
    z2a, x2, h1b = _ln_fwd(x1, y2a, g2a, row(P["ln_ffn_g"][0]), row(P["ln_ffn_b"][0]), sc1b, sh1b, token, "ln_ffn0")

    W["swa_in"], W["swa_o"] = fetch("swa", "wait", h1b)
    proj_b = _mm_nn(h1b, W["swa_in"], F32, "swa_in_proj")
    tabs_f = _rope_tables(positions, 1.0)
    tabs_b = _rope_tables(positions, -1.0)
    qk = _rope([(proj_b, True, D + Wk)], tabs_f, F32, "rope_fwd")
    sinks = _pad_cols(P["swa_sinks"].reshape(1, Hq), LANE)
    o_b, lse_b = _swa_attn_fwd(qk, proj_b, sinks, Hq, "swa_attn_fwd")
    token = fetch("ffn1", "forward", o_b)
    y1b = _mm_nn(o_b, W["swa_o"], F32, "swa_out_proj")
    z1b, x3, h2b = _ln_fwd(x2, y1b, g1b, row(P["ln_mix_g"][1]), row(P["ln_mix_b"][1]), sc2b, sh2b, token, "ln_mix1")
    W["up"][1], W["down"][1] = fetch("ffn1", "wait", h2b)
    u_b, uc_b, a_b = _ffn_up(h2b, W["up"][1], cw[1], cb[1], "ffn_up1")
    y2b = _mm_nn(a_b, W["down"][1], F32, "ffn_down1", tk=2816)
    z2b, dout, loss_row = _ln_fwd_loss(x3, y2b, g2b, row(P["ln_ffn_g"][1]), row(P["ln_ffn_b"][1]), target, "ln_ffn1_loss")

    def ffn_backward(dy, a, u, uc, h_in, l, tag):
        d_down = _mm_tn(a, dy[None], BF16, "ffn_dwdown" + tag, tm=1408, tn=1024)[0]
        token = on_grads("ffn_w_down" + tag, d_down)
        da = _mm_nt(dy[None], W["down"][l][None], F32, "ffn_da" + tag)
        du, dcw, dcb = _ffn_bwd_elem(da, u, uc, cw[l], token, "ffn_bwd_elem" + tag)
        d_up = _mm_tn(h_in, du, BF16, "ffn_dwup" + tag)
        token = on_grads("ffn_w_up" + tag, d_up)
        dh = _mm_nt(du, W["up"][l], F32, "ffn_dh" + tag, tn=1024, tk=1408)
        return dh, token, jnp.transpose(dcw, (1, 0, 2)).reshape(3, -1), dcb.reshape(-1)

    dz2b, dy2b, dg_f1, db_f1, dgate2b = _ln_bwd(dout, z2b, y2b, g2b, row(P["ln_ffn_g"][1]), "ln_ffn1_bwd")
    dh2b, token, dcw1, dcb1 = ffn_backward(dy2b, a_b, u_b, uc_b, h2b, 1, "1")
    dx3, dsc2b, dsh2b = _mod_bwd(dz2b, dh2b, x3, sc2b, token, "mod_ffn1_bwd")

    dz1b, dy1b, dg_m1, db_m1, dgate1b = _ln_bwd(dx3, z1b, y1b, g1b, row(P["ln_mix_g"][1]), "ln_mix1_bwd")
    token = on_grads("swa_w_o", _mm_tn(o_b, dy1b[None], BF16, "swa_dwo")[0])
    do_b = _mm_nt(dy1b[None], W["swa_o"][None], F32, "swa_do")
    dq_b, dk_b, dv_b, dsinks = _swa_attn_bwd(qk, proj_b, sinks, do_b, lse_b, Hq, token, "swa_attn_bwd")
    dproj_b = _rope([(dq_b, True, D), (dk_b, True, Wk), (dv_b, False, Wk)], tabs_b, BF16, "rope_bwd")
    token = on_grads("swa_w_in", _mm_tn(dproj_b, h1b[None], BF16, "swa_dwin", tm=1280, tn=1024)[0])
    dh1b = _mm_nt(dproj_b[None], W["swa_in"][None], F32, "swa_dh", tk=1280)
    dx2, dsc1b, dsh1b = _mod_bwd(dz1b, dh1b, x2, sc1b, token, "mod_mix1_bwd")

    dz2a, dy2a, dg_f0, db_f0, dgate2a = _ln_bwd(dx2, z2a, y2a, g2a, row(P["ln_ffn_g"][0]), "ln_ffn0_bwd")
    dh2a, token, dcw0, dcb0 = ffn_backward(dy2a, a_a, u_a, uc_a, h2a, 0, "0")
    dx1, dsc2a, dsh2a = _mod_bwd(dz2a, dh2a, x1, sc2a, token, "mod_ffn0_bwd")

    dz1a, dy1a, dg_m0, db_m0, dgate1a = _ln_bwd(dx1, z1a, y1a, g1a, row(P["ln_mix_g"][0]), "ln_mix0_bwd")
    token = on_grads("fox_w_o", _mm_tn(o_a, dy1a[None], BF16, "fox_dwo")[0])
    do_a = _mm_nt(dy1a[None], W["fox_o"][None], F32, "fox_do")
    dq_a, dk_a, dv_a, dcum_row, dcum_col = _fox_attn_bwd(proj_a, do_a, o_a, cum_row, lse_a, Hf, token, "fox_attn_bwd")
    dflT, dbf = _fox_prep_bwd(dcum_row.reshape(Hf, S), dcum_col.reshape(Hf, S), flT, bf_col, "fox_cumsum_bwd")
    n_pad = W["fox_in"].shape[1]
    dproj_a = jnp.concatenate([dq_a, dk_a.astype(BF16), dv_a.astype(BF16),
                               _pad_cols(dflT.T, n_pad - 3 * D).astype(BF16)], axis=1)
    dh1a = _mm_nt(dproj_a[None], W["fox_in"][None], F32, "fox_dh", tn=1024, tk=896)
    grad_x, dsc1a, dsh1a = _mod_bwd(dz1a, dh1a, x, sc1a, token, "mod_mix0_bwd")

    dmod = jnp.stack([jnp.concatenate([dsh1a, dsc1a, dgate1a, dsh2a, dsc2a, dgate2a], axis=1)[0],
                      jnp.concatenate([dsh1b, dsc1b, dgate1b, dsh2b, dsc2b, dgate2b], axis=1)[0]])
    small = dict(dmod=dmod, conv_b=jnp.stack([dcb0, dcb1]), conv_w=jnp.stack([dcw0, dcw1]),
                 ln_mix_g=jnp.concatenate([dg_m0, dg_m1]), ln_mix_b=jnp.concatenate([db_m0, db_m1]),
                 ln_ffn_g=jnp.concatenate([dg_f0, dg_f1]), ln_ffn_b=jnp.concatenate([db_f0, db_f1]),
                 fox_b_f=dbf.reshape(-1), swa_sinks=dsinks[0, :Hq], loss=loss_row[0, 0].reshape(1))
    exchanged = on_small(small)
    on_grads("fox_w_in", _mm_tn(dproj_a, h1a[None], BF16, "fox_dwin", tm=896, tn=1024)[0], exchanged)
    return grad_x


SMALL_ORDER = ("dmod", "conv_b", "conv_w", "ln_mix_g", "ln_mix_b", "ln_ffn_g", "ln_ffn_b", "fox_b_f", "swa_sinks", "loss")


def _pack_rows(arrays):
    chunks, spans, off = [], [], 0
    for a in arrays:
        flat = a.reshape(-1)
        n = -(-flat.shape[0] // LANE) * LANE
        chunks.append(jnp.pad(flat, (0, n - flat.shape[0])))
        spans.append((off, flat.shape[0], a.shape))
        off += n
    total = -(-off // (8 * LANE)) * (8 * LANE)
    chunks.append(jnp.zeros((total - off,), F32))
    return jnp.concatenate(chunks).reshape(-1, LANE), spans


def _unpack_rows(packed, spans):
    flat = packed.reshape(-1)
    return [flat[off:off + n].reshape(shape) for off, n, shape in spans]


def kernel(x, c, positions, fox_w_in, fox_b_f, fox_w_o, swa_w_in, swa_sinks, swa_w_o, ada_w, ada_b, ffn_w_up, ffn_conv_w, ffn_conv_b, ffn_w_down, ln_mix_g, ln_mix_b, ln_ffn_g, ln_ffn_b, loss_target, m_fox_w_in, m_fox_b_f, m_fox_w_o, m_swa_w_in, m_swa_sinks, m_swa_w_o, m_ada_w, m_ada_b, m_ffn_w_up, m_ffn_conv_w, m_ffn_conv_b, m_ffn_w_down, m_ln_mix_g, m_ln_mix_b, m_ln_ffn_g, m_ln_ffn_b, v_fox_w_in, v_fox_b_f, v_fox_w_o, v_swa_w_in, v_swa_sinks, v_swa_w_o, v_ada_w, v_ada_b, v_ffn_w_up, v_ffn_conv_w, v_ffn_conv_b, v_ffn_w_down, v_ln_mix_g, v_ln_mix_b, v_ln_ffn_g, v_ln_ffn_b):
    S, D = x.shape[1], x.shape[2]
    L = ada_w.shape[0]
    me = 4 * lax.axis_index("x") + 2 * lax.axis_index("y") + lax.axis_index("c")
    n_ada = ada_w.shape[2]
    cu = ffn_w_up.shape[2]
    F = 4 * cu
    n_in = fox_w_in.shape[2] * N_DEV
    n_in_pad = -(-n_in // LANE) * LANE

    gather_groups = dict(
        fox=([fox_w_in[0].astype(BF16), fox_w_o[0].astype(BF16)], ["major", "rows"]),
        ffn0=([ffn_w_up[0].astype(BF16), ffn_w_down[0].astype(BF16)], ["halves", "rows"]),
        swa=([swa_w_in[0].astype(BF16), swa_w_o[0].astype(BF16)], ["major", "rows"]),
        ffn1=([ffn_w_up[1].astype(BF16), ffn_w_down[1].astype(BF16)], ["halves", "rows"]))
    starts_after = dict(fox=["ffn0", "swa"], ffn0=["ffn1"])
    gathers = {}

    def start_group(group, after):
        shards, kinds = gather_groups[group]
        send, recv, local, thru, bufs, token = _gather_start(shards, kinds, after, "gather_start_" + group)
        gathers[group] = dict(send=send, recv=recv, local=local, shards=thru, bufs=bufs, kinds=kinds,
                              shapes=[a.shape for a in shards], token=token)
        return token

    c_rows = _all_gather_small(c.reshape(-1, LANE), "gather_c")
    fox_started = start_group("fox", c_rows)
    c_all = c_rows.reshape(N_DEV, D)
    b_cols = lax.dynamic_slice_in_dim(ada_b, me * n_ada, n_ada, axis=1).reshape(L, 1, n_ada)
    mod_blk = _ada_mod(jnp.pad(c_all, ((0, ADA_ROWS - N_DEV), (0, 0))), ada_w, b_cols, "ada_mod")[:, :N_DEV]
    mod_all = _all_gather_small(mod_blk.reshape(-1, LANE), "gather_mod", after=[fox_started])
    mod_all = mod_all.reshape(N_DEV, L, N_DEV, n_ada)
    mod_mine = lax.dynamic_index_in_dim(mod_all, me, axis=2, keepdims=False)
    mod_mine = jnp.transpose(mod_mine, (1, 0, 2)).reshape(L, N_DEV * n_ada)
    mods = [[mod_mine[l, k * D:(k + 1) * D].reshape(1, D) for k in range(6)] for l in range(L)]

    P = dict(fox_b_f=fox_b_f[0], swa_sinks=swa_sinks[0], conv_b=ffn_conv_b,
             ln_mix_g=ln_mix_g, ln_mix_b=ln_mix_b, ln_ffn_g=ln_ffn_g, ln_ffn_b=ln_ffn_b)
    cw_rows = _all_gather_small(_pack_rows([ffn_conv_w])[0], "gather_conv_w", after=[mod_all])
    n_cw = ffn_conv_w.size
    cw_dev = cw_rows.reshape(N_DEV, -1)[:, :n_cw].reshape(N_DEV, L, 3, cu)
    P["conv_w"] = jnp.transpose(cw_dev, (1, 2, 0, 3)).reshape(L, 3, N_DEV * cu)

    def natural(g, pad_to=None):
        slabs = [g[k] for k in range(N_DEV)]
        if pad_to is not None:
            slabs.append(jnp.zeros((D, pad_to - N_DEV * g.shape[2]), g.dtype))
        return jnp.concatenate(slabs, axis=1)

    def forward_stage(group, after):
        s = gathers[group]
        s["fsend"], s["frecv"], s["bufs"], token = _gather_forward(s["recv"], s["bufs"], s["kinds"], s["shapes"], after,
                                                                   "gather_forward_" + group)
        for nxt in starts_after.get(group, ()):
            token = start_group(nxt, token)
        return token

    def fetch(group, stage, after):
        if stage == "forward":
            return forward_stage(group, after)
        if group == "fox":
            after = forward_stage(group, cw_rows)
        s = gathers.pop(group)
        first, second = _gather_wait(s["send"], s["recv"], s["local"], s["fsend"], s["frecv"], s["shards"], s["bufs"],
                                     s["kinds"], after, "gather_wait_" + group)
        if group == "fox":
            return natural(first, n_in_pad), second
        if group == "swa":
            return natural(first), second
        return first, second

    out, pending = {}, {}

    def columns_major(g_t, n):
        return g_t[:n].reshape(N_DEV, n // N_DEV, D)

    def transposed(a):
        return jnp.transpose(a, (0, 2, 1))

    big = dict(
        ffn_w_down1=("ffn_w_down", "rows", 1, (ffn_w_down, m_ffn_w_down, v_ffn_w_down)),
        ffn_w_up1=("ffn_w_up", "halves", 1, (ffn_w_up, m_ffn_w_up, v_ffn_w_up)),
        swa_w_o=("swa_w_o", "rows", 0, (swa_w_o, m_swa_w_o, v_swa_w_o)),
        swa_w_in=("swa_w_in", "major", 0, tuple(transposed(a) for a in (swa_w_in, m_swa_w_in, v_swa_w_in))),
        ffn_w_down0=("ffn_w_down", "rows", 0, (ffn_w_down, m_ffn_w_down, v_ffn_w_down)),
        ffn_w_up0=("ffn_w_up", "halves", 0, (ffn_w_up, m_ffn_w_up, v_ffn_w_up)),
        fox_w_o=("fox_w_o", "rows", 0, (fox_w_o, m_fox_w_o, v_fox_w_o)),
        fox_w_in=("fox_w_in", "major", 0, (fox_w_in, m_fox_w_in, v_fox_w_in)))
    finish_at = dict(swa_w_o=["ffn_w_down1"], ffn_w_up0=["ffn_w_up1", "swa_w_o", "swa_w_in"], fox_w_o=["ffn_w_down0"],
                     fox_w_in=["ffn_w_up0"])
    tail = {}

    def finish(name, after):
        send, recv, thru, lands = pending.pop(name)
        param, kind, layer, wmv = big[name]
        if name == "fox_w_in":
            one = tail["last_start"][0, 0] + 1.0
            wmv = tuple(transposed(a * one) for a in wmv)
        landed, = _scatter_wait(send, recv, thru, lands, [kind], after, "scatter_wait_" + name)
        res = _adam_sum(landed, *wmv, layer, out.get(param), "adam_" + name, by_cols=kind == "major")
        out[param] = [transposed(r) for r in res] if kind == "major" else res

    def on_grads(name, g, after=None):
        kind = big[name][1]
        src = columns_major(g, n_in if name == "fox_w_in" else g.shape[0]) if kind == "major" else g
        send, recv, thru, lands, token = _scatter_start([src], [kind], after, "scatter_start_" + name)
        pending[name] = (send, recv, thru, lands)
        for done in finish_at.get(name, ()):
            finish(done, [token])
        tail["last_start"] = token
        return token

    def on_small(small):
        packed, tail["spans"] = _pack_rows([small[k] for k in SMALL_ORDER])
        send, recv, thru, lands, token = _scatter_start([packed], ["whole"], None, "small_grads_start")
        tail["small"] = (send, recv, thru, lands)
        return token

    grad_x = _local_step(x[0], loss_target[0], positions[0], mods, fetch, P, on_grads, on_small)

    spans = tail["spans"]
    done_first = [out[k][1] for k in ("ffn_w_up", "ffn_w_down", "swa_w_in", "swa_w_o")]
    gathered, = _scatter_wait(*tail["small"], ["whole"], done_first, "small_grads_wait")
    totals = dict(zip(SMALL_ORDER, _unpack_rows(_sum_slots(gathered, tail["last_start"], "sum_small_grads"), spans)))
    loss = totals["loss"].reshape(())
    n_mod = L * 6 * D
    dmod_all = gathered.reshape(N_DEV, -1)[:, :n_mod].reshape(N_DEV, L, 6 * D)
    dmod_cols = jnp.transpose(lax.dynamic_slice_in_dim(dmod_all, me * n_ada, n_ada, axis=2), (1, 0, 2))
    out["ada_w"] = _ada_bwd_adam(c_all.T, dmod_cols, ada_w, m_ada_w, v_ada_w, "adam_ada_w")

    g_small = dict(fox_b_f=totals["fox_b_f"].reshape(fox_b_f.shape), swa_sinks=totals["swa_sinks"].reshape(swa_sinks.shape),
                   ada_b=totals["dmod"].reshape(ada_b.shape), ffn_conv_b=totals["conv_b"].reshape(ffn_conv_b.shape),
                   ffn_conv_w=lax.dynamic_slice_in_dim(totals["conv_w"].reshape(L, 3, 2 * F), me * cu, cu, axis=2),
                   ln_mix_g=totals["ln_mix_g"], ln_mix_b=totals["ln_mix_b"],
                   ln_ffn_g=totals["ln_ffn_g"], ln_ffn_b=totals["ln_ffn_b"])
    small_names = ("fox_b_f", "swa_sinks", "ada_b", "ffn_conv_b", "ffn_conv_w", "ln_mix_g", "ln_mix_b", "ln_ffn_g", "ln_ffn_b")
    w_small = dict(fox_b_f=(fox_b_f, m_fox_b_f, v_fox_b_f), swa_sinks=(swa_sinks, m_swa_sinks, v_swa_sinks),
                   ada_b=(ada_b, m_ada_b, v_ada_b), ffn_conv_b=(ffn_conv_b, m_ffn_conv_b, v_ffn_conv_b),
                   ffn_conv_w=(ffn_conv_w, m_ffn_conv_w, v_ffn_conv_w),
                   ln_mix_g=(ln_mix_g, m_ln_mix_g, v_ln_mix_g), ln_mix_b=(ln_mix_b, m_ln_mix_b, v_ln_mix_b),
                   ln_ffn_g=(ln_ffn_g, m_ln_ffn_g, v_ln_ffn_g), ln_ffn_b=(ln_ffn_b, m_ln_ffn_b, v_ln_ffn_b))
    pk_g, sp = _pack_rows([g_small[k] for k in small_names])
    pk_w = _pack_rows([w_small[k][0] for k in small_names])[0]
    pk_m = _pack_rows([w_small[k][1] for k in small_names])[0]
    pk_v = _pack_rows([w_small[k][2] for k in small_names])[0]
    res = _adam(pk_g[None], pk_w[None], pk_m[None], pk_v[None], "adam_small")
    settled = [res[0], out["ada_w"][1]] + [out[k][1] for k in ("ffn_w_up", "ffn_w_down", "swa_w_in", "swa_w_o")]
    finish("fox_w_o", settled)
    finish("fox_w_in", settled)
    res = [dict(zip(small_names, _unpack_rows(r[0], sp))) for r in res]
    for k in small_names:
        out[k] = (g_small[k], res[0][k], res[1][k], res[2][k])

    order = ("fox_w_in", "fox_b_f", "fox_w_o", "swa_w_in", "swa_sinks", "swa_w_o", "ada_w", "ada_b", "ffn_w_up",
             "ffn_conv_w", "ffn_conv_b", "ffn_w_down", "ln_mix_g", "ln_mix_b", "ln_ffn_g", "ln_ffn_b")
    return (loss, grad_x[None], *[out[k][0] for k in order], *[out[k][1] for k in order],
            *[out[k][2] for k in order], *[out[k][3] for k in order])
```

```python
import functools

import jax
import jax.numpy as jnp
from jax import lax
from jax.experimental import pallas as pl
from jax.experimental.pallas import tpu as pltpu

F32 = jnp.float32
BF16 = jnp.bfloat16
MESH = pl.DeviceIdType.MESH
N_DEV = 8
AXES = ("x", "y", "c")

DEPTH = 2
ALPHA = (2.0 * DEPTH) ** 0.25
LN_EPS = 1e-5
FOX_HEAD_DIM = 128
SWA_HEAD_DIM = 64
SWA_GROUP = 8
SWA_WINDOW = 128
Q_BLOCK = 128
ROPE_DIM = 16
ROPE_THETA = 500000.0

ADAM_LR = 0.001
ADAM_B1 = 0.9
ADAM_B2 = 0.999
ADAM_EPS = 1e-08
ADAM_WD = 0.01
ADAM_STEP = 10

LANE = 128
MIB = 1024 * 1024


def _tile(n, pref, unit=LANE):
    if n <= pref:
        return n
    t = (pref // unit) * unit
    while t >= unit:
        if n % t == 0:
            return t
        t -= unit
    return n


def _params(sem, vmem_mib=48):
    return pltpu.CompilerParams(dimension_semantics=sem, vmem_limit_bytes=vmem_mib * MIB)


def _sigmoid(x):
    return 1.0 / (1.0 + jnp.exp(-x))


def _mm_call(dot, grid_mnk, in_specs, out_spec, out_shape, k_axis, nk, tm, tn, name, operands):
    sem = ("parallel",) * (len(grid_mnk) - 1) + ("arbitrary",)

    if nk == 1:
        def body(a_ref, b_ref, o_ref):
            o_ref[...] = dot(a_ref[...], b_ref[...]).astype(o_ref.dtype)
        scratch = []
    else:
        def body(a_ref, b_ref, o_ref, acc_ref):
            k = pl.program_id(k_axis)

            @pl.when(k == 0)
            def _():
                acc_ref[...] = jnp.zeros_like(acc_ref)

            acc_ref[...] += dot(a_ref[...], b_ref[...])

            @pl.when(k == nk - 1)
            def _():
                o_ref[...] = acc_ref[...].astype(o_ref.dtype)
        scratch = [pltpu.VMEM((tm, tn), F32)]

    return pl.pallas_call(
        body, name=name, grid=grid_mnk, in_specs=in_specs, out_specs=out_spec, out_shape=out_shape,
        scratch_shapes=scratch, compiler_params=_params(sem, 56),
    )(*operands)


def _dot(dims):
    def dot(a, b):
        return lax.dot_general(a.astype(BF16), b.astype(BF16), (dims, ((), ())), preferred_element_type=F32)
    return dot


def _mm_nn(a, b, out_dtype, name, tm=2048, tn=512, tk=2048):
    M, K = a.shape
    N = b.shape[1]
    tm, tn, tk = _tile(M, tm), _tile(N, tn), _tile(K, tk)
    nk = K // tk
    return _mm_call(
        _dot(((1,), (0,))), (M // tm, N // tn, nk),
        [pl.BlockSpec((tm, tk), lambda i, j, k: (i, k)), pl.BlockSpec((tk, tn), lambda i, j, k: (k, j))],
        pl.BlockSpec((tm, tn), lambda i, j, k: (i, j)), jax.ShapeDtypeStruct((M, N), out_dtype),
        2, nk, tm, tn, name, (a, b))


def _mm_nt(a, b, out_dtype, name, tm=2048, tn=512, tk=2048):
    P, M, K = a.shape
    N = b.shape[1]
    tm, tn, tk = _tile(M, tm), _tile(N, tn), _tile(K, tk)
    nk = K // tk
    return _mm_call(
        _dot(((1,), (1,))), (M // tm, N // tn, P * nk),
        [pl.BlockSpec((None, tm, tk), lambda i, j, k: (k // nk, i, k % nk)),
         pl.BlockSpec((None, tn, tk), lambda i, j, k: (k // nk, j, k % nk))],
        pl.BlockSpec((tm, tn), lambda i, j, k: (i, j)), jax.ShapeDtypeStruct((M, N), out_dtype),
        2, P * nk, tm, tn, name, (a, b))


def _mm_tn(a, b, out_dtype, name, tm=2048, tn=512, tk=2048):
    K, M = a.shape
    P, _, N = b.shape
    tm, tn, tk = _tile(M, tm), _tile(N, tn), _tile(K, tk)
    nk = K // tk
    return _mm_call(
        _dot(((0,), (0,))), (P, M // tm, N // tn, nk),
        [pl.BlockSpec((tk, tm), lambda p, i, j, k: (k, i)), pl.BlockSpec((None, tk, tn), lambda p, i, j, k: (p, k, j))],
        pl.BlockSpec((None, tm, tn), lambda p, i, j, k: (p, i, j)), jax.ShapeDtypeStruct((P, M, N), out_dtype),
        3, nk, tm, tn, name, (a, b))


ROW_TILE = 256


def _row_spec(tm, D):
    return pl.BlockSpec((tm, D), lambda i: (i, 0))


def _vec_spec(D):
    return pl.BlockSpec((1, D), lambda i: (0, 0))


def _modulate(x, sc, sh, name):
    S, D = x.shape
    tm = _tile(S, ROW_TILE, 8)

    def body(x_ref, sc_ref, sh_ref, h_ref):
        h_ref[...] = (x_ref[...] * (1.0 + sc_ref[...]) + sh_ref[...]).astype(BF16)

    return pl.pallas_call(
        body, name=name, grid=(S // tm,),
        in_specs=[_row_spec(tm, D), _vec_spec(D), _vec_spec(D)],
        out_specs=_row_spec(tm, D),
        out_shape=jax.ShapeDtypeStruct((S, D), BF16),
        compiler_params=_params(("parallel",)),
    )(x, sc, sh)


def _layer_norm_rows(z, gamma, beta):
    mu = jnp.mean(z, axis=-1, keepdims=True)
    zc = z - mu
    var = jnp.mean(zc * zc, axis=-1, keepdims=True)
    return zc * lax.rsqrt(var + LN_EPS) * gamma + beta


def _ln_fwd(x, y, gate, gamma, beta, sc_n, sh_n, after, name):
    S, D = x.shape
    tm = _tile(S, ROW_TILE, 8)

    def body(x_ref, y_ref, gate_ref, g_ref, b_ref, sc_ref, sh_ref, after_ref, z_ref, xo_ref, hn_ref):
        z = ALPHA * x_ref[...] + (1.0 + gate_ref[...]) * y_ref[...]
        xo = _layer_norm_rows(z, g_ref[...], b_ref[...])
        z_ref[...] = z
        xo_ref[...] = xo
        hn_ref[...] = (xo * (1.0 + sc_ref[...]) + sh_ref[...]).astype(BF16)

    return pl.pallas_call(
        body, name=name, grid=(S // tm,),
        in_specs=[_row_spec(tm, D), _row_spec(tm, D)] + [_vec_spec(D)] * 5 + [pl.BlockSpec(memory_space=pl.ANY)],
        out_specs=[_row_spec(tm, D)] * 3,
        out_shape=[jax.ShapeDtypeStruct((S, D), F32), jax.ShapeDtypeStruct((S, D), F32),
                   jax.ShapeDtypeStruct((S, D), BF16)],
        compiler_params=_params(("parallel",)),
    )(x, y, gate, gamma, beta, sc_n, sh_n, after)


def _ln_fwd_loss(x, y, gate, gamma, beta, target, name):
    S, D = x.shape
    tm = _tile(S, ROW_TILE, 8)

    def body(x_ref, y_ref, gate_ref, g_ref, b_ref, t_ref, z_ref, dout_ref, loss_ref):
        @pl.when(pl.program_id(0) == 0)
        def _():
            loss_ref[...] = jnp.zeros_like(loss_ref)

        z = ALPHA * x_ref[...] + (1.0 + gate_ref[...]) * y_ref[...]
        xo = _layer_norm_rows(z, g_ref[...], b_ref[...])
        err = xo - t_ref[...]
        z_ref[...] = z
        dout_ref[...] = err * (1.0 / D)
        loss_ref[...] += (0.5 / D) * jnp.sum(err * err)

    return pl.pallas_call(
        body, name=name, grid=(S // tm,),
        in_specs=[_row_spec(tm, D), _row_spec(tm, D)] + [_vec_spec(D)] * 3 + [_row_spec(tm, D)],
        out_specs=[_row_spec(tm, D), _row_spec(tm, D), pl.BlockSpec((1, LANE), lambda i: (0, 0))],
        out_shape=[jax.ShapeDtypeStruct((S, D), F32), jax.ShapeDtypeStruct((S, D), F32),
                   jax.ShapeDtypeStruct((1, LANE), F32)],
        compiler_params=_params(("arbitrary",)),
    )(x, y, gate, gamma, beta, target)


def _ln_bwd(dout, z, y, gate, gamma, name):
    S, D = z.shape
    tm = _tile(S, ROW_TILE, 8)

    def body(dout_ref, z_ref, y_ref, gate_ref, g_ref, dz_ref, dy_ref, dg_ref, db_ref, dgate_ref):
        @pl.when(pl.program_id(0) == 0)
        def _():
            dg_ref[...] = jnp.zeros_like(dg_ref)
            db_ref[...] = jnp.zeros_like(db_ref)
            dgate_ref[...] = jnp.zeros_like(dgate_ref)

        z = z_ref[...]
        dout = dout_ref[...]
        mu = jnp.mean(z, axis=-1, keepdims=True)
        zc = z - mu
        var = jnp.mean(zc * zc, axis=-1, keepdims=True)
        rstd = lax.rsqrt(var + LN_EPS)
        xhat = zc * rstd
        dxhat = dout * g_ref[...]
        m1 = jnp.mean(dxhat, axis=-1, keepdims=True)
        m2 = jnp.mean(dxhat * xhat, axis=-1, keepdims=True)
        dz = rstd * (dxhat - m1 - xhat * m2)
        dz_ref[...] = dz
        dy_ref[...] = (dz * (1.0 + gate_ref[...])).astype(BF16)
        dg_ref[...] += jnp.sum(dout * xhat, axis=0, keepdims=True)
        db_ref[...] += jnp.sum(dout, axis=0, keepdims=True)
        dgate_ref[...] += jnp.sum(dz * y_ref[...], axis=0, keepdims=True)

    return pl.pallas_call(
        body, name=name, grid=(S // tm,),
        in_specs=[_row_spec(tm, D)] * 3 + [_vec_spec(D)] * 2,
        out_specs=[_row_spec(tm, D), _row_spec(tm, D)] + [_vec_spec(D)] * 3,
        out_shape=[jax.ShapeDtypeStruct((S, D), F32), jax.ShapeDtypeStruct((S, D), BF16)]
        + [jax.ShapeDtypeStruct((1, D), F32)] * 3,
        compiler_params=_params(("arbitrary",)),
    )(dout, z, y, gate, gamma)


def _mod_bwd(dz, dh, xin, sc, after, name):
    S, D = dz.shape
    tm = _tile(S, ROW_TILE, 8)

    def body(dz_ref, dh_ref, x_ref, sc_ref, after_ref, dx_ref, dsc_ref, dsh_ref):
        @pl.when(pl.program_id(0) == 0)
        def _():
            dsc_ref[...] = jnp.zeros_like(dsc_ref)
            dsh_ref[...] = jnp.zeros_like(dsh_ref)

        dh = dh_ref[...]
        dx_ref[...] = ALPHA * dz_ref[...] + dh * (1.0 + sc_ref[...])
        dsc_ref[...] += jnp.sum(dh * x_ref[...], axis=0, keepdims=True)
        dsh_ref[...] += jnp.sum(dh, axis=0, keepdims=True)

    return pl.pallas_call(
        body, name=name, grid=(S // tm,),
        in_specs=[_row_spec(tm, D)] * 3 + [_vec_spec(D), pl.BlockSpec(memory_space=pl.ANY)],
        out_specs=[_row_spec(tm, D), _vec_spec(D), _vec_spec(D)],
        out_shape=[jax.ShapeDtypeStruct((S, D), F32)] + [jax.ShapeDtypeStruct((1, D), F32)] * 2,
        compiler_params=_params(("arbitrary",)),
    )(dz, dh, xin, sc, after)


def _shift_down(u, k, row):
    return jnp.where(row >= k, pltpu.roll(u, k, axis=0), 0.0)


def _shift_up(u, k, row, S):
    return jnp.where(row < S - k, pltpu.roll(u, S - k, axis=0), 0.0)


def _ffn_up(h, w, cw, cb, name):
    S, D = h.shape
    F = w.shape[2]
    tn = _tile(F, 256)

    piece = S // 2 if S % 16 == 0 else S
    halo = 8

    def body(h_ref, w_ref, cw_ref, cb_ref, u_ref, uc_ref, a_ref):
        tails = [None, None]
        for lo in range(0, S, piece):
            rows = slice(lo, lo + piece)
            conv = []
            for p in range(2):
                u = jnp.dot(h_ref[rows, :], w_ref[p], preferred_element_type=F32)
                u_ref[p, rows, :] = u.astype(BF16)
                above, tails[p] = tails[p], u[piece - halo:]
                if above is not None:
                    u = jnp.concatenate([above, u], axis=0)
                row = lax.broadcasted_iota(jnp.int32, u.shape, 0)
                cwp = cw_ref[p]
                uc = _shift_down(u, 2, row) * cwp[0:1] + _shift_down(u, 1, row) * cwp[1:2] + u * cwp[2:3] + cb_ref[p]
                uc = uc if above is None else uc[halo:]
                uc_ref[p, rows, :] = uc.astype(BF16)
                conv.append(uc)
            g, v = conv
            a_ref[rows, :] = (g * _sigmoid(g) * v).astype(BF16)

    half = pl.BlockSpec((2, S, tn), lambda j: (0, 0, j))
    return pl.pallas_call(
        body, name=name, grid=(F // tn,),
        in_specs=[pl.BlockSpec((S, D), lambda j: (0, 0)), pl.BlockSpec((2, D, tn), lambda j: (0, 0, j)),
                  pl.BlockSpec((2, 3, tn), lambda j: (0, 0, j)), pl.BlockSpec((2, 1, tn), lambda j: (0, 0, j))],
        out_specs=[half, half, pl.BlockSpec((S, tn), lambda j: (0, j))],
        out_shape=[jax.ShapeDtypeStruct((2, S, F), BF16), jax.ShapeDtypeStruct((2, S, F), BF16),
                   jax.ShapeDtypeStruct((S, F), BF16)],
        compiler_params=_params(("parallel",), 56),
    )(h, w, cw, cb)


def _ffn_bwd_elem(da, u, uc, cw, after, name):
    _, S, F = u.shape
    tn = _tile(F, 256)

    def body(da_ref, u_ref, uc_ref, cw_ref, after_ref, du_ref, dcw_ref, dcb_ref):
        row = lax.broadcasted_iota(jnp.int32, (S, tn), 0)
        da = da_ref[...]
        g, v = uc_ref[0].astype(F32), uc_ref[1].astype(F32)
        sg = _sigmoid(g)
        d_conv = (da * v * (sg * (1.0 + g * (1.0 - sg))), da * (g * sg))
        for p in range(2):
            d = d_conv[p]
            cwp = cw_ref[p]
            u = u_ref[p].astype(F32)
            d1, d2 = _shift_up(d, 1, row, S), _shift_up(d, 2, row, S)
            dcb_ref[p] = jnp.sum(d, axis=0, keepdims=True)
            dcw_ref[p, 0:1, :] = jnp.sum(d2 * u, axis=0, keepdims=True)
            dcw_ref[p, 1:2, :] = jnp.sum(d1 * u, axis=0, keepdims=True)
            dcw_ref[p, 2:3, :] = jnp.sum(d * u, axis=0, keepdims=True)
            du_ref[p] = (d * cwp[2:3] + d1 * cwp[1:2] + d2 * cwp[0:1]).astype(BF16)

    half = pl.BlockSpec((2, S, tn), lambda j: (0, 0, j))
    return pl.pallas_call(
        body, name=name, grid=(F // tn,),
        in_specs=[pl.BlockSpec((S, tn), lambda j: (0, j)), half, half, pl.BlockSpec((2, 3, tn), lambda j: (0, 0, j)),
                  pl.BlockSpec(memory_space=pl.ANY)],
        out_specs=[half, pl.BlockSpec((2, 3, tn), lambda j: (0, 0, j)), pl.BlockSpec((2, 1, tn), lambda j: (0, 0, j))],
        out_shape=[jax.ShapeDtypeStruct((2, S, F), BF16), jax.ShapeDtypeStruct((2, 3, F), F32),
                   jax.ShapeDtypeStruct((2, 1, F), F32)],
        compiler_params=_params(("parallel",), 56),
    )(da, u, uc, cw, after)


def _split3(x):
    hi = x.astype(BF16)
    r1 = x - hi.astype(F32)
    mid = r1.astype(BF16)
    lo = (r1 - mid.astype(F32)).astype(BF16)
    return hi, mid, lo


def _tri_matmul(x, upper, S):
    tc = _tile(S, 512)
    parts = _split3(x)
    outs = []
    for b in range(S // tc):
        r = lax.broadcasted_iota(jnp.int32, (S, tc), 0)
        c = lax.broadcasted_iota(jnp.int32, (S, tc), 1) + b * tc
        tri = jnp.where((r <= c) if upper else (r >= c), 1.0, 0.0).astype(BF16)
        acc = jnp.dot(parts[0], tri, preferred_element_type=F32)
        acc += jnp.dot(parts[1], tri, preferred_element_type=F32)
        acc += jnp.dot(parts[2], tri, preferred_element_type=F32)
        outs.append(acc)
    return outs, tc


def _fox_prep(flT, bf, name):
    H, S = flT.shape

    def body(fl_ref, b_ref, cum_ref):
        zz = fl_ref[...] + b_ref[...]
        lf = jnp.minimum(zz, 0.0) - jnp.log(1.0 + jnp.exp(-jnp.abs(zz)))
        outs, tc = _tri_matmul(lf, True, S)
        for b, o in enumerate(outs):
            cum_ref[:, b * tc:(b + 1) * tc] = o

    return pl.pallas_call(
        body, name=name,
        in_specs=[pl.BlockSpec(memory_space=pltpu.VMEM)] * 2,
        out_specs=pl.BlockSpec(memory_space=pltpu.VMEM),
        out_shape=jax.ShapeDtypeStruct((H, S), F32),
        compiler_params=pltpu.CompilerParams(vmem_limit_bytes=48 * MIB),
    )(flT, bf)


def _fox_prep_bwd(dcum_key, dcum_query, flT, bf, name):
    H, S = flT.shape

    def body(dck_ref, dcq_ref, fl_ref, b_ref, dfl_ref, dbf_ref):
        zz = fl_ref[...] + b_ref[...]
        outs, tc = _tri_matmul(dck_ref[...] + dcq_ref[...], False, S)
        total = jnp.zeros((H, 1), F32)
        for b, o in enumerate(outs):
            dfl = o * _sigmoid(-zz[:, b * tc:(b + 1) * tc])
            dfl_ref[:, b * tc:(b + 1) * tc] = dfl
            total += jnp.sum(dfl, axis=1, keepdims=True)
        dbf_ref[...] = total

    return pl.pallas_call(
        body, name=name,
        in_specs=[pl.BlockSpec(memory_space=pltpu.VMEM)] * 4,
        out_specs=[pl.BlockSpec(memory_space=pltpu.VMEM)] * 2,
        out_shape=[jax.ShapeDtypeStruct((H, S), F32), jax.ShapeDtypeStruct((H, 1), F32)],
        compiler_params=pltpu.CompilerParams(vmem_limit_bytes=48 * MIB),
    )(dcum_key, dcum_query, flT, bf)


FOX_TQ = 256
FOX_TC = 512


def _fox_scores(q, k_ref, ck_ref, i, lo, n, tq):
    k = k_ref[lo:lo + n, :].astype(BF16)
    s = lax.dot_general(q, k, (((1,), (1,)), ((), ())), preferred_element_type=F32) - ck_ref[:, lo:lo + n]
    qpos = i * tq + lax.broadcasted_iota(jnp.int32, (tq, n), 0)
    kpos = lo + lax.broadcasted_iota(jnp.int32, (tq, n), 1)
    return jnp.where(kpos <= qpos, s, -jnp.inf)


FOX_FWD_SPLITS = 4


def _fox_attn_fwd_part(proj, cum_row, H, q_lo, q_hi, filled, name):
    S = proj.shape[0]
    dh = FOX_HEAD_DIM
    tq = _tile(q_hi - q_lo, 2 * FOX_TQ)
    first = q_lo // tq
    scale = dh ** -0.5
    n_keep = 0 if filled is None else 2

    def body(q_ref, k_ref, v_ref, ck_ref, *rest):
        o_ref, lse_ref = rest[n_keep:]
        q = (q_ref[...] * scale).astype(BF16)
        s = _fox_scores(q, k_ref, ck_ref, first + pl.program_id(1), 0, q_hi, tq)
        m = jnp.max(s, axis=-1, keepdims=True)
        p = jnp.exp(s - m)
        l = jnp.sum(p, axis=-1, keepdims=True)
        o = jnp.dot(p.astype(BF16), v_ref[...].astype(BF16), preferred_element_type=F32) / l
        o_ref[...] = o.astype(BF16)
        lse_ref[...] = m + jnp.log(l)

    return pl.pallas_call(
        body, name=name, grid=(H, (q_hi - q_lo) // tq),
        in_specs=[pl.BlockSpec((tq, dh), lambda h, i: (first + i, h)),
                  pl.BlockSpec((q_hi, dh), lambda h, i: (0, H + h)),
                  pl.BlockSpec((q_hi, dh), lambda h, i: (0, 2 * H + h)),
                  pl.BlockSpec((None, 1, q_hi), lambda h, i: (h, 0, 0))] + [pl.BlockSpec(memory_space=pl.ANY)] * n_keep,
        out_specs=[pl.BlockSpec((tq, dh), lambda h, i: (first + i, h)),
                   pl.BlockSpec((None, tq, 1), lambda h, i: (h, first + i, 0))],
        out_shape=[jax.ShapeDtypeStruct((S, H * dh), BF16), jax.ShapeDtypeStruct((H, S, 1), F32)],
        input_output_aliases={4 + k: k for k in range(n_keep)},
        compiler_params=_params(("parallel", "parallel")),
    )(proj, proj, proj, cum_row, *(filled or ()))


def _fox_attn_fwd(proj, cum_row, H, name):
    S = proj.shape[0]
    n = FOX_FWD_SPLITS if S % (FOX_FWD_SPLITS * FOX_TQ) == 0 else 1
    out = None
    for part in range(n):
        out = _fox_attn_fwd_part(proj, cum_row, H, part * S // n, (part + 1) * S // n, out, name + str(part))
    return out


def _fox_attn_bwd(proj, do, o, cum_row, lse, H, after, name):
    S = proj.shape[0]
    dh = FOX_HEAD_DIM
    tq = _tile(S, 2 * FOX_TQ)
    tc = _tile(S, FOX_TC)
    scale = dh ** -0.5

    def body(q_ref, k_ref, v_ref, do_ref, o_ref, ck_ref, lse_ref, after_ref,
             dq_ref, dk_ref, dv_ref, dck_ref, dcq_ref, acc_ref):
        i = pl.program_id(1)

        @pl.when(i == 0)
        def _():
            dk_ref[...] = jnp.zeros_like(dk_ref)
            dv_ref[...] = jnp.zeros_like(dv_ref)
            dck_ref[...] = jnp.zeros_like(dck_ref)

        acc_ref[...] = jnp.zeros_like(acc_ref)
        dcq_ref[...] = jnp.zeros_like(dcq_ref)
        q = (q_ref[...] * scale).astype(BF16)
        do_f = do_ref[...]
        do_b = do_f.astype(BF16)
        delta = jnp.sum(do_f * o_ref[...].astype(F32), axis=-1, keepdims=True)
        lse_q = lse_ref[...]
        for c in range(S // tc):
            @pl.when(c * tc <= i * tq + tq - 1)
            def _():
                rows = slice(c * tc, (c + 1) * tc)
                p = jnp.exp(_fox_scores(q, k_ref, ck_ref, i, c * tc, tc, tq) - lse_q)
                dp = lax.dot_general(do_b, v_ref[rows, :].astype(BF16), (((1,), (1,)), ((), ())),
                                     preferred_element_type=F32)
                ds = p * (dp - delta)
                ds_b = ds.astype(BF16)
                acc_ref[...] += jnp.dot(ds_b, k_ref[rows, :].astype(BF16), preferred_element_type=F32)
                dk_ref[rows, :] += lax.dot_general(ds_b, q, (((0,), (0,)), ((), ())), preferred_element_type=F32)
                dv_ref[rows, :] += lax.dot_general(p.astype(BF16), do_b, (((0,), (0,)), ((), ())),
                                                   preferred_element_type=F32)
                dck_ref[:, rows] -= jnp.sum(ds, axis=0, keepdims=True)
                dcq_ref[...] += jnp.sum(ds, axis=-1, keepdims=True)
        dq_ref[...] = (acc_ref[...] * scale).astype(BF16)

    W = H * dh
    return pl.pallas_call(
        body, name=name, grid=(H, S // tq),
        in_specs=[pl.BlockSpec((tq, dh), lambda h, i: (i, h)),
                  pl.BlockSpec((S, dh), lambda h, i: (0, H + h)),
                  pl.BlockSpec((S, dh), lambda h, i: (0, 2 * H + h)),
                  pl.BlockSpec((tq, dh), lambda h, i: (i, h)),
                  pl.BlockSpec((tq, dh), lambda h, i: (i, h)),
                  pl.BlockSpec((None, 1, S), lambda h, i: (h, 0, 0)),
                  pl.BlockSpec((None, tq, 1), lambda h, i: (h, i, 0)),
                  pl.BlockSpec(memory_space=pl.ANY)],
        out_specs=[pl.BlockSpec((tq, dh), lambda h, i: (i, h)),
                   pl.BlockSpec((S, dh), lambda h, i: (0, h)),
                   pl.BlockSpec((S, dh), lambda h, i: (0, h)),
                   pl.BlockSpec((None, 1, S), lambda h, i: (h, 0, 0)),
                   pl.BlockSpec((None, tq, 1), lambda h, i: (h, i, 0))],
        out_shape=[jax.ShapeDtypeStruct((S, W), BF16), jax.ShapeDtypeStruct((S, W), F32),
                   jax.ShapeDtypeStruct((S, W), F32), jax.ShapeDtypeStruct((H, 1, S), F32),
                   jax.ShapeDtypeStruct((H, S, 1), F32)],
        scratch_shapes=[pltpu.VMEM((tq, dh), F32)],
        compiler_params=_params(("parallel", "arbitrary")),
    )(proj, proj, proj, do, o, cum_row, lse, after)


def _rope(parts, tabs, out_dtype, name):
    S = parts[0][0].shape[0]
    widths = [w for _, _, w in parts]
    total = sum(widths)
    tm = _tile(S, ROW_TILE, 8)
    flags = [r for _, r, _ in parts]

    def body(*refs):
        in_refs = refs[:len(parts)]
        cos_ref, sa_ref, sb_ref, o_ref = refs[len(parts):]
        cos, sa, sb = cos_ref[...], sa_ref[...], sb_ref[...]
        off = 0
        for ref, rot, w in zip(in_refs, flags, widths):
            for j in range(w // LANE):
                t = ref[:, j * LANE:(j + 1) * LANE]
                if rot:
                    t = t * cos + pltpu.roll(t, LANE - ROPE_DIM // 2, axis=1) * sa + pltpu.roll(t, ROPE_DIM // 2, axis=1) * sb
                o_ref[:, off + j * LANE:off + (j + 1) * LANE] = t.astype(o_ref.dtype)
            off += w

    return pl.pallas_call(
        body, name=name, grid=(S // tm,),
        in_specs=[pl.BlockSpec((tm, w), lambda i: (i, 0)) for w in widths] + [_row_spec(tm, LANE)] * 3,
        out_specs=_row_spec(tm, total),
        out_shape=jax.ShapeDtypeStruct((S, total), out_dtype),
        compiler_params=_params(("parallel",)),
    )(*[a for a, _, _ in parts], *tabs)


def _swa_band(ref_p, ref_c, hk):
    dh = SWA_HEAD_DIM
    return jnp.concatenate([ref_p[:, hk * dh:(hk + 1) * dh], ref_c[:, hk * dh:(hk + 1) * dh]], axis=0).astype(BF16)


def _swa_bias(G):
    qi = jnp.arange(G * Q_BLOCK)[:, None] % Q_BLOCK
    kj = jnp.arange(2 * Q_BLOCK)[None, :]
    rel = qi + Q_BLOCK - kj
    window = (rel >= 0) & (rel < SWA_WINDOW)
    both = jnp.stack([window & (kj >= Q_BLOCK), window])
    return jnp.where(both, 0.0, -jnp.inf).astype(F32)


def _swa_stack(ref, hk, G):
    dh = SWA_HEAD_DIM
    return jnp.concatenate([ref[:, (hk * G + g) * dh:(hk * G + g + 1) * dh] for g in range(G)], axis=0)


def _swa_unstack(ref, stacked, hk, G):
    dh, QB = SWA_HEAD_DIM, Q_BLOCK
    for g in range(0, G, 2):
        c0 = (hk * G + g) * dh
        pair = jnp.concatenate([stacked[g * QB:(g + 1) * QB], stacked[(g + 1) * QB:(g + 2) * QB]], axis=1)
        ref[:, c0:c0 + 2 * dh] = pair.astype(ref.dtype)


def _swa_sink_rows(sink_ref, hk, G):
    return jnp.concatenate([jnp.broadcast_to(sink_ref[0:1, hk * G + g:hk * G + g + 1], (Q_BLOCK, 1)) for g in range(G)],
                           axis=0)


def _swa_attn_fwd(qk, proj, sinks, Hq, name):
    S = qk.shape[0]
    dh, G, QB = SWA_HEAD_DIM, SWA_GROUP, Q_BLOCK
    Hk = Hq // G
    Wq, Wk = Hq * dh, Hk * dh
    nb = S // QB
    scale = dh ** -0.5

    per_step = 4 if nb % 4 == 0 else 2

    def body(q_ref, kp_ref, kc_ref, vp_ref, vc_ref, sink_ref, bias_ref, o_ref, lse_ref):
        n = pl.program_id(0)
        lane = lax.broadcasted_iota(jnp.int32, (QB, LANE), 1)
        blocks = [pl.ds(sub * QB, QB) for sub in range(per_step)]
        for sub, rows in enumerate(blocks):
            bias = bias_ref[jnp.minimum(n, 1)] if sub == 0 else bias_ref[1]
            k_band = (kp_ref, kc_ref.at[rows]) if sub == 0 else (kc_ref.at[blocks[sub - 1]], kc_ref.at[rows])
            v_band = (vp_ref, vc_ref.at[rows]) if sub == 0 else (vc_ref.at[blocks[sub - 1]], vc_ref.at[rows])
            lse_tile = jnp.zeros((QB, LANE), F32)
            for hk in range(Hk):
                kb = _swa_band(*k_band, hk)
                vb = _swa_band(*v_band, hk)
                q = (_swa_stack(q_ref.at[rows], hk, G) * scale).astype(BF16)
                sk = _swa_sink_rows(sink_ref, hk, G)
                s = lax.dot_general(q, kb, (((1,), (1,)), ((), ())), preferred_element_type=F32) + bias
                m = jnp.maximum(jnp.max(s, axis=-1, keepdims=True), sk)
                p = jnp.exp(s - m)
                l = jnp.sum(p, axis=-1, keepdims=True) + jnp.exp(sk - m)
                o = jnp.dot(p.astype(BF16), vb, preferred_element_type=F32) / l
                lse = m + jnp.log(l)
                for g in range(G):
                    lse_tile = jnp.where(lane == hk * G + g, lse[g * QB:(g + 1) * QB], lse_tile)
                _swa_unstack(o_ref.at[rows], o, hk, G)
            lse_ref[rows, :] = lse_tile

    kcol, vcol = Wq // Wk, (Wq + Wk) // Wk
    before = lambda n: jnp.maximum(per_step * n - 1, 0)
    rows_per_step = per_step * QB
    return pl.pallas_call(
        body, name=name, grid=(nb // per_step,),
        in_specs=[pl.BlockSpec((rows_per_step, Wq), lambda n: (n, 0)),
                  pl.BlockSpec((QB, Wk), lambda n: (before(n), kcol)),
                  pl.BlockSpec((rows_per_step, Wk), lambda n: (n, kcol)),
                  pl.BlockSpec((QB, Wk), lambda n: (before(n), vcol)),
                  pl.BlockSpec((rows_per_step, Wk), lambda n: (n, vcol)),
                  pl.BlockSpec((1, LANE), lambda n: (0, 0)),
                  pl.BlockSpec((2, G * QB, 2 * QB), lambda n: (0, 0, 0))],
        out_specs=[pl.BlockSpec((rows_per_step, Wq), lambda n: (n, 0)),
                   pl.BlockSpec((rows_per_step, LANE), lambda n: (n, 0))],
        out_shape=[jax.ShapeDtypeStruct((S, Wq), BF16), jax.ShapeDtypeStruct((S, LANE), F32)],
        compiler_params=_params(("parallel",)),
    )(qk, qk, qk, proj, proj, sinks, _swa_bias(G))


def _swa_attn_bwd(qk, proj, sinks, do, lse, Hq, after, name):
    S = qk.shape[0]
    dh, G, QB = SWA_HEAD_DIM, SWA_GROUP, Q_BLOCK
    Hk = Hq // G
    Wq, Wk = Hq * dh, Hk * dh
    nb = S // QB
    scale = dh ** -0.5

    def body(q_ref, kp_ref, kc_ref, vp_ref, vc_ref, sink_ref, do_ref, lse_ref, bias_ref, after_ref,
             dq_ref, dk_ref, dv_ref, dsink_ref, carry_k, carry_v):
        n = pl.program_id(0)

        @pl.when(n == 0)
        def _():
            dsink_ref[...] = jnp.zeros_like(dsink_ref)

        @pl.when(n < nb)
        def _():
            bias = bias_ref[...]
            lane = lax.broadcasted_iota(jnp.int32, (1, LANE), 1)
            dsink = jnp.zeros((1, LANE), F32)
            dk_heads, dv_heads = [], []
            for hk in range(Hk):
                kb = _swa_band(kp_ref, kc_ref, hk)
                vb = _swa_band(vp_ref, vc_ref, hk)
                q = (_swa_stack(q_ref, hk, G) * scale).astype(BF16)
                do_s = _swa_stack(do_ref, hk, G).astype(BF16)
                lse = jnp.concatenate([lse_ref[:, hk * G + g:hk * G + g + 1] for g in range(G)], axis=0)
                s = lax.dot_general(q, kb, (((1,), (1,)), ((), ())), preferred_element_type=F32) + bias
                p = jnp.exp(s - lse)
                p_sink = jnp.exp(_swa_sink_rows(sink_ref, hk, G) - lse)
                dp = lax.dot_general(do_s, vb, (((1,), (1,)), ((), ())), preferred_element_type=F32)
                delta = jnp.sum(p * dp, axis=-1, keepdims=True)
                ds_b = (p * (dp - delta)).astype(BF16)
                _swa_unstack(dq_ref, jnp.dot(ds_b, kb, preferred_element_type=F32) * scale, hk, G)
                dk_heads.append(lax.dot_general(ds_b, q, (((0,), (0,)), ((), ())), preferred_element_type=F32))
                dv_heads.append(lax.dot_general(p.astype(BF16), do_s, (((0,), (0,)), ((), ())), preferred_element_type=F32))
                sink_term = p_sink * delta
                for g in range(G):
                    dsink = jnp.where(lane == hk * G + g,
                                      -jnp.sum(sink_term[g * QB:(g + 1) * QB], axis=0, keepdims=True), dsink)
            dsink_ref[...] += dsink
            dk_all = jnp.concatenate(dk_heads, axis=1)
            dv_all = jnp.concatenate(dv_heads, axis=1)

            @pl.when(n > 0)
            def _():
                dk_ref[...] = carry_k[...] + dk_all[:QB]
                dv_ref[...] = carry_v[...] + dv_all[:QB]

            carry_k[...] = dk_all[QB:]
            carry_v[...] = dv_all[QB:]

        @pl.when(n == nb)
        def _():
            dk_ref[...] = carry_k[...]
            dv_ref[...] = carry_v[...]

    kcol, vcol = Wq // Wk, (Wq + Wk) // Wk
    cur = lambda n: jnp.minimum(n, nb - 1)
    prev = lambda n: jnp.maximum(jnp.minimum(n, nb - 1) - 1, 0)
    return pl.pallas_call(
        body, name=name, grid=(nb + 1,),
        in_specs=[pl.BlockSpec((QB, Wq), lambda n: (cur(n), 0)),
                  pl.BlockSpec((QB, Wk), lambda n: (prev(n), kcol)),
                  pl.BlockSpec((QB, Wk), lambda n: (cur(n), kcol)),
                  pl.BlockSpec((QB, Wk), lambda n: (prev(n), vcol)),
                  pl.BlockSpec((QB, Wk), lambda n: (cur(n), vcol)),
                  pl.BlockSpec((1, LANE), lambda n: (0, 0)),
                  pl.BlockSpec((QB, Wq), lambda n: (cur(n), 0)),
                  pl.BlockSpec((QB, LANE), lambda n: (cur(n), 0)),
                  pl.BlockSpec((None, G * QB, 2 * QB), lambda n: (jnp.minimum(n, 1), 0, 0)),
                  pl.BlockSpec(memory_space=pl.ANY)],
        out_specs=[pl.BlockSpec((QB, Wq), lambda n: (cur(n), 0)),
                   pl.BlockSpec((QB, Wk), lambda n: (jnp.maximum(n - 1, 0), 0)),
                   pl.BlockSpec((QB, Wk), lambda n: (jnp.maximum(n - 1, 0), 0)),
                   pl.BlockSpec((1, LANE), lambda n: (0, 0))],
        out_shape=[jax.ShapeDtypeStruct((S, Wq), F32), jax.ShapeDtypeStruct((S, Wk), F32),
                   jax.ShapeDtypeStruct((S, Wk), F32), jax.ShapeDtypeStruct((1, LANE), F32)],
        scratch_shapes=[pltpu.VMEM((QB, Wk), F32), pltpu.VMEM((QB, Wk), F32)],
        compiler_params=_params(("arbitrary",)),
    )(qk, qk, qk, proj, proj, sinks, do, lse, _swa_bias(G), after)


ADA_ROWS = 16


def _ada_mod(c_pad, w, b, name):
    L, D, N = w.shape
    tn = _tile(N, 512)

    def body(c_ref, w_ref, b_ref, o_ref):
        c = c_ref[...]
        c = c * _sigmoid(c)
        ch = c.astype(BF16)
        cl = (c - ch.astype(F32)).astype(BF16)
        ww = w_ref[...]
        wh = ww.astype(BF16)
        wl = (ww - wh.astype(F32)).astype(BF16)
        acc = jnp.dot(ch, wh, preferred_element_type=F32)
        acc += jnp.dot(ch, wl, preferred_element_type=F32)
        acc += jnp.dot(cl, wh, preferred_element_type=F32)
        o_ref[...] = acc + b_ref[...]

    return pl.pallas_call(
        body, name=name, grid=(L, N // tn),
        in_specs=[pl.BlockSpec((ADA_ROWS, D), lambda l, j: (0, 0)),
                  pl.BlockSpec((None, D, tn), lambda l, j: (l, 0, j)),
                  pl.BlockSpec((None, 1, tn), lambda l, j: (l, 0, j))],
        out_specs=pl.BlockSpec((None, ADA_ROWS, tn), lambda l, j: (l, 0, j)),
        out_shape=jax.ShapeDtypeStruct((L, ADA_ROWS, N), F32),
        compiler_params=_params(("parallel", "parallel")),
    )(c_pad, w, b)


def _ada_bwd_adam(cT, dm, w, m, v, name):
    D = cT.shape[0]
    L, B, N = dm.shape
    tm = _tile(D, 256)

    def body(c_ref, dm_ref, w_ref, m_ref, v_ref, g_ref, d_ref, mo_ref, vo_ref):
        c = c_ref[...]
        c = c * _sigmoid(c)
        dmv = dm_ref[...]
        g = c[:, 0:1] * dmv[0:1, :]
        for b in range(1, B):
            g += c[:, b:b + 1] * dmv[b:b + 1, :]
        delta, mn, vn = _adamw_math(w_ref[...], g, m_ref[...], v_ref[...])
        g_ref[...] = g
        d_ref[...] = delta
        mo_ref[...] = mn
        vo_ref[...] = vn

    spec = pl.BlockSpec((None, tm, N), lambda l, i: (l, i, 0))
    return pl.pallas_call(
        body, name=name, grid=(L, D // tm),
        in_specs=[pl.BlockSpec((tm, B), lambda l, i: (i, 0)), pl.BlockSpec((None, B, N), lambda l, i: (l, 0, 0)),
                  spec, spec, spec],
        out_specs=[spec] * 4,
        out_shape=[jax.ShapeDtypeStruct((L, D, N), F32)] * 4,
        compiler_params=_params(("parallel", "parallel")),
    )(cT, dm, w, m, v)


def _adamw_math(w, g, m, v):
    m = ADAM_B1 * m + (1.0 - ADAM_B1) * g
    v = ADAM_B2 * v + (1.0 - ADAM_B2) * (g * g)
    m_hat = m / (1.0 - ADAM_B1 ** ADAM_STEP)
    v_hat = v / (1.0 - ADAM_B2 ** ADAM_STEP)
    delta = -ADAM_LR * (m_hat / (jnp.sqrt(v_hat) + ADAM_EPS) + ADAM_WD * w)
    return delta, m, v


def _adam_rows(R, C):
    lanes = -(-C // LANE) * LANE
    return _tile(R, max(8, (262144 // lanes) // 8 * 8), 8)


def _adam_sum(recv, w, m, v, layer, filled, name, by_cols=False):
    P, R, C = recv.shape
    L = w.shape[0]
    n_keep = 0 if filled is None else 4
    if by_cols:
        tc = _tile(C, 256)
        grid, spec = (C // tc,), pl.BlockSpec((None, R, tc), lambda i: (layer, 0, i))
        recv_spec = pl.BlockSpec((P, R, tc), lambda i: (0, 0, i))
    else:
        tr = _adam_rows(R, C)
        grid, spec = (R // tr,), pl.BlockSpec((None, tr, C), lambda i: (layer, i, 0))
        recv_spec = pl.BlockSpec((P, tr, C), lambda i: (0, i, 0))

    def body(r_ref, w_ref, m_ref, v_ref, *rest):
        g_ref, d_ref, mo_ref, vo_ref = rest[n_keep:]
        g = r_ref[0].astype(F32)
        for p in range(1, P):
            g = g + r_ref[p].astype(F32)
        delta, mn, vn = _adamw_math(w_ref[...], g, m_ref[...], v_ref[...])
        g_ref[...] = g
        d_ref[...] = delta
        mo_ref[...] = mn
        vo_ref[...] = vn

    return pl.pallas_call(
        body, name=name, grid=grid,
        in_specs=[recv_spec, spec, spec, spec] + [pl.BlockSpec(memory_space=pl.ANY)] * n_keep,
        out_specs=[spec] * 4,
        out_shape=[jax.ShapeDtypeStruct((L, R, C), F32)] * 4,
        input_output_aliases={4 + k: k for k in range(n_keep)},
        compiler_params=_params(("parallel",)),
    )(recv, w, m, v, *(filled or ()))


def _adam(g, w, m, v, name):
    L, R, C = w.shape
    tr = _adam_rows(R, C)

    def body(g_ref, w_ref, m_ref, v_ref, d_ref, mo_ref, vo_ref):
        delta, mn, vn = _adamw_math(w_ref[...], g_ref[...], m_ref[...], v_ref[...])
        d_ref[...] = delta
        mo_ref[...] = mn
        vo_ref[...] = vn

    spec = pl.BlockSpec((None, tr, C), lambda l, i: (l, i, 0))
    return pl.pallas_call(
        body, name=name, grid=(L, R // tr),
        in_specs=[spec] * 4, out_specs=[spec] * 3,
        out_shape=[jax.ShapeDtypeStruct((L, R, C), F32)] * 3,
        compiler_params=_params(("parallel", "parallel")),
    )(g, w, m, v)


def _sum_slots(x, after, name):
    P, R, C = x.shape

    def body(x_ref, after_ref, o_ref):
        acc = x_ref[0]
        for p in range(1, P):
            acc = acc + x_ref[p]
        o_ref[...] = acc

    return pl.pallas_call(
        body, name=name,
        in_specs=[pl.BlockSpec(memory_space=pltpu.VMEM), pl.BlockSpec(memory_space=pl.ANY)],
        out_specs=pl.BlockSpec(memory_space=pltpu.VMEM),
        out_shape=jax.ShapeDtypeStruct((R, C), F32),
        compiler_params=pltpu.CompilerParams(vmem_limit_bytes=48 * MIB),
    )(x, after)


def _my_pos():
    return lax.axis_index("x"), lax.axis_index("y"), lax.axis_index("c")


def _all_gather_small(x, name, after=()):
    R, C = x.shape
    n_after = len(after)

    def body(x_ref, *rest):
        out_ref, send_sems, recv_sems = rest[n_after:]
        x_, y_, c_ = _my_pos()
        me, sibling = (x_, y_, c_), (x_, y_, 1 - c_)
        chips = [(1 - x_, y_), (x_, 1 - y_), (1 - x_, 1 - y_)]

        def slot(px, py, pc):
            return out_ref.at[4 * px + 2 * py + pc]

        def copy(k, block, to):
            return pltpu.make_async_remote_copy(
                src_ref=slot(*block), dst_ref=slot(*block), send_sem=send_sems.at[k], recv_sem=recv_sems.at[k],
                device_id=to, device_id_type=MESH)

        out_ref[4 * x_ + 2 * y_ + c_] = x_ref[...]
        first = [copy(0, me, sibling)] + [copy(1 + j, me, (*chip, c_)) for j, chip in enumerate(chips)]
        for cp in first:
            cp.start()
        passed = [copy(4 + j, (*chip, c_), sibling) for j, chip in enumerate(chips)]
        for j, chip in enumerate(chips):
            copy(1 + j, (*chip, c_), me).wait_recv()
            passed[j].start()
        copy(0, sibling, me).wait_recv()
        for j, chip in enumerate(chips):
            copy(4 + j, (*chip, 1 - c_), me).wait_recv()
        for cp in first + passed:
            cp.wait_send()

    return pl.pallas_call(
        body, name=name,
        in_specs=[pl.BlockSpec(memory_space=pltpu.VMEM)] + [pl.BlockSpec(memory_space=pl.ANY)] * n_after,
        out_specs=pl.BlockSpec(memory_space=pltpu.VMEM),
        out_shape=jax.ShapeDtypeStruct((N_DEV, R, C), x.dtype),
        scratch_shapes=[pltpu.SemaphoreType.DMA((7,)), pltpu.SemaphoreType.DMA((7,))],
        compiler_params=pltpu.CompilerParams(vmem_limit_bytes=48 * MIB),
    )(x, *after)


HBM_SPEC = pl.BlockSpec(memory_space=pltpu.HBM)
SEM_SPEC = pl.BlockSpec(memory_space=pltpu.SEMAPHORE)
ANY_SPEC = pl.BlockSpec(memory_space=pl.ANY)
SPLIT_EFFECT = pltpu.SideEffectType.DATAFLOW_SIDE_EFFECTING


def _in_hbm(a):
    return pltpu.with_memory_space_constraint(a, pltpu.HBM)


def _gathered_shape(a, kind):
    if kind == "major":
        return (N_DEV,) + a.shape
    if kind == "rows":
        return (N_DEV * a.shape[0], a.shape[1])
    return (2, a.shape[0], 4 * a.shape[1])


def _gather_slot(ref, kind, block, shard_shape):
    px, py, pc = block
    if kind == "major":
        return ref.at[4 * px + 2 * py + pc]
    if kind == "rows":
        r = shard_shape[0]
        return ref.at[pl.ds(pl.multiple_of((4 * px + 2 * py + pc) * r, r), r), :]
    cu = shard_shape[1]
    return ref.at[px, :, pl.ds(pl.multiple_of((2 * py + pc) * cu, cu), cu)]


def _gather_peers():
    x_, y_, c_ = _my_pos()
    return (x_, y_, c_), (x_, y_, 1 - c_), [(1 - x_, y_), (x_, 1 - y_), (1 - x_, 1 - y_)]


def _gather_start(shards, kinds, after, name):
    n = len(shards)
    bufs = [lax.empty(_gathered_shape(a, k), a.dtype) for a, k in zip(shards, kinds)]
    extra = [] if after is None else [after]

    def body(*refs):
        shard_refs, buf_refs = refs[:n], refs[n:2 * n]
        send_sems, recv_sems, local_sems = refs[2 * n + len(extra):2 * n + len(extra) + 3]
        token = refs[-1]
        me, sibling, chips = _gather_peers()
        for e in range(n):
            mine = _gather_slot(buf_refs[e], kinds[e], me, shards[e].shape)
            pltpu.make_async_copy(shard_refs[e], mine, local_sems.at[e]).start()
            for k, to in enumerate([sibling] + [(*chip, me[2]) for chip in chips]):
                pltpu.make_async_remote_copy(
                    src_ref=shard_refs[e], dst_ref=mine, send_sem=send_sems.at[4 * e + k],
                    recv_sem=recv_sems.at[4 * e + k], device_id=to, device_id_type=MESH).start()
        token[...] = jnp.zeros_like(token)

    out = pl.pallas_call(
        body, name=name,
        out_shape=(pltpu.SemaphoreType.DMA((4 * n,)), pltpu.SemaphoreType.DMA((4 * n,)), pltpu.SemaphoreType.DMA((n,)),
                   *[pltpu.HBM(a.shape, a.dtype) for a in shards], *[pltpu.HBM(a.shape, a.dtype) for a in bufs],
                   jax.ShapeDtypeStruct((8, LANE), F32)),
        in_specs=[HBM_SPEC] * (2 * n) + [ANY_SPEC] * len(extra),
        out_specs=(SEM_SPEC, SEM_SPEC, SEM_SPEC, *[HBM_SPEC] * (2 * n), pl.BlockSpec(memory_space=pltpu.VMEM)),
        input_output_aliases={i: 3 + i for i in range(2 * n)},
        compiler_params=pltpu.CompilerParams(has_side_effects=SPLIT_EFFECT),
    )(*[_in_hbm(a) for a in shards], *[_in_hbm(a) for a in bufs], *extra)
    return out[0], out[1], out[2], out[3:3 + n], out[3 + n:3 + 2 * n], out[-1]


def _gather_forward(recv_sems, bufs, kinds, shard_shapes, after, name):
    n = len(bufs)

    def body(*refs):
        buf_refs, recv_in = refs[:n], refs[n]
        fsend, frecv = refs[n + 2], refs[n + 3]
        token = refs[-1]
        me, sibling, chips = _gather_peers()
        for e in range(n):
            for j, chip in enumerate(chips):
                slot = _gather_slot(buf_refs[e], kinds[e], (*chip, me[2]), shard_shapes[e])
                pltpu.make_async_remote_copy(
                    src_ref=slot, dst_ref=slot, send_sem=recv_in.at[4 * e + 1 + j], recv_sem=recv_in.at[4 * e + 1 + j],
                    device_id=me, device_id_type=MESH).wait_recv()
                pltpu.make_async_remote_copy(
                    src_ref=slot, dst_ref=slot, send_sem=fsend.at[3 * e + j], recv_sem=frecv.at[3 * e + j],
                    device_id=sibling, device_id_type=MESH).start()
        token[...] = jnp.zeros_like(token)

    out = pl.pallas_call(
        body, name=name,
        out_shape=(pltpu.SemaphoreType.DMA((3 * n,)), pltpu.SemaphoreType.DMA((3 * n,)),
                   *[pltpu.HBM(a.shape, a.dtype) for a in bufs], jax.ShapeDtypeStruct((8, LANE), F32)),
        in_specs=[HBM_SPEC] * n + [SEM_SPEC, ANY_SPEC],
        out_specs=(SEM_SPEC, SEM_SPEC, *[HBM_SPEC] * n, pl.BlockSpec(memory_space=pltpu.VMEM)),
        input_output_aliases={i: 2 + i for i in range(n)},
        compiler_params=pltpu.CompilerParams(has_side_effects=SPLIT_EFFECT),
    )(*bufs, recv_sems, after)
    return out[0], out[1], out[2:2 + n], out[-1]


def _gather_wait(send_sems, recv_sems, local_sems, fsend, frecv, shards, bufs, kinds, after, name):
    n = len(bufs)

    def body(*refs):
        shard_refs, buf_refs = refs[:n], refs[n:2 * n]
        send_in, recv_in, local_in, fsend_in, frecv_in = refs[2 * n:2 * n + 5]
        me, sibling, chips = _gather_peers()

        def arrival(slot, sem):
            return pltpu.make_async_remote_copy(src_ref=slot, dst_ref=slot, send_sem=sem, recv_sem=sem,
                                                device_id=me, device_id_type=MESH)

        for e in range(n):
            shape = shards[e].shape
            mine = _gather_slot(buf_refs[e], kinds[e], me, shape)
            pltpu.make_async_copy(shard_refs[e], mine, local_in.at[e]).wait()
            arrival(_gather_slot(buf_refs[e], kinds[e], sibling, shape), recv_in.at[4 * e]).wait_recv()
            for j, chip in enumerate(chips):
                arrival(_gather_slot(buf_refs[e], kinds[e], (*chip, 1 - me[2]), shape), frecv_in.at[3 * e + j]).wait_recv()
            for k in range(4):
                arrival(mine, send_in.at[4 * e + k]).wait_send()
            for j in range(3):
                arrival(mine, fsend_in.at[3 * e + j]).wait_send()

    out = pl.pallas_call(
        body, name=name,
        out_shape=(*[pltpu.HBM(a.shape, a.dtype) for a in shards], *[pltpu.HBM(a.shape, a.dtype) for a in bufs]),
        in_specs=[HBM_SPEC] * (2 * n) + [SEM_SPEC] * 5 + [ANY_SPEC],
        out_specs=tuple([HBM_SPEC] * (2 * n)),
        input_output_aliases={i: i for i in range(2 * n)},
        compiler_params=pltpu.CompilerParams(has_side_effects=SPLIT_EFFECT),
    )(*shards, *bufs, send_sems, recv_sems, local_sems, fsend, frecv, after)
    return out[n:]


def _grad_slice(ref, kind, j):
    if kind == "whole":
        return ref
    if kind == "major":
        return ref.at[j]
    if kind == "rows":
        r = ref.shape[0] // N_DEV
        return ref.at[pl.ds(j * r, r), :]
    cu = ref.shape[2] // 4
    return ref.at[j // 4, :, pl.ds((j % 4) * cu, cu)]


def _slice_shape(a, kind):
    if kind == "whole":
        return a.shape
    if kind == "major":
        return a.shape[1:]
    if kind == "rows":
        return (a.shape[0] // N_DEV, a.shape[1])
    return (a.shape[1], a.shape[2] // 4)


def _scatter_copies(srcs, lands, kinds, send_sems, recv_sems):
    x_, y_, c_ = _my_pos()
    me = 4 * x_ + 2 * y_ + c_
    n = len(srcs)

    def remote(e, j):
        return pltpu.make_async_remote_copy(
            src_ref=_grad_slice(srcs[e], kinds[e], j), dst_ref=lands[e].at[me],
            send_sem=send_sems.at[e * N_DEV + j], recv_sem=recv_sems.at[e * N_DEV + me],
            device_id=(j // 4, (j // 2) % 2, j % 2), device_id_type=MESH)

    def local(e, j):
        return pltpu.make_async_copy(_grad_slice(srcs[e], kinds[e], j), lands[e].at[j], recv_sems.at[e * N_DEV + j])

    def arrival(e, i):
        return pltpu.make_async_remote_copy(
            src_ref=_grad_slice(srcs[e], kinds[e], i), dst_ref=lands[e].at[i],
            send_sem=send_sems.at[e * N_DEV + i], recv_sem=recv_sems.at[e * N_DEV + i],
            device_id=(i // 4, (i // 2) % 2, i % 2), device_id_type=MESH)

    def start():
        for e in range(n):
            for j in range(N_DEV):
                @pl.when(me == j)
                def _():
                    local(e, j).start()

                @pl.when(me != j)
                def _():
                    remote(e, j).start()

    def wait():
        for e in range(n):
            for i in range(N_DEV):
                @pl.when(me == i)
                def _():
                    local(e, i).wait()

                @pl.when(me != i)
                def _():
                    arrival(e, i).wait_recv()
        for e in range(n):
            for j in range(N_DEV):
                @pl.when(me != j)
                def _():
                    remote(e, j).wait_send()

    return start, wait


def _scatter_start(srcs, kinds, after, name):
    n = len(srcs)
    lands = [lax.empty((N_DEV,) + _slice_shape(a, k), a.dtype) for a, k in zip(srcs, kinds)]
    extra = [] if after is None else [after]

    def body(*refs):
        src_refs, land_refs = refs[:n], refs[n:2 * n]
        send_sems, recv_sems = refs[2 * n + len(extra)], refs[2 * n + len(extra) + 1]
        token = refs[-1]
        start, _ = _scatter_copies(src_refs, land_refs, kinds, send_sems, recv_sems)
        start()
        token[...] = jnp.zeros_like(token)

    out = pl.pallas_call(
        body, name=name,
        out_shape=(pltpu.SemaphoreType.DMA((n * N_DEV,)), pltpu.SemaphoreType.DMA((n * N_DEV,)),
                   *[pltpu.HBM(a.shape, a.dtype) for a in srcs], *[pltpu.HBM(a.shape, a.dtype) for a in lands],
                   jax.ShapeDtypeStruct((8, LANE), F32)),
        in_specs=[HBM_SPEC] * (2 * n) + [ANY_SPEC] * len(extra),
        out_specs=(SEM_SPEC, SEM_SPEC, *[HBM_SPEC] * (2 * n), pl.BlockSpec(memory_space=pltpu.VMEM)),
        input_output_aliases={i: 2 + i for i in range(2 * n)},
        compiler_params=pltpu.CompilerParams(has_side_effects=SPLIT_EFFECT),
    )(*[_in_hbm(a) for a in srcs], *[_in_hbm(a) for a in lands], *extra)
    return out[0], out[1], out[2:2 + n], out[2 + n:2 + 2 * n], out[-1]


def _scatter_wait(send_sems, recv_sems, srcs, lands, kinds, after, name):
    n = len(srcs)

    def body(*refs):
        src_refs, land_refs = refs[:n], refs[n:2 * n]
        _, wait = _scatter_copies(src_refs, land_refs, kinds, refs[2 * n], refs[2 * n + 1])
        wait()

    out = pl.pallas_call(
        body, name=name,
        out_shape=(*[pltpu.HBM(a.shape, a.dtype) for a in srcs], *[pltpu.HBM(a.shape, a.dtype) for a in lands]),
        in_specs=[HBM_SPEC] * (2 * n) + [SEM_SPEC, SEM_SPEC] + [ANY_SPEC] * len(after),
        out_specs=tuple([HBM_SPEC] * (2 * n)),
        input_output_aliases={i: i for i in range(2 * n)},
        compiler_params=pltpu.CompilerParams(has_side_effects=SPLIT_EFFECT),
    )(*srcs, *lands, send_sems, recv_sems, *after)
    return out[n:]


def _rope_tables(positions, sign):
    half = ROPE_DIM // 2
    inv_freq = ROPE_THETA ** (-jnp.arange(0, ROPE_DIM, 2, dtype=F32) / ROPE_DIM)
    ang = positions.astype(F32)[:, None] * inv_freq
    reps = LANE // half
    cos = jnp.tile(jnp.cos(ang), (1, reps))
    sin = jnp.tile(jnp.sin(ang), (1, reps)) * sign
    d = jnp.arange(LANE) % SWA_HEAD_DIM
    return (jnp.where(d < ROPE_DIM, cos, 1.0), jnp.where(d < half, -sin, 0.0),
            jnp.where((d >= half) & (d < ROPE_DIM), sin, 0.0))


def _pad_cols(a, n):
    return jnp.pad(a, ((0, 0), (0, n - a.shape[1])))


def _local_step(x, target, positions, mods, fetch, P, on_grads, on_small):
    S, D = x.shape
    Hf = D // FOX_HEAD_DIM
    Hq = D // SWA_HEAD_DIM
    Hk = Hq // SWA_GROUP
    Wk = Hk * SWA_HEAD_DIM
    n_in = 3 * D + Hf
    (sh1a, sc1a, g1a, sh2a, sc2a, g2a), (sh1b, sc1b, g1b, sh2b, sc2b, g2b) = mods
    row = lambda v: v.reshape(1, -1)
    cw = [jnp.transpose(P["conv_w"][l].reshape(3, 2, -1), (1, 0, 2)) for l in range(2)]
    cb = [P["conv_b"][l].reshape(2, 1, -1) for l in range(2)]

    W = dict(up=[None, None], down=[None, None])
    h1a = _modulate(x, sc1a, sh1a, "modulate_in")
    W["fox_in"], W["fox_o"] = fetch("fox", "wait", h1a)
    proj_a = _mm_nn(h1a, W["fox_in"], F32, "fox_in_proj", tn=896)
    flT = proj_a[:, 3 * D:n_in].T
    bf_col = P["fox_b_f"].reshape(Hf, 1)
    cumT = _fox_prep(flT, bf_col, "fox_cumsum")
    cum_row = cumT.reshape(Hf, 1, S)
    o_a, lse_a = _fox_attn_fwd(proj_a, cum_row, Hf, "fox_attn_fwd")
    token = fetch("ffn0", "forward", o_a)
    y1a = _mm_nn(o_a, W["fox_o"], F32, "fox_out_proj")
    z1a, x1, h2a = _ln_fwd(x, y1a, g1a, row(P["ln_mix_g"][0]), row(P["ln_mix_b"][0]), sc2a, sh2a, token, "ln_mix0")
    W["up"][0], W["down"][0] = fetch("ffn0", "wait", h2a)
    u_a, uc_a, a_a = _ffn_up(h2a, W["up"][0], cw[0], cb[0], "ffn_up0")
    token = fetch("swa", "forward", a_a)
    y2a = _mm_nn(a_a, W["down"][0], F32, "ffn_down0", tk=2816)
    z2a, x2, h1b = _ln_fwd(x1, y2a, g2a, row(P["ln_ffn_g"][0]), row(P["ln_ffn_b"][0]), sc1b, sh1b, token, "ln_ffn0")

    W["swa_in"], W["swa_o"] = fetch("swa", "wait", h1b)
    proj_b = _mm_nn(h1b, W["swa_in"], F32, "swa_in_proj")
    tabs_f = _rope_tables(positions, 1.0)
    tabs_b = _rope_tables(positions, -1.0)
    qk = _rope([(proj_b, True, D + Wk)], tabs_f, F32, "rope_fwd")
    sinks = _pad_cols(P["swa_sinks"].reshape(1, Hq), LANE)
    o_b, lse_b = _swa_attn_fwd(qk, proj_b, sinks, Hq, "swa_attn_fwd")
    token = fetch("ffn1", "forward", o_b)
    y1b = _mm_nn(o_b, W["swa_o"], F32, "swa_out_proj")
    z1b, x3, h2b = _ln_fwd(x2, y1b, g1b, row(P["ln_mix_g"][1]), row(P["ln_mix_b"][1]), sc2b, sh2b, token, "ln_mix1")
    W["up"][1], W["down"][1] = fetch("ffn1", "wait", h2b)
    u_b, uc_b, a_b = _ffn_up(h2b, W["up"][1], cw[1], cb[1], "ffn_up1")
    y2b = _mm_nn(a_b, W["down"][1], F32, "ffn_down1", tk=2816)
    z2b, dout, loss_row = _ln_fwd_loss(x3, y2b, g2b, row(P["ln_ffn_g"][1]), row(P["ln_ffn_b"][1]), target, "ln_ffn1_loss")

    def ffn_backward(dy, a, u, uc, h_in, l, tag):
        d_down = _mm_tn(a, dy[None], BF16, "ffn_dwdown" + tag, tm=1408, tn=1024)[0]
        token = on_grads("ffn_w_down" + tag, d_down)
        da = _mm_nt(dy[None], W["down"][l][None], F32, "ffn_da" + tag)
        du, dcw, dcb = _ffn_bwd_elem(da, u, uc, cw[l], token, "ffn_bwd_elem" + tag)
        d_up = _mm_tn(h_in, du, BF16, "ffn_dwup" + tag)
        token = on_grads("ffn_w_up" + tag, d_up)
        dh = _mm_nt(du, W["up"][l], F32, "ffn_dh" + tag, tn=512, tk=2816)
        return dh, token, jnp.transpose(dcw, (1, 0, 2)).reshape(3, -1), dcb.reshape(-1)

    dz2b, dy2b, dg_f1, db_f1, dgate2b = _ln_bwd(dout, z2b, y2b, g2b, row(P["ln_ffn_g"][1]), "ln_ffn1_bwd")
    dh2b, token, dcw1, dcb1 = ffn_backward(dy2b, a_b, u_b, uc_b, h2b, 1, "1")
    dx3, dsc2b, dsh2b = _mod_bwd(dz2b, dh2b, x3, sc2b, token, "mod_ffn1_bwd")

    dz1b, dy1b, dg_m1, db_m1, dgate1b = _ln_bwd(dx3, z1b, y1b, g1b, row(P["ln_mix_g"][1]), "ln_mix1_bwd")
    token = on_grads("swa_w_o", _mm_tn(o_b, dy1b[None], BF16, "swa_dwo")[0])
    do_b = _mm_nt(dy1b[None], W["swa_o"][None], F32, "swa_do")
    dq_b, dk_b, dv_b, dsinks = _swa_attn_bwd(qk, proj_b, sinks, do_b, lse_b, Hq, token, "swa_attn_bwd")
    dproj_b = _rope([(dq_b, True, D), (dk_b, True, Wk), (dv_b, False, Wk)], tabs_b, BF16, "rope_bwd")
    token = on_grads("swa_w_in", _mm_tn(dproj_b, h1b[None], BF16, "swa_dwin", tm=1280, tn=1024)[0])
    dh1b = _mm_nt(dproj_b[None], W["swa_in"][None], F32, "swa_dh", tk=1280)
    dx2, dsc1b, dsh1b = _mod_bwd(dz1b, dh1b, x2, sc1b, token, "mod_mix1_bwd")

    dz2a, dy2a, dg_f0, db_f0, dgate2a = _ln_bwd(dx2, z2a, y2a, g2a, row(P["ln_ffn_g"][0]), "ln_ffn0_bwd")
    dh2a, token, dcw0, dcb0 = ffn_backward(dy2a, a_a, u_a, uc_a, h2a, 0, "0")
    dx1, dsc2a, dsh2a = _mod_bwd(dz2a, dh2a, x1, sc2a, token, "mod_ffn0_bwd")

    dz1a, dy1a, dg_m0, db_m0, dgate1a = _ln_bwd(dx1, z1a, y1a, g1a, row(P["ln_mix_g"][0]), "ln_mix0_bwd")
    token = on_grads("fox_w_o", _mm_tn(o_a, dy1a[None], BF16, "fox_dwo")[0])
    do_a = _mm_nt(dy1a[None], W["fox_o"][None], F32, "fox_do")
    dq_a, dk_a, dv_a, dcum_row, dcum_col = _fox_attn_bwd(proj_a, do_a, o_a, cum_row, lse_a, Hf, token, "fox_attn_bwd")
    dflT, dbf = _fox_prep_bwd(dcum_row.reshape(Hf, S), dcum_col.reshape(Hf, S), flT, bf_col, "fox_cumsum_bwd")
    n_pad = W["fox_in"].shape[1]
    dproj_a = jnp.concatenate([dq_a, dk_a.astype(BF16), dv_a.astype(BF16),
                               _pad_cols(dflT.T, n_pad - 3 * D).astype(BF16)], axis=1)
    dh1a = _mm_nt(dproj_a[None], W["fox_in"][None], F32, "fox_dh", tn=1024, tk=896)
    grad_x, dsc1a, dsh1a = _mod_bwd(dz1a, dh1a, x, sc1a, token, "mod_mix0_bwd")

    dmod = jnp.stack([jnp.concatenate([dsh1a, dsc1a, dgate1a, dsh2a, dsc2a, dgate2a], axis=1)[0],
                      jnp.concatenate([dsh1b, dsc1b, dgate1b, dsh2b, dsc2b, dgate2b], axis=1)[0]])
    small = dict(dmod=dmod, conv_b=jnp.stack([dcb0, dcb1]), conv_w=jnp.stack([dcw0, dcw1]),
                 ln_mix_g=jnp.concatenate([dg_m0, dg_m1]), ln_mix_b=jnp.concatenate([db_m0, db_m1]),
                 ln_ffn_g=jnp.concatenate([dg_f0, dg_f1]), ln_ffn_b=jnp.concatenate([db_f0, db_f1]),
                 fox_b_f=dbf.reshape(-1), swa_sinks=dsinks[0, :Hq], loss=loss_row[0, 0].reshape(1))
    exchanged = on_small(small)
    on_grads("fox_w_in", _mm_tn(dproj_a, h1a[None], BF16, "fox_dwin", tm=896, tn=1024)[0], exchanged)
    return grad_x


SMALL_ORDER = ("dmod", "conv_b", "conv_w", "ln_mix_g", "ln_mix_b", "ln_ffn_g", "ln_ffn_b", "fox_b_f", "swa_sinks", "loss")


def _pack_rows(arrays):
    chunks, spans, off = [], [], 0
    for a in arrays:
        flat = a.reshape(-1)
        n = -(-flat.shape[0] // LANE) * LANE
        chunks.append(jnp.pad(flat, (0, n - flat.shape[0])))
        spans.append((off, flat.shape[0], a.shape))
        off += n
    total = -(-off // (8 * LANE)) * (8 * LANE)
    chunks.append(jnp.zeros((total - off,), F32))
    return jnp.concatenate(chunks).reshape(-1, LANE), spans


def _unpack_rows(packed, spans):
    flat = packed.reshape(-1)
    return [flat[off:off + n].reshape(shape) for off, n, shape in spans]


def kernel(x, c, positions, fox_w_in, fox_b_f, fox_w_o, swa_w_in, swa_sinks, swa_w_o, ada_w, ada_b, ffn_w_up, ffn_conv_w, ffn_conv_b, ffn_w_down, ln_mix_g, ln_mix_b, ln_ffn_g, ln_ffn_b, loss_target, m_fox_w_in, m_fox_b_f, m_fox_w_o, m_swa_w_in, m_swa_sinks, m_swa_w_o, m_ada_w, m_ada_b, m_ffn_w_up, m_ffn_conv_w, m_ffn_conv_b, m_ffn_w_down, m_ln_mix_g, m_ln_mix_b, m_ln_ffn_g, m_ln_ffn_b, v_fox_w_in, v_fox_b_f, v_fox_w_o, v_swa_w_in, v_swa_sinks, v_swa_w_o, v_ada_w, v_ada_b, v_ffn_w_up, v_ffn_conv_w, v_ffn_conv_b, v_ffn_w_down, v_ln_mix_g, v_ln_mix_b, v_ln_ffn_g, v_ln_ffn_b):
    S, D = x.shape[1], x.shape[2]
    L = ada_w.shape[0]
    me = 4 * lax.axis_index("x") + 2 * lax.axis_index("y") + lax.axis_index("c")
    n_ada = ada_w.shape[2]
    cu = ffn_w_up.shape[2]
    F = 4 * cu
    n_in = fox_w_in.shape[2] * N_DEV
    n_in_pad = -(-n_in // LANE) * LANE

    gather_groups = dict(
        fox=([fox_w_in[0].astype(BF16), fox_w_o[0].astype(BF16)], ["major", "rows"]),
        ffn0=([ffn_w_up[0].astype(BF16), ffn_w_down[0].astype(BF16)], ["halves", "rows"]),
        swa=([swa_w_in[0].astype(BF16), swa_w_o[0].astype(BF16)], ["major", "rows"]),
        ffn1=([ffn_w_up[1].astype(BF16), ffn_w_down[1].astype(BF16)], ["halves", "rows"]))
    starts_after = dict(fox=["ffn0", "swa"], ffn0=["ffn1"])
    gathers = {}

    def start_group(group, after):
        shards, kinds = gather_groups[group]
        send, recv, local, thru, bufs, token = _gather_start(shards, kinds, after, "gather_start_" + group)
        gathers[group] = dict(send=send, recv=recv, local=local, shards=thru, bufs=bufs, kinds=kinds,
                              shapes=[a.shape for a in shards], token=token)
        return token

    fox_started = start_group("fox", None)

    c_all = _all_gather_small(c.reshape(-1, LANE), "gather_c", after=[fox_started]).reshape(N_DEV, D)
    b_cols = lax.dynamic_slice_in_dim(ada_b, me * n_ada, n_ada, axis=1).reshape(L, 1, n_ada)
    mod_blk = _ada_mod(jnp.pad(c_all, ((0, ADA_ROWS - N_DEV), (0, 0))), ada_w, b_cols, "ada_mod")[:, :N_DEV]
    mod_all = _all_gather_small(mod_blk.reshape(-1, LANE), "gather_mod").reshape(N_DEV, L, N_DEV, n_ada)
    mod_mine = lax.dynamic_index_in_dim(mod_all, me, axis=2, keepdims=False)
    mod_mine = jnp.transpose(mod_mine, (1, 0, 2)).reshape(L, N_DEV * n_ada)
    mods = [[mod_mine[l, k * D:(k + 1) * D].reshape(1, D) for k in range(6)] for l in range(L)]

    P = dict(fox_b_f=fox_b_f[0], swa_sinks=swa_sinks[0], conv_b=ffn_conv_b,
             ln_mix_g=ln_mix_g, ln_mix_b=ln_mix_b, ln_ffn_g=ln_ffn_g, ln_ffn_b=ln_ffn_b)
    cw_rows = _all_gather_small(_pack_rows([ffn_conv_w])[0], "gather_conv_w", after=[mod_all])
    n_cw = ffn_conv_w.size
    cw_dev = cw_rows.reshape(N_DEV, -1)[:, :n_cw].reshape(N_DEV, L, 3, cu)
    P["conv_w"] = jnp.transpose(cw_dev, (1, 2, 0, 3)).reshape(L, 3, N_DEV * cu)

    def natural(g, pad_to=None):
        slabs = [g[k] for k in range(N_DEV)]
        if pad_to is not None:
            slabs.append(jnp.zeros((D, pad_to - N_DEV * g.shape[2]), g.dtype))
        return jnp.concatenate(slabs, axis=1)

    def forward_stage(group, after):
        s = gathers[group]
        s["fsend"], s["frecv"], s["bufs"], token = _gather_forward(s["recv"], s["bufs"], s["kinds"], s["shapes"], after,
                                                                   "gather_forward_" + group)
        for nxt in starts_after.get(group, ()):
            token = start_group(nxt, token)
        return token

    def fetch(group, stage, after):
        if stage == "forward":
            return forward_stage(group, after)
        if group == "fox":
            after = forward_stage(group, cw_rows)
        s = gathers.pop(group)
        first, second = _gather_wait(s["send"], s["recv"], s["local"], s["fsend"], s["frecv"], s["shards"], s["bufs"],
                                     s["kinds"], after, "gather_wait_" + group)
        if group == "fox":
            return natural(first, n_in_pad), second
        if group == "swa":
            return natural(first), second
        return first, second

    out, pending = {}, {}

    def columns_major(g_t, n):
        return g_t[:n].reshape(N_DEV, n // N_DEV, D)

    def transposed(a):
        return jnp.transpose(a, (0, 2, 1))

    big = dict(
        ffn_w_down1=("ffn_w_down", "rows", 1, (ffn_w_down, m_ffn_w_down, v_ffn_w_down)),
        ffn_w_up1=("ffn_w_up", "halves", 1, (ffn_w_up, m_ffn_w_up, v_ffn_w_up)),
        swa_w_o=("swa_w_o", "rows", 0, (swa_w_o, m_swa_w_o, v_swa_w_o)),
        swa_w_in=("swa_w_in", "major", 0, tuple(transposed(a) for a in (swa_w_in, m_swa_w_in, v_swa_w_in))),
        ffn_w_down0=("ffn_w_down", "rows", 0, (ffn_w_down, m_ffn_w_down, v_ffn_w_down)),
        ffn_w_up0=("ffn_w_up", "halves", 0, (ffn_w_up, m_ffn_w_up, v_ffn_w_up)),
        fox_w_o=("fox_w_o", "rows", 0, (fox_w_o, m_fox_w_o, v_fox_w_o)),
        fox_w_in=("fox_w_in", "major", 0, (fox_w_in, m_fox_w_in, v_fox_w_in)))
    finish_at = dict(swa_w_o=["ffn_w_down1"], ffn_w_up0=["ffn_w_up1", "swa_w_o", "swa_w_in"], fox_w_o=["ffn_w_down0"],
                     fox_w_in=["ffn_w_up0"])
    tail = {}

    def finish(name, after):
        send, recv, thru, lands = pending.pop(name)
        param, kind, layer, wmv = big[name]
        if name == "fox_w_in":
            one = tail["last_start"][0, 0] + 1.0
            wmv = tuple(transposed(a * one) for a in wmv)
        landed, = _scatter_wait(send, recv, thru, lands, [kind], after, "scatter_wait_" + name)
        res = _adam_sum(landed, *wmv, layer, out.get(param), "adam_" + name, by_cols=kind == "major")
        out[param] = [transposed(r) for r in res] if kind == "major" else res

    def on_grads(name, g, after=None):
        kind = big[name][1]
        src = columns_major(g, n_in if name == "fox_w_in" else g.shape[0]) if kind == "major" else g
        send, recv, thru, lands, token = _scatter_start([src], [kind], after, "scatter_start_" + name)
        pending[name] = (send, recv, thru, lands)
        for done in finish_at.get(name, ()):
            finish(done, [token])
        tail["last_start"] = token
        return token

    def on_small(small):
        packed, tail["spans"] = _pack_rows([small[k] for k in SMALL_ORDER])
        send, recv, thru, lands, token = _scatter_start([packed], ["whole"], None, "small_grads_start")
        tail["small"] = (send, recv, thru, lands)
        return token

    grad_x = _local_step(x[0], loss_target[0], positions[0], mods, fetch, P, on_grads, on_small)

    spans = tail["spans"]
    done_first = [out[k][1] for k in ("ffn_w_up", "ffn_w_down", "swa_w_in", "swa_w_o")]
    gathered, = _scatter_wait(*tail["small"], ["whole"], done_first, "small_grads_wait")
    totals = dict(zip(SMALL_ORDER, _unpack_rows(_sum_slots(gathered, tail["last_start"], "sum_small_grads"), spans)))
    loss = totals["loss"].reshape(())
    n_mod = L * 6 * D
    dmod_all = gathered.reshape(N_DEV, -1)[:, :n_mod].reshape(N_DEV, L, 6 * D)
    dmod_cols = jnp.transpose(lax.dynamic_slice_in_dim(dmod_all, me * n_ada, n_ada, axis=2), (1, 0, 2))
    out["ada_w"] = _ada_bwd_adam(c_all.T, dmod_cols, ada_w, m_ada_w, v_ada_w, "adam_ada_w")

    g_small = dict(fox_b_f=totals["fox_b_f"].reshape(fox_b_f.shape), swa_sinks=totals["swa_sinks"].reshape(swa_sinks.shape),
                   ada_b=totals["dmod"].reshape(ada_b.shape), ffn_conv_b=totals["conv_b"].reshape(ffn_conv_b.shape),
                   ffn_conv_w=lax.dynamic_slice_in_dim(totals["conv_w"].reshape(L, 3, 2 * F), me * cu, cu, axis=2),
                   ln_mix_g=totals["ln_mix_g"], ln_mix_b=totals["ln_mix_b"],
                   ln_ffn_g=totals["ln_ffn_g"], ln_ffn_b=totals["ln_ffn_b"])
    small_names = ("fox_b_f", "swa_sinks", "ada_b", "ffn_conv_b", "ffn_conv_w", "ln_mix_g", "ln_mix_b", "ln_ffn_g", "ln_ffn_b")
    w_small = dict(fox_b_f=(fox_b_f, m_fox_b_f, v_fox_b_f), swa_sinks=(swa_sinks, m_swa_sinks, v_swa_sinks),
                   ada_b=(ada_b, m_ada_b, v_ada_b), ffn_conv_b=(ffn_conv_b, m_ffn_conv_b, v_ffn_conv_b),
                   ffn_conv_w=(ffn_conv_w, m_ffn_conv_w, v_ffn_conv_w),
                   ln_mix_g=(ln_mix_g, m_ln_mix_g, v_ln_mix_g), ln_mix_b=(ln_mix_b, m_ln_mix_b, v_ln_mix_b),
                   ln_ffn_g=(ln_ffn_g, m_ln_ffn_g, v_ln_ffn_g), ln_ffn_b=(ln_ffn_b, m_ln_ffn_b, v_ln_ffn_b))
    pk_g, sp = _pack_rows([g_small[k] for k in small_names])
    pk_w = _pack_rows([w_small[k][0] for k in small_names])[0]
    pk_m = _pack_rows([w_small[k][1] for k in small_names])[0]
    pk_v = _pack_rows([w_small[k][2] for k in small_names])[0]
    res = _adam(pk_g[None], pk_w[None], pk_m[None], pk_v[None], "adam_small")
    settled = [res[0], out["ada_w"][1]] + [out[k][1] for k in ("ffn_w_up", "ffn_w_down", "swa_w_in", "swa_w_o")]
    finish("fox_w_o", settled)
    finish("fox_w_in", settled)
    res = [dict(zip(small_names, _unpack_rows(r[0], sp))) for r in res]
    for k in small_names:
        out[k] = (g_small[k], res[0][k], res[1][k], res[2][k])

    order = ("fox_w_in", "fox_b_f", "fox_w_o", "swa_w_in", "swa_sinks", "swa_w_o", "ada_w", "ada_b", "ffn_w_up",
             "ffn_conv_w", "ffn_conv_b", "ffn_w_down", "ln_mix_g", "ln_mix_b", "ln_ffn_g", "ln_ffn_b")
    return (loss, grad_x[None], *[out[k][0] for k in order], *[out[k][1] for k in order],
            *[out[k][2] for k in order], *[out[k][3] for k in order])
```

```python
import functools

import jax
import jax.numpy as jnp
from jax import lax
from jax.experimental import pallas as pl
from jax.experimental.pallas import tpu as pltpu

F32 = jnp.float32
BF16 = jnp.bfloat16
MESH = pl.DeviceIdType.MESH
N_DEV = 8
AXES = ("x", "y", "c")

DEPTH = 2
ALPHA = (2.0 * DEPTH) ** 0.25
LN_EPS = 1e-5
FOX_HEAD_DIM = 128
SWA_HEAD_DIM = 64
SWA_GROUP = 8
SWA_WINDOW = 128
Q_BLOCK = 128
ROPE_DIM = 16
ROPE_THETA = 500000.0

ADAM_LR = 0.001
ADAM_B1 = 0.9
ADAM_B2 = 0.999
ADAM_EPS = 1e-08
ADAM_WD = 0.01
ADAM_STEP = 10

LANE = 128
MIB = 1024 * 1024


def _tile(n, pref, unit=LANE):
    if n <= pref:
        return n
    t = (pref // unit) * unit
    while t >= unit:
        if n % t == 0:
            return t
        t -= unit
    return n


def _params(sem, vmem_mib=48):
    return pltpu.CompilerParams(dimension_semantics=sem, vmem_limit_bytes=vmem_mib * MIB)


def _sigmoid(x):
    return 1.0 / (1.0 + jnp.exp(-x))


def _mm_call(dot, grid_mnk, in_specs, out_spec, out_shape, k_axis, nk, tm, tn, name, operands):
    sem = ("parallel",) * (len(grid_mnk) - 1) + ("arbitrary",)

    if nk == 1:
        def body(a_ref, b_ref, o_ref):
            o_ref[...] = dot(a_ref[...], b_ref[...]).astype(o_ref.dtype)
        scratch = []
    else:
        def body(a_ref, b_ref, o_ref, acc_ref):
            k = pl.program_id(k_axis)

            @pl.when(k == 0)
            def _():
                acc_ref[...] = jnp.zeros_like(acc_ref)

            acc_ref[...] += dot(a_ref[...], b_ref[...])

            @pl.when(k == nk - 1)
            def _():
                o_ref[...] = acc_ref[...].astype(o_ref.dtype)
        scratch = [pltpu.VMEM((tm, tn), F32)]

    return pl.pallas_call(
        body, name=name, grid=grid_mnk, in_specs=in_specs, out_specs=out_spec, out_shape=out_shape,
        scratch_shapes=scratch, compiler_params=_params(sem, 56),
    )(*operands)


def _dot(dims):
    def dot(a, b):
        return lax.dot_general(a.astype(BF16), b.astype(BF16), (dims, ((), ())), preferred_element_type=F32)
    return dot


def _mm_nn(a, b, out_dtype, name, tm=2048, tn=512, tk=2048):
    M, K = a.shape
    N = b.shape[1]
    tm, tn, tk = _tile(M, tm), _tile(N, tn), _tile(K, tk)
    nk = K // tk
    return _mm_call(
        _dot(((1,), (0,))), (M // tm, N // tn, nk),
        [pl.BlockSpec((tm, tk), lambda i, j, k: (i, k)), pl.BlockSpec((tk, tn), lambda i, j, k: (k, j))],
        pl.BlockSpec((tm, tn), lambda i, j, k: (i, j)), jax.ShapeDtypeStruct((M, N), out_dtype),
        2, nk, tm, tn, name, (a, b))


def _mm_nt(a, b, out_dtype, name, tm=2048, tn=512, tk=2048):
    P, M, K = a.shape
    N = b.shape[1]
    tm, tn, tk = _tile(M, tm), _tile(N, tn), _tile(K, tk)
    nk = K // tk
    return _mm_call(
        _dot(((1,), (1,))), (M // tm, N // tn, P * nk),
        [pl.BlockSpec((None, tm, tk), lambda i, j, k: (k // nk, i, k % nk)),
         pl.BlockSpec((None, tn, tk), lambda i, j, k: (k // nk, j, k % nk))],
        pl.BlockSpec((tm, tn), lambda i, j, k: (i, j)), jax.ShapeDtypeStruct((M, N), out_dtype),
        2, P * nk, tm, tn, name, (a, b))


def _mm_tn(a, b, out_dtype, name, tm=2048, tn=512, tk=2048):
    K, M = a.shape
    P, _, N = b.shape
    tm, tn, tk = _tile(M, tm), _tile(N, tn), _tile(K, tk)
    nk = K // tk
    return _mm_call(
        _dot(((0,), (0,))), (P, M // tm, N // tn, nk),
        [pl.BlockSpec((tk, tm), lambda p, i, j, k: (k, i)), pl.BlockSpec((None, tk, tn), lambda p, i, j, k: (p, k, j))],
        pl.BlockSpec((None, tm, tn), lambda p, i, j, k: (p, i, j)), jax.ShapeDtypeStruct((P, M, N), out_dtype),
        3, nk, tm, tn, name, (a, b))


ROW_TILE = 256


def _row_spec(tm, D):
    return pl.BlockSpec((tm, D), lambda i: (i, 0))


def _vec_spec(D):
    return pl.BlockSpec((1, D), lambda i: (0, 0))


def _modulate(x, sc, sh, name):
    S, D = x.shape
    tm = _tile(S, ROW_TILE, 8)

    def body(x_ref, sc_ref, sh_ref, h_ref):
        h_ref[...] = (x_ref[...] * (1.0 + sc_ref[...]) + sh_ref[...]).astype(BF16)

    return pl.pallas_call(
        body, name=name, grid=(S // tm,),
        in_specs=[_row_spec(tm, D), _vec_spec(D), _vec_spec(D)],
        out_specs=_row_spec(tm, D),
        out_shape=jax.ShapeDtypeStruct((S, D), BF16),
        compiler_params=_params(("parallel",)),
    )(x, sc, sh)


def _layer_norm_rows(z, gamma, beta):
    mu = jnp.mean(z, axis=-1, keepdims=True)
    zc = z - mu
    var = jnp.mean(zc * zc, axis=-1, keepdims=True)
    return zc * lax.rsqrt(var + LN_EPS) * gamma + beta


def _ln_fwd(x, y, gate, gamma, beta, sc_n, sh_n, after, name):
    S, D = x.shape
    tm = _tile(S, ROW_TILE, 8)

    def body(x_ref, y_ref, gate_ref, g_ref, b_ref, sc_ref, sh_ref, after_ref, z_ref, xo_ref, hn_ref):
        z = ALPHA * x_ref[...] + (1.0 + gate_ref[...]) * y_ref[...]
        xo = _layer_norm_rows(z, g_ref[...], b_ref[...])
        z_ref[...] = z
        xo_ref[...] = xo
        hn_ref[...] = (xo * (1.0 + sc_ref[...]) + sh_ref[...]).astype(BF16)

    return pl.pallas_call(
        body, name=name, grid=(S // tm,),
        in_specs=[_row_spec(tm, D), _row_spec(tm, D)] + [_vec_spec(D)] * 5 + [pl.BlockSpec(memory_space=pl.ANY)],
        out_specs=[_row_spec(tm, D)] * 3,
        out_shape=[jax.ShapeDtypeStruct((S, D), F32), jax.ShapeDtypeStruct((S, D), F32),
                   jax.ShapeDtypeStruct((S, D), BF16)],
        compiler_params=_params(("parallel",)),
    )(x, y, gate, gamma, beta, sc_n, sh_n, after)


def _ln_fwd_loss(x, y, gate, gamma, beta, target, name):
    S, D = x.shape
    tm = _tile(S, ROW_TILE, 8)

    def body(x_ref, y_ref, gate_ref, g_ref, b_ref, t_ref, z_ref, dout_ref, loss_ref):
        @pl.when(pl.program_id(0) == 0)
        def _():
            loss_ref[...] = jnp.zeros_like(loss_ref)

        z = ALPHA * x_ref[...] + (1.0 + gate_ref[...]) * y_ref[...]
        xo = _layer_norm_rows(z, g_ref[...], b_ref[...])
        err = xo - t_ref[...]
        z_ref[...] = z
        dout_ref[...] = err * (1.0 / D)
        loss_ref[...] += (0.5 / D) * jnp.sum(err * err)

    return pl.pallas_call(
        body, name=name, grid=(S // tm,),
        in_specs=[_row_spec(tm, D), _row_spec(tm, D)] + [_vec_spec(D)] * 3 + [_row_spec(tm, D)],
        out_specs=[_row_spec(tm, D), _row_spec(tm, D), pl.BlockSpec((1, LANE), lambda i: (0, 0))],
        out_shape=[jax.ShapeDtypeStruct((S, D), F32), jax.ShapeDtypeStruct((S, D), F32),
                   jax.ShapeDtypeStruct((1, LANE), F32)],
        compiler_params=_params(("arbitrary",)),
    )(x, y, gate, gamma, beta, target)


def _ln_bwd(dout, z, y, gate, gamma, name):
    S, D = z.shape
    tm = _tile(S, ROW_TILE, 8)

    def body(dout_ref, z_ref, y_ref, gate_ref, g_ref, dz_ref, dy_ref, dg_ref, db_ref, dgate_ref):
        @pl.when(pl.program_id(0) == 0)
        def _():
            dg_ref[...] = jnp.zeros_like(dg_ref)
            db_ref[...] = jnp.zeros_like(db_ref)
            dgate_ref[...] = jnp.zeros_like(dgate_ref)

        z = z_ref[...]
        dout = dout_ref[...]
        mu = jnp.mean(z, axis=-1, keepdims=True)
        zc = z - mu
        var = jnp.mean(zc * zc, axis=-1, keepdims=True)
        rstd = lax.rsqrt(var + LN_EPS)
        xhat = zc * rstd
        dxhat = dout * g_ref[...]
        m1 = jnp.mean(dxhat, axis=-1, keepdims=True)
        m2 = jnp.mean(dxhat * xhat, axis=-1, keepdims=True)
        dz = rstd * (dxhat - m1 - xhat * m2)
        dz_ref[...] = dz
        dy_ref[...] = (dz * (1.0 + gate_ref[...])).astype(BF16)
        dg_ref[...] += jnp.sum(dout * xhat, axis=0, keepdims=True)
        db_ref[...] += jnp.sum(dout, axis=0, keepdims=True)
        dgate_ref[...] += jnp.sum(dz * y_ref[...], axis=0, keepdims=True)

    return pl.pallas_call(
        body, name=name, grid=(S // tm,),
        in_specs=[_row_spec(tm, D)] * 3 + [_vec_spec(D)] * 2,
        out_specs=[_row_spec(tm, D), _row_spec(tm, D)] + [_vec_spec(D)] * 3,
        out_shape=[jax.ShapeDtypeStruct((S, D), F32), jax.ShapeDtypeStruct((S, D), BF16)]
        + [jax.ShapeDtypeStruct((1, D), F32)] * 3,
        compiler_params=_params(("arbitrary",)),
    )(dout, z, y, gate, gamma)


def _mod_bwd(dz, dh, xin, sc, after, name):
    S, D = dz.shape
    tm = _tile(S, ROW_TILE, 8)

    def body(dz_ref, dh_ref, x_ref, sc_ref, after_ref, dx_ref, dsc_ref, dsh_ref):
        @pl.when(pl.program_id(0) == 0)
        def _():
            dsc_ref[...] = jnp.zeros_like(dsc_ref)
            dsh_ref[...] = jnp.zeros_like(dsh_ref)

        dh = dh_ref[...]
        dx_ref[...] = ALPHA * dz_ref[...] + dh * (1.0 + sc_ref[...])
        dsc_ref[...] += jnp.sum(dh * x_ref[...], axis=0, keepdims=True)
        dsh_ref[...] += jnp.sum(dh, axis=0, keepdims=True)

    return pl.pallas_call(
        body, name=name, grid=(S // tm,),
        in_specs=[_row_spec(tm, D)] * 3 + [_vec_spec(D), pl.BlockSpec(memory_space=pl.ANY)],
        out_specs=[_row_spec(tm, D), _vec_spec(D), _vec_spec(D)],
        out_shape=[jax.ShapeDtypeStruct((S, D), F32)] + [jax.ShapeDtypeStruct((1, D), F32)] * 2,
        compiler_params=_params(("arbitrary",)),
    )(dz, dh, xin, sc, after)


def _shift_down(u, k, row):
    return jnp.where(row >= k, pltpu.roll(u, k, axis=0), 0.0)


def _shift_up(u, k, row, S):
    return jnp.where(row < S - k, pltpu.roll(u, S - k, axis=0), 0.0)


def _ffn_up(h, w, cw, cb, name):
    S, D = h.shape
    F = w.shape[2]
    tn = _tile(F, 256)

    piece = S // 2 if S % 16 == 0 else S
    halo = 8

    def body(h_ref, w_ref, cw_ref, cb_ref, u_ref, uc_ref, a_ref):
        tails = [None, None]
        for lo in range(0, S, piece):
            rows = slice(lo, lo + piece)
            conv = []
            for p in range(2):
                u = jnp.dot(h_ref[rows, :], w_ref[p], preferred_element_type=F32)
                u_ref[p, rows, :] = u.astype(BF16)
                above, tails[p] = tails[p], u[piece - halo:]
                if above is not None:
                    u = jnp.concatenate([above, u], axis=0)
                row = lax.broadcasted_iota(jnp.int32, u.shape, 0)
                cwp = cw_ref[p]
                uc = _shift_down(u, 2, row) * cwp[0:1] + _shift_down(u, 1, row) * cwp[1:2] + u * cwp[2:3] + cb_ref[p]
                uc = uc if above is None else uc[halo:]
                uc_ref[p, rows, :] = uc.astype(BF16)
                conv.append(uc)
            g, v = conv
            a_ref[rows, :] = (g * _sigmoid(g) * v).astype(BF16)

    half = pl.BlockSpec((2, S, tn), lambda j: (0, 0, j))
    return pl.pallas_call(
        body, name=name, grid=(F // tn,),
        in_specs=[pl.BlockSpec((S, D), lambda j: (0, 0)), pl.BlockSpec((2, D, tn), lambda j: (0, 0, j)),
                  pl.BlockSpec((2, 3, tn), lambda j: (0, 0, j)), pl.BlockSpec((2, 1, tn), lambda j: (0, 0, j))],
        out_specs=[half, half, pl.BlockSpec((S, tn), lambda j: (0, j))],
        out_shape=[jax.ShapeDtypeStruct((2, S, F), BF16), jax.ShapeDtypeStruct((2, S, F), BF16),
                   jax.ShapeDtypeStruct((S, F), BF16)],
        compiler_params=_params(("parallel",), 56),
    )(h, w, cw, cb)


def _ffn_bwd_elem(da, u, uc, cw, after, name):
    _, S, F = u.shape
    tn = _tile(F, 256)

    def body(da_ref, u_ref, uc_ref, cw_ref, after_ref, du_ref, dcw_ref, dcb_ref):
        row = lax.broadcasted_iota(jnp.int32, (S, tn), 0)
        da = da_ref[...].astype(F32)
        g, v = uc_ref[0].astype(F32), uc_ref[1].astype(F32)
        sg = _sigmoid(g)
        d_conv = (da * v * (sg * (1.0 + g * (1.0 - sg))), da * (g * sg))
        for p in range(2):
            d = d_conv[p]
            cwp = cw_ref[p]
            u = u_ref[p].astype(F32)
            d1, d2 = _shift_up(d, 1, row, S), _shift_up(d, 2, row, S)
            dcb_ref[p] = jnp.sum(d, axis=0, keepdims=True)
            dcw_ref[p, 0:1, :] = jnp.sum(d2 * u, axis=0, keepdims=True)
            dcw_ref[p, 1:2, :] = jnp.sum(d1 * u, axis=0, keepdims=True)
            dcw_ref[p, 2:3, :] = jnp.sum(d * u, axis=0, keepdims=True)
            du_ref[p] = (d * cwp[2:3] + d1 * cwp[1:2] + d2 * cwp[0:1]).astype(BF16)

    half = pl.BlockSpec((2, S, tn), lambda j: (0, 0, j))
    return pl.pallas_call(
        body, name=name, grid=(F // tn,),
        in_specs=[pl.BlockSpec((S, tn), lambda j: (0, j)), half, half, pl.BlockSpec((2, 3, tn), lambda j: (0, 0, j)),
                  pl.BlockSpec(memory_space=pl.ANY)],
        out_specs=[half, pl.BlockSpec((2, 3, tn), lambda j: (0, 0, j)), pl.BlockSpec((2, 1, tn), lambda j: (0, 0, j))],
        out_shape=[jax.ShapeDtypeStruct((2, S, F), BF16), jax.ShapeDtypeStruct((2, 3, F), F32),
                   jax.ShapeDtypeStruct((2, 1, F), F32)],
        compiler_params=_params(("parallel",), 56),
    )(da, u, uc, cw, after)


def _split3(x):
    hi = x.astype(BF16)
    r1 = x - hi.astype(F32)
    mid = r1.astype(BF16)
    lo = (r1 - mid.astype(F32)).astype(BF16)
    return hi, mid, lo


def _tri_matmul(x, upper, S):
    tc = _tile(S, 512)
    parts = _split3(x)
    outs = []
    for b in range(S // tc):
        r = lax.broadcasted_iota(jnp.int32, (S, tc), 0)
        c = lax.broadcasted_iota(jnp.int32, (S, tc), 1) + b * tc
        tri = jnp.where((r <= c) if upper else (r >= c), 1.0, 0.0).astype(BF16)
        acc = jnp.dot(parts[0], tri, preferred_element_type=F32)
        acc += jnp.dot(parts[1], tri, preferred_element_type=F32)
        acc += jnp.dot(parts[2], tri, preferred_element_type=F32)
        outs.append(acc)
    return outs, tc


def _fox_prep(flT, bf, name):
    H, S = flT.shape

    def body(fl_ref, b_ref, cum_ref):
        zz = fl_ref[...] + b_ref[...]
        lf = jnp.minimum(zz, 0.0) - jnp.log(1.0 + jnp.exp(-jnp.abs(zz)))
        outs, tc = _tri_matmul(lf, True, S)
        for b, o in enumerate(outs):
            cum_ref[:, b * tc:(b + 1) * tc] = o

    return pl.pallas_call(
        body, name=name,
        in_specs=[pl.BlockSpec(memory_space=pltpu.VMEM)] * 2,
        out_specs=pl.BlockSpec(memory_space=pltpu.VMEM),
        out_shape=jax.ShapeDtypeStruct((H, S), F32),
        compiler_params=pltpu.CompilerParams(vmem_limit_bytes=48 * MIB),
    )(flT, bf)


def _fox_prep_bwd(dcum_key, dcum_query, flT, bf, name):
    H, S = flT.shape

    def body(dck_ref, dcq_ref, fl_ref, b_ref, dfl_ref, dbf_ref):
        zz = fl_ref[...] + b_ref[...]
        outs, tc = _tri_matmul(dck_ref[...] + dcq_ref[...], False, S)
        total = jnp.zeros((H, 1), F32)
        for b, o in enumerate(outs):
            dfl = o * _sigmoid(-zz[:, b * tc:(b + 1) * tc])
            dfl_ref[:, b * tc:(b + 1) * tc] = dfl
            total += jnp.sum(dfl, axis=1, keepdims=True)
        dbf_ref[...] = total

    return pl.pallas_call(
        body, name=name,
        in_specs=[pl.BlockSpec(memory_space=pltpu.VMEM)] * 4,
        out_specs=[pl.BlockSpec(memory_space=pltpu.VMEM)] * 2,
        out_shape=[jax.ShapeDtypeStruct((H, S), F32), jax.ShapeDtypeStruct((H, 1), F32)],
        compiler_params=pltpu.CompilerParams(vmem_limit_bytes=48 * MIB),
    )(dcum_key, dcum_query, flT, bf)


FOX_TQ = 256
FOX_TC = 512


def _fox_scores(q, k_ref, ck_ref, i, lo, n, tq):
    k = k_ref[lo:lo + n, :].astype(BF16)
    s = lax.dot_general(q, k, (((1,), (1,)), ((), ())), preferred_element_type=F32) - ck_ref[:, lo:lo + n]
    qpos = i * tq + lax.broadcasted_iota(jnp.int32, (tq, n), 0)
    kpos = lo + lax.broadcasted_iota(jnp.int32, (tq, n), 1)
    return jnp.where(kpos <= qpos, s, -jnp.inf)


FOX_FWD_SPLITS = 4


def _fox_attn_fwd_part(proj, cum_row, H, q_lo, q_hi, filled, name):
    S = proj.shape[0]
    dh = FOX_HEAD_DIM
    tq = _tile(q_hi - q_lo, 2 * FOX_TQ)
    first = q_lo // tq
    scale = dh ** -0.5
    n_keep = 0 if filled is None else 2

    def body(q_ref, k_ref, v_ref, ck_ref, *rest):
        o_ref, lse_ref = rest[n_keep:]
        q = (q_ref[...] * scale).astype(BF16)
        s = _fox_scores(q, k_ref, ck_ref, first + pl.program_id(1), 0, q_hi, tq)
        m = jnp.max(s, axis=-1, keepdims=True)
        p = jnp.exp(s - m)
        l = jnp.sum(p, axis=-1, keepdims=True)
        o = jnp.dot(p.astype(BF16), v_ref[...].astype(BF16), preferred_element_type=F32) / l
        o_ref[...] = o.astype(BF16)
        lse_ref[...] = m + jnp.log(l)

    return pl.pallas_call(
        body, name=name, grid=(H, (q_hi - q_lo) // tq),
        in_specs=[pl.BlockSpec((tq, dh), lambda h, i: (first + i, h)),
                  pl.BlockSpec((q_hi, dh), lambda h, i: (0, H + h)),
                  pl.BlockSpec((q_hi, dh), lambda h, i: (0, 2 * H + h)),
                  pl.BlockSpec((None, 1, q_hi), lambda h, i: (h, 0, 0))] + [pl.BlockSpec(memory_space=pl.ANY)] * n_keep,
        out_specs=[pl.BlockSpec((tq, dh), lambda h, i: (first + i, h)),
                   pl.BlockSpec((None, tq, 1), lambda h, i: (h, first + i, 0))],
        out_shape=[jax.ShapeDtypeStruct((S, H * dh), BF16), jax.ShapeDtypeStruct((H, S, 1), F32)],
        input_output_aliases={4 + k: k for k in range(n_keep)},
        compiler_params=_params(("parallel", "parallel")),
    )(proj, proj, proj, cum_row, *(filled or ()))


def _fox_attn_fwd(proj, cum_row, H, name):
    S = proj.shape[0]
    n = FOX_FWD_SPLITS if S % (FOX_FWD_SPLITS * FOX_TQ) == 0 else 1
    out = None
    for part in range(n):
        out = _fox_attn_fwd_part(proj, cum_row, H, part * S // n, (part + 1) * S // n, out, name + str(part))
    return out


def _fox_attn_bwd(proj, do, o, cum_row, lse, H, after, name):
    S = proj.shape[0]
    dh = FOX_HEAD_DIM
    tq = _tile(S, 2 * FOX_TQ)
    tc = _tile(S, FOX_TC)
    scale = dh ** -0.5

    def body(q_ref, k_ref, v_ref, do_ref, o_ref, ck_ref, lse_ref, after_ref,
             dq_ref, dk_ref, dv_ref, dck_ref, dcq_ref, acc_ref):
        i = pl.program_id(1)

        @pl.when(i == 0)
        def _():
            dk_ref[...] = jnp.zeros_like(dk_ref)
            dv_ref[...] = jnp.zeros_like(dv_ref)
            dck_ref[...] = jnp.zeros_like(dck_ref)

        acc_ref[...] = jnp.zeros_like(acc_ref)
        dcq_ref[...] = jnp.zeros_like(dcq_ref)
        q = (q_ref[...] * scale).astype(BF16)
        do_f = do_ref[...]
        do_b = do_f.astype(BF16)
        delta = jnp.sum(do_f * o_ref[...].astype(F32), axis=-1, keepdims=True)
        lse_q = lse_ref[...]
        for c in range(S // tc):
            @pl.when(c * tc <= i * tq + tq - 1)
            def _():
                rows = slice(c * tc, (c + 1) * tc)
                p = jnp.exp(_fox_scores(q, k_ref, ck_ref, i, c * tc, tc, tq) - lse_q)
                dp = lax.dot_general(do_b, v_ref[rows, :].astype(BF16), (((1,), (1,)), ((), ())),
                                     preferred_element_type=F32)
                ds = p * (dp - delta)
                ds_b = ds.astype(BF16)
                acc_ref[...] += jnp.dot(ds_b, k_ref[rows, :].astype(BF16), preferred_element_type=F32)
                dk_ref[rows, :] += lax.dot_general(ds_b, q, (((0,), (0,)), ((), ())), preferred_element_type=F32)
                dv_ref[rows, :] += lax.dot_general(p.astype(BF16), do_b, (((0,), (0,)), ((), ())),
                                                   preferred_element_type=F32)
                dck_ref[:, rows] -= jnp.sum(ds, axis=0, keepdims=True)
                dcq_ref[...] += jnp.sum(ds, axis=-1, keepdims=True)
        dq_ref[...] = (acc_ref[...] * scale).astype(BF16)

    W = H * dh
    return pl.pallas_call(
        body, name=name, grid=(H, S // tq),
        in_specs=[pl.BlockSpec((tq, dh), lambda h, i: (i, h)),
                  pl.BlockSpec((S, dh), lambda h, i: (0, H + h)),
                  pl.BlockSpec((S, dh), lambda h, i: (0, 2 * H + h)),
                  pl.BlockSpec((tq, dh), lambda h, i: (i, h)),
                  pl.BlockSpec((tq, dh), lambda h, i: (i, h)),
                  pl.BlockSpec((None, 1, S), lambda h, i: (h, 0, 0)),
                  pl.BlockSpec((None, tq, 1), lambda h, i: (h, i, 0)),
                  pl.BlockSpec(memory_space=pl.ANY)],
        out_specs=[pl.BlockSpec((tq, dh), lambda h, i: (i, h)),
                   pl.BlockSpec((S, dh), lambda h, i: (0, h)),
                   pl.BlockSpec((S, dh), lambda h, i: (0, h)),
                   pl.BlockSpec((None, 1, S), lambda h, i: (h, 0, 0)),
                   pl.BlockSpec((None, tq, 1), lambda h, i: (h, i, 0))],
        out_shape=[jax.ShapeDtypeStruct((S, W), BF16), jax.ShapeDtypeStruct((S, W), F32),
                   jax.ShapeDtypeStruct((S, W), F32), jax.ShapeDtypeStruct((H, 1, S), F32),
                   jax.ShapeDtypeStruct((H, S, 1), F32)],
        scratch_shapes=[pltpu.VMEM((tq, dh), F32)],
        compiler_params=_params(("parallel", "arbitrary")),
    )(proj, proj, proj, do, o, cum_row, lse, after)


def _rope(parts, tabs, out_dtype, name):
    S = parts[0][0].shape[0]
    widths = [w for _, _, w in parts]
    total = sum(widths)
    tm = _tile(S, ROW_TILE, 8)
    flags = [r for _, r, _ in parts]

    def body(*refs):
        in_refs = refs[:len(parts)]
        cos_ref, sa_ref, sb_ref, o_ref = refs[len(parts):]
        cos, sa, sb = cos_ref[...], sa_ref[...], sb_ref[...]
        off = 0
        for ref, rot, w in zip(in_refs, flags, widths):
            for j in range(w // LANE):
                t = ref[:, j * LANE:(j + 1) * LANE]
                if rot:
                    t = t * cos + pltpu.roll(t, LANE - ROPE_DIM // 2, axis=1) * sa + pltpu.roll(t, ROPE_DIM // 2, axis=1) * sb
                o_ref[:, off + j * LANE:off + (j + 1) * LANE] = t.astype(o_ref.dtype)
            off += w

    return pl.pallas_call(
        body, name=name, grid=(S // tm,),
        in_specs=[pl.BlockSpec((tm, w), lambda i: (i, 0)) for w in widths] + [_row_spec(tm, LANE)] * 3,
        out_specs=_row_spec(tm, total),
        out_shape=jax.ShapeDtypeStruct((S, total), out_dtype),
        compiler_params=_params(("parallel",)),
    )(*[a for a, _, _ in parts], *tabs)


def _swa_band(ref_p, ref_c, hk):
    dh = SWA_HEAD_DIM
    return jnp.concatenate([ref_p[:, hk * dh:(hk + 1) * dh], ref_c[:, hk * dh:(hk + 1) * dh]], axis=0).astype(BF16)


def _swa_bias(G):
    qi = jnp.arange(G * Q_BLOCK)[:, None] % Q_BLOCK
    kj = jnp.arange(2 * Q_BLOCK)[None, :]
    rel = qi + Q_BLOCK - kj
    window = (rel >= 0) & (rel < SWA_WINDOW)
    both = jnp.stack([window & (kj >= Q_BLOCK), window])
    return jnp.where(both, 0.0, -jnp.inf).astype(F32)


def _swa_stack(ref, hk, G):
    dh = SWA_HEAD_DIM
    return jnp.concatenate([ref[:, (hk * G + g) * dh:(hk * G + g + 1) * dh] for g in range(G)], axis=0)


def _swa_unstack(ref, stacked, hk, G):
    dh, QB = SWA_HEAD_DIM, Q_BLOCK
    for g in range(0, G, 2):
        c0 = (hk * G + g) * dh
        pair = jnp.concatenate([stacked[g * QB:(g + 1) * QB], stacked[(g + 1) * QB:(g + 2) * QB]], axis=1)
        ref[:, c0:c0 + 2 * dh] = pair.astype(ref.dtype)


def _swa_sink_rows(sink_ref, hk, G):
    return jnp.concatenate([jnp.broadcast_to(sink_ref[0:1, hk * G + g:hk * G + g + 1], (Q_BLOCK, 1)) for g in range(G)],
                           axis=0)


def _swa_attn_fwd(qk, proj, sinks, Hq, name):
    S = qk.shape[0]
    dh, G, QB = SWA_HEAD_DIM, SWA_GROUP, Q_BLOCK
    Hk = Hq // G
    Wq, Wk = Hq * dh, Hk * dh
    nb = S // QB
    scale = dh ** -0.5

    per_step = 4 if nb % 4 == 0 else 2

    def body(q_ref, kp_ref, kc_ref, vp_ref, vc_ref, sink_ref, bias_ref, o_ref, lse_ref):
        n = pl.program_id(0)
        lane = lax.broadcasted_iota(jnp.int32, (QB, LANE), 1)
        blocks = [pl.ds(sub * QB, QB) for sub in range(per_step)]
        for sub, rows in enumerate(blocks):
            bias = bias_ref[jnp.minimum(n, 1)] if sub == 0 else bias_ref[1]
            k_band = (kp_ref, kc_ref.at[rows]) if sub == 0 else (kc_ref.at[blocks[sub - 1]], kc_ref.at[rows])
            v_band = (vp_ref, vc_ref.at[rows]) if sub == 0 else (vc_ref.at[blocks[sub - 1]], vc_ref.at[rows])
            lse_tile = jnp.zeros((QB, LANE), F32)
            for hk in range(Hk):
                kb = _swa_band(*k_band, hk)
                vb = _swa_band(*v_band, hk)
                q = (_swa_stack(q_ref.at[rows], hk, G) * scale).astype(BF16)
                sk = _swa_sink_rows(sink_ref, hk, G)
                s = lax.dot_general(q, kb, (((1,), (1,)), ((), ())), preferred_element_type=F32) + bias
                m = jnp.maximum(jnp.max(s, axis=-1, keepdims=True), sk)
                p = jnp.exp(s - m)
                l = jnp.sum(p, axis=-1, keepdims=True) + jnp.exp(sk - m)
                o = jnp.dot(p.astype(BF16), vb, preferred_element_type=F32) / l
                lse = m + jnp.log(l)
                for g in range(G):
                    lse_tile = jnp.where(lane == hk * G + g, lse[g * QB:(g + 1) * QB], lse_tile)
                _swa_unstack(o_ref.at[rows], o, hk, G)
            lse_ref[rows, :] = lse_tile

    kcol, vcol = Wq // Wk, (Wq + Wk) // Wk
    before = lambda n: jnp.maximum(per_step * n - 1, 0)
    rows_per_step = per_step * QB
    return pl.pallas_call(
        body, name=name, grid=(nb // per_step,),
        in_specs=[pl.BlockSpec((rows_per_step, Wq), lambda n: (n, 0)),
                  pl.BlockSpec((QB, Wk), lambda n: (before(n), kcol)),
                  pl.BlockSpec((rows_per_step, Wk), lambda n: (n, kcol)),
                  pl.BlockSpec((QB, Wk), lambda n: (before(n), vcol)),
                  pl.BlockSpec((rows_per_step, Wk), lambda n: (n, vcol)),
                  pl.BlockSpec((1, LANE), lambda n: (0, 0)),
                  pl.BlockSpec((2, G * QB, 2 * QB), lambda n: (0, 0, 0))],
        out_specs=[pl.BlockSpec((rows_per_step, Wq), lambda n: (n, 0)),
                   pl.BlockSpec((rows_per_step, LANE), lambda n: (n, 0))],
        out_shape=[jax.ShapeDtypeStruct((S, Wq), BF16), jax.ShapeDtypeStruct((S, LANE), F32)],
        compiler_params=_params(("parallel",)),
    )(qk, qk, qk, proj, proj, sinks, _swa_bias(G))


def _swa_attn_bwd(qk, proj, sinks, do, lse, Hq, after, name):
    S = qk.shape[0]
    dh, G, QB = SWA_HEAD_DIM, SWA_GROUP, Q_BLOCK
    Hk = Hq // G
    Wq, Wk = Hq * dh, Hk * dh
    nb = S // QB
    scale = dh ** -0.5

    def body(q_ref, kp_ref, kc_ref, vp_ref, vc_ref, sink_ref, do_ref, lse_ref, bias_ref, after_ref,
             dq_ref, dk_ref, dv_ref, dsink_ref, carry_k, carry_v):
        n = pl.program_id(0)

        @pl.when(n == 0)
        def _():
            dsink_ref[...] = jnp.zeros_like(dsink_ref)

        @pl.when(n < nb)
        def _():
            bias = bias_ref[...]
            lane = lax.broadcasted_iota(jnp.int32, (1, LANE), 1)
            dsink = jnp.zeros((1, LANE), F32)
            dk_heads, dv_heads = [], []
            for hk in range(Hk):
                kb = _swa_band(kp_ref, kc_ref, hk)
                vb = _swa_band(vp_ref, vc_ref, hk)
                q = (_swa_stack(q_ref, hk, G) * scale).astype(BF16)
                do_s = _swa_stack(do_ref, hk, G).astype(BF16)
                lse = jnp.concatenate([lse_ref[:, hk * G + g:hk * G + g + 1] for g in range(G)], axis=0)
                s = lax.dot_general(q, kb, (((1,), (1,)), ((), ())), preferred_element_type=F32) + bias
                p = jnp.exp(s - lse)
                p_sink = jnp.exp(_swa_sink_rows(sink_ref, hk, G) - lse)
                dp = lax.dot_general(do_s, vb, (((1,), (1,)), ((), ())), preferred_element_type=F32)
                delta = jnp.sum(p * dp, axis=-1, keepdims=True)
                ds_b = (p * (dp - delta)).astype(BF16)
                _swa_unstack(dq_ref, jnp.dot(ds_b, kb, preferred_element_type=F32) * scale, hk, G)
                dk_heads.append(lax.dot_general(ds_b, q, (((0,), (0,)), ((), ())), preferred_element_type=F32))
                dv_heads.append(lax.dot_general(p.astype(BF16), do_s, (((0,), (0,)), ((), ())), preferred_element_type=F32))
                sink_term = p_sink * delta
                for g in range(G):
                    dsink = jnp.where(lane == hk * G + g,
                                      -jnp.sum(sink_term[g * QB:(g + 1) * QB], axis=0, keepdims=True), dsink)
            dsink_ref[...] += dsink
            dk_all = jnp.concatenate(dk_heads, axis=1)
            dv_all = jnp.concatenate(dv_heads, axis=1)

            @pl.when(n > 0)
            def _():
                dk_ref[...] = carry_k[...] + dk_all[:QB]
                dv_ref[...] = carry_v[...] + dv_all[:QB]

            carry_k[...] = dk_all[QB:]
            carry_v[...] = dv_all[QB:]

        @pl.when(n == nb)
        def _():
            dk_ref[...] = carry_k[...]
            dv_ref[...] = carry_v[...]

    kcol, vcol = Wq // Wk, (Wq + Wk) // Wk
    cur = lambda n: jnp.minimum(n, nb - 1)
    prev = lambda n: jnp.maximum(jnp.minimum(n, nb - 1) - 1, 0)
    return pl.pallas_call(
        body, name=name, grid=(nb + 1,),
        in_specs=[pl.BlockSpec((QB, Wq), lambda n: (cur(n), 0)),
                  pl.BlockSpec((QB, Wk), lambda n: (prev(n), kcol)),
                  pl.BlockSpec((QB, Wk), lambda n: (cur(n), kcol)),
                  pl.BlockSpec((QB, Wk), lambda n: (prev(n), vcol)),
                  pl.BlockSpec((QB, Wk), lambda n: (cur(n), vcol)),
                  pl.BlockSpec((1, LANE), lambda n: (0, 0)),
                  pl.BlockSpec((QB, Wq), lambda n: (cur(n), 0)),
                  pl.BlockSpec((QB, LANE), lambda n: (cur(n), 0)),
                  pl.BlockSpec((None, G * QB, 2 * QB), lambda n: (jnp.minimum(n, 1), 0, 0)),
                  pl.BlockSpec(memory_space=pl.ANY)],
        out_specs=[pl.BlockSpec((QB, Wq), lambda n: (cur(n), 0)),
                   pl.BlockSpec((QB, Wk), lambda n: (jnp.maximum(n - 1, 0), 0)),
                   pl.BlockSpec((QB, Wk), lambda n: (jnp.maximum(n - 1, 0), 0)),
                   pl.BlockSpec((1, LANE), lambda n: (0, 0))],
        out_shape=[jax.ShapeDtypeStruct((S, Wq), F32), jax.ShapeDtypeStruct((S, Wk), F32),
                   jax.ShapeDtypeStruct((S, Wk), F32), jax.ShapeDtypeStruct((1, LANE), F32)],
        scratch_shapes=[pltpu.VMEM((QB, Wk), F32), pltpu.VMEM((QB, Wk), F32)],
        compiler_params=_params(("arbitrary",)),
    )(qk, qk, qk, proj, proj, sinks, do, lse, _swa_bias(G), after)


ADA_ROWS = 16


def _ada_mod(c_pad, w, b, name):
    L, D, N = w.shape
    tn = _tile(N, 512)

    def body(c_ref, w_ref, b_ref, o_ref):
        c = c_ref[...]
        c = c * _sigmoid(c)
        ch = c.astype(BF16)
        cl = (c - ch.astype(F32)).astype(BF16)
        ww = w_ref[...]
        wh = ww.astype(BF16)
        wl = (ww - wh.astype(F32)).astype(BF16)
        acc = jnp.dot(ch, wh, preferred_element_type=F32)
        acc += jnp.dot(ch, wl, preferred_element_type=F32)
        acc += jnp.dot(cl, wh, preferred_element_type=F32)
        o_ref[...] = acc + b_ref[...]

    return pl.pallas_call(
        body, name=name, grid=(L, N // tn),
        in_specs=[pl.BlockSpec((ADA_ROWS, D), lambda l, j: (0, 0)),
                  pl.BlockSpec((None, D, tn), lambda l, j: (l, 0, j)),
                  pl.BlockSpec((None, 1, tn), lambda l, j: (l, 0, j))],
        out_specs=pl.BlockSpec((None, ADA_ROWS, tn), lambda l, j: (l, 0, j)),
        out_shape=jax.ShapeDtypeStruct((L, ADA_ROWS, N), F32),
        compiler_params=_params(("parallel", "parallel")),
    )(c_pad, w, b)


def _ada_bwd_adam(cT, dm, w, m, v, name):
    D = cT.shape[0]
    L, B, N = dm.shape
    tm = _tile(D, 256)

    def body(c_ref, dm_ref, w_ref, m_ref, v_ref, g_ref, d_ref, mo_ref, vo_ref):
        c = c_ref[...]
        c = c * _sigmoid(c)
        dmv = dm_ref[...]
        g = c[:, 0:1] * dmv[0:1, :]
        for b in range(1, B):
            g += c[:, b:b + 1] * dmv[b:b + 1, :]
        delta, mn, vn = _adamw_math(w_ref[...], g, m_ref[...], v_ref[...])
        g_ref[...] = g
        d_ref[...] = delta
        mo_ref[...] = mn
        vo_ref[...] = vn

    spec = pl.BlockSpec((None, tm, N), lambda l, i: (l, i, 0))
    return pl.pallas_call(
        body, name=name, grid=(L, D // tm),
        in_specs=[pl.BlockSpec((tm, B), lambda l, i: (i, 0)), pl.BlockSpec((None, B, N), lambda l, i: (l, 0, 0)),
                  spec, spec, spec],
        out_specs=[spec] * 4,
        out_shape=[jax.ShapeDtypeStruct((L, D, N), F32)] * 4,
        compiler_params=_params(("parallel", "parallel")),
    )(cT, dm, w, m, v)


def _adamw_math(w, g, m, v):
    m = ADAM_B1 * m + (1.0 - ADAM_B1) * g
    v = ADAM_B2 * v + (1.0 - ADAM_B2) * (g * g)
    m_hat = m / (1.0 - ADAM_B1 ** ADAM_STEP)
    v_hat = v / (1.0 - ADAM_B2 ** ADAM_STEP)
    delta = -ADAM_LR * (m_hat / (jnp.sqrt(v_hat) + ADAM_EPS) + ADAM_WD * w)
    return delta, m, v


def _adam_rows(R, C):
    lanes = -(-C // LANE) * LANE
    return _tile(R, max(8, (262144 // lanes) // 8 * 8), 8)


def _adam_sum(recv, w, m, v, layer, filled, name, by_cols=False):
    P, R, C = recv.shape
    L = w.shape[0]
    n_keep = 0 if filled is None else 4
    if by_cols:
        tc = _tile(C, 256)
        grid, spec = (C // tc,), pl.BlockSpec((None, R, tc), lambda i: (layer, 0, i))
        recv_spec = pl.BlockSpec((P, R, tc), lambda i: (0, 0, i))
    else:
        tr = _adam_rows(R, C)
        grid, spec = (R // tr,), pl.BlockSpec((None, tr, C), lambda i: (layer, i, 0))
        recv_spec = pl.BlockSpec((P, tr, C), lambda i: (0, i, 0))

    def body(r_ref, w_ref, m_ref, v_ref, *rest):
        g_ref, d_ref, mo_ref, vo_ref = rest[n_keep:]
        g = r_ref[0].astype(F32)
        for p in range(1, P):
            g = g + r_ref[p].astype(F32)
        delta, mn, vn = _adamw_math(w_ref[...], g, m_ref[...], v_ref[...])
        g_ref[...] = g
        d_ref[...] = delta
        mo_ref[...] = mn
        vo_ref[...] = vn

    return pl.pallas_call(
        body, name=name, grid=grid,
        in_specs=[recv_spec, spec, spec, spec] + [pl.BlockSpec(memory_space=pl.ANY)] * n_keep,
        out_specs=[spec] * 4,
        out_shape=[jax.ShapeDtypeStruct((L, R, C), F32)] * 4,
        input_output_aliases={4 + k: k for k in range(n_keep)},
        compiler_params=_params(("parallel",)),
    )(recv, w, m, v, *(filled or ()))


def _adam(g, w, m, v, name):
    L, R, C = w.shape
    tr = _adam_rows(R, C)

    def body(g_ref, w_ref, m_ref, v_ref, d_ref, mo_ref, vo_ref):
        delta, mn, vn = _adamw_math(w_ref[...], g_ref[...], m_ref[...], v_ref[...])
        d_ref[...] = delta
        mo_ref[...] = mn
        vo_ref[...] = vn

    spec = pl.BlockSpec((None, tr, C), lambda l, i: (l, i, 0))
    return pl.pallas_call(
        body, name=name, grid=(L, R // tr),
        in_specs=[spec] * 4, out_specs=[spec] * 3,
        out_shape=[jax.ShapeDtypeStruct((L, R, C), F32)] * 3,
        compiler_params=_params(("parallel", "parallel")),
    )(g, w, m, v)


def _sum_slots(x, after, name):
    P, R, C = x.shape

    def body(x_ref, after_ref, o_ref):
        acc = x_ref[0]
        for p in range(1, P):
            acc = acc + x_ref[p]
        o_ref[...] = acc

    return pl.pallas_call(
        body, name=name,
        in_specs=[pl.BlockSpec(memory_space=pltpu.VMEM), pl.BlockSpec(memory_space=pl.ANY)],
        out_specs=pl.BlockSpec(memory_space=pltpu.VMEM),
        out_shape=jax.ShapeDtypeStruct((R, C), F32),
        compiler_params=pltpu.CompilerParams(vmem_limit_bytes=48 * MIB),
    )(x, after)


def _my_pos():
    return lax.axis_index("x"), lax.axis_index("y"), lax.axis_index("c")


def _all_gather_small(x, name, after=()):
    R, C = x.shape
    n_after = len(after)

    def body(x_ref, *rest):
        out_ref, send_sems, recv_sems = rest[n_after:]
        x_, y_, c_ = _my_pos()
        me, sibling = (x_, y_, c_), (x_, y_, 1 - c_)
        chips = [(1 - x_, y_), (x_, 1 - y_), (1 - x_, 1 - y_)]

        def slot(px, py, pc):
            return out_ref.at[4 * px + 2 * py + pc]

        def copy(k, block, to):
            return pltpu.make_async_remote_copy(
                src_ref=slot(*block), dst_ref=slot(*block), send_sem=send_sems.at[k], recv_sem=recv_sems.at[k],
                device_id=to, device_id_type=MESH)

        out_ref[4 * x_ + 2 * y_ + c_] = x_ref[...]
        first = [copy(0, me, sibling)] + [copy(1 + j, me, (*chip, c_)) for j, chip in enumerate(chips)]
        for cp in first:
            cp.start()
        passed = [copy(4 + j, (*chip, c_), sibling) for j, chip in enumerate(chips)]
        for j, chip in enumerate(chips):
            copy(1 + j, (*chip, c_), me).wait_recv()
            passed[j].start()
        copy(0, sibling, me).wait_recv()
        for j, chip in enumerate(chips):
            copy(4 + j, (*chip, 1 - c_), me).wait_recv()
        for cp in first + passed:
            cp.wait_send()

    return pl.pallas_call(
        body, name=name,
        in_specs=[pl.BlockSpec(memory_space=pltpu.VMEM)] + [pl.BlockSpec(memory_space=pl.ANY)] * n_after,
        out_specs=pl.BlockSpec(memory_space=pltpu.VMEM),
        out_shape=jax.ShapeDtypeStruct((N_DEV, R, C), x.dtype),
        scratch_shapes=[pltpu.SemaphoreType.DMA((7,)), pltpu.SemaphoreType.DMA((7,))],
        compiler_params=pltpu.CompilerParams(vmem_limit_bytes=48 * MIB),
    )(x, *after)


HBM_SPEC = pl.BlockSpec(memory_space=pltpu.HBM)
SEM_SPEC = pl.BlockSpec(memory_space=pltpu.SEMAPHORE)
ANY_SPEC = pl.BlockSpec(memory_space=pl.ANY)
SPLIT_EFFECT = pltpu.SideEffectType.DATAFLOW_SIDE_EFFECTING


def _in_hbm(a):
    return pltpu.with_memory_space_constraint(a, pltpu.HBM)


def _gathered_shape(a, kind):
    if kind == "major":
        return (N_DEV,) + a.shape
    if kind == "rows":
        return (N_DEV * a.shape[0], a.shape[1])
    return (2, a.shape[0], 4 * a.shape[1])


def _gather_slot(ref, kind, block, shard_shape):
    px, py, pc = block
    if kind == "major":
        return ref.at[4 * px + 2 * py + pc]
    if kind == "rows":
        r = shard_shape[0]
        return ref.at[pl.ds(pl.multiple_of((4 * px + 2 * py + pc) * r, r), r), :]
    cu = shard_shape[1]
    return ref.at[px, :, pl.ds(pl.multiple_of((2 * py + pc) * cu, cu), cu)]


def _gather_peers():
    x_, y_, c_ = _my_pos()
    return (x_, y_, c_), (x_, y_, 1 - c_), [(1 - x_, y_), (x_, 1 - y_), (1 - x_, 1 - y_)]


def _gather_start(shards, kinds, after, name):
    n = len(shards)
    bufs = [lax.empty(_gathered_shape(a, k), a.dtype) for a, k in zip(shards, kinds)]
    extra = [] if after is None else [after]

    def body(*refs):
        shard_refs, buf_refs = refs[:n], refs[n:2 * n]
        send_sems, recv_sems, local_sems = refs[2 * n + len(extra):2 * n + len(extra) + 3]
        token = refs[-1]
        me, sibling, chips = _gather_peers()
        for e in range(n):
            mine = _gather_slot(buf_refs[e], kinds[e], me, shards[e].shape)
            pltpu.make_async_copy(shard_refs[e], mine, local_sems.at[e]).start()
            for k, to in enumerate([sibling] + [(*chip, me[2]) for chip in chips]):
                pltpu.make_async_remote_copy(
                    src_ref=shard_refs[e], dst_ref=mine, send_sem=send_sems.at[4 * e + k],
                    recv_sem=recv_sems.at[4 * e + k], device_id=to, device_id_type=MESH).start()
        token[...] = jnp.zeros_like(token)

    out = pl.pallas_call(
        body, name=name,
        out_shape=(pltpu.SemaphoreType.DMA((4 * n,)), pltpu.SemaphoreType.DMA((4 * n,)), pltpu.SemaphoreType.DMA((n,)),
                   *[pltpu.HBM(a.shape, a.dtype) for a in shards], *[pltpu.HBM(a.shape, a.dtype) for a in bufs],
                   jax.ShapeDtypeStruct((8, LANE), F32)),
        in_specs=[HBM_SPEC] * (2 * n) + [ANY_SPEC] * len(extra),
        out_specs=(SEM_SPEC, SEM_SPEC, SEM_SPEC, *[HBM_SPEC] * (2 * n), pl.BlockSpec(memory_space=pltpu.VMEM)),
        input_output_aliases={i: 3 + i for i in range(2 * n)},
        compiler_params=pltpu.CompilerParams(has_side_effects=SPLIT_EFFECT),
    )(*[_in_hbm(a) for a in shards], *[_in_hbm(a) for a in bufs], *extra)
    return out[0], out[1], out[2], out[3:3 + n], out[3 + n:3 + 2 * n], out[-1]


def _gather_forward(recv_sems, bufs, kinds, shard_shapes, after, name):
    n = len(bufs)

    def body(*refs):
        buf_refs, recv_in = refs[:n], refs[n]
        fsend, frecv = refs[n + 2], refs[n + 3]
        token = refs[-1]
        me, sibling, chips = _gather_peers()
        for e in range(n):
            for j, chip in enumerate(chips):
                slot = _gather_slot(buf_refs[e], kinds[e], (*chip, me[2]), shard_shapes[e])
                pltpu.make_async_remote_copy(
                    src_ref=slot, dst_ref=slot, send_sem=recv_in.at[4 * e + 1 + j], recv_sem=recv_in.at[4 * e + 1 + j],
                    device_id=me, device_id_type=MESH).wait_recv()
                pltpu.make_async_remote_copy(
                    src_ref=slot, dst_ref=slot, send_sem=fsend.at[3 * e + j], recv_sem=frecv.at[3 * e + j],
                    device_id=sibling, device_id_type=MESH).start()
        token[...] = jnp.zeros_like(token)

    out = pl.pallas_call(
        body, name=name,
        out_shape=(pltpu.SemaphoreType.DMA((3 * n,)), pltpu.SemaphoreType.DMA((3 * n,)),
                   *[pltpu.HBM(a.shape, a.dtype) for a in bufs], jax.ShapeDtypeStruct((8, LANE), F32)),
        in_specs=[HBM_SPEC] * n + [SEM_SPEC, ANY_SPEC],
        out_specs=(SEM_SPEC, SEM_SPEC, *[HBM_SPEC] * n, pl.BlockSpec(memory_space=pltpu.VMEM)),
        input_output_aliases={i: 2 + i for i in range(n)},
        compiler_params=pltpu.CompilerParams(has_side_effects=SPLIT_EFFECT),
    )(*bufs, recv_sems, after)
    return out[0], out[1], out[2:2 + n], out[-1]


def _gather_wait(send_sems, recv_sems, local_sems, fsend, frecv, shards, bufs, kinds, after, name):
    n = len(bufs)

    def body(*refs):
        shard_refs, buf_refs = refs[:n], refs[n:2 * n]
        send_in, recv_in, local_in, fsend_in, frecv_in = refs[2 * n:2 * n + 5]
        me, sibling, chips = _gather_peers()

        def arrival(slot, sem):
            return pltpu.make_async_remote_copy(src_ref=slot, dst_ref=slot, send_sem=sem, recv_sem=sem,
                                                device_id=me, device_id_type=MESH)

        for e in range(n):
            shape = shards[e].shape
            mine = _gather_slot(buf_refs[e], kinds[e], me, shape)
            pltpu.make_async_copy(shard_refs[e], mine, local_in.at[e]).wait()
            arrival(_gather_slot(buf_refs[e], kinds[e], sibling, shape), recv_in.at[4 * e]).wait_recv()
            for j, chip in enumerate(chips):
                arrival(_gather_slot(buf_refs[e], kinds[e], (*chip, 1 - me[2]), shape), frecv_in.at[3 * e + j]).wait_recv()
            for k in range(4):
                arrival(mine, send_in.at[4 * e + k]).wait_send()
            for j in range(3):
                arrival(mine, fsend_in.at[3 * e + j]).wait_send()

    out = pl.pallas_call(
        body, name=name,
        out_shape=(*[pltpu.HBM(a.shape, a.dtype) for a in shards], *[pltpu.HBM(a.shape, a.dtype) for a in bufs]),
        in_specs=[HBM_SPEC] * (2 * n) + [SEM_SPEC] * 5 + [ANY_SPEC],
        out_specs=tuple([HBM_SPEC] * (2 * n)),
        input_output_aliases={i: i for i in range(2 * n)},
        compiler_params=pltpu.CompilerParams(has_side_effects=SPLIT_EFFECT),
    )(*shards, *bufs, send_sems, recv_sems, local_sems, fsend, frecv, after)
    return out[n:]


def _grad_slice(ref, kind, j):
    if kind == "whole":
        return ref
    if kind == "major":
        return ref.at[j]
    if kind == "rows":
        r = ref.shape[0] // N_DEV
        return ref.at[pl.ds(j * r, r), :]
    cu = ref.shape[2] // 4
    return ref.at[j // 4, :, pl.ds((j % 4) * cu, cu)]


def _slice_shape(a, kind):
    if kind == "whole":
        return a.shape
    if kind == "major":
        return a.shape[1:]
    if kind == "rows":
        return (a.shape[0] // N_DEV, a.shape[1])
    return (a.shape[1], a.shape[2] // 4)


def _scatter_copies(srcs, lands, kinds, send_sems, recv_sems):
    x_, y_, c_ = _my_pos()
    me = 4 * x_ + 2 * y_ + c_
    n = len(srcs)

    def remote(e, j):
        return pltpu.make_async_remote_copy(
            src_ref=_grad_slice(srcs[e], kinds[e], j), dst_ref=lands[e].at[me],
            send_sem=send_sems.at[e * N_DEV + j], recv_sem=recv_sems.at[e * N_DEV + me],
            device_id=(j // 4, (j // 2) % 2, j % 2), device_id_type=MESH)

    def local(e, j):
        return pltpu.make_async_copy(_grad_slice(srcs[e], kinds[e], j), lands[e].at[j], recv_sems.at[e * N_DEV + j])

    def arrival(e, i):
        return pltpu.make_async_remote_copy(
            src_ref=_grad_slice(srcs[e], kinds[e], i), dst_ref=lands[e].at[i],
            send_sem=send_sems.at[e * N_DEV + i], recv_sem=recv_sems.at[e * N_DEV + i],
            device_id=(i // 4, (i // 2) % 2, i % 2), device_id_type=MESH)

    def start():
        for e in range(n):
            for j in range(N_DEV):
                @pl.when(me == j)
                def _():
                    local(e, j).start()

                @pl.when(me != j)
                def _():
                    remote(e, j).start()

    def wait():
        for e in range(n):
            for i in range(N_DEV):
                @pl.when(me == i)
                def _():
                    local(e, i).wait()

                @pl.when(me != i)
                def _():
                    arrival(e, i).wait_recv()
        for e in range(n):
            for j in range(N_DEV):
                @pl.when(me != j)
                def _():
                    remote(e, j).wait_send()

    return start, wait


def _scatter_start(srcs, kinds, after, name):
    n = len(srcs)
    lands = [lax.empty((N_DEV,) + _slice_shape(a, k), a.dtype) for a, k in zip(srcs, kinds)]
    extra = [] if after is None else [after]

    def body(*refs):
        src_refs, land_refs = refs[:n], refs[n:2 * n]
        send_sems, recv_sems = refs[2 * n + len(extra)], refs[2 * n + len(extra) + 1]
        token = refs[-1]
        start, _ = _scatter_copies(src_refs, land_refs, kinds, send_sems, recv_sems)
        start()
        token[...] = jnp.zeros_like(token)

    out = pl.pallas_call(
        body, name=name,
        out_shape=(pltpu.SemaphoreType.DMA((n * N_DEV,)), pltpu.SemaphoreType.DMA((n * N_DEV,)),
                   *[pltpu.HBM(a.shape, a.dtype) for a in srcs], *[pltpu.HBM(a.shape, a.dtype) for a in lands],
                   jax.ShapeDtypeStruct((8, LANE), F32)),
        in_specs=[HBM_SPEC] * (2 * n) + [ANY_SPEC] * len(extra),
        out_specs=(SEM_SPEC, SEM_SPEC, *[HBM_SPEC] * (2 * n), pl.BlockSpec(memory_space=pltpu.VMEM)),
        input_output_aliases={i: 2 + i for i in range(2 * n)},
        compiler_params=pltpu.CompilerParams(has_side_effects=SPLIT_EFFECT),
    )(*[_in_hbm(a) for a in srcs], *[_in_hbm(a) for a in lands], *extra)
    return out[0], out[1], out[2:2 + n], out[2 + n:2 + 2 * n], out[-1]


def _scatter_wait(send_sems, recv_sems, srcs, lands, kinds, after, name):
    n = len(srcs)

    def body(*refs):
        src_refs, land_refs = refs[:n], refs[n:2 * n]
        _, wait = _scatter_copies(src_refs, land_refs, kinds, refs[2 * n], refs[2 * n + 1])
        wait()

    out = pl.pallas_call(
        body, name=name,
        out_shape=(*[pltpu.HBM(a.shape, a.dtype) for a in srcs], *[pltpu.HBM(a.shape, a.dtype) for a in lands]),
        in_specs=[HBM_SPEC] * (2 * n) + [SEM_SPEC, SEM_SPEC] + [ANY_SPEC] * len(after),
        out_specs=tuple([HBM_SPEC] * (2 * n)),
        input_output_aliases={i: i for i in range(2 * n)},
        compiler_params=pltpu.CompilerParams(has_side_effects=SPLIT_EFFECT),
    )(*srcs, *lands, send_sems, recv_sems, *after)
    return out[n:]


def _rope_tables(positions, sign):
    half = ROPE_DIM // 2
    inv_freq = ROPE_THETA ** (-jnp.arange(0, ROPE_DIM, 2, dtype=F32) / ROPE_DIM)
    ang = positions.astype(F32)[:, None] * inv_freq
    reps = LANE // half
    cos = jnp.tile(jnp.cos(ang), (1, reps))
    sin = jnp.tile(jnp.sin(ang), (1, reps)) * sign
    d = jnp.arange(LANE) % SWA_HEAD_DIM
    return (jnp.where(d < ROPE_DIM, cos, 1.0), jnp.where(d < half, -sin, 0.0),
            jnp.where((d >= half) & (d < ROPE_DIM), sin, 0.0))


def _pad_cols(a, n):
    return jnp.pad(a, ((0, 0), (0, n - a.shape[1])))


def _local_step(x, target, positions, mods, fetch, P, on_grads, on_small):
    S, D = x.shape
    Hf = D // FOX_HEAD_DIM
    Hq = D // SWA_HEAD_DIM
    Hk = Hq // SWA_GROUP
    Wk = Hk * SWA_HEAD_DIM
    n_in = 3 * D + Hf
    (sh1a, sc1a, g1a, sh2a, sc2a, g2a), (sh1b, sc1b, g1b, sh2b, sc2b, g2b) = mods
    row = lambda v: v.reshape(1, -1)
    cw = [jnp.transpose(P["conv_w"][l].reshape(3, 2, -1), (1, 0, 2)) for l in range(2)]
    cb = [P["conv_b"][l].reshape(2, 1, -1) for l in range(2)]

    W = dict(up=[None, None], down=[None, None])
    h1a = _modulate(x, sc1a, sh1a, "modulate_in")
    W["fox_in"], W["fox_o"] = fetch("fox", "wait", h1a)
    proj_a = _mm_nn(h1a, W["fox_in"], F32, "fox_in_proj", tn=896)
    flT = proj_a[:, 3 * D:n_in].T
    bf_col = P["fox_b_f"].reshape(Hf, 1)
    cumT = _fox_prep(flT, bf_col, "fox_cumsum")
    cum_row = cumT.reshape(Hf, 1, S)
    o_a, lse_a = _fox_attn_fwd(proj_a, cum_row, Hf, "fox_attn_fwd")
    token = fetch("ffn0", "forward", o_a)
    y1a = _mm_nn(o_a, W["fox_o"], F32, "fox_out_proj")
    z1a, x1, h2a = _ln_fwd(x, y1a, g1a, row(P["ln_mix_g"][0]), row(P["ln_mix_b"][0]), sc2a, sh2a, token, "ln_mix0")
    W["up"][0], W["down"][0] = fetch("ffn0", "wait", h2a)
    u_a, uc_a, a_a = _ffn_up(h2a, W["up"][0], cw[0], cb[0], "ffn_up0")
    token = fetch("swa", "forward", a_a)
    y2a = _mm_nn(a_a, W["down"][0], F32, "ffn_down0", tk=2816)
    z2a, x2, h1b = _ln_fwd(x1, y2a, g2a, row(P["ln_ffn_g"][0]), row(P["ln_ffn_b"][0]), sc1b, sh1b, token, "ln_ffn0")

    W["swa_in"], W["swa_o"] = fetch("swa", "wait", h1b)
    proj_b = _mm_nn(h1b, W["swa_in"], F32, "swa_in_proj")
    tabs_f = _rope_tables(positions, 1.0)
    tabs_b = _rope_tables(positions, -1.0)
    qk = _rope([(proj_b, True, D + Wk)], tabs_f, F32, "rope_fwd")
    sinks = _pad_cols(P["swa_sinks"].reshape(1, Hq), LANE)
    o_b, lse_b = _swa_attn_fwd(qk, proj_b, sinks, Hq, "swa_attn_fwd")
    token = fetch("ffn1", "forward", o_b)
    y1b = _mm_nn(o_b, W["swa_o"], F32, "swa_out_proj")
    z1b, x3, h2b = _ln_fwd(x2, y1b, g1b, row(P["ln_mix_g"][1]), row(P["ln_mix_b"][1]), sc2b, sh2b, token, "ln_mix1")
    W["up"][1], W["down"][1] = fetch("ffn1", "wait", h2b)
    u_b, uc_b, a_b = _ffn_up(h2b, W["up"][1], cw[1], cb[1], "ffn_up1")
    y2b = _mm_nn(a_b, W["down"][1], F32, "ffn_down1", tk=2816)
    z2b, dout, loss_row = _ln_fwd_loss(x3, y2b, g2b, row(P["ln_ffn_g"][1]), row(P["ln_ffn_b"][1]), target, "ln_ffn1_loss")

    def ffn_backward(dy, a, u, uc, h_in, l, tag):
        d_down = _mm_tn(a, dy[None], BF16, "ffn_dwdown" + tag, tm=1408, tn=1024)[0]
        token = on_grads("ffn_w_down" + tag, d_down)
        da = _mm_nt(dy[None], W["down"][l][None], BF16, "ffn_da" + tag)
        du, dcw, dcb = _ffn_bwd_elem(da, u, uc, cw[l], token, "ffn_bwd_elem" + tag)
        d_up = _mm_tn(h_in, du, BF16, "ffn_dwup" + tag)
        token = on_grads("ffn_w_up" + tag, d_up)
        dh = _mm_nt(du, W["up"][l], F32, "ffn_dh" + tag, tn=512, tk=2816)
        return dh, token, jnp.transpose(dcw, (1, 0, 2)).reshape(3, -1), dcb.reshape(-1)

    dz2b, dy2b, dg_f1, db_f1, dgate2b = _ln_bwd(dout, z2b, y2b, g2b, row(P["ln_ffn_g"][1]), "ln_ffn1_bwd")
    dh2b, token, dcw1, dcb1 = ffn_backward(dy2b, a_b, u_b, uc_b, h2b, 1, "1")
    dx3, dsc2b, dsh2b = _mod_bwd(dz2b, dh2b, x3, sc2b, token, "mod_ffn1_bwd")

    dz1b, dy1b, dg_m1, db_m1, dgate1b = _ln_bwd(dx3, z1b, y1b, g1b, row(P["ln_mix_g"][1]), "ln_mix1_bwd")
    token = on_grads("swa_w_o", _mm_tn(o_b, dy1b[None], BF16, "swa_dwo")[0])
    do_b = _mm_nt(dy1b[None], W["swa_o"][None], F32, "swa_do")
    dq_b, dk_b, dv_b, dsinks = _swa_attn_bwd(qk, proj_b, sinks, do_b, lse_b, Hq, token, "swa_attn_bwd")
    dproj_b = _rope([(dq_b, True, D), (dk_b, True, Wk), (dv_b, False, Wk)], tabs_b, BF16, "rope_bwd")
    token = on_grads("swa_w_in", _mm_tn(dproj_b, h1b[None], BF16, "swa_dwin", tm=1280, tn=1024)[0])
    dh1b = _mm_nt(dproj_b[None], W["swa_in"][None], F32, "swa_dh", tk=1280)
    dx2, dsc1b, dsh1b = _mod_bwd(dz1b, dh1b, x2, sc1b, token, "mod_mix1_bwd")

    dz2a, dy2a, dg_f0, db_f0, dgate2a = _ln_bwd(dx2, z2a, y2a, g2a, row(P["ln_ffn_g"][0]), "ln_ffn0_bwd")
    dh2a, token, dcw0, dcb0 = ffn_backward(dy2a, a_a, u_a, uc_a, h2a, 0, "0")
    dx1, dsc2a, dsh2a = _mod_bwd(dz2a, dh2a, x1, sc2a, token, "mod_ffn0_bwd")

    dz1a, dy1a, dg_m0, db_m0, dgate1a = _ln_bwd(dx1, z1a, y1a, g1a, row(P["ln_mix_g"][0]), "ln_mix0_bwd")
    token = on_grads("fox_w_o", _mm_tn(o_a, dy1a[None], BF16, "fox_dwo")[0])
    do_a = _mm_nt(dy1a[None], W["fox_o"][None], F32, "fox_do")
    dq_a, dk_a, dv_a, dcum_row, dcum_col = _fox_attn_bwd(proj_a, do_a, o_a, cum_row, lse_a, Hf, token, "fox_attn_bwd")
    dflT, dbf = _fox_prep_bwd(dcum_row.reshape(Hf, S), dcum_col.reshape(Hf, S), flT, bf_col, "fox_cumsum_bwd")
    n_pad = W["fox_in"].shape[1]
    dproj_a = jnp.concatenate([dq_a, dk_a.astype(BF16), dv_a.astype(BF16),
                               _pad_cols(dflT.T, n_pad - 3 * D).astype(BF16)], axis=1)
    dh1a = _mm_nt(dproj_a[None], W["fox_in"][None], F32, "fox_dh", tn=1024, tk=896)
    grad_x, dsc1a, dsh1a = _mod_bwd(dz1a, dh1a, x, sc1a, token, "mod_mix0_bwd")

    dmod = jnp.stack([jnp.concatenate([dsh1a, dsc1a, dgate1a, dsh2a, dsc2a, dgate2a], axis=1)[0],
                      jnp.concatenate([dsh1b, dsc1b, dgate1b, dsh2b, dsc2b, dgate2b], axis=1)[0]])
    small = dict(dmod=dmod, conv_b=jnp.stack([dcb0, dcb1]), conv_w=jnp.stack([dcw0, dcw1]),
                 ln_mix_g=jnp.concatenate([dg_m0, dg_m1]), ln_mix_b=jnp.concatenate([db_m0, db_m1]),
                 ln_ffn_g=jnp.concatenate([dg_f0, dg_f1]), ln_ffn_b=jnp.concatenate([db_f0, db_f1]),
                 fox_b_f=dbf.reshape(-1), swa_sinks=dsinks[0, :Hq], loss=loss_row[0, 0].reshape(1))
    exchanged = on_small(small)
    on_grads("fox_w_in", _mm_tn(dproj_a, h1a[None], BF16, "fox_dwin", tm=896, tn=1024)[0], exchanged)
    return grad_x


SMALL_ORDER = ("dmod", "conv_b", "conv_w", "ln_mix_g", "ln_mix_b", "ln_ffn_g", "ln_ffn_b", "fox_b_f", "swa_sinks", "loss")


def _pack_rows(arrays):
    chunks, spans, off = [], [], 0
    for a in arrays:
        flat = a.reshape(-1)
        n = -(-flat.shape[0] // LANE) * LANE
        chunks.append(jnp.pad(flat, (0, n - flat.shape[0])))
        spans.append((off, flat.shape[0], a.shape))
        off += n
    total = -(-off // (8 * LANE)) * (8 * LANE)
    chunks.append(jnp.zeros((total - off,), F32))
    return jnp.concatenate(chunks).reshape(-1, LANE), spans


def _unpack_rows(packed, spans):
    flat = packed.reshape(-1)
    return [flat[off:off + n].reshape(shape) for off, n, shape in spans]


def kernel(x, c, positions, fox_w_in, fox_b_f, fox_w_o, swa_w_in, swa_sinks, swa_w_o, ada_w, ada_b, ffn_w_up, ffn_conv_w, ffn_conv_b, ffn_w_down, ln_mix_g, ln_mix_b, ln_ffn_g, ln_ffn_b, loss_target, m_fox_w_in, m_fox_b_f, m_fox_w_o, m_swa_w_in, m_swa_sinks, m_swa_w_o, m_ada_w, m_ada_b, m_ffn_w_up, m_ffn_conv_w, m_ffn_conv_b, m_ffn_w_down, m_ln_mix_g, m_ln_mix_b, m_ln_ffn_g, m_ln_ffn_b, v_fox_w_in, v_fox_b_f, v_fox_w_o, v_swa_w_in, v_swa_sinks, v_swa_w_o, v_ada_w, v_ada_b, v_ffn_w_up, v_ffn_conv_w, v_ffn_conv_b, v_ffn_w_down, v_ln_mix_g, v_ln_mix_b, v_ln_ffn_g, v_ln_ffn_b):
    S, D = x.shape[1], x.shape[2]
    L = ada_w.shape[0]
    me = 4 * lax.axis_index("x") + 2 * lax.axis_index("y") + lax.axis_index("c")
    n_ada = ada_w.shape[2]
    cu = ffn_w_up.shape[2]
    F = 4 * cu
    n_in = fox_w_in.shape[2] * N_DEV
    n_in_pad = -(-n_in // LANE) * LANE

    gather_groups = dict(
        fox=([fox_w_in[0].astype(BF16), fox_w_o[0].astype(BF16)], ["major", "rows"]),
        ffn0=([ffn_w_up[0].astype(BF16), ffn_w_down[0].astype(BF16)], ["halves", "rows"]),
        swa=([swa_w_in[0].astype(BF16), swa_w_o[0].astype(BF16)], ["major", "rows"]),
        ffn1=([ffn_w_up[1].astype(BF16), ffn_w_down[1].astype(BF16)], ["halves", "rows"]))
    starts_after = dict(fox=["ffn0", "swa"], ffn0=["ffn1"])
    gathers = {}

    def start_group(group, after):
        shards, kinds = gather_groups[group]
        send, recv, local, thru, bufs, token = _gather_start(shards, kinds, after, "gather_start_" + group)
        gathers[group] = dict(send=send, recv=recv, local=local, shards=thru, bufs=bufs, kinds=kinds,
                              shapes=[a.shape for a in shards], token=token)
        return token

    fox_started = start_group("fox", None)

    c_all = _all_gather_small(c.reshape(-1, LANE), "gather_c", after=[fox_started]).reshape(N_DEV, D)
    b_cols = lax.dynamic_slice_in_dim(ada_b, me * n_ada, n_ada, axis=1).reshape(L, 1, n_ada)
    mod_blk = _ada_mod(jnp.pad(c_all, ((0, ADA_ROWS - N_DEV), (0, 0))), ada_w, b_cols, "ada_mod")[:, :N_DEV]
    mod_all = _all_gather_small(mod_blk.reshape(-1, LANE), "gather_mod").reshape(N_DEV, L, N_DEV, n_ada)
    mod_mine = lax.dynamic_index_in_dim(mod_all, me, axis=2, keepdims=False)
    mod_mine = jnp.transpose(mod_mine, (1, 0, 2)).reshape(L, N_DEV * n_ada)
    mods = [[mod_mine[l, k * D:(k + 1) * D].reshape(1, D) for k in range(6)] for l in range(L)]

    P = dict(fox_b_f=fox_b_f[0], swa_sinks=swa_sinks[0], conv_b=ffn_conv_b,
             ln_mix_g=ln_mix_g, ln_mix_b=ln_mix_b, ln_ffn_g=ln_ffn_g, ln_ffn_b=ln_ffn_b)
    cw_rows = _all_gather_small(_pack_rows([ffn_conv_w])[0], "gather_conv_w", after=[mod_all])
    n_cw = ffn_conv_w.size
    cw_dev = cw_rows.reshape(N_DEV, -1)[:, :n_cw].reshape(N_DEV, L, 3, cu)
    P["conv_w"] = jnp.transpose(cw_dev, (1, 2, 0, 3)).reshape(L, 3, N_DEV * cu)

    def natural(g, pad_to=None):
        slabs = [g[k] for k in range(N_DEV)]
        if pad_to is not None:
            slabs.append(jnp.zeros((D, pad_to - N_DEV * g.shape[2]), g.dtype))
        return jnp.concatenate(slabs, axis=1)

    def forward_stage(group, after):
        s = gathers[group]
        s["fsend"], s["frecv"], s["bufs"], token = _gather_forward(s["recv"], s["bufs"], s["kinds"], s["shapes"], after,
                                                                   "gather_forward_" + group)
        for nxt in starts_after.get(group, ()):
            token = start_group(nxt, token)
        return token

    def fetch(group, stage, after):
        if stage == "forward":
            return forward_stage(group, after)
        if group == "fox":
            after = forward_stage(group, cw_rows)
        s = gathers.pop(group)
        first, second = _gather_wait(s["send"], s["recv"], s["local"], s["fsend"], s["frecv"], s["shards"], s["bufs"],
                                     s["kinds"], after, "gather_wait_" + group)
        if group == "fox":
            return natural(first, n_in_pad), second
        if group == "swa":
            return natural(first), second
        return first, second

    out, pending = {}, {}

    def columns_major(g_t, n):
        return g_t[:n].reshape(N_DEV, n // N_DEV, D)

    def transposed(a):
        return jnp.transpose(a, (0, 2, 1))

    big = dict(
        ffn_w_down1=("ffn_w_down", "rows", 1, (ffn_w_down, m_ffn_w_down, v_ffn_w_down)),
        ffn_w_up1=("ffn_w_up", "halves", 1, (ffn_w_up, m_ffn_w_up, v_ffn_w_up)),
        swa_w_o=("swa_w_o", "rows", 0, (swa_w_o, m_swa_w_o, v_swa_w_o)),
        swa_w_in=("swa_w_in", "major", 0, tuple(transposed(a) for a in (swa_w_in, m_swa_w_in, v_swa_w_in))),
        ffn_w_down0=("ffn_w_down", "rows", 0, (ffn_w_down, m_ffn_w_down, v_ffn_w_down)),
        ffn_w_up0=("ffn_w_up", "halves", 0, (ffn_w_up, m_ffn_w_up, v_ffn_w_up)),
        fox_w_o=("fox_w_o", "rows", 0, (fox_w_o, m_fox_w_o, v_fox_w_o)),
        fox_w_in=("fox_w_in", "major", 0, (fox_w_in, m_fox_w_in, v_fox_w_in)))
    finish_at = dict(swa_w_o=["ffn_w_down1"], ffn_w_up0=["ffn_w_up1", "swa_w_o", "swa_w_in"], fox_w_o=["ffn_w_down0"],
                     fox_w_in=["ffn_w_up0"])
    tail = {}

    def finish(name, after):
        send, recv, thru, lands = pending.pop(name)
        param, kind, layer, wmv = big[name]
        if name == "fox_w_in":
            one = tail["last_start"][0, 0] + 1.0
            wmv = tuple(transposed(a * one) for a in wmv)
        landed, = _scatter_wait(send, recv, thru, lands, [kind], after, "scatter_wait_" + name)
        res = _adam_sum(landed, *wmv, layer, out.get(param), "adam_" + name, by_cols=kind == "major")
        out[param] = [transposed(r) for r in res] if kind == "major" else res

    def on_grads(name, g, after=None):
        kind = big[name][1]
        src = columns_major(g, n_in if name == "fox_w_in" else g.shape[0]) if kind == "major" else g
        send, recv, thru, lands, token = _scatter_start([src], [kind], after, "scatter_start_" + name)
        pending[name] = (send, recv, thru, lands)
        for done in finish_at.get(name, ()):
            finish(done, [token])
        tail["last_start"] = token
        return token

    def on_small(small):
        packed, tail["spans"] = _pack_rows([small[k] for k in SMALL_ORDER])
        send, recv, thru, lands, token = _scatter_start([packed], ["whole"], None, "small_grads_start")
        tail["small"] = (send, recv, thru, lands)
        return token

    grad_x = _local_step(x[0], loss_target[0], positions[0], mods, fetch, P, on_grads, on_small)

    spans = tail["spans"]
    done_first = [out[k][1] for k in ("ffn_w_up", "ffn_w_down", "swa_w_in", "swa_w_o")]
    gathered, = _scatter_wait(*tail["small"], ["whole"], done_first, "small_grads_wait")
    totals = dict(zip(SMALL_ORDER, _unpack_rows(_sum_slots(gathered, tail["last_start"], "sum_small_grads"), spans)))
    loss = totals["loss"].reshape(())
    n_mod = L * 6 * D
    dmod_all = gathered.reshape(N_DEV, -1)[:, :n_mod].reshape(N_DEV, L, 6 * D)
    dmod_cols = jnp.transpose(lax.dynamic_slice_in_dim(dmod_all, me * n_ada, n_ada, axis=2), (1, 0, 2))
    out["ada_w"] = _ada_bwd_adam(c_all.T, dmod_cols, ada_w, m_ada_w, v_ada_w, "adam_ada_w")

    g_small = dict(fox_b_f=totals["fox_b_f"].reshape(fox_b_f.shape), swa_sinks=totals["swa_sinks"].reshape(swa_sinks.shape),
                   ada_b=totals["dmod"].reshape(ada_b.shape), ffn_conv_b=totals["conv_b"].reshape(ffn_conv_b.shape),
                   ffn_conv_w=lax.dynamic_slice_in_dim(totals["conv_w"].reshape(L, 3, 2 * F), me * cu, cu, axis=2),
                   ln_mix_g=totals["ln_mix_g"], ln_mix_b=totals["ln_mix_b"],
                   ln_ffn_g=totals["ln_ffn_g"], ln_ffn_b=totals["ln_ffn_b"])
    small_names = ("fox_b_f", "swa_sinks", "ada_b", "ffn_conv_b", "ffn_conv_w", "ln_mix_g", "ln_mix_b", "ln_ffn_g", "ln_ffn_b")
    w_small = dict(fox_b_f=(fox_b_f, m_fox_b_f, v_fox_b_f), swa_sinks=(swa_sinks, m_swa_sinks, v_swa_sinks),
                   ada_b=(ada_b, m_ada_b, v_ada_b), ffn_conv_b=(ffn_conv_b, m_ffn_conv_b, v_ffn_conv_b),
                   ffn_conv_w=(ffn_conv_w, m_ffn_conv_w, v_ffn_conv_w),
                   ln_mix_g=(ln_mix_g, m_ln_mix_g, v_ln_mix_g), ln_mix_b=(ln_mix_b, m_ln_mix_b, v_ln_mix_b),
                   ln_ffn_g=(ln_ffn_g, m_ln_ffn_g, v_ln_ffn_g), ln_ffn_b=(ln_ffn_b, m_ln_ffn_b, v_ln_ffn_b))
    pk_g, sp = _pack_rows([g_small[k] for k in small_names])
    pk_w = _pack_rows([w_small[k][0] for k in small_names])[0]
    pk_m = _pack_rows([w_small[k][1] for k in small_names])[0]
    pk_v = _pack_rows([w_small[k][2] for k in small_names])[0]
    res = _adam(pk_g[None], pk_w[None], pk_m[None], pk_v[None], "adam_small")
    settled = [res[0], out["ada_w"][1]] + [out[k][1] for k in ("ffn_w_up", "ffn_w_down", "swa_w_in", "swa_w_o")]
    finish("fox_w_o", settled)
    finish("fox_w_in", settled)
    res = [dict(zip(small_names, _unpack_rows(r[0], sp))) for r in res]
    for k in small_names:
        out[k] = (g_small[k], res[0][k], res[1][k], res[2][k])

    order = ("fox_w_in", "fox_b_f", "fox_w_o", "swa_w_in", "swa_sinks", "swa_w_o", "ada_w", "ada_b", "ffn_w_up",
             "ffn_conv_w", "ffn_conv_b", "ffn_w_down", "ln_mix_g", "ln_mix_b", "ln_ffn_g", "ln_ffn_b")
    return (loss, grad_x[None], *[out[k][0] for k in order], *[out[k][1] for k in order],
            *[out[k][2] for k in order], *[out[k][3] for k in order])
```
